```python
import jax, jax.numpy as jnp
from jax import lax
import numpy as np

D_MODEL = 2048
BATCH = 8
SEQ = 8192
DEPTH = 2

N_A = DEPTH // 2
N_B = DEPTH - N_A
N_HEADS = 16
HEAD_DIM = D_MODEL // N_HEADS
D_FF = ((8 * D_MODEL // 3 + 255) // 256) * 256
CONV_WIDTH = 3
Q_BLOCK = 128
N_SUB = 3
N_MOD = 3
EPS = 1e-6

kernel_name = "yoco_shortconv_fox_macaron_adaln"


def rmsnorm(x, g):
    x32 = x.astype(jnp.float32)
    y = x32 * lax.rsqrt(jnp.mean(x32 * x32, axis=-1, keepdims=True) + EPS)
    return y.astype(x.dtype) * g


def modulate(h, shift, scale):
    return h * (1.0 + scale[:, None, :]) + shift[:, None, :]


def swiglu(h, w_in, w_out):
    a, b = jnp.split(h @ w_in, 2, axis=-1)
    return (jax.nn.silu(a) * b) @ w_out


def short_gated_conv(h, w_in, conv_w, conv_b, w_out):
    bg, cg, xv = jnp.split(h @ w_in, 3, axis=-1)
    u = cg * xv
    u = lax.conv_general_dilated(
        u, conv_w[:, None, :], window_strides=(1,),
        padding=[(CONV_WIDTH - 1, 0)],
        dimension_numbers=("NWC", "WIO", "NWC"),
        feature_group_count=D_MODEL) + conv_b
    return (bg * u) @ w_out


def forgetting_attention(q, k, v, fcum):
    b, s_len, h, dh = q.shape
    nb = s_len // Q_BLOCK
    scale = 1.0 / float(np.sqrt(dh))
    qb = q.reshape(b, nb, Q_BLOCK, h, dh).transpose(1, 0, 2, 3, 4)
    fb = fcum.reshape(b, h, nb, Q_BLOCK).transpose(2, 0, 1, 3)
    kpos = jnp.arange(s_len)

    def one_block(args):
        qi, fi, i = args
        logits = jnp.einsum("bqhd,bkhd->bhqk", qi, k,
                            preferred_element_type=jnp.float32) * scale
        logits = logits + fi[..., :, None] - fcum[:, :, None, :]
        qpos = i * Q_BLOCK + jnp.arange(Q_BLOCK)
        mask = qpos[:, None] >= kpos[None, :]
        logits = jnp.where(mask[None, None], logits, -jnp.inf)
        p = jax.nn.softmax(logits, axis=-1)
        return jnp.einsum("bhqk,bkhd->bqhd", p.astype(v.dtype), v)

    out = lax.map(one_block, (qb, fb, jnp.arange(nb)))
    return out.transpose(1, 0, 2, 3, 4).reshape(b, s_len, h * dh)


def _fwd_setup_inputs(seed: int = 0) -> dict:
    key = jax.random.key(seed)
    ks = jax.random.split(key, 24)
    D, F, H = D_MODEL, D_FF, N_HEADS
    nrm = lambda k, shape, fan_in: jax.random.normal(k, shape, jnp.float32) * (fan_in ** -0.5)
    return {
        "x": jax.random.normal(ks[0], (BATCH, SEQ, D), jnp.float32),
        "c": jax.random.normal(ks[1], (BATCH, D), jnp.float32),
        "norm_g": 1.0 + 0.05 * jax.random.normal(ks[2], (DEPTH, N_SUB, D), jnp.float32),
        "w_ada": nrm(ks[3], (DEPTH, D, N_SUB * N_MOD * D), D),
        "b_ada": 0.02 * jax.random.normal(ks[4], (DEPTH, N_SUB * N_MOD * D), jnp.float32),
        "w_ffn_in": nrm(ks[5], (DEPTH, 2, D, 2 * F), D),
        "w_ffn_out": nrm(ks[6], (DEPTH, 2, F, D), F),
        "w_conv_in": nrm(ks[7], (N_A, D, 3 * D), D),
        "conv_w": nrm(ks[8], (N_A, CONV_WIDTH, D), CONV_WIDTH),
        "conv_b": 0.02 * jax.random.normal(ks[9], (N_A, D), jnp.float32),
        "w_conv_out": nrm(ks[10], (N_A, D, D), D),
        "kv_norm_g": 1.0 + 0.05 * jax.random.normal(ks[11], (D,), jnp.float32),
        "w_ada_kv": nrm(ks[12], (D, 2 * D), D),
        "b_ada_kv": 0.02 * jax.random.normal(ks[13], (2 * D,), jnp.float32),
        "w_kvf": jnp.concatenate([nrm(ks[14], (D, 2 * D), D),
                                  0.1 * nrm(ks[15], (D, H), D)], axis=-1),
        "b_fgate": jax.random.uniform(ks[16], (H,), jnp.float32, 1.0, 6.0),
        "w_q": nrm(ks[17], (N_B, D, D), D),
        "w_o": nrm(ks[18], (N_B, D, D), D),
        "final_g": 1.0 + 0.05 * jax.random.normal(ks[19], (D,), jnp.float32),
    }


def _fwd_reference(x, c, norm_g, w_ada, b_ada, w_ffn_in, w_ffn_out, w_conv_in, conv_w, conv_b,
              w_conv_out, kv_norm_g, w_ada_kv, b_ada_kv, w_kvf, b_fgate, w_q, w_o, final_g):
    b, s_len, d = x.shape
    cond = jax.nn.silu(c)
    k = v = fcum = None
    for l in range(DEPTH):
        ada = (cond @ w_ada[l] + b_ada[l]).reshape(b, N_SUB, N_MOD, d)

        h = modulate(rmsnorm(x, norm_g[l, 0]), ada[:, 0, 0], ada[:, 0, 1])
        x = x + 0.5 * ada[:, 0, 2][:, None, :] * swiglu(h, w_ffn_in[l, 0], w_ffn_out[l, 0])

        h = modulate(rmsnorm(x, norm_g[l, 1]), ada[:, 1, 0], ada[:, 1, 1])
        if l < N_A:
            y = short_gated_conv(h, w_conv_in[l], conv_w[l], conv_b[l], w_conv_out[l])
        else:
            if l == N_A:
                ada_kv = (cond @ w_ada_kv + b_ada_kv).reshape(b, 2, d)
                hkv = modulate(rmsnorm(x, kv_norm_g), ada_kv[:, 0], ada_kv[:, 1])
                kvf = hkv @ w_kvf
                k = kvf[..., :d].reshape(b, s_len, N_HEADS, HEAD_DIM)
                v = kvf[..., d:2 * d].reshape(b, s_len, N_HEADS, HEAD_DIM)
                zf = (kvf[..., 2 * d:] + b_fgate).astype(jnp.float32)
                fcum = jnp.cumsum(jax.nn.log_sigmoid(zf), axis=1).transpose(0, 2, 1)
            j = l - N_A
            q = (h @ w_q[j]).reshape(b, s_len, N_HEADS, HEAD_DIM)
            y = forgetting_attention(q, k, v, fcum) @ w_o[j]
        x = x + ada[:, 1, 2][:, None, :] * y

        h = modulate(rmsnorm(x, norm_g[l, 2]), ada[:, 2, 0], ada[:, 2, 1])
        x = x + 0.5 * ada[:, 2, 2][:, None, :] * swiglu(h, w_ffn_in[l, 1], w_ffn_out[l, 1])
    return rmsnorm(x, final_g)


import jax as _jax
import jax.numpy as _jnp

TWIN_FORMAT = 'train_step'
FWD_PARAMS = ['x', 'c', 'norm_g', 'w_ada', 'b_ada', 'w_ffn_in', 'w_ffn_out', 'w_conv_in', 'conv_w', 'conv_b', 'w_conv_out', 'kv_norm_g', 'w_ada_kv', 'b_ada_kv', 'w_kvf', 'b_fgate', 'w_q', 'w_o', 'final_g']
TWIN_WEIGHTS = ['norm_g', 'w_ada', 'b_ada', 'w_ffn_in', 'w_ffn_out', 'w_conv_in', 'conv_w', 'conv_b', 'w_conv_out', 'kv_norm_g', 'w_ada_kv', 'b_ada_kv', 'w_kvf', 'b_fgate', 'w_q', 'w_o', 'final_g']
TWIN_DIFF_INPUT = 'x'
TWIN_INPUTS = ['x', 'c', 'norm_g', 'w_ada', 'b_ada', 'w_ffn_in', 'w_ffn_out', 'w_conv_in', 'conv_w', 'conv_b', 'w_conv_out', 'kv_norm_g', 'w_ada_kv', 'b_ada_kv', 'w_kvf', 'b_fgate', 'w_q', 'w_o', 'final_g', 'loss_target', 'm_norm_g', 'm_w_ada', 'm_b_ada', 'm_w_ffn_in', 'm_w_ffn_out', 'm_w_conv_in', 'm_conv_w', 'm_conv_b', 'm_w_conv_out', 'm_kv_norm_g', 'm_w_ada_kv', 'm_b_ada_kv', 'm_w_kvf', 'm_b_fgate', 'm_w_q', 'm_w_o', 'm_final_g', 'v_norm_g', 'v_w_ada', 'v_b_ada', 'v_w_ffn_in', 'v_w_ffn_out', 'v_w_conv_in', 'v_conv_w', 'v_conv_b', 'v_w_conv_out', 'v_kv_norm_g', 'v_w_ada_kv', 'v_b_ada_kv', 'v_w_kvf', 'v_b_fgate', 'v_w_q', 'v_w_o', 'v_final_g']
TWIN_OUTPUTS = ['loss', 'grad_x', 'grad_norm_g', 'grad_w_ada', 'grad_b_ada', 'grad_w_ffn_in', 'grad_w_ffn_out', 'grad_w_conv_in', 'grad_conv_w', 'grad_conv_b', 'grad_w_conv_out', 'grad_kv_norm_g', 'grad_w_ada_kv', 'grad_b_ada_kv', 'grad_w_kvf', 'grad_b_fgate', 'grad_w_q', 'grad_w_o', 'grad_final_g', 'delta_norm_g', 'delta_w_ada', 'delta_b_ada', 'delta_w_ffn_in', 'delta_w_ffn_out', 'delta_w_conv_in', 'delta_conv_w', 'delta_conv_b', 'delta_w_conv_out', 'delta_kv_norm_g', 'delta_w_ada_kv', 'delta_b_ada_kv', 'delta_w_kvf', 'delta_b_fgate', 'delta_w_q', 'delta_w_o', 'delta_final_g', 'new_m_norm_g', 'new_m_w_ada', 'new_m_b_ada', 'new_m_w_ffn_in', 'new_m_w_ffn_out', 'new_m_w_conv_in', 'new_m_conv_w', 'new_m_conv_b', 'new_m_w_conv_out', 'new_m_kv_norm_g', 'new_m_w_ada_kv', 'new_m_b_ada_kv', 'new_m_w_kvf', 'new_m_b_fgate', 'new_m_w_q', 'new_m_w_o', 'new_m_final_g', 'new_v_norm_g', 'new_v_w_ada', 'new_v_b_ada', 'new_v_w_ffn_in', 'new_v_w_ffn_out', 'new_v_w_conv_in', 'new_v_conv_w', 'new_v_conv_b', 'new_v_w_conv_out', 'new_v_kv_norm_g', 'new_v_w_ada_kv', 'new_v_b_ada_kv', 'new_v_w_kvf', 'new_v_b_fgate', 'new_v_w_q', 'new_v_w_o', 'new_v_final_g']
TWIN_LEAF_KINDS = {'loss': 'loss', 'grad_x': 'grad_x', 'grad_norm_g': 'grad_w', 'grad_w_ada': 'grad_w', 'grad_b_ada': 'grad_w', 'grad_w_ffn_in': 'grad_w', 'grad_w_ffn_out': 'grad_w', 'grad_w_conv_in': 'grad_w', 'grad_conv_w': 'grad_w', 'grad_conv_b': 'grad_w', 'grad_w_conv_out': 'grad_w', 'grad_kv_norm_g': 'grad_w', 'grad_w_ada_kv': 'grad_w', 'grad_b_ada_kv': 'grad_w', 'grad_w_kvf': 'grad_w', 'grad_b_fgate': 'grad_w', 'grad_w_q': 'grad_w', 'grad_w_o': 'grad_w', 'grad_final_g': 'grad_w', 'delta_norm_g': 'delta_w', 'delta_w_ada': 'delta_w', 'delta_b_ada': 'delta_w', 'delta_w_ffn_in': 'delta_w', 'delta_w_ffn_out': 'delta_w', 'delta_w_conv_in': 'delta_w', 'delta_conv_w': 'delta_w', 'delta_conv_b': 'delta_w', 'delta_w_conv_out': 'delta_w', 'delta_kv_norm_g': 'delta_w', 'delta_w_ada_kv': 'delta_w', 'delta_b_ada_kv': 'delta_w', 'delta_w_kvf': 'delta_w', 'delta_b_fgate': 'delta_w', 'delta_w_q': 'delta_w', 'delta_w_o': 'delta_w', 'delta_final_g': 'delta_w', 'new_m_norm_g': 'new_m', 'new_m_w_ada': 'new_m', 'new_m_b_ada': 'new_m', 'new_m_w_ffn_in': 'new_m', 'new_m_w_ffn_out': 'new_m', 'new_m_w_conv_in': 'new_m', 'new_m_conv_w': 'new_m', 'new_m_conv_b': 'new_m', 'new_m_w_conv_out': 'new_m', 'new_m_kv_norm_g': 'new_m', 'new_m_w_ada_kv': 'new_m', 'new_m_b_ada_kv': 'new_m', 'new_m_w_kvf': 'new_m', 'new_m_b_fgate': 'new_m', 'new_m_w_q': 'new_m', 'new_m_w_o': 'new_m', 'new_m_final_g': 'new_m', 'new_v_norm_g': 'new_v', 'new_v_w_ada': 'new_v', 'new_v_b_ada': 'new_v', 'new_v_w_ffn_in': 'new_v', 'new_v_w_ffn_out': 'new_v', 'new_v_w_conv_in': 'new_v', 'new_v_conv_w': 'new_v', 'new_v_conv_b': 'new_v', 'new_v_w_conv_out': 'new_v', 'new_v_kv_norm_g': 'new_v', 'new_v_w_ada_kv': 'new_v', 'new_v_b_ada_kv': 'new_v', 'new_v_w_kvf': 'new_v', 'new_v_b_fgate': 'new_v', 'new_v_w_q': 'new_v', 'new_v_w_o': 'new_v', 'new_v_final_g': 'new_v'}


def _forward(args):
    return _fwd_reference(*[args[k] for k in FWD_PARAMS])


def _output_shape():
    def fwd():
        inp = _fwd_setup_inputs(0)
        return _fwd_reference(*[inp[k] for k in FWD_PARAMS])
    out = _jax.eval_shape(fwd)
    return out.shape, out.dtype

N_MICROBATCH = 1
ADAM_LR = 0.001
ADAM_B1 = 0.9
ADAM_B2 = 0.999
ADAM_EPS = 1e-08
ADAM_WD = 0.01
ADAM_STEP = 10
PER_EXAMPLE_BATCH_AXIS = {'x': 0, 'c': 0, 'loss_target': 0}
SHARED_INPUTS = []
_WEIGHT_DTYPES = {'norm_g': _jnp.float32, 'w_ada': _jnp.float32, 'b_ada': _jnp.float32, 'w_ffn_in': _jnp.float32, 'w_ffn_out': _jnp.float32, 'w_conv_in': _jnp.float32, 'conv_w': _jnp.float32, 'conv_b': _jnp.float32, 'w_conv_out': _jnp.float32, 'kv_norm_g': _jnp.float32, 'w_ada_kv': _jnp.float32, 'b_ada_kv': _jnp.float32, 'w_kvf': _jnp.float32, 'b_fgate': _jnp.float32, 'w_q': _jnp.float32, 'w_o': _jnp.float32, 'final_g': _jnp.float32}
MOMENT_SCALE = {'norm_g': 7.474380e-02, 'w_ada': 1.027656e-01, 'b_ada': 2.350299e-01, 'w_ffn_in': 2.103441e-02, 'w_ffn_out': 3.453891e-02, 'w_conv_in': 1.025229e-01, 'conv_w': 1.054004e-01, 'conv_b': 6.954987e-02, 'w_conv_out': 1.059671e-01, 'kv_norm_g': 2.875678e-02, 'w_ada_kv': 4.030978e-02, 'b_ada_kv': 7.860439e-02, 'w_kvf': 4.093549e-02, 'b_fgate': 7.501187e-02, 'w_q': 2.010675e-02, 'w_o': 5.775600e-02, 'final_g': 3.327260e+01}


def _to_microbatches(a, axis):
    t = _jnp.moveaxis(a, axis, 0)
    t = t.reshape((N_MICROBATCH, t.shape[0] // N_MICROBATCH) + t.shape[1:])
    return _jnp.moveaxis(t, 1, axis + 1)


def setup_inputs(seed: int = 0) -> dict:
    inp = _fwd_setup_inputs(seed)
    key = _jax.random.fold_in(_jax.random.key(seed), 7919)
    shape, _ = _output_shape()
    out = dict(inp)
    out["loss_target"] = _jax.random.normal(_jax.random.fold_in(key, 0), shape, _jnp.float32)
    for i, name in enumerate(TWIN_WEIGHTS):
        w = inp[name].astype(_jnp.float32)
        if MOMENT_SCALE is None:
            s = _jnp.sqrt(_jnp.mean(_jnp.square(w)) + 1e-30)
        else:
            s = MOMENT_SCALE[name]
        km, kv = _jax.random.split(_jax.random.fold_in(key, i + 1))
        out[name] = w
        out["m_" + name] = s * _jax.random.normal(km, w.shape, _jnp.float32)
        out["v_" + name] = (s * s) * _jax.random.uniform(kv, w.shape, _jnp.float32, 0.5, 1.5)
    if N_MICROBATCH > 1:
        for name, axis in PER_EXAMPLE_BATCH_AXIS.items():
            out[name] = _to_microbatches(out[name], axis)
    return {'x': out['x'], 'c': out['c'], 'norm_g': out['norm_g'], 'w_ada': out['w_ada'], 'b_ada': out['b_ada'], 'w_ffn_in': out['w_ffn_in'], 'w_ffn_out': out['w_ffn_out'], 'w_conv_in': out['w_conv_in'], 'conv_w': out['conv_w'], 'conv_b': out['conv_b'], 'w_conv_out': out['w_conv_out'], 'kv_norm_g': out['kv_norm_g'], 'w_ada_kv': out['w_ada_kv'], 'b_ada_kv': out['b_ada_kv'], 'w_kvf': out['w_kvf'], 'b_fgate': out['b_fgate'], 'w_q': out['w_q'], 'w_o': out['w_o'], 'final_g': out['final_g'], 'loss_target': out['loss_target'], 'm_norm_g': out['m_norm_g'], 'm_w_ada': out['m_w_ada'], 'm_b_ada': out['m_b_ada'], 'm_w_ffn_in': out['m_w_ffn_in'], 'm_w_ffn_out': out['m_w_ffn_out'], 'm_w_conv_in': out['m_w_conv_in'], 'm_conv_w': out['m_conv_w'], 'm_conv_b': out['m_conv_b'], 'm_w_conv_out': out['m_w_conv_out'], 'm_kv_norm_g': out['m_kv_norm_g'], 'm_w_ada_kv': out['m_w_ada_kv'], 'm_b_ada_kv': out['m_b_ada_kv'], 'm_w_kvf': out['m_w_kvf'], 'm_b_fgate': out['m_b_fgate'], 'm_w_q': out['m_w_q'], 'm_w_o': out['m_w_o'], 'm_final_g': out['m_final_g'], 'v_norm_g': out['v_norm_g'], 'v_w_ada': out['v_w_ada'], 'v_b_ada': out['v_b_ada'], 'v_w_ffn_in': out['v_w_ffn_in'], 'v_w_ffn_out': out['v_w_ffn_out'], 'v_w_conv_in': out['v_w_conv_in'], 'v_conv_w': out['v_conv_w'], 'v_conv_b': out['v_conv_b'], 'v_w_conv_out': out['v_w_conv_out'], 'v_kv_norm_g': out['v_kv_norm_g'], 'v_w_ada_kv': out['v_w_ada_kv'], 'v_b_ada_kv': out['v_b_ada_kv'], 'v_w_kvf': out['v_w_kvf'], 'v_b_fgate': out['v_b_fgate'], 'v_w_q': out['v_w_q'], 'v_w_o': out['v_w_o'], 'v_final_g': out['v_final_g']}


def _loss(weights, diff, rest, loss_target):
    with _jax.named_scope("forward"):
        args = {**rest, TWIN_DIFF_INPUT: diff, **{k: w.astype(_WEIGHT_DTYPES[k]) for k, w in weights.items()}}
        y = _forward(args)
    with _jax.named_scope("loss_head"):
        err = _jnp.square(y.astype(_jnp.float32) - loss_target)
        return 0.5 * _jnp.sum(_jnp.mean(err, axis=-1)) if err.ndim else 0.5 * err


def _adamw(w, g, m, v):
    m = ADAM_B1 * m + (1.0 - ADAM_B1) * g
    v = ADAM_B2 * v + (1.0 - ADAM_B2) * _jnp.square(g)
    m_hat = m / (1.0 - ADAM_B1 ** ADAM_STEP)
    v_hat = v / (1.0 - ADAM_B2 ** ADAM_STEP)
    delta = -ADAM_LR * (m_hat / (_jnp.sqrt(v_hat) + ADAM_EPS) + ADAM_WD * w)
    return delta, m, v


def reference(x, c, norm_g, w_ada, b_ada, w_ffn_in, w_ffn_out, w_conv_in, conv_w, conv_b, w_conv_out, kv_norm_g, w_ada_kv, b_ada_kv, w_kvf, b_fgate, w_q, w_o, final_g, loss_target, m_norm_g, m_w_ada, m_b_ada, m_w_ffn_in, m_w_ffn_out, m_w_conv_in, m_conv_w, m_conv_b, m_w_conv_out, m_kv_norm_g, m_w_ada_kv, m_b_ada_kv, m_w_kvf, m_b_fgate, m_w_q, m_w_o, m_final_g, v_norm_g, v_w_ada, v_b_ada, v_w_ffn_in, v_w_ffn_out, v_w_conv_in, v_conv_w, v_conv_b, v_w_conv_out, v_kv_norm_g, v_w_ada_kv, v_b_ada_kv, v_w_kvf, v_b_fgate, v_w_q, v_w_o, v_final_g):
    given = dict(x=x, c=c, norm_g=norm_g, w_ada=w_ada, b_ada=b_ada, w_ffn_in=w_ffn_in, w_ffn_out=w_ffn_out, w_conv_in=w_conv_in, conv_w=conv_w, conv_b=conv_b, w_conv_out=w_conv_out, kv_norm_g=kv_norm_g, w_ada_kv=w_ada_kv, b_ada_kv=b_ada_kv, w_kvf=w_kvf, b_fgate=b_fgate, w_q=w_q, w_o=w_o, final_g=final_g, loss_target=loss_target, m_norm_g=m_norm_g, m_w_ada=m_w_ada, m_b_ada=m_b_ada, m_w_ffn_in=m_w_ffn_in, m_w_ffn_out=m_w_ffn_out, m_w_conv_in=m_w_conv_in, m_conv_w=m_conv_w, m_conv_b=m_conv_b, m_w_conv_out=m_w_conv_out, m_kv_norm_g=m_kv_norm_g, m_w_ada_kv=m_w_ada_kv, m_b_ada_kv=m_b_ada_kv, m_w_kvf=m_w_kvf, m_b_fgate=m_b_fgate, m_w_q=m_w_q, m_w_o=m_w_o, m_final_g=m_final_g, v_norm_g=v_norm_g, v_w_ada=v_w_ada, v_b_ada=v_b_ada, v_w_ffn_in=v_w_ffn_in, v_w_ffn_out=v_w_ffn_out, v_w_conv_in=v_w_conv_in, v_conv_w=v_conv_w, v_conv_b=v_conv_b, v_w_conv_out=v_w_conv_out, v_kv_norm_g=v_kv_norm_g, v_w_ada_kv=v_w_ada_kv, v_b_ada_kv=v_b_ada_kv, v_w_kvf=v_w_kvf, v_b_fgate=v_b_fgate, v_w_q=v_w_q, v_w_o=v_w_o, v_final_g=v_final_g)
    weights = {n: given[n] for n in TWIN_WEIGHTS}
    shared = {n: given[n] for n in SHARED_INPUTS}
    per_example = {n: given[n] for n in ['x', 'c']}
    grad_fn = _jax.value_and_grad(_loss, argnums=(0, 1))

    def one_microbatch(ex, loss_target):
        ex = dict(ex)
        diff = ex.pop(TWIN_DIFF_INPUT)
        return grad_fn(weights, diff, {**shared, **ex}, loss_target)

    if N_MICROBATCH == 1:
        loss, (grad_w, grad_x) = one_microbatch(per_example, given["loss_target"])
    else:
        def body(carry, xs):
            loss_sum, grad_sum = carry
            l_k, (gw_k, gx_k) = one_microbatch(xs[0], xs[1])
            with _jax.named_scope("update"):
                return (loss_sum + l_k, _jax.tree.map(_jnp.add, grad_sum, gw_k)), gx_k

        init = (_jnp.zeros((), _jnp.float32), _jax.tree.map(_jnp.zeros_like, weights))
        (loss, grad_w), grad_x = _jax.lax.scan(body, init, (per_example, given["loss_target"]))
    with _jax.named_scope("update"):
        delta_w, new_m, new_v = {}, {}, {}
        for n in TWIN_WEIGHTS:
            delta_w[n], new_m[n], new_v[n] = _adamw(weights[n], grad_w[n], given["m_" + n], given["v_" + n])
    return (loss, grad_x, *[grad_w[n] for n in TWIN_WEIGHTS], *[delta_w[n] for n in TWIN_WEIGHTS],
            *[new_m[n] for n in TWIN_WEIGHTS], *[new_v[n] for n in TWIN_WEIGHTS])
```

```python
import functools
import math

import jax
import jax.numpy as jnp
from jax import lax
from jax.experimental import pallas as pl
from jax.experimental.pallas import tpu as pltpu

F32 = jnp.float32
BF16 = jnp.bfloat16
MESH = pl.DeviceIdType.MESH

N_DEV = 8
LANES = 128
SUBLANES = 8
VMEM_LIMIT = 56 * 2**20
EPS = 1e-6
ADAM_LR, ADAM_B1, ADAM_B2, ADAM_EPS, ADAM_WD, ADAM_STEP = 0.001, 0.9, 0.999, 1e-08, 0.01, 10

_DN = {"nn": (((1,), (0,)), ((), ())), "nt": (((1,), (1,)), ((), ())), "tn": (((0,), (0,)), ((), ()))}


def _tile(n, pref, align):
    t = min(pref, n)
    t -= t % align
    while t >= align:
        if n % t == 0:
            return t
        t -= align
    return n


def _params(n_grid):
    return pltpu.CompilerParams(vmem_limit_bytes=VMEM_LIMIT, dimension_semantics=("arbitrary",) * n_grid)


def _silu(v):
    return v * jax.nn.sigmoid(v)


def _position():
    x, y, c = lax.axis_index("x"), lax.axis_index("y"), lax.axis_index("c")
    return x, y, c


def _allgather_small(name, v):
    def body(v_ref, out_ref, send_sems, recv_sems, local_sem):
        x, y, c = _position()
        me, sibling = (x, y, c), (x, y, 1 - c)
        chips = [(1 - x, y), (x, 1 - y), (1 - x, 1 - y)]

        def slot(px, py, pc):
            return out_ref.at[4 * px + 2 * py + pc]

        def copy(k, block, to, src=None):
            return pltpu.make_async_remote_copy(
                src_ref=slot(*block) if src is None else src, dst_ref=slot(*block),
                send_sem=send_sems.at[k], recv_sem=recv_sems.at[k], device_id=to, device_id_type=MESH)

        mine = pltpu.make_async_copy(v_ref, slot(*me), local_sem)
        mine.start()
        first = [copy(0, me, sibling, src=v_ref)]
        first += [copy(1 + j, me, (*chip, c), src=v_ref) for j, chip in enumerate(chips)]
        for cp in first:
            cp.start()
        passed = [copy(4 + j, (*chip, c), sibling) for j, chip in enumerate(chips)]
        for j, chip in enumerate(chips):
            copy(1 + j, (*chip, c), me).wait_recv()
            passed[j].start()
        copy(0, sibling, me).wait_recv()
        for j, chip in enumerate(chips):
            copy(4 + j, (*chip, 1 - c), me).wait_recv()
        for cp in first + passed:
            cp.wait_send()
        mine.wait()

    return pl.pallas_call(
        body, name=name,
        out_shape=jax.ShapeDtypeStruct((N_DEV,) + v.shape, v.dtype),
        in_specs=[pl.BlockSpec(memory_space=pltpu.VMEM)],
        out_specs=pl.BlockSpec(memory_space=pltpu.VMEM),
        scratch_shapes=[pltpu.SemaphoreType.DMA((7,)), pltpu.SemaphoreType.DMA((7,)), pltpu.SemaphoreType.DMA],
        compiler_params=pltpu.CompilerParams(vmem_limit_bytes=VMEM_LIMIT),
    )(v)


def _allgather_hbm(name, shards):
    n = len(shards)

    def body(*refs):
        ins, outs = refs[:n], refs[n:2 * n]
        send_sems, recv_sems, local_sems = refs[2 * n:]
        x, y, c = _position()
        me, sibling = (x, y, c), (x, y, 1 - c)
        chips = [(1 - x, y), (x, 1 - y), (1 - x, 1 - y)]

        def copy(a, k, block, to, src=None):
            px, py, pc = block
            dst = outs[a].at[4 * px + 2 * py + pc]
            return pltpu.make_async_remote_copy(
                src_ref=dst if src is None else src, dst_ref=dst,
                send_sem=send_sems.at[7 * a + k], recv_sem=recv_sems.at[7 * a + k], device_id=to, device_id_type=MESH)

        local, first, passed = [], [], []
        for a in range(n):
            cp = pltpu.make_async_copy(ins[a], outs[a].at[4 * x + 2 * y + c], local_sems.at[a])
            cp.start()
            local.append(cp)
            mine = [copy(a, 1 + j, me, (*chip, c), src=ins[a]) for j, chip in enumerate(chips)]
            mine.append(copy(a, 0, me, sibling, src=ins[a]))
            for cp in mine:
                cp.start()
            first += mine
        for a in range(n):
            for j, chip in enumerate(chips):
                copy(a, 1 + j, (*chip, c), me).wait_recv()
                cp = copy(a, 4 + j, (*chip, c), sibling)
                cp.start()
                passed.append(cp)
        for a in range(n):
            copy(a, 0, sibling, me).wait_recv()
            for j, chip in enumerate(chips):
                copy(a, 4 + j, (*chip, 1 - c), me).wait_recv()
        for cp in first + passed:
            cp.wait_send()
        for cp in local:
            cp.wait()

    any_spec = pl.BlockSpec(memory_space=pl.ANY)
    return pl.pallas_call(
        body, name=name,
        out_shape=[jax.ShapeDtypeStruct((N_DEV,) + s.shape, s.dtype) for s in shards],
        in_specs=[any_spec] * n, out_specs=[any_spec] * n,
        scratch_shapes=[pltpu.SemaphoreType.DMA((7 * n,)), pltpu.SemaphoreType.DMA((7 * n,)),
                        pltpu.SemaphoreType.DMA((n,))],
    )(*shards)


def _alltoall_hbm(name, parts):
    n = len(parts)

    def body(*refs):
        ins, outs = refs[:n], refs[n:2 * n]
        send_sems, recv_sems, local_sems = refs[2 * n:]
        x, y, c = _position()
        me = 4 * x + 2 * y + c

        def peer(k):
            px = 1 - x if k & 4 else x
            py = 1 - y if k & 2 else y
            pc = 1 - c if k & 1 else c
            return (px, py, pc), 4 * px + 2 * py + pc

        def copy(a, k):
            to, to_id = peer(k)
            return pltpu.make_async_remote_copy(
                src_ref=ins[a].at[to_id], dst_ref=outs[a].at[me],
                send_sem=send_sems.at[7 * a + k - 1], recv_sem=recv_sems.at[7 * a + k - 1],
                device_id=to, device_id_type=MESH)

        def landed(a, k):
            to, to_id = peer(k)
            return pltpu.make_async_remote_copy(
                src_ref=ins[a].at[to_id], dst_ref=outs[a].at[to_id],
                send_sem=send_sems.at[7 * a + k - 1], recv_sem=recv_sems.at[7 * a + k - 1],
                device_id=to, device_id_type=MESH)

        local, sent = [], []
        for a in range(n):
            cp = pltpu.make_async_copy(ins[a].at[me], outs[a].at[me], local_sems.at[a])
            cp.start()
            local.append(cp)
            for k in range(1, N_DEV):
                cp = copy(a, k)
                cp.start()
                sent.append(cp)
        for a in range(n):
            for k in range(1, N_DEV):
                landed(a, k).wait_recv()
        for cp in sent:
            cp.wait_send()
        for cp in local:
            cp.wait()

    any_spec = pl.BlockSpec(memory_space=pl.ANY)
    return pl.pallas_call(
        body, name=name,
        out_shape=[jax.ShapeDtypeStruct(p.shape, p.dtype) for p in parts],
        in_specs=[any_spec] * n, out_specs=[any_spec] * n,
        scratch_shapes=[pltpu.SemaphoreType.DMA((7 * n,)), pltpu.SemaphoreType.DMA((7 * n,)),
                        pltpu.SemaphoreType.DMA((n,))],
    )(*parts)


def _matmul(name, mode, grid, a, a_spec, bs, b_specs, extras, extra_specs, out_shapes, out_specs, acc_shape,
            epilogue, a_fn=None):
    nk, nb, ne, no = grid[2], len(bs), len(extras), len(out_shapes)

    def body(*refs):
        a_ref, b_refs, e_refs = refs[0], refs[1:1 + nb], refs[1 + nb:1 + nb + ne]
        o_refs, acc_refs = refs[1 + nb + ne:1 + nb + ne + no], refs[1 + nb + ne + no:]
        a_val = a_ref[...]
        if a_fn is not None:
            a_val = a_fn(a_val)
        a_val = a_val.astype(BF16)
        prods = [lax.dot_general(a_val, b_ref[...].astype(BF16), _DN[mode], preferred_element_type=F32)
                 for b_ref in b_refs]

        def finish(accs):
            outs = epilogue(accs, [e[...] for e in e_refs])
            for o_ref, o in zip(o_refs, outs):
                o_ref[...] = o.astype(o_ref.dtype)

        if nk == 1:
            finish(prods)
        else:
            kk = pl.program_id(2)

            @pl.when(kk == 0)
            def _():
                for acc, p in zip(acc_refs, prods):
                    acc[...] = p

            @pl.when(kk > 0)
            def _():
                for acc, p in zip(acc_refs, prods):
                    acc[...] += p

            @pl.when(kk == nk - 1)
            def _():
                finish([acc[...] for acc in acc_refs])

    scratch = [] if nk == 1 else [pltpu.VMEM(acc_shape, F32) for _ in range(nb)]
    return pl.pallas_call(
        body, name=name, grid=grid, in_specs=[a_spec] + list(b_specs) + list(extra_specs), out_specs=list(out_specs),
        out_shape=list(out_shapes), scratch_shapes=scratch, compiler_params=_params(3),
    )(a, *bs, *extras)


def _mm_tiles(m, n, k):
    return _tile(m, 512, 16), _tile(n, 512, LANES), _tile(k, 1024, LANES)


def _mm_nn(name, a, b, out_dtype, b_col0=0, n=None):
    m, k = a.shape
    n = b.shape[1] if n is None else n
    tm, tn, tk = _mm_tiles(m, n, k)
    off = b_col0 // tn
    assert b_col0 % tn == 0
    return _matmul(
        name, "nn", (m // tm, n // tn, k // tk), a, pl.BlockSpec((tm, tk), lambda i, j, kk: (i, kk)),
        [b], [pl.BlockSpec((tk, tn), lambda i, j, kk: (kk, j + off))], [], [],
        [jax.ShapeDtypeStruct((m, n), out_dtype)], [pl.BlockSpec((tm, tn), lambda i, j, kk: (i, j))], (tm, tn),
        lambda accs, ex: accs)[0]


def _mm_bias_rows(name, a, b, bias, a_fn):
    m, k = a.shape
    n = b.shape[1]
    tn = _tile(n, 768, LANES)
    return _matmul(
        name, "nn", (1, n // tn, 1), a, pl.BlockSpec((m, k), lambda i, j, kk: (0, 0)),
        [b], [pl.BlockSpec((k, tn), lambda i, j, kk: (0, j))],
        [bias], [pl.BlockSpec((1, tn), lambda i, j, kk: (0, j))],
        [jax.ShapeDtypeStruct((m, n), F32)], [pl.BlockSpec((m, tn), lambda i, j, kk: (0, j))], (m, tn),
        lambda accs, ex: [accs[0] + ex[0]], a_fn=a_fn)[0]


def _mm_swiglu(name, h, w_in, f):
    s, d = h.shape
    tm, tn, tk = _mm_tiles(s, f, d)
    nf = f // tn

    def epilogue(accs, ex):
        a_pre, b_pre = accs
        return [jnp.stack([a_pre, b_pre]), _silu(a_pre) * b_pre]

    return _matmul(
        name, "nn", (s // tm, nf, d // tk), h, pl.BlockSpec((tm, tk), lambda i, j, kk: (i, kk)),
        [w_in, w_in], [pl.BlockSpec((tk, tn), lambda i, j, kk: (kk, j)),
                       pl.BlockSpec((tk, tn), lambda i, j, kk: (kk, j + nf))], [], [],
        [jax.ShapeDtypeStruct((2, s, f), F32), jax.ShapeDtypeStruct((s, f), BF16)],
        [pl.BlockSpec((2, tm, tn), lambda i, j, kk: (0, i, j)), pl.BlockSpec((tm, tn), lambda i, j, kk: (i, j))],
        (tm, tn), epilogue)


def _mm_residual(name, a, w, x_in, gvec):
    s, k = a.shape
    d = w.shape[1]
    tm, tn, tk = _mm_tiles(s, d, k)

    def epilogue(accs, ex):
        x_blk, g_row = ex
        return [accs[0], x_blk + g_row * accs[0]]

    tile = pl.BlockSpec((tm, tn), lambda i, j, kk: (i, j))
    return _matmul(
        name, "nn", (s // tm, d // tn, k // tk), a, pl.BlockSpec((tm, tk), lambda i, j, kk: (i, kk)),
        [w], [pl.BlockSpec((tk, tn), lambda i, j, kk: (kk, j))],
        [x_in, gvec], [tile, pl.BlockSpec((1, tn), lambda i, j, kk: (0, j))],
        [jax.ShapeDtypeStruct((s, d), F32), jax.ShapeDtypeStruct((s, d), F32)], [tile, tile], (tm, tn), epilogue)


def _mm_nt(name, a, w, out_dtype, w_col0=0, extra_add=None):
    s, k = a.shape
    n = w.shape[0]
    tm, tn, tk = _mm_tiles(s, n, k)
    off = w_col0 // tk
    assert w_col0 % tk == 0
    tile = pl.BlockSpec((tm, tn), lambda i, j, kk: (i, j))
    extras, especs = ([], []) if extra_add is None else ([extra_add], [tile])
    return _matmul(
        name, "nt", (s // tm, n // tn, k // tk), a, pl.BlockSpec((tm, tk), lambda i, j, kk: (i, kk)),
        [w], [pl.BlockSpec((tn, tk), lambda i, j, kk: (j, kk + off))], extras, especs,
        [jax.ShapeDtypeStruct((s, n), out_dtype)], [tile], (tm, tn),
        lambda accs, ex: [accs[0] + ex[0]] if ex else accs)[0]


def _mm_nt_parts(name, a3, w, out_dtype):
    p, s, k = a3.shape
    n = w.shape[0]
    tm, tn, tk = _mm_tiles(s, n, k)
    nkp = k // tk
    return _matmul(
        name, "nt", (s // tm, n // tn, p * nkp), a3,
        pl.BlockSpec((None, tm, tk), lambda i, j, kk: (kk // nkp, i, kk % nkp)),
        [w], [pl.BlockSpec((tn, tk), lambda i, j, kk: (j, kk))], [], [],
        [jax.ShapeDtypeStruct((s, n), out_dtype)], [pl.BlockSpec((tm, tn), lambda i, j, kk: (i, j))], (tm, tn),
        lambda accs, ex: accs)[0]


def _mm_nt_swiglu_bwd(name, dy, w_out, ab):
    s, d = dy.shape
    f = w_out.shape[0]
    tm, tn, tk = _mm_tiles(s, f, d)

    def epilogue(accs, ex):
        du, a_pre, b_pre = accs[0], ex[0][0], ex[0][1]
        sig = jax.nn.sigmoid(a_pre)
        da = du * b_pre * (sig * (1.0 + a_pre * (1.0 - sig)))
        db = du * (a_pre * sig)
        return [jnp.stack([da, db])]

    pair = pl.BlockSpec((2, tm, tn), lambda i, j, kk: (0, i, j))
    return _matmul(
        name, "nt", (s // tm, f // tn, d // tk), dy, pl.BlockSpec((tm, tk), lambda i, j, kk: (i, kk)),
        [w_out], [pl.BlockSpec((tn, tk), lambda i, j, kk: (j, kk))], [ab], [pair],
        [jax.ShapeDtypeStruct((2, s, f), BF16)], [pair], (tm, tn), epilogue)[0]


def _mm_tn(name, a, b, out_dtype, a_fn=None):
    s, m = a.shape
    tm, tk = _tile(m, 512, LANES), _tile(s, 1024, 16)
    if b.ndim == 3:
        npart = b.shape[2]
        n = b.shape[0] * npart
        tn = _tile(npart, 512, LANES)
        nj = npart // tn
        b_spec = pl.BlockSpec((None, tk, tn), lambda i, j, kk: (j // nj, kk, j % nj))
    else:
        n = b.shape[1]
        tn = _tile(n, 512, LANES)
        b_spec = pl.BlockSpec((tk, tn), lambda i, j, kk: (kk, j))
    return _matmul(
        name, "tn", (m // tm, n // tn, s // tk), a, pl.BlockSpec((tk, tm), lambda i, j, kk: (kk, i)),
        [b], [b_spec], [], [],
        [jax.ShapeDtypeStruct((m, n), out_dtype)], [pl.BlockSpec((tm, tn), lambda i, j, kk: (i, j))], (tm, tn),
        lambda accs, ex: accs, a_fn=a_fn)[0]


def _row_tile(s):
    return _tile(s, 256, 16)


def _fold_rows(v):
    return jnp.sum(v.reshape(v.shape[0] // SUBLANES, SUBLANES, v.shape[1]), axis=0)


def _accumulate(i, acc_ref, val):
    @pl.when(i == 0)
    def _():
        acc_ref[...] = val

    @pl.when(i > 0)
    def _():
        acc_ref[...] += val


def _norm_mod(name, x, g, shift, scale):
    s, d = x.shape
    tb = _row_tile(s)

    def body(x_ref, g_ref, sh_ref, sc_ref, h_ref):
        xv = x_ref[...]
        rstd = lax.rsqrt(jnp.mean(xv * xv, axis=-1, keepdims=True) + EPS)
        h_ref[...] = ((xv * rstd) * g_ref[...] * (1.0 + sc_ref[...]) + sh_ref[...]).astype(BF16)

    row = pl.BlockSpec((tb, d), lambda i: (i, 0))
    vec = pl.BlockSpec((1, d), lambda i: (0, 0))
    return pl.pallas_call(body, name=name, grid=(s // tb,), in_specs=[row, vec, vec, vec], out_specs=row,
                          out_shape=jax.ShapeDtypeStruct((s, d), BF16), compiler_params=_params(1))(x, g, shift, scale)


def _rms_bwd(xv, dn_g, d):
    rstd = lax.rsqrt(jnp.mean(xv * xv, axis=-1, keepdims=True) + EPS)
    xhat = xv * rstd
    dx = rstd * (dn_g - xhat * (jnp.sum(dn_g * xhat, axis=-1, keepdims=True) * (1.0 / d)))
    return xhat, dx


def _norm_mod_bwd(name, dh, x, g, scale, dx_in):
    s, d = x.shape
    tb = _row_tile(s)
    nsteps = s // tb

    def body(dh_ref, x_ref, g_ref, sc_ref, dxin_ref, dx_ref, dsh_ref, dsc_ref, dg_ref, a_sh, a_sc, a_g):
        i = pl.program_id(0)
        dhv, gv = dh_ref[...], g_ref[...]
        dn = dhv * (1.0 + sc_ref[...])
        xhat, dx = _rms_bwd(x_ref[...], dn * gv, d)
        dx_ref[...] = dxin_ref[...] + dx
        _accumulate(i, a_sh, _fold_rows(dhv))
        _accumulate(i, a_sc, _fold_rows(dhv * (xhat * gv)))
        _accumulate(i, a_g, _fold_rows(dn * xhat))

        @pl.when(i == nsteps - 1)
        def _():
            dsh_ref[...] = jnp.sum(a_sh[...], axis=0, keepdims=True)
            dsc_ref[...] = jnp.sum(a_sc[...], axis=0, keepdims=True)
            dg_ref[...] = jnp.sum(a_g[...], axis=0, keepdims=True)

    row = pl.BlockSpec((tb, d), lambda i: (i, 0))
    vec = pl.BlockSpec((1, d), lambda i: (0, 0))
    vshape = jax.ShapeDtypeStruct((1, d), F32)
    return pl.pallas_call(
        body, name=name, grid=(nsteps,), in_specs=[row, row, vec, vec, row], out_specs=[row, vec, vec, vec],
        out_shape=[jax.ShapeDtypeStruct((s, d), F32), vshape, vshape, vshape],
        scratch_shapes=[pltpu.VMEM((SUBLANES, d), F32)] * 3, compiler_params=_params(1))(dh, x, g, scale, dx_in)


def _gate_bwd(name, dx, y, gvec, coef):
    s, d = dx.shape
    tb = _row_tile(s)
    nsteps = s // tb

    def body(dx_ref, y_ref, g_ref, dy_ref, dg_ref, acc):
        i = pl.program_id(0)
        dxv = dx_ref[...]
        dy_ref[...] = (dxv * g_ref[...]).astype(BF16)
        _accumulate(i, acc, _fold_rows(dxv * y_ref[...]))

        @pl.when(i == nsteps - 1)
        def _():
            dg_ref[...] = coef * jnp.sum(acc[...], axis=0, keepdims=True)

    row = pl.BlockSpec((tb, d), lambda i: (i, 0))
    vec = pl.BlockSpec((1, d), lambda i: (0, 0))
    return pl.pallas_call(
        body, name=name, grid=(nsteps,), in_specs=[row, row, vec], out_specs=[row, vec],
        out_shape=[jax.ShapeDtypeStruct((s, d), BF16), jax.ShapeDtypeStruct((1, d), F32)],
        scratch_shapes=[pltpu.VMEM((SUBLANES, d), F32)], compiler_params=_params(1))(dx, y, gvec)


def _final_loss(name, x, target, g):
    s, d = x.shape
    tb = _row_tile(s)
    nsteps = s // tb

    def body(x_ref, t_ref, g_ref, dx_ref, loss_ref, dg_ref, a_loss, a_g):
        i = pl.program_id(0)
        xv, gv = x_ref[...], g_ref[...]
        rstd = lax.rsqrt(jnp.mean(xv * xv, axis=-1, keepdims=True) + EPS)
        xhat = xv * rstd
        err = xhat * gv - t_ref[...]
        dyv = err * (1.0 / d)
        dn_g = dyv * gv
        dx_ref[...] = rstd * (dn_g - xhat * (jnp.sum(dn_g * xhat, axis=-1, keepdims=True) * (1.0 / d)))
        _accumulate(i, a_loss, _fold_rows(err * err))
        _accumulate(i, a_g, _fold_rows(dyv * xhat))

        @pl.when(i == nsteps - 1)
        def _():
            total = jnp.sum(jnp.sum(a_loss[...], axis=0, keepdims=True), axis=1, keepdims=True) * (0.5 / d)
            loss_ref[...] = jnp.broadcast_to(total, (1, LANES))
            dg_ref[...] = jnp.sum(a_g[...], axis=0, keepdims=True)

    row = pl.BlockSpec((tb, d), lambda i: (i, 0))
    vec = pl.BlockSpec((1, d), lambda i: (0, 0))
    return pl.pallas_call(
        body, name=name, grid=(nsteps,), in_specs=[row, row, vec],
        out_specs=[row, pl.BlockSpec((1, LANES), lambda i: (0, 0)), vec],
        out_shape=[jax.ShapeDtypeStruct((s, d), F32), jax.ShapeDtypeStruct((1, LANES), F32),
                   jax.ShapeDtypeStruct((1, d), F32)],
        scratch_shapes=[pltpu.VMEM((SUBLANES, d), F32)] * 2, compiler_params=_params(1))(x, target, g)


def _shift_rows(cur, halo, n):
    rolled = pltpu.roll(cur, n, 0)
    row = lax.broadcasted_iota(jnp.int32, cur.shape, 0)
    for r in range(n):
        rolled = jnp.where(row == r, halo[SUBLANES - n + r:SUBLANES - n + r + 1, :], rolled)
    return rolled


def _shift_rows_up(cur, halo, n):
    tb = cur.shape[0]
    rolled = pltpu.roll(cur, tb - n, 0)
    row = lax.broadcasted_iota(jnp.int32, cur.shape, 0)
    for r in range(n):
        rolled = jnp.where(row == tb - n + r, halo[r:r + 1, :], rolled)
    return rolled


def _conv_specs(s, d, tb, tn):
    nd = d // tn
    hb = tb // SUBLANES
    cur = lambda part: pl.BlockSpec((tb, tn), lambda i, j: (i, j + part * nd))
    prev = lambda part: pl.BlockSpec((SUBLANES, tn), lambda i, j: (jnp.maximum(i * hb - 1, 0), j + part * nd))
    nxt = lambda part: pl.BlockSpec((SUBLANES, tn), lambda i, j: (jnp.minimum((i + 1) * hb, s // SUBLANES - 1),
                                                                  j + part * nd))
    return cur, prev, nxt


def _conv_fwd(name, p, conv_w, conv_b):
    s, d = p.shape[0], p.shape[1] // 3
    tb, tn = _row_tile(s), _tile(d, 512, LANES)
    cur, prev, _ = _conv_specs(s, d, tb, tn)

    def body(bg_ref, cg_ref, xv_ref, cgp_ref, xvp_ref, w_ref, b_ref, z_ref):
        i = pl.program_id(0)
        u = cg_ref[...] * xv_ref[...]
        up = jnp.where(i > 0, cgp_ref[...] * xvp_ref[...], 0.0)
        w = w_ref[...]
        uc = w[0:1, :] * _shift_rows(u, up, 2) + w[1:2, :] * _shift_rows(u, up, 1) + w[2:3, :] * u + b_ref[...]
        z_ref[...] = (bg_ref[...] * uc).astype(BF16)

    return pl.pallas_call(
        body, name=name, grid=(s // tb, d // tn),
        in_specs=[cur(0), cur(1), cur(2), prev(1), prev(2), pl.BlockSpec((3, tn), lambda i, j: (0, j)),
                  pl.BlockSpec((1, tn), lambda i, j: (0, j))],
        out_specs=pl.BlockSpec((tb, tn), lambda i, j: (i, j)),
        out_shape=jax.ShapeDtypeStruct((s, d), BF16), compiler_params=_params(2))(p, p, p, p, p, conv_w, conv_b)


def _conv_bwd(name, dz, p, conv_w, conv_b):
    s, d = dz.shape
    tb, tn = _row_tile(s), _tile(d, 512, LANES)
    nsteps = s // tb
    cur, prev, nxt = _conv_specs(s, d, tb, tn)

    def body(dz_ref, bg_ref, cg_ref, xv_ref, cgp_ref, xvp_ref, dzn_ref, bgn_ref, w_ref, b_ref,
             dp_ref, dw_ref, db_ref, a_w0, a_w1, a_w2, a_b):
        j, i = pl.program_id(0), pl.program_id(1)
        cg, xv, bg, dzv = cg_ref[...], xv_ref[...], bg_ref[...], dz_ref[...]
        u = cg * xv
        up = jnp.where(i > 0, cgp_ref[...] * xvp_ref[...], 0.0)
        w = w_ref[...]
        u1, u2 = _shift_rows(u, up, 1), _shift_rows(u, up, 2)
        uc = w[0:1, :] * u2 + w[1:2, :] * u1 + w[2:3, :] * u + b_ref[...]
        duc = dzv * bg
        ducn = jnp.where(i < nsteps - 1, dzn_ref[...] * bgn_ref[...], 0.0)
        du = w[2:3, :] * duc + w[1:2, :] * _shift_rows_up(duc, ducn, 1) + w[0:1, :] * _shift_rows_up(duc, ducn, 2)
        dp_ref[0] = (dzv * uc).astype(BF16)
        dp_ref[1] = (du * xv).astype(BF16)
        dp_ref[2] = (du * cg).astype(BF16)
        _accumulate(i, a_w0, _fold_rows(duc * u2))
        _accumulate(i, a_w1, _fold_rows(duc * u1))
        _accumulate(i, a_w2, _fold_rows(duc * u))
        _accumulate(i, a_b, _fold_rows(duc))

        @pl.when(i == nsteps - 1)
        def _():
            dw_ref[0:1, :] = jnp.sum(a_w0[...], axis=0, keepdims=True)
            dw_ref[1:2, :] = jnp.sum(a_w1[...], axis=0, keepdims=True)
            dw_ref[2:3, :] = jnp.sum(a_w2[...], axis=0, keepdims=True)
            db_ref[...] = jnp.sum(a_b[...], axis=0, keepdims=True)

    swap = lambda spec: pl.BlockSpec(spec.block_shape, lambda j, i, _m=spec.index_map: _m(i, j))
    return pl.pallas_call(
        body, name=name, grid=(d // tn, nsteps),
        in_specs=[pl.BlockSpec((tb, tn), lambda j, i: (i, j)), swap(cur(0)), swap(cur(1)), swap(cur(2)),
                  swap(prev(1)), swap(prev(2)), swap(nxt(0)), swap(nxt(0)),
                  pl.BlockSpec((3, tn), lambda j, i: (0, j)), pl.BlockSpec((1, tn), lambda j, i: (0, j))],
        out_specs=[pl.BlockSpec((3, tb, tn), lambda j, i: (0, i, j)), pl.BlockSpec((3, tn), lambda j, i: (0, j)),
                   pl.BlockSpec((1, tn), lambda j, i: (0, j))],
        out_shape=[jax.ShapeDtypeStruct((3, s, d), BF16), jax.ShapeDtypeStruct((3, d), F32),
                   jax.ShapeDtypeStruct((1, d), F32)],
        scratch_shapes=[pltpu.VMEM((SUBLANES, tn), F32)] * 4, compiler_params=_params(2),
    )(dz, p, p, p, p, p, dz, p, conv_w, conv_b)


def _block_cumsum(v):
    tb = v.shape[0]
    row = lax.broadcasted_iota(jnp.int32, v.shape, 0)
    sh = 1
    while sh < tb:
        v = v + jnp.where(row >= sh, pltpu.roll(v, sh, 0), 0.0)
        sh *= 2
    return v


def _fgate_fwd(name, zf, b_f):
    s = zf.shape[0]
    tb = _tile(s, 512, SUBLANES)

    def body(z_ref, b_ref, f_ref, carry):
        i = pl.program_id(0)
        z = z_ref[...] + b_ref[...]
        ls = jnp.minimum(z, 0.0) - jnp.log(1.0 + jnp.exp(-jnp.abs(z)))
        run = _block_cumsum(ls)

        @pl.when(i == 0)
        def _():
            carry[...] = jnp.zeros_like(carry)

        out = run + carry[...]
        f_ref[...] = out
        carry[...] = out[tb - 1:tb, :]

    blk = pl.BlockSpec((tb, LANES), lambda i: (i, 0))
    return pl.pallas_call(
        body, name=name, grid=(s // tb,), in_specs=[blk, pl.BlockSpec((1, LANES), lambda i: (0, 0))], out_specs=blk,
        out_shape=jax.ShapeDtypeStruct((s, LANES), F32), scratch_shapes=[pltpu.VMEM((1, LANES), F32)],
        compiler_params=_params(1))(zf, b_f)


def _fgate_bwd(name, dfcum, zf, b_f):
    s = zf.shape[0]
    tb = _tile(s, 512, SUBLANES)
    nsteps = s // tb

    def body(df_ref, z_ref, b_ref, dz_ref, db_ref, carry, acc):
        i = pl.program_id(0)
        df = df_ref[...]
        incl = _block_cumsum(df)
        total = incl[tb - 1:tb, :]

        @pl.when(i == 0)
        def _():
            carry[...] = jnp.zeros_like(carry)

        suffix = total - incl + df + carry[...]
        carry[...] += total
        dzv = suffix * jax.nn.sigmoid(-(z_ref[...] + b_ref[...]))
        dz_ref[...] = dzv
        _accumulate(i, acc, _fold_rows(dzv))

        @pl.when(i == nsteps - 1)
        def _():
            db_ref[...] = jnp.sum(acc[...], axis=0, keepdims=True)

    blk = pl.BlockSpec((tb, LANES), lambda i: (nsteps - 1 - i, 0))
    vec = pl.BlockSpec((1, LANES), lambda i: (0, 0))
    return pl.pallas_call(
        body, name=name, grid=(nsteps,), in_specs=[blk, blk, vec], out_specs=[blk, vec],
        out_shape=[jax.ShapeDtypeStruct((s, LANES), F32), jax.ShapeDtypeStruct((1, LANES), F32)],
        scratch_shapes=[pltpu.VMEM((1, LANES), F32), pltpu.VMEM((SUBLANES, LANES), F32)],
        compiler_params=_params(1))(dfcum, zf, b_f)


def _lane_column(block, lane):
    sel = lax.broadcasted_iota(jnp.int32, block.shape, 1) == lane
    return jnp.sum(jnp.where(sel, block, 0.0), axis=1, keepdims=True)


def _attn_tiles(s):
    t = _tile(s, 512, LANES)
    return t, t


def _attn_fwd(name, q, kv, fcum, fcum_t, n_heads):
    s, d = q.shape
    dh = d // n_heads
    tq, tk = _attn_tiles(s)
    scale = 1.0 / math.sqrt(dh)

    def body(q_ref, k_ref, v_ref, f_ref, ft_ref, o_ref, l_ref):
        h, qi = pl.program_id(0), pl.program_id(1)
        qv = q_ref[...]
        fq = _lane_column(f_ref[...], h)

        def step(kj, carry, diagonal):
            m, l, acc = carry
            start = pl.multiple_of(kj * tk, tk)
            kb = k_ref[pl.ds(start, tk), :]
            vb = v_ref[pl.ds(start, tk), :]
            sc = lax.dot_general(qv, kb, _DN["nt"], preferred_element_type=F32) * scale
            sc = sc + fq - ft_ref[:, pl.ds(start, tk)]
            if diagonal:
                keep = lax.broadcasted_iota(jnp.int32, sc.shape, 0) >= lax.broadcasted_iota(jnp.int32, sc.shape, 1)
                sc = jnp.where(keep, sc, -jnp.inf)
            m_new = jnp.maximum(m, jnp.max(sc, axis=1, keepdims=True))
            alpha = jnp.exp(m - m_new)
            pr = jnp.exp(sc - m_new)
            l = alpha * l + jnp.sum(pr, axis=1, keepdims=True)
            acc = alpha * acc + lax.dot_general(pr.astype(BF16), vb, _DN["nn"], preferred_element_type=F32)
            return m_new, l, acc

        init = (jnp.full((tq, 1), -jnp.inf, F32), jnp.zeros((tq, 1), F32), jnp.zeros((tq, dh), F32))
        carry = lax.fori_loop(0, qi, lambda kj, cr: step(kj, cr, False), init)
        m, l, acc = step(qi, carry, True)
        o_ref[...] = (acc / l).astype(BF16)
        l_ref[...] = jnp.broadcast_to(m + jnp.log(l), (tq, LANES))

    nh = n_heads
    return pl.pallas_call(
        body, name=name, grid=(nh, s // tq),
        in_specs=[pl.BlockSpec((tq, dh), lambda h, i: (i, h)),
                  pl.BlockSpec((s, dh), lambda h, i: (0, h)),
                  pl.BlockSpec((s, dh), lambda h, i: (0, nh + h)),
                  pl.BlockSpec((tq, LANES), lambda h, i: (i, 0)),
                  pl.BlockSpec((None, 1, s), lambda h, i: (h, 0, 0))],
        out_specs=[pl.BlockSpec((tq, dh), lambda h, i: (i, h)),
                   pl.BlockSpec((None, tq, LANES), lambda h, i: (h, i, 0))],
        out_shape=[jax.ShapeDtypeStruct((s, d), BF16), jax.ShapeDtypeStruct((nh, s, LANES), F32)],
        compiler_params=_params(2))(q, kv, kv, fcum, fcum_t)


def _attn_delta(name, d_o, o, n_heads):
    s, d = d_o.shape
    dh = d // n_heads
    tb = _row_tile(s)

    def body(do_ref, o_ref, dl_ref, dob_ref):
        dov = do_ref[...]
        prod = dov * o_ref[...].astype(F32)
        lane = lax.broadcasted_iota(jnp.int32, (tb, LANES), 1)
        out = jnp.zeros((tb, LANES), F32)
        for h in range(n_heads):
            out = jnp.where(lane == h, jnp.sum(prod[:, h * dh:(h + 1) * dh], axis=1, keepdims=True), out)
        dl_ref[...] = out
        dob_ref[...] = dov.astype(BF16)

    row = pl.BlockSpec((tb, d), lambda i: (i, 0))
    return pl.pallas_call(
        body, name=name, grid=(s // tb,), in_specs=[row, row],
        out_specs=[pl.BlockSpec((tb, LANES), lambda i: (i, 0)), row],
        out_shape=[jax.ShapeDtypeStruct((s, LANES), F32), jax.ShapeDtypeStruct((s, d), BF16)],
        compiler_params=_params(1))(d_o, o)


def _attn_dq(name, q, kv, d_o, fcum, fcum_t, lse, delta, n_heads):
    s, d = q.shape
    dh = d // n_heads
    tq, tk = _attn_tiles(s)
    scale = 1.0 / math.sqrt(dh)

    def body(q_ref, k_ref, v_ref, do_ref, f_ref, ft_ref, l_ref, dl_ref, dq_ref, df_ref):
        h, qi = pl.program_id(0), pl.program_id(1)
        qv, dov = q_ref[...], do_ref[...]
        fq = _lane_column(f_ref[...], h)
        lse_col = l_ref[:, 0:1]
        dl_col = _lane_column(dl_ref[...], h)

        def step(kj, carry, diagonal):
            acc, dfq = carry
            start = pl.multiple_of(kj * tk, tk)
            kb = k_ref[pl.ds(start, tk), :]
            vb = v_ref[pl.ds(start, tk), :]
            sc = lax.dot_general(qv, kb, _DN["nt"], preferred_element_type=F32) * scale
            sc = sc + fq - ft_ref[:, pl.ds(start, tk)]
            pr = jnp.exp(sc - lse_col)
            if diagonal:
                keep = lax.broadcasted_iota(jnp.int32, sc.shape, 0) >= lax.broadcasted_iota(jnp.int32, sc.shape, 1)
                pr = jnp.where(keep, pr, 0.0)
            dp = lax.dot_general(dov, vb, _DN["nt"], preferred_element_type=F32)
            ds = pr * (dp - dl_col)
            acc = acc + lax.dot_general(ds.astype(BF16), kb, _DN["nn"], preferred_element_type=F32)
            return acc, dfq + jnp.sum(ds, axis=1, keepdims=True)

        init = (jnp.zeros((tq, dh), F32), jnp.zeros((tq, 1), F32))
        carry = lax.fori_loop(0, qi, lambda kj, cr: step(kj, cr, False), init)
        acc, dfq = step(qi, carry, True)
        dq_ref[...] = (acc * scale).astype(BF16)
        df_ref[...] = jnp.broadcast_to(dfq, (tq, LANES))

    nh = n_heads
    qspec = pl.BlockSpec((tq, dh), lambda h, i: (i, h))
    lanes = pl.BlockSpec((tq, LANES), lambda h, i: (i, 0))
    return pl.pallas_call(
        body, name=name, grid=(nh, s // tq),
        in_specs=[qspec, pl.BlockSpec((s, dh), lambda h, i: (0, h)), pl.BlockSpec((s, dh), lambda h, i: (0, nh + h)),
                  qspec, lanes, pl.BlockSpec((None, 1, s), lambda h, i: (h, 0, 0)),
                  pl.BlockSpec((None, tq, LANES), lambda h, i: (h, i, 0)), lanes],
        out_specs=[qspec, pl.BlockSpec((None, tq, LANES), lambda h, i: (h, i, 0))],
        out_shape=[jax.ShapeDtypeStruct((s, d), BF16), jax.ShapeDtypeStruct((nh, s, LANES), F32)],
        compiler_params=_params(2))(q, kv, kv, d_o, fcum, fcum_t, lse, delta)


def _attn_dkv(name, q, kv, d_o, fcum, fcum_t, lse_t, delta_t, n_heads):
    s, d = q.shape
    dh = d // n_heads
    tq, tk = _attn_tiles(s)
    nq = s // tq
    scale = 1.0 / math.sqrt(dh)

    def body(q_ref, k_ref, v_ref, do_ref, f_ref, ft_ref, lt_ref, dlt_ref, dkv_ref, df_ref):
        h, kj = pl.program_id(0), pl.program_id(1)
        kb, vb = k_ref[...], v_ref[...]
        fk = _lane_column(f_ref[...], h)

        def step(qi, carry, diagonal):
            dk, dv, dfk = carry
            start = pl.multiple_of(qi * tq, tq)
            qb = q_ref[pl.ds(start, tq), :]
            dob = do_ref[pl.ds(start, tq), :]
            sc = lax.dot_general(kb, qb, _DN["nt"], preferred_element_type=F32) * scale
            sc = sc + ft_ref[:, pl.ds(start, tq)] - fk
            pr = jnp.exp(sc - lt_ref[:, pl.ds(start, tq)])
            if diagonal:
                keep = lax.broadcasted_iota(jnp.int32, sc.shape, 1) >= lax.broadcasted_iota(jnp.int32, sc.shape, 0)
                pr = jnp.where(keep, pr, 0.0)
            dv = dv + lax.dot_general(pr.astype(BF16), dob, _DN["nn"], preferred_element_type=F32)
            dp = lax.dot_general(vb, dob, _DN["nt"], preferred_element_type=F32)
            ds = pr * (dp - dlt_ref[:, pl.ds(start, tq)])
            dk = dk + lax.dot_general(ds.astype(BF16), qb, _DN["nn"], preferred_element_type=F32)
            dfk = dfk - jnp.sum(ds, axis=1, keepdims=True)
            return dk, dv, dfk

        init = (jnp.zeros((tk, dh), F32), jnp.zeros((tk, dh), F32), jnp.zeros((tk, 1), F32))
        carry = step(kj, init, True)
        dk, dv, dfk = lax.fori_loop(kj + 1, nq, lambda qi, cr: step(qi, cr, False), carry)
        dkv_ref[0] = (dk * scale).astype(BF16)
        dkv_ref[1] = dv.astype(BF16)
        df_ref[...] = jnp.broadcast_to(dfk, (tk, LANES))

    nh = n_heads
    full = lambda col0: pl.BlockSpec((s, dh), lambda h, j: (0, col0 + h))
    row_t = pl.BlockSpec((None, 1, s), lambda h, j: (h, 0, 0))
    return pl.pallas_call(
        body, name=name, grid=(nh, s // tk),
        in_specs=[full(0), pl.BlockSpec((tk, dh), lambda h, j: (j, h)), pl.BlockSpec((tk, dh), lambda h, j: (j, nh + h)),
                  full(0), pl.BlockSpec((tk, LANES), lambda h, j: (j, 0)), row_t, row_t, row_t],
        out_specs=[pl.BlockSpec((2, tk, dh), lambda h, j: (0, j, h)),
                   pl.BlockSpec((None, tk, LANES), lambda h, j: (h, j, 0))],
        out_shape=[jax.ShapeDtypeStruct((2, s, d), BF16), jax.ShapeDtypeStruct((nh, s, LANES), F32)],
        compiler_params=_params(2))(q, kv, kv, d_o, fcum, fcum_t, lse_t, delta_t)


def _sum_slots(name, g):
    n, r, lanes = g.shape

    def body(g_ref, o_ref):
        total = g_ref[0]
        for i in range(1, n):
            total = total + g_ref[i]
        o_ref[...] = total

    return pl.pallas_call(
        body, name=name, out_shape=jax.ShapeDtypeStruct((r, lanes), F32),
        in_specs=[pl.BlockSpec(memory_space=pltpu.VMEM)], out_specs=pl.BlockSpec(memory_space=pltpu.VMEM),
        compiler_params=pltpu.CompilerParams(vmem_limit_bytes=VMEM_LIMIT))(g)


def _adamw(name, parts, w, m, v):
    npart, r, c = parts.shape
    tb = _tile(r, max(SUBLANES, (2**18 // c) // SUBLANES * SUBLANES), SUBLANES)
    c1 = 1.0 - ADAM_B1 ** ADAM_STEP
    c2 = 1.0 - ADAM_B2 ** ADAM_STEP

    def body(p_ref, w_ref, m_ref, v_ref, g_out, d_out, m_out, v_out):
        g = p_ref[0].astype(F32)
        for i in range(1, npart):
            g = g + p_ref[i].astype(F32)
        m_new = ADAM_B1 * m_ref[...] + (1.0 - ADAM_B1) * g
        v_new = ADAM_B2 * v_ref[...] + (1.0 - ADAM_B2) * (g * g)
        m_hat = m_new / c1
        v_hat = v_new / c2
        g_out[...] = g
        d_out[...] = -ADAM_LR * (m_hat / (jnp.sqrt(v_hat) + ADAM_EPS) + ADAM_WD * w_ref[...])
        m_out[...] = m_new
        v_out[...] = v_new

    blk = pl.BlockSpec((tb, c), lambda i: (i, 0))
    shape = jax.ShapeDtypeStruct((r, c), F32)
    return pl.pallas_call(
        body, name=name, grid=(r // tb,), in_specs=[pl.BlockSpec((npart, tb, c), lambda i: (0, i, 0)), blk, blk, blk],
        out_specs=[blk] * 4, out_shape=[shape] * 4, compiler_params=_params(1))(parts, w, m, v)


def _pack(vectors):
    flat = jnp.concatenate([v.reshape(-1).astype(F32) for v in vectors])
    pad = (-flat.shape[0]) % (SUBLANES * LANES)
    return jnp.pad(flat, (0, pad)).reshape(-1, LANES)


def _unpack(flat, shapes):
    out, pos = [], 0
    for shp in shapes:
        size = math.prod(shp)
        out.append(flat[..., pos:pos + size].reshape(flat.shape[:-1] + tuple(shp)))
        pos += size
    return out


def _unshard_last(g, lead):
    nd = len(lead)
    return jnp.moveaxis(g, 0, nd).reshape(tuple(lead) + (-1,))


def _my_slice(full, me, width, axis):
    return lax.dynamic_slice_in_dim(full, me * width, width, axis)


def kernel(x, c, norm_g, w_ada, b_ada, w_ffn_in, w_ffn_out, w_conv_in, conv_w, conv_b, w_conv_out, kv_norm_g, w_ada_kv, b_ada_kv, w_kvf, b_fgate, w_q, w_o, final_g, loss_target, m_norm_g, m_w_ada, m_b_ada, m_w_ffn_in, m_w_ffn_out, m_w_conv_in, m_conv_w, m_conv_b, m_w_conv_out, m_kv_norm_g, m_w_ada_kv, m_b_ada_kv, m_w_kvf, m_b_fgate, m_w_q, m_w_o, m_final_g, v_norm_g, v_w_ada, v_b_ada, v_w_ffn_in, v_w_ffn_out, v_w_conv_in, v_conv_w, v_conv_b, v_w_conv_out, v_kv_norm_g, v_w_ada_kv, v_b_ada_kv, v_w_kvf, v_b_fgate, v_w_q, v_w_o, v_final_g):
    s, d = x.shape[1], x.shape[2]
    n_heads = b_fgate.shape[0]
    n_layers = w_ada.shape[0]
    dsh = d // N_DEV
    f = w_ffn_out.shape[2] * N_DEV
    ada_w = w_ada.shape[2]
    kv_w = w_ada_kv.shape[1]
    kvf_w = w_kvf.shape[1]
    assert n_layers == 2 and w_conv_in.shape[0] == 1 and w_q.shape[0] == 1
    assert (d // n_heads) % LANES == 0 and n_heads <= LANES and N_DEV * kvf_w == 2 * d + n_heads
    me = 4 * lax.axis_index("x") + 2 * lax.axis_index("y") + lax.axis_index("c")
    x0, target = x[0], loss_target[0]

    small = _allgather_small("gather_small_params", _pack([c, norm_g, conv_w, conv_b]))
    c_all, ng_sh, cw_sh, cb_sh = _unpack(small.reshape(N_DEV, -1), [(d,), (n_layers, 3, dsh), (3, dsh), (dsh,)])
    norm_g_full = _unshard_last(ng_sh, (n_layers, 3))
    conv_w_full = _unshard_last(cw_sh, (3,))
    conv_b_full = _unshard_last(cb_sh, ()).reshape(1, d)
    c_rows = jnp.pad(c_all, ((0, 16 - N_DEV), (0, 0)))

    ada_cols = [
        _mm_bias_rows(f"ada_rows_{l}", c_rows, w_ada[l], _my_slice(b_ada[l], me, ada_w, 0).reshape(1, ada_w), _silu)
        for l in range(n_layers)]
    ada_cols.append(_mm_bias_rows("ada_rows_kv", c_rows, w_ada_kv, _my_slice(b_ada_kv, me, kv_w, 0).reshape(1, kv_w),
                                  _silu))
    ada_part = jnp.concatenate([a[:N_DEV] for a in ada_cols], axis=1)
    ada_all = _allgather_small("gather_ada_rows", _pack([ada_part]))
    ada_all = ada_all.reshape(N_DEV, -1)[:, :ada_part.size].reshape(N_DEV, N_DEV, -1)
    ada_mine = lax.dynamic_index_in_dim(ada_all, me, axis=1, keepdims=False)
    ada = [ada_mine[:, l * ada_w:(l + 1) * ada_w].reshape(3, 3, 1, d) for l in range(n_layers)]
    ada_kv = ada_mine[:, n_layers * ada_w:].reshape(2, 1, d)

    shards = [w_ffn_in, w_ffn_out, w_conv_in[0], w_conv_out[0], w_kvf, w_q[0], w_o[0]]
    g_in, g_out, g_ci, g_co, g_kvf, g_q, g_o = _allgather_hbm("gather_weights", [w.astype(BF16) for w in shards])
    wf_in = _unshard_last(g_in, (n_layers, 2, d))
    wf_out = jnp.moveaxis(g_out, 0, 2).reshape(n_layers, 2, f, d)
    wc_in = _unshard_last(g_ci, (d,))
    wc_out, wq, wo = g_co.reshape(d, d), g_q.reshape(d, d), g_o.reshape(d, d)
    wkvf = _unshard_last(g_kvf, (d,))
    wkv = wkvf[:, :2 * d]
    wf_gate = jnp.pad(wkvf[:, 2 * d:], ((0, 0), (0, LANES - n_heads)))
    b_f = jnp.pad(b_fgate, (0, LANES - n_heads)).reshape(1, LANES)

    def ffn_fwd(tag, xs, l, sub, which):
        shift, scale, gate = ada[l][sub]
        h = _norm_mod(f"norm_{tag}", xs, norm_g_full[l, sub].reshape(1, d), shift, scale)
        ab, u = _mm_swiglu(f"ffn_in_{tag}", h, wf_in[l, which], f)
        y, x_next = _mm_residual(f"ffn_out_{tag}", u, wf_out[l, which], xs, 0.5 * gate)
        return x_next, (xs, h, ab, u, y)

    x1, save_f0 = ffn_fwd("l0a", x0, 0, 0, 0)
    shift, scale, gate = ada[0][1]
    h_c = _norm_mod("norm_conv", x1, norm_g_full[0, 1].reshape(1, d), shift, scale)
    p_c = _mm_nn("conv_in", h_c, wc_in, F32)
    z_c = _conv_fwd("conv_mix", p_c, conv_w_full, conv_b_full)
    y_c, x2 = _mm_residual("conv_out", z_c, wc_out, x1, gate)
    x3, save_f1 = ffn_fwd("l0b", x2, 0, 2, 1)
    x4, save_f2 = ffn_fwd("l1a", x3, 1, 0, 0)
    shift, scale, gate_a = ada[1][1]
    h_q = _norm_mod("norm_q", x4, norm_g_full[1, 1].reshape(1, d), shift, scale)
    h_kv = _norm_mod("norm_kv", x4, kv_norm_g.reshape(1, d), ada_kv[0], ada_kv[1])
    q = _mm_nn("attn_q", h_q, wq, BF16)
    kv = _mm_nn("attn_kv", h_kv, wkv, BF16)
    zf = _mm_nn("attn_fgate", h_kv, wf_gate, F32)
    fcum = _fgate_fwd("fgate_cumsum", zf, b_f)
    fcum_t = fcum[:, :n_heads].T.reshape(n_heads, 1, s)
    o, lse = _attn_fwd("attn_fwd", q, kv, fcum, fcum_t, n_heads)
    y_a, x5 = _mm_residual("attn_out", o, wo, x4, gate_a)
    x6, save_f3 = ffn_fwd("l1b", x5, 1, 2, 1)
    dx, loss_part, d_final_g = _final_loss("final_loss", x6, target, final_g.reshape(1, d))

    d_ada = [[[None] * 3 for _ in range(3)] for _ in range(n_layers)]
    d_norm_g = [[None] * 3 for _ in range(n_layers)]
    dw_in = [[None] * 2 for _ in range(n_layers)]
    dw_out = [[None] * 2 for _ in range(n_layers)]

    def ffn_bwd(tag, dxs, saved, l, sub, which):
        xs, h, ab, u, y = saved
        _, scale, gate = ada[l][sub]
        dy, d_ada[l][sub][2] = _gate_bwd(f"gate_bwd_{tag}", dxs, y, 0.5 * gate, 0.5)
        dab = _mm_nt_swiglu_bwd(f"ffn_out_bwd_{tag}", dy, wf_out[l, which], ab)
        dw_out[l][which] = _mm_tn(f"ffn_out_dw_{tag}", u, dy, BF16)
        dh = _mm_nt_parts(f"ffn_in_bwd_{tag}", dab, wf_in[l, which], F32)
        dw_in[l][which] = _mm_tn(f"ffn_in_dw_{tag}", h, dab, BF16)
        dxs, d_ada[l][sub][0], d_ada[l][sub][1], d_norm_g[l][sub] = _norm_mod_bwd(
            f"norm_bwd_{tag}", dh, xs, norm_g_full[l, sub].reshape(1, d), scale, dxs)
        return dxs

    dx = ffn_bwd("l1b", dx, save_f3, 1, 2, 1)
    dy, d_ada[1][1][2] = _gate_bwd("gate_bwd_attn", dx, y_a, gate_a, 1.0)
    d_o = _mm_nt("attn_out_bwd", dy, wo, F32)
    dwo = _mm_tn("attn_out_dw", o, dy, BF16)
    delta, d_ob = _attn_delta("attn_delta", d_o, o, n_heads)
    delta_t = delta[:, :n_heads].T.reshape(n_heads, 1, s)
    lse_t = lse[:, :, 0].reshape(n_heads, 1, s)
    dq, dfq = _attn_dq("attn_dq", q, kv, d_ob, fcum, fcum_t, lse, delta, n_heads)
    dkv, dfk = _attn_dkv("attn_dkv", q, kv, d_ob, fcum, fcum_t, lse_t, delta_t, n_heads)
    dfcum = jnp.pad((dfq[:, :, 0] + dfk[:, :, 0]).T, ((0, 0), (0, LANES - n_heads)))
    dzf, d_bf = _fgate_bwd("fgate_bwd", dfcum, zf, b_f)
    dh_q = _mm_nt("attn_q_bwd", dq, wq, F32)
    dwq = _mm_tn("attn_q_dw", h_q, dq, BF16)
    dh_kv = _mm_nt_parts("attn_kv_bwd", dkv, wkv, F32)
    dh_kv = _mm_nt("attn_fgate_bwd", dzf, wf_gate, F32, extra_add=dh_kv)
    dwkv = _mm_tn("attn_kv_dw", h_kv, dkv, BF16)
    dwf = _mm_tn("attn_fgate_dw", h_kv, dzf, BF16)
    dx, d_ada[1][1][0], d_ada[1][1][1], d_norm_g[1][1] = _norm_mod_bwd(
        "norm_bwd_q", dh_q, x4, norm_g_full[1, 1].reshape(1, d), ada[1][1][1], dx)
    dx, d_kv_shift, d_kv_scale, d_kv_norm_g = _norm_mod_bwd(
        "norm_bwd_kv", dh_kv, x4, kv_norm_g.reshape(1, d), ada_kv[1], dx)
    dx = ffn_bwd("l1a", dx, save_f2, 1, 0, 0)
    dx = ffn_bwd("l0b", dx, save_f1, 0, 2, 1)
    dy, d_ada[0][1][2] = _gate_bwd("gate_bwd_conv", dx, y_c, ada[0][1][2], 1.0)
    dz = _mm_nt("conv_out_bwd", dy, wc_out, F32)
    dwc_out = _mm_tn("conv_out_dw", z_c, dy, BF16)
    dp, d_conv_w, d_conv_b = _conv_bwd("conv_mix_bwd", dz, p_c, conv_w_full, conv_b_full)
    dh_c = _mm_nt_parts("conv_in_bwd", dp, wc_in, F32)
    dwc_in = _mm_tn("conv_in_dw", h_c, dp, BF16)
    dx, d_ada[0][1][0], d_ada[0][1][1], d_norm_g[0][1] = _norm_mod_bwd(
        "norm_bwd_conv", dh_c, x1, norm_g_full[0, 1].reshape(1, d), ada[0][1][1], dx)
    dx = ffn_bwd("l0a", dx, save_f0, 0, 0, 0)
    grad_x = dx.reshape(1, s, d)

    d_ada_flat = jnp.concatenate([v.reshape(-1) for l in range(n_layers) for sub in range(3) for v in d_ada[l][sub]])
    d_norm_flat = jnp.concatenate([d_norm_g[l][sub].reshape(-1) for l in range(n_layers) for sub in range(3)])
    small_shapes = [(n_layers * 9 * d,), (2 * d,), (n_layers * 3 * d,), (3, d), (d,), (d,), (LANES,), (d,), (LANES,)]
    small_parts = _pack([d_ada_flat, d_kv_shift, d_kv_scale, d_norm_flat, d_conv_w, d_conv_b, d_kv_norm_g, d_bf,
                         d_final_g, loss_part])
    small_all = _allgather_small("gather_small_grads", small_parts)
    small_sum = _sum_slots("sum_small_grads", small_all).reshape(-1)
    g_b_ada, g_b_ada_kv, g_norm_full, g_conv_w_full, g_conv_b_full, g_kv_norm, g_bf, g_final, loss_v = _unpack(
        small_sum, small_shapes)
    loss = loss_v[0]
    d_ada_rows = small_all.reshape(N_DEV, -1)[:, :n_layers * 9 * d + 2 * d]
    d_ada_rows = jnp.pad(d_ada_rows, ((0, 16 - N_DEV), (0, 0)))

    results = {}

    def update(key, parts, w, m, v):
        shape = w.shape
        c_dim = shape[-1]
        outs = _adamw(f"adamw_{key}", parts.reshape(parts.shape[0], -1, c_dim), w.reshape(-1, c_dim),
                      m.reshape(-1, c_dim), v.reshape(-1, c_dim))
        results[key] = [o.reshape(shape) for o in outs]

    g_w_ada = []
    for l in range(n_layers):
        cols = _my_slice(d_ada_rows[:, l * 9 * d:(l + 1) * 9 * d], me, ada_w, 1)
        g_w_ada.append(_mm_tn(f"ada_dw_{l}", c_rows, cols, F32, a_fn=_silu))
    update("w_ada", jnp.stack(g_w_ada).reshape(1, n_layers * d, ada_w), w_ada, m_w_ada, v_w_ada)
    cols = _my_slice(d_ada_rows[:, n_layers * 9 * d:], me, kv_w, 1)
    update("w_ada_kv", _mm_tn("ada_dw_kv", c_rows, cols, F32, a_fn=_silu).reshape(1, d, kv_w),
           w_ada_kv, m_w_ada_kv, v_w_ada_kv)

    p_in = jnp.stack([jnp.stack(r) for r in dw_in])
    p_in = jnp.moveaxis(p_in.reshape(n_layers, 2, d, N_DEV, -1), 3, 0)
    p_out = jnp.stack([jnp.stack(r) for r in dw_out])
    p_out = jnp.moveaxis(p_out.reshape(n_layers, 2, N_DEV, -1, d), 2, 0)
    p_ci = jnp.moveaxis(dwc_in.reshape(d, N_DEV, -1), 1, 0)[:, None]
    p_co = dwc_out.reshape(N_DEV, 1, dsh, d)
    dwkvf = jnp.concatenate([dwkv, dwf[:, :n_heads]], axis=1)
    p_kvf = jnp.moveaxis(dwkvf.reshape(d, N_DEV, kvf_w), 1, 0)
    p_q, p_o = dwq.reshape(N_DEV, 1, dsh, d), dwo.reshape(N_DEV, 1, dsh, d)
    r_in, r_out, r_ci, r_co, r_kvf, r_q, r_o = _alltoall_hbm(
        "scatter_weight_grads", [p_in, p_out, p_ci, p_co, p_kvf, p_q, p_o])
    update("w_ffn_in", r_in, w_ffn_in, m_w_ffn_in, v_w_ffn_in)
    update("w_ffn_out", r_out, w_ffn_out, m_w_ffn_out, v_w_ffn_out)
    update("w_conv_in", r_ci, w_conv_in, m_w_conv_in, v_w_conv_in)
    update("w_conv_out", r_co, w_conv_out, m_w_conv_out, v_w_conv_out)
    update("w_kvf", r_kvf, w_kvf, m_w_kvf, v_w_kvf)
    update("w_q", r_q, w_q, m_w_q, v_w_q)
    update("w_o", r_o, w_o, m_w_o, v_w_o)

    small_keys = ["norm_g", "b_ada", "conv_w", "conv_b", "kv_norm_g", "b_ada_kv", "b_fgate", "final_g"]
    small_w = [norm_g, b_ada, conv_w, conv_b, kv_norm_g, b_ada_kv, b_fgate, final_g]
    small_m = [m_norm_g, m_b_ada, m_conv_w, m_conv_b, m_kv_norm_g, m_b_ada_kv, m_b_fgate, m_final_g]
    small_v = [v_norm_g, v_b_ada, v_conv_w, v_conv_b, v_kv_norm_g, v_b_ada_kv, v_b_fgate, v_final_g]
    small_g = [
        _my_slice(g_norm_full.reshape(n_layers, 3, d), me, dsh, 2), g_b_ada.reshape(b_ada.shape),
        _my_slice(g_conv_w_full, me, dsh, 1).reshape(conv_w.shape), _my_slice(g_conv_b_full, me, dsh, 0).reshape(
            conv_b.shape), g_kv_norm, g_b_ada_kv, g_bf[:n_heads], g_final]
    packed = _adamw("adamw_small", _pack(small_g)[None], _pack(small_w), _pack(small_m), _pack(small_v))
    for vals, idx in zip(packed, range(4)):
        for key, val in zip(small_keys, _unpack(vals.reshape(-1), [w.shape for w in small_w])):
            results.setdefault(key, [None] * 4)[idx] = val

    order = ["norm_g", "w_ada", "b_ada", "w_ffn_in", "w_ffn_out", "w_conv_in", "conv_w", "conv_b", "w_conv_out",
             "kv_norm_g", "w_ada_kv", "b_ada_kv", "w_kvf", "b_fgate", "w_q", "w_o", "final_g"]
    return (loss, grad_x, *[results[k][0] for k in order], *[results[k][1] for k in order],
            *[results[k][2] for k in order], *[results[k][3] for k in order])
```

```python
import functools
import math

import jax
import jax.numpy as jnp
from jax import lax
from jax.experimental import pallas as pl
from jax.experimental.pallas import tpu as pltpu

F32 = jnp.float32
BF16 = jnp.bfloat16
MESH = pl.DeviceIdType.MESH

N_DEV = 8
LANES = 128
SUBLANES = 8
VMEM_LIMIT = 56 * 2**20
EPS = 1e-6
ADAM_LR, ADAM_B1, ADAM_B2, ADAM_EPS, ADAM_WD, ADAM_STEP = 0.001, 0.9, 0.999, 1e-08, 0.01, 10

_DN = {"nn": (((1,), (0,)), ((), ())), "nt": (((1,), (1,)), ((), ())), "tn": (((0,), (0,)), ((), ()))}


def _tile(n, pref, align):
    t = min(pref, n)
    t -= t % align
    while t >= align:
        if n % t == 0:
            return t
        t -= align
    return n


def _params(n_grid):
    return pltpu.CompilerParams(vmem_limit_bytes=VMEM_LIMIT, dimension_semantics=("arbitrary",) * n_grid)


def _silu(v):
    return v * jax.nn.sigmoid(v)


def _position():
    x, y, c = lax.axis_index("x"), lax.axis_index("y"), lax.axis_index("c")
    return x, y, c


def _allgather_small(name, v):
    def body(v_ref, out_ref, send_sems, recv_sems, local_sem):
        x, y, c = _position()
        me, sibling = (x, y, c), (x, y, 1 - c)
        chips = [(1 - x, y), (x, 1 - y), (1 - x, 1 - y)]

        def slot(px, py, pc):
            return out_ref.at[4 * px + 2 * py + pc]

        def copy(k, block, to, src=None):
            return pltpu.make_async_remote_copy(
                src_ref=slot(*block) if src is None else src, dst_ref=slot(*block),
                send_sem=send_sems.at[k], recv_sem=recv_sems.at[k], device_id=to, device_id_type=MESH)

        mine = pltpu.make_async_copy(v_ref, slot(*me), local_sem)
        mine.start()
        first = [copy(0, me, sibling, src=v_ref)]
        first += [copy(1 + j, me, (*chip, c), src=v_ref) for j, chip in enumerate(chips)]
        for cp in first:
            cp.start()
        passed = [copy(4 + j, (*chip, c), sibling) for j, chip in enumerate(chips)]
        for j, chip in enumerate(chips):
            copy(1 + j, (*chip, c), me).wait_recv()
            passed[j].start()
        copy(0, sibling, me).wait_recv()
        for j, chip in enumerate(chips):
            copy(4 + j, (*chip, 1 - c), me).wait_recv()
        for cp in first + passed:
            cp.wait_send()
        mine.wait()

    return pl.pallas_call(
        body, name=name,
        out_shape=jax.ShapeDtypeStruct((N_DEV,) + v.shape, v.dtype),
        in_specs=[pl.BlockSpec(memory_space=pltpu.VMEM)],
        out_specs=pl.BlockSpec(memory_space=pltpu.VMEM),
        scratch_shapes=[pltpu.SemaphoreType.DMA((7,)), pltpu.SemaphoreType.DMA((7,)), pltpu.SemaphoreType.DMA],
        compiler_params=pltpu.CompilerParams(vmem_limit_bytes=VMEM_LIMIT),
    )(v)


def _allgather_hbm(name, shards):
    n = len(shards)

    def body(*refs):
        ins, outs = refs[:n], refs[n:2 * n]
        send_sems, recv_sems, local_sems = refs[2 * n:]
        x, y, c = _position()
        me, sibling = (x, y, c), (x, y, 1 - c)
        chips = [(1 - x, y), (x, 1 - y), (1 - x, 1 - y)]

        def copy(a, k, block, to, src=None):
            px, py, pc = block
            dst = outs[a].at[4 * px + 2 * py + pc]
            return pltpu.make_async_remote_copy(
                src_ref=dst if src is None else src, dst_ref=dst,
                send_sem=send_sems.at[7 * a + k], recv_sem=recv_sems.at[7 * a + k], device_id=to, device_id_type=MESH)

        local, first, passed = [], [], []
        for a in range(n):
            cp = pltpu.make_async_copy(ins[a], outs[a].at[4 * x + 2 * y + c], local_sems.at[a])
            cp.start()
            local.append(cp)
            mine = [copy(a, 1 + j, me, (*chip, c), src=ins[a]) for j, chip in enumerate(chips)]
            mine.append(copy(a, 0, me, sibling, src=ins[a]))
            for cp in mine:
                cp.start()
            first += mine
        for a in range(n):
            for j, chip in enumerate(chips):
                copy(a, 1 + j, (*chip, c), me).wait_recv()
                cp = copy(a, 4 + j, (*chip, c), sibling)
                cp.start()
                passed.append(cp)
        for a in range(n):
            copy(a, 0, sibling, me).wait_recv()
            for j, chip in enumerate(chips):
                copy(a, 4 + j, (*chip, 1 - c), me).wait_recv()
        for cp in first + passed:
            cp.wait_send()
        for cp in local:
            cp.wait()

    any_spec = pl.BlockSpec(memory_space=pl.ANY)
    return pl.pallas_call(
        body, name=name,
        out_shape=[jax.ShapeDtypeStruct((N_DEV,) + s.shape, s.dtype) for s in shards],
        in_specs=[any_spec] * n, out_specs=[any_spec] * n,
        scratch_shapes=[pltpu.SemaphoreType.DMA((7 * n,)), pltpu.SemaphoreType.DMA((7 * n,)),
                        pltpu.SemaphoreType.DMA((n,))],
    )(*shards)


def _exchange_sibling(name, parts):
    n = len(parts)

    def body(*refs):
        ins, outs = refs[:n], refs[n:2 * n]
        send_sems, recv_sems = refs[2 * n:]
        x, y, c = _position()
        copies = [pltpu.make_async_remote_copy(
            src_ref=ins[a].at[1 - c], dst_ref=outs[a], send_sem=send_sems.at[a], recv_sem=recv_sems.at[a],
            device_id=(x, y, 1 - c), device_id_type=MESH) for a in range(n)]
        for cp in copies:
            cp.start()
        for cp in copies:
            cp.wait()

    any_spec = pl.BlockSpec(memory_space=pl.ANY)
    return pl.pallas_call(
        body, name=name, out_shape=[jax.ShapeDtypeStruct(p.shape[1:], p.dtype) for p in parts],
        in_specs=[any_spec] * n, out_specs=[any_spec] * n,
        scratch_shapes=[pltpu.SemaphoreType.DMA((n,)), pltpu.SemaphoreType.DMA((n,))],
    )(*parts)


def _pair_add(name, mine, theirs, core):
    _, nchip, r, c = mine.shape
    tb = _tile(r, max(16, (2**19 // c) // 16 * 16), 16)

    def body(core_ref, a_ref, b_ref, o_ref):
        o_ref[...] = (a_ref[...].astype(F32) + b_ref[...].astype(F32)).astype(o_ref.dtype)

    return pl.pallas_call(
        body, name=name,
        grid_spec=pltpu.PrefetchScalarGridSpec(
            num_scalar_prefetch=1, grid=(nchip, r // tb),
            in_specs=[pl.BlockSpec((None, None, tb, c), lambda k, i, core_ref: (core_ref[0], k, i, 0)),
                      pl.BlockSpec((None, tb, c), lambda k, i, core_ref: (k, i, 0))],
            out_specs=pl.BlockSpec((None, tb, c), lambda k, i, core_ref: (k, i, 0))),
        out_shape=jax.ShapeDtypeStruct(theirs.shape, theirs.dtype), compiler_params=_params(2))(core, mine, theirs)


def _exchange_chips(name, parts):
    n = len(parts)

    def body(*refs):
        ins, outs = refs[:n], refs[n:2 * n]
        send_sems, recv_sems, local_sems = refs[2 * n:]
        x, y, c = _position()
        my_chip = 2 * x + y

        def peer(k):
            px = 1 - x if k & 2 else x
            py = 1 - y if k & 1 else y
            return (px, py, c), 2 * px + py

        def copy(a, k, landed):
            to, to_chip = peer(k)
            return pltpu.make_async_remote_copy(
                src_ref=ins[a].at[to_chip], dst_ref=outs[a].at[to_chip if landed else my_chip],
                send_sem=send_sems.at[3 * a + k - 1], recv_sem=recv_sems.at[3 * a + k - 1],
                device_id=to, device_id_type=MESH)

        local, sent = [], []
        for a in range(n):
            cp = pltpu.make_async_copy(ins[a].at[my_chip], outs[a].at[my_chip], local_sems.at[a])
            cp.start()
            local.append(cp)
            for k in range(1, 4):
                cp = copy(a, k, False)
                cp.start()
                sent.append(cp)
        for a in range(n):
            for k in range(1, 4):
                copy(a, k, True).wait_recv()
        for cp in sent:
            cp.wait_send()
        for cp in local:
            cp.wait()

    any_spec = pl.BlockSpec(memory_space=pl.ANY)
    return pl.pallas_call(
        body, name=name, out_shape=[jax.ShapeDtypeStruct(p.shape, p.dtype) for p in parts],
        in_specs=[any_spec] * n, out_specs=[any_spec] * n,
        scratch_shapes=[pltpu.SemaphoreType.DMA((3 * n,)), pltpu.SemaphoreType.DMA((3 * n,)),
                        pltpu.SemaphoreType.DMA((n,))],
    )(*parts)


def _matmul(name, mode, grid, a, a_spec, bs, b_specs, extras, extra_specs, out_shapes, out_specs, acc_shape,
            epilogue, a_fn=None):
    nk, nb, ne, no = grid[2], len(bs), len(extras), len(out_shapes)

    def body(*refs):
        a_ref, b_refs, e_refs = refs[0], refs[1:1 + nb], refs[1 + nb:1 + nb + ne]
        o_refs, acc_refs = refs[1 + nb + ne:1 + nb + ne + no], refs[1 + nb + ne + no:]
        a_val = a_ref[...]
        if a_fn is not None:
            a_val = a_fn(a_val)
        a_val = a_val.astype(BF16)

        def product(b_ref):
            return lax.dot_general(a_val, b_ref[...].astype(BF16), _DN[mode], preferred_element_type=F32)

        def finish(accs):
            outs = epilogue(accs, [e[...] for e in e_refs])
            for o_ref, o in zip(o_refs, outs):
                o_ref[...] = o.astype(o_ref.dtype)

        if nk == 1:
            finish([product(b_ref) for b_ref in b_refs])
        else:
            kk = pl.program_id(2)

            @pl.when(kk == 0)
            def _():
                for acc in acc_refs:
                    acc[...] = jnp.zeros_like(acc)

            for acc, b_ref in zip(acc_refs, b_refs):
                acc[...] += product(b_ref)

            @pl.when(kk == nk - 1)
            def _():
                finish([acc[...] for acc in acc_refs])

    scratch = [] if nk == 1 else [pltpu.VMEM(acc_shape, F32) for _ in range(nb)]
    return pl.pallas_call(
        body, name=name, grid=grid, in_specs=[a_spec] + list(b_specs) + list(extra_specs), out_specs=list(out_specs),
        out_shape=list(out_shapes), scratch_shapes=scratch, compiler_params=_params(3),
    )(a, *bs, *extras)


def _mm_tiles(m, n, k, tiles):
    return _tile(m, tiles[0], 16), _tile(n, tiles[1], LANES), _tile(k, tiles[2], LANES)


FFN_TILES = {
    "l0a": {"in": (1024, 512, 512), "out": (1024, 1024, 512), "out_bwd": (1024, 512, 512),
            "in_bwd": (1024, 1024, 512), "out_dw": (512, 2048, 512), "in_dw": (2048, 512, 512)},
    "l0b": {"in": (1024, 512, 2048), "out": (1024, 256, 5632), "out_bwd": (1024, 512, 2048),
            "in_bwd": (1024, 1024, 1408), "out_dw": (512, 2048, 1024), "in_dw": (2048, 512, 1024)},
    "l1a": {"in": (512, 512, 2048), "out": (512, 512, 5632), "out_bwd": (512, 512, 2048),
            "in_bwd": (512, 512, 5632), "out_dw": (512, 1024, 512), "in_dw": (1024, 512, 512)},
    "l1b": {"in": (2048, 256, 2048), "out": (1024, 1024, 1408), "out_bwd": (2048, 256, 2048),
            "in_bwd": (1024, 512, 2816), "out_dw": (512, 1024, 2048), "in_dw": (1024, 512, 2048)},
}
FULL_K = (1024, 1024, 2048)
LOOP_K = (1024, 1024, 512)


def _mm_nn(name, a, b, out_dtype, b_col0=0, n=None, tiles=FULL_K):
    m, k = a.shape
    n = b.shape[1] if n is None else n
    tm, tn, tk = _mm_tiles(m, n, k, tiles)
    off = b_col0 // tn
    assert b_col0 % tn == 0
    return _matmul(
        name, "nn", (m // tm, n // tn, k // tk), a, pl.BlockSpec((tm, tk), lambda i, j, kk: (i, kk)),
        [b], [pl.BlockSpec((tk, tn), lambda i, j, kk: (kk, j + off))], [], [],
        [jax.ShapeDtypeStruct((m, n), out_dtype)], [pl.BlockSpec((tm, tn), lambda i, j, kk: (i, j))], (tm, tn),
        lambda accs, ex: accs)[0]


def _mm_bias_rows(name, a, b, bias, a_fn):
    m, k = a.shape
    n = b.shape[1]
    tn = _tile(n, 768, LANES)
    return _matmul(
        name, "nn", (1, n // tn, 1), a, pl.BlockSpec((m, k), lambda i, j, kk: (0, 0)),
        [b], [pl.BlockSpec((k, tn), lambda i, j, kk: (0, j))],
        [bias], [pl.BlockSpec((1, tn), lambda i, j, kk: (0, j))],
        [jax.ShapeDtypeStruct((m, n), F32)], [pl.BlockSpec((m, tn), lambda i, j, kk: (0, j))], (m, tn),
        lambda accs, ex: [accs[0] + ex[0]], a_fn=a_fn)[0]


def _mm_swiglu(name, h, w_in, f, tiles):
    s, d = h.shape
    tm, tn, tk = _mm_tiles(s, f, d, tiles)
    nf = f // tn

    def epilogue(accs, ex):
        a_pre, b_pre = accs
        return [jnp.stack([a_pre, b_pre]), _silu(a_pre) * b_pre]

    return _matmul(
        name, "nn", (s // tm, nf, d // tk), h, pl.BlockSpec((tm, tk), lambda i, j, kk: (i, kk)),
        [w_in, w_in], [pl.BlockSpec((tk, tn), lambda i, j, kk: (kk, j)),
                       pl.BlockSpec((tk, tn), lambda i, j, kk: (kk, j + nf))], [], [],
        [jax.ShapeDtypeStruct((2, s, f), F32), jax.ShapeDtypeStruct((s, f), BF16)],
        [pl.BlockSpec((2, tm, tn), lambda i, j, kk: (0, i, j)), pl.BlockSpec((tm, tn), lambda i, j, kk: (i, j))],
        (tm, tn), epilogue)


def _mm_residual(name, a, w, x_in, gvec, tiles=FULL_K):
    s, k = a.shape
    d = w.shape[1]
    tm, tn, tk = _mm_tiles(s, d, k, tiles)

    def epilogue(accs, ex):
        x_blk, g_row = ex
        return [accs[0], x_blk + g_row * accs[0]]

    tile = pl.BlockSpec((tm, tn), lambda i, j, kk: (i, j))
    return _matmul(
        name, "nn", (s // tm, d // tn, k // tk), a, pl.BlockSpec((tm, tk), lambda i, j, kk: (i, kk)),
        [w], [pl.BlockSpec((tk, tn), lambda i, j, kk: (kk, j))],
        [x_in, gvec], [tile, pl.BlockSpec((1, tn), lambda i, j, kk: (0, j))],
        [jax.ShapeDtypeStruct((s, d), F32), jax.ShapeDtypeStruct((s, d), F32)], [tile, tile], (tm, tn), epilogue)


def _mm_nt(name, a, w, out_dtype, w_col0=0, extra_add=None, tiles=FULL_K):
    s, k = a.shape
    n = w.shape[0]
    tm, tn, tk = _mm_tiles(s, n, k, tiles)
    off = w_col0 // tk
    assert w_col0 % tk == 0
    tile = pl.BlockSpec((tm, tn), lambda i, j, kk: (i, j))
    extras, especs = ([], []) if extra_add is None else ([extra_add], [tile])
    return _matmul(
        name, "nt", (s // tm, n // tn, k // tk), a, pl.BlockSpec((tm, tk), lambda i, j, kk: (i, kk)),
        [w], [pl.BlockSpec((tn, tk), lambda i, j, kk: (j, kk + off))], extras, especs,
        [jax.ShapeDtypeStruct((s, n), out_dtype)], [tile], (tm, tn),
        lambda accs, ex: [accs[0] + ex[0]] if ex else accs)[0]


def _mm_nt_parts(name, a3, w, out_dtype, tiles=LOOP_K):
    p, s, k = a3.shape
    n = w.shape[0]
    tm, tn, tk = _mm_tiles(s, n, k, tiles)
    nkp = k // tk
    return _matmul(
        name, "nt", (s // tm, n // tn, p * nkp), a3,
        pl.BlockSpec((None, tm, tk), lambda i, j, kk: (kk // nkp, i, kk % nkp)),
        [w], [pl.BlockSpec((tn, tk), lambda i, j, kk: (j, kk))], [], [],
        [jax.ShapeDtypeStruct((s, n), out_dtype)], [pl.BlockSpec((tm, tn), lambda i, j, kk: (i, j))], (tm, tn),
        lambda accs, ex: accs)[0]


def _mm_nt_swiglu_bwd(name, dy, w_out, ab, tiles):
    s, d = dy.shape
    f = w_out.shape[0]
    tm, tn, tk = _mm_tiles(s, f, d, tiles)

    def epilogue(accs, ex):
        du, a_pre, b_pre = accs[0], ex[0][0], ex[0][1]
        sig = jax.nn.sigmoid(a_pre)
        da = du * b_pre * (sig * (1.0 + a_pre * (1.0 - sig)))
        db = du * (a_pre * sig)
        return [jnp.stack([da, db])]

    pair = pl.BlockSpec((2, tm, tn), lambda i, j, kk: (0, i, j))
    return _matmul(
        name, "nt", (s // tm, f // tn, d // tk), dy, pl.BlockSpec((tm, tk), lambda i, j, kk: (i, kk)),
        [w_out], [pl.BlockSpec((tn, tk), lambda i, j, kk: (j, kk))], [ab], [pair],
        [jax.ShapeDtypeStruct((2, s, f), BF16)], [pair], (tm, tn), epilogue)[0]


def _mm_tn(name, a, b, out_dtype, a_fn=None, tiles=LOOP_K):
    s, m = a.shape
    tm, tk = _tile(m, tiles[0], LANES), _tile(s, tiles[2], 16)
    if b.ndim == 3:
        npart = b.shape[2]
        n = b.shape[0] * npart
        tn = _tile(npart, tiles[1], LANES)
        nj = npart // tn
        b_spec = pl.BlockSpec((None, tk, tn), lambda i, j, kk: (j // nj, kk, j % nj))
    else:
        n = b.shape[1]
        tn = _tile(n, tiles[1], LANES)
        b_spec = pl.BlockSpec((tk, tn), lambda i, j, kk: (kk, j))
    return _matmul(
        name, "tn", (m // tm, n // tn, s // tk), a, pl.BlockSpec((tk, tm), lambda i, j, kk: (kk, i)),
        [b], [b_spec], [], [],
        [jax.ShapeDtypeStruct((m, n), out_dtype)], [pl.BlockSpec((tm, tn), lambda i, j, kk: (i, j))], (tm, tn),
        lambda accs, ex: accs, a_fn=a_fn)[0]


def _row_tile(s):
    return _tile(s, 256, 16)


def _fold_rows(v):
    return jnp.sum(v.reshape(v.shape[0] // SUBLANES, SUBLANES, v.shape[1]), axis=0)


def _accumulate(i, acc_ref, val):
    @pl.when(i == 0)
    def _():
        acc_ref[...] = val

    @pl.when(i > 0)
    def _():
        acc_ref[...] += val


def _norm_mod(name, x, g, shift, scale):
    s, d = x.shape
    tb = _row_tile(s)

    def body(x_ref, g_ref, sh_ref, sc_ref, h_ref):
        xv = x_ref[...]
        rstd = lax.rsqrt(jnp.mean(xv * xv, axis=-1, keepdims=True) + EPS)
        h_ref[...] = ((xv * rstd) * g_ref[...] * (1.0 + sc_ref[...]) + sh_ref[...]).astype(BF16)

    row = pl.BlockSpec((tb, d), lambda i: (i, 0))
    vec = pl.BlockSpec((1, d), lambda i: (0, 0))
    return pl.pallas_call(body, name=name, grid=(s // tb,), in_specs=[row, vec, vec, vec], out_specs=row,
                          out_shape=jax.ShapeDtypeStruct((s, d), BF16), compiler_params=_params(1))(x, g, shift, scale)


def _rms_bwd(xv, dn_g, d):
    rstd = lax.rsqrt(jnp.mean(xv * xv, axis=-1, keepdims=True) + EPS)
    xhat = xv * rstd
    dx = rstd * (dn_g - xhat * (jnp.sum(dn_g * xhat, axis=-1, keepdims=True) * (1.0 / d)))
    return xhat, dx


def _norm_mod_bwd(name, dh, x, g, scale, dx_in):
    s, d = x.shape
    tb = _row_tile(s)
    nsteps = s // tb

    def body(dh_ref, x_ref, g_ref, sc_ref, dxin_ref, dx_ref, dsh_ref, dsc_ref, dg_ref, a_sh, a_sc, a_g):
        i = pl.program_id(0)
        dhv, gv = dh_ref[...], g_ref[...]
        dn = dhv * (1.0 + sc_ref[...])
        xhat, dx = _rms_bwd(x_ref[...], dn * gv, d)
        dx_ref[...] = dxin_ref[...] + dx
        _accumulate(i, a_sh, _fold_rows(dhv))
        _accumulate(i, a_sc, _fold_rows(dhv * (xhat * gv)))
        _accumulate(i, a_g, _fold_rows(dn * xhat))

        @pl.when(i == nsteps - 1)
        def _():
            dsh_ref[...] = jnp.sum(a_sh[...], axis=0, keepdims=True)
            dsc_ref[...] = jnp.sum(a_sc[...], axis=0, keepdims=True)
            dg_ref[...] = jnp.sum(a_g[...], axis=0, keepdims=True)

    row = pl.BlockSpec((tb, d), lambda i: (i, 0))
    vec = pl.BlockSpec((1, d), lambda i: (0, 0))
    vshape = jax.ShapeDtypeStruct((1, d), F32)
    return pl.pallas_call(
        body, name=name, grid=(nsteps,), in_specs=[row, row, vec, vec, row], out_specs=[row, vec, vec, vec],
        out_shape=[jax.ShapeDtypeStruct((s, d), F32), vshape, vshape, vshape],
        scratch_shapes=[pltpu.VMEM((SUBLANES, d), F32)] * 3, compiler_params=_params(1))(dh, x, g, scale, dx_in)


def _gate_bwd(name, dx, y, gvec, coef):
    s, d = dx.shape
    tb = _row_tile(s)
    nsteps = s // tb

    def body(dx_ref, y_ref, g_ref, dy_ref, dg_ref, acc):
        i = pl.program_id(0)
        dxv = dx_ref[...]
        dy_ref[...] = (dxv * g_ref[...]).astype(BF16)
        _accumulate(i, acc, _fold_rows(dxv * y_ref[...]))

        @pl.when(i == nsteps - 1)
        def _():
            dg_ref[...] = coef * jnp.sum(acc[...], axis=0, keepdims=True)

    row = pl.BlockSpec((tb, d), lambda i: (i, 0))
    vec = pl.BlockSpec((1, d), lambda i: (0, 0))
    return pl.pallas_call(
        body, name=name, grid=(nsteps,), in_specs=[row, row, vec], out_specs=[row, vec],
        out_shape=[jax.ShapeDtypeStruct((s, d), BF16), jax.ShapeDtypeStruct((1, d), F32)],
        scratch_shapes=[pltpu.VMEM((SUBLANES, d), F32)], compiler_params=_params(1))(dx, y, gvec)


def _final_loss(name, x, target, g):
    s, d = x.shape
    tb = _row_tile(s)
    nsteps = s // tb

    def body(x_ref, t_ref, g_ref, dx_ref, loss_ref, dg_ref, a_loss, a_g):
        i = pl.program_id(0)
        xv, gv = x_ref[...], g_ref[...]
        rstd = lax.rsqrt(jnp.mean(xv * xv, axis=-1, keepdims=True) + EPS)
        xhat = xv * rstd
        err = xhat * gv - t_ref[...]
        dyv = err * (1.0 / d)
        dn_g = dyv * gv
        dx_ref[...] = rstd * (dn_g - xhat * (jnp.sum(dn_g * xhat, axis=-1, keepdims=True) * (1.0 / d)))
        _accumulate(i, a_loss, _fold_rows(err * err))
        _accumulate(i, a_g, _fold_rows(dyv * xhat))

        @pl.when(i == nsteps - 1)
        def _():
            total = jnp.sum(jnp.sum(a_loss[...], axis=0, keepdims=True), axis=1, keepdims=True) * (0.5 / d)
            loss_ref[...] = jnp.broadcast_to(total, (1, LANES))
            dg_ref[...] = jnp.sum(a_g[...], axis=0, keepdims=True)

    row = pl.BlockSpec((tb, d), lambda i: (i, 0))
    vec = pl.BlockSpec((1, d), lambda i: (0, 0))
    return pl.pallas_call(
        body, name=name, grid=(nsteps,), in_specs=[row, row, vec],
        out_specs=[row, pl.BlockSpec((1, LANES), lambda i: (0, 0)), vec],
        out_shape=[jax.ShapeDtypeStruct((s, d), F32), jax.ShapeDtypeStruct((1, LANES), F32),
                   jax.ShapeDtypeStruct((1, d), F32)],
        scratch_shapes=[pltpu.VMEM((SUBLANES, d), F32)] * 2, compiler_params=_params(1))(x, target, g)


def _shift_rows(cur, halo, n):
    rolled = pltpu.roll(cur, n, 0)
    row = lax.broadcasted_iota(jnp.int32, cur.shape, 0)
    for r in range(n):
        rolled = jnp.where(row == r, halo[SUBLANES - n + r:SUBLANES - n + r + 1, :], rolled)
    return rolled


def _shift_rows_up(cur, halo, n):
    tb = cur.shape[0]
    rolled = pltpu.roll(cur, tb - n, 0)
    row = lax.broadcasted_iota(jnp.int32, cur.shape, 0)
    for r in range(n):
        rolled = jnp.where(row == tb - n + r, halo[r:r + 1, :], rolled)
    return rolled


def _conv_specs(s, d, tb, tn):
    nd = d // tn
    hb = tb // SUBLANES
    cur = lambda part: pl.BlockSpec((tb, tn), lambda i, j: (i, j + part * nd))
    prev = lambda part: pl.BlockSpec((SUBLANES, tn), lambda i, j: (jnp.maximum(i * hb - 1, 0), j + part * nd))
    nxt = lambda part: pl.BlockSpec((SUBLANES, tn), lambda i, j: (jnp.minimum((i + 1) * hb, s // SUBLANES - 1),
                                                                  j + part * nd))
    return cur, prev, nxt


def _conv_fwd(name, p, conv_w, conv_b):
    s, d = p.shape[0], p.shape[1] // 3
    tb, tn = _row_tile(s), _tile(d, 512, LANES)
    cur, prev, _ = _conv_specs(s, d, tb, tn)

    def body(bg_ref, cg_ref, xv_ref, cgp_ref, xvp_ref, w_ref, b_ref, z_ref):
        i = pl.program_id(0)
        u = cg_ref[...] * xv_ref[...]
        up = jnp.where(i > 0, cgp_ref[...] * xvp_ref[...], 0.0)
        w = w_ref[...]
        uc = w[0:1, :] * _shift_rows(u, up, 2) + w[1:2, :] * _shift_rows(u, up, 1) + w[2:3, :] * u + b_ref[...]
        z_ref[...] = (bg_ref[...] * uc).astype(BF16)

    return pl.pallas_call(
        body, name=name, grid=(s // tb, d // tn),
        in_specs=[cur(0), cur(1), cur(2), prev(1), prev(2), pl.BlockSpec((3, tn), lambda i, j: (0, j)),
                  pl.BlockSpec((1, tn), lambda i, j: (0, j))],
        out_specs=pl.BlockSpec((tb, tn), lambda i, j: (i, j)),
        out_shape=jax.ShapeDtypeStruct((s, d), BF16), compiler_params=_params(2))(p, p, p, p, p, conv_w, conv_b)


def _conv_bwd(name, dz, p, conv_w, conv_b):
    s, d = dz.shape
    tb, tn = _row_tile(s), _tile(d, 512, LANES)
    nsteps = s // tb
    cur, prev, nxt = _conv_specs(s, d, tb, tn)

    def body(dz_ref, bg_ref, cg_ref, xv_ref, cgp_ref, xvp_ref, dzn_ref, bgn_ref, w_ref, b_ref,
             dp_ref, dw_ref, db_ref, a_w0, a_w1, a_w2, a_b):
        j, i = pl.program_id(0), pl.program_id(1)
        cg, xv, bg, dzv = cg_ref[...], xv_ref[...], bg_ref[...], dz_ref[...]
        u = cg * xv
        up = jnp.where(i > 0, cgp_ref[...] * xvp_ref[...], 0.0)
        w = w_ref[...]
        u1, u2 = _shift_rows(u, up, 1), _shift_rows(u, up, 2)
        uc = w[0:1, :] * u2 + w[1:2, :] * u1 + w[2:3, :] * u + b_ref[...]
        duc = dzv * bg
        ducn = jnp.where(i < nsteps - 1, dzn_ref[...] * bgn_ref[...], 0.0)
        du = w[2:3, :] * duc + w[1:2, :] * _shift_rows_up(duc, ducn, 1) + w[0:1, :] * _shift_rows_up(duc, ducn, 2)
        dp_ref[0] = (dzv * uc).astype(BF16)
        dp_ref[1] = (du * xv).astype(BF16)
        dp_ref[2] = (du * cg).astype(BF16)
        _accumulate(i, a_w0, _fold_rows(duc * u2))
        _accumulate(i, a_w1, _fold_rows(duc * u1))
        _accumulate(i, a_w2, _fold_rows(duc * u))
        _accumulate(i, a_b, _fold_rows(duc))

        @pl.when(i == nsteps - 1)
        def _():
            dw_ref[0:1, :] = jnp.sum(a_w0[...], axis=0, keepdims=True)
            dw_ref[1:2, :] = jnp.sum(a_w1[...], axis=0, keepdims=True)
            dw_ref[2:3, :] = jnp.sum(a_w2[...], axis=0, keepdims=True)
            db_ref[...] = jnp.sum(a_b[...], axis=0, keepdims=True)

    swap = lambda spec: pl.BlockSpec(spec.block_shape, lambda j, i, _m=spec.index_map: _m(i, j))
    return pl.pallas_call(
        body, name=name, grid=(d // tn, nsteps),
        in_specs=[pl.BlockSpec((tb, tn), lambda j, i: (i, j)), swap(cur(0)), swap(cur(1)), swap(cur(2)),
                  swap(prev(1)), swap(prev(2)), swap(nxt(0)), swap(nxt(0)),
                  pl.BlockSpec((3, tn), lambda j, i: (0, j)), pl.BlockSpec((1, tn), lambda j, i: (0, j))],
        out_specs=[pl.BlockSpec((3, tb, tn), lambda j, i: (0, i, j)), pl.BlockSpec((3, tn), lambda j, i: (0, j)),
                   pl.BlockSpec((1, tn), lambda j, i: (0, j))],
        out_shape=[jax.ShapeDtypeStruct((3, s, d), BF16), jax.ShapeDtypeStruct((3, d), F32),
                   jax.ShapeDtypeStruct((1, d), F32)],
        scratch_shapes=[pltpu.VMEM((SUBLANES, tn), F32)] * 4, compiler_params=_params(2),
    )(dz, p, p, p, p, p, dz, p, conv_w, conv_b)


def _block_cumsum(v):
    tb = v.shape[0]
    row = lax.broadcasted_iota(jnp.int32, v.shape, 0)
    sh = 1
    while sh < tb:
        v = v + jnp.where(row >= sh, pltpu.roll(v, sh, 0), 0.0)
        sh *= 2
    return v


def _fgate_fwd(name, zf, b_f):
    s = zf.shape[0]
    tb = _tile(s, 512, SUBLANES)

    def body(z_ref, b_ref, f_ref, carry):
        i = pl.program_id(0)
        z = z_ref[...] + b_ref[...]
        ls = jnp.minimum(z, 0.0) - jnp.log(1.0 + jnp.exp(-jnp.abs(z)))
        run = _block_cumsum(ls)

        @pl.when(i == 0)
        def _():
            carry[...] = jnp.zeros_like(carry)

        out = run + carry[...]
        f_ref[...] = out
        carry[...] = out[tb - 1:tb, :]

    blk = pl.BlockSpec((tb, LANES), lambda i: (i, 0))
    return pl.pallas_call(
        body, name=name, grid=(s // tb,), in_specs=[blk, pl.BlockSpec((1, LANES), lambda i: (0, 0))], out_specs=blk,
        out_shape=jax.ShapeDtypeStruct((s, LANES), F32), scratch_shapes=[pltpu.VMEM((1, LANES), F32)],
        compiler_params=_params(1))(zf, b_f)


def _fgate_bwd(name, dfcum, zf, b_f):
    s = zf.shape[0]
    tb = _tile(s, 512, SUBLANES)
    nsteps = s // tb

    def body(df_ref, z_ref, b_ref, dz_ref, db_ref, carry, acc):
        i = pl.program_id(0)
        df = df_ref[...]
        incl = _block_cumsum(df)
        total = incl[tb - 1:tb, :]

        @pl.when(i == 0)
        def _():
            carry[...] = jnp.zeros_like(carry)

        suffix = total - incl + df + carry[...]
        carry[...] += total
        dzv = suffix * jax.nn.sigmoid(-(z_ref[...] + b_ref[...]))
        dz_ref[...] = dzv
        _accumulate(i, acc, _fold_rows(dzv))

        @pl.when(i == nsteps - 1)
        def _():
            db_ref[...] = jnp.sum(acc[...], axis=0, keepdims=True)

    blk = pl.BlockSpec((tb, LANES), lambda i: (nsteps - 1 - i, 0))
    vec = pl.BlockSpec((1, LANES), lambda i: (0, 0))
    return pl.pallas_call(
        body, name=name, grid=(nsteps,), in_specs=[blk, blk, vec], out_specs=[blk, vec],
        out_shape=[jax.ShapeDtypeStruct((s, LANES), F32), jax.ShapeDtypeStruct((1, LANES), F32)],
        scratch_shapes=[pltpu.VMEM((1, LANES), F32), pltpu.VMEM((SUBLANES, LANES), F32)],
        compiler_params=_params(1))(dfcum, zf, b_f)


def _lane_column(block, lane):
    sel = lax.broadcasted_iota(jnp.int32, block.shape, 1) == lane
    return jnp.sum(jnp.where(sel, block, 0.0), axis=1, keepdims=True)


def _attn_tiles(s):
    t = _tile(s, 512, LANES)
    return t, t


def _attn_fwd(name, q, kv, fcum, fcum_t, n_heads):
    s, d = q.shape
    dh = d // n_heads
    tq, tk = _attn_tiles(s)
    scale = 1.0 / math.sqrt(dh)

    def body(q_ref, k_ref, v_ref, f_ref, ft_ref, o_ref, l_ref):
        h, qi = pl.program_id(0), pl.program_id(1)
        qv = q_ref[...]
        fq = _lane_column(f_ref[...], h)

        def step(kj, carry, diagonal):
            m, l, acc = carry
            start = pl.multiple_of(kj * tk, tk)
            kb = k_ref[pl.ds(start, tk), :]
            vb = v_ref[pl.ds(start, tk), :]
            sc = lax.dot_general(qv, kb, _DN["nt"], preferred_element_type=F32) * scale
            sc = sc + fq - ft_ref[:, pl.ds(start, tk)]
            if diagonal:
                keep = lax.broadcasted_iota(jnp.int32, sc.shape, 0) >= lax.broadcasted_iota(jnp.int32, sc.shape, 1)
                sc = jnp.where(keep, sc, -jnp.inf)
            m_new = jnp.maximum(m, jnp.max(sc, axis=1, keepdims=True))
            alpha = jnp.exp(m - m_new)
            pr = jnp.exp(sc - m_new)
            l = alpha * l + jnp.sum(pr, axis=1, keepdims=True)
            acc = alpha * acc + lax.dot_general(pr.astype(BF16), vb, _DN["nn"], preferred_element_type=F32)
            return m_new, l, acc

        init = (jnp.full((tq, 1), -jnp.inf, F32), jnp.zeros((tq, 1), F32), jnp.zeros((tq, dh), F32))
        carry = lax.fori_loop(0, qi, lambda kj, cr: step(kj, cr, False), init)
        m, l, acc = step(qi, carry, True)
        o_ref[...] = (acc / l).astype(BF16)
        l_ref[...] = jnp.broadcast_to(m + jnp.log(l), (tq, LANES))

    nh = n_heads
    return pl.pallas_call(
        body, name=name, grid=(nh, s // tq),
        in_specs=[pl.BlockSpec((tq, dh), lambda h, i: (i, h)),
                  pl.BlockSpec((s, dh), lambda h, i: (0, h)),
                  pl.BlockSpec((s, dh), lambda h, i: (0, nh + h)),
                  pl.BlockSpec((tq, LANES), lambda h, i: (i, 0)),
                  pl.BlockSpec((None, 1, s), lambda h, i: (h, 0, 0))],
        out_specs=[pl.BlockSpec((tq, dh), lambda h, i: (i, h)),
                   pl.BlockSpec((None, tq, LANES), lambda h, i: (h, i, 0))],
        out_shape=[jax.ShapeDtypeStruct((s, d), BF16), jax.ShapeDtypeStruct((nh, s, LANES), F32)],
        compiler_params=_params(2))(q, kv, kv, fcum, fcum_t)


def _attn_delta(name, d_o, o, n_heads):
    s, d = d_o.shape
    dh = d // n_heads
    tb = _row_tile(s)

    def body(do_ref, o_ref, dl_ref, dob_ref):
        dov = do_ref[...]
        prod = dov * o_ref[...].astype(F32)
        lane = lax.broadcasted_iota(jnp.int32, (tb, LANES), 1)
        out = jnp.zeros((tb, LANES), F32)
        for h in range(n_heads):
            out = jnp.where(lane == h, jnp.sum(prod[:, h * dh:(h + 1) * dh], axis=1, keepdims=True), out)
        dl_ref[...] = out
        dob_ref[...] = dov.astype(BF16)

    row = pl.BlockSpec((tb, d), lambda i: (i, 0))
    return pl.pallas_call(
        body, name=name, grid=(s // tb,), in_specs=[row, row],
        out_specs=[pl.BlockSpec((tb, LANES), lambda i: (i, 0)), row],
        out_shape=[jax.ShapeDtypeStruct((s, LANES), F32), jax.ShapeDtypeStruct((s, d), BF16)],
        compiler_params=_params(1))(d_o, o)


def _attn_dq(name, q, kv, d_o, fcum, fcum_t, lse, delta, n_heads):
    s, d = q.shape
    dh = d // n_heads
    tq, tk = _attn_tiles(s)
    scale = 1.0 / math.sqrt(dh)

    def body(q_ref, k_ref, v_ref, do_ref, f_ref, ft_ref, l_ref, dl_ref, dq_ref, df_ref):
        h, qi = pl.program_id(0), pl.program_id(1)
        qv, dov = q_ref[...], do_ref[...]
        fq = _lane_column(f_ref[...], h)
        lse_col = l_ref[:, 0:1]
        dl_col = _lane_column(dl_ref[...], h)

        def step(kj, carry, diagonal):
            acc, dfq = carry
            start = pl.multiple_of(kj * tk, tk)
            kb = k_ref[pl.ds(start, tk), :]
            vb = v_ref[pl.ds(start, tk), :]
            sc = lax.dot_general(qv, kb, _DN["nt"], preferred_element_type=F32) * scale
            sc = sc + fq - ft_ref[:, pl.ds(start, tk)]
            pr = jnp.exp(sc - lse_col)
            if diagonal:
                keep = lax.broadcasted_iota(jnp.int32, sc.shape, 0) >= lax.broadcasted_iota(jnp.int32, sc.shape, 1)
                pr = jnp.where(keep, pr, 0.0)
            dp = lax.dot_general(dov, vb, _DN["nt"], preferred_element_type=F32)
            ds = pr * (dp - dl_col)
            acc = acc + lax.dot_general(ds.astype(BF16), kb, _DN["nn"], preferred_element_type=F32)
            return acc, dfq + jnp.sum(ds, axis=1, keepdims=True)

        init = (jnp.zeros((tq, dh), F32), jnp.zeros((tq, 1), F32))
        carry = lax.fori_loop(0, qi, lambda kj, cr: step(kj, cr, False), init)
        acc, dfq = step(qi, carry, True)
        dq_ref[...] = (acc * scale).astype(BF16)
        df_ref[...] = jnp.broadcast_to(dfq, (tq, LANES))

    nh = n_heads
    qspec = pl.BlockSpec((tq, dh), lambda h, i: (i, h))
    lanes = pl.BlockSpec((tq, LANES), lambda h, i: (i, 0))
    return pl.pallas_call(
        body, name=name, grid=(nh, s // tq),
        in_specs=[qspec, pl.BlockSpec((s, dh), lambda h, i: (0, h)), pl.BlockSpec((s, dh), lambda h, i: (0, nh + h)),
                  qspec, lanes, pl.BlockSpec((None, 1, s), lambda h, i: (h, 0, 0)),
                  pl.BlockSpec((None, tq, LANES), lambda h, i: (h, i, 0)), lanes],
        out_specs=[qspec, pl.BlockSpec((None, tq, LANES), lambda h, i: (h, i, 0))],
        out_shape=[jax.ShapeDtypeStruct((s, d), BF16), jax.ShapeDtypeStruct((nh, s, LANES), F32)],
        compiler_params=_params(2))(q, kv, kv, d_o, fcum, fcum_t, lse, delta)


def _attn_dkv(name, q, kv, d_o, fcum, fcum_t, lse_t, delta_t, n_heads):
    s, d = q.shape
    dh = d // n_heads
    tq, tk = _attn_tiles(s)
    nq = s // tq
    scale = 1.0 / math.sqrt(dh)

    def body(q_ref, k_ref, v_ref, do_ref, f_ref, ft_ref, lt_ref, dlt_ref, dkv_ref, df_ref):
        h, kj = pl.program_id(0), pl.program_id(1)
        kb, vb = k_ref[...], v_ref[...]
        fk = _lane_column(f_ref[...], h)

        def step(qi, carry, diagonal):
            dk, dv, dfk = carry
            start = pl.multiple_of(qi * tq, tq)
            qb = q_ref[pl.ds(start, tq), :]
            dob = do_ref[pl.ds(start, tq), :]
            sc = lax.dot_general(kb, qb, _DN["nt"], preferred_element_type=F32) * scale
            sc = sc + ft_ref[:, pl.ds(start, tq)] - fk
            pr = jnp.exp(sc - lt_ref[:, pl.ds(start, tq)])
            if diagonal:
                keep = lax.broadcasted_iota(jnp.int32, sc.shape, 1) >= lax.broadcasted_iota(jnp.int32, sc.shape, 0)
                pr = jnp.where(keep, pr, 0.0)
            dv = dv + lax.dot_general(pr.astype(BF16), dob, _DN["nn"], preferred_element_type=F32)
            dp = lax.dot_general(vb, dob, _DN["nt"], preferred_element_type=F32)
            ds = pr * (dp - dlt_ref[:, pl.ds(start, tq)])
            dk = dk + lax.dot_general(ds.astype(BF16), qb, _DN["nn"], preferred_element_type=F32)
            dfk = dfk - jnp.sum(ds, axis=1, keepdims=True)
            return dk, dv, dfk

        init = (jnp.zeros((tk, dh), F32), jnp.zeros((tk, dh), F32), jnp.zeros((tk, 1), F32))
        carry = step(kj, init, True)
        dk, dv, dfk = lax.fori_loop(kj + 1, nq, lambda qi, cr: step(qi, cr, False), carry)
        dkv_ref[0] = (dk * scale).astype(BF16)
        dkv_ref[1] = dv.astype(BF16)
        df_ref[...] = jnp.broadcast_to(dfk, (tk, LANES))

    nh = n_heads
    full = lambda col0: pl.BlockSpec((s, dh), lambda h, j: (0, col0 + h))
    row_t = pl.BlockSpec((None, 1, s), lambda h, j: (h, 0, 0))
    return pl.pallas_call(
        body, name=name, grid=(nh, s // tk),
        in_specs=[full(0), pl.BlockSpec((tk, dh), lambda h, j: (j, h)), pl.BlockSpec((tk, dh), lambda h, j: (j, nh + h)),
                  full(0), pl.BlockSpec((tk, LANES), lambda h, j: (j, 0)), row_t, row_t, row_t],
        out_specs=[pl.BlockSpec((2, tk, dh), lambda h, j: (0, j, h)),
                   pl.BlockSpec((None, tk, LANES), lambda h, j: (h, j, 0))],
        out_shape=[jax.ShapeDtypeStruct((2, s, d), BF16), jax.ShapeDtypeStruct((nh, s, LANES), F32)],
        compiler_params=_params(2))(q, kv, kv, d_o, fcum, fcum_t, lse_t, delta_t)


def _sum_slots(name, g):
    n, r, lanes = g.shape

    def body(g_ref, o_ref):
        total = g_ref[0]
        for i in range(1, n):
            total = total + g_ref[i]
        o_ref[...] = total

    return pl.pallas_call(
        body, name=name, out_shape=jax.ShapeDtypeStruct((r, lanes), F32),
        in_specs=[pl.BlockSpec(memory_space=pltpu.VMEM)], out_specs=pl.BlockSpec(memory_space=pltpu.VMEM),
        compiler_params=pltpu.CompilerParams(vmem_limit_bytes=VMEM_LIMIT))(g)


def _adamw(name, parts, w, m, v):
    npart, r, c = parts.shape
    tb = _tile(r, max(SUBLANES, (2**18 // c) // SUBLANES * SUBLANES), SUBLANES)
    c1 = 1.0 - ADAM_B1 ** ADAM_STEP
    c2 = 1.0 - ADAM_B2 ** ADAM_STEP

    def body(p_ref, w_ref, m_ref, v_ref, g_out, d_out, m_out, v_out):
        g = p_ref[0].astype(F32)
        for i in range(1, npart):
            g = g + p_ref[i].astype(F32)
        m_new = ADAM_B1 * m_ref[...] + (1.0 - ADAM_B1) * g
        v_new = ADAM_B2 * v_ref[...] + (1.0 - ADAM_B2) * (g * g)
        m_hat = m_new / c1
        v_hat = v_new / c2
        g_out[...] = g
        d_out[...] = -ADAM_LR * (m_hat / (jnp.sqrt(v_hat) + ADAM_EPS) + ADAM_WD * w_ref[...])
        m_out[...] = m_new
        v_out[...] = v_new

    blk = pl.BlockSpec((tb, c), lambda i: (i, 0))
    shape = jax.ShapeDtypeStruct((r, c), F32)
    return pl.pallas_call(
        body, name=name, grid=(r // tb,), in_specs=[pl.BlockSpec((npart, tb, c), lambda i: (0, i, 0)), blk, blk, blk],
        out_specs=[blk] * 4, out_shape=[shape] * 4, compiler_params=_params(1))(parts, w, m, v)


def _pack(vectors):
    flat = jnp.concatenate([v.reshape(-1).astype(F32) for v in vectors])
    pad = (-flat.shape[0]) % (SUBLANES * LANES)
    return jnp.pad(flat, (0, pad)).reshape(-1, LANES)


def _unpack(flat, shapes):
    out, pos = [], 0
    for shp in shapes:
        size = math.prod(shp)
        out.append(flat[..., pos:pos + size].reshape(flat.shape[:-1] + tuple(shp)))
        pos += size
    return out


def _unshard_last(g, lead):
    nd = len(lead)
    return jnp.moveaxis(g, 0, nd).reshape(tuple(lead) + (-1,))


def _by_core_and_chip(p):
    return jnp.swapaxes(p.reshape((4, 2) + p.shape[1:]), 0, 1)


def _my_slice(full, me, width, axis):
    return lax.dynamic_slice_in_dim(full, me * width, width, axis)


def kernel(x, c, norm_g, w_ada, b_ada, w_ffn_in, w_ffn_out, w_conv_in, conv_w, conv_b, w_conv_out, kv_norm_g, w_ada_kv, b_ada_kv, w_kvf, b_fgate, w_q, w_o, final_g, loss_target, m_norm_g, m_w_ada, m_b_ada, m_w_ffn_in, m_w_ffn_out, m_w_conv_in, m_conv_w, m_conv_b, m_w_conv_out, m_kv_norm_g, m_w_ada_kv, m_b_ada_kv, m_w_kvf, m_b_fgate, m_w_q, m_w_o, m_final_g, v_norm_g, v_w_ada, v_b_ada, v_w_ffn_in, v_w_ffn_out, v_w_conv_in, v_conv_w, v_conv_b, v_w_conv_out, v_kv_norm_g, v_w_ada_kv, v_b_ada_kv, v_w_kvf, v_b_fgate, v_w_q, v_w_o, v_final_g):
    s, d = x.shape[1], x.shape[2]
    n_heads = b_fgate.shape[0]
    n_layers = w_ada.shape[0]
    dsh = d // N_DEV
    f = w_ffn_out.shape[2] * N_DEV
    ada_w = w_ada.shape[2]
    kv_w = w_ada_kv.shape[1]
    kvf_w = w_kvf.shape[1]
    assert n_layers == 2 and w_conv_in.shape[0] == 1 and w_q.shape[0] == 1
    assert (d // n_heads) % LANES == 0 and n_heads <= LANES and N_DEV * kvf_w == 2 * d + n_heads
    me = 4 * lax.axis_index("x") + 2 * lax.axis_index("y") + lax.axis_index("c")
    x0, target = x[0], loss_target[0]

    small = _allgather_small("gather_small_params", _pack([c, norm_g, conv_w, conv_b]))
    c_all, ng_sh, cw_sh, cb_sh = _unpack(small.reshape(N_DEV, -1), [(d,), (n_layers, 3, dsh), (3, dsh), (dsh,)])
    norm_g_full = _unshard_last(ng_sh, (n_layers, 3))
    conv_w_full = _unshard_last(cw_sh, (3,))
    conv_b_full = _unshard_last(cb_sh, ()).reshape(1, d)
    c_rows = jnp.pad(c_all, ((0, 16 - N_DEV), (0, 0)))

    ada_cols = [
        _mm_bias_rows(f"ada_rows_{l}", c_rows, w_ada[l], _my_slice(b_ada[l], me, ada_w, 0).reshape(1, ada_w), _silu)
        for l in range(n_layers)]
    ada_cols.append(_mm_bias_rows("ada_rows_kv", c_rows, w_ada_kv, _my_slice(b_ada_kv, me, kv_w, 0).reshape(1, kv_w),
                                  _silu))
    ada_part = jnp.concatenate([a[:N_DEV] for a in ada_cols], axis=1)
    ada_all = _allgather_small("gather_ada_rows", _pack([ada_part]))
    ada_all = ada_all.reshape(N_DEV, -1)[:, :ada_part.size].reshape(N_DEV, N_DEV, -1)
    ada_mine = lax.dynamic_index_in_dim(ada_all, me, axis=1, keepdims=False)
    ada = [ada_mine[:, l * ada_w:(l + 1) * ada_w].reshape(3, 3, 1, d) for l in range(n_layers)]
    ada_kv = ada_mine[:, n_layers * ada_w:].reshape(2, 1, d)

    shards = [w_ffn_in, w_ffn_out, w_conv_in[0], w_conv_out[0], w_kvf, w_q[0], w_o[0]]
    g_in, g_out, g_ci, g_co, g_kvf, g_q, g_o = _allgather_hbm("gather_weights", [w.astype(BF16) for w in shards])
    wf_in = _unshard_last(g_in, (n_layers, 2, d))
    wf_out = jnp.moveaxis(g_out, 0, 2).reshape(n_layers, 2, f, d)
    wc_in = _unshard_last(g_ci, (d,))
    wc_out, wq, wo = g_co.reshape(d, d), g_q.reshape(d, d), g_o.reshape(d, d)
    wkvf = _unshard_last(g_kvf, (d,))
    wkv = wkvf[:, :2 * d]
    wf_gate = jnp.pad(wkvf[:, 2 * d:], ((0, 0), (0, LANES - n_heads)))
    b_f = jnp.pad(b_fgate, (0, LANES - n_heads)).reshape(1, LANES)

    def ffn_fwd(tag, xs, l, sub, which):
        shift, scale, gate = ada[l][sub]
        tiles = FFN_TILES[tag]
        h = _norm_mod(f"norm_{tag}", xs, norm_g_full[l, sub].reshape(1, d), shift, scale)
        ab, u = _mm_swiglu(f"ffn_in_{tag}", h, wf_in[l, which], f, tiles["in"])
        y, x_next = _mm_residual(f"ffn_out_{tag}", u, wf_out[l, which], xs, 0.5 * gate, tiles["out"])
        return x_next, (xs, h, ab, u, y)

    x1, save_f0 = ffn_fwd("l0a", x0, 0, 0, 0)
    shift, scale, gate = ada[0][1]
    h_c = _norm_mod("norm_conv", x1, norm_g_full[0, 1].reshape(1, d), shift, scale)
    p_c = _mm_nn("conv_in", h_c, wc_in, F32)
    z_c = _conv_fwd("conv_mix", p_c, conv_w_full, conv_b_full)
    y_c, x2 = _mm_residual("conv_out", z_c, wc_out, x1, gate)
    x3, save_f1 = ffn_fwd("l0b", x2, 0, 2, 1)
    x4, save_f2 = ffn_fwd("l1a", x3, 1, 0, 0)
    shift, scale, gate_a = ada[1][1]
    h_q = _norm_mod("norm_q", x4, norm_g_full[1, 1].reshape(1, d), shift, scale)
    h_kv = _norm_mod("norm_kv", x4, kv_norm_g.reshape(1, d), ada_kv[0], ada_kv[1])
    q = _mm_nn("attn_q", h_q, wq, BF16)
    kv = _mm_nn("attn_kv", h_kv, wkv, BF16)
    zf = _mm_nn("attn_fgate", h_kv, wf_gate, F32)
    fcum = _fgate_fwd("fgate_cumsum", zf, b_f)
    fcum_t = fcum[:, :n_heads].T.reshape(n_heads, 1, s)
    o, lse = _attn_fwd("attn_fwd", q, kv, fcum, fcum_t, n_heads)
    y_a, x5 = _mm_residual("attn_out", o, wo, x4, gate_a)
    x6, save_f3 = ffn_fwd("l1b", x5, 1, 2, 1)
    dx, loss_part, d_final_g = _final_loss("final_loss", x6, target, final_g.reshape(1, d))

    d_ada = [[[None] * 3 for _ in range(3)] for _ in range(n_layers)]
    d_norm_g = [[None] * 3 for _ in range(n_layers)]
    dw_in = [[None] * 2 for _ in range(n_layers)]
    dw_out = [[None] * 2 for _ in range(n_layers)]

    def ffn_bwd(tag, dxs, saved, l, sub, which):
        xs, h, ab, u, y = saved
        _, scale, gate = ada[l][sub]
        tiles = FFN_TILES[tag]
        dy, d_ada[l][sub][2] = _gate_bwd(f"gate_bwd_{tag}", dxs, y, 0.5 * gate, 0.5)
        dab = _mm_nt_swiglu_bwd(f"ffn_out_bwd_{tag}", dy, wf_out[l, which], ab, tiles["out_bwd"])
        dw_out[l][which] = _mm_tn(f"ffn_out_dw_{tag}", u, dy, BF16, tiles=tiles["out_dw"])
        dh = _mm_nt_parts(f"ffn_in_bwd_{tag}", dab, wf_in[l, which], F32, tiles["in_bwd"])
        dw_in[l][which] = _mm_tn(f"ffn_in_dw_{tag}", h, dab, BF16, tiles=tiles["in_dw"])
        dxs, d_ada[l][sub][0], d_ada[l][sub][1], d_norm_g[l][sub] = _norm_mod_bwd(
            f"norm_bwd_{tag}", dh, xs, norm_g_full[l, sub].reshape(1, d), scale, dxs)
        return dxs

    dx = ffn_bwd("l1b", dx, save_f3, 1, 2, 1)
    dy, d_ada[1][1][2] = _gate_bwd("gate_bwd_attn", dx, y_a, gate_a, 1.0)
    d_o = _mm_nt("attn_out_bwd", dy, wo, F32)
    dwo = _mm_tn("attn_out_dw", o, dy, BF16)
    delta, d_ob = _attn_delta("attn_delta", d_o, o, n_heads)
    delta_t = delta[:, :n_heads].T.reshape(n_heads, 1, s)
    lse_t = lse[:, :, 0].reshape(n_heads, 1, s)
    dq, dfq = _attn_dq("attn_dq", q, kv, d_ob, fcum, fcum_t, lse, delta, n_heads)
    dkv, dfk = _attn_dkv("attn_dkv", q, kv, d_ob, fcum, fcum_t, lse_t, delta_t, n_heads)
    dfcum = jnp.pad((dfq[:, :, 0] + dfk[:, :, 0]).T, ((0, 0), (0, LANES - n_heads)))
    dzf, d_bf = _fgate_bwd("fgate_bwd", dfcum, zf, b_f)
    dh_q = _mm_nt("attn_q_bwd", dq, wq, F32)
    dwq = _mm_tn("attn_q_dw", h_q, dq, BF16)
    dh_kv = _mm_nt_parts("attn_kv_bwd", dkv, wkv, F32)
    dh_kv = _mm_nt("attn_fgate_bwd", dzf, wf_gate, F32, extra_add=dh_kv)
    dwkv = _mm_tn("attn_kv_dw", h_kv, dkv, BF16)
    dwf = _mm_tn("attn_fgate_dw", h_kv, dzf, BF16)
    dx, d_ada[1][1][0], d_ada[1][1][1], d_norm_g[1][1] = _norm_mod_bwd(
        "norm_bwd_q", dh_q, x4, norm_g_full[1, 1].reshape(1, d), ada[1][1][1], dx)
    dx, d_kv_shift, d_kv_scale, d_kv_norm_g = _norm_mod_bwd(
        "norm_bwd_kv", dh_kv, x4, kv_norm_g.reshape(1, d), ada_kv[1], dx)
    dx = ffn_bwd("l1a", dx, save_f2, 1, 0, 0)
    dx = ffn_bwd("l0b", dx, save_f1, 0, 2, 1)
    dy, d_ada[0][1][2] = _gate_bwd("gate_bwd_conv", dx, y_c, ada[0][1][2], 1.0)
    dz = _mm_nt("conv_out_bwd", dy, wc_out, F32)
    dwc_out = _mm_tn("conv_out_dw", z_c, dy, BF16)
    dp, d_conv_w, d_conv_b = _conv_bwd("conv_mix_bwd", dz, p_c, conv_w_full, conv_b_full)
    dh_c = _mm_nt_parts("conv_in_bwd", dp, wc_in, F32)
    dwc_in = _mm_tn("conv_in_dw", h_c, dp, BF16)
    dx, d_ada[0][1][0], d_ada[0][1][1], d_norm_g[0][1] = _norm_mod_bwd(
        "norm_bwd_conv", dh_c, x1, norm_g_full[0, 1].reshape(1, d), ada[0][1][1], dx)
    dx = ffn_bwd("l0a", dx, save_f0, 0, 0, 0)
    grad_x = dx.reshape(1, s, d)

    d_ada_flat = jnp.concatenate([v.reshape(-1) for l in range(n_layers) for sub in range(3) for v in d_ada[l][sub]])
    d_norm_flat = jnp.concatenate([d_norm_g[l][sub].reshape(-1) for l in range(n_layers) for sub in range(3)])
    small_shapes = [(n_layers * 9 * d,), (2 * d,), (n_layers * 3 * d,), (3, d), (d,), (d,), (LANES,), (d,), (LANES,)]
    small_parts = _pack([d_ada_flat, d_kv_shift, d_kv_scale, d_norm_flat, d_conv_w, d_conv_b, d_kv_norm_g, d_bf,
                         d_final_g, loss_part])
    small_all = _allgather_small("gather_small_grads", small_parts)
    small_sum = _sum_slots("sum_small_grads", small_all).reshape(-1)
    g_b_ada, g_b_ada_kv, g_norm_full, g_conv_w_full, g_conv_b_full, g_kv_norm, g_bf, g_final, loss_v = _unpack(
        small_sum, small_shapes)
    loss = loss_v[0]
    d_ada_rows = small_all.reshape(N_DEV, -1)[:, :n_layers * 9 * d + 2 * d]
    d_ada_rows = jnp.pad(d_ada_rows, ((0, 16 - N_DEV), (0, 0)))

    results = {}

    def update(key, parts, w, m, v):
        shape = w.shape
        c_dim = shape[-1]
        outs = _adamw(f"adamw_{key}", parts.reshape(parts.shape[0], -1, c_dim), w.reshape(-1, c_dim),
                      m.reshape(-1, c_dim), v.reshape(-1, c_dim))
        results[key] = [o.reshape(shape) for o in outs]

    g_w_ada = []
    for l in range(n_layers):
        cols = _my_slice(d_ada_rows[:, l * 9 * d:(l + 1) * 9 * d], me, ada_w, 1)
        g_w_ada.append(_mm_tn(f"ada_dw_{l}", c_rows, cols, F32, a_fn=_silu))
    update("w_ada", jnp.stack(g_w_ada).reshape(1, n_layers * d, ada_w), w_ada, m_w_ada, v_w_ada)
    cols = _my_slice(d_ada_rows[:, n_layers * 9 * d:], me, kv_w, 1)
    update("w_ada_kv", _mm_tn("ada_dw_kv", c_rows, cols, F32, a_fn=_silu).reshape(1, d, kv_w),
           w_ada_kv, m_w_ada_kv, v_w_ada_kv)

    p_in = jnp.stack([jnp.stack(r) for r in dw_in])
    p_in = jnp.moveaxis(p_in.reshape(n_layers, 2, d, N_DEV, -1), 3, 0)
    p_out = jnp.stack([jnp.stack(r) for r in dw_out])
    p_out = jnp.moveaxis(p_out.reshape(n_layers, 2, N_DEV, -1, d), 2, 0)
    p_ci = jnp.moveaxis(dwc_in.reshape(d, N_DEV, -1), 1, 0)[:, None]
    p_co = dwc_out.reshape(N_DEV, 1, dsh, d)
    dwkvf = jnp.concatenate([dwkv, dwf[:, :n_heads]], axis=1)
    p_kvf = jnp.moveaxis(dwkvf.reshape(d, N_DEV, kvf_w), 1, 0)
    p_q, p_o = dwq.reshape(N_DEV, 1, dsh, d), dwo.reshape(N_DEV, 1, dsh, d)
    partials = [_by_core_and_chip(p) for p in [p_in, p_out, p_ci, p_co, p_kvf, p_q, p_o]]
    from_sibling = _exchange_sibling("reduce_within_chip", partials)
    core = lax.axis_index("c").astype(jnp.int32).reshape(1)
    chip_sums = []
    for idx, (mine, theirs) in enumerate(zip(partials, from_sibling)):
        c_dim = mine.shape[-1]
        total = _pair_add(f"add_sibling_{idx}", mine.reshape(2, 4, -1, c_dim), theirs.reshape(4, -1, c_dim), core)
        chip_sums.append(total.reshape(theirs.shape))
    r_in, r_out, r_ci, r_co, r_kvf, r_q, r_o = _exchange_chips("scatter_weight_grads", chip_sums)
    update("w_ffn_in", r_in, w_ffn_in, m_w_ffn_in, v_w_ffn_in)
    update("w_ffn_out", r_out, w_ffn_out, m_w_ffn_out, v_w_ffn_out)
    update("w_conv_in", r_ci, w_conv_in, m_w_conv_in, v_w_conv_in)
    update("w_conv_out", r_co, w_conv_out, m_w_conv_out, v_w_conv_out)
    update("w_kvf", r_kvf, w_kvf, m_w_kvf, v_w_kvf)
    update("w_q", r_q, w_q, m_w_q, v_w_q)
    update("w_o", r_o, w_o, m_w_o, v_w_o)

    small_keys = ["norm_g", "b_ada", "conv_w", "conv_b", "kv_norm_g", "b_ada_kv", "b_fgate", "final_g"]
    small_w = [norm_g, b_ada, conv_w, conv_b, kv_norm_g, b_ada_kv, b_fgate, final_g]
    small_m = [m_norm_g, m_b_ada, m_conv_w, m_conv_b, m_kv_norm_g, m_b_ada_kv, m_b_fgate, m_final_g]
    small_v = [v_norm_g, v_b_ada, v_conv_w, v_conv_b, v_kv_norm_g, v_b_ada_kv, v_b_fgate, v_final_g]
    small_g = [
        _my_slice(g_norm_full.reshape(n_layers, 3, d), me, dsh, 2), g_b_ada.reshape(b_ada.shape),
        _my_slice(g_conv_w_full, me, dsh, 1).reshape(conv_w.shape), _my_slice(g_conv_b_full, me, dsh, 0).reshape(
            conv_b.shape), g_kv_norm, g_b_ada_kv, g_bf[:n_heads], g_final]
    packed = _adamw("adamw_small", _pack(small_g)[None], _pack(small_w), _pack(small_m), _pack(small_v))
    for vals, idx in zip(packed, range(4)):
        for key, val in zip(small_keys, _unpack(vals.reshape(-1), [w.shape for w in small_w])):
            results.setdefault(key, [None] * 4)[idx] = val

    order = ["norm_g", "w_ada", "b_ada", "w_ffn_in", "w_ffn_out", "w_conv_in", "conv_w", "conv_b", "w_conv_out",
             "kv_norm_g", "w_ada_kv", "b_ada_kv", "w_kvf", "b_fgate", "w_q", "w_o", "final_g"]
    return (loss, grad_x, *[results[k][0] for k in order], *[results[k][1] for k in order],
            *[results[k][2] for k in order], *[results[k][3] for k in order])
```

```python
import functools
import math

import jax
import jax.numpy as jnp
from jax import lax
from jax.experimental import pallas as pl
from jax.experimental.pallas import tpu as pltpu

F32 = jnp.float32
BF16 = jnp.bfloat16
MESH = pl.DeviceIdType.MESH

N_DEV = 8
LANES = 128
SUBLANES = 8
VMEM_LIMIT = 56 * 2**20
EPS = 1e-6
ADAM_LR, ADAM_B1, ADAM_B2, ADAM_EPS, ADAM_WD, ADAM_STEP = 0.001, 0.9, 0.999, 1e-08, 0.01, 10

_DN = {"nn": (((1,), (0,)), ((), ())), "nt": (((1,), (1,)), ((), ())), "tn": (((0,), (0,)), ((), ()))}


def _tile(n, pref, align):
    t = min(pref, n)
    t -= t % align
    while t >= align:
        if n % t == 0:
            return t
        t -= align
    return n


def _params(n_grid):
    return pltpu.CompilerParams(vmem_limit_bytes=VMEM_LIMIT, dimension_semantics=("arbitrary",) * n_grid)


def _silu(v):
    return v * jax.nn.sigmoid(v)


def _position():
    x, y, c = lax.axis_index("x"), lax.axis_index("y"), lax.axis_index("c")
    return x, y, c


def _allgather_small(name, v):
    def body(v_ref, out_ref, send_sems, recv_sems, local_sem):
        x, y, c = _position()
        me, sibling = (x, y, c), (x, y, 1 - c)
        chips = [(1 - x, y), (x, 1 - y), (1 - x, 1 - y)]

        def slot(px, py, pc):
            return out_ref.at[4 * px + 2 * py + pc]

        def copy(k, block, to, src=None):
            return pltpu.make_async_remote_copy(
                src_ref=slot(*block) if src is None else src, dst_ref=slot(*block),
                send_sem=send_sems.at[k], recv_sem=recv_sems.at[k], device_id=to, device_id_type=MESH)

        mine = pltpu.make_async_copy(v_ref, slot(*me), local_sem)
        mine.start()
        first = [copy(0, me, sibling, src=v_ref)]
        first += [copy(1 + j, me, (*chip, c), src=v_ref) for j, chip in enumerate(chips)]
        for cp in first:
            cp.start()
        passed = [copy(4 + j, (*chip, c), sibling) for j, chip in enumerate(chips)]
        for j, chip in enumerate(chips):
            copy(1 + j, (*chip, c), me).wait_recv()
            passed[j].start()
        copy(0, sibling, me).wait_recv()
        for j, chip in enumerate(chips):
            copy(4 + j, (*chip, 1 - c), me).wait_recv()
        for cp in first + passed:
            cp.wait_send()
        mine.wait()

    return pl.pallas_call(
        body, name=name,
        out_shape=jax.ShapeDtypeStruct((N_DEV,) + v.shape, v.dtype),
        in_specs=[pl.BlockSpec(memory_space=pltpu.VMEM)],
        out_specs=pl.BlockSpec(memory_space=pltpu.VMEM),
        scratch_shapes=[pltpu.SemaphoreType.DMA((7,)), pltpu.SemaphoreType.DMA((7,)), pltpu.SemaphoreType.DMA],
        compiler_params=pltpu.CompilerParams(vmem_limit_bytes=VMEM_LIMIT),
    )(v)


def _allgather_hbm(name, shards):
    n = len(shards)

    def body(*refs):
        ins, outs = refs[:n], refs[n:2 * n]
        send_sems, recv_sems, local_sems = refs[2 * n:]
        x, y, c = _position()
        me, sibling = (x, y, c), (x, y, 1 - c)
        chips = [(1 - x, y), (x, 1 - y), (1 - x, 1 - y)]

        def copy(a, k, block, to, src=None):
            px, py, pc = block
            dst = outs[a].at[4 * px + 2 * py + pc]
            return pltpu.make_async_remote_copy(
                src_ref=dst if src is None else src, dst_ref=dst,
                send_sem=send_sems.at[7 * a + k], recv_sem=recv_sems.at[7 * a + k], device_id=to, device_id_type=MESH)

        local, first, passed = [], [], []
        for a in range(n):
            cp = pltpu.make_async_copy(ins[a], outs[a].at[4 * x + 2 * y + c], local_sems.at[a])
            cp.start()
            local.append(cp)
            mine = [copy(a, 1 + j, me, (*chip, c), src=ins[a]) for j, chip in enumerate(chips)]
            mine.append(copy(a, 0, me, sibling, src=ins[a]))
            for cp in mine:
                cp.start()
            first += mine
        for a in range(n):
            for j, chip in enumerate(chips):
                copy(a, 1 + j, (*chip, c), me).wait_recv()
                cp = copy(a, 4 + j, (*chip, c), sibling)
                cp.start()
                passed.append(cp)
        for a in range(n):
            copy(a, 0, sibling, me).wait_recv()
            for j, chip in enumerate(chips):
                copy(a, 4 + j, (*chip, 1 - c), me).wait_recv()
        for cp in first + passed:
            cp.wait_send()
        for cp in local:
            cp.wait()

    any_spec = pl.BlockSpec(memory_space=pl.ANY)
    return pl.pallas_call(
        body, name=name,
        out_shape=[jax.ShapeDtypeStruct((N_DEV,) + s.shape, s.dtype) for s in shards],
        in_specs=[any_spec] * n, out_specs=[any_spec] * n,
        scratch_shapes=[pltpu.SemaphoreType.DMA((7 * n,)), pltpu.SemaphoreType.DMA((7 * n,)),
                        pltpu.SemaphoreType.DMA((n,))],
    )(*shards)


def _exchange_sibling(name, parts):
    n = len(parts)

    def body(*refs):
        ins, outs = refs[:n], refs[n:2 * n]
        send_sems, recv_sems = refs[2 * n:]
        x, y, c = _position()
        copies = [pltpu.make_async_remote_copy(
            src_ref=ins[a].at[1 - c], dst_ref=outs[a], send_sem=send_sems.at[a], recv_sem=recv_sems.at[a],
            device_id=(x, y, 1 - c), device_id_type=MESH) for a in range(n)]
        for cp in copies:
            cp.start()
        for cp in copies:
            cp.wait()

    any_spec = pl.BlockSpec(memory_space=pl.ANY)
    return pl.pallas_call(
        body, name=name, out_shape=[jax.ShapeDtypeStruct(p.shape[1:], p.dtype) for p in parts],
        in_specs=[any_spec] * n, out_specs=[any_spec] * n,
        scratch_shapes=[pltpu.SemaphoreType.DMA((n,)), pltpu.SemaphoreType.DMA((n,))],
    )(*parts)


def _pair_add(name, mine, theirs, core):
    _, nchip, r, c = mine.shape
    tb = _tile(r, max(16, (2**19 // c) // 16 * 16), 16)

    def body(core_ref, a_ref, b_ref, o_ref):
        o_ref[...] = (a_ref[...].astype(F32) + b_ref[...].astype(F32)).astype(o_ref.dtype)

    return pl.pallas_call(
        body, name=name,
        grid_spec=pltpu.PrefetchScalarGridSpec(
            num_scalar_prefetch=1, grid=(nchip, r // tb),
            in_specs=[pl.BlockSpec((None, None, tb, c), lambda k, i, core_ref: (core_ref[0], k, i, 0)),
                      pl.BlockSpec((None, tb, c), lambda k, i, core_ref: (k, i, 0))],
            out_specs=pl.BlockSpec((None, tb, c), lambda k, i, core_ref: (k, i, 0))),
        out_shape=jax.ShapeDtypeStruct(theirs.shape, theirs.dtype), compiler_params=_params(2))(core, mine, theirs)


def _exchange_chips(name, parts):
    n = len(parts)

    def body(*refs):
        ins, outs = refs[:n], refs[n:2 * n]
        send_sems, recv_sems, local_sems = refs[2 * n:]
        x, y, c = _position()
        my_chip = 2 * x + y

        def peer(k):
            px = 1 - x if k & 2 else x
            py = 1 - y if k & 1 else y
            return (px, py, c), 2 * px + py

        def copy(a, k, landed):
            to, to_chip = peer(k)
            return pltpu.make_async_remote_copy(
                src_ref=ins[a].at[to_chip], dst_ref=outs[a].at[to_chip if landed else my_chip],
                send_sem=send_sems.at[3 * a + k - 1], recv_sem=recv_sems.at[3 * a + k - 1],
                device_id=to, device_id_type=MESH)

        local, sent = [], []
        for a in range(n):
            cp = pltpu.make_async_copy(ins[a].at[my_chip], outs[a].at[my_chip], local_sems.at[a])
            cp.start()
            local.append(cp)
            for k in range(1, 4):
                cp = copy(a, k, False)
                cp.start()
                sent.append(cp)
        for a in range(n):
            for k in range(1, 4):
                copy(a, k, True).wait_recv()
        for cp in sent:
            cp.wait_send()
        for cp in local:
            cp.wait()

    any_spec = pl.BlockSpec(memory_space=pl.ANY)
    return pl.pallas_call(
        body, name=name, out_shape=[jax.ShapeDtypeStruct(p.shape, p.dtype) for p in parts],
        in_specs=[any_spec] * n, out_specs=[any_spec] * n,
        scratch_shapes=[pltpu.SemaphoreType.DMA((3 * n,)), pltpu.SemaphoreType.DMA((3 * n,)),
                        pltpu.SemaphoreType.DMA((n,))],
    )(*parts)


def _matmul(name, mode, grid, a, a_spec, bs, b_specs, extras, extra_specs, out_shapes, out_specs, acc_shape,
            epilogue, a_fn=None):
    nk, nb, ne, no = grid[2], len(bs), len(extras), len(out_shapes)

    def body(*refs):
        a_ref, b_refs, e_refs = refs[0], refs[1:1 + nb], refs[1 + nb:1 + nb + ne]
        o_refs, acc_refs = refs[1 + nb + ne:1 + nb + ne + no], refs[1 + nb + ne + no:]
        a_val = a_ref[...]
        if a_fn is not None:
            a_val = a_fn(a_val)
        a_val = a_val.astype(BF16)

        def product(b_ref):
            return lax.dot_general(a_val, b_ref[...].astype(BF16), _DN[mode], preferred_element_type=F32)

        def finish(accs):
            outs = epilogue(accs, [e[...] for e in e_refs])
            for o_ref, o in zip(o_refs, outs):
                o_ref[...] = o.astype(o_ref.dtype)

        if nk == 1:
            finish([product(b_ref) for b_ref in b_refs])
        else:
            kk = pl.program_id(2)

            @pl.when(kk == 0)
            def _():
                for acc in acc_refs:
                    acc[...] = jnp.zeros_like(acc)

            for acc, b_ref in zip(acc_refs, b_refs):
                acc[...] += product(b_ref)

            @pl.when(kk == nk - 1)
            def _():
                finish([acc[...] for acc in acc_refs])

    scratch = [] if nk == 1 else [pltpu.VMEM(acc_shape, F32) for _ in range(nb)]
    return pl.pallas_call(
        body, name=name, grid=grid, in_specs=[a_spec] + list(b_specs) + list(extra_specs), out_specs=list(out_specs),
        out_shape=list(out_shapes), scratch_shapes=scratch, compiler_params=_params(3),
    )(a, *bs, *extras)


def _mm_tiles(m, n, k, tiles):
    return _tile(m, tiles[0], 16), _tile(n, tiles[1], LANES), _tile(k, tiles[2], LANES)


_FFN_BEST = {"in": (1024, 512, 2048), "out": (1024, 256, 5632), "out_bwd": (2048, 256, 2048),
             "in_bwd": (1024, 512, 2816), "out_dw": (512, 2048, 1024), "in_dw": (1024, 512, 2048)}
FFN_TILES = {
    "l0a": _FFN_BEST,
    "l0b": _FFN_BEST,
    "l1a": {"in": (512, 1408, 2048), "out": (1024, 512, 5632), "out_bwd": (2048, 512, 2048),
            "in_bwd": (1024, 512, 5632), "out_dw": (512, 2048, 2048), "in_dw": (2048, 512, 2048)},
    "l1b": {"in": (1024, 512, 2048), "out": (512, 1024, 5632), "out_bwd": (1024, 256, 2048),
            "in_bwd": (1024, 1024, 2816), "out_dw": (512, 1024, 4096), "in_dw": (1024, 512, 4096)},
}
FULL_K = (1024, 1024, 2048)
LOOP_K = (1024, 1024, 2048)


def _mm_nn(name, a, b, out_dtype, b_col0=0, n=None, tiles=FULL_K):
    m, k = a.shape
    n = b.shape[1] if n is None else n
    tm, tn, tk = _mm_tiles(m, n, k, tiles)
    off = b_col0 // tn
    assert b_col0 % tn == 0
    return _matmul(
        name, "nn", (m // tm, n // tn, k // tk), a, pl.BlockSpec((tm, tk), lambda i, j, kk: (i, kk)),
        [b], [pl.BlockSpec((tk, tn), lambda i, j, kk: (kk, j + off))], [], [],
        [jax.ShapeDtypeStruct((m, n), out_dtype)], [pl.BlockSpec((tm, tn), lambda i, j, kk: (i, j))], (tm, tn),
        lambda accs, ex: accs)[0]


def _mm_bias_rows(name, a, b, bias, a_fn):
    m, k = a.shape
    n = b.shape[1]
    tn = _tile(n, 768, LANES)
    return _matmul(
        name, "nn", (1, n // tn, 1), a, pl.BlockSpec((m, k), lambda i, j, kk: (0, 0)),
        [b], [pl.BlockSpec((k, tn), lambda i, j, kk: (0, j))],
        [bias], [pl.BlockSpec((1, tn), lambda i, j, kk: (0, j))],
        [jax.ShapeDtypeStruct((m, n), F32)], [pl.BlockSpec((m, tn), lambda i, j, kk: (0, j))], (m, tn),
        lambda accs, ex: [accs[0] + ex[0]], a_fn=a_fn)[0]


def _mm_swiglu(name, h, w_in, f, tiles):
    s, d = h.shape
    tm, tn, tk = _mm_tiles(s, f, d, tiles)
    nf = f // tn

    def epilogue(accs, ex):
        a_pre, b_pre = accs
        return [jnp.stack([a_pre, b_pre]), _silu(a_pre) * b_pre]

    return _matmul(
        name, "nn", (s // tm, nf, d // tk), h, pl.BlockSpec((tm, tk), lambda i, j, kk: (i, kk)),
        [w_in, w_in], [pl.BlockSpec((tk, tn), lambda i, j, kk: (kk, j)),
                       pl.BlockSpec((tk, tn), lambda i, j, kk: (kk, j + nf))], [], [],
        [jax.ShapeDtypeStruct((2, s, f), F32), jax.ShapeDtypeStruct((s, f), BF16)],
        [pl.BlockSpec((2, tm, tn), lambda i, j, kk: (0, i, j)), pl.BlockSpec((tm, tn), lambda i, j, kk: (i, j))],
        (tm, tn), epilogue)


def _mm_residual(name, a, w, x_in, gvec, tiles=FULL_K):
    s, k = a.shape
    d = w.shape[1]
    tm, tn, tk = _mm_tiles(s, d, k, tiles)

    def epilogue(accs, ex):
        x_blk, g_row = ex
        return [accs[0], x_blk + g_row * accs[0]]

    tile = pl.BlockSpec((tm, tn), lambda i, j, kk: (i, j))
    return _matmul(
        name, "nn", (s // tm, d // tn, k // tk), a, pl.BlockSpec((tm, tk), lambda i, j, kk: (i, kk)),
        [w], [pl.BlockSpec((tk, tn), lambda i, j, kk: (kk, j))],
        [x_in, gvec], [tile, pl.BlockSpec((1, tn), lambda i, j, kk: (0, j))],
        [jax.ShapeDtypeStruct((s, d), F32), jax.ShapeDtypeStruct((s, d), F32)], [tile, tile], (tm, tn), epilogue)


def _mm_nt(name, a, w, out_dtype, w_col0=0, extra_add=None, tiles=FULL_K):
    s, k = a.shape
    n = w.shape[0]
    tm, tn, tk = _mm_tiles(s, n, k, tiles)
    off = w_col0 // tk
    assert w_col0 % tk == 0
    tile = pl.BlockSpec((tm, tn), lambda i, j, kk: (i, j))
    extras, especs = ([], []) if extra_add is None else ([extra_add], [tile])
    return _matmul(
        name, "nt", (s // tm, n // tn, k // tk), a, pl.BlockSpec((tm, tk), lambda i, j, kk: (i, kk)),
        [w], [pl.BlockSpec((tn, tk), lambda i, j, kk: (j, kk + off))], extras, especs,
        [jax.ShapeDtypeStruct((s, n), out_dtype)], [tile], (tm, tn),
        lambda accs, ex: [accs[0] + ex[0]] if ex else accs)[0]


def _mm_nt_parts(name, a3, w, out_dtype, tiles=LOOP_K):
    p, s, k = a3.shape
    n = w.shape[0]
    tm, tn, tk = _mm_tiles(s, n, k, tiles)
    nkp = k // tk
    return _matmul(
        name, "nt", (s // tm, n // tn, p * nkp), a3,
        pl.BlockSpec((None, tm, tk), lambda i, j, kk: (kk // nkp, i, kk % nkp)),
        [w], [pl.BlockSpec((tn, tk), lambda i, j, kk: (j, kk))], [], [],
        [jax.ShapeDtypeStruct((s, n), out_dtype)], [pl.BlockSpec((tm, tn), lambda i, j, kk: (i, j))], (tm, tn),
        lambda accs, ex: accs)[0]


def _mm_nt_swiglu_bwd(name, dy, w_out, ab, tiles):
    s, d = dy.shape
    f = w_out.shape[0]
    tm, tn, tk = _mm_tiles(s, f, d, tiles)

    def epilogue(accs, ex):
        du, a_pre, b_pre = accs[0], ex[0][0], ex[0][1]
        sig = jax.nn.sigmoid(a_pre)
        da = du * b_pre * (sig * (1.0 + a_pre * (1.0 - sig)))
        db = du * (a_pre * sig)
        return [jnp.stack([da, db])]

    pair = pl.BlockSpec((2, tm, tn), lambda i, j, kk: (0, i, j))
    return _matmul(
        name, "nt", (s // tm, f // tn, d // tk), dy, pl.BlockSpec((tm, tk), lambda i, j, kk: (i, kk)),
        [w_out], [pl.BlockSpec((tn, tk), lambda i, j, kk: (j, kk))], [ab], [pair],
        [jax.ShapeDtypeStruct((2, s, f), BF16)], [pair], (tm, tn), epilogue)[0]


def _mm_tn(name, a, b, out_dtype, a_fn=None, tiles=LOOP_K):
    s, m = a.shape
    tm, tk = _tile(m, tiles[0], LANES), _tile(s, tiles[2], 16)
    if b.ndim == 3:
        npart = b.shape[2]
        n = b.shape[0] * npart
        tn = _tile(npart, tiles[1], LANES)
        nj = npart // tn
        b_spec = pl.BlockSpec((None, tk, tn), lambda i, j, kk: (j // nj, kk, j % nj))
    else:
        n = b.shape[1]
        tn = _tile(n, tiles[1], LANES)
        b_spec = pl.BlockSpec((tk, tn), lambda i, j, kk: (kk, j))
    return _matmul(
        name, "tn", (m // tm, n // tn, s // tk), a, pl.BlockSpec((tk, tm), lambda i, j, kk: (kk, i)),
        [b], [b_spec], [], [],
        [jax.ShapeDtypeStruct((m, n), out_dtype)], [pl.BlockSpec((tm, tn), lambda i, j, kk: (i, j))], (tm, tn),
        lambda accs, ex: accs, a_fn=a_fn)[0]


def _row_tile(s):
    return _tile(s, 256, 16)


def _fold_rows(v):
    return jnp.sum(v.reshape(v.shape[0] // SUBLANES, SUBLANES, v.shape[1]), axis=0)


def _accumulate(i, acc_ref, val):
    @pl.when(i == 0)
    def _():
        acc_ref[...] = val

    @pl.when(i > 0)
    def _():
        acc_ref[...] += val


def _norm_mod(name, x, g, shift, scale):
    s, d = x.shape
    tb = _row_tile(s)

    def body(x_ref, g_ref, sh_ref, sc_ref, h_ref):
        xv = x_ref[...]
        rstd = lax.rsqrt(jnp.mean(xv * xv, axis=-1, keepdims=True) + EPS)
        h_ref[...] = ((xv * rstd) * g_ref[...] * (1.0 + sc_ref[...]) + sh_ref[...]).astype(BF16)

    row = pl.BlockSpec((tb, d), lambda i: (i, 0))
    vec = pl.BlockSpec((1, d), lambda i: (0, 0))
    return pl.pallas_call(body, name=name, grid=(s // tb,), in_specs=[row, vec, vec, vec], out_specs=row,
                          out_shape=jax.ShapeDtypeStruct((s, d), BF16), compiler_params=_params(1))(x, g, shift, scale)


def _rms_bwd(xv, dn_g, d):
    rstd = lax.rsqrt(jnp.mean(xv * xv, axis=-1, keepdims=True) + EPS)
    xhat = xv * rstd
    dx = rstd * (dn_g - xhat * (jnp.sum(dn_g * xhat, axis=-1, keepdims=True) * (1.0 / d)))
    return xhat, dx


def _norm_mod_bwd(name, dh, x, g, scale, dx_in):
    s, d = x.shape
    tb = _row_tile(s)
    nsteps = s // tb

    def body(dh_ref, x_ref, g_ref, sc_ref, dxin_ref, dx_ref, dsh_ref, dsc_ref, dg_ref, a_sh, a_sc, a_g):
        i = pl.program_id(0)
        dhv, gv = dh_ref[...], g_ref[...]
        dn = dhv * (1.0 + sc_ref[...])
        xhat, dx = _rms_bwd(x_ref[...], dn * gv, d)
        dx_ref[...] = dxin_ref[...] + dx
        _accumulate(i, a_sh, _fold_rows(dhv))
        _accumulate(i, a_sc, _fold_rows(dhv * (xhat * gv)))
        _accumulate(i, a_g, _fold_rows(dn * xhat))

        @pl.when(i == nsteps - 1)
        def _():
            dsh_ref[...] = jnp.sum(a_sh[...], axis=0, keepdims=True)
            dsc_ref[...] = jnp.sum(a_sc[...], axis=0, keepdims=True)
            dg_ref[...] = jnp.sum(a_g[...], axis=0, keepdims=True)

    row = pl.BlockSpec((tb, d), lambda i: (i, 0))
    vec = pl.BlockSpec((1, d), lambda i: (0, 0))
    vshape = jax.ShapeDtypeStruct((1, d), F32)
    return pl.pallas_call(
        body, name=name, grid=(nsteps,), in_specs=[row, row, vec, vec, row], out_specs=[row, vec, vec, vec],
        out_shape=[jax.ShapeDtypeStruct((s, d), F32), vshape, vshape, vshape],
        scratch_shapes=[pltpu.VMEM((SUBLANES, d), F32)] * 3, compiler_params=_params(1))(dh, x, g, scale, dx_in)


def _gate_bwd(name, dx, y, gvec, coef):
    s, d = dx.shape
    tb = _row_tile(s)
    nsteps = s // tb

    def body(dx_ref, y_ref, g_ref, dy_ref, dg_ref, acc):
        i = pl.program_id(0)
        dxv = dx_ref[...]
        dy_ref[...] = (dxv * g_ref[...]).astype(BF16)
        _accumulate(i, acc, _fold_rows(dxv * y_ref[...]))

        @pl.when(i == nsteps - 1)
        def _():
            dg_ref[...] = coef * jnp.sum(acc[...], axis=0, keepdims=True)

    row = pl.BlockSpec((tb, d), lambda i: (i, 0))
    vec = pl.BlockSpec((1, d), lambda i: (0, 0))
    return pl.pallas_call(
        body, name=name, grid=(nsteps,), in_specs=[row, row, vec], out_specs=[row, vec],
        out_shape=[jax.ShapeDtypeStruct((s, d), BF16), jax.ShapeDtypeStruct((1, d), F32)],
        scratch_shapes=[pltpu.VMEM((SUBLANES, d), F32)], compiler_params=_params(1))(dx, y, gvec)


def _final_loss(name, x, target, g):
    s, d = x.shape
    tb = _row_tile(s)
    nsteps = s // tb

    def body(x_ref, t_ref, g_ref, dx_ref, loss_ref, dg_ref, a_loss, a_g):
        i = pl.program_id(0)
        xv, gv = x_ref[...], g_ref[...]
        rstd = lax.rsqrt(jnp.mean(xv * xv, axis=-1, keepdims=True) + EPS)
        xhat = xv * rstd
        err = xhat * gv - t_ref[...]
        dyv = err * (1.0 / d)
        dn_g = dyv * gv
        dx_ref[...] = rstd * (dn_g - xhat * (jnp.sum(dn_g * xhat, axis=-1, keepdims=True) * (1.0 / d)))
        _accumulate(i, a_loss, _fold_rows(err * err))
        _accumulate(i, a_g, _fold_rows(dyv * xhat))

        @pl.when(i == nsteps - 1)
        def _():
            total = jnp.sum(jnp.sum(a_loss[...], axis=0, keepdims=True), axis=1, keepdims=True) * (0.5 / d)
            loss_ref[...] = jnp.broadcast_to(total, (1, LANES))
            dg_ref[...] = jnp.sum(a_g[...], axis=0, keepdims=True)

    row = pl.BlockSpec((tb, d), lambda i: (i, 0))
    vec = pl.BlockSpec((1, d), lambda i: (0, 0))
    return pl.pallas_call(
        body, name=name, grid=(nsteps,), in_specs=[row, row, vec],
        out_specs=[row, pl.BlockSpec((1, LANES), lambda i: (0, 0)), vec],
        out_shape=[jax.ShapeDtypeStruct((s, d), F32), jax.ShapeDtypeStruct((1, LANES), F32),
                   jax.ShapeDtypeStruct((1, d), F32)],
        scratch_shapes=[pltpu.VMEM((SUBLANES, d), F32)] * 2, compiler_params=_params(1))(x, target, g)


def _shift_rows(cur, halo, n):
    rolled = pltpu.roll(cur, n, 0)
    row = lax.broadcasted_iota(jnp.int32, cur.shape, 0)
    for r in range(n):
        rolled = jnp.where(row == r, halo[SUBLANES - n + r:SUBLANES - n + r + 1, :], rolled)
    return rolled


def _shift_rows_up(cur, halo, n):
    tb = cur.shape[0]
    rolled = pltpu.roll(cur, tb - n, 0)
    row = lax.broadcasted_iota(jnp.int32, cur.shape, 0)
    for r in range(n):
        rolled = jnp.where(row == tb - n + r, halo[r:r + 1, :], rolled)
    return rolled


def _conv_specs(s, d, tb, tn):
    nd = d // tn
    hb = tb // SUBLANES
    cur = lambda part: pl.BlockSpec((tb, tn), lambda i, j: (i, j + part * nd))
    prev = lambda part: pl.BlockSpec((SUBLANES, tn), lambda i, j: (jnp.maximum(i * hb - 1, 0), j + part * nd))
    nxt = lambda part: pl.BlockSpec((SUBLANES, tn), lambda i, j: (jnp.minimum((i + 1) * hb, s // SUBLANES - 1),
                                                                  j + part * nd))
    return cur, prev, nxt


def _conv_fwd(name, p, conv_w, conv_b):
    s, d = p.shape[0], p.shape[1] // 3
    tb, tn = _row_tile(s), _tile(d, 512, LANES)
    cur, prev, _ = _conv_specs(s, d, tb, tn)

    def body(bg_ref, cg_ref, xv_ref, cgp_ref, xvp_ref, w_ref, b_ref, z_ref):
        i = pl.program_id(0)
        u = cg_ref[...] * xv_ref[...]
        up = jnp.where(i > 0, cgp_ref[...] * xvp_ref[...], 0.0)
        w = w_ref[...]
        uc = w[0:1, :] * _shift_rows(u, up, 2) + w[1:2, :] * _shift_rows(u, up, 1) + w[2:3, :] * u + b_ref[...]
        z_ref[...] = (bg_ref[...] * uc).astype(BF16)

    return pl.pallas_call(
        body, name=name, grid=(s // tb, d // tn),
        in_specs=[cur(0), cur(1), cur(2), prev(1), prev(2), pl.BlockSpec((3, tn), lambda i, j: (0, j)),
                  pl.BlockSpec((1, tn), lambda i, j: (0, j))],
        out_specs=pl.BlockSpec((tb, tn), lambda i, j: (i, j)),
        out_shape=jax.ShapeDtypeStruct((s, d), BF16), compiler_params=_params(2))(p, p, p, p, p, conv_w, conv_b)


def _conv_bwd(name, dz, p, conv_w, conv_b):
    s, d = dz.shape
    tb, tn = _row_tile(s), _tile(d, 512, LANES)
    nsteps = s // tb
    cur, prev, nxt = _conv_specs(s, d, tb, tn)

    def body(dz_ref, bg_ref, cg_ref, xv_ref, cgp_ref, xvp_ref, dzn_ref, bgn_ref, w_ref, b_ref,
             dp_ref, dw_ref, db_ref, a_w0, a_w1, a_w2, a_b):
        j, i = pl.program_id(0), pl.program_id(1)
        cg, xv, bg, dzv = cg_ref[...], xv_ref[...], bg_ref[...], dz_ref[...]
        u = cg * xv
        up = jnp.where(i > 0, cgp_ref[...] * xvp_ref[...], 0.0)
        w = w_ref[...]
        u1, u2 = _shift_rows(u, up, 1), _shift_rows(u, up, 2)
        uc = w[0:1, :] * u2 + w[1:2, :] * u1 + w[2:3, :] * u + b_ref[...]
        duc = dzv * bg
        ducn = jnp.where(i < nsteps - 1, dzn_ref[...] * bgn_ref[...], 0.0)
        du = w[2:3, :] * duc + w[1:2, :] * _shift_rows_up(duc, ducn, 1) + w[0:1, :] * _shift_rows_up(duc, ducn, 2)
        dp_ref[0] = (dzv * uc).astype(BF16)
        dp_ref[1] = (du * xv).astype(BF16)
        dp_ref[2] = (du * cg).astype(BF16)
        _accumulate(i, a_w0, _fold_rows(duc * u2))
        _accumulate(i, a_w1, _fold_rows(duc * u1))
        _accumulate(i, a_w2, _fold_rows(duc * u))
        _accumulate(i, a_b, _fold_rows(duc))

        @pl.when(i == nsteps - 1)
        def _():
            dw_ref[0:1, :] = jnp.sum(a_w0[...], axis=0, keepdims=True)
            dw_ref[1:2, :] = jnp.sum(a_w1[...], axis=0, keepdims=True)
            dw_ref[2:3, :] = jnp.sum(a_w2[...], axis=0, keepdims=True)
            db_ref[...] = jnp.sum(a_b[...], axis=0, keepdims=True)

    swap = lambda spec: pl.BlockSpec(spec.block_shape, lambda j, i, _m=spec.index_map: _m(i, j))
    return pl.pallas_call(
        body, name=name, grid=(d // tn, nsteps),
        in_specs=[pl.BlockSpec((tb, tn), lambda j, i: (i, j)), swap(cur(0)), swap(cur(1)), swap(cur(2)),
                  swap(prev(1)), swap(prev(2)), swap(nxt(0)), swap(nxt(0)),
                  pl.BlockSpec((3, tn), lambda j, i: (0, j)), pl.BlockSpec((1, tn), lambda j, i: (0, j))],
        out_specs=[pl.BlockSpec((3, tb, tn), lambda j, i: (0, i, j)), pl.BlockSpec((3, tn), lambda j, i: (0, j)),
                   pl.BlockSpec((1, tn), lambda j, i: (0, j))],
        out_shape=[jax.ShapeDtypeStruct((3, s, d), BF16), jax.ShapeDtypeStruct((3, d), F32),
                   jax.ShapeDtypeStruct((1, d), F32)],
        scratch_shapes=[pltpu.VMEM((SUBLANES, tn), F32)] * 4, compiler_params=_params(2),
    )(dz, p, p, p, p, p, dz, p, conv_w, conv_b)


def _block_cumsum(v):
    tb = v.shape[0]
    row = lax.broadcasted_iota(jnp.int32, v.shape, 0)
    sh = 1
    while sh < tb:
        v = v + jnp.where(row >= sh, pltpu.roll(v, sh, 0), 0.0)
        sh *= 2
    return v


def _fgate_fwd(name, zf, b_f):
    s = zf.shape[0]
    tb = _tile(s, 512, SUBLANES)

    def body(z_ref, b_ref, f_ref, ft_ref, carry):
        i = pl.program_id(0)
        z = z_ref[...] + b_ref[...]
        ls = jnp.minimum(z, 0.0) - jnp.log(1.0 + jnp.exp(-jnp.abs(z)))
        run = _block_cumsum(ls)

        @pl.when(i == 0)
        def _():
            carry[...] = jnp.zeros_like(carry)

        out = run + carry[...]
        f_ref[...] = out
        ft_ref[...] = jnp.transpose(out)
        carry[...] = out[tb - 1:tb, :]

    blk = pl.BlockSpec((tb, LANES), lambda i: (i, 0))
    return pl.pallas_call(
        body, name=name, grid=(s // tb,), in_specs=[blk, pl.BlockSpec((1, LANES), lambda i: (0, 0))],
        out_specs=[blk, pl.BlockSpec((LANES, tb), lambda i: (0, i))],
        out_shape=[jax.ShapeDtypeStruct((s, LANES), F32), jax.ShapeDtypeStruct((LANES, s), F32)],
        scratch_shapes=[pltpu.VMEM((1, LANES), F32)], compiler_params=_params(1))(zf, b_f)


def _fgate_bwd(name, dfcum, zf, b_f):
    s = zf.shape[0]
    tb = _tile(s, 512, SUBLANES)
    nsteps = s // tb

    def body(df_ref, z_ref, b_ref, dz_ref, db_ref, carry, acc):
        i = pl.program_id(0)
        df = df_ref[...]
        incl = _block_cumsum(df)
        total = incl[tb - 1:tb, :]

        @pl.when(i == 0)
        def _():
            carry[...] = jnp.zeros_like(carry)

        suffix = total - incl + df + carry[...]
        carry[...] += total
        dzv = suffix * jax.nn.sigmoid(-(z_ref[...] + b_ref[...]))
        dz_ref[...] = dzv
        _accumulate(i, acc, _fold_rows(dzv))

        @pl.when(i == nsteps - 1)
        def _():
            db_ref[...] = jnp.sum(acc[...], axis=0, keepdims=True)

    blk = pl.BlockSpec((tb, LANES), lambda i: (nsteps - 1 - i, 0))
    vec = pl.BlockSpec((1, LANES), lambda i: (0, 0))
    return pl.pallas_call(
        body, name=name, grid=(nsteps,), in_specs=[blk, blk, vec], out_specs=[blk, vec],
        out_shape=[jax.ShapeDtypeStruct((s, LANES), F32), jax.ShapeDtypeStruct((1, LANES), F32)],
        scratch_shapes=[pltpu.VMEM((1, LANES), F32), pltpu.VMEM((SUBLANES, LANES), F32)],
        compiler_params=_params(1))(dfcum, zf, b_f)


def _lane_column(block, lane):
    sel = lax.broadcasted_iota(jnp.int32, block.shape, 1) == lane
    return jnp.sum(jnp.where(sel, block, 0.0), axis=1, keepdims=True)


def _col_to_row(col):
    return jnp.transpose(jnp.broadcast_to(col, (col.shape[0], LANES)))[0:1, :]


def _row_to_col(row):
    return jnp.transpose(jnp.broadcast_to(row, (LANES, row.shape[1])))[:, 0:1]


def _attn_tiles(s):
    t = _tile(s, 512, LANES)
    return t, t


def _attn_fwd(name, q, kv, fcum, fcum_t, n_heads):
    s, d = q.shape
    dh = d // n_heads
    tq, tk = _attn_tiles(s)
    scale = 1.0 / math.sqrt(dh)

    def body(q_ref, k_ref, v_ref, f_ref, ft_ref, o_ref, l_ref):
        h, qi = pl.program_id(0), pl.program_id(1)
        qv = q_ref[...]
        fq = _lane_column(f_ref[...], h)

        def step(kj, carry, diagonal):
            m, l, acc = carry
            start = pl.multiple_of(kj * tk, tk)
            kb = k_ref[pl.ds(start, tk), :]
            vb = v_ref[pl.ds(start, tk), :]
            sc = lax.dot_general(qv, kb, _DN["nt"], preferred_element_type=F32) * scale
            sc = sc + fq - ft_ref[:, pl.ds(start, tk)]
            if diagonal:
                keep = lax.broadcasted_iota(jnp.int32, sc.shape, 0) >= lax.broadcasted_iota(jnp.int32, sc.shape, 1)
                sc = jnp.where(keep, sc, -jnp.inf)
            m_new = jnp.maximum(m, jnp.max(sc, axis=1, keepdims=True))
            alpha = jnp.exp(m - m_new)
            pr = jnp.exp(sc - m_new)
            l = alpha * l + jnp.sum(pr, axis=1, keepdims=True)
            acc = alpha * acc + lax.dot_general(pr.astype(BF16), vb, _DN["nn"], preferred_element_type=F32)
            return m_new, l, acc

        init = (jnp.full((tq, 1), -jnp.inf, F32), jnp.zeros((tq, 1), F32), jnp.zeros((tq, dh), F32))
        carry = lax.fori_loop(0, qi, lambda kj, cr: step(kj, cr, False), init)
        m, l, acc = step(qi, carry, True)
        o_ref[...] = (acc / l).astype(BF16)
        l_ref[...] = _col_to_row(m + jnp.log(l))

    nh = n_heads
    return pl.pallas_call(
        body, name=name, grid=(nh, s // tq),
        in_specs=[pl.BlockSpec((tq, dh), lambda h, i: (i, h)),
                  pl.BlockSpec((s, dh), lambda h, i: (0, h)),
                  pl.BlockSpec((s, dh), lambda h, i: (0, nh + h)),
                  pl.BlockSpec((tq, LANES), lambda h, i: (i, 0)),
                  pl.BlockSpec((None, 1, s), lambda h, i: (h, 0, 0))],
        out_specs=[pl.BlockSpec((tq, dh), lambda h, i: (i, h)),
                   pl.BlockSpec((None, 1, tq), lambda h, i: (h, 0, i))],
        out_shape=[jax.ShapeDtypeStruct((s, d), BF16), jax.ShapeDtypeStruct((nh, 1, s), F32)],
        compiler_params=_params(2))(q, kv, kv, fcum, fcum_t)


def _attn_delta(name, d_o, o, n_heads):
    s, d = d_o.shape
    dh = d // n_heads
    tb = _row_tile(s)

    def body(do_ref, o_ref, dl_ref, dlt_ref, dob_ref):
        dov = do_ref[...]
        prod = dov * o_ref[...].astype(F32)
        lane = lax.broadcasted_iota(jnp.int32, (tb, LANES), 1)
        out = jnp.zeros((tb, LANES), F32)
        for h in range(n_heads):
            out = jnp.where(lane == h, jnp.sum(prod[:, h * dh:(h + 1) * dh], axis=1, keepdims=True), out)
        dl_ref[...] = out
        dlt_ref[...] = jnp.transpose(out)
        dob_ref[...] = dov.astype(BF16)

    row = pl.BlockSpec((tb, d), lambda i: (i, 0))
    return pl.pallas_call(
        body, name=name, grid=(s // tb,), in_specs=[row, row],
        out_specs=[pl.BlockSpec((tb, LANES), lambda i: (i, 0)), pl.BlockSpec((LANES, tb), lambda i: (0, i)), row],
        out_shape=[jax.ShapeDtypeStruct((s, LANES), F32), jax.ShapeDtypeStruct((LANES, s), F32),
                   jax.ShapeDtypeStruct((s, d), BF16)],
        compiler_params=_params(1))(d_o, o)


def _attn_dq(name, q, kv, d_o, fcum, fcum_t, lse_t, delta, n_heads):
    s, d = q.shape
    dh = d // n_heads
    tq, tk = _attn_tiles(s)
    scale = 1.0 / math.sqrt(dh)

    def body(q_ref, k_ref, v_ref, do_ref, f_ref, ft_ref, l_ref, dl_ref, dq_ref, df_ref):
        h, qi = pl.program_id(0), pl.program_id(1)
        qv, dov = q_ref[...], do_ref[...]
        fq = _lane_column(f_ref[...], h)
        lse_col = _row_to_col(l_ref[...])
        dl_col = _lane_column(dl_ref[...], h)

        def step(kj, carry, diagonal):
            acc, dfq = carry
            start = pl.multiple_of(kj * tk, tk)
            kb = k_ref[pl.ds(start, tk), :]
            vb = v_ref[pl.ds(start, tk), :]
            sc = lax.dot_general(qv, kb, _DN["nt"], preferred_element_type=F32) * scale
            sc = sc + fq - ft_ref[:, pl.ds(start, tk)]
            pr = jnp.exp(sc - lse_col)
            if diagonal:
                keep = lax.broadcasted_iota(jnp.int32, sc.shape, 0) >= lax.broadcasted_iota(jnp.int32, sc.shape, 1)
                pr = jnp.where(keep, pr, 0.0)
            dp = lax.dot_general(dov, vb, _DN["nt"], preferred_element_type=F32)
            ds = pr * (dp - dl_col)
            acc = acc + lax.dot_general(ds.astype(BF16), kb, _DN["nn"], preferred_element_type=F32)
            return acc, dfq + jnp.sum(ds, axis=1, keepdims=True)

        init = (jnp.zeros((tq, dh), F32), jnp.zeros((tq, 1), F32))
        carry = lax.fori_loop(0, qi, lambda kj, cr: step(kj, cr, False), init)
        acc, dfq = step(qi, carry, True)
        dq_ref[...] = (acc * scale).astype(BF16)
        df_ref[...] = _col_to_row(dfq)

    nh = n_heads
    qspec = pl.BlockSpec((tq, dh), lambda h, i: (i, h))
    lanes = pl.BlockSpec((tq, LANES), lambda h, i: (i, 0))
    return pl.pallas_call(
        body, name=name, grid=(nh, s // tq),
        in_specs=[qspec, pl.BlockSpec((s, dh), lambda h, i: (0, h)), pl.BlockSpec((s, dh), lambda h, i: (0, nh + h)),
                  qspec, lanes, pl.BlockSpec((None, 1, s), lambda h, i: (h, 0, 0)),
                  pl.BlockSpec((None, 1, tq), lambda h, i: (h, 0, i)), lanes],
        out_specs=[qspec, pl.BlockSpec((None, 1, tq), lambda h, i: (h, 0, i))],
        out_shape=[jax.ShapeDtypeStruct((s, d), BF16), jax.ShapeDtypeStruct((nh, 1, s), F32)],
        compiler_params=_params(2))(q, kv, kv, d_o, fcum, fcum_t, lse_t, delta)


def _attn_dkv(name, q, kv, d_o, fcum, fcum_t, lse_t, delta_t, n_heads):
    s, d = q.shape
    dh = d // n_heads
    tq, tk = _attn_tiles(s)
    nq = s // tq
    scale = 1.0 / math.sqrt(dh)

    def body(q_ref, k_ref, v_ref, do_ref, f_ref, ft_ref, lt_ref, dlt_ref, dkv_ref, df_ref):
        h, kj = pl.program_id(0), pl.program_id(1)
        kb, vb = k_ref[...], v_ref[...]
        fk = _lane_column(f_ref[...], h)

        def step(qi, carry, diagonal):
            dk, dv, dfk = carry
            start = pl.multiple_of(qi * tq, tq)
            qb = q_ref[pl.ds(start, tq), :]
            dob = do_ref[pl.ds(start, tq), :]
            sc = lax.dot_general(kb, qb, _DN["nt"], preferred_element_type=F32) * scale
            sc = sc + ft_ref[:, pl.ds(start, tq)] - fk
            pr = jnp.exp(sc - lt_ref[:, pl.ds(start, tq)])
            if diagonal:
                keep = lax.broadcasted_iota(jnp.int32, sc.shape, 1) >= lax.broadcasted_iota(jnp.int32, sc.shape, 0)
                pr = jnp.where(keep, pr, 0.0)
            dv = dv + lax.dot_general(pr.astype(BF16), dob, _DN["nn"], preferred_element_type=F32)
            dp = lax.dot_general(vb, dob, _DN["nt"], preferred_element_type=F32)
            ds = pr * (dp - dlt_ref[:, pl.ds(start, tq)])
            dk = dk + lax.dot_general(ds.astype(BF16), qb, _DN["nn"], preferred_element_type=F32)
            dfk = dfk - jnp.sum(ds, axis=1, keepdims=True)
            return dk, dv, dfk

        init = (jnp.zeros((tk, dh), F32), jnp.zeros((tk, dh), F32), jnp.zeros((tk, 1), F32))
        carry = step(kj, init, True)
        dk, dv, dfk = lax.fori_loop(kj + 1, nq, lambda qi, cr: step(qi, cr, False), carry)
        dkv_ref[0] = (dk * scale).astype(BF16)
        dkv_ref[1] = dv.astype(BF16)
        df_ref[...] = _col_to_row(dfk)

    nh = n_heads
    full = lambda col0: pl.BlockSpec((s, dh), lambda h, j: (0, col0 + h))
    row_t = pl.BlockSpec((None, 1, s), lambda h, j: (h, 0, 0))
    return pl.pallas_call(
        body, name=name, grid=(nh, s // tk),
        in_specs=[full(0), pl.BlockSpec((tk, dh), lambda h, j: (j, h)), pl.BlockSpec((tk, dh), lambda h, j: (j, nh + h)),
                  full(0), pl.BlockSpec((tk, LANES), lambda h, j: (j, 0)), row_t, row_t, row_t],
        out_specs=[pl.BlockSpec((2, tk, dh), lambda h, j: (0, j, h)),
                   pl.BlockSpec((None, 1, tk), lambda h, j: (h, 0, j))],
        out_shape=[jax.ShapeDtypeStruct((2, s, d), BF16), jax.ShapeDtypeStruct((nh, 1, s), F32)],
        compiler_params=_params(2))(q, kv, kv, d_o, fcum, fcum_t, lse_t, delta_t)


def _sum_slots(name, g):
    n, r, lanes = g.shape

    def body(g_ref, o_ref):
        total = g_ref[0]
        for i in range(1, n):
            total = total + g_ref[i]
        o_ref[...] = total

    return pl.pallas_call(
        body, name=name, out_shape=jax.ShapeDtypeStruct((r, lanes), F32),
        in_specs=[pl.BlockSpec(memory_space=pltpu.VMEM)], out_specs=pl.BlockSpec(memory_space=pltpu.VMEM),
        compiler_params=pltpu.CompilerParams(vmem_limit_bytes=VMEM_LIMIT))(g)


def _adamw(name, parts, w, m, v):
    npart, r, c = parts.shape
    tb = _tile(r, max(SUBLANES, (2**18 // c) // SUBLANES * SUBLANES), SUBLANES)
    c1 = 1.0 - ADAM_B1 ** ADAM_STEP
    c2 = 1.0 - ADAM_B2 ** ADAM_STEP

    def body(p_ref, w_ref, m_ref, v_ref, g_out, d_out, m_out, v_out):
        g = p_ref[0].astype(F32)
        for i in range(1, npart):
            g = g + p_ref[i].astype(F32)
        m_new = ADAM_B1 * m_ref[...] + (1.0 - ADAM_B1) * g
        v_new = ADAM_B2 * v_ref[...] + (1.0 - ADAM_B2) * (g * g)
        m_hat = m_new / c1
        v_hat = v_new / c2
        g_out[...] = g
        d_out[...] = -ADAM_LR * (m_hat / (jnp.sqrt(v_hat) + ADAM_EPS) + ADAM_WD * w_ref[...])
        m_out[...] = m_new
        v_out[...] = v_new

    blk = pl.BlockSpec((tb, c), lambda i: (i, 0))
    shape = jax.ShapeDtypeStruct((r, c), F32)
    return pl.pallas_call(
        body, name=name, grid=(r // tb,), in_specs=[pl.BlockSpec((npart, tb, c), lambda i: (0, i, 0)), blk, blk, blk],
        out_specs=[blk] * 4, out_shape=[shape] * 4, compiler_params=_params(1))(parts, w, m, v)


def _pack(vectors):
    flat = jnp.concatenate([v.reshape(-1).astype(F32) for v in vectors])
    pad = (-flat.shape[0]) % (SUBLANES * LANES)
    return jnp.pad(flat, (0, pad)).reshape(-1, LANES)


def _unpack(flat, shapes):
    out, pos = [], 0
    for shp in shapes:
        size = math.prod(shp)
        out.append(flat[..., pos:pos + size].reshape(flat.shape[:-1] + tuple(shp)))
        pos += size
    return out


def _unshard_last(g, lead):
    nd = len(lead)
    return jnp.moveaxis(g, 0, nd).reshape(tuple(lead) + (-1,))


def _by_core_and_chip(p):
    return jnp.swapaxes(p.reshape((4, 2) + p.shape[1:]), 0, 1)


def _my_slice(full, me, width, axis):
    return lax.dynamic_slice_in_dim(full, me * width, width, axis)


def kernel(x, c, norm_g, w_ada, b_ada, w_ffn_in, w_ffn_out, w_conv_in, conv_w, conv_b, w_conv_out, kv_norm_g, w_ada_kv, b_ada_kv, w_kvf, b_fgate, w_q, w_o, final_g, loss_target, m_norm_g, m_w_ada, m_b_ada, m_w_ffn_in, m_w_ffn_out, m_w_conv_in, m_conv_w, m_conv_b, m_w_conv_out, m_kv_norm_g, m_w_ada_kv, m_b_ada_kv, m_w_kvf, m_b_fgate, m_w_q, m_w_o, m_final_g, v_norm_g, v_w_ada, v_b_ada, v_w_ffn_in, v_w_ffn_out, v_w_conv_in, v_conv_w, v_conv_b, v_w_conv_out, v_kv_norm_g, v_w_ada_kv, v_b_ada_kv, v_w_kvf, v_b_fgate, v_w_q, v_w_o, v_final_g):
    s, d = x.shape[1], x.shape[2]
    n_heads = b_fgate.shape[0]
    n_layers = w_ada.shape[0]
    dsh = d // N_DEV
    f = w_ffn_out.shape[2] * N_DEV
    ada_w = w_ada.shape[2]
    kv_w = w_ada_kv.shape[1]
    kvf_w = w_kvf.shape[1]
    assert n_layers == 2 and w_conv_in.shape[0] == 1 and w_q.shape[0] == 1
    assert (d // n_heads) % LANES == 0 and n_heads <= LANES and N_DEV * kvf_w == 2 * d + n_heads
    me = 4 * lax.axis_index("x") + 2 * lax.axis_index("y") + lax.axis_index("c")
    x0, target = x[0], loss_target[0]

    small = _allgather_small("gather_small_params", _pack([c, norm_g, conv_w, conv_b]))
    c_all, ng_sh, cw_sh, cb_sh = _unpack(small.reshape(N_DEV, -1), [(d,), (n_layers, 3, dsh), (3, dsh), (dsh,)])
    norm_g_full = _unshard_last(ng_sh, (n_layers, 3))
    conv_w_full = _unshard_last(cw_sh, (3,))
    conv_b_full = _unshard_last(cb_sh, ()).reshape(1, d)
    c_rows = jnp.pad(c_all, ((0, 16 - N_DEV), (0, 0)))

    ada_cols = [
        _mm_bias_rows(f"ada_rows_{l}", c_rows, w_ada[l], _my_slice(b_ada[l], me, ada_w, 0).reshape(1, ada_w), _silu)
        for l in range(n_layers)]
    ada_cols.append(_mm_bias_rows("ada_rows_kv", c_rows, w_ada_kv, _my_slice(b_ada_kv, me, kv_w, 0).reshape(1, kv_w),
                                  _silu))
    ada_part = jnp.concatenate([a[:N_DEV] for a in ada_cols], axis=1)
    ada_all = _allgather_small("gather_ada_rows", _pack([ada_part]))
    ada_all = ada_all.reshape(N_DEV, -1)[:, :ada_part.size].reshape(N_DEV, N_DEV, -1)
    ada_mine = lax.dynamic_index_in_dim(ada_all, me, axis=1, keepdims=False)
    ada = [ada_mine[:, l * ada_w:(l + 1) * ada_w].reshape(3, 3, 1, d) for l in range(n_layers)]
    ada_kv = ada_mine[:, n_layers * ada_w:].reshape(2, 1, d)

    shards = [w_ffn_in, w_ffn_out, w_conv_in[0], w_conv_out[0], w_kvf, w_q[0], w_o[0]]
    g_in, g_out, g_ci, g_co, g_kvf, g_q, g_o = _allgather_hbm("gather_weights", [w.astype(BF16) for w in shards])
    wf_in = _unshard_last(g_in, (n_layers, 2, d))
    wf_out = jnp.moveaxis(g_out, 0, 2).reshape(n_layers, 2, f, d)
    wc_in = _unshard_last(g_ci, (d,))
    wc_out, wq, wo = g_co.reshape(d, d), g_q.reshape(d, d), g_o.reshape(d, d)
    wkvf = _unshard_last(g_kvf, (d,))
    wkv = wkvf[:, :2 * d]
    wf_gate = jnp.pad(wkvf[:, 2 * d:], ((0, 0), (0, LANES - n_heads)))
    b_f = jnp.pad(b_fgate, (0, LANES - n_heads)).reshape(1, LANES)

    def ffn_fwd(tag, xs, l, sub, which):
        shift, scale, gate = ada[l][sub]
        tiles = FFN_TILES[tag]
        h = _norm_mod(f"norm_{tag}", xs, norm_g_full[l, sub].reshape(1, d), shift, scale)
        ab, u = _mm_swiglu(f"ffn_in_{tag}", h, wf_in[l, which], f, tiles["in"])
        y, x_next = _mm_residual(f"ffn_out_{tag}", u, wf_out[l, which], xs, 0.5 * gate, tiles["out"])
        return x_next, (xs, h, ab, u, y)

    x1, save_f0 = ffn_fwd("l0a", x0, 0, 0, 0)
    shift, scale, gate = ada[0][1]
    h_c = _norm_mod("norm_conv", x1, norm_g_full[0, 1].reshape(1, d), shift, scale)
    p_c = _mm_nn("conv_in", h_c, wc_in, F32)
    z_c = _conv_fwd("conv_mix", p_c, conv_w_full, conv_b_full)
    y_c, x2 = _mm_residual("conv_out", z_c, wc_out, x1, gate)
    x3, save_f1 = ffn_fwd("l0b", x2, 0, 2, 1)
    x4, save_f2 = ffn_fwd("l1a", x3, 1, 0, 0)
    shift, scale, gate_a = ada[1][1]
    h_q = _norm_mod("norm_q", x4, norm_g_full[1, 1].reshape(1, d), shift, scale)
    h_kv = _norm_mod("norm_kv", x4, kv_norm_g.reshape(1, d), ada_kv[0], ada_kv[1])
    q = _mm_nn("attn_q", h_q, wq, BF16)
    kv = _mm_nn("attn_kv", h_kv, wkv, BF16)
    zf = _mm_nn("attn_fgate", h_kv, wf_gate, F32)
    fcum, fcum_lanes = _fgate_fwd("fgate_cumsum", zf, b_f)
    fcum_t = fcum_lanes[:n_heads].reshape(n_heads, 1, s)
    o, lse_t = _attn_fwd("attn_fwd", q, kv, fcum, fcum_t, n_heads)
    y_a, x5 = _mm_residual("attn_out", o, wo, x4, gate_a)
    x6, save_f3 = ffn_fwd("l1b", x5, 1, 2, 1)
    dx, loss_part, d_final_g = _final_loss("final_loss", x6, target, final_g.reshape(1, d))

    d_ada = [[[None] * 3 for _ in range(3)] for _ in range(n_layers)]
    d_norm_g = [[None] * 3 for _ in range(n_layers)]
    dw_in = [[None] * 2 for _ in range(n_layers)]
    dw_out = [[None] * 2 for _ in range(n_layers)]

    def ffn_bwd(tag, dxs, saved, l, sub, which):
        xs, h, ab, u, y = saved
        _, scale, gate = ada[l][sub]
        tiles = FFN_TILES[tag]
        dy, d_ada[l][sub][2] = _gate_bwd(f"gate_bwd_{tag}", dxs, y, 0.5 * gate, 0.5)
        dab = _mm_nt_swiglu_bwd(f"ffn_out_bwd_{tag}", dy, wf_out[l, which], ab, tiles["out_bwd"])
        dw_out[l][which] = _mm_tn(f"ffn_out_dw_{tag}", u, dy, BF16, tiles=tiles["out_dw"])
        dh = _mm_nt_parts(f"ffn_in_bwd_{tag}", dab, wf_in[l, which], F32, tiles["in_bwd"])
        dw_in[l][which] = _mm_tn(f"ffn_in_dw_{tag}", h, dab, BF16, tiles=tiles["in_dw"])
        dxs, d_ada[l][sub][0], d_ada[l][sub][1], d_norm_g[l][sub] = _norm_mod_bwd(
            f"norm_bwd_{tag}", dh, xs, norm_g_full[l, sub].reshape(1, d), scale, dxs)
        return dxs

    dx = ffn_bwd("l1b", dx, save_f3, 1, 2, 1)
    dy, d_ada[1][1][2] = _gate_bwd("gate_bwd_attn", dx, y_a, gate_a, 1.0)
    d_o = _mm_nt("attn_out_bwd", dy, wo, F32)
    dwo = _mm_tn("attn_out_dw", o, dy, BF16)
    delta, delta_lanes, d_ob = _attn_delta("attn_delta", d_o, o, n_heads)
    delta_t = delta_lanes[:n_heads].reshape(n_heads, 1, s)
    dq, dfq = _attn_dq("attn_dq", q, kv, d_ob, fcum, fcum_t, lse_t, delta, n_heads)
    dkv, dfk = _attn_dkv("attn_dkv", q, kv, d_ob, fcum, fcum_t, lse_t, delta_t, n_heads)
    dfcum = jnp.pad((dfq + dfk).reshape(n_heads, s).T, ((0, 0), (0, LANES - n_heads)))
    dzf, d_bf = _fgate_bwd("fgate_bwd", dfcum, zf, b_f)
    dh_q = _mm_nt("attn_q_bwd", dq, wq, F32)
    dwq = _mm_tn("attn_q_dw", h_q, dq, BF16)
    dh_kv = _mm_nt_parts("attn_kv_bwd", dkv, wkv, F32)
    dh_kv = _mm_nt("attn_fgate_bwd", dzf, wf_gate, F32, extra_add=dh_kv)
    dwkv = _mm_tn("attn_kv_dw", h_kv, dkv, BF16)
    dwf = _mm_tn("attn_fgate_dw", h_kv, dzf, BF16)
    dx, d_ada[1][1][0], d_ada[1][1][1], d_norm_g[1][1] = _norm_mod_bwd(
        "norm_bwd_q", dh_q, x4, norm_g_full[1, 1].reshape(1, d), ada[1][1][1], dx)
    dx, d_kv_shift, d_kv_scale, d_kv_norm_g = _norm_mod_bwd(
        "norm_bwd_kv", dh_kv, x4, kv_norm_g.reshape(1, d), ada_kv[1], dx)
    dx = ffn_bwd("l1a", dx, save_f2, 1, 0, 0)
    dx = ffn_bwd("l0b", dx, save_f1, 0, 2, 1)
    dy, d_ada[0][1][2] = _gate_bwd("gate_bwd_conv", dx, y_c, ada[0][1][2], 1.0)
    dz = _mm_nt("conv_out_bwd", dy, wc_out, F32)
    dwc_out = _mm_tn("conv_out_dw", z_c, dy, BF16)
    dp, d_conv_w, d_conv_b = _conv_bwd("conv_mix_bwd", dz, p_c, conv_w_full, conv_b_full)
    dh_c = _mm_nt_parts("conv_in_bwd", dp, wc_in, F32)
    dwc_in = _mm_tn("conv_in_dw", h_c, dp, BF16)
    dx, d_ada[0][1][0], d_ada[0][1][1], d_norm_g[0][1] = _norm_mod_bwd(
        "norm_bwd_conv", dh_c, x1, norm_g_full[0, 1].reshape(1, d), ada[0][1][1], dx)
    dx = ffn_bwd("l0a", dx, save_f0, 0, 0, 0)
    grad_x = dx.reshape(1, s, d)

    d_ada_flat = jnp.concatenate([v.reshape(-1) for l in range(n_layers) for sub in range(3) for v in d_ada[l][sub]])
    d_norm_flat = jnp.concatenate([d_norm_g[l][sub].reshape(-1) for l in range(n_layers) for sub in range(3)])
    small_shapes = [(n_layers * 9 * d,), (2 * d,), (n_layers * 3 * d,), (3, d), (d,), (d,), (LANES,), (d,), (LANES,)]
    small_parts = _pack([d_ada_flat, d_kv_shift, d_kv_scale, d_norm_flat, d_conv_w, d_conv_b, d_kv_norm_g, d_bf,
                         d_final_g, loss_part])
    small_all = _allgather_small("gather_small_grads", small_parts)
    small_sum = _sum_slots("sum_small_grads", small_all).reshape(-1)
    g_b_ada, g_b_ada_kv, g_norm_full, g_conv_w_full, g_conv_b_full, g_kv_norm, g_bf, g_final, loss_v = _unpack(
        small_sum, small_shapes)
    loss = loss_v[0]
    d_ada_rows = small_all.reshape(N_DEV, -1)[:, :n_layers * 9 * d + 2 * d]
    d_ada_rows = jnp.pad(d_ada_rows, ((0, 16 - N_DEV), (0, 0)))

    results = {}

    def update(key, parts, w, m, v):
        shape = w.shape
        c_dim = shape[-1]
        outs = _adamw(f"adamw_{key}", parts.reshape(parts.shape[0], -1, c_dim), w.reshape(-1, c_dim),
                      m.reshape(-1, c_dim), v.reshape(-1, c_dim))
        results[key] = [o.reshape(shape) for o in outs]

    g_w_ada = []
    for l in range(n_layers):
        cols = _my_slice(d_ada_rows[:, l * 9 * d:(l + 1) * 9 * d], me, ada_w, 1)
        g_w_ada.append(_mm_tn(f"ada_dw_{l}", c_rows, cols, F32, a_fn=_silu))
    update("w_ada", jnp.stack(g_w_ada).reshape(1, n_layers * d, ada_w), w_ada, m_w_ada, v_w_ada)
    cols = _my_slice(d_ada_rows[:, n_layers * 9 * d:], me, kv_w, 1)
    update("w_ada_kv", _mm_tn("ada_dw_kv", c_rows, cols, F32, a_fn=_silu).reshape(1, d, kv_w),
           w_ada_kv, m_w_ada_kv, v_w_ada_kv)

    p_in = jnp.stack([jnp.stack(r) for r in dw_in])
    p_in = jnp.moveaxis(p_in.reshape(n_layers, 2, d, N_DEV, -1), 3, 0)
    p_out = jnp.stack([jnp.stack(r) for r in dw_out])
    p_out = jnp.moveaxis(p_out.reshape(n_layers, 2, N_DEV, -1, d), 2, 0)
    p_ci = jnp.moveaxis(dwc_in.reshape(d, N_DEV, -1), 1, 0)[:, None]
    p_co = dwc_out.reshape(N_DEV, 1, dsh, d)
    dwkvf = jnp.concatenate([dwkv, dwf[:, :n_heads]], axis=1)
    p_kvf = jnp.moveaxis(dwkvf.reshape(d, N_DEV, kvf_w), 1, 0)
    p_q, p_o = dwq.reshape(N_DEV, 1, dsh, d), dwo.reshape(N_DEV, 1, dsh, d)
    partials = [_by_core_and_chip(p) for p in [p_in, p_out, p_ci, p_co, p_kvf, p_q, p_o]]
    from_sibling = _exchange_sibling("reduce_within_chip", partials)
    core = lax.axis_index("c").astype(jnp.int32).reshape(1)
    chip_sums = []
    for idx, (mine, theirs) in enumerate(zip(partials, from_sibling)):
        c_dim = mine.shape[-1]
        total = _pair_add(f"add_sibling_{idx}", mine.reshape(2, 4, -1, c_dim), theirs.reshape(4, -1, c_dim), core)
        chip_sums.append(total.reshape(theirs.shape))
    r_in, r_out, r_ci, r_co, r_kvf, r_q, r_o = _exchange_chips("scatter_weight_grads", chip_sums)
    update("w_ffn_in", r_in, w_ffn_in, m_w_ffn_in, v_w_ffn_in)
    update("w_ffn_out", r_out, w_ffn_out, m_w_ffn_out, v_w_ffn_out)
    update("w_conv_in", r_ci, w_conv_in, m_w_conv_in, v_w_conv_in)
    update("w_conv_out", r_co, w_conv_out, m_w_conv_out, v_w_conv_out)
    update("w_kvf", r_kvf, w_kvf, m_w_kvf, v_w_kvf)
    update("w_q", r_q, w_q, m_w_q, v_w_q)
    update("w_o", r_o, w_o, m_w_o, v_w_o)

    small_keys = ["norm_g", "b_ada", "conv_w", "conv_b", "kv_norm_g", "b_ada_kv", "b_fgate", "final_g"]
    small_w = [norm_g, b_ada, conv_w, conv_b, kv_norm_g, b_ada_kv, b_fgate, final_g]
    small_m = [m_norm_g, m_b_ada, m_conv_w, m_conv_b, m_kv_norm_g, m_b_ada_kv, m_b_fgate, m_final_g]
    small_v = [v_norm_g, v_b_ada, v_conv_w, v_conv_b, v_kv_norm_g, v_b_ada_kv, v_b_fgate, v_final_g]
    small_g = [
        _my_slice(g_norm_full.reshape(n_layers, 3, d), me, dsh, 2), g_b_ada.reshape(b_ada.shape),
        _my_slice(g_conv_w_full, me, dsh, 1).reshape(conv_w.shape), _my_slice(g_conv_b_full, me, dsh, 0).reshape(
            conv_b.shape), g_kv_norm, g_b_ada_kv, g_bf[:n_heads], g_final]
    packed = _adamw("adamw_small", _pack(small_g)[None], _pack(small_w), _pack(small_m), _pack(small_v))
    for vals, idx in zip(packed, range(4)):
        for key, val in zip(small_keys, _unpack(vals.reshape(-1), [w.shape for w in small_w])):
            results.setdefault(key, [None] * 4)[idx] = val

    order = ["norm_g", "w_ada", "b_ada", "w_ffn_in", "w_ffn_out", "w_conv_in", "conv_w", "conv_b", "w_conv_out",
             "kv_norm_g", "w_ada_kv", "b_ada_kv", "w_kvf", "b_fgate", "w_q", "w_o", "final_g"]
    return (loss, grad_x, *[results[k][0] for k in order], *[results[k][1] for k in order],
            *[results[k][2] for k in order], *[results[k][3] for k in order])
```

```python
import functools
import math

import jax
import jax.numpy as jnp
from jax import lax
from jax.experimental import pallas as pl
from jax.experimental.pallas import tpu as pltpu

F32 = jnp.float32
BF16 = jnp.bfloat16
MESH = pl.DeviceIdType.MESH

N_DEV = 8
LANES = 128
SUBLANES = 8
VMEM_LIMIT = 56 * 2**20
EPS = 1e-6
ADAM_LR, ADAM_B1, ADAM_B2, ADAM_EPS, ADAM_WD, ADAM_STEP = 0.001, 0.9, 0.999, 1e-08, 0.01, 10

_DN = {"nn": (((1,), (0,)), ((), ())), "nt": (((1,), (1,)), ((), ())), "tn": (((0,), (0,)), ((), ()))}


def _tile(n, pref, align):
    t = min(pref, n)
    t -= t % align
    while t >= align:
        if n % t == 0:
            return t
        t -= align
    return n


def _params(n_grid):
    return pltpu.CompilerParams(vmem_limit_bytes=VMEM_LIMIT, dimension_semantics=("arbitrary",) * n_grid)


def _silu(v):
    return v * jax.nn.sigmoid(v)


def _position():
    x, y, c = lax.axis_index("x"), lax.axis_index("y"), lax.axis_index("c")
    return x, y, c


def _allgather_small(name, v):
    def body(v_ref, out_ref, send_sems, recv_sems, local_sem):
        x, y, c = _position()
        me, sibling = (x, y, c), (x, y, 1 - c)
        chips = [(1 - x, y), (x, 1 - y), (1 - x, 1 - y)]

        def slot(px, py, pc):
            return out_ref.at[4 * px + 2 * py + pc]

        def copy(k, block, to, src=None):
            return pltpu.make_async_remote_copy(
                src_ref=slot(*block) if src is None else src, dst_ref=slot(*block),
                send_sem=send_sems.at[k], recv_sem=recv_sems.at[k], device_id=to, device_id_type=MESH)

        mine = pltpu.make_async_copy(v_ref, slot(*me), local_sem)
        mine.start()
        first = [copy(0, me, sibling, src=v_ref)]
        first += [copy(1 + j, me, (*chip, c), src=v_ref) for j, chip in enumerate(chips)]
        for cp in first:
            cp.start()
        passed = [copy(4 + j, (*chip, c), sibling) for j, chip in enumerate(chips)]
        for j, chip in enumerate(chips):
            copy(1 + j, (*chip, c), me).wait_recv()
            passed[j].start()
        copy(0, sibling, me).wait_recv()
        for j, chip in enumerate(chips):
            copy(4 + j, (*chip, 1 - c), me).wait_recv()
        for cp in first + passed:
            cp.wait_send()
        mine.wait()

    return pl.pallas_call(
        body, name=name,
        out_shape=jax.ShapeDtypeStruct((N_DEV,) + v.shape, v.dtype),
        in_specs=[pl.BlockSpec(memory_space=pltpu.VMEM)],
        out_specs=pl.BlockSpec(memory_space=pltpu.VMEM),
        scratch_shapes=[pltpu.SemaphoreType.DMA((7,)), pltpu.SemaphoreType.DMA((7,)), pltpu.SemaphoreType.DMA],
        compiler_params=pltpu.CompilerParams(vmem_limit_bytes=VMEM_LIMIT),
    )(v)


def _owner_view(ref, kind, shard_shape, owner):
    r, w = shard_shape
    if kind == "rows":
        return ref.at[pl.ds(pl.multiple_of(owner * r, 16), r)]
    if kind == "cols":
        return ref.at[:, pl.ds(pl.multiple_of(owner * w, LANES), w)]
    return ref.at[owner]


def _full_shape(kind, shard_shape):
    r, w = shard_shape
    return {"rows": (N_DEV * r, w), "cols": (r, N_DEV * w), "lead": (N_DEV, r, w)}[kind]


class _GatherComm:
    mid_fraction = 0.75

    def __init__(self, items):
        self.ins = [shard for shard, _ in items]
        self.kinds = [kind for _, kind in items]
        n = len(items)
        self.out_shapes = [jax.ShapeDtypeStruct(_full_shape(k, a.shape), a.dtype) for a, k in items]
        self.sem_shapes = [pltpu.SemaphoreType.DMA((7 * n,)), pltpu.SemaphoreType.DMA((7 * n,)),
                           pltpu.SemaphoreType.DMA((n,))]
        self.results = None

    def _copy(self, refs, a, k, block, to, from_input=False):
        ins, outs, (send_sems, recv_sems, _) = refs
        px, py, pc = block
        dst = _owner_view(outs[a], self.kinds[a], self.ins[a].shape, 4 * px + 2 * py + pc)
        return pltpu.make_async_remote_copy(
            src_ref=ins[a] if from_input else dst, dst_ref=dst,
            send_sem=send_sems.at[7 * a + k], recv_sem=recv_sems.at[7 * a + k], device_id=to, device_id_type=MESH)

    def _local(self, refs, a):
        ins, outs, (_, _, local_sems) = refs
        x, y, c = _position()
        dst = _owner_view(outs[a], self.kinds[a], self.ins[a].shape, 4 * x + 2 * y + c)
        return pltpu.make_async_copy(ins[a], dst, local_sems.at[a])

    def start(self, *refs):
        x, y, c = _position()
        me, sibling = (x, y, c), (x, y, 1 - c)
        chips = [(1 - x, y), (x, 1 - y), (1 - x, 1 - y)]
        for a in range(len(self.ins)):
            self._local(refs, a).start()
            for j, chip in enumerate(chips):
                self._copy(refs, a, 1 + j, me, (*chip, c), from_input=True).start()
            self._copy(refs, a, 0, me, sibling, from_input=True).start()

    def mid(self, *refs):
        x, y, c = _position()
        chips = [(1 - x, y), (x, 1 - y), (1 - x, 1 - y)]
        for a in range(len(self.ins)):
            for j, chip in enumerate(chips):
                self._copy(refs, a, 1 + j, (*chip, c), (x, y, c)).wait_recv()
                self._copy(refs, a, 4 + j, (*chip, c), (x, y, 1 - c)).start()

    def finish(self, *refs):
        x, y, c = _position()
        me, sibling = (x, y, c), (x, y, 1 - c)
        chips = [(1 - x, y), (x, 1 - y), (1 - x, 1 - y)]
        for a in range(len(self.ins)):
            self._copy(refs, a, 0, sibling, me).wait_recv()
            for j, chip in enumerate(chips):
                self._copy(refs, a, 4 + j, (*chip, 1 - c), me).wait_recv()
        for a in range(len(self.ins)):
            self._copy(refs, a, 0, me, sibling, from_input=True).wait_send()
            for j, chip in enumerate(chips):
                self._copy(refs, a, 1 + j, me, (*chip, c), from_input=True).wait_send()
                self._copy(refs, a, 4 + j, (*chip, c), sibling).wait_send()
            self._local(refs, a).wait()


class _SiblingComm:
    mid_fraction = None

    def __init__(self, items):
        self.ins = [p for p, _, _ in items]
        self.kinds = [kind for _, kind, _ in items]
        self.shapes = [shape for _, _, shape in items]
        n = len(items)
        self.out_shapes = [jax.ShapeDtypeStruct((4,) + tuple(shape), p.dtype) for p, _, shape in items]
        self.sem_shapes = [pltpu.SemaphoreType.DMA((4 * n,)), pltpu.SemaphoreType.DMA((4 * n,))]
        self.results = None

    def _copies(self, refs):
        ins, outs, (send_sems, recv_sems) = refs
        x, y, c = _position()
        return [pltpu.make_async_remote_copy(
            src_ref=_owner_view(ins[a], self.kinds[a], self.shapes[a], 2 * k + 1 - c), dst_ref=outs[a].at[k],
            send_sem=send_sems.at[4 * a + k], recv_sem=recv_sems.at[4 * a + k],
            device_id=(x, y, 1 - c), device_id_type=MESH) for a in range(len(self.ins)) for k in range(4)]

    def start(self, *refs):
        for cp in self._copies(refs):
            cp.start()

    def finish(self, *refs):
        for cp in self._copies(refs):
            cp.wait()


class _ChipComm:
    mid_fraction = None

    def __init__(self, items):
        self.ins = list(items)
        n = len(items)
        self.out_shapes = [jax.ShapeDtypeStruct(p.shape, p.dtype) for p in items]
        self.sem_shapes = [pltpu.SemaphoreType.DMA((3 * n,)), pltpu.SemaphoreType.DMA((3 * n,)),
                           pltpu.SemaphoreType.DMA((n,))]
        self.results = None

    def _copy(self, refs, a, k, landed):
        ins, outs, (send_sems, recv_sems, _) = refs
        x, y, c = _position()
        px = 1 - x if k & 2 else x
        py = 1 - y if k & 1 else y
        to_chip, my_chip = 2 * px + py, 2 * x + y
        return pltpu.make_async_remote_copy(
            src_ref=ins[a].at[to_chip], dst_ref=outs[a].at[to_chip if landed else my_chip],
            send_sem=send_sems.at[3 * a + k - 1], recv_sem=recv_sems.at[3 * a + k - 1],
            device_id=(px, py, c), device_id_type=MESH)

    def _local(self, refs, a):
        ins, outs, (_, _, local_sems) = refs
        x, y, _ = _position()
        return pltpu.make_async_copy(ins[a].at[2 * x + y], outs[a].at[2 * x + y], local_sems.at[a])

    def start(self, *refs):
        for a in range(len(self.ins)):
            self._local(refs, a).start()
            for k in range(1, 4):
                self._copy(refs, a, k, False).start()

    def finish(self, *refs):
        for a in range(len(self.ins)):
            for k in range(1, 4):
                self._copy(refs, a, k, True).wait_recv()
        for a in range(len(self.ins)):
            for k in range(1, 4):
                self._copy(refs, a, k, False).wait_send()
            self._local(refs, a).wait()


def _run_comm(name, comm):
    n_in, n_out = len(comm.ins), len(comm.out_shapes)

    def body(*refs):
        parts = (refs[:n_in], refs[n_in:n_in + n_out], refs[n_in + n_out:])
        comm.start(*parts)
        if comm.mid_fraction is not None:
            comm.mid(*parts)
        comm.finish(*parts)

    any_spec = pl.BlockSpec(memory_space=pl.ANY)
    return pl.pallas_call(
        body, name=name, out_shape=list(comm.out_shapes), in_specs=[any_spec] * n_in, out_specs=[any_spec] * n_out,
        scratch_shapes=list(comm.sem_shapes))(*comm.ins)


def _carry(comm, body, grid, in_specs, out_specs, out_shapes, scratch_shapes, operands):
    if comm is None:
        return body, in_specs, out_specs, out_shapes, scratch_shapes, operands
    n_in, n_out, n_scr = len(in_specs), len(out_specs), len(scratch_shapes)
    c_in, c_out = len(comm.ins), len(comm.out_shapes)
    total = math.prod(grid)
    mid_step = None if comm.mid_fraction is None else min(total - 1, int(total * comm.mid_fraction))

    def wrapped(*refs):
        ins, c_ins = refs[:n_in], refs[n_in:n_in + c_in]
        outs = refs[n_in + c_in:n_in + c_in + n_out]
        c_outs = refs[n_in + c_in + n_out:n_in + c_in + n_out + c_out]
        scr = refs[n_in + c_in + n_out + c_out:n_in + c_in + n_out + c_out + n_scr]
        parts = (c_ins, c_outs, refs[n_in + c_in + n_out + c_out + n_scr:])
        step = pl.program_id(0)
        for axis in range(1, len(grid)):
            step = step * grid[axis] + pl.program_id(axis)

        @pl.when(step == 0)
        def _():
            comm.start(*parts)

        body(*ins, *outs, *scr)

        if mid_step is not None:
            @pl.when(step == mid_step)
            def _():
                comm.mid(*parts)

        @pl.when(step == total - 1)
        def _():
            comm.finish(*parts)

    any_spec = pl.BlockSpec(memory_space=pl.ANY)
    return (wrapped, list(in_specs) + [any_spec] * c_in, list(out_specs) + [any_spec] * c_out,
            list(out_shapes) + list(comm.out_shapes), list(scratch_shapes) + list(comm.sem_shapes),
            list(operands) + list(comm.ins))


def _pair_add(name, partial, kind, shard_shape, theirs, core):
    r, w = shard_shape
    tb = _tile(r, max(16, (2**19 // w) // 16 * 16), 16)
    nb = r // tb
    if kind == "rows":
        mine = pl.BlockSpec((tb, w), lambda k, i, core_ref: ((2 * k + core_ref[0]) * nb + i, 0))
    elif kind == "cols":
        mine = pl.BlockSpec((tb, w), lambda k, i, core_ref: (i, 2 * k + core_ref[0]))
    else:
        mine = pl.BlockSpec((None, tb, w), lambda k, i, core_ref: (2 * k + core_ref[0], i, 0))
    slot = pl.BlockSpec((None, tb, w), lambda k, i, core_ref: (k, i, 0))

    def body(core_ref, a_ref, b_ref, o_ref):
        o_ref[...] = (a_ref[...].astype(F32) + b_ref[...].astype(F32)).astype(o_ref.dtype)

    return pl.pallas_call(
        body, name=name,
        grid_spec=pltpu.PrefetchScalarGridSpec(num_scalar_prefetch=1, grid=(4, nb), in_specs=[mine, slot],
                                               out_specs=slot),
        out_shape=jax.ShapeDtypeStruct(theirs.shape, theirs.dtype), compiler_params=_params(2))(core, partial, theirs)


def _matmul(name, mode, grid, a, a_spec, bs, b_specs, extras, extra_specs, out_shapes, out_specs, acc_shape,
            epilogue, a_fn=None, comm=None):
    nk, nb, ne, no = grid[2], len(bs), len(extras), len(out_shapes)

    def body(*refs):
        a_ref, b_refs, e_refs = refs[0], refs[1:1 + nb], refs[1 + nb:1 + nb + ne]
        o_refs, acc_refs = refs[1 + nb + ne:1 + nb + ne + no], refs[1 + nb + ne + no:]
        a_val = a_ref[...]
        if a_fn is not None:
            a_val = a_fn(a_val)
        a_val = a_val.astype(BF16)

        def product(b_ref):
            return lax.dot_general(a_val, b_ref[...].astype(BF16), _DN[mode], preferred_element_type=F32)

        def finish(accs):
            outs = epilogue(accs, [e[...] for e in e_refs])
            for o_ref, o in zip(o_refs, outs):
                o_ref[...] = o.astype(o_ref.dtype)

        if nk == 1:
            finish([product(b_ref) for b_ref in b_refs])
        else:
            kk = pl.program_id(2)

            @pl.when(kk == 0)
            def _():
                for acc in acc_refs:
                    acc[...] = jnp.zeros_like(acc)

            for acc, b_ref in zip(acc_refs, b_refs):
                acc[...] += product(b_ref)

            @pl.when(kk == nk - 1)
            def _():
                finish([acc[...] for acc in acc_refs])

    scratch = [] if nk == 1 else [pltpu.VMEM(acc_shape, F32) for _ in range(nb)]
    body, in_specs, out_specs, out_shape, scratch, operands = _carry(
        comm, body, grid, [a_spec] + list(b_specs) + list(extra_specs), list(out_specs), list(out_shapes), scratch,
        [a, *bs, *extras])
    outs = pl.pallas_call(
        body, name=name, grid=grid, in_specs=in_specs, out_specs=out_specs, out_shape=out_shape,
        scratch_shapes=scratch, compiler_params=_params(3))(*operands)
    if comm is not None:
        comm.results = outs[no:]
    return outs[:no]


def _mm_tiles(m, n, k, tiles):
    return _tile(m, tiles[0], 16), _tile(n, tiles[1], LANES), _tile(k, tiles[2], LANES)


_FFN_BEST = {"in": (1024, 512, 2048), "out": (1024, 512, 5632), "out_bwd": (2048, 512, 2048),
             "in_bwd": (1024, 1024, 2816), "out_dw": (512, 2048, 2048), "in_dw": (1024, 512, 4096)}
FFN_TILES = {"l0a": _FFN_BEST, "l0b": _FFN_BEST, "l1a": _FFN_BEST,
             "l1b": dict(_FFN_BEST, in_bwd=(1024, 2048, 1408))}
FULL_K = (1024, 1024, 2048)
LOOP_K = (1024, 1024, 2048)


def _mm_nn(name, a, b, out_dtype, b_col0=0, n=None, tiles=FULL_K):
    m, k = a.shape
    n = b.shape[1] if n is None else n
    tm, tn, tk = _mm_tiles(m, n, k, tiles)
    off = b_col0 // tn
    assert b_col0 % tn == 0
    return _matmul(
        name, "nn", (m // tm, n // tn, k // tk), a, pl.BlockSpec((tm, tk), lambda i, j, kk: (i, kk)),
        [b], [pl.BlockSpec((tk, tn), lambda i, j, kk: (kk, j + off))], [], [],
        [jax.ShapeDtypeStruct((m, n), out_dtype)], [pl.BlockSpec((tm, tn), lambda i, j, kk: (i, j))], (tm, tn),
        lambda accs, ex: accs)[0]


def _mm_bias_rows(name, a, b, bias, a_fn):
    m, k = a.shape
    n = b.shape[1]
    tn = _tile(n, 768, LANES)
    return _matmul(
        name, "nn", (1, n // tn, 1), a, pl.BlockSpec((m, k), lambda i, j, kk: (0, 0)),
        [b], [pl.BlockSpec((k, tn), lambda i, j, kk: (0, j))],
        [bias], [pl.BlockSpec((1, tn), lambda i, j, kk: (0, j))],
        [jax.ShapeDtypeStruct((m, n), F32)], [pl.BlockSpec((m, tn), lambda i, j, kk: (0, j))], (m, tn),
        lambda accs, ex: [accs[0] + ex[0]], a_fn=a_fn)[0]


def _mm_swiglu(name, h, w_in, f, tiles, comm=None):
    s, d = h.shape
    tm, tn, tk = _mm_tiles(s, f, d, tiles)
    nf = f // tn

    def epilogue(accs, ex):
        a_pre, b_pre = accs
        return [jnp.stack([a_pre, b_pre]), _silu(a_pre) * b_pre]

    return _matmul(
        name, "nn", (s // tm, nf, d // tk), h, pl.BlockSpec((tm, tk), lambda i, j, kk: (i, kk)),
        [w_in, w_in], [pl.BlockSpec((tk, tn), lambda i, j, kk: (kk, j)),
                       pl.BlockSpec((tk, tn), lambda i, j, kk: (kk, j + nf))], [], [],
        [jax.ShapeDtypeStruct((2, s, f), F32), jax.ShapeDtypeStruct((s, f), BF16)],
        [pl.BlockSpec((2, tm, tn), lambda i, j, kk: (0, i, j)), pl.BlockSpec((tm, tn), lambda i, j, kk: (i, j))],
        (tm, tn), epilogue, comm=comm)


def _mm_residual(name, a, w, x_in, gvec, tiles=FULL_K, comm=None):
    s, k = a.shape
    d = w.shape[1]
    tm, tn, tk = _mm_tiles(s, d, k, tiles)

    def epilogue(accs, ex):
        x_blk, g_row = ex
        return [accs[0], x_blk + g_row * accs[0]]

    tile = pl.BlockSpec((tm, tn), lambda i, j, kk: (i, j))
    return _matmul(
        name, "nn", (s // tm, d // tn, k // tk), a, pl.BlockSpec((tm, tk), lambda i, j, kk: (i, kk)),
        [w], [pl.BlockSpec((tk, tn), lambda i, j, kk: (kk, j))],
        [x_in, gvec], [tile, pl.BlockSpec((1, tn), lambda i, j, kk: (0, j))],
        [jax.ShapeDtypeStruct((s, d), F32), jax.ShapeDtypeStruct((s, d), F32)], [tile, tile], (tm, tn), epilogue,
        comm=comm)


def _mm_nt(name, a, w, out_dtype, w_col0=0, extra_add=None, tiles=FULL_K):
    s, k = a.shape
    n = w.shape[0]
    tm, tn, tk = _mm_tiles(s, n, k, tiles)
    off = w_col0 // tk
    assert w_col0 % tk == 0
    tile = pl.BlockSpec((tm, tn), lambda i, j, kk: (i, j))
    extras, especs = ([], []) if extra_add is None else ([extra_add], [tile])
    return _matmul(
        name, "nt", (s // tm, n // tn, k // tk), a, pl.BlockSpec((tm, tk), lambda i, j, kk: (i, kk)),
        [w], [pl.BlockSpec((tn, tk), lambda i, j, kk: (j, kk + off))], extras, especs,
        [jax.ShapeDtypeStruct((s, n), out_dtype)], [tile], (tm, tn),
        lambda accs, ex: [accs[0] + ex[0]] if ex else accs)[0]


def _mm_nt_parts(name, a3, w, out_dtype, tiles=LOOP_K, comm=None):
    p, s, k = a3.shape
    n = w.shape[0]
    tm, tn, tk = _mm_tiles(s, n, k, tiles)
    nkp = k // tk
    return _matmul(
        name, "nt", (s // tm, n // tn, p * nkp), a3,
        pl.BlockSpec((None, tm, tk), lambda i, j, kk: (kk // nkp, i, kk % nkp)),
        [w], [pl.BlockSpec((tn, tk), lambda i, j, kk: (j, kk))], [], [],
        [jax.ShapeDtypeStruct((s, n), out_dtype)], [pl.BlockSpec((tm, tn), lambda i, j, kk: (i, j))], (tm, tn),
        lambda accs, ex: accs, comm=comm)[0]


def _mm_nt_swiglu_bwd(name, dy, w_out, ab, tiles):
    s, d = dy.shape
    f = w_out.shape[0]
    tm, tn, tk = _mm_tiles(s, f, d, tiles)

    def epilogue(accs, ex):
        du, a_pre, b_pre = accs[0], ex[0][0], ex[0][1]
        sig = jax.nn.sigmoid(a_pre)
        da = du * b_pre * (sig * (1.0 + a_pre * (1.0 - sig)))
        db = du * (a_pre * sig)
        return [jnp.stack([da, db])]

    pair = pl.BlockSpec((2, tm, tn), lambda i, j, kk: (0, i, j))
    return _matmul(
        name, "nt", (s // tm, f // tn, d // tk), dy, pl.BlockSpec((tm, tk), lambda i, j, kk: (i, kk)),
        [w_out], [pl.BlockSpec((tn, tk), lambda i, j, kk: (j, kk))], [ab], [pair],
        [jax.ShapeDtypeStruct((2, s, f), BF16)], [pair], (tm, tn), epilogue)[0]


def _mm_tn(name, a, b, out_dtype, a_fn=None, tiles=LOOP_K, comm=None):
    s, m = a.shape
    tm, tk = _tile(m, tiles[0], LANES), _tile(s, tiles[2], 16)
    if b.ndim == 3:
        npart = b.shape[2]
        n = b.shape[0] * npart
        tn = _tile(npart, tiles[1], LANES)
        nj = npart // tn
        b_spec = pl.BlockSpec((None, tk, tn), lambda i, j, kk: (j // nj, kk, j % nj))
    else:
        n = b.shape[1]
        tn = _tile(n, tiles[1], LANES)
        b_spec = pl.BlockSpec((tk, tn), lambda i, j, kk: (kk, j))
    return _matmul(
        name, "tn", (m // tm, n // tn, s // tk), a, pl.BlockSpec((tk, tm), lambda i, j, kk: (kk, i)),
        [b], [b_spec], [], [],
        [jax.ShapeDtypeStruct((m, n), out_dtype)], [pl.BlockSpec((tm, tn), lambda i, j, kk: (i, j))], (tm, tn),
        lambda accs, ex: accs, a_fn=a_fn, comm=comm)[0]


def _row_tile(s):
    return _tile(s, 256, 16)


def _fold_rows(v):
    return jnp.sum(v.reshape(v.shape[0] // SUBLANES, SUBLANES, v.shape[1]), axis=0)


def _accumulate(i, acc_ref, val):
    @pl.when(i == 0)
    def _():
        acc_ref[...] = val

    @pl.when(i > 0)
    def _():
        acc_ref[...] += val


def _norm_mod(name, x, g, shift, scale):
    s, d = x.shape
    tb = _row_tile(s)

    def body(x_ref, g_ref, sh_ref, sc_ref, h_ref):
        xv = x_ref[...]
        rstd = lax.rsqrt(jnp.mean(xv * xv, axis=-1, keepdims=True) + EPS)
        h_ref[...] = ((xv * rstd) * g_ref[...] * (1.0 + sc_ref[...]) + sh_ref[...]).astype(BF16)

    row = pl.BlockSpec((tb, d), lambda i: (i, 0))
    vec = pl.BlockSpec((1, d), lambda i: (0, 0))
    return pl.pallas_call(body, name=name, grid=(s // tb,), in_specs=[row, vec, vec, vec], out_specs=row,
                          out_shape=jax.ShapeDtypeStruct((s, d), BF16), compiler_params=_params(1))(x, g, shift, scale)


def _rms_bwd(xv, dn_g, d):
    rstd = lax.rsqrt(jnp.mean(xv * xv, axis=-1, keepdims=True) + EPS)
    xhat = xv * rstd
    dx = rstd * (dn_g - xhat * (jnp.sum(dn_g * xhat, axis=-1, keepdims=True) * (1.0 / d)))
    return xhat, dx


def _norm_mod_bwd(name, dh, x, g, scale, dx_in):
    s, d = x.shape
    tb = _row_tile(s)
    nsteps = s // tb

    def body(dh_ref, x_ref, g_ref, sc_ref, dxin_ref, dx_ref, dsh_ref, dsc_ref, dg_ref, a_sh, a_sc, a_g):
        i = pl.program_id(0)
        dhv, gv = dh_ref[...], g_ref[...]
        dn = dhv * (1.0 + sc_ref[...])
        xhat, dx = _rms_bwd(x_ref[...], dn * gv, d)
        dx_ref[...] = dxin_ref[...] + dx
        _accumulate(i, a_sh, _fold_rows(dhv))
        _accumulate(i, a_sc, _fold_rows(dhv * (xhat * gv)))
        _accumulate(i, a_g, _fold_rows(dn * xhat))

        @pl.when(i == nsteps - 1)
        def _():
            dsh_ref[...] = jnp.sum(a_sh[...], axis=0, keepdims=True)
            dsc_ref[...] = jnp.sum(a_sc[...], axis=0, keepdims=True)
            dg_ref[...] = jnp.sum(a_g[...], axis=0, keepdims=True)

    row = pl.BlockSpec((tb, d), lambda i: (i, 0))
    vec = pl.BlockSpec((1, d), lambda i: (0, 0))
    vshape = jax.ShapeDtypeStruct((1, d), F32)
    return pl.pallas_call(
        body, name=name, grid=(nsteps,), in_specs=[row, row, vec, vec, row], out_specs=[row, vec, vec, vec],
        out_shape=[jax.ShapeDtypeStruct((s, d), F32), vshape, vshape, vshape],
        scratch_shapes=[pltpu.VMEM((SUBLANES, d), F32)] * 3, compiler_params=_params(1))(dh, x, g, scale, dx_in)


def _gate_bwd(name, dx, y, gvec, coef):
    s, d = dx.shape
    tb = _row_tile(s)
    nsteps = s // tb

    def body(dx_ref, y_ref, g_ref, dy_ref, dg_ref, acc):
        i = pl.program_id(0)
        dxv = dx_ref[...]
        dy_ref[...] = (dxv * g_ref[...]).astype(BF16)
        _accumulate(i, acc, _fold_rows(dxv * y_ref[...]))

        @pl.when(i == nsteps - 1)
        def _():
            dg_ref[...] = coef * jnp.sum(acc[...], axis=0, keepdims=True)

    row = pl.BlockSpec((tb, d), lambda i: (i, 0))
    vec = pl.BlockSpec((1, d), lambda i: (0, 0))
    return pl.pallas_call(
        body, name=name, grid=(nsteps,), in_specs=[row, row, vec], out_specs=[row, vec],
        out_shape=[jax.ShapeDtypeStruct((s, d), BF16), jax.ShapeDtypeStruct((1, d), F32)],
        scratch_shapes=[pltpu.VMEM((SUBLANES, d), F32)], compiler_params=_params(1))(dx, y, gvec)


def _final_loss(name, x, target, g):
    s, d = x.shape
    tb = _row_tile(s)
    nsteps = s // tb

    def body(x_ref, t_ref, g_ref, dx_ref, loss_ref, dg_ref, a_loss, a_g):
        i = pl.program_id(0)
        xv, gv = x_ref[...], g_ref[...]
        rstd = lax.rsqrt(jnp.mean(xv * xv, axis=-1, keepdims=True) + EPS)
        xhat = xv * rstd
        err = xhat * gv - t_ref[...]
        dyv = err * (1.0 / d)
        dn_g = dyv * gv
        dx_ref[...] = rstd * (dn_g - xhat * (jnp.sum(dn_g * xhat, axis=-1, keepdims=True) * (1.0 / d)))
        _accumulate(i, a_loss, _fold_rows(err * err))
        _accumulate(i, a_g, _fold_rows(dyv * xhat))

        @pl.when(i == nsteps - 1)
        def _():
            total = jnp.sum(jnp.sum(a_loss[...], axis=0, keepdims=True), axis=1, keepdims=True) * (0.5 / d)
            loss_ref[...] = jnp.broadcast_to(total, (1, LANES))
            dg_ref[...] = jnp.sum(a_g[...], axis=0, keepdims=True)

    row = pl.BlockSpec((tb, d), lambda i: (i, 0))
    vec = pl.BlockSpec((1, d), lambda i: (0, 0))
    return pl.pallas_call(
        body, name=name, grid=(nsteps,), in_specs=[row, row, vec],
        out_specs=[row, pl.BlockSpec((1, LANES), lambda i: (0, 0)), vec],
        out_shape=[jax.ShapeDtypeStruct((s, d), F32), jax.ShapeDtypeStruct((1, LANES), F32),
                   jax.ShapeDtypeStruct((1, d), F32)],
        scratch_shapes=[pltpu.VMEM((SUBLANES, d), F32)] * 2, compiler_params=_params(1))(x, target, g)


def _shift_rows(cur, halo, n):
    rolled = pltpu.roll(cur, n, 0)
    row = lax.broadcasted_iota(jnp.int32, cur.shape, 0)
    for r in range(n):
        rolled = jnp.where(row == r, halo[SUBLANES - n + r:SUBLANES - n + r + 1, :], rolled)
    return rolled


def _shift_rows_up(cur, halo, n):
    tb = cur.shape[0]
    rolled = pltpu.roll(cur, tb - n, 0)
    row = lax.broadcasted_iota(jnp.int32, cur.shape, 0)
    for r in range(n):
        rolled = jnp.where(row == tb - n + r, halo[r:r + 1, :], rolled)
    return rolled


def _conv_specs(s, d, tb, tn):
    nd = d // tn
    hb = tb // SUBLANES
    cur = lambda part: pl.BlockSpec((tb, tn), lambda i, j: (i, j + part * nd))
    prev = lambda part: pl.BlockSpec((SUBLANES, tn), lambda i, j: (jnp.maximum(i * hb - 1, 0), j + part * nd))
    nxt = lambda part: pl.BlockSpec((SUBLANES, tn), lambda i, j: (jnp.minimum((i + 1) * hb, s // SUBLANES - 1),
                                                                  j + part * nd))
    return cur, prev, nxt


def _conv_fwd(name, p, conv_w, conv_b):
    s, d = p.shape[0], p.shape[1] // 3
    tb, tn = _row_tile(s), _tile(d, 512, LANES)
    cur, prev, _ = _conv_specs(s, d, tb, tn)

    def body(bg_ref, cg_ref, xv_ref, cgp_ref, xvp_ref, w_ref, b_ref, z_ref):
        i = pl.program_id(0)
        u = cg_ref[...] * xv_ref[...]
        up = jnp.where(i > 0, cgp_ref[...] * xvp_ref[...], 0.0)
        w = w_ref[...]
        uc = w[0:1, :] * _shift_rows(u, up, 2) + w[1:2, :] * _shift_rows(u, up, 1) + w[2:3, :] * u + b_ref[...]
        z_ref[...] = (bg_ref[...] * uc).astype(BF16)

    return pl.pallas_call(
        body, name=name, grid=(s // tb, d // tn),
        in_specs=[cur(0), cur(1), cur(2), prev(1), prev(2), pl.BlockSpec((3, tn), lambda i, j: (0, j)),
                  pl.BlockSpec((1, tn), lambda i, j: (0, j))],
        out_specs=pl.BlockSpec((tb, tn), lambda i, j: (i, j)),
        out_shape=jax.ShapeDtypeStruct((s, d), BF16), compiler_params=_params(2))(p, p, p, p, p, conv_w, conv_b)


def _conv_bwd(name, dz, p, conv_w, conv_b):
    s, d = dz.shape
    tb, tn = _row_tile(s), _tile(d, 512, LANES)
    nsteps = s // tb
    cur, prev, nxt = _conv_specs(s, d, tb, tn)

    def body(dz_ref, bg_ref, cg_ref, xv_ref, cgp_ref, xvp_ref, dzn_ref, bgn_ref, w_ref, b_ref,
             dp_ref, dw_ref, db_ref, a_w0, a_w1, a_w2, a_b):
        j, i = pl.program_id(0), pl.program_id(1)
        cg, xv, bg, dzv = cg_ref[...], xv_ref[...], bg_ref[...], dz_ref[...]
        u = cg * xv
        up = jnp.where(i > 0, cgp_ref[...] * xvp_ref[...], 0.0)
        w = w_ref[...]
        u1, u2 = _shift_rows(u, up, 1), _shift_rows(u, up, 2)
        uc = w[0:1, :] * u2 + w[1:2, :] * u1 + w[2:3, :] * u + b_ref[...]
        duc = dzv * bg
        ducn = jnp.where(i < nsteps - 1, dzn_ref[...] * bgn_ref[...], 0.0)
        du = w[2:3, :] * duc + w[1:2, :] * _shift_rows_up(duc, ducn, 1) + w[0:1, :] * _shift_rows_up(duc, ducn, 2)
        dp_ref[0] = (dzv * uc).astype(BF16)
        dp_ref[1] = (du * xv).astype(BF16)
        dp_ref[2] = (du * cg).astype(BF16)
        _accumulate(i, a_w0, _fold_rows(duc * u2))
        _accumulate(i, a_w1, _fold_rows(duc * u1))
        _accumulate(i, a_w2, _fold_rows(duc * u))
        _accumulate(i, a_b, _fold_rows(duc))

        @pl.when(i == nsteps - 1)
        def _():
            dw_ref[0:1, :] = jnp.sum(a_w0[...], axis=0, keepdims=True)
            dw_ref[1:2, :] = jnp.sum(a_w1[...], axis=0, keepdims=True)
            dw_ref[2:3, :] = jnp.sum(a_w2[...], axis=0, keepdims=True)
            db_ref[...] = jnp.sum(a_b[...], axis=0, keepdims=True)

    swap = lambda spec: pl.BlockSpec(spec.block_shape, lambda j, i, _m=spec.index_map: _m(i, j))
    return pl.pallas_call(
        body, name=name, grid=(d // tn, nsteps),
        in_specs=[pl.BlockSpec((tb, tn), lambda j, i: (i, j)), swap(cur(0)), swap(cur(1)), swap(cur(2)),
                  swap(prev(1)), swap(prev(2)), swap(nxt(0)), swap(nxt(0)),
                  pl.BlockSpec((3, tn), lambda j, i: (0, j)), pl.BlockSpec((1, tn), lambda j, i: (0, j))],
        out_specs=[pl.BlockSpec((3, tb, tn), lambda j, i: (0, i, j)), pl.BlockSpec((3, tn), lambda j, i: (0, j)),
                   pl.BlockSpec((1, tn), lambda j, i: (0, j))],
        out_shape=[jax.ShapeDtypeStruct((3, s, d), BF16), jax.ShapeDtypeStruct((3, d), F32),
                   jax.ShapeDtypeStruct((1, d), F32)],
        scratch_shapes=[pltpu.VMEM((SUBLANES, tn), F32)] * 4, compiler_params=_params(2),
    )(dz, p, p, p, p, p, dz, p, conv_w, conv_b)


def _block_cumsum(v):
    tb = v.shape[0]
    row = lax.broadcasted_iota(jnp.int32, v.shape, 0)
    sh = 1
    while sh < tb:
        v = v + jnp.where(row >= sh, pltpu.roll(v, sh, 0), 0.0)
        sh *= 2
    return v


def _fgate_fwd(name, zf, b_f):
    s = zf.shape[0]
    tb = _tile(s, 512, SUBLANES)

    def body(z_ref, b_ref, f_ref, ft_ref, carry):
        i = pl.program_id(0)
        z = z_ref[...] + b_ref[...]
        ls = jnp.minimum(z, 0.0) - jnp.log(1.0 + jnp.exp(-jnp.abs(z)))
        run = _block_cumsum(ls)

        @pl.when(i == 0)
        def _():
            carry[...] = jnp.zeros_like(carry)

        out = run + carry[...]
        f_ref[...] = out
        ft_ref[...] = jnp.transpose(out)
        carry[...] = out[tb - 1:tb, :]

    blk = pl.BlockSpec((tb, LANES), lambda i: (i, 0))
    return pl.pallas_call(
        body, name=name, grid=(s // tb,), in_specs=[blk, pl.BlockSpec((1, LANES), lambda i: (0, 0))],
        out_specs=[blk, pl.BlockSpec((LANES, tb), lambda i: (0, i))],
        out_shape=[jax.ShapeDtypeStruct((s, LANES), F32), jax.ShapeDtypeStruct((LANES, s), F32)],
        scratch_shapes=[pltpu.VMEM((1, LANES), F32)], compiler_params=_params(1))(zf, b_f)


def _fgate_bwd(name, dfcum, zf, b_f):
    s = zf.shape[0]
    tb = _tile(s, 512, SUBLANES)
    nsteps = s // tb

    def body(df_ref, z_ref, b_ref, dz_ref, db_ref, carry, acc):
        i = pl.program_id(0)
        df = df_ref[...]
        incl = _block_cumsum(df)
        total = incl[tb - 1:tb, :]

        @pl.when(i == 0)
        def _():
            carry[...] = jnp.zeros_like(carry)

        suffix = total - incl + df + carry[...]
        carry[...] += total
        dzv = suffix * jax.nn.sigmoid(-(z_ref[...] + b_ref[...]))
        dz_ref[...] = dzv
        _accumulate(i, acc, _fold_rows(dzv))

        @pl.when(i == nsteps - 1)
        def _():
            db_ref[...] = jnp.sum(acc[...], axis=0, keepdims=True)

    blk = pl.BlockSpec((tb, LANES), lambda i: (nsteps - 1 - i, 0))
    vec = pl.BlockSpec((1, LANES), lambda i: (0, 0))
    return pl.pallas_call(
        body, name=name, grid=(nsteps,), in_specs=[blk, blk, vec], out_specs=[blk, vec],
        out_shape=[jax.ShapeDtypeStruct((s, LANES), F32), jax.ShapeDtypeStruct((1, LANES), F32)],
        scratch_shapes=[pltpu.VMEM((1, LANES), F32), pltpu.VMEM((SUBLANES, LANES), F32)],
        compiler_params=_params(1))(dfcum, zf, b_f)


def _lane_column(block, lane):
    sel = lax.broadcasted_iota(jnp.int32, block.shape, 1) == lane
    return jnp.sum(jnp.where(sel, block, 0.0), axis=1, keepdims=True)


def _col_to_row(col):
    return jnp.transpose(jnp.broadcast_to(col, (col.shape[0], LANES)))[0:1, :]


def _row_to_col(row):
    return jnp.transpose(jnp.broadcast_to(row, (LANES, row.shape[1])))[:, 0:1]


def _attn_tiles(s):
    t = _tile(s, 512, LANES)
    return t, t


def _attn_fwd(name, q, kv, fcum, fcum_t, n_heads):
    s, d = q.shape
    dh = d // n_heads
    tq, tk = _attn_tiles(s)
    scale = 1.0 / math.sqrt(dh)

    def body(q_ref, k_ref, v_ref, f_ref, ft_ref, o_ref, l_ref):
        h, qi = pl.program_id(0), pl.program_id(1)
        qv = q_ref[...]
        fq = _lane_column(f_ref[...], h)

        def step(kj, carry, diagonal):
            m, l, acc = carry
            start = pl.multiple_of(kj * tk, tk)
            kb = k_ref[pl.ds(start, tk), :]
            vb = v_ref[pl.ds(start, tk), :]
            sc = lax.dot_general(qv, kb, _DN["nt"], preferred_element_type=F32) * scale
            sc = sc + fq - ft_ref[:, pl.ds(start, tk)]
            if diagonal:
                keep = lax.broadcasted_iota(jnp.int32, sc.shape, 0) >= lax.broadcasted_iota(jnp.int32, sc.shape, 1)
                sc = jnp.where(keep, sc, -jnp.inf)
            m_new = jnp.maximum(m, jnp.max(sc, axis=1, keepdims=True))
            alpha = jnp.exp(m - m_new)
            pr = jnp.exp(sc - m_new)
            l = alpha * l + jnp.sum(pr, axis=1, keepdims=True)
            acc = alpha * acc + lax.dot_general(pr.astype(BF16), vb, _DN["nn"], preferred_element_type=F32)
            return m_new, l, acc

        init = (jnp.full((tq, 1), -jnp.inf, F32), jnp.zeros((tq, 1), F32), jnp.zeros((tq, dh), F32))
        carry = lax.fori_loop(0, qi, lambda kj, cr: step(kj, cr, False), init)
        m, l, acc = step(qi, carry, True)
        o_ref[...] = (acc / l).astype(BF16)
        l_ref[...] = _col_to_row(m + jnp.log(l))

    nh = n_heads
    return pl.pallas_call(
        body, name=name, grid=(nh, s // tq),
        in_specs=[pl.BlockSpec((tq, dh), lambda h, i: (i, h)),
                  pl.BlockSpec((s, dh), lambda h, i: (0, h)),
                  pl.BlockSpec((s, dh), lambda h, i: (0, nh + h)),
                  pl.BlockSpec((tq, LANES), lambda h, i: (i, 0)),
                  pl.BlockSpec((None, 1, s), lambda h, i: (h, 0, 0))],
        out_specs=[pl.BlockSpec((tq, dh), lambda h, i: (i, h)),
                   pl.BlockSpec((None, 1, tq), lambda h, i: (h, 0, i))],
        out_shape=[jax.ShapeDtypeStruct((s, d), BF16), jax.ShapeDtypeStruct((nh, 1, s), F32)],
        compiler_params=_params(2))(q, kv, kv, fcum, fcum_t)


def _attn_delta(name, d_o, o, n_heads):
    s, d = d_o.shape
    dh = d // n_heads
    tb = _row_tile(s)

    def body(do_ref, o_ref, dl_ref, dlt_ref, dob_ref):
        dov = do_ref[...]
        prod = dov * o_ref[...].astype(F32)
        lane = lax.broadcasted_iota(jnp.int32, (tb, LANES), 1)
        out = jnp.zeros((tb, LANES), F32)
        for h in range(n_heads):
            out = jnp.where(lane == h, jnp.sum(prod[:, h * dh:(h + 1) * dh], axis=1, keepdims=True), out)
        dl_ref[...] = out
        dlt_ref[...] = jnp.transpose(out)
        dob_ref[...] = dov.astype(BF16)

    row = pl.BlockSpec((tb, d), lambda i: (i, 0))
    return pl.pallas_call(
        body, name=name, grid=(s // tb,), in_specs=[row, row],
        out_specs=[pl.BlockSpec((tb, LANES), lambda i: (i, 0)), pl.BlockSpec((LANES, tb), lambda i: (0, i)), row],
        out_shape=[jax.ShapeDtypeStruct((s, LANES), F32), jax.ShapeDtypeStruct((LANES, s), F32),
                   jax.ShapeDtypeStruct((s, d), BF16)],
        compiler_params=_params(1))(d_o, o)


def _attn_dq(name, q, kv, d_o, fcum, fcum_t, lse_t, delta, n_heads):
    s, d = q.shape
    dh = d // n_heads
    tq, tk = _attn_tiles(s)
    scale = 1.0 / math.sqrt(dh)

    def body(q_ref, k_ref, v_ref, do_ref, f_ref, ft_ref, l_ref, dl_ref, dq_ref, df_ref):
        h, qi = pl.program_id(0), pl.program_id(1)
        qv, dov = q_ref[...], do_ref[...]
        fq = _lane_column(f_ref[...], h)
        lse_col = _row_to_col(l_ref[...])
        dl_col = _lane_column(dl_ref[...], h)

        def step(kj, carry, diagonal):
            acc, dfq = carry
            start = pl.multiple_of(kj * tk, tk)
            kb = k_ref[pl.ds(start, tk), :]
            vb = v_ref[pl.ds(start, tk), :]
            sc = lax.dot_general(qv, kb, _DN["nt"], preferred_element_type=F32) * scale
            sc = sc + fq - ft_ref[:, pl.ds(start, tk)]
            pr = jnp.exp(sc - lse_col)
            if diagonal:
                keep = lax.broadcasted_iota(jnp.int32, sc.shape, 0) >= lax.broadcasted_iota(jnp.int32, sc.shape, 1)
                pr = jnp.where(keep, pr, 0.0)
            dp = lax.dot_general(dov, vb, _DN["nt"], preferred_element_type=F32)
            ds = pr * (dp - dl_col)
            acc = acc + lax.dot_general(ds.astype(BF16), kb, _DN["nn"], preferred_element_type=F32)
            return acc, dfq + jnp.sum(ds, axis=1, keepdims=True)

        init = (jnp.zeros((tq, dh), F32), jnp.zeros((tq, 1), F32))
        carry = lax.fori_loop(0, qi, lambda kj, cr: step(kj, cr, False), init)
        acc, dfq = step(qi, carry, True)
        dq_ref[...] = (acc * scale).astype(BF16)
        df_ref[...] = _col_to_row(dfq)

    nh = n_heads
    qspec = pl.BlockSpec((tq, dh), lambda h, i: (i, h))
    lanes = pl.BlockSpec((tq, LANES), lambda h, i: (i, 0))
    return pl.pallas_call(
        body, name=name, grid=(nh, s // tq),
        in_specs=[qspec, pl.BlockSpec((s, dh), lambda h, i: (0, h)), pl.BlockSpec((s, dh), lambda h, i: (0, nh + h)),
                  qspec, lanes, pl.BlockSpec((None, 1, s), lambda h, i: (h, 0, 0)),
                  pl.BlockSpec((None, 1, tq), lambda h, i: (h, 0, i)), lanes],
        out_specs=[qspec, pl.BlockSpec((None, 1, tq), lambda h, i: (h, 0, i))],
        out_shape=[jax.ShapeDtypeStruct((s, d), BF16), jax.ShapeDtypeStruct((nh, 1, s), F32)],
        compiler_params=_params(2))(q, kv, kv, d_o, fcum, fcum_t, lse_t, delta)


def _attn_dkv(name, q, kv, d_o, fcum, fcum_t, lse_t, delta_t, n_heads):
    s, d = q.shape
    dh = d // n_heads
    tq, tk = _attn_tiles(s)
    nq = s // tq
    scale = 1.0 / math.sqrt(dh)

    def body(q_ref, k_ref, v_ref, do_ref, f_ref, ft_ref, lt_ref, dlt_ref, dkv_ref, df_ref):
        h, kj = pl.program_id(0), pl.program_id(1)
        kb, vb = k_ref[...], v_ref[...]
        fk = _lane_column(f_ref[...], h)

        def step(qi, carry, diagonal):
            dk, dv, dfk = carry
            start = pl.multiple_of(qi * tq, tq)
            qb = q_ref[pl.ds(start, tq), :]
            dob = do_ref[pl.ds(start, tq), :]
            sc = lax.dot_general(kb, qb, _DN["nt"], preferred_element_type=F32) * scale
            sc = sc + ft_ref[:, pl.ds(start, tq)] - fk
            pr = jnp.exp(sc - lt_ref[:, pl.ds(start, tq)])
            if diagonal:
                keep = lax.broadcasted_iota(jnp.int32, sc.shape, 1) >= lax.broadcasted_iota(jnp.int32, sc.shape, 0)
                pr = jnp.where(keep, pr, 0.0)
            dv = dv + lax.dot_general(pr.astype(BF16), dob, _DN["nn"], preferred_element_type=F32)
            dp = lax.dot_general(vb, dob, _DN["nt"], preferred_element_type=F32)
            ds = pr * (dp - dlt_ref[:, pl.ds(start, tq)])
            dk = dk + lax.dot_general(ds.astype(BF16), qb, _DN["nn"], preferred_element_type=F32)
            dfk = dfk - jnp.sum(ds, axis=1, keepdims=True)
            return dk, dv, dfk

        init = (jnp.zeros((tk, dh), F32), jnp.zeros((tk, dh), F32), jnp.zeros((tk, 1), F32))
        carry = step(kj, init, True)
        dk, dv, dfk = lax.fori_loop(kj + 1, nq, lambda qi, cr: step(qi, cr, False), carry)
        dkv_ref[0] = (dk * scale).astype(BF16)
        dkv_ref[1] = dv.astype(BF16)
        df_ref[...] = _col_to_row(dfk)

    nh = n_heads
    full = lambda col0: pl.BlockSpec((s, dh), lambda h, j: (0, col0 + h))
    row_t = pl.BlockSpec((None, 1, s), lambda h, j: (h, 0, 0))
    return pl.pallas_call(
        body, name=name, grid=(nh, s // tk),
        in_specs=[full(0), pl.BlockSpec((tk, dh), lambda h, j: (j, h)), pl.BlockSpec((tk, dh), lambda h, j: (j, nh + h)),
                  full(0), pl.BlockSpec((tk, LANES), lambda h, j: (j, 0)), row_t, row_t, row_t],
        out_specs=[pl.BlockSpec((2, tk, dh), lambda h, j: (0, j, h)),
                   pl.BlockSpec((None, 1, tk), lambda h, j: (h, 0, j))],
        out_shape=[jax.ShapeDtypeStruct((2, s, d), BF16), jax.ShapeDtypeStruct((nh, 1, s), F32)],
        compiler_params=_params(2))(q, kv, kv, d_o, fcum, fcum_t, lse_t, delta_t)


def _sum_slots(name, g):
    n, r, lanes = g.shape

    def body(g_ref, o_ref):
        total = g_ref[0]
        for i in range(1, n):
            total = total + g_ref[i]
        o_ref[...] = total

    return pl.pallas_call(
        body, name=name, out_shape=jax.ShapeDtypeStruct((r, lanes), F32),
        in_specs=[pl.BlockSpec(memory_space=pltpu.VMEM)], out_specs=pl.BlockSpec(memory_space=pltpu.VMEM),
        compiler_params=pltpu.CompilerParams(vmem_limit_bytes=VMEM_LIMIT))(g)


def _adamw(name, parts, w, m, v, row0=0, previous=None):
    npart, r, c = parts.shape
    tb = _tile(math.gcd(r, row0) if row0 else r, max(SUBLANES, (2**18 // c) // SUBLANES * SUBLANES), SUBLANES)
    off = row0 // tb
    c1 = 1.0 - ADAM_B1 ** ADAM_STEP
    c2 = 1.0 - ADAM_B2 ** ADAM_STEP

    def body(p_ref, w_ref, m_ref, v_ref, *rest):
        g_out, d_out, m_out, v_out = rest[-4:]
        g = p_ref[0].astype(F32)
        for i in range(1, npart):
            g = g + p_ref[i].astype(F32)
        m_new = ADAM_B1 * m_ref[...] + (1.0 - ADAM_B1) * g
        v_new = ADAM_B2 * v_ref[...] + (1.0 - ADAM_B2) * (g * g)
        m_hat = m_new / c1
        v_hat = v_new / c2
        g_out[...] = g
        d_out[...] = -ADAM_LR * (m_hat / (jnp.sqrt(v_hat) + ADAM_EPS) + ADAM_WD * w_ref[...])
        m_out[...] = m_new
        v_out[...] = v_new

    blk = pl.BlockSpec((tb, c), lambda i: (i + off, 0))
    shape = jax.ShapeDtypeStruct(w.shape, F32)
    in_specs = [pl.BlockSpec((npart, tb, c), lambda i: (0, i, 0)), blk, blk, blk]
    operands = [parts, w, m, v]
    aliases = {}
    if previous is not None:
        in_specs += [pl.BlockSpec(memory_space=pl.ANY)] * 4
        operands += list(previous)
        aliases = {4 + k: k for k in range(4)}
    return pl.pallas_call(
        body, name=name, grid=(r // tb,), in_specs=in_specs, out_specs=[blk] * 4, out_shape=[shape] * 4,
        input_output_aliases=aliases, compiler_params=_params(1))(*operands)


def _pack(vectors):
    flat = jnp.concatenate([v.reshape(-1).astype(F32) for v in vectors])
    pad = (-flat.shape[0]) % (SUBLANES * LANES)
    return jnp.pad(flat, (0, pad)).reshape(-1, LANES)


def _unpack(flat, shapes):
    out, pos = [], 0
    for shp in shapes:
        size = math.prod(shp)
        out.append(flat[..., pos:pos + size].reshape(flat.shape[:-1] + tuple(shp)))
        pos += size
    return out


def _unshard_last(g, lead):
    nd = len(lead)
    return jnp.moveaxis(g, 0, nd).reshape(tuple(lead) + (-1,))


def _my_slice(full, me, width, axis):
    return lax.dynamic_slice_in_dim(full, me * width, width, axis)


def kernel(x, c, norm_g, w_ada, b_ada, w_ffn_in, w_ffn_out, w_conv_in, conv_w, conv_b, w_conv_out, kv_norm_g, w_ada_kv, b_ada_kv, w_kvf, b_fgate, w_q, w_o, final_g, loss_target, m_norm_g, m_w_ada, m_b_ada, m_w_ffn_in, m_w_ffn_out, m_w_conv_in, m_conv_w, m_conv_b, m_w_conv_out, m_kv_norm_g, m_w_ada_kv, m_b_ada_kv, m_w_kvf, m_b_fgate, m_w_q, m_w_o, m_final_g, v_norm_g, v_w_ada, v_b_ada, v_w_ffn_in, v_w_ffn_out, v_w_conv_in, v_conv_w, v_conv_b, v_w_conv_out, v_kv_norm_g, v_w_ada_kv, v_b_ada_kv, v_w_kvf, v_b_fgate, v_w_q, v_w_o, v_final_g):
    s, d = x.shape[1], x.shape[2]
    n_heads = b_fgate.shape[0]
    n_layers = w_ada.shape[0]
    dsh = d // N_DEV
    f = w_ffn_out.shape[2] * N_DEV
    ada_w = w_ada.shape[2]
    kv_w = w_ada_kv.shape[1]
    kvf_w = w_kvf.shape[1]
    assert n_layers == 2 and w_conv_in.shape[0] == 1 and w_q.shape[0] == 1
    assert (d // n_heads) % LANES == 0 and n_heads <= LANES and N_DEV * kvf_w == 2 * d + n_heads
    me = 4 * lax.axis_index("x") + 2 * lax.axis_index("y") + lax.axis_index("c")
    x0, target = x[0], loss_target[0]

    small = _allgather_small("gather_small_params", _pack([c, norm_g, conv_w, conv_b]))
    c_all, ng_sh, cw_sh, cb_sh = _unpack(small.reshape(N_DEV, -1), [(d,), (n_layers, 3, dsh), (3, dsh), (dsh,)])
    norm_g_full = _unshard_last(ng_sh, (n_layers, 3))
    conv_w_full = _unshard_last(cw_sh, (3,))
    conv_b_full = _unshard_last(cb_sh, ()).reshape(1, d)
    c_rows = jnp.pad(c_all, ((0, 16 - N_DEV), (0, 0)))

    ada_cols = [
        _mm_bias_rows(f"ada_rows_{l}", c_rows, w_ada[l], _my_slice(b_ada[l], me, ada_w, 0).reshape(1, ada_w), _silu)
        for l in range(n_layers)]
    ada_cols.append(_mm_bias_rows("ada_rows_kv", c_rows, w_ada_kv, _my_slice(b_ada_kv, me, kv_w, 0).reshape(1, kv_w),
                                  _silu))
    ada_part = jnp.concatenate([a[:N_DEV] for a in ada_cols], axis=1)
    ada_all = _allgather_small("gather_ada_rows", _pack([ada_part]))
    ada_all = ada_all.reshape(N_DEV, -1)[:, :ada_part.size].reshape(N_DEV, N_DEV, -1)
    ada_mine = lax.dynamic_index_in_dim(ada_all, me, axis=1, keepdims=False)
    ada = [ada_mine[:, l * ada_w:(l + 1) * ada_w].reshape(3, 3, 1, d) for l in range(n_layers)]
    ada_kv = ada_mine[:, n_layers * ada_w:].reshape(2, 1, d)

    def as_item(w, kind):
        return (w.astype(BF16), kind)

    gather_l0a = _GatherComm([as_item(w_ffn_in[0, 0], "cols"), as_item(w_ffn_out[0, 0], "rows")])
    gather_conv = _GatherComm([as_item(w_conv_in[0], "cols"), as_item(w_conv_out[0], "rows"),
                               as_item(w_ffn_out[0, 1], "rows")])
    gather_l0b = _GatherComm([as_item(w_ffn_in[0, 1], "cols")])
    gather_attn = _GatherComm([as_item(w_ffn_out[1, 0], "rows"), as_item(w_q[0], "rows"), as_item(w_kvf, "lead")])
    gather_l1a = _GatherComm([as_item(w_ffn_in[1, 0], "cols")])
    gather_out = _GatherComm([as_item(w_o[0], "rows"), as_item(w_ffn_out[1, 1], "rows")])
    gather_l1b = _GatherComm([as_item(w_ffn_in[1, 1], "cols")])
    wf_in00, wf_out00 = _run_comm("gather_ffn_l0a", gather_l0a)
    b_f = jnp.pad(b_fgate, (0, LANES - n_heads)).reshape(1, LANES)

    def ffn_fwd(tag, xs, l, sub, w_in, w_out, comm_in, comm_out):
        shift, scale, gate = ada[l][sub]
        tiles = FFN_TILES[tag]
        h = _norm_mod(f"norm_{tag}", xs, norm_g_full[l, sub].reshape(1, d), shift, scale)
        ab, u = _mm_swiglu(f"ffn_in_{tag}", h, w_in, f, tiles["in"], comm=comm_in)
        y, x_next = _mm_residual(f"ffn_out_{tag}", u, w_out, xs, 0.5 * gate, tiles["out"], comm=comm_out)
        return x_next, (xs, h, ab, u, y, w_in, w_out)

    x1, save_f0 = ffn_fwd("l0a", x0, 0, 0, wf_in00, wf_out00, gather_conv, gather_l0b)
    wc_in, wc_out, wf_out01 = gather_conv.results
    (wf_in01,) = gather_l0b.results
    shift, scale, gate = ada[0][1]
    h_c = _norm_mod("norm_conv", x1, norm_g_full[0, 1].reshape(1, d), shift, scale)
    p_c = _mm_nn("conv_in", h_c, wc_in, F32)
    z_c = _conv_fwd("conv_mix", p_c, conv_w_full, conv_b_full)
    y_c, x2 = _mm_residual("conv_out", z_c, wc_out, x1, gate)
    x3, save_f1 = ffn_fwd("l0b", x2, 0, 2, wf_in01, wf_out01, gather_attn, gather_l1a)
    wf_out10, wq, g_kvf = gather_attn.results
    (wf_in10,) = gather_l1a.results
    wkvf = _unshard_last(g_kvf, (d,))
    wkv = wkvf[:, :2 * d]
    wf_gate = jnp.pad(wkvf[:, 2 * d:], ((0, 0), (0, LANES - n_heads)))
    x4, save_f2 = ffn_fwd("l1a", x3, 1, 0, wf_in10, wf_out10, gather_out, gather_l1b)
    wo, wf_out11 = gather_out.results
    (wf_in11,) = gather_l1b.results
    shift, scale, gate_a = ada[1][1]
    h_q = _norm_mod("norm_q", x4, norm_g_full[1, 1].reshape(1, d), shift, scale)
    h_kv = _norm_mod("norm_kv", x4, kv_norm_g.reshape(1, d), ada_kv[0], ada_kv[1])
    q = _mm_nn("attn_q", h_q, wq, BF16)
    kv = _mm_nn("attn_kv", h_kv, wkv, BF16)
    zf = _mm_nn("attn_fgate", h_kv, wf_gate, F32)
    fcum, fcum_lanes = _fgate_fwd("fgate_cumsum", zf, b_f)
    fcum_t = fcum_lanes[:n_heads].reshape(n_heads, 1, s)
    o, lse_t = _attn_fwd("attn_fwd", q, kv, fcum, fcum_t, n_heads)
    y_a, x5 = _mm_residual("attn_out", o, wo, x4, gate_a)
    x6, save_f3 = ffn_fwd("l1b", x5, 1, 2, wf_in11, wf_out11, None, None)
    dx, loss_part, d_final_g = _final_loss("final_loss", x6, target, final_g.reshape(1, d))

    d_ada = [[[None] * 3 for _ in range(3)] for _ in range(n_layers)]
    d_norm_g = [[None] * 3 for _ in range(n_layers)]
    core = lax.axis_index("c").astype(jnp.int32).reshape(1)
    in_shard, out_shard, row_shard = (d, 2 * f // N_DEV), (f // N_DEV, d), (dsh, d)

    def reduce_within_chip(tag, items):
        sibling = _SiblingComm(items)
        theirs = _run_comm(f"reduce_sibling_{tag}", sibling)
        return _ChipComm([_pair_add(f"add_sibling_{tag}_{n}", p, kind, shape, t, core)
                          for n, ((p, kind, shape), t) in enumerate(zip(items, theirs))])

    def ffn_bwd(tag, dxs, saved, l, sub, comm_bwd, comm_dw):
        xs, h, ab, u, y, w_in, w_out = saved
        _, scale, gate = ada[l][sub]
        tiles = FFN_TILES[tag]
        dy, d_ada[l][sub][2] = _gate_bwd(f"gate_bwd_{tag}", dxs, y, 0.5 * gate, 0.5)
        dab = _mm_nt_swiglu_bwd(f"ffn_out_bwd_{tag}", dy, w_out, ab, tiles["out_bwd"])
        dw_out = _mm_tn(f"ffn_out_dw_{tag}", u, dy, BF16, tiles=tiles["out_dw"])
        dh = _mm_nt_parts(f"ffn_in_bwd_{tag}", dab, w_in, F32, tiles["in_bwd"], comm=comm_bwd)
        dw_in = _mm_tn(f"ffn_in_dw_{tag}", h, dab, BF16, tiles=tiles["in_dw"], comm=comm_dw)
        dxs, d_ada[l][sub][0], d_ada[l][sub][1], d_norm_g[l][sub] = _norm_mod_bwd(
            f"norm_bwd_{tag}", dh, xs, norm_g_full[l, sub].reshape(1, d), scale, dxs)
        return dxs, reduce_within_chip(tag, [(dw_in, "cols", in_shard), (dw_out, "rows", out_shard)])

    dx, scatter_l1b = ffn_bwd("l1b", dx, save_f3, 1, 2, None, None)
    dy, d_ada[1][1][2] = _gate_bwd("gate_bwd_attn", dx, y_a, gate_a, 1.0)
    d_o = _mm_nt("attn_out_bwd", dy, wo, F32)
    dwo = _mm_tn("attn_out_dw", o, dy, BF16)
    delta, delta_lanes, d_ob = _attn_delta("attn_delta", d_o, o, n_heads)
    delta_t = delta_lanes[:n_heads].reshape(n_heads, 1, s)
    dq, dfq = _attn_dq("attn_dq", q, kv, d_ob, fcum, fcum_t, lse_t, delta, n_heads)
    dkv, dfk = _attn_dkv("attn_dkv", q, kv, d_ob, fcum, fcum_t, lse_t, delta_t, n_heads)
    dfcum = jnp.pad((dfq + dfk).reshape(n_heads, s).T, ((0, 0), (0, LANES - n_heads)))
    dzf, d_bf = _fgate_bwd("fgate_bwd", dfcum, zf, b_f)
    dh_q = _mm_nt("attn_q_bwd", dq, wq, F32)
    dwq = _mm_tn("attn_q_dw", h_q, dq, BF16)
    dh_kv = _mm_nt_parts("attn_kv_bwd", dkv, wkv, F32)
    dh_kv = _mm_nt("attn_fgate_bwd", dzf, wf_gate, F32, extra_add=dh_kv)
    dwkv = _mm_tn("attn_kv_dw", h_kv, dkv, BF16)
    dwf = _mm_tn("attn_fgate_dw", h_kv, dzf, BF16)
    dwkvf = jnp.concatenate([dwkv, dwf[:, :n_heads]], axis=1)
    p_kvf = jnp.moveaxis(dwkvf.reshape(d, N_DEV, kvf_w), 1, 0)
    scatter_attn = reduce_within_chip(
        "attn", [(dwo, "rows", row_shard), (dwq, "rows", row_shard), (p_kvf, "lead", (d, kvf_w))])
    dx, d_ada[1][1][0], d_ada[1][1][1], d_norm_g[1][1] = _norm_mod_bwd(
        "norm_bwd_q", dh_q, x4, norm_g_full[1, 1].reshape(1, d), ada[1][1][1], dx)
    dx, d_kv_shift, d_kv_scale, d_kv_norm_g = _norm_mod_bwd(
        "norm_bwd_kv", dh_kv, x4, kv_norm_g.reshape(1, d), ada_kv[1], dx)
    dx, scatter_l1a = ffn_bwd("l1a", dx, save_f2, 1, 0, scatter_l1b, scatter_attn)
    dx, scatter_l0b = ffn_bwd("l0b", dx, save_f1, 0, 2, scatter_l1a, None)
    dy, d_ada[0][1][2] = _gate_bwd("gate_bwd_conv", dx, y_c, ada[0][1][2], 1.0)
    dz = _mm_nt("conv_out_bwd", dy, wc_out, F32)
    dwc_out = _mm_tn("conv_out_dw", z_c, dy, BF16)
    dp, d_conv_w, d_conv_b = _conv_bwd("conv_mix_bwd", dz, p_c, conv_w_full, conv_b_full)
    dh_c = _mm_nt_parts("conv_in_bwd", dp, wc_in, F32)
    dwc_in = _mm_tn("conv_in_dw", h_c, dp, BF16)
    scatter_conv = reduce_within_chip("conv", [(dwc_out, "rows", row_shard), (dwc_in, "cols", (d, 3 * d // N_DEV))])
    dx, d_ada[0][1][0], d_ada[0][1][1], d_norm_g[0][1] = _norm_mod_bwd(
        "norm_bwd_conv", dh_c, x1, norm_g_full[0, 1].reshape(1, d), ada[0][1][1], dx)
    dx, scatter_l0a = ffn_bwd("l0a", dx, save_f0, 0, 0, scatter_l0b, scatter_conv)
    scatter_l0a.results = _run_comm("scatter_ffn_l0a", scatter_l0a)
    grad_x = dx.reshape(1, s, d)

    d_ada_flat = jnp.concatenate([v.reshape(-1) for l in range(n_layers) for sub in range(3) for v in d_ada[l][sub]])
    d_norm_flat = jnp.concatenate([d_norm_g[l][sub].reshape(-1) for l in range(n_layers) for sub in range(3)])
    small_shapes = [(n_layers * 9 * d,), (2 * d,), (n_layers * 3 * d,), (3, d), (d,), (d,), (LANES,), (d,), (LANES,)]
    small_parts = _pack([d_ada_flat, d_kv_shift, d_kv_scale, d_norm_flat, d_conv_w, d_conv_b, d_kv_norm_g, d_bf,
                         d_final_g, loss_part])
    small_all = _allgather_small("gather_small_grads", small_parts)
    small_sum = _sum_slots("sum_small_grads", small_all).reshape(-1)
    g_b_ada, g_b_ada_kv, g_norm_full, g_conv_w_full, g_conv_b_full, g_kv_norm, g_bf, g_final, loss_v = _unpack(
        small_sum, small_shapes)
    loss = loss_v[0]
    d_ada_rows = small_all.reshape(N_DEV, -1)[:, :n_layers * 9 * d + 2 * d]
    d_ada_rows = jnp.pad(d_ada_rows, ((0, 16 - N_DEV), (0, 0)))

    results = {}

    def update(key, parts, w, m, v):
        shape = w.shape
        c_dim = shape[-1]
        outs = _adamw(f"adamw_{key}", parts.reshape(parts.shape[0], -1, c_dim), w.reshape(-1, c_dim),
                      m.reshape(-1, c_dim), v.reshape(-1, c_dim))
        results[key] = [o.reshape(shape) for o in outs]

    g_w_ada = []
    for l in range(n_layers):
        cols = _my_slice(d_ada_rows[:, l * 9 * d:(l + 1) * 9 * d], me, ada_w, 1)
        g_w_ada.append(_mm_tn(f"ada_dw_{l}", c_rows, cols, F32, a_fn=_silu))
    update("w_ada", jnp.stack(g_w_ada).reshape(1, n_layers * d, ada_w), w_ada, m_w_ada, v_w_ada)
    cols = _my_slice(d_ada_rows[:, n_layers * 9 * d:], me, kv_w, 1)
    update("w_ada_kv", _mm_tn("ada_dw_kv", c_rows, cols, F32, a_fn=_silu).reshape(1, d, kv_w),
           w_ada_kv, m_w_ada_kv, v_w_ada_kv)

    def update_rows(key, parts_list, w, m, v):
        c_dim = w.shape[-1]
        flat = [t.reshape(-1, c_dim) for t in (w, m, v)]
        outs = None
        for n, parts in enumerate(parts_list):
            outs = _adamw(f"adamw_{key}_{n}", parts, *flat, row0=n * parts.shape[1], previous=outs)
        results[key] = [o.reshape(w.shape) for o in outs]

    ffn_scatters = [scatter_l0a, scatter_l0b, scatter_l1a, scatter_l1b]
    update_rows("w_ffn_in", [sc.results[0] for sc in ffn_scatters], w_ffn_in, m_w_ffn_in, v_w_ffn_in)
    update_rows("w_ffn_out", [sc.results[1] for sc in ffn_scatters], w_ffn_out, m_w_ffn_out, v_w_ffn_out)
    r_co, r_ci = scatter_conv.results
    r_o, r_q, r_kvf = scatter_attn.results
    update_rows("w_conv_in", [r_ci], w_conv_in, m_w_conv_in, v_w_conv_in)
    update_rows("w_conv_out", [r_co], w_conv_out, m_w_conv_out, v_w_conv_out)
    update_rows("w_kvf", [r_kvf], w_kvf, m_w_kvf, v_w_kvf)
    update_rows("w_q", [r_q], w_q, m_w_q, v_w_q)
    update_rows("w_o", [r_o], w_o, m_w_o, v_w_o)

    small_keys = ["norm_g", "b_ada", "conv_w", "conv_b", "kv_norm_g", "b_ada_kv", "b_fgate", "final_g"]
    small_w = [norm_g, b_ada, conv_w, conv_b, kv_norm_g, b_ada_kv, b_fgate, final_g]
    small_m = [m_norm_g, m_b_ada, m_conv_w, m_conv_b, m_kv_norm_g, m_b_ada_kv, m_b_fgate, m_final_g]
    small_v = [v_norm_g, v_b_ada, v_conv_w, v_conv_b, v_kv_norm_g, v_b_ada_kv, v_b_fgate, v_final_g]
    small_g = [
        _my_slice(g_norm_full.reshape(n_layers, 3, d), me, dsh, 2), g_b_ada.reshape(b_ada.shape),
        _my_slice(g_conv_w_full, me, dsh, 1).reshape(conv_w.shape), _my_slice(g_conv_b_full, me, dsh, 0).reshape(
            conv_b.shape), g_kv_norm, g_b_ada_kv, g_bf[:n_heads], g_final]
    packed = _adamw("adamw_small", _pack(small_g)[None], _pack(small_w), _pack(small_m), _pack(small_v))
    for vals, idx in zip(packed, range(4)):
        for key, val in zip(small_keys, _unpack(vals.reshape(-1), [w.shape for w in small_w])):
            results.setdefault(key, [None] * 4)[idx] = val

    order = ["norm_g", "w_ada", "b_ada", "w_ffn_in", "w_ffn_out", "w_conv_in", "conv_w", "conv_b", "w_conv_out",
             "kv_norm_g", "w_ada_kv", "b_ada_kv", "w_kvf", "b_fgate", "w_q", "w_o", "final_g"]
    return (loss, grad_x, *[results[k][0] for k in order], *[results[k][1] for k in order],
            *[results[k][2] for k in order], *[results[k][3] for k in order])
```

```python
import functools
import math

import jax
import jax.numpy as jnp
from jax import lax
from jax.experimental import pallas as pl
from jax.experimental.pallas import tpu as pltpu

F32 = jnp.float32
BF16 = jnp.bfloat16
MESH = pl.DeviceIdType.MESH

N_DEV = 8
LANES = 128
SUBLANES = 8
VMEM_LIMIT = 56 * 2**20
EPS = 1e-6
LOG2E = math.log2(math.e)
ADAM_LR, ADAM_B1, ADAM_B2, ADAM_EPS, ADAM_WD, ADAM_STEP = 0.001, 0.9, 0.999, 1e-08, 0.01, 10

_DN = {"nn": (((1,), (0,)), ((), ())), "nt": (((1,), (1,)), ((), ())), "tn": (((0,), (0,)), ((), ()))}


def _tile(n, pref, align):
    t = min(pref, n)
    t -= t % align
    while t >= align:
        if n % t == 0:
            return t
        t -= align
    return n


def _params(n_grid):
    return pltpu.CompilerParams(vmem_limit_bytes=VMEM_LIMIT, dimension_semantics=("arbitrary",) * n_grid)


def _silu(v):
    return v * jax.nn.sigmoid(v)


def _position():
    x, y, c = lax.axis_index("x"), lax.axis_index("y"), lax.axis_index("c")
    return x, y, c


def _allgather_small(name, v):
    def body(v_ref, out_ref, send_sems, recv_sems, local_sem):
        x, y, c = _position()
        me, sibling = (x, y, c), (x, y, 1 - c)
        chips = [(1 - x, y), (x, 1 - y), (1 - x, 1 - y)]

        def slot(px, py, pc):
            return out_ref.at[4 * px + 2 * py + pc]

        def copy(k, block, to, src=None):
            return pltpu.make_async_remote_copy(
                src_ref=slot(*block) if src is None else src, dst_ref=slot(*block),
                send_sem=send_sems.at[k], recv_sem=recv_sems.at[k], device_id=to, device_id_type=MESH)

        mine = pltpu.make_async_copy(v_ref, slot(*me), local_sem)
        mine.start()
        first = [copy(0, me, sibling, src=v_ref)]
        first += [copy(1 + j, me, (*chip, c), src=v_ref) for j, chip in enumerate(chips)]
        for cp in first:
            cp.start()
        passed = [copy(4 + j, (*chip, c), sibling) for j, chip in enumerate(chips)]
        for j, chip in enumerate(chips):
            copy(1 + j, (*chip, c), me).wait_recv()
            passed[j].start()
        copy(0, sibling, me).wait_recv()
        for j, chip in enumerate(chips):
            copy(4 + j, (*chip, 1 - c), me).wait_recv()
        for cp in first + passed:
            cp.wait_send()
        mine.wait()

    return pl.pallas_call(
        body, name=name,
        out_shape=jax.ShapeDtypeStruct((N_DEV,) + v.shape, v.dtype),
        in_specs=[pl.BlockSpec(memory_space=pltpu.VMEM)],
        out_specs=pl.BlockSpec(memory_space=pltpu.VMEM),
        scratch_shapes=[pltpu.SemaphoreType.DMA((7,)), pltpu.SemaphoreType.DMA((7,)), pltpu.SemaphoreType.DMA],
        compiler_params=pltpu.CompilerParams(vmem_limit_bytes=VMEM_LIMIT),
    )(v)


def _owner_view(ref, kind, shard_shape, owner):
    r, w = shard_shape
    if kind == "rows":
        return ref.at[pl.ds(pl.multiple_of(owner * r, 16), r)]
    if kind == "cols":
        return ref.at[:, pl.ds(pl.multiple_of(owner * w, LANES), w)]
    return ref.at[owner]


def _full_shape(kind, shard_shape):
    r, w = shard_shape
    return {"rows": (N_DEV * r, w), "cols": (r, N_DEV * w), "lead": (N_DEV, r, w)}[kind]


class _GatherComm:
    mid_fraction = 0.85

    def __init__(self, items):
        self.ins = [shard for shard, _ in items]
        self.kinds = [kind for _, kind in items]
        n = len(items)
        self.out_shapes = [jax.ShapeDtypeStruct(_full_shape(k, a.shape), a.dtype) for a, k in items]
        self.sem_shapes = [pltpu.SemaphoreType.DMA((7 * n,)), pltpu.SemaphoreType.DMA((7 * n,)),
                           pltpu.SemaphoreType.DMA((n,))]
        self.results = None

    def _copy(self, refs, a, k, block, to, from_input=False):
        ins, outs, (send_sems, recv_sems, _) = refs
        px, py, pc = block
        dst = _owner_view(outs[a], self.kinds[a], self.ins[a].shape, 4 * px + 2 * py + pc)
        return pltpu.make_async_remote_copy(
            src_ref=ins[a] if from_input else dst, dst_ref=dst,
            send_sem=send_sems.at[7 * a + k], recv_sem=recv_sems.at[7 * a + k], device_id=to, device_id_type=MESH)

    def _local(self, refs, a):
        ins, outs, (_, _, local_sems) = refs
        x, y, c = _position()
        dst = _owner_view(outs[a], self.kinds[a], self.ins[a].shape, 4 * x + 2 * y + c)
        return pltpu.make_async_copy(ins[a], dst, local_sems.at[a])

    def start(self, *refs):
        x, y, c = _position()
        me, sibling = (x, y, c), (x, y, 1 - c)
        chips = [(1 - x, y), (x, 1 - y), (1 - x, 1 - y)]
        for a in range(len(self.ins)):
            self._local(refs, a).start()
            for j, chip in enumerate(chips):
                self._copy(refs, a, 1 + j, me, (*chip, c), from_input=True).start()
            self._copy(refs, a, 0, me, sibling, from_input=True).start()

    def mid(self, *refs):
        x, y, c = _position()
        chips = [(1 - x, y), (x, 1 - y), (1 - x, 1 - y)]
        for a in range(len(self.ins)):
            for j, chip in enumerate(chips):
                self._copy(refs, a, 1 + j, (*chip, c), (x, y, c)).wait_recv()
                self._copy(refs, a, 4 + j, (*chip, c), (x, y, 1 - c)).start()

    def finish(self, *refs):
        x, y, c = _position()
        me, sibling = (x, y, c), (x, y, 1 - c)
        chips = [(1 - x, y), (x, 1 - y), (1 - x, 1 - y)]
        for a in range(len(self.ins)):
            self._copy(refs, a, 0, sibling, me).wait_recv()
            for j, chip in enumerate(chips):
                self._copy(refs, a, 4 + j, (*chip, 1 - c), me).wait_recv()
        for a in range(len(self.ins)):
            self._copy(refs, a, 0, me, sibling, from_input=True).wait_send()
            for j, chip in enumerate(chips):
                self._copy(refs, a, 1 + j, me, (*chip, c), from_input=True).wait_send()
                self._copy(refs, a, 4 + j, (*chip, c), sibling).wait_send()
            self._local(refs, a).wait()


class _SiblingComm:
    mid_fraction = None

    def __init__(self, items):
        self.ins = [p for p, _, _ in items]
        self.kinds = [kind for _, kind, _ in items]
        self.shapes = [shape for _, _, shape in items]
        n = len(items)
        self.out_shapes = [jax.ShapeDtypeStruct((4,) + tuple(shape), p.dtype) for p, _, shape in items]
        self.sem_shapes = [pltpu.SemaphoreType.DMA((4 * n,)), pltpu.SemaphoreType.DMA((4 * n,))]
        self.results = None

    def _copies(self, refs):
        ins, outs, (send_sems, recv_sems) = refs
        x, y, c = _position()
        return [pltpu.make_async_remote_copy(
            src_ref=_owner_view(ins[a], self.kinds[a], self.shapes[a], 2 * k + 1 - c), dst_ref=outs[a].at[k],
            send_sem=send_sems.at[4 * a + k], recv_sem=recv_sems.at[4 * a + k],
            device_id=(x, y, 1 - c), device_id_type=MESH) for a in range(len(self.ins)) for k in range(4)]

    def start(self, *refs):
        for cp in self._copies(refs):
            cp.start()

    def finish(self, *refs):
        for cp in self._copies(refs):
            cp.wait()


class _ChipComm:
    mid_fraction = None

    def __init__(self, items):
        self.ins = list(items)
        n = len(items)
        self.out_shapes = [jax.ShapeDtypeStruct(p.shape, p.dtype) for p in items]
        self.sem_shapes = [pltpu.SemaphoreType.DMA((3 * n,)), pltpu.SemaphoreType.DMA((3 * n,)),
                           pltpu.SemaphoreType.DMA((n,))]
        self.results = None

    def _copy(self, refs, a, k, landed):
        ins, outs, (send_sems, recv_sems, _) = refs
        x, y, c = _position()
        px = 1 - x if k & 2 else x
        py = 1 - y if k & 1 else y
        to_chip, my_chip = 2 * px + py, 2 * x + y
        return pltpu.make_async_remote_copy(
            src_ref=ins[a].at[to_chip], dst_ref=outs[a].at[to_chip if landed else my_chip],
            send_sem=send_sems.at[3 * a + k - 1], recv_sem=recv_sems.at[3 * a + k - 1],
            device_id=(px, py, c), device_id_type=MESH)

    def _local(self, refs, a):
        ins, outs, (_, _, local_sems) = refs
        x, y, _ = _position()
        return pltpu.make_async_copy(ins[a].at[2 * x + y], outs[a].at[2 * x + y], local_sems.at[a])

    def start(self, *refs):
        for a in range(len(self.ins)):
            self._local(refs, a).start()
            for k in range(1, 4):
                self._copy(refs, a, k, False).start()

    def finish(self, *refs):
        for a in range(len(self.ins)):
            for k in range(1, 4):
                self._copy(refs, a, k, True).wait_recv()
        for a in range(len(self.ins)):
            for k in range(1, 4):
                self._copy(refs, a, k, False).wait_send()
            self._local(refs, a).wait()


def _run_comm(name, comm):
    n_in, n_out = len(comm.ins), len(comm.out_shapes)

    def body(*refs):
        parts = (refs[:n_in], refs[n_in:n_in + n_out], refs[n_in + n_out:])
        comm.start(*parts)
        if comm.mid_fraction is not None:
            comm.mid(*parts)
        comm.finish(*parts)

    any_spec = pl.BlockSpec(memory_space=pl.ANY)
    return pl.pallas_call(
        body, name=name, out_shape=list(comm.out_shapes), in_specs=[any_spec] * n_in, out_specs=[any_spec] * n_out,
        scratch_shapes=list(comm.sem_shapes))(*comm.ins)


def _carry(comm, body, grid, in_specs, out_specs, out_shapes, scratch_shapes, operands):
    if comm is None:
        return body, in_specs, out_specs, out_shapes, scratch_shapes, operands
    n_in, n_out, n_scr = len(in_specs), len(out_specs), len(scratch_shapes)
    c_in, c_out = len(comm.ins), len(comm.out_shapes)
    total = math.prod(grid)
    mid_step = None if comm.mid_fraction is None else min(total - 1, int(total * comm.mid_fraction))

    def wrapped(*refs):
        ins, c_ins = refs[:n_in], refs[n_in:n_in + c_in]
        outs = refs[n_in + c_in:n_in + c_in + n_out]
        c_outs = refs[n_in + c_in + n_out:n_in + c_in + n_out + c_out]
        scr = refs[n_in + c_in + n_out + c_out:n_in + c_in + n_out + c_out + n_scr]
        parts = (c_ins, c_outs, refs[n_in + c_in + n_out + c_out + n_scr:])
        step = pl.program_id(0)
        for axis in range(1, len(grid)):
            step = step * grid[axis] + pl.program_id(axis)

        @pl.when(step == 0)
        def _():
            comm.start(*parts)

        body(*ins, *outs, *scr)

        if mid_step is not None:
            @pl.when(step == mid_step)
            def _():
                comm.mid(*parts)

        @pl.when(step == total - 1)
        def _():
            comm.finish(*parts)

    any_spec = pl.BlockSpec(memory_space=pl.ANY)
    return (wrapped, list(in_specs) + [any_spec] * c_in, list(out_specs) + [any_spec] * c_out,
            list(out_shapes) + list(comm.out_shapes), list(scratch_shapes) + list(comm.sem_shapes),
            list(operands) + list(comm.ins))


def _pair_add(name, partial, kind, shard_shape, theirs, core):
    r, w = shard_shape
    tb = _tile(r, max(16, (2**19 // w) // 16 * 16), 16)
    nb = r // tb
    if kind == "rows":
        mine = pl.BlockSpec((tb, w), lambda k, i, core_ref: ((2 * k + core_ref[0]) * nb + i, 0))
    elif kind == "cols":
        mine = pl.BlockSpec((tb, w), lambda k, i, core_ref: (i, 2 * k + core_ref[0]))
    else:
        mine = pl.BlockSpec((None, tb, w), lambda k, i, core_ref: (2 * k + core_ref[0], i, 0))
    slot = pl.BlockSpec((None, tb, w), lambda k, i, core_ref: (k, i, 0))

    def body(core_ref, a_ref, b_ref, o_ref):
        o_ref[...] = (a_ref[...].astype(F32) + b_ref[...].astype(F32)).astype(o_ref.dtype)

    return pl.pallas_call(
        body, name=name,
        grid_spec=pltpu.PrefetchScalarGridSpec(num_scalar_prefetch=1, grid=(4, nb), in_specs=[mine, slot],
                                               out_specs=slot),
        out_shape=jax.ShapeDtypeStruct(theirs.shape, theirs.dtype), compiler_params=_params(2))(core, partial, theirs)


def _matmul(name, mode, grid, a, a_spec, bs, b_specs, extras, extra_specs, out_shapes, out_specs, acc_shape,
            epilogue, a_fn=None, comm=None):
    nk, nb, ne, no = grid[2], len(bs), len(extras), len(out_shapes)

    def body(*refs):
        a_ref, b_refs, e_refs = refs[0], refs[1:1 + nb], refs[1 + nb:1 + nb + ne]
        o_refs, acc_refs = refs[1 + nb + ne:1 + nb + ne + no], refs[1 + nb + ne + no:]
        a_val = a_ref[...]
        if a_fn is not None:
            a_val = a_fn(a_val)
        a_val = a_val.astype(BF16)

        def product(b_ref):
            return lax.dot_general(a_val, b_ref[...].astype(BF16), _DN[mode], preferred_element_type=F32)

        def finish(accs):
            outs = epilogue(accs, [e[...] for e in e_refs])
            for o_ref, o in zip(o_refs, outs):
                o_ref[...] = o.astype(o_ref.dtype)

        if nk == 1:
            finish([product(b_ref) for b_ref in b_refs])
        else:
            kk = pl.program_id(2)

            @pl.when(kk == 0)
            def _():
                for acc in acc_refs:
                    acc[...] = jnp.zeros_like(acc)

            for acc, b_ref in zip(acc_refs, b_refs):
                acc[...] += product(b_ref)

            @pl.when(kk == nk - 1)
            def _():
                finish([acc[...] for acc in acc_refs])

    scratch = [] if nk == 1 else [pltpu.VMEM(acc_shape, F32) for _ in range(nb)]
    body, in_specs, out_specs, out_shape, scratch, operands = _carry(
        comm, body, grid, [a_spec] + list(b_specs) + list(extra_specs), list(out_specs), list(out_shapes), scratch,
        [a, *bs, *extras])
    outs = pl.pallas_call(
        body, name=name, grid=grid, in_specs=in_specs, out_specs=out_specs, out_shape=out_shape,
        scratch_shapes=scratch, compiler_params=_params(3))(*operands)
    if comm is not None:
        comm.results = outs[no:]
    return outs[:no]


def _mm_tiles(m, n, k, tiles):
    return _tile(m, tiles[0], 16), _tile(n, tiles[1], LANES), _tile(k, tiles[2], LANES)


_FFN_BEST = {"in": (1024, 512, 2048), "out": (1024, 512, 5632), "out_bwd": (2048, 512, 2048),
             "in_bwd": (1024, 1024, 2816), "out_dw": (512, 2048, 2048), "in_dw": (1024, 512, 4096)}
FFN_TILES = {"l0a": _FFN_BEST, "l0b": _FFN_BEST, "l1a": _FFN_BEST,
             "l1b": dict(_FFN_BEST, in_bwd=(1024, 2048, 1408))}
FULL_K = (1024, 1024, 2048)
LOOP_K = (1024, 1024, 2048)


def _mm_nn(name, a, b, out_dtype, b_col0=0, n=None, tiles=FULL_K, comm=None):
    m, k = a.shape
    n = b.shape[1] if n is None else n
    tm, tn, tk = _mm_tiles(m, n, k, tiles)
    off = b_col0 // tn
    assert b_col0 % tn == 0
    return _matmul(
        name, "nn", (m // tm, n // tn, k // tk), a, pl.BlockSpec((tm, tk), lambda i, j, kk: (i, kk)),
        [b], [pl.BlockSpec((tk, tn), lambda i, j, kk: (kk, j + off))], [], [],
        [jax.ShapeDtypeStruct((m, n), out_dtype)], [pl.BlockSpec((tm, tn), lambda i, j, kk: (i, j))], (tm, tn),
        lambda accs, ex: accs, comm=comm)[0]


def _mm_bias_rows(name, a, b, bias, a_fn):
    m, k = a.shape
    n = b.shape[1]
    tn = _tile(n, 768, LANES)
    return _matmul(
        name, "nn", (1, n // tn, 1), a, pl.BlockSpec((m, k), lambda i, j, kk: (0, 0)),
        [b], [pl.BlockSpec((k, tn), lambda i, j, kk: (0, j))],
        [bias], [pl.BlockSpec((1, tn), lambda i, j, kk: (0, j))],
        [jax.ShapeDtypeStruct((m, n), F32)], [pl.BlockSpec((m, tn), lambda i, j, kk: (0, j))], (m, tn),
        lambda accs, ex: [accs[0] + ex[0]], a_fn=a_fn)[0]


def _mm_swiglu(name, h, w_in, f, tiles, comm=None):
    s, d = h.shape
    tm, tn, tk = _mm_tiles(s, f, d, tiles)
    nf = f // tn

    def epilogue(accs, ex):
        a_pre, b_pre = accs
        return [jnp.stack([a_pre, b_pre]), _silu(a_pre) * b_pre]

    return _matmul(
        name, "nn", (s // tm, nf, d // tk), h, pl.BlockSpec((tm, tk), lambda i, j, kk: (i, kk)),
        [w_in, w_in], [pl.BlockSpec((tk, tn), lambda i, j, kk: (kk, j)),
                       pl.BlockSpec((tk, tn), lambda i, j, kk: (kk, j + nf))], [], [],
        [jax.ShapeDtypeStruct((2, s, f), F32), jax.ShapeDtypeStruct((s, f), BF16)],
        [pl.BlockSpec((2, tm, tn), lambda i, j, kk: (0, i, j)), pl.BlockSpec((tm, tn), lambda i, j, kk: (i, j))],
        (tm, tn), epilogue, comm=comm)


def _mm_residual(name, a, w, x_in, gvec, tiles=FULL_K, comm=None):
    s, k = a.shape
    d = w.shape[1]
    tm, tn, tk = _mm_tiles(s, d, k, tiles)

    def epilogue(accs, ex):
        x_blk, g_row = ex
        return [accs[0], x_blk + g_row * accs[0]]

    tile = pl.BlockSpec((tm, tn), lambda i, j, kk: (i, j))
    return _matmul(
        name, "nn", (s // tm, d // tn, k // tk), a, pl.BlockSpec((tm, tk), lambda i, j, kk: (i, kk)),
        [w], [pl.BlockSpec((tk, tn), lambda i, j, kk: (kk, j))],
        [x_in, gvec], [tile, pl.BlockSpec((1, tn), lambda i, j, kk: (0, j))],
        [jax.ShapeDtypeStruct((s, d), F32), jax.ShapeDtypeStruct((s, d), F32)], [tile, tile], (tm, tn), epilogue,
        comm=comm)


def _mm_nt(name, a, w, out_dtype, w_col0=0, extra_add=None, tiles=FULL_K):
    s, k = a.shape
    n = w.shape[0]
    tm, tn, tk = _mm_tiles(s, n, k, tiles)
    off = w_col0 // tk
    assert w_col0 % tk == 0
    tile = pl.BlockSpec((tm, tn), lambda i, j, kk: (i, j))
    extras, especs = ([], []) if extra_add is None else ([extra_add], [tile])
    return _matmul(
        name, "nt", (s // tm, n // tn, k // tk), a, pl.BlockSpec((tm, tk), lambda i, j, kk: (i, kk)),
        [w], [pl.BlockSpec((tn, tk), lambda i, j, kk: (j, kk + off))], extras, especs,
        [jax.ShapeDtypeStruct((s, n), out_dtype)], [tile], (tm, tn),
        lambda accs, ex: [accs[0] + ex[0]] if ex else accs)[0]


def _mm_nt_parts(name, a3, w, out_dtype, tiles=LOOP_K, comm=None):
    p, s, k = a3.shape
    n = w.shape[0]
    tm, tn, tk = _mm_tiles(s, n, k, tiles)
    nkp = k // tk
    return _matmul(
        name, "nt", (s // tm, n // tn, p * nkp), a3,
        pl.BlockSpec((None, tm, tk), lambda i, j, kk: (kk // nkp, i, kk % nkp)),
        [w], [pl.BlockSpec((tn, tk), lambda i, j, kk: (j, kk))], [], [],
        [jax.ShapeDtypeStruct((s, n), out_dtype)], [pl.BlockSpec((tm, tn), lambda i, j, kk: (i, j))], (tm, tn),
        lambda accs, ex: accs, comm=comm)[0]


def _mm_nt_swiglu_bwd(name, dy, w_out, ab, tiles):
    s, d = dy.shape
    f = w_out.shape[0]
    tm, tn, tk = _mm_tiles(s, f, d, tiles)

    def epilogue(accs, ex):
        du, a_pre, b_pre = accs[0], ex[0][0], ex[0][1]
        sig = jax.nn.sigmoid(a_pre)
        da = du * b_pre * (sig * (1.0 + a_pre * (1.0 - sig)))
        db = du * (a_pre * sig)
        return [jnp.stack([da, db])]

    pair = pl.BlockSpec((2, tm, tn), lambda i, j, kk: (0, i, j))
    return _matmul(
        name, "nt", (s // tm, f // tn, d // tk), dy, pl.BlockSpec((tm, tk), lambda i, j, kk: (i, kk)),
        [w_out], [pl.BlockSpec((tn, tk), lambda i, j, kk: (j, kk))], [ab], [pair],
        [jax.ShapeDtypeStruct((2, s, f), BF16)], [pair], (tm, tn), epilogue)[0]


def _mm_tn(name, a, b, out_dtype, a_fn=None, tiles=LOOP_K, comm=None):
    s, m = a.shape
    tm, tk = _tile(m, tiles[0], LANES), _tile(s, tiles[2], 16)
    if b.ndim == 3:
        npart = b.shape[2]
        n = b.shape[0] * npart
        tn = _tile(npart, tiles[1], LANES)
        nj = npart // tn
        b_spec = pl.BlockSpec((None, tk, tn), lambda i, j, kk: (j // nj, kk, j % nj))
    else:
        n = b.shape[1]
        tn = _tile(n, tiles[1], LANES)
        b_spec = pl.BlockSpec((tk, tn), lambda i, j, kk: (kk, j))
    return _matmul(
        name, "tn", (m // tm, n // tn, s // tk), a, pl.BlockSpec((tk, tm), lambda i, j, kk: (kk, i)),
        [b], [b_spec], [], [],
        [jax.ShapeDtypeStruct((m, n), out_dtype)], [pl.BlockSpec((tm, tn), lambda i, j, kk: (i, j))], (tm, tn),
        lambda accs, ex: accs, a_fn=a_fn, comm=comm)[0]


def _row_tile(s):
    return _tile(s, 256, 16)


def _fold_rows(v):
    return jnp.sum(v.reshape(v.shape[0] // SUBLANES, SUBLANES, v.shape[1]), axis=0)


def _accumulate(i, acc_ref, val):
    @pl.when(i == 0)
    def _():
        acc_ref[...] = val

    @pl.when(i > 0)
    def _():
        acc_ref[...] += val


def _norm_mod(name, x, g, shift, scale):
    s, d = x.shape
    tb = _row_tile(s)

    def body(x_ref, g_ref, sh_ref, sc_ref, h_ref):
        xv = x_ref[...]
        rstd = lax.rsqrt(jnp.mean(xv * xv, axis=-1, keepdims=True) + EPS)
        h_ref[...] = ((xv * rstd) * g_ref[...] * (1.0 + sc_ref[...]) + sh_ref[...]).astype(BF16)

    row = pl.BlockSpec((tb, d), lambda i: (i, 0))
    vec = pl.BlockSpec((1, d), lambda i: (0, 0))
    return pl.pallas_call(body, name=name, grid=(s // tb,), in_specs=[row, vec, vec, vec], out_specs=row,
                          out_shape=jax.ShapeDtypeStruct((s, d), BF16), compiler_params=_params(1))(x, g, shift, scale)


def _rms_bwd(xv, dn_g, d):
    rstd = lax.rsqrt(jnp.mean(xv * xv, axis=-1, keepdims=True) + EPS)
    xhat = xv * rstd
    dx = rstd * (dn_g - xhat * (jnp.sum(dn_g * xhat, axis=-1, keepdims=True) * (1.0 / d)))
    return xhat, dx


def _norm_mod_bwd(name, dh, x, g, scale, dx_in):
    s, d = x.shape
    tb = _row_tile(s)
    nsteps = s // tb

    def body(dh_ref, x_ref, g_ref, sc_ref, dxin_ref, dx_ref, dsh_ref, dsc_ref, dg_ref, a_sh, a_sc, a_g):
        i = pl.program_id(0)
        dhv, gv = dh_ref[...], g_ref[...]
        dn = dhv * (1.0 + sc_ref[...])
        xhat, dx = _rms_bwd(x_ref[...], dn * gv, d)
        dx_ref[...] = dxin_ref[...] + dx
        _accumulate(i, a_sh, _fold_rows(dhv))
        _accumulate(i, a_sc, _fold_rows(dhv * (xhat * gv)))
        _accumulate(i, a_g, _fold_rows(dn * xhat))

        @pl.when(i == nsteps - 1)
        def _():
            dsh_ref[...] = jnp.sum(a_sh[...], axis=0, keepdims=True)
            dsc_ref[...] = jnp.sum(a_sc[...], axis=0, keepdims=True)
            dg_ref[...] = jnp.sum(a_g[...], axis=0, keepdims=True)

    row = pl.BlockSpec((tb, d), lambda i: (i, 0))
    vec = pl.BlockSpec((1, d), lambda i: (0, 0))
    vshape = jax.ShapeDtypeStruct((1, d), F32)
    return pl.pallas_call(
        body, name=name, grid=(nsteps,), in_specs=[row, row, vec, vec, row], out_specs=[row, vec, vec, vec],
        out_shape=[jax.ShapeDtypeStruct((s, d), F32), vshape, vshape, vshape],
        scratch_shapes=[pltpu.VMEM((SUBLANES, d), F32)] * 3, compiler_params=_params(1))(dh, x, g, scale, dx_in)


def _gate_bwd(name, dx, y, gvec, coef):
    s, d = dx.shape
    tb = _row_tile(s)
    nsteps = s // tb

    def body(dx_ref, y_ref, g_ref, dy_ref, dg_ref, acc):
        i = pl.program_id(0)
        dxv = dx_ref[...]
        dy_ref[...] = (dxv * g_ref[...]).astype(BF16)
        _accumulate(i, acc, _fold_rows(dxv * y_ref[...]))

        @pl.when(i == nsteps - 1)
        def _():
            dg_ref[...] = coef * jnp.sum(acc[...], axis=0, keepdims=True)

    row = pl.BlockSpec((tb, d), lambda i: (i, 0))
    vec = pl.BlockSpec((1, d), lambda i: (0, 0))
    return pl.pallas_call(
        body, name=name, grid=(nsteps,), in_specs=[row, row, vec], out_specs=[row, vec],
        out_shape=[jax.ShapeDtypeStruct((s, d), BF16), jax.ShapeDtypeStruct((1, d), F32)],
        scratch_shapes=[pltpu.VMEM((SUBLANES, d), F32)], compiler_params=_params(1))(dx, y, gvec)


def _final_loss(name, x, target, g):
    s, d = x.shape
    tb = _row_tile(s)
    nsteps = s // tb

    def body(x_ref, t_ref, g_ref, dx_ref, loss_ref, dg_ref, a_loss, a_g):
        i = pl.program_id(0)
        xv, gv = x_ref[...], g_ref[...]
        rstd = lax.rsqrt(jnp.mean(xv * xv, axis=-1, keepdims=True) + EPS)
        xhat = xv * rstd
        err = xhat * gv - t_ref[...]
        dyv = err * (1.0 / d)
        dn_g = dyv * gv
        dx_ref[...] = rstd * (dn_g - xhat * (jnp.sum(dn_g * xhat, axis=-1, keepdims=True) * (1.0 / d)))
        _accumulate(i, a_loss, _fold_rows(err * err))
        _accumulate(i, a_g, _fold_rows(dyv * xhat))

        @pl.when(i == nsteps - 1)
        def _():
            total = jnp.sum(jnp.sum(a_loss[...], axis=0, keepdims=True), axis=1, keepdims=True) * (0.5 / d)
            loss_ref[...] = jnp.broadcast_to(total, (1, LANES))
            dg_ref[...] = jnp.sum(a_g[...], axis=0, keepdims=True)

    row = pl.BlockSpec((tb, d), lambda i: (i, 0))
    vec = pl.BlockSpec((1, d), lambda i: (0, 0))
    return pl.pallas_call(
        body, name=name, grid=(nsteps,), in_specs=[row, row, vec],
        out_specs=[row, pl.BlockSpec((1, LANES), lambda i: (0, 0)), vec],
        out_shape=[jax.ShapeDtypeStruct((s, d), F32), jax.ShapeDtypeStruct((1, LANES), F32),
                   jax.ShapeDtypeStruct((1, d), F32)],
        scratch_shapes=[pltpu.VMEM((SUBLANES, d), F32)] * 2, compiler_params=_params(1))(x, target, g)


def _shift_rows(cur, halo, n):
    rolled = pltpu.roll(cur, n, 0)
    row = lax.broadcasted_iota(jnp.int32, cur.shape, 0)
    for r in range(n):
        rolled = jnp.where(row == r, halo[SUBLANES - n + r:SUBLANES - n + r + 1, :], rolled)
    return rolled


def _shift_rows_up(cur, halo, n):
    tb = cur.shape[0]
    rolled = pltpu.roll(cur, tb - n, 0)
    row = lax.broadcasted_iota(jnp.int32, cur.shape, 0)
    for r in range(n):
        rolled = jnp.where(row == tb - n + r, halo[r:r + 1, :], rolled)
    return rolled


def _conv_specs(s, d, tb, tn):
    nd = d // tn
    hb = tb // SUBLANES
    cur = lambda part: pl.BlockSpec((tb, tn), lambda i, j: (i, j + part * nd))
    prev = lambda part: pl.BlockSpec((SUBLANES, tn), lambda i, j: (jnp.maximum(i * hb - 1, 0), j + part * nd))
    nxt = lambda part: pl.BlockSpec((SUBLANES, tn), lambda i, j: (jnp.minimum((i + 1) * hb, s // SUBLANES - 1),
                                                                  j + part * nd))
    return cur, prev, nxt


def _conv_fwd(name, p, conv_w, conv_b):
    s, d = p.shape[0], p.shape[1] // 3
    tb, tn = _row_tile(s), _tile(d, 512, LANES)
    cur, prev, _ = _conv_specs(s, d, tb, tn)

    def body(bg_ref, cg_ref, xv_ref, cgp_ref, xvp_ref, w_ref, b_ref, z_ref):
        i = pl.program_id(0)
        u = cg_ref[...] * xv_ref[...]
        up = jnp.where(i > 0, cgp_ref[...] * xvp_ref[...], 0.0)
        w = w_ref[...]
        uc = w[0:1, :] * _shift_rows(u, up, 2) + w[1:2, :] * _shift_rows(u, up, 1) + w[2:3, :] * u + b_ref[...]
        z_ref[...] = (bg_ref[...] * uc).astype(BF16)

    return pl.pallas_call(
        body, name=name, grid=(s // tb, d // tn),
        in_specs=[cur(0), cur(1), cur(2), prev(1), prev(2), pl.BlockSpec((3, tn), lambda i, j: (0, j)),
                  pl.BlockSpec((1, tn), lambda i, j: (0, j))],
        out_specs=pl.BlockSpec((tb, tn), lambda i, j: (i, j)),
        out_shape=jax.ShapeDtypeStruct((s, d), BF16), compiler_params=_params(2))(p, p, p, p, p, conv_w, conv_b)


def _conv_bwd(name, dz, p, conv_w, conv_b):
    s, d = dz.shape
    tb, tn = _row_tile(s), _tile(d, 512, LANES)
    nsteps = s // tb
    cur, prev, nxt = _conv_specs(s, d, tb, tn)

    def body(dz_ref, bg_ref, cg_ref, xv_ref, cgp_ref, xvp_ref, dzn_ref, bgn_ref, w_ref, b_ref,
             dp_ref, dw_ref, db_ref, a_w0, a_w1, a_w2, a_b):
        j, i = pl.program_id(0), pl.program_id(1)
        cg, xv, bg, dzv = cg_ref[...], xv_ref[...], bg_ref[...], dz_ref[...]
        u = cg * xv
        up = jnp.where(i > 0, cgp_ref[...] * xvp_ref[...], 0.0)
        w = w_ref[...]
        u1, u2 = _shift_rows(u, up, 1), _shift_rows(u, up, 2)
        uc = w[0:1, :] * u2 + w[1:2, :] * u1 + w[2:3, :] * u + b_ref[...]
        duc = dzv * bg
        ducn = jnp.where(i < nsteps - 1, dzn_ref[...] * bgn_ref[...], 0.0)
        du = w[2:3, :] * duc + w[1:2, :] * _shift_rows_up(duc, ducn, 1) + w[0:1, :] * _shift_rows_up(duc, ducn, 2)
        dp_ref[0] = (dzv * uc).astype(BF16)
        dp_ref[1] = (du * xv).astype(BF16)
        dp_ref[2] = (du * cg).astype(BF16)
        _accumulate(i, a_w0, _fold_rows(duc * u2))
        _accumulate(i, a_w1, _fold_rows(duc * u1))
        _accumulate(i, a_w2, _fold_rows(duc * u))
        _accumulate(i, a_b, _fold_rows(duc))

        @pl.when(i == nsteps - 1)
        def _():
            dw_ref[0:1, :] = jnp.sum(a_w0[...], axis=0, keepdims=True)
            dw_ref[1:2, :] = jnp.sum(a_w1[...], axis=0, keepdims=True)
            dw_ref[2:3, :] = jnp.sum(a_w2[...], axis=0, keepdims=True)
            db_ref[...] = jnp.sum(a_b[...], axis=0, keepdims=True)

    swap = lambda spec: pl.BlockSpec(spec.block_shape, lambda j, i, _m=spec.index_map: _m(i, j))
    return pl.pallas_call(
        body, name=name, grid=(d // tn, nsteps),
        in_specs=[pl.BlockSpec((tb, tn), lambda j, i: (i, j)), swap(cur(0)), swap(cur(1)), swap(cur(2)),
                  swap(prev(1)), swap(prev(2)), swap(nxt(0)), swap(nxt(0)),
                  pl.BlockSpec((3, tn), lambda j, i: (0, j)), pl.BlockSpec((1, tn), lambda j, i: (0, j))],
        out_specs=[pl.BlockSpec((3, tb, tn), lambda j, i: (0, i, j)), pl.BlockSpec((3, tn), lambda j, i: (0, j)),
                   pl.BlockSpec((1, tn), lambda j, i: (0, j))],
        out_shape=[jax.ShapeDtypeStruct((3, s, d), BF16), jax.ShapeDtypeStruct((3, d), F32),
                   jax.ShapeDtypeStruct((1, d), F32)],
        scratch_shapes=[pltpu.VMEM((SUBLANES, tn), F32)] * 4, compiler_params=_params(2),
    )(dz, p, p, p, p, p, dz, p, conv_w, conv_b)


def _block_cumsum(v):
    tb = v.shape[0]
    row = lax.broadcasted_iota(jnp.int32, v.shape, 0)
    sh = 1
    while sh < tb:
        v = v + jnp.where(row >= sh, pltpu.roll(v, sh, 0), 0.0)
        sh *= 2
    return v


def _fgate_fwd(name, zf, b_f):
    s = zf.shape[0]
    tb = _tile(s, 512, SUBLANES)

    def body(z_ref, b_ref, f_ref, ft_ref, carry):
        i = pl.program_id(0)
        z = z_ref[...] + b_ref[...]
        ls = jnp.minimum(z, 0.0) - jnp.log(1.0 + jnp.exp(-jnp.abs(z)))
        run = _block_cumsum(ls)

        @pl.when(i == 0)
        def _():
            carry[...] = jnp.zeros_like(carry)

        out = run + carry[...]
        f_ref[...] = out
        ft_ref[...] = jnp.transpose(out)
        carry[...] = out[tb - 1:tb, :]

    blk = pl.BlockSpec((tb, LANES), lambda i: (i, 0))
    return pl.pallas_call(
        body, name=name, grid=(s // tb,), in_specs=[blk, pl.BlockSpec((1, LANES), lambda i: (0, 0))],
        out_specs=[blk, pl.BlockSpec((LANES, tb), lambda i: (0, i))],
        out_shape=[jax.ShapeDtypeStruct((s, LANES), F32), jax.ShapeDtypeStruct((LANES, s), F32)],
        scratch_shapes=[pltpu.VMEM((1, LANES), F32)], compiler_params=_params(1))(zf, b_f)


def _fgate_bwd(name, dfcum, zf, b_f):
    s = zf.shape[0]
    tb = _tile(s, 512, SUBLANES)
    nsteps = s // tb

    def body(df_ref, z_ref, b_ref, dz_ref, db_ref, carry, acc):
        i = pl.program_id(0)
        df = df_ref[...]
        incl = _block_cumsum(df)
        total = incl[tb - 1:tb, :]

        @pl.when(i == 0)
        def _():
            carry[...] = jnp.zeros_like(carry)

        suffix = total - incl + df + carry[...]
        carry[...] += total
        dzv = suffix * jax.nn.sigmoid(-(z_ref[...] + b_ref[...]))
        dz_ref[...] = dzv
        _accumulate(i, acc, _fold_rows(dzv))

        @pl.when(i == nsteps - 1)
        def _():
            db_ref[...] = jnp.sum(acc[...], axis=0, keepdims=True)

    blk = pl.BlockSpec((tb, LANES), lambda i: (nsteps - 1 - i, 0))
    vec = pl.BlockSpec((1, LANES), lambda i: (0, 0))
    return pl.pallas_call(
        body, name=name, grid=(nsteps,), in_specs=[blk, blk, vec], out_specs=[blk, vec],
        out_shape=[jax.ShapeDtypeStruct((s, LANES), F32), jax.ShapeDtypeStruct((1, LANES), F32)],
        scratch_shapes=[pltpu.VMEM((1, LANES), F32), pltpu.VMEM((SUBLANES, LANES), F32)],
        compiler_params=_params(1))(dfcum, zf, b_f)


def _lane_column(block, lane):
    sel = lax.broadcasted_iota(jnp.int32, block.shape, 1) == lane
    return jnp.sum(jnp.where(sel, block, 0.0), axis=1, keepdims=True)


def _col_to_row(col):
    return jnp.transpose(jnp.broadcast_to(col, (col.shape[0], LANES)))[0:1, :]


def _row_to_col(row):
    return jnp.transpose(jnp.broadcast_to(row, (LANES, row.shape[1])))[:, 0:1]


def _attn_tiles(s):
    t = _tile(s, 512, LANES)
    return t, t


def _attn_fwd(name, q, kv, fcum_t, n_heads):
    s, d = q.shape
    dh = d // n_heads
    tq, tk = _attn_tiles(s)
    scale2 = LOG2E / math.sqrt(dh)

    def body(q_ref, k_ref, v_ref, ft_ref, o_ref, l_ref):
        qi = pl.program_id(1)
        qv = q_ref[...]

        def step(kj, carry, diagonal):
            m, l, acc = carry
            start = pl.multiple_of(kj * tk, tk)
            kb = k_ref[pl.ds(start, tk), :]
            vb = v_ref[pl.ds(start, tk), :]
            sc = lax.dot_general(qv, kb, _DN["nt"], preferred_element_type=F32) * scale2
            sc = sc - ft_ref[:, pl.ds(start, tk)] * LOG2E
            if diagonal:
                keep = lax.broadcasted_iota(jnp.int32, sc.shape, 0) >= lax.broadcasted_iota(jnp.int32, sc.shape, 1)
                sc = jnp.where(keep, sc, -jnp.inf)
            m_new = jnp.maximum(m, jnp.max(sc, axis=1, keepdims=True))
            alpha = jnp.exp2(m - m_new)
            pr = jnp.exp2(sc - m_new)
            l = alpha * l + jnp.sum(pr, axis=1, keepdims=True)
            acc = alpha * acc + lax.dot_general(pr.astype(BF16), vb, _DN["nn"], preferred_element_type=F32)
            return m_new, l, acc

        init = (jnp.full((tq, 1), -jnp.inf, F32), jnp.zeros((tq, 1), F32), jnp.zeros((tq, dh), F32))
        carry = lax.fori_loop(0, qi, lambda kj, cr: step(kj, cr, False), init)
        m, l, acc = step(qi, carry, True)
        o_ref[...] = (acc / l).astype(BF16)
        l_ref[...] = _col_to_row(m + jnp.log2(l))

    nh = n_heads
    return pl.pallas_call(
        body, name=name, grid=(nh, s // tq),
        in_specs=[pl.BlockSpec((tq, dh), lambda h, i: (i, h)),
                  pl.BlockSpec((s, dh), lambda h, i: (0, h)),
                  pl.BlockSpec((s, dh), lambda h, i: (0, nh + h)),
                  pl.BlockSpec((None, 1, s), lambda h, i: (h, 0, 0))],
        out_specs=[pl.BlockSpec((tq, dh), lambda h, i: (i, h)),
                   pl.BlockSpec((None, 1, tq), lambda h, i: (h, 0, i))],
        out_shape=[jax.ShapeDtypeStruct((s, d), BF16), jax.ShapeDtypeStruct((nh, 1, s), F32)],
        compiler_params=_params(2))(q, kv, kv, fcum_t)


def _attn_delta(name, d_o, o, n_heads):
    s, d = d_o.shape
    dh = d // n_heads
    tb = _row_tile(s)

    def body(do_ref, o_ref, dl_ref, dlt_ref, dob_ref):
        dov = do_ref[...]
        prod = dov * o_ref[...].astype(F32)
        lane = lax.broadcasted_iota(jnp.int32, (tb, LANES), 1)
        out = jnp.zeros((tb, LANES), F32)
        for h in range(n_heads):
            out = jnp.where(lane == h, jnp.sum(prod[:, h * dh:(h + 1) * dh], axis=1, keepdims=True), out)
        dl_ref[...] = out
        dlt_ref[...] = jnp.transpose(out)
        dob_ref[...] = dov.astype(BF16)

    row = pl.BlockSpec((tb, d), lambda i: (i, 0))
    return pl.pallas_call(
        body, name=name, grid=(s // tb,), in_specs=[row, row],
        out_specs=[pl.BlockSpec((tb, LANES), lambda i: (i, 0)), pl.BlockSpec((LANES, tb), lambda i: (0, i)), row],
        out_shape=[jax.ShapeDtypeStruct((s, LANES), F32), jax.ShapeDtypeStruct((LANES, s), F32),
                   jax.ShapeDtypeStruct((s, d), BF16)],
        compiler_params=_params(1))(d_o, o)


def _attn_dq(name, q, kv, d_o, fcum_t, lse_t, delta, n_heads):
    s, d = q.shape
    dh = d // n_heads
    tq, tk = _attn_tiles(s)
    scale = 1.0 / math.sqrt(dh)
    scale2 = LOG2E * scale

    def body(q_ref, k_ref, v_ref, do_ref, ft_ref, l_ref, dl_ref, dq_ref, df_ref):
        h, qi = pl.program_id(0), pl.program_id(1)
        qv, dov = q_ref[...], do_ref[...]
        lse_col = _row_to_col(l_ref[...])
        dl_col = _lane_column(dl_ref[...], h)

        def step(kj, carry, diagonal):
            acc, dfq = carry
            start = pl.multiple_of(kj * tk, tk)
            kb = k_ref[pl.ds(start, tk), :]
            vb = v_ref[pl.ds(start, tk), :]
            sc = lax.dot_general(qv, kb, _DN["nt"], preferred_element_type=F32) * scale2
            sc = sc - ft_ref[:, pl.ds(start, tk)] * LOG2E
            pr = jnp.exp2(sc - lse_col)
            if diagonal:
                keep = lax.broadcasted_iota(jnp.int32, sc.shape, 0) >= lax.broadcasted_iota(jnp.int32, sc.shape, 1)
                pr = jnp.where(keep, pr, 0.0)
            dp = lax.dot_general(dov, vb, _DN["nt"], preferred_element_type=F32)
            ds = pr * (dp - dl_col)
            acc = acc + lax.dot_general(ds.astype(BF16), kb, _DN["nn"], preferred_element_type=F32)
            return acc, dfq + jnp.sum(ds, axis=1, keepdims=True)

        init = (jnp.zeros((tq, dh), F32), jnp.zeros((tq, 1), F32))
        carry = lax.fori_loop(0, qi, lambda kj, cr: step(kj, cr, False), init)
        acc, dfq = step(qi, carry, True)
        dq_ref[...] = (acc * scale).astype(BF16)
        df_ref[...] = _col_to_row(dfq)

    nh = n_heads
    qspec = pl.BlockSpec((tq, dh), lambda h, i: (i, h))
    lanes = pl.BlockSpec((tq, LANES), lambda h, i: (i, 0))
    return pl.pallas_call(
        body, name=name, grid=(nh, s // tq),
        in_specs=[qspec, pl.BlockSpec((s, dh), lambda h, i: (0, h)), pl.BlockSpec((s, dh), lambda h, i: (0, nh + h)),
                  qspec, pl.BlockSpec((None, 1, s), lambda h, i: (h, 0, 0)),
                  pl.BlockSpec((None, 1, tq), lambda h, i: (h, 0, i)), lanes],
        out_specs=[qspec, pl.BlockSpec((None, 1, tq), lambda h, i: (h, 0, i))],
        out_shape=[jax.ShapeDtypeStruct((s, d), BF16), jax.ShapeDtypeStruct((nh, 1, s), F32)],
        compiler_params=_params(2))(q, kv, kv, d_o, fcum_t, lse_t, delta)


def _attn_dkv(name, q, kv, d_o, fcum, lse_t, delta_t, n_heads):
    s, d = q.shape
    dh = d // n_heads
    tq, tk = _attn_tiles(s)
    nq = s // tq
    scale = 1.0 / math.sqrt(dh)
    scale2 = LOG2E * scale

    def body(q_ref, k_ref, v_ref, do_ref, f_ref, lt_ref, dlt_ref, dkv_ref, df_ref):
        h, kj = pl.program_id(0), pl.program_id(1)
        kb, vb = k_ref[...], v_ref[...]
        fk2 = _lane_column(f_ref[...], h) * LOG2E

        def step(qi, carry, diagonal):
            dk, dv, dfk = carry
            start = pl.multiple_of(qi * tq, tq)
            qb = q_ref[pl.ds(start, tq), :]
            dob = do_ref[pl.ds(start, tq), :]
            sc = lax.dot_general(kb, qb, _DN["nt"], preferred_element_type=F32) * scale2 - fk2
            pr = jnp.exp2(sc - lt_ref[:, pl.ds(start, tq)])
            if diagonal:
                keep = lax.broadcasted_iota(jnp.int32, sc.shape, 1) >= lax.broadcasted_iota(jnp.int32, sc.shape, 0)
                pr = jnp.where(keep, pr, 0.0)
            dv = dv + lax.dot_general(pr.astype(BF16), dob, _DN["nn"], preferred_element_type=F32)
            dp = lax.dot_general(vb, dob, _DN["nt"], preferred_element_type=F32)
            ds = pr * (dp - dlt_ref[:, pl.ds(start, tq)])
            dk = dk + lax.dot_general(ds.astype(BF16), qb, _DN["nn"], preferred_element_type=F32)
            dfk = dfk - jnp.sum(ds, axis=1, keepdims=True)
            return dk, dv, dfk

        init = (jnp.zeros((tk, dh), F32), jnp.zeros((tk, dh), F32), jnp.zeros((tk, 1), F32))
        carry = step(kj, init, True)
        dk, dv, dfk = lax.fori_loop(kj + 1, nq, lambda qi, cr: step(qi, cr, False), carry)
        dkv_ref[0] = (dk * scale).astype(BF16)
        dkv_ref[1] = dv.astype(BF16)
        df_ref[...] = _col_to_row(dfk)

    nh = n_heads
    full = lambda col0: pl.BlockSpec((s, dh), lambda h, j: (0, col0 + h))
    row_t = pl.BlockSpec((None, 1, s), lambda h, j: (h, 0, 0))
    return pl.pallas_call(
        body, name=name, grid=(nh, s // tk),
        in_specs=[full(0), pl.BlockSpec((tk, dh), lambda h, j: (j, h)), pl.BlockSpec((tk, dh), lambda h, j: (j, nh + h)),
                  full(0), pl.BlockSpec((tk, LANES), lambda h, j: (j, 0)), row_t, row_t],
        out_specs=[pl.BlockSpec((2, tk, dh), lambda h, j: (0, j, h)),
                   pl.BlockSpec((None, 1, tk), lambda h, j: (h, 0, j))],
        out_shape=[jax.ShapeDtypeStruct((2, s, d), BF16), jax.ShapeDtypeStruct((nh, 1, s), F32)],
        compiler_params=_params(2))(q, kv, kv, d_o, fcum, lse_t, delta_t)


def _sum_slots(name, g):
    n, r, lanes = g.shape

    def body(g_ref, o_ref):
        total = g_ref[0]
        for i in range(1, n):
            total = total + g_ref[i]
        o_ref[...] = total

    return pl.pallas_call(
        body, name=name, out_shape=jax.ShapeDtypeStruct((r, lanes), F32),
        in_specs=[pl.BlockSpec(memory_space=pltpu.VMEM)], out_specs=pl.BlockSpec(memory_space=pltpu.VMEM),
        compiler_params=pltpu.CompilerParams(vmem_limit_bytes=VMEM_LIMIT))(g)


def _adamw(name, parts, w, m, v, row0=0, previous=None):
    npart, r, c = parts.shape
    tb = _tile(math.gcd(r, row0) if row0 else r, max(SUBLANES, (2**18 // c) // SUBLANES * SUBLANES), SUBLANES)
    off = row0 // tb
    c1 = 1.0 - ADAM_B1 ** ADAM_STEP
    c2 = 1.0 - ADAM_B2 ** ADAM_STEP

    def body(p_ref, w_ref, m_ref, v_ref, *rest):
        g_out, d_out, m_out, v_out = rest[-4:]
        g = p_ref[0].astype(F32)
        for i in range(1, npart):
            g = g + p_ref[i].astype(F32)
        m_new = ADAM_B1 * m_ref[...] + (1.0 - ADAM_B1) * g
        v_new = ADAM_B2 * v_ref[...] + (1.0 - ADAM_B2) * (g * g)
        m_hat = m_new / c1
        v_hat = v_new / c2
        g_out[...] = g
        d_out[...] = -ADAM_LR * (m_hat / (jnp.sqrt(v_hat) + ADAM_EPS) + ADAM_WD * w_ref[...])
        m_out[...] = m_new
        v_out[...] = v_new

    blk = pl.BlockSpec((tb, c), lambda i: (i + off, 0))
    shape = jax.ShapeDtypeStruct(w.shape, F32)
    in_specs = [pl.BlockSpec((npart, tb, c), lambda i: (0, i, 0)), blk, blk, blk]
    operands = [parts, w, m, v]
    aliases = {}
    if previous is not None:
        in_specs += [pl.BlockSpec(memory_space=pl.ANY)] * 4
        operands += list(previous)
        aliases = {4 + k: k for k in range(4)}
    return pl.pallas_call(
        body, name=name, grid=(r // tb,), in_specs=in_specs, out_specs=[blk] * 4, out_shape=[shape] * 4,
        input_output_aliases=aliases, compiler_params=_params(1))(*operands)


def _pack(vectors):
    flat = jnp.concatenate([v.reshape(-1).astype(F32) for v in vectors])
    pad = (-flat.shape[0]) % (SUBLANES * LANES)
    return jnp.pad(flat, (0, pad)).reshape(-1, LANES)


def _unpack(flat, shapes):
    out, pos = [], 0
    for shp in shapes:
        size = math.prod(shp)
        out.append(flat[..., pos:pos + size].reshape(flat.shape[:-1] + tuple(shp)))
        pos += size
    return out


def _unshard_last(g, lead):
    nd = len(lead)
    return jnp.moveaxis(g, 0, nd).reshape(tuple(lead) + (-1,))


def _my_slice(full, me, width, axis):
    return lax.dynamic_slice_in_dim(full, me * width, width, axis)


def kernel(x, c, norm_g, w_ada, b_ada, w_ffn_in, w_ffn_out, w_conv_in, conv_w, conv_b, w_conv_out, kv_norm_g, w_ada_kv, b_ada_kv, w_kvf, b_fgate, w_q, w_o, final_g, loss_target, m_norm_g, m_w_ada, m_b_ada, m_w_ffn_in, m_w_ffn_out, m_w_conv_in, m_conv_w, m_conv_b, m_w_conv_out, m_kv_norm_g, m_w_ada_kv, m_b_ada_kv, m_w_kvf, m_b_fgate, m_w_q, m_w_o, m_final_g, v_norm_g, v_w_ada, v_b_ada, v_w_ffn_in, v_w_ffn_out, v_w_conv_in, v_conv_w, v_conv_b, v_w_conv_out, v_kv_norm_g, v_w_ada_kv, v_b_ada_kv, v_w_kvf, v_b_fgate, v_w_q, v_w_o, v_final_g):
    s, d = x.shape[1], x.shape[2]
    n_heads = b_fgate.shape[0]
    n_layers = w_ada.shape[0]
    dsh = d // N_DEV
    f = w_ffn_out.shape[2] * N_DEV
    ada_w = w_ada.shape[2]
    kv_w = w_ada_kv.shape[1]
    kvf_w = w_kvf.shape[1]
    assert n_layers == 2 and w_conv_in.shape[0] == 1 and w_q.shape[0] == 1
    assert (d // n_heads) % LANES == 0 and n_heads <= LANES and N_DEV * kvf_w == 2 * d + n_heads
    me = 4 * lax.axis_index("x") + 2 * lax.axis_index("y") + lax.axis_index("c")
    x0, target = x[0], loss_target[0]

    small = _allgather_small("gather_small_params", _pack([c, norm_g, conv_w, conv_b]))
    c_all, ng_sh, cw_sh, cb_sh = _unpack(small.reshape(N_DEV, -1), [(d,), (n_layers, 3, dsh), (3, dsh), (dsh,)])
    norm_g_full = _unshard_last(ng_sh, (n_layers, 3))
    conv_w_full = _unshard_last(cw_sh, (3,))
    conv_b_full = _unshard_last(cb_sh, ()).reshape(1, d)
    c_rows = jnp.pad(c_all, ((0, 16 - N_DEV), (0, 0)))

    ada_cols = [
        _mm_bias_rows(f"ada_rows_{l}", c_rows, w_ada[l], _my_slice(b_ada[l], me, ada_w, 0).reshape(1, ada_w), _silu)
        for l in range(n_layers)]
    ada_cols.append(_mm_bias_rows("ada_rows_kv", c_rows, w_ada_kv, _my_slice(b_ada_kv, me, kv_w, 0).reshape(1, kv_w),
                                  _silu))
    ada_part = jnp.concatenate([a[:N_DEV] for a in ada_cols], axis=1)
    ada_all = _allgather_small("gather_ada_rows", _pack([ada_part]))
    ada_all = ada_all.reshape(N_DEV, -1)[:, :ada_part.size].reshape(N_DEV, N_DEV, -1)
    ada_mine = lax.dynamic_index_in_dim(ada_all, me, axis=1, keepdims=False)
    ada = [ada_mine[:, l * ada_w:(l + 1) * ada_w].reshape(3, 3, 1, d) for l in range(n_layers)]
    ada_kv = ada_mine[:, n_layers * ada_w:].reshape(2, 1, d)

    def as_item(w, kind):
        return (w.astype(BF16), kind)

    gather_first = _GatherComm([as_item(w_ffn_in[0, 0], "cols")])
    gather_l0b = _GatherComm([as_item(w_ffn_out[0, 0], "rows"), as_item(w_ffn_in[0, 1], "cols")])
    gather_conv = _GatherComm([as_item(w_conv_in[0], "cols"), as_item(w_conv_out[0], "rows")])
    gather_l0b_out = _GatherComm([as_item(w_ffn_out[0, 1], "rows")])
    gather_l1a = _GatherComm([as_item(w_ffn_out[1, 0], "rows"), as_item(w_ffn_in[1, 0], "cols")])
    gather_attn = _GatherComm([as_item(w_q[0], "rows"), as_item(w_kvf, "lead"), as_item(w_o[0], "rows")])
    gather_l1b = _GatherComm([as_item(w_ffn_out[1, 1], "rows"), as_item(w_ffn_in[1, 1], "cols")])
    (wf_in00,) = _run_comm("gather_ffn_first", gather_first)
    b_f = jnp.pad(b_fgate, (0, LANES - n_heads)).reshape(1, LANES)

    def ffn_fwd(tag, xs, l, sub, w_in, w_out, comm_in, comm_out):
        shift, scale, gate = ada[l][sub]
        tiles = FFN_TILES[tag]
        h = _norm_mod(f"norm_{tag}", xs, norm_g_full[l, sub].reshape(1, d), shift, scale)
        ab, u = _mm_swiglu(f"ffn_in_{tag}", h, w_in, f, tiles["in"], comm=comm_in)
        y, x_next = _mm_residual(f"ffn_out_{tag}", u, w_out, xs, 0.5 * gate, tiles["out"], comm=comm_out)
        return x_next, (xs, h, ab, u, y, w_in, w_out)

    h = _norm_mod("norm_l0a", x0, norm_g_full[0, 0].reshape(1, d), ada[0][0][0], ada[0][0][1])
    ab, u = _mm_swiglu("ffn_in_l0a", h, wf_in00, f, FFN_TILES["l0a"]["in"], comm=gather_l0b)
    wf_out00, wf_in01 = gather_l0b.results
    y, x1 = _mm_residual("ffn_out_l0a", u, wf_out00, x0, 0.5 * ada[0][0][2], FFN_TILES["l0a"]["out"], comm=gather_conv)
    save_f0 = (x0, h, ab, u, y, wf_in00, wf_out00)
    wc_in, wc_out = gather_conv.results
    shift, scale, gate = ada[0][1]
    h_c = _norm_mod("norm_conv", x1, norm_g_full[0, 1].reshape(1, d), shift, scale)
    p_c = _mm_nn("conv_in", h_c, wc_in, F32, comm=gather_l0b_out)
    (wf_out01,) = gather_l0b_out.results
    z_c = _conv_fwd("conv_mix", p_c, conv_w_full, conv_b_full)
    y_c, x2 = _mm_residual("conv_out", z_c, wc_out, x1, gate)
    x3, save_f1 = ffn_fwd("l0b", x2, 0, 2, wf_in01, wf_out01, gather_l1a, gather_attn)
    wf_out10, wf_in10 = gather_l1a.results
    wq, g_kvf, wo = gather_attn.results
    wkvf = _unshard_last(g_kvf, (d,))
    wkv = wkvf[:, :2 * d]
    wf_gate = jnp.pad(wkvf[:, 2 * d:], ((0, 0), (0, LANES - n_heads)))
    x4, save_f2 = ffn_fwd("l1a", x3, 1, 0, wf_in10, wf_out10, gather_l1b, None)
    wf_out11, wf_in11 = gather_l1b.results
    shift, scale, gate_a = ada[1][1]
    h_q = _norm_mod("norm_q", x4, norm_g_full[1, 1].reshape(1, d), shift, scale)
    h_kv = _norm_mod("norm_kv", x4, kv_norm_g.reshape(1, d), ada_kv[0], ada_kv[1])
    q = _mm_nn("attn_q", h_q, wq, BF16)
    kv = _mm_nn("attn_kv", h_kv, wkv, BF16)
    zf = _mm_nn("attn_fgate", h_kv, wf_gate, F32)
    fcum, fcum_lanes = _fgate_fwd("fgate_cumsum", zf, b_f)
    fcum_t = fcum_lanes[:n_heads].reshape(n_heads, 1, s)
    o, lse_t = _attn_fwd("attn_fwd", q, kv, fcum_t, n_heads)
    y_a, x5 = _mm_residual("attn_out", o, wo, x4, gate_a)
    x6, save_f3 = ffn_fwd("l1b", x5, 1, 2, wf_in11, wf_out11, None, None)
    dx, loss_part, d_final_g = _final_loss("final_loss", x6, target, final_g.reshape(1, d))

    d_ada = [[[None] * 3 for _ in range(3)] for _ in range(n_layers)]
    d_norm_g = [[None] * 3 for _ in range(n_layers)]
    core = lax.axis_index("c").astype(jnp.int32).reshape(1)
    in_shard, out_shard, row_shard = (d, 2 * f // N_DEV), (f // N_DEV, d), (dsh, d)

    def reduce_within_chip(tag, items):
        sibling = _SiblingComm(items)
        theirs = _run_comm(f"reduce_sibling_{tag}", sibling)
        return _ChipComm([_pair_add(f"add_sibling_{tag}_{n}", p, kind, shape, t, core)
                          for n, ((p, kind, shape), t) in enumerate(zip(items, theirs))])

    def ffn_bwd(tag, dxs, saved, l, sub, comm_dw):
        xs, h, ab, u, y, w_in, w_out = saved
        _, scale, gate = ada[l][sub]
        tiles = FFN_TILES[tag]
        dy, d_ada[l][sub][2] = _gate_bwd(f"gate_bwd_{tag}", dxs, y, 0.5 * gate, 0.5)
        dab = _mm_nt_swiglu_bwd(f"ffn_out_bwd_{tag}", dy, w_out, ab, tiles["out_bwd"])
        dw_in = _mm_tn(f"ffn_in_dw_{tag}", h, dab, BF16, tiles=tiles["in_dw"])
        dw_out = _mm_tn(f"ffn_out_dw_{tag}", u, dy, BF16, tiles=tiles["out_dw"], comm=comm_dw)
        scatter = reduce_within_chip(tag, [(dw_in, "cols", in_shard), (dw_out, "rows", out_shard)])
        dh = _mm_nt_parts(f"ffn_in_bwd_{tag}", dab, w_in, F32, tiles["in_bwd"], comm=scatter)
        dxs, d_ada[l][sub][0], d_ada[l][sub][1], d_norm_g[l][sub] = _norm_mod_bwd(
            f"norm_bwd_{tag}", dh, xs, norm_g_full[l, sub].reshape(1, d), scale, dxs)
        return dxs, scatter

    dx, scatter_l1b = ffn_bwd("l1b", dx, save_f3, 1, 2, None)
    dy, d_ada[1][1][2] = _gate_bwd("gate_bwd_attn", dx, y_a, gate_a, 1.0)
    d_o = _mm_nt("attn_out_bwd", dy, wo, F32)
    dwo = _mm_tn("attn_out_dw", o, dy, BF16)
    delta, delta_lanes, d_ob = _attn_delta("attn_delta", d_o, o, n_heads)
    delta_t = delta_lanes[:n_heads].reshape(n_heads, 1, s)
    dq, dfq = _attn_dq("attn_dq", q, kv, d_ob, fcum_t, lse_t, delta, n_heads)
    dkv, dfk = _attn_dkv("attn_dkv", q, kv, d_ob, fcum, lse_t, delta_t, n_heads)
    dfcum = jnp.pad((dfq + dfk).reshape(n_heads, s).T, ((0, 0), (0, LANES - n_heads)))
    dzf, d_bf = _fgate_bwd("fgate_bwd", dfcum, zf, b_f)
    dh_q = _mm_nt("attn_q_bwd", dq, wq, F32)
    dwq = _mm_tn("attn_q_dw", h_q, dq, BF16)
    dh_kv = _mm_nt_parts("attn_kv_bwd", dkv, wkv, F32)
    dh_kv = _mm_nt("attn_fgate_bwd", dzf, wf_gate, F32, extra_add=dh_kv)
    dwkv = _mm_tn("attn_kv_dw", h_kv, dkv, BF16)
    dwf = _mm_tn("attn_fgate_dw", h_kv, dzf, BF16)
    dwkvf = jnp.concatenate([dwkv, dwf[:, :n_heads]], axis=1)
    p_kvf = jnp.moveaxis(dwkvf.reshape(d, N_DEV, kvf_w), 1, 0)
    scatter_attn = reduce_within_chip(
        "attn", [(dwo, "rows", row_shard), (dwq, "rows", row_shard), (p_kvf, "lead", (d, kvf_w))])
    dx, d_ada[1][1][0], d_ada[1][1][1], d_norm_g[1][1] = _norm_mod_bwd(
        "norm_bwd_q", dh_q, x4, norm_g_full[1, 1].reshape(1, d), ada[1][1][1], dx)
    dx, d_kv_shift, d_kv_scale, d_kv_norm_g = _norm_mod_bwd(
        "norm_bwd_kv", dh_kv, x4, kv_norm_g.reshape(1, d), ada_kv[1], dx)
    dx, scatter_l1a = ffn_bwd("l1a", dx, save_f2, 1, 0, scatter_attn)
    dx, scatter_l0b = ffn_bwd("l0b", dx, save_f1, 0, 2, None)
    dy, d_ada[0][1][2] = _gate_bwd("gate_bwd_conv", dx, y_c, ada[0][1][2], 1.0)
    dz = _mm_nt("conv_out_bwd", dy, wc_out, F32)
    dwc_out = _mm_tn("conv_out_dw", z_c, dy, BF16)
    dp, d_conv_w, d_conv_b = _conv_bwd("conv_mix_bwd", dz, p_c, conv_w_full, conv_b_full)
    dh_c = _mm_nt_parts("conv_in_bwd", dp, wc_in, F32)
    dwc_in = _mm_tn("conv_in_dw", h_c, dp, BF16)
    scatter_conv = reduce_within_chip("conv", [(dwc_out, "rows", row_shard), (dwc_in, "cols", (d, 3 * d // N_DEV))])
    dx, d_ada[0][1][0], d_ada[0][1][1], d_norm_g[0][1] = _norm_mod_bwd(
        "norm_bwd_conv", dh_c, x1, norm_g_full[0, 1].reshape(1, d), ada[0][1][1], dx)
    dx, scatter_l0a = ffn_bwd("l0a", dx, save_f0, 0, 0, scatter_conv)
    grad_x = dx.reshape(1, s, d)

    d_ada_flat = jnp.concatenate([v.reshape(-1) for l in range(n_layers) for sub in range(3) for v in d_ada[l][sub]])
    d_norm_flat = jnp.concatenate([d_norm_g[l][sub].reshape(-1) for l in range(n_layers) for sub in range(3)])
    small_shapes = [(n_layers * 9 * d,), (2 * d,), (n_layers * 3 * d,), (3, d), (d,), (d,), (LANES,), (d,), (LANES,)]
    small_parts = _pack([d_ada_flat, d_kv_shift, d_kv_scale, d_norm_flat, d_conv_w, d_conv_b, d_kv_norm_g, d_bf,
                         d_final_g, loss_part])
    small_all = _allgather_small("gather_small_grads", small_parts)
    small_sum = _sum_slots("sum_small_grads", small_all).reshape(-1)
    g_b_ada, g_b_ada_kv, g_norm_full, g_conv_w_full, g_conv_b_full, g_kv_norm, g_bf, g_final, loss_v = _unpack(
        small_sum, small_shapes)
    loss = loss_v[0]
    d_ada_rows = small_all.reshape(N_DEV, -1)[:, :n_layers * 9 * d + 2 * d]
    d_ada_rows = jnp.pad(d_ada_rows, ((0, 16 - N_DEV), (0, 0)))

    results = {}

    def update(key, parts, w, m, v):
        shape = w.shape
        c_dim = shape[-1]
        outs = _adamw(f"adamw_{key}", parts.reshape(parts.shape[0], -1, c_dim), w.reshape(-1, c_dim),
                      m.reshape(-1, c_dim), v.reshape(-1, c_dim))
        results[key] = [o.reshape(shape) for o in outs]

    g_w_ada = []
    for l in range(n_layers):
        cols = _my_slice(d_ada_rows[:, l * 9 * d:(l + 1) * 9 * d], me, ada_w, 1)
        g_w_ada.append(_mm_tn(f"ada_dw_{l}", c_rows, cols, F32, a_fn=_silu))
    update("w_ada", jnp.stack(g_w_ada).reshape(1, n_layers * d, ada_w), w_ada, m_w_ada, v_w_ada)
    cols = _my_slice(d_ada_rows[:, n_layers * 9 * d:], me, kv_w, 1)
    update("w_ada_kv", _mm_tn("ada_dw_kv", c_rows, cols, F32, a_fn=_silu).reshape(1, d, kv_w),
           w_ada_kv, m_w_ada_kv, v_w_ada_kv)

    def update_rows(key, parts_list, w, m, v):
        c_dim = w.shape[-1]
        flat = [t.reshape(-1, c_dim) for t in (w, m, v)]
        outs = None
        for n, parts in enumerate(parts_list):
            outs = _adamw(f"adamw_{key}_{n}", parts, *flat, row0=n * parts.shape[1], previous=outs)
        results[key] = [o.reshape(w.shape) for o in outs]

    ffn_scatters = [scatter_l0a, scatter_l0b, scatter_l1a, scatter_l1b]
    update_rows("w_ffn_in", [sc.results[0] for sc in ffn_scatters], w_ffn_in, m_w_ffn_in, v_w_ffn_in)
    update_rows("w_ffn_out", [sc.results[1] for sc in ffn_scatters], w_ffn_out, m_w_ffn_out, v_w_ffn_out)
    r_co, r_ci = scatter_conv.results
    r_o, r_q, r_kvf = scatter_attn.results
    update_rows("w_conv_in", [r_ci], w_conv_in, m_w_conv_in, v_w_conv_in)
    update_rows("w_conv_out", [r_co], w_conv_out, m_w_conv_out, v_w_conv_out)
    update_rows("w_kvf", [r_kvf], w_kvf, m_w_kvf, v_w_kvf)
    update_rows("w_q", [r_q], w_q, m_w_q, v_w_q)
    update_rows("w_o", [r_o], w_o, m_w_o, v_w_o)

    small_keys = ["norm_g", "b_ada", "conv_w", "conv_b", "kv_norm_g", "b_ada_kv", "b_fgate", "final_g"]
    small_w = [norm_g, b_ada, conv_w, conv_b, kv_norm_g, b_ada_kv, b_fgate, final_g]
    small_m = [m_norm_g, m_b_ada, m_conv_w, m_conv_b, m_kv_norm_g, m_b_ada_kv, m_b_fgate, m_final_g]
    small_v = [v_norm_g, v_b_ada, v_conv_w, v_conv_b, v_kv_norm_g, v_b_ada_kv, v_b_fgate, v_final_g]
    small_g = [
        _my_slice(g_norm_full.reshape(n_layers, 3, d), me, dsh, 2), g_b_ada.reshape(b_ada.shape),
        _my_slice(g_conv_w_full, me, dsh, 1).reshape(conv_w.shape), _my_slice(g_conv_b_full, me, dsh, 0).reshape(
            conv_b.shape), g_kv_norm, g_b_ada_kv, g_bf[:n_heads], g_final]
    packed = _adamw("adamw_small", _pack(small_g)[None], _pack(small_w), _pack(small_m), _pack(small_v))
    for vals, idx in zip(packed, range(4)):
        for key, val in zip(small_keys, _unpack(vals.reshape(-1), [w.shape for w in small_w])):
            results.setdefault(key, [None] * 4)[idx] = val

    order = ["norm_g", "w_ada", "b_ada", "w_ffn_in", "w_ffn_out", "w_conv_in", "conv_w", "conv_b", "w_conv_out",
             "kv_norm_g", "w_ada_kv", "b_ada_kv", "w_kvf", "b_fgate", "w_q", "w_o", "final_g"]
    return (loss, grad_x, *[results[k][0] for k in order], *[results[k][1] for k in order],
            *[results[k][2] for k in order], *[results[k][3] for k in order])
```

```python
import functools
import math

import jax
import jax.numpy as jnp
from jax import lax
from jax.experimental import pallas as pl
from jax.experimental.pallas import tpu as pltpu

F32 = jnp.float32
BF16 = jnp.bfloat16
MESH = pl.DeviceIdType.MESH

N_DEV = 8
LANES = 128
SUBLANES = 8
VMEM_LIMIT = 56 * 2**20
EPS = 1e-6
LOG2E = math.log2(math.e)
ADAM_LR, ADAM_B1, ADAM_B2, ADAM_EPS, ADAM_WD, ADAM_STEP = 0.001, 0.9, 0.999, 1e-08, 0.01, 10

_DN = {"nn": (((1,), (0,)), ((), ())), "nt": (((1,), (1,)), ((), ())), "tn": (((0,), (0,)), ((), ()))}


def _tile(n, pref, align):
    t = min(pref, n)
    t -= t % align
    while t >= align:
        if n % t == 0:
            return t
        t -= align
    return n


def _params(n_grid):
    return pltpu.CompilerParams(vmem_limit_bytes=VMEM_LIMIT, dimension_semantics=("arbitrary",) * n_grid)


def _silu(v):
    return v * jax.nn.sigmoid(v)


def _position():
    x, y, c = lax.axis_index("x"), lax.axis_index("y"), lax.axis_index("c")
    return x, y, c


def _allgather_small(name, v):
    def body(v_ref, out_ref, send_sems, recv_sems, local_sem):
        x, y, c = _position()
        me, sibling = (x, y, c), (x, y, 1 - c)
        chips = [(1 - x, y), (x, 1 - y), (1 - x, 1 - y)]

        def slot(px, py, pc):
            return out_ref.at[4 * px + 2 * py + pc]

        def copy(k, block, to, src=None):
            return pltpu.make_async_remote_copy(
                src_ref=slot(*block) if src is None else src, dst_ref=slot(*block),
                send_sem=send_sems.at[k], recv_sem=recv_sems.at[k], device_id=to, device_id_type=MESH)

        mine = pltpu.make_async_copy(v_ref, slot(*me), local_sem)
        mine.start()
        first = [copy(0, me, sibling, src=v_ref)]
        first += [copy(1 + j, me, (*chip, c), src=v_ref) for j, chip in enumerate(chips)]
        for cp in first:
            cp.start()
        passed = [copy(4 + j, (*chip, c), sibling) for j, chip in enumerate(chips)]
        for j, chip in enumerate(chips):
            copy(1 + j, (*chip, c), me).wait_recv()
            passed[j].start()
        copy(0, sibling, me).wait_recv()
        for j, chip in enumerate(chips):
            copy(4 + j, (*chip, 1 - c), me).wait_recv()
        for cp in first + passed:
            cp.wait_send()
        mine.wait()

    return pl.pallas_call(
        body, name=name,
        out_shape=jax.ShapeDtypeStruct((N_DEV,) + v.shape, v.dtype),
        in_specs=[pl.BlockSpec(memory_space=pltpu.VMEM)],
        out_specs=pl.BlockSpec(memory_space=pltpu.VMEM),
        scratch_shapes=[pltpu.SemaphoreType.DMA((7,)), pltpu.SemaphoreType.DMA((7,)), pltpu.SemaphoreType.DMA],
        compiler_params=pltpu.CompilerParams(vmem_limit_bytes=VMEM_LIMIT),
    )(v)


def _owner_view(ref, kind, shard_shape, owner):
    r, w = shard_shape
    if kind == "rows":
        return ref.at[pl.ds(pl.multiple_of(owner * r, 16), r)]
    if kind == "cols":
        return ref.at[:, pl.ds(pl.multiple_of(owner * w, LANES), w)]
    return ref.at[owner]


def _full_shape(kind, shard_shape):
    r, w = shard_shape
    return {"rows": (N_DEV * r, w), "cols": (r, N_DEV * w), "lead": (N_DEV, r, w)}[kind]


class _GatherComm:
    mid_fraction = 0.85

    def __init__(self, items):
        self.ins = [shard for shard, _ in items]
        self.kinds = [kind for _, kind in items]
        n = len(items)
        self.out_shapes = [jax.ShapeDtypeStruct(_full_shape(k, a.shape), a.dtype) for a, k in items]
        self.sem_shapes = [pltpu.SemaphoreType.DMA((7 * n,)), pltpu.SemaphoreType.DMA((7 * n,)),
                           pltpu.SemaphoreType.DMA((n,))]
        self.results = None

    def _copy(self, refs, a, k, block, to, from_input=False):
        ins, outs, (send_sems, recv_sems, _) = refs
        px, py, pc = block
        dst = _owner_view(outs[a], self.kinds[a], self.ins[a].shape, 4 * px + 2 * py + pc)
        return pltpu.make_async_remote_copy(
            src_ref=ins[a] if from_input else dst, dst_ref=dst,
            send_sem=send_sems.at[7 * a + k], recv_sem=recv_sems.at[7 * a + k], device_id=to, device_id_type=MESH)

    def _local(self, refs, a):
        ins, outs, (_, _, local_sems) = refs
        x, y, c = _position()
        dst = _owner_view(outs[a], self.kinds[a], self.ins[a].shape, 4 * x + 2 * y + c)
        return pltpu.make_async_copy(ins[a], dst, local_sems.at[a])

    def start(self, *refs):
        x, y, c = _position()
        me, sibling = (x, y, c), (x, y, 1 - c)
        chips = [(1 - x, y), (x, 1 - y), (1 - x, 1 - y)]
        for a in range(len(self.ins)):
            self._local(refs, a).start()
            for j, chip in enumerate(chips):
                self._copy(refs, a, 1 + j, me, (*chip, c), from_input=True).start()
            self._copy(refs, a, 0, me, sibling, from_input=True).start()

    def mid(self, *refs):
        x, y, c = _position()
        chips = [(1 - x, y), (x, 1 - y), (1 - x, 1 - y)]
        for a in range(len(self.ins)):
            for j, chip in enumerate(chips):
                self._copy(refs, a, 1 + j, (*chip, c), (x, y, c)).wait_recv()
                self._copy(refs, a, 4 + j, (*chip, c), (x, y, 1 - c)).start()

    def finish(self, *refs):
        x, y, c = _position()
        me, sibling = (x, y, c), (x, y, 1 - c)
        chips = [(1 - x, y), (x, 1 - y), (1 - x, 1 - y)]
        for a in range(len(self.ins)):
            self._copy(refs, a, 0, sibling, me).wait_recv()
            for j, chip in enumerate(chips):
                self._copy(refs, a, 4 + j, (*chip, 1 - c), me).wait_recv()
        for a in range(len(self.ins)):
            self._copy(refs, a, 0, me, sibling, from_input=True).wait_send()
            for j, chip in enumerate(chips):
                self._copy(refs, a, 1 + j, me, (*chip, c), from_input=True).wait_send()
                self._copy(refs, a, 4 + j, (*chip, c), sibling).wait_send()
            self._local(refs, a).wait()


class _SiblingComm:
    mid_fraction = None

    def __init__(self, items):
        self.ins = [p for p, _, _ in items]
        self.kinds = [kind for _, kind, _ in items]
        self.shapes = [shape for _, _, shape in items]
        n = len(items)
        self.out_shapes = [jax.ShapeDtypeStruct((4,) + tuple(shape), p.dtype) for p, _, shape in items]
        self.sem_shapes = [pltpu.SemaphoreType.DMA((4 * n,)), pltpu.SemaphoreType.DMA((4 * n,))]
        self.results = None

    def _copies(self, refs):
        ins, outs, (send_sems, recv_sems) = refs
        x, y, c = _position()
        return [pltpu.make_async_remote_copy(
            src_ref=_owner_view(ins[a], self.kinds[a], self.shapes[a], 2 * k + 1 - c), dst_ref=outs[a].at[k],
            send_sem=send_sems.at[4 * a + k], recv_sem=recv_sems.at[4 * a + k],
            device_id=(x, y, 1 - c), device_id_type=MESH) for a in range(len(self.ins)) for k in range(4)]

    def start(self, *refs):
        for cp in self._copies(refs):
            cp.start()

    def finish(self, *refs):
        for cp in self._copies(refs):
            cp.wait()


class _ChipComm:
    mid_fraction = None

    def __init__(self, items):
        self.ins = list(items)
        n = len(items)
        self.out_shapes = [jax.ShapeDtypeStruct(p.shape, p.dtype) for p in items]
        self.sem_shapes = [pltpu.SemaphoreType.DMA((3 * n,)), pltpu.SemaphoreType.DMA((3 * n,)),
                           pltpu.SemaphoreType.DMA((n,))]
        self.results = None

    def _copy(self, refs, a, k, landed):
        ins, outs, (send_sems, recv_sems, _) = refs
        x, y, c = _position()
        px = 1 - x if k & 2 else x
        py = 1 - y if k & 1 else y
        to_chip, my_chip = 2 * px + py, 2 * x + y
        return pltpu.make_async_remote_copy(
            src_ref=ins[a].at[to_chip], dst_ref=outs[a].at[to_chip if landed else my_chip],
            send_sem=send_sems.at[3 * a + k - 1], recv_sem=recv_sems.at[3 * a + k - 1],
            device_id=(px, py, c), device_id_type=MESH)

    def _local(self, refs, a):
        ins, outs, (_, _, local_sems) = refs
        x, y, _ = _position()
        return pltpu.make_async_copy(ins[a].at[2 * x + y], outs[a].at[2 * x + y], local_sems.at[a])

    def start(self, *refs):
        for a in range(len(self.ins)):
            self._local(refs, a).start()
            for k in range(1, 4):
                self._copy(refs, a, k, False).start()

    def finish(self, *refs):
        for a in range(len(self.ins)):
            for k in range(1, 4):
                self._copy(refs, a, k, True).wait_recv()
        for a in range(len(self.ins)):
            for k in range(1, 4):
                self._copy(refs, a, k, False).wait_send()
            self._local(refs, a).wait()


def _run_comm(name, comm):
    n_in, n_out = len(comm.ins), len(comm.out_shapes)

    def body(*refs):
        parts = (refs[:n_in], refs[n_in:n_in + n_out], refs[n_in + n_out:])
        comm.start(*parts)
        if comm.mid_fraction is not None:
            comm.mid(*parts)
        comm.finish(*parts)

    any_spec = pl.BlockSpec(memory_space=pl.ANY)
    return pl.pallas_call(
        body, name=name, out_shape=list(comm.out_shapes), in_specs=[any_spec] * n_in, out_specs=[any_spec] * n_out,
        scratch_shapes=list(comm.sem_shapes))(*comm.ins)


def _carry(comm, body, grid, in_specs, out_specs, out_shapes, scratch_shapes, operands):
    if comm is None:
        return body, in_specs, out_specs, out_shapes, scratch_shapes, operands
    n_in, n_out, n_scr = len(in_specs), len(out_specs), len(scratch_shapes)
    c_in, c_out = len(comm.ins), len(comm.out_shapes)
    total = math.prod(grid)
    mid_step = None if comm.mid_fraction is None else min(total - 1, int(total * comm.mid_fraction))

    def wrapped(*refs):
        ins, c_ins = refs[:n_in], refs[n_in:n_in + c_in]
        outs = refs[n_in + c_in:n_in + c_in + n_out]
        c_outs = refs[n_in + c_in + n_out:n_in + c_in + n_out + c_out]
        scr = refs[n_in + c_in + n_out + c_out:n_in + c_in + n_out + c_out + n_scr]
        parts = (c_ins, c_outs, refs[n_in + c_in + n_out + c_out + n_scr:])
        step = pl.program_id(0)
        for axis in range(1, len(grid)):
            step = step * grid[axis] + pl.program_id(axis)

        @pl.when(step == 0)
        def _():
            comm.start(*parts)

        body(*ins, *outs, *scr)

        if mid_step is not None:
            @pl.when(step == mid_step)
            def _():
                comm.mid(*parts)

        @pl.when(step == total - 1)
        def _():
            comm.finish(*parts)

    any_spec = pl.BlockSpec(memory_space=pl.ANY)
    return (wrapped, list(in_specs) + [any_spec] * c_in, list(out_specs) + [any_spec] * c_out,
            list(out_shapes) + list(comm.out_shapes), list(scratch_shapes) + list(comm.sem_shapes),
            list(operands) + list(comm.ins))


def _pair_add(name, partial, kind, shard_shape, theirs, core):
    r, w = shard_shape
    tb = _tile(r, max(16, (2**19 // w) // 16 * 16), 16)
    nb = r // tb
    if kind == "rows":
        mine = pl.BlockSpec((tb, w), lambda k, i, core_ref: ((2 * k + core_ref[0]) * nb + i, 0))
    elif kind == "cols":
        mine = pl.BlockSpec((tb, w), lambda k, i, core_ref: (i, 2 * k + core_ref[0]))
    else:
        mine = pl.BlockSpec((None, tb, w), lambda k, i, core_ref: (2 * k + core_ref[0], i, 0))
    slot = pl.BlockSpec((None, tb, w), lambda k, i, core_ref: (k, i, 0))

    def body(core_ref, a_ref, b_ref, o_ref):
        o_ref[...] = (a_ref[...].astype(F32) + b_ref[...].astype(F32)).astype(o_ref.dtype)

    return pl.pallas_call(
        body, name=name,
        grid_spec=pltpu.PrefetchScalarGridSpec(num_scalar_prefetch=1, grid=(4, nb), in_specs=[mine, slot],
                                               out_specs=slot),
        out_shape=jax.ShapeDtypeStruct(theirs.shape, theirs.dtype), compiler_params=_params(2))(core, partial, theirs)


def _matmul(name, mode, grid, a, a_spec, bs, b_specs, extras, extra_specs, out_shapes, out_specs, acc_shape,
            epilogue, a_fn=None, comm=None):
    nk, nb, ne, no = grid[2], len(bs), len(extras), len(out_shapes)

    def body(*refs):
        a_ref, b_refs, e_refs = refs[0], refs[1:1 + nb], refs[1 + nb:1 + nb + ne]
        o_refs, acc_refs = refs[1 + nb + ne:1 + nb + ne + no], refs[1 + nb + ne + no:]
        a_val = a_ref[...]
        if a_fn is not None:
            a_val = a_fn(a_val)
        a_val = a_val.astype(BF16)

        def product(b_ref):
            return lax.dot_general(a_val, b_ref[...].astype(BF16), _DN[mode], preferred_element_type=F32)

        def finish(accs):
            outs = epilogue(accs, [e[...] for e in e_refs])
            for o_ref, o in zip(o_refs, outs):
                o_ref[...] = o.astype(o_ref.dtype)

        if nk == 1:
            finish([product(b_ref) for b_ref in b_refs])
        else:
            kk = pl.program_id(2)

            @pl.when(kk == 0)
            def _():
                for acc in acc_refs:
                    acc[...] = jnp.zeros_like(acc)

            for acc, b_ref in zip(acc_refs, b_refs):
                acc[...] += product(b_ref)

            @pl.when(kk == nk - 1)
            def _():
                finish([acc[...] for acc in acc_refs])

    scratch = [] if nk == 1 else [pltpu.VMEM(acc_shape, F32) for _ in range(nb)]
    body, in_specs, out_specs, out_shape, scratch, operands = _carry(
        comm, body, grid, [a_spec] + list(b_specs) + list(extra_specs), list(out_specs), list(out_shapes), scratch,
        [a, *bs, *extras])
    outs = pl.pallas_call(
        body, name=name, grid=grid, in_specs=in_specs, out_specs=out_specs, out_shape=out_shape,
        scratch_shapes=scratch, compiler_params=_params(3))(*operands)
    if comm is not None:
        comm.results = outs[no:]
    return outs[:no]


def _mm_tiles(m, n, k, tiles):
    return _tile(m, tiles[0], 16), _tile(n, tiles[1], LANES), _tile(k, tiles[2], LANES)


_FFN_BEST = {"in": (1024, 512, 2048), "out": (1024, 512, 5632), "out_bwd": (2048, 512, 2048),
             "in_bwd": (1024, 1024, 2816), "out_dw": (512, 2048, 2048), "in_dw": (1024, 512, 4096)}
FFN_TILES = {"l0a": _FFN_BEST, "l0b": _FFN_BEST, "l1a": _FFN_BEST,
             "l1b": dict(_FFN_BEST, in_bwd=(1024, 2048, 1408))}
FULL_K = (1024, 1024, 2048)
LOOP_K = (1024, 1024, 2048)


def _mm_nn(name, a, b, out_dtype, b_col0=0, n=None, tiles=FULL_K, comm=None):
    m, k = a.shape
    n = b.shape[1] if n is None else n
    tm, tn, tk = _mm_tiles(m, n, k, tiles)
    off = b_col0 // tn
    assert b_col0 % tn == 0
    return _matmul(
        name, "nn", (m // tm, n // tn, k // tk), a, pl.BlockSpec((tm, tk), lambda i, j, kk: (i, kk)),
        [b], [pl.BlockSpec((tk, tn), lambda i, j, kk: (kk, j + off))], [], [],
        [jax.ShapeDtypeStruct((m, n), out_dtype)], [pl.BlockSpec((tm, tn), lambda i, j, kk: (i, j))], (tm, tn),
        lambda accs, ex: accs, comm=comm)[0]


def _mm_bias_rows(name, a, b, bias, a_fn):
    m, k = a.shape
    n = b.shape[1]
    tn = _tile(n, 768, LANES)
    return _matmul(
        name, "nn", (1, n // tn, 1), a, pl.BlockSpec((m, k), lambda i, j, kk: (0, 0)),
        [b], [pl.BlockSpec((k, tn), lambda i, j, kk: (0, j))],
        [bias], [pl.BlockSpec((1, tn), lambda i, j, kk: (0, j))],
        [jax.ShapeDtypeStruct((m, n), F32)], [pl.BlockSpec((m, tn), lambda i, j, kk: (0, j))], (m, tn),
        lambda accs, ex: [accs[0] + ex[0]], a_fn=a_fn)[0]


def _mm_swiglu(name, h, w_in, f, tiles, comm=None):
    s, d = h.shape
    tm, tn, tk = _mm_tiles(s, f, d, tiles)
    nf = f // tn

    def epilogue(accs, ex):
        a_pre, b_pre = accs
        return [jnp.stack([a_pre, b_pre]), _silu(a_pre) * b_pre]

    return _matmul(
        name, "nn", (s // tm, nf, d // tk), h, pl.BlockSpec((tm, tk), lambda i, j, kk: (i, kk)),
        [w_in, w_in], [pl.BlockSpec((tk, tn), lambda i, j, kk: (kk, j)),
                       pl.BlockSpec((tk, tn), lambda i, j, kk: (kk, j + nf))], [], [],
        [jax.ShapeDtypeStruct((2, s, f), BF16), jax.ShapeDtypeStruct((s, f), BF16)],
        [pl.BlockSpec((2, tm, tn), lambda i, j, kk: (0, i, j)), pl.BlockSpec((tm, tn), lambda i, j, kk: (i, j))],
        (tm, tn), epilogue, comm=comm)


def _mm_residual(name, a, w, x_in, gvec, tiles=FULL_K, comm=None):
    s, k = a.shape
    d = w.shape[1]
    tm, tn, tk = _mm_tiles(s, d, k, tiles)

    def epilogue(accs, ex):
        x_blk, g_row = ex
        return [accs[0], x_blk + g_row * accs[0]]

    tile = pl.BlockSpec((tm, tn), lambda i, j, kk: (i, j))
    return _matmul(
        name, "nn", (s // tm, d // tn, k // tk), a, pl.BlockSpec((tm, tk), lambda i, j, kk: (i, kk)),
        [w], [pl.BlockSpec((tk, tn), lambda i, j, kk: (kk, j))],
        [x_in, gvec], [tile, pl.BlockSpec((1, tn), lambda i, j, kk: (0, j))],
        [jax.ShapeDtypeStruct((s, d), F32), jax.ShapeDtypeStruct((s, d), F32)], [tile, tile], (tm, tn), epilogue,
        comm=comm)


def _mm_nt(name, a, w, out_dtype, w_col0=0, extra_add=None, tiles=FULL_K):
    s, k = a.shape
    n = w.shape[0]
    tm, tn, tk = _mm_tiles(s, n, k, tiles)
    off = w_col0 // tk
    assert w_col0 % tk == 0
    tile = pl.BlockSpec((tm, tn), lambda i, j, kk: (i, j))
    extras, especs = ([], []) if extra_add is None else ([extra_add], [tile])
    return _matmul(
        name, "nt", (s // tm, n // tn, k // tk), a, pl.BlockSpec((tm, tk), lambda i, j, kk: (i, kk)),
        [w], [pl.BlockSpec((tn, tk), lambda i, j, kk: (j, kk + off))], extras, especs,
        [jax.ShapeDtypeStruct((s, n), out_dtype)], [tile], (tm, tn),
        lambda accs, ex: [accs[0] + ex[0]] if ex else accs)[0]


def _mm_nt_parts(name, a3, w, out_dtype, tiles=LOOP_K, comm=None):
    p, s, k = a3.shape
    n = w.shape[0]
    tm, tn, tk = _mm_tiles(s, n, k, tiles)
    nkp = k // tk
    return _matmul(
        name, "nt", (s // tm, n // tn, p * nkp), a3,
        pl.BlockSpec((None, tm, tk), lambda i, j, kk: (kk // nkp, i, kk % nkp)),
        [w], [pl.BlockSpec((tn, tk), lambda i, j, kk: (j, kk))], [], [],
        [jax.ShapeDtypeStruct((s, n), out_dtype)], [pl.BlockSpec((tm, tn), lambda i, j, kk: (i, j))], (tm, tn),
        lambda accs, ex: accs, comm=comm)[0]


def _mm_nt_swiglu_bwd(name, dy, w_out, ab, tiles):
    s, d = dy.shape
    f = w_out.shape[0]
    tm, tn, tk = _mm_tiles(s, f, d, tiles)

    def epilogue(accs, ex):
        du, a_pre, b_pre = accs[0], ex[0][0].astype(F32), ex[0][1].astype(F32)
        sig = jax.nn.sigmoid(a_pre)
        da = du * b_pre * (sig * (1.0 + a_pre * (1.0 - sig)))
        db = du * (a_pre * sig)
        return [jnp.stack([da, db])]

    pair = pl.BlockSpec((2, tm, tn), lambda i, j, kk: (0, i, j))
    return _matmul(
        name, "nt", (s // tm, f // tn, d // tk), dy, pl.BlockSpec((tm, tk), lambda i, j, kk: (i, kk)),
        [w_out], [pl.BlockSpec((tn, tk), lambda i, j, kk: (j, kk))], [ab], [pair],
        [jax.ShapeDtypeStruct((2, s, f), BF16)], [pair], (tm, tn), epilogue)[0]


def _mm_tn(name, a, b, out_dtype, a_fn=None, tiles=LOOP_K, comm=None):
    s, m = a.shape
    tm, tk = _tile(m, tiles[0], LANES), _tile(s, tiles[2], 16)
    if b.ndim == 3:
        npart = b.shape[2]
        n = b.shape[0] * npart
        tn = _tile(npart, tiles[1], LANES)
        nj = npart // tn
        b_spec = pl.BlockSpec((None, tk, tn), lambda i, j, kk: (j // nj, kk, j % nj))
    else:
        n = b.shape[1]
        tn = _tile(n, tiles[1], LANES)
        b_spec = pl.BlockSpec((tk, tn), lambda i, j, kk: (kk, j))
    return _matmul(
        name, "tn", (m // tm, n // tn, s // tk), a, pl.BlockSpec((tk, tm), lambda i, j, kk: (kk, i)),
        [b], [b_spec], [], [],
        [jax.ShapeDtypeStruct((m, n), out_dtype)], [pl.BlockSpec((tm, tn), lambda i, j, kk: (i, j))], (tm, tn),
        lambda accs, ex: accs, a_fn=a_fn, comm=comm)[0]


def _row_tile(s):
    return _tile(s, 256, 16)


def _fold_rows(v):
    return jnp.sum(v.reshape(v.shape[0] // SUBLANES, SUBLANES, v.shape[1]), axis=0)


def _accumulate(i, acc_ref, val):
    @pl.when(i == 0)
    def _():
        acc_ref[...] = val

    @pl.when(i > 0)
    def _():
        acc_ref[...] += val


def _norm_mod(name, x, g, shift, scale):
    s, d = x.shape
    tb = _row_tile(s)

    def body(x_ref, g_ref, sh_ref, sc_ref, h_ref):
        xv = x_ref[...]
        rstd = lax.rsqrt(jnp.mean(xv * xv, axis=-1, keepdims=True) + EPS)
        h_ref[...] = ((xv * rstd) * g_ref[...] * (1.0 + sc_ref[...]) + sh_ref[...]).astype(BF16)

    row = pl.BlockSpec((tb, d), lambda i: (i, 0))
    vec = pl.BlockSpec((1, d), lambda i: (0, 0))
    return pl.pallas_call(body, name=name, grid=(s // tb,), in_specs=[row, vec, vec, vec], out_specs=row,
                          out_shape=jax.ShapeDtypeStruct((s, d), BF16), compiler_params=_params(1))(x, g, shift, scale)


def _rms_bwd(xv, dn_g, d):
    rstd = lax.rsqrt(jnp.mean(xv * xv, axis=-1, keepdims=True) + EPS)
    xhat = xv * rstd
    dx = rstd * (dn_g - xhat * (jnp.sum(dn_g * xhat, axis=-1, keepdims=True) * (1.0 / d)))
    return xhat, dx


def _norm_mod_bwd(name, dh, x, g, scale, dx_in):
    s, d = x.shape
    tb = _row_tile(s)
    nsteps = s // tb

    def body(dh_ref, x_ref, g_ref, sc_ref, dxin_ref, dx_ref, dsh_ref, dsc_ref, dg_ref, a_sh, a_sc, a_g):
        i = pl.program_id(0)
        dhv, gv = dh_ref[...], g_ref[...]
        dn = dhv * (1.0 + sc_ref[...])
        xhat, dx = _rms_bwd(x_ref[...], dn * gv, d)
        dx_ref[...] = dxin_ref[...] + dx
        _accumulate(i, a_sh, _fold_rows(dhv))
        _accumulate(i, a_sc, _fold_rows(dhv * (xhat * gv)))
        _accumulate(i, a_g, _fold_rows(dn * xhat))

        @pl.when(i == nsteps - 1)
        def _():
            dsh_ref[...] = jnp.sum(a_sh[...], axis=0, keepdims=True)
            dsc_ref[...] = jnp.sum(a_sc[...], axis=0, keepdims=True)
            dg_ref[...] = jnp.sum(a_g[...], axis=0, keepdims=True)

    row = pl.BlockSpec((tb, d), lambda i: (i, 0))
    vec = pl.BlockSpec((1, d), lambda i: (0, 0))
    vshape = jax.ShapeDtypeStruct((1, d), F32)
    return pl.pallas_call(
        body, name=name, grid=(nsteps,), in_specs=[row, row, vec, vec, row], out_specs=[row, vec, vec, vec],
        out_shape=[jax.ShapeDtypeStruct((s, d), F32), vshape, vshape, vshape],
        scratch_shapes=[pltpu.VMEM((SUBLANES, d), F32)] * 3, compiler_params=_params(1))(dh, x, g, scale, dx_in)


def _gate_bwd(name, dx, y, gvec, coef):
    s, d = dx.shape
    tb = _row_tile(s)
    nsteps = s // tb

    def body(dx_ref, y_ref, g_ref, dy_ref, dg_ref, acc):
        i = pl.program_id(0)
        dxv = dx_ref[...]
        dy_ref[...] = (dxv * g_ref[...]).astype(BF16)
        _accumulate(i, acc, _fold_rows(dxv * y_ref[...]))

        @pl.when(i == nsteps - 1)
        def _():
            dg_ref[...] = coef * jnp.sum(acc[...], axis=0, keepdims=True)

    row = pl.BlockSpec((tb, d), lambda i: (i, 0))
    vec = pl.BlockSpec((1, d), lambda i: (0, 0))
    return pl.pallas_call(
        body, name=name, grid=(nsteps,), in_specs=[row, row, vec], out_specs=[row, vec],
        out_shape=[jax.ShapeDtypeStruct((s, d), BF16), jax.ShapeDtypeStruct((1, d), F32)],
        scratch_shapes=[pltpu.VMEM((SUBLANES, d), F32)], compiler_params=_params(1))(dx, y, gvec)


def _final_loss(name, x, target, g):
    s, d = x.shape
    tb = _row_tile(s)
    nsteps = s // tb

    def body(x_ref, t_ref, g_ref, dx_ref, loss_ref, dg_ref, a_loss, a_g):
        i = pl.program_id(0)
        xv, gv = x_ref[...], g_ref[...]
        rstd = lax.rsqrt(jnp.mean(xv * xv, axis=-1, keepdims=True) + EPS)
        xhat = xv * rstd
        err = xhat * gv - t_ref[...]
        dyv = err * (1.0 / d)
        dn_g = dyv * gv
        dx_ref[...] = rstd * (dn_g - xhat * (jnp.sum(dn_g * xhat, axis=-1, keepdims=True) * (1.0 / d)))
        _accumulate(i, a_loss, _fold_rows(err * err))
        _accumulate(i, a_g, _fold_rows(dyv * xhat))

        @pl.when(i == nsteps - 1)
        def _():
            total = jnp.sum(jnp.sum(a_loss[...], axis=0, keepdims=True), axis=1, keepdims=True) * (0.5 / d)
            loss_ref[...] = jnp.broadcast_to(total, (1, LANES))
            dg_ref[...] = jnp.sum(a_g[...], axis=0, keepdims=True)

    row = pl.BlockSpec((tb, d), lambda i: (i, 0))
    vec = pl.BlockSpec((1, d), lambda i: (0, 0))
    return pl.pallas_call(
        body, name=name, grid=(nsteps,), in_specs=[row, row, vec],
        out_specs=[row, pl.BlockSpec((1, LANES), lambda i: (0, 0)), vec],
        out_shape=[jax.ShapeDtypeStruct((s, d), F32), jax.ShapeDtypeStruct((1, LANES), F32),
                   jax.ShapeDtypeStruct((1, d), F32)],
        scratch_shapes=[pltpu.VMEM((SUBLANES, d), F32)] * 2, compiler_params=_params(1))(x, target, g)


def _shift_rows(cur, halo, n):
    rolled = pltpu.roll(cur, n, 0)
    row = lax.broadcasted_iota(jnp.int32, cur.shape, 0)
    for r in range(n):
        rolled = jnp.where(row == r, halo[SUBLANES - n + r:SUBLANES - n + r + 1, :], rolled)
    return rolled


def _shift_rows_up(cur, halo, n):
    tb = cur.shape[0]
    rolled = pltpu.roll(cur, tb - n, 0)
    row = lax.broadcasted_iota(jnp.int32, cur.shape, 0)
    for r in range(n):
        rolled = jnp.where(row == tb - n + r, halo[r:r + 1, :], rolled)
    return rolled


def _conv_specs(s, d, tb, tn):
    nd = d // tn
    hb = tb // SUBLANES
    cur = lambda part: pl.BlockSpec((tb, tn), lambda i, j: (i, j + part * nd))
    prev = lambda part: pl.BlockSpec((SUBLANES, tn), lambda i, j: (jnp.maximum(i * hb - 1, 0), j + part * nd))
    nxt = lambda part: pl.BlockSpec((SUBLANES, tn), lambda i, j: (jnp.minimum((i + 1) * hb, s // SUBLANES - 1),
                                                                  j + part * nd))
    return cur, prev, nxt


def _conv_fwd(name, p, conv_w, conv_b):
    s, d = p.shape[0], p.shape[1] // 3
    tb, tn = _row_tile(s), _tile(d, 512, LANES)
    cur, prev, _ = _conv_specs(s, d, tb, tn)

    def body(bg_ref, cg_ref, xv_ref, cgp_ref, xvp_ref, w_ref, b_ref, z_ref):
        i = pl.program_id(0)
        u = cg_ref[...] * xv_ref[...]
        up = jnp.where(i > 0, cgp_ref[...] * xvp_ref[...], 0.0)
        w = w_ref[...]
        uc = w[0:1, :] * _shift_rows(u, up, 2) + w[1:2, :] * _shift_rows(u, up, 1) + w[2:3, :] * u + b_ref[...]
        z_ref[...] = (bg_ref[...] * uc).astype(BF16)

    return pl.pallas_call(
        body, name=name, grid=(s // tb, d // tn),
        in_specs=[cur(0), cur(1), cur(2), prev(1), prev(2), pl.BlockSpec((3, tn), lambda i, j: (0, j)),
                  pl.BlockSpec((1, tn), lambda i, j: (0, j))],
        out_specs=pl.BlockSpec((tb, tn), lambda i, j: (i, j)),
        out_shape=jax.ShapeDtypeStruct((s, d), BF16), compiler_params=_params(2))(p, p, p, p, p, conv_w, conv_b)


def _conv_bwd(name, dz, p, conv_w, conv_b):
    s, d = dz.shape
    tb, tn = _row_tile(s), _tile(d, 512, LANES)
    nsteps = s // tb
    cur, prev, nxt = _conv_specs(s, d, tb, tn)

    def body(dz_ref, bg_ref, cg_ref, xv_ref, cgp_ref, xvp_ref, dzn_ref, bgn_ref, w_ref, b_ref,
             dp_ref, dw_ref, db_ref, a_w0, a_w1, a_w2, a_b):
        j, i = pl.program_id(0), pl.program_id(1)
        cg, xv, bg, dzv = cg_ref[...], xv_ref[...], bg_ref[...], dz_ref[...]
        u = cg * xv
        up = jnp.where(i > 0, cgp_ref[...] * xvp_ref[...], 0.0)
        w = w_ref[...]
        u1, u2 = _shift_rows(u, up, 1), _shift_rows(u, up, 2)
        uc = w[0:1, :] * u2 + w[1:2, :] * u1 + w[2:3, :] * u + b_ref[...]
        duc = dzv * bg
        ducn = jnp.where(i < nsteps - 1, dzn_ref[...] * bgn_ref[...], 0.0)
        du = w[2:3, :] * duc + w[1:2, :] * _shift_rows_up(duc, ducn, 1) + w[0:1, :] * _shift_rows_up(duc, ducn, 2)
        dp_ref[0] = (dzv * uc).astype(BF16)
        dp_ref[1] = (du * xv).astype(BF16)
        dp_ref[2] = (du * cg).astype(BF16)
        _accumulate(i, a_w0, _fold_rows(duc * u2))
        _accumulate(i, a_w1, _fold_rows(duc * u1))
        _accumulate(i, a_w2, _fold_rows(duc * u))
        _accumulate(i, a_b, _fold_rows(duc))

        @pl.when(i == nsteps - 1)
        def _():
            dw_ref[0:1, :] = jnp.sum(a_w0[...], axis=0, keepdims=True)
            dw_ref[1:2, :] = jnp.sum(a_w1[...], axis=0, keepdims=True)
            dw_ref[2:3, :] = jnp.sum(a_w2[...], axis=0, keepdims=True)
            db_ref[...] = jnp.sum(a_b[...], axis=0, keepdims=True)

    swap = lambda spec: pl.BlockSpec(spec.block_shape, lambda j, i, _m=spec.index_map: _m(i, j))
    return pl.pallas_call(
        body, name=name, grid=(d // tn, nsteps),
        in_specs=[pl.BlockSpec((tb, tn), lambda j, i: (i, j)), swap(cur(0)), swap(cur(1)), swap(cur(2)),
                  swap(prev(1)), swap(prev(2)), swap(nxt(0)), swap(nxt(0)),
                  pl.BlockSpec((3, tn), lambda j, i: (0, j)), pl.BlockSpec((1, tn), lambda j, i: (0, j))],
        out_specs=[pl.BlockSpec((3, tb, tn), lambda j, i: (0, i, j)), pl.BlockSpec((3, tn), lambda j, i: (0, j)),
                   pl.BlockSpec((1, tn), lambda j, i: (0, j))],
        out_shape=[jax.ShapeDtypeStruct((3, s, d), BF16), jax.ShapeDtypeStruct((3, d), F32),
                   jax.ShapeDtypeStruct((1, d), F32)],
        scratch_shapes=[pltpu.VMEM((SUBLANES, tn), F32)] * 4, compiler_params=_params(2),
    )(dz, p, p, p, p, p, dz, p, conv_w, conv_b)


def _block_cumsum(v):
    tb = v.shape[0]
    row = lax.broadcasted_iota(jnp.int32, v.shape, 0)
    sh = 1
    while sh < tb:
        v = v + jnp.where(row >= sh, pltpu.roll(v, sh, 0), 0.0)
        sh *= 2
    return v


def _fgate_fwd(name, zf, b_f):
    s = zf.shape[0]
    tb = _tile(s, 512, SUBLANES)

    def body(z_ref, b_ref, f_ref, ft_ref, carry):
        i = pl.program_id(0)
        z = z_ref[...] + b_ref[...]
        ls = jnp.minimum(z, 0.0) - jnp.log(1.0 + jnp.exp(-jnp.abs(z)))
        run = _block_cumsum(ls)

        @pl.when(i == 0)
        def _():
            carry[...] = jnp.zeros_like(carry)

        out = run + carry[...]
        f_ref[...] = out
        ft_ref[...] = jnp.transpose(out)
        carry[...] = out[tb - 1:tb, :]

    blk = pl.BlockSpec((tb, LANES), lambda i: (i, 0))
    return pl.pallas_call(
        body, name=name, grid=(s // tb,), in_specs=[blk, pl.BlockSpec((1, LANES), lambda i: (0, 0))],
        out_specs=[blk, pl.BlockSpec((LANES, tb), lambda i: (0, i))],
        out_shape=[jax.ShapeDtypeStruct((s, LANES), F32), jax.ShapeDtypeStruct((LANES, s), F32)],
        scratch_shapes=[pltpu.VMEM((1, LANES), F32)], compiler_params=_params(1))(zf, b_f)


def _fgate_bwd(name, dfcum, zf, b_f):
    s = zf.shape[0]
    tb = _tile(s, 512, SUBLANES)
    nsteps = s // tb

    def body(df_ref, z_ref, b_ref, dz_ref, db_ref, carry, acc):
        i = pl.program_id(0)
        df = df_ref[...]
        incl = _block_cumsum(df)
        total = incl[tb - 1:tb, :]

        @pl.when(i == 0)
        def _():
            carry[...] = jnp.zeros_like(carry)

        suffix = total - incl + df + carry[...]
        carry[...] += total
        dzv = suffix * jax.nn.sigmoid(-(z_ref[...] + b_ref[...]))
        dz_ref[...] = dzv
        _accumulate(i, acc, _fold_rows(dzv))

        @pl.when(i == nsteps - 1)
        def _():
            db_ref[...] = jnp.sum(acc[...], axis=0, keepdims=True)

    blk = pl.BlockSpec((tb, LANES), lambda i: (nsteps - 1 - i, 0))
    vec = pl.BlockSpec((1, LANES), lambda i: (0, 0))
    return pl.pallas_call(
        body, name=name, grid=(nsteps,), in_specs=[blk, blk, vec], out_specs=[blk, vec],
        out_shape=[jax.ShapeDtypeStruct((s, LANES), F32), jax.ShapeDtypeStruct((1, LANES), F32)],
        scratch_shapes=[pltpu.VMEM((1, LANES), F32), pltpu.VMEM((SUBLANES, LANES), F32)],
        compiler_params=_params(1))(dfcum, zf, b_f)


def _lane_column(block, lane):
    sel = lax.broadcasted_iota(jnp.int32, block.shape, 1) == lane
    return jnp.sum(jnp.where(sel, block, 0.0), axis=1, keepdims=True)


def _col_to_row(col):
    return jnp.transpose(jnp.broadcast_to(col, (col.shape[0], LANES)))[0:1, :]


def _attn_tiles(s):
    t = _tile(s, 512, LANES)
    return t, t


def _attn_fwd(name, q, kv, fcum_t, n_heads):
    s, d = q.shape
    dh = d // n_heads
    tq, tk = _attn_tiles(s)
    scale2 = LOG2E / math.sqrt(dh)

    def body(q_ref, k_ref, v_ref, ft_ref, o_ref, l_ref):
        qi = pl.program_id(1)
        qv = q_ref[...]

        def step(kj, carry, diagonal):
            m, l, acc = carry
            start = pl.multiple_of(kj * tk, tk)
            kb = k_ref[pl.ds(start, tk), :]
            vb = v_ref[pl.ds(start, tk), :]
            sc = lax.dot_general(qv, kb, _DN["nt"], preferred_element_type=F32) * scale2
            sc = sc - ft_ref[:, pl.ds(start, tk)] * LOG2E
            if diagonal:
                keep = lax.broadcasted_iota(jnp.int32, sc.shape, 0) >= lax.broadcasted_iota(jnp.int32, sc.shape, 1)
                sc = jnp.where(keep, sc, -jnp.inf)
            m_new = jnp.maximum(m, jnp.max(sc, axis=1, keepdims=True))
            alpha = jnp.exp2(m - m_new)
            pr = jnp.exp2(sc - m_new)
            l = alpha * l + jnp.sum(pr, axis=1, keepdims=True)
            acc = alpha * acc + lax.dot_general(pr.astype(BF16), vb, _DN["nn"], preferred_element_type=F32)
            return m_new, l, acc

        init = (jnp.full((tq, 1), -jnp.inf, F32), jnp.zeros((tq, 1), F32), jnp.zeros((tq, dh), F32))
        carry = lax.fori_loop(0, qi, lambda kj, cr: step(kj, cr, False), init)
        m, l, acc = step(qi, carry, True)
        o_ref[...] = (acc / l).astype(BF16)
        l_ref[...] = _col_to_row(m + jnp.log2(l))

    nh = n_heads
    return pl.pallas_call(
        body, name=name, grid=(nh, s // tq),
        in_specs=[pl.BlockSpec((tq, dh), lambda h, i: (i, h)),
                  pl.BlockSpec((s, dh), lambda h, i: (0, h)),
                  pl.BlockSpec((s, dh), lambda h, i: (0, nh + h)),
                  pl.BlockSpec((None, 1, s), lambda h, i: (h, 0, 0))],
        out_specs=[pl.BlockSpec((tq, dh), lambda h, i: (i, h)),
                   pl.BlockSpec((None, 1, tq), lambda h, i: (h, 0, i))],
        out_shape=[jax.ShapeDtypeStruct((s, d), BF16), jax.ShapeDtypeStruct((nh, 1, s), F32)],
        compiler_params=_params(2))(q, kv, kv, fcum_t)


def _attn_delta(name, d_o, o, n_heads):
    s, d = d_o.shape
    dh = d // n_heads
    tb = _row_tile(s)

    def body(do_ref, o_ref, dlt_ref, dob_ref):
        dov = do_ref[...]
        prod = dov * o_ref[...].astype(F32)
        lane = lax.broadcasted_iota(jnp.int32, (tb, LANES), 1)
        out = jnp.zeros((tb, LANES), F32)
        for h in range(n_heads):
            out = jnp.where(lane == h, jnp.sum(prod[:, h * dh:(h + 1) * dh], axis=1, keepdims=True), out)
        dlt_ref[...] = jnp.transpose(out)
        dob_ref[...] = dov.astype(BF16)

    row = pl.BlockSpec((tb, d), lambda i: (i, 0))
    return pl.pallas_call(
        body, name=name, grid=(s // tb,), in_specs=[row, row],
        out_specs=[pl.BlockSpec((LANES, tb), lambda i: (0, i)), row],
        out_shape=[jax.ShapeDtypeStruct((LANES, s), F32), jax.ShapeDtypeStruct((s, d), BF16)],
        compiler_params=_params(1))(d_o, o)


def _attn_bwd(name, q, kv, d_o, fcum, lse_t, delta_t, n_heads):
    s, d = q.shape
    dh = d // n_heads
    tq, tk = _attn_tiles(s)
    nq, nk = s // tq, s // tk
    scale = 1.0 / math.sqrt(dh)
    scale2 = LOG2E * scale

    def body(q_ref, k_ref, v_ref, do_ref, f_ref, lt_ref, dlt_ref, dq_ref, dkv_ref, dfq_ref, dfk_ref, dq_acc, dfq_acc):
        h, kj = pl.program_id(0), pl.program_id(1)
        kb, vb = k_ref[...], v_ref[...]
        fk2 = _lane_column(f_ref[...], h) * LOG2E

        @pl.when(kj == 0)
        def _():
            dq_acc[...] = jnp.zeros_like(dq_acc)
            dfq_acc[...] = jnp.zeros_like(dfq_acc)

        def step(qi, carry, diagonal):
            dk, dv, dfk = carry
            start = pl.multiple_of(qi * tq, tq)
            qb = q_ref[pl.ds(start, tq), :]
            dob = do_ref[pl.ds(start, tq), :]
            sc = lax.dot_general(kb, qb, _DN["nt"], preferred_element_type=F32) * scale2 - fk2
            pr = jnp.exp2(sc - lt_ref[:, pl.ds(start, tq)])
            if diagonal:
                keep = lax.broadcasted_iota(jnp.int32, sc.shape, 1) >= lax.broadcasted_iota(jnp.int32, sc.shape, 0)
                pr = jnp.where(keep, pr, 0.0)
            dv = dv + lax.dot_general(pr.astype(BF16), dob, _DN["nn"], preferred_element_type=F32)
            dp = lax.dot_general(vb, dob, _DN["nt"], preferred_element_type=F32)
            ds = pr * (dp - dlt_ref[:, pl.ds(start, tq)])
            ds_b = ds.astype(BF16)
            dk = dk + lax.dot_general(ds_b, qb, _DN["nn"], preferred_element_type=F32)
            dfk = dfk - jnp.sum(ds, axis=1, keepdims=True)
            dq_acc[pl.ds(start, tq), :] += lax.dot_general(ds_b, kb, _DN["tn"], preferred_element_type=F32)
            dfq_acc[:, pl.ds(start, tq)] += jnp.sum(ds, axis=0, keepdims=True)
            return dk, dv, dfk

        init = (jnp.zeros((tk, dh), F32), jnp.zeros((tk, dh), F32), jnp.zeros((tk, 1), F32))
        carry = step(kj, init, True)
        dk, dv, dfk = lax.fori_loop(kj + 1, nq, lambda qi, cr: step(qi, cr, False), carry)
        dkv_ref[0] = (dk * scale).astype(BF16)
        dkv_ref[1] = dv.astype(BF16)
        dfk_ref[...] = _col_to_row(dfk)

        @pl.when(kj == nk - 1)
        def _():
            dq_ref[...] = (dq_acc[...] * scale).astype(BF16)
            dfq_ref[...] = dfq_acc[...]

    nh = n_heads
    full = pl.BlockSpec((s, dh), lambda h, j: (0, h))
    row_t = pl.BlockSpec((None, 1, s), lambda h, j: (h, 0, 0))
    return pl.pallas_call(
        body, name=name, grid=(nh, nk),
        in_specs=[full, pl.BlockSpec((tk, dh), lambda h, j: (j, h)), pl.BlockSpec((tk, dh), lambda h, j: (j, nh + h)),
                  full, pl.BlockSpec((tk, LANES), lambda h, j: (j, 0)), row_t, row_t],
        out_specs=[full, pl.BlockSpec((2, tk, dh), lambda h, j: (0, j, h)), row_t,
                   pl.BlockSpec((None, 1, tk), lambda h, j: (h, 0, j))],
        out_shape=[jax.ShapeDtypeStruct((s, d), BF16), jax.ShapeDtypeStruct((2, s, d), BF16),
                   jax.ShapeDtypeStruct((nh, 1, s), F32), jax.ShapeDtypeStruct((nh, 1, s), F32)],
        scratch_shapes=[pltpu.VMEM((s, dh), F32), pltpu.VMEM((1, s), F32)],
        compiler_params=_params(2))(q, kv, kv, d_o, fcum, lse_t, delta_t)


def _sum_slots(name, g):
    n, r, lanes = g.shape

    def body(g_ref, o_ref):
        total = g_ref[0]
        for i in range(1, n):
            total = total + g_ref[i]
        o_ref[...] = total

    return pl.pallas_call(
        body, name=name, out_shape=jax.ShapeDtypeStruct((r, lanes), F32),
        in_specs=[pl.BlockSpec(memory_space=pltpu.VMEM)], out_specs=pl.BlockSpec(memory_space=pltpu.VMEM),
        compiler_params=pltpu.CompilerParams(vmem_limit_bytes=VMEM_LIMIT))(g)


def _adamw(name, parts, w, m, v, row0=0, previous=None):
    npart, r, c = parts.shape
    tb = _tile(math.gcd(r, row0) if row0 else r, max(SUBLANES, (2**18 // c) // SUBLANES * SUBLANES), SUBLANES)
    off = row0 // tb
    c1 = 1.0 - ADAM_B1 ** ADAM_STEP
    c2 = 1.0 - ADAM_B2 ** ADAM_STEP

    def body(p_ref, w_ref, m_ref, v_ref, *rest):
        g_out, d_out, m_out, v_out = rest[-4:]
        g = p_ref[0].astype(F32)
        for i in range(1, npart):
            g = g + p_ref[i].astype(F32)
        m_new = ADAM_B1 * m_ref[...] + (1.0 - ADAM_B1) * g
        v_new = ADAM_B2 * v_ref[...] + (1.0 - ADAM_B2) * (g * g)
        m_hat = m_new / c1
        v_hat = v_new / c2
        g_out[...] = g
        d_out[...] = -ADAM_LR * (m_hat / (jnp.sqrt(v_hat) + ADAM_EPS) + ADAM_WD * w_ref[...])
        m_out[...] = m_new
        v_out[...] = v_new

    blk = pl.BlockSpec((tb, c), lambda i: (i + off, 0))
    shape = jax.ShapeDtypeStruct(w.shape, F32)
    in_specs = [pl.BlockSpec((npart, tb, c), lambda i: (0, i, 0)), blk, blk, blk]
    operands = [parts, w, m, v]
    aliases = {}
    if previous is not None:
        in_specs += [pl.BlockSpec(memory_space=pl.ANY)] * 4
        operands += list(previous)
        aliases = {4 + k: k for k in range(4)}
    return pl.pallas_call(
        body, name=name, grid=(r // tb,), in_specs=in_specs, out_specs=[blk] * 4, out_shape=[shape] * 4,
        input_output_aliases=aliases, compiler_params=_params(1))(*operands)


def _pack(vectors):
    flat = jnp.concatenate([v.reshape(-1).astype(F32) for v in vectors])
    pad = (-flat.shape[0]) % (SUBLANES * LANES)
    return jnp.pad(flat, (0, pad)).reshape(-1, LANES)


def _unpack(flat, shapes):
    out, pos = [], 0
    for shp in shapes:
        size = math.prod(shp)
        out.append(flat[..., pos:pos + size].reshape(flat.shape[:-1] + tuple(shp)))
        pos += size
    return out


def _unshard_last(g, lead):
    nd = len(lead)
    return jnp.moveaxis(g, 0, nd).reshape(tuple(lead) + (-1,))


def _my_slice(full, me, width, axis):
    return lax.dynamic_slice_in_dim(full, me * width, width, axis)


def kernel(x, c, norm_g, w_ada, b_ada, w_ffn_in, w_ffn_out, w_conv_in, conv_w, conv_b, w_conv_out, kv_norm_g, w_ada_kv, b_ada_kv, w_kvf, b_fgate, w_q, w_o, final_g, loss_target, m_norm_g, m_w_ada, m_b_ada, m_w_ffn_in, m_w_ffn_out, m_w_conv_in, m_conv_w, m_conv_b, m_w_conv_out, m_kv_norm_g, m_w_ada_kv, m_b_ada_kv, m_w_kvf, m_b_fgate, m_w_q, m_w_o, m_final_g, v_norm_g, v_w_ada, v_b_ada, v_w_ffn_in, v_w_ffn_out, v_w_conv_in, v_conv_w, v_conv_b, v_w_conv_out, v_kv_norm_g, v_w_ada_kv, v_b_ada_kv, v_w_kvf, v_b_fgate, v_w_q, v_w_o, v_final_g):
    s, d = x.shape[1], x.shape[2]
    n_heads = b_fgate.shape[0]
    n_layers = w_ada.shape[0]
    dsh = d // N_DEV
    f = w_ffn_out.shape[2] * N_DEV
    ada_w = w_ada.shape[2]
    kv_w = w_ada_kv.shape[1]
    kvf_w = w_kvf.shape[1]
    assert n_layers == 2 and w_conv_in.shape[0] == 1 and w_q.shape[0] == 1
    assert (d // n_heads) % LANES == 0 and n_heads <= LANES and N_DEV * kvf_w == 2 * d + n_heads
    me = 4 * lax.axis_index("x") + 2 * lax.axis_index("y") + lax.axis_index("c")
    x0, target = x[0], loss_target[0]

    small = _allgather_small("gather_small_params", _pack([c, norm_g, conv_w, conv_b]))
    c_all, ng_sh, cw_sh, cb_sh = _unpack(small.reshape(N_DEV, -1), [(d,), (n_layers, 3, dsh), (3, dsh), (dsh,)])
    norm_g_full = _unshard_last(ng_sh, (n_layers, 3))
    conv_w_full = _unshard_last(cw_sh, (3,))
    conv_b_full = _unshard_last(cb_sh, ()).reshape(1, d)
    c_rows = jnp.pad(c_all, ((0, 16 - N_DEV), (0, 0)))

    ada_cols = [
        _mm_bias_rows(f"ada_rows_{l}", c_rows, w_ada[l], _my_slice(b_ada[l], me, ada_w, 0).reshape(1, ada_w), _silu)
        for l in range(n_layers)]
    ada_cols.append(_mm_bias_rows("ada_rows_kv", c_rows, w_ada_kv, _my_slice(b_ada_kv, me, kv_w, 0).reshape(1, kv_w),
                                  _silu))
    ada_part = jnp.concatenate([a[:N_DEV] for a in ada_cols], axis=1)
    ada_all = _allgather_small("gather_ada_rows", _pack([ada_part]))
    ada_all = ada_all.reshape(N_DEV, -1)[:, :ada_part.size].reshape(N_DEV, N_DEV, -1)
    ada_mine = lax.dynamic_index_in_dim(ada_all, me, axis=1, keepdims=False)
    ada = [ada_mine[:, l * ada_w:(l + 1) * ada_w].reshape(3, 3, 1, d) for l in range(n_layers)]
    ada_kv = ada_mine[:, n_layers * ada_w:].reshape(2, 1, d)

    def as_item(w, kind):
        return (w.astype(BF16), kind)

    gather_first = _GatherComm([as_item(w_ffn_in[0, 0], "cols")])
    gather_l0b = _GatherComm([as_item(w_ffn_out[0, 0], "rows"), as_item(w_ffn_in[0, 1], "cols")])
    gather_conv = _GatherComm([as_item(w_conv_in[0], "cols"), as_item(w_conv_out[0], "rows")])
    gather_l0b_out = _GatherComm([as_item(w_ffn_out[0, 1], "rows")])
    gather_l1a = _GatherComm([as_item(w_ffn_out[1, 0], "rows"), as_item(w_ffn_in[1, 0], "cols")])
    gather_attn = _GatherComm([as_item(w_q[0], "rows"), as_item(w_kvf, "lead"), as_item(w_o[0], "rows")])
    gather_l1b = _GatherComm([as_item(w_ffn_out[1, 1], "rows"), as_item(w_ffn_in[1, 1], "cols")])
    (wf_in00,) = _run_comm("gather_ffn_first", gather_first)
    b_f = jnp.pad(b_fgate, (0, LANES - n_heads)).reshape(1, LANES)

    def ffn_fwd(tag, xs, l, sub, w_in, w_out, comm_in, comm_out):
        shift, scale, gate = ada[l][sub]
        tiles = FFN_TILES[tag]
        h = _norm_mod(f"norm_{tag}", xs, norm_g_full[l, sub].reshape(1, d), shift, scale)
        ab, u = _mm_swiglu(f"ffn_in_{tag}", h, w_in, f, tiles["in"], comm=comm_in)
        y, x_next = _mm_residual(f"ffn_out_{tag}", u, w_out, xs, 0.5 * gate, tiles["out"], comm=comm_out)
        return x_next, (xs, h, ab, u, y, w_in, w_out)

    h = _norm_mod("norm_l0a", x0, norm_g_full[0, 0].reshape(1, d), ada[0][0][0], ada[0][0][1])
    ab, u = _mm_swiglu("ffn_in_l0a", h, wf_in00, f, FFN_TILES["l0a"]["in"], comm=gather_l0b)
    wf_out00, wf_in01 = gather_l0b.results
    y, x1 = _mm_residual("ffn_out_l0a", u, wf_out00, x0, 0.5 * ada[0][0][2], FFN_TILES["l0a"]["out"], comm=gather_conv)
    save_f0 = (x0, h, ab, u, y, wf_in00, wf_out00)
    wc_in, wc_out = gather_conv.results
    shift, scale, gate = ada[0][1]
    h_c = _norm_mod("norm_conv", x1, norm_g_full[0, 1].reshape(1, d), shift, scale)
    p_c = _mm_nn("conv_in", h_c, wc_in, F32, comm=gather_l0b_out)
    (wf_out01,) = gather_l0b_out.results
    z_c = _conv_fwd("conv_mix", p_c, conv_w_full, conv_b_full)
    y_c, x2 = _mm_residual("conv_out", z_c, wc_out, x1, gate)
    x3, save_f1 = ffn_fwd("l0b", x2, 0, 2, wf_in01, wf_out01, gather_l1a, gather_attn)
    wf_out10, wf_in10 = gather_l1a.results
    wq, g_kvf, wo = gather_attn.results
    wkvf = _unshard_last(g_kvf, (d,))
    wkv = wkvf[:, :2 * d]
    wf_gate = jnp.pad(wkvf[:, 2 * d:], ((0, 0), (0, LANES - n_heads)))
    x4, save_f2 = ffn_fwd("l1a", x3, 1, 0, wf_in10, wf_out10, gather_l1b, None)
    wf_out11, wf_in11 = gather_l1b.results
    shift, scale, gate_a = ada[1][1]
    h_q = _norm_mod("norm_q", x4, norm_g_full[1, 1].reshape(1, d), shift, scale)
    h_kv = _norm_mod("norm_kv", x4, kv_norm_g.reshape(1, d), ada_kv[0], ada_kv[1])
    q = _mm_nn("attn_q", h_q, wq, BF16)
    kv = _mm_nn("attn_kv", h_kv, wkv, BF16)
    zf = _mm_nn("attn_fgate", h_kv, wf_gate, F32)
    fcum, fcum_lanes = _fgate_fwd("fgate_cumsum", zf, b_f)
    fcum_t = fcum_lanes[:n_heads].reshape(n_heads, 1, s)
    o, lse_t = _attn_fwd("attn_fwd", q, kv, fcum_t, n_heads)
    y_a, x5 = _mm_residual("attn_out", o, wo, x4, gate_a)
    x6, save_f3 = ffn_fwd("l1b", x5, 1, 2, wf_in11, wf_out11, None, None)
    dx, loss_part, d_final_g = _final_loss("final_loss", x6, target, final_g.reshape(1, d))

    d_ada = [[[None] * 3 for _ in range(3)] for _ in range(n_layers)]
    d_norm_g = [[None] * 3 for _ in range(n_layers)]
    core = lax.axis_index("c").astype(jnp.int32).reshape(1)
    in_shard, out_shard, row_shard = (d, 2 * f // N_DEV), (f // N_DEV, d), (dsh, d)

    def reduce_within_chip(tag, items):
        sibling = _SiblingComm(items)
        theirs = _run_comm(f"reduce_sibling_{tag}", sibling)
        return _ChipComm([_pair_add(f"add_sibling_{tag}_{n}", p, kind, shape, t, core)
                          for n, ((p, kind, shape), t) in enumerate(zip(items, theirs))])

    def ffn_bwd(tag, dxs, saved, l, sub, comm_dw):
        xs, h, ab, u, y, w_in, w_out = saved
        _, scale, gate = ada[l][sub]
        tiles = FFN_TILES[tag]
        dy, d_ada[l][sub][2] = _gate_bwd(f"gate_bwd_{tag}", dxs, y, 0.5 * gate, 0.5)
        dab = _mm_nt_swiglu_bwd(f"ffn_out_bwd_{tag}", dy, w_out, ab, tiles["out_bwd"])
        dw_in = _mm_tn(f"ffn_in_dw_{tag}", h, dab, BF16, tiles=tiles["in_dw"])
        dw_out = _mm_tn(f"ffn_out_dw_{tag}", u, dy, BF16, tiles=tiles["out_dw"], comm=comm_dw)
        scatter = reduce_within_chip(tag, [(dw_in, "cols", in_shard), (dw_out, "rows", out_shard)])
        dh = _mm_nt_parts(f"ffn_in_bwd_{tag}", dab, w_in, F32, tiles["in_bwd"], comm=scatter)
        dxs, d_ada[l][sub][0], d_ada[l][sub][1], d_norm_g[l][sub] = _norm_mod_bwd(
            f"norm_bwd_{tag}", dh, xs, norm_g_full[l, sub].reshape(1, d), scale, dxs)
        return dxs, scatter

    dx, scatter_l1b = ffn_bwd("l1b", dx, save_f3, 1, 2, None)
    dy, d_ada[1][1][2] = _gate_bwd("gate_bwd_attn", dx, y_a, gate_a, 1.0)
    d_o = _mm_nt("attn_out_bwd", dy, wo, F32)
    dwo = _mm_tn("attn_out_dw", o, dy, BF16)
    delta_lanes, d_ob = _attn_delta("attn_delta", d_o, o, n_heads)
    delta_t = delta_lanes[:n_heads].reshape(n_heads, 1, s)
    dq, dkv, dfq, dfk = _attn_bwd("attn_bwd", q, kv, d_ob, fcum, lse_t, delta_t, n_heads)
    dfcum = jnp.pad((dfq + dfk).reshape(n_heads, s).T, ((0, 0), (0, LANES - n_heads)))
    dzf, d_bf = _fgate_bwd("fgate_bwd", dfcum, zf, b_f)
    dh_q = _mm_nt("attn_q_bwd", dq, wq, F32)
    dwq = _mm_tn("attn_q_dw", h_q, dq, BF16)
    dh_kv = _mm_nt_parts("attn_kv_bwd", dkv, wkv, F32)
    dh_kv = _mm_nt("attn_fgate_bwd", dzf, wf_gate, F32, extra_add=dh_kv)
    dwkv = _mm_tn("attn_kv_dw", h_kv, dkv, BF16)
    dwf = _mm_tn("attn_fgate_dw", h_kv, dzf, BF16)
    dwkvf = jnp.concatenate([dwkv, dwf[:, :n_heads]], axis=1)
    p_kvf = jnp.moveaxis(dwkvf.reshape(d, N_DEV, kvf_w), 1, 0)
    scatter_attn = reduce_within_chip(
        "attn", [(dwo, "rows", row_shard), (dwq, "rows", row_shard), (p_kvf, "lead", (d, kvf_w))])
    dx, d_ada[1][1][0], d_ada[1][1][1], d_norm_g[1][1] = _norm_mod_bwd(
        "norm_bwd_q", dh_q, x4, norm_g_full[1, 1].reshape(1, d), ada[1][1][1], dx)
    dx, d_kv_shift, d_kv_scale, d_kv_norm_g = _norm_mod_bwd(
        "norm_bwd_kv", dh_kv, x4, kv_norm_g.reshape(1, d), ada_kv[1], dx)
    dx, scatter_l1a = ffn_bwd("l1a", dx, save_f2, 1, 0, scatter_attn)
    dx, scatter_l0b = ffn_bwd("l0b", dx, save_f1, 0, 2, None)
    dy, d_ada[0][1][2] = _gate_bwd("gate_bwd_conv", dx, y_c, ada[0][1][2], 1.0)
    dz = _mm_nt("conv_out_bwd", dy, wc_out, F32)
    dwc_out = _mm_tn("conv_out_dw", z_c, dy, BF16)
    dp, d_conv_w, d_conv_b = _conv_bwd("conv_mix_bwd", dz, p_c, conv_w_full, conv_b_full)
    dh_c = _mm_nt_parts("conv_in_bwd", dp, wc_in, F32)
    dwc_in = _mm_tn("conv_in_dw", h_c, dp, BF16)
    scatter_conv = reduce_within_chip("conv", [(dwc_out, "rows", row_shard), (dwc_in, "cols", (d, 3 * d // N_DEV))])
    dx, d_ada[0][1][0], d_ada[0][1][1], d_norm_g[0][1] = _norm_mod_bwd(
        "norm_bwd_conv", dh_c, x1, norm_g_full[0, 1].reshape(1, d), ada[0][1][1], dx)
    dx, scatter_l0a = ffn_bwd("l0a", dx, save_f0, 0, 0, scatter_conv)
    grad_x = dx.reshape(1, s, d)

    d_ada_flat = jnp.concatenate([v.reshape(-1) for l in range(n_layers) for sub in range(3) for v in d_ada[l][sub]])
    d_norm_flat = jnp.concatenate([d_norm_g[l][sub].reshape(-1) for l in range(n_layers) for sub in range(3)])
    small_shapes = [(n_layers * 9 * d,), (2 * d,), (n_layers * 3 * d,), (3, d), (d,), (d,), (LANES,), (d,), (LANES,)]
    small_parts = _pack([d_ada_flat, d_kv_shift, d_kv_scale, d_norm_flat, d_conv_w, d_conv_b, d_kv_norm_g, d_bf,
                         d_final_g, loss_part])
    small_all = _allgather_small("gather_small_grads", small_parts)
    small_sum = _sum_slots("sum_small_grads", small_all).reshape(-1)
    g_b_ada, g_b_ada_kv, g_norm_full, g_conv_w_full, g_conv_b_full, g_kv_norm, g_bf, g_final, loss_v = _unpack(
        small_sum, small_shapes)
    loss = loss_v[0]
    d_ada_rows = small_all.reshape(N_DEV, -1)[:, :n_layers * 9 * d + 2 * d]
    d_ada_rows = jnp.pad(d_ada_rows, ((0, 16 - N_DEV), (0, 0)))

    results = {}

    def update(key, parts, w, m, v):
        shape = w.shape
        c_dim = shape[-1]
        outs = _adamw(f"adamw_{key}", parts.reshape(parts.shape[0], -1, c_dim), w.reshape(-1, c_dim),
                      m.reshape(-1, c_dim), v.reshape(-1, c_dim))
        results[key] = [o.reshape(shape) for o in outs]

    g_w_ada = []
    for l in range(n_layers):
        cols = _my_slice(d_ada_rows[:, l * 9 * d:(l + 1) * 9 * d], me, ada_w, 1)
        g_w_ada.append(_mm_tn(f"ada_dw_{l}", c_rows, cols, F32, a_fn=_silu))
    update("w_ada", jnp.stack(g_w_ada).reshape(1, n_layers * d, ada_w), w_ada, m_w_ada, v_w_ada)
    cols = _my_slice(d_ada_rows[:, n_layers * 9 * d:], me, kv_w, 1)
    update("w_ada_kv", _mm_tn("ada_dw_kv", c_rows, cols, F32, a_fn=_silu).reshape(1, d, kv_w),
           w_ada_kv, m_w_ada_kv, v_w_ada_kv)

    def update_rows(key, parts_list, w, m, v):
        c_dim = w.shape[-1]
        flat = [t.reshape(-1, c_dim) for t in (w, m, v)]
        outs = None
        for n, parts in enumerate(parts_list):
            outs = _adamw(f"adamw_{key}_{n}", parts, *flat, row0=n * parts.shape[1], previous=outs)
        results[key] = [o.reshape(w.shape) for o in outs]

    ffn_scatters = [scatter_l0a, scatter_l0b, scatter_l1a, scatter_l1b]
    update_rows("w_ffn_in", [sc.results[0] for sc in ffn_scatters], w_ffn_in, m_w_ffn_in, v_w_ffn_in)
    update_rows("w_ffn_out", [sc.results[1] for sc in ffn_scatters], w_ffn_out, m_w_ffn_out, v_w_ffn_out)
    r_co, r_ci = scatter_conv.results
    r_o, r_q, r_kvf = scatter_attn.results
    update_rows("w_conv_in", [r_ci], w_conv_in, m_w_conv_in, v_w_conv_in)
    update_rows("w_conv_out", [r_co], w_conv_out, m_w_conv_out, v_w_conv_out)
    update_rows("w_kvf", [r_kvf], w_kvf, m_w_kvf, v_w_kvf)
    update_rows("w_q", [r_q], w_q, m_w_q, v_w_q)
    update_rows("w_o", [r_o], w_o, m_w_o, v_w_o)

    small_keys = ["norm_g", "b_ada", "conv_w", "conv_b", "kv_norm_g", "b_ada_kv", "b_fgate", "final_g"]
    small_w = [norm_g, b_ada, conv_w, conv_b, kv_norm_g, b_ada_kv, b_fgate, final_g]
    small_m = [m_norm_g, m_b_ada, m_conv_w, m_conv_b, m_kv_norm_g, m_b_ada_kv, m_b_fgate, m_final_g]
    small_v = [v_norm_g, v_b_ada, v_conv_w, v_conv_b, v_kv_norm_g, v_b_ada_kv, v_b_fgate, v_final_g]
    small_g = [
        _my_slice(g_norm_full.reshape(n_layers, 3, d), me, dsh, 2), g_b_ada.reshape(b_ada.shape),
        _my_slice(g_conv_w_full, me, dsh, 1).reshape(conv_w.shape), _my_slice(g_conv_b_full, me, dsh, 0).reshape(
            conv_b.shape), g_kv_norm, g_b_ada_kv, g_bf[:n_heads], g_final]
    packed = _adamw("adamw_small", _pack(small_g)[None], _pack(small_w), _pack(small_m), _pack(small_v))
    for vals, idx in zip(packed, range(4)):
        for key, val in zip(small_keys, _unpack(vals.reshape(-1), [w.shape for w in small_w])):
            results.setdefault(key, [None] * 4)[idx] = val

    order = ["norm_g", "w_ada", "b_ada", "w_ffn_in", "w_ffn_out", "w_conv_in", "conv_w", "conv_b", "w_conv_out",
             "kv_norm_g", "w_ada_kv", "b_ada_kv", "w_kvf", "b_fgate", "w_q", "w_o", "final_g"]
    return (loss, grad_x, *[results[k][0] for k in order], *[results[k][1] for k in order],
            *[results[k][2] for k in order], *[results[k][3] for k in order])
```

```python
import functools
import math

import jax
import jax.numpy as jnp
from jax import lax
from jax.experimental import pallas as pl
from jax.experimental.pallas import tpu as pltpu

F32 = jnp.float32
BF16 = jnp.bfloat16
MESH = pl.DeviceIdType.MESH

N_DEV = 8
LANES = 128
SUBLANES = 8
VMEM_LIMIT = 56 * 2**20
EPS = 1e-6
LOG2E = math.log2(math.e)
ADAM_LR, ADAM_B1, ADAM_B2, ADAM_EPS, ADAM_WD, ADAM_STEP = 0.001, 0.9, 0.999, 1e-08, 0.01, 10

_DN = {"nn": (((1,), (0,)), ((), ())), "nt": (((1,), (1,)), ((), ())), "tn": (((0,), (0,)), ((), ()))}


def _tile(n, pref, align):
    t = min(pref, n)
    t -= t % align
    while t >= align:
        if n % t == 0:
            return t
        t -= align
    return n


def _params(n_grid):
    return pltpu.CompilerParams(vmem_limit_bytes=VMEM_LIMIT, dimension_semantics=("arbitrary",) * n_grid)


def _silu(v):
    return v * jax.nn.sigmoid(v)


def _position():
    x, y, c = lax.axis_index("x"), lax.axis_index("y"), lax.axis_index("c")
    return x, y, c


def _allgather_small(name, v):
    def body(v_ref, out_ref, send_sems, recv_sems, local_sem):
        x, y, c = _position()
        me, sibling = (x, y, c), (x, y, 1 - c)
        chips = [(1 - x, y), (x, 1 - y), (1 - x, 1 - y)]

        def slot(px, py, pc):
            return out_ref.at[4 * px + 2 * py + pc]

        def copy(k, block, to, src=None):
            return pltpu.make_async_remote_copy(
                src_ref=slot(*block) if src is None else src, dst_ref=slot(*block),
                send_sem=send_sems.at[k], recv_sem=recv_sems.at[k], device_id=to, device_id_type=MESH)

        mine = pltpu.make_async_copy(v_ref, slot(*me), local_sem)
        mine.start()
        first = [copy(0, me, sibling, src=v_ref)]
        first += [copy(1 + j, me, (*chip, c), src=v_ref) for j, chip in enumerate(chips)]
        for cp in first:
            cp.start()
        passed = [copy(4 + j, (*chip, c), sibling) for j, chip in enumerate(chips)]
        for j, chip in enumerate(chips):
            copy(1 + j, (*chip, c), me).wait_recv()
            passed[j].start()
        copy(0, sibling, me).wait_recv()
        for j, chip in enumerate(chips):
            copy(4 + j, (*chip, 1 - c), me).wait_recv()
        for cp in first + passed:
            cp.wait_send()
        mine.wait()

    return pl.pallas_call(
        body, name=name,
        out_shape=jax.ShapeDtypeStruct((N_DEV,) + v.shape, v.dtype),
        in_specs=[pl.BlockSpec(memory_space=pltpu.VMEM)],
        out_specs=pl.BlockSpec(memory_space=pltpu.VMEM),
        scratch_shapes=[pltpu.SemaphoreType.DMA((7,)), pltpu.SemaphoreType.DMA((7,)), pltpu.SemaphoreType.DMA],
        compiler_params=pltpu.CompilerParams(vmem_limit_bytes=VMEM_LIMIT),
    )(v)


def _owner_view(ref, kind, shard_shape, owner):
    r, w = shard_shape
    if kind == "rows":
        return ref.at[pl.ds(pl.multiple_of(owner * r, 16), r)]
    if kind == "cols":
        return ref.at[:, pl.ds(pl.multiple_of(owner * w, LANES), w)]
    return ref.at[owner]


def _full_shape(kind, shard_shape):
    r, w = shard_shape
    return {"rows": (N_DEV * r, w), "cols": (r, N_DEV * w), "lead": (N_DEV, r, w)}[kind]


class _GatherComm:
    mid_fraction = 0.85

    def __init__(self, items):
        self.ins = [shard for shard, _ in items]
        self.kinds = [kind for _, kind in items]
        n = len(items)
        self.out_shapes = [jax.ShapeDtypeStruct(_full_shape(k, a.shape), a.dtype) for a, k in items]
        self.sem_shapes = [pltpu.SemaphoreType.DMA((7 * n,)), pltpu.SemaphoreType.DMA((7 * n,)),
                           pltpu.SemaphoreType.DMA((n,))]
        self.results = None

    def _copy(self, refs, a, k, block, to, from_input=False):
        ins, outs, (send_sems, recv_sems, _) = refs
        px, py, pc = block
        dst = _owner_view(outs[a], self.kinds[a], self.ins[a].shape, 4 * px + 2 * py + pc)
        return pltpu.make_async_remote_copy(
            src_ref=ins[a] if from_input else dst, dst_ref=dst,
            send_sem=send_sems.at[7 * a + k], recv_sem=recv_sems.at[7 * a + k], device_id=to, device_id_type=MESH)

    def _local(self, refs, a):
        ins, outs, (_, _, local_sems) = refs
        x, y, c = _position()
        dst = _owner_view(outs[a], self.kinds[a], self.ins[a].shape, 4 * x + 2 * y + c)
        return pltpu.make_async_copy(ins[a], dst, local_sems.at[a])

    def start(self, *refs):
        x, y, c = _position()
        me, sibling = (x, y, c), (x, y, 1 - c)
        chips = [(1 - x, y), (x, 1 - y), (1 - x, 1 - y)]
        for a in range(len(self.ins)):
            self._local(refs, a).start()
            for j, chip in enumerate(chips):
                self._copy(refs, a, 1 + j, me, (*chip, c), from_input=True).start()
            self._copy(refs, a, 0, me, sibling, from_input=True).start()

    def mid(self, *refs):
        x, y, c = _position()
        chips = [(1 - x, y), (x, 1 - y), (1 - x, 1 - y)]
        for a in range(len(self.ins)):
            for j, chip in enumerate(chips):
                self._copy(refs, a, 1 + j, (*chip, c), (x, y, c)).wait_recv()
                self._copy(refs, a, 4 + j, (*chip, c), (x, y, 1 - c)).start()

    def finish(self, *refs):
        x, y, c = _position()
        me, sibling = (x, y, c), (x, y, 1 - c)
        chips = [(1 - x, y), (x, 1 - y), (1 - x, 1 - y)]
        for a in range(len(self.ins)):
            self._copy(refs, a, 0, sibling, me).wait_recv()
            for j, chip in enumerate(chips):
                self._copy(refs, a, 4 + j, (*chip, 1 - c), me).wait_recv()
        for a in range(len(self.ins)):
            self._copy(refs, a, 0, me, sibling, from_input=True).wait_send()
            for j, chip in enumerate(chips):
                self._copy(refs, a, 1 + j, me, (*chip, c), from_input=True).wait_send()
                self._copy(refs, a, 4 + j, (*chip, c), sibling).wait_send()
            self._local(refs, a).wait()


class _SiblingComm:
    mid_fraction = None

    def __init__(self, items):
        self.ins = [p for p, _, _ in items]
        self.kinds = [kind for _, kind, _ in items]
        self.shapes = [shape for _, _, shape in items]
        n = len(items)
        self.out_shapes = [jax.ShapeDtypeStruct((4,) + tuple(shape), p.dtype) for p, _, shape in items]
        self.sem_shapes = [pltpu.SemaphoreType.DMA((4 * n,)), pltpu.SemaphoreType.DMA((4 * n,))]
        self.results = None

    def _copies(self, refs):
        ins, outs, (send_sems, recv_sems) = refs
        x, y, c = _position()
        return [pltpu.make_async_remote_copy(
            src_ref=_owner_view(ins[a], self.kinds[a], self.shapes[a], 2 * k + 1 - c), dst_ref=outs[a].at[k],
            send_sem=send_sems.at[4 * a + k], recv_sem=recv_sems.at[4 * a + k],
            device_id=(x, y, 1 - c), device_id_type=MESH) for a in range(len(self.ins)) for k in range(4)]

    def start(self, *refs):
        for cp in self._copies(refs):
            cp.start()

    def finish(self, *refs):
        for cp in self._copies(refs):
            cp.wait()


class _ChipComm:
    mid_fraction = None

    def __init__(self, items):
        self.ins = list(items)
        n = len(items)
        self.out_shapes = [jax.ShapeDtypeStruct(p.shape, p.dtype) for p in items]
        self.sem_shapes = [pltpu.SemaphoreType.DMA((3 * n,)), pltpu.SemaphoreType.DMA((3 * n,)),
                           pltpu.SemaphoreType.DMA((n,))]
        self.results = None

    def _copy(self, refs, a, k, landed):
        ins, outs, (send_sems, recv_sems, _) = refs
        x, y, c = _position()
        px = 1 - x if k & 2 else x
        py = 1 - y if k & 1 else y
        to_chip, my_chip = 2 * px + py, 2 * x + y
        return pltpu.make_async_remote_copy(
            src_ref=ins[a].at[to_chip], dst_ref=outs[a].at[to_chip if landed else my_chip],
            send_sem=send_sems.at[3 * a + k - 1], recv_sem=recv_sems.at[3 * a + k - 1],
            device_id=(px, py, c), device_id_type=MESH)

    def _local(self, refs, a):
        ins, outs, (_, _, local_sems) = refs
        x, y, _ = _position()
        return pltpu.make_async_copy(ins[a].at[2 * x + y], outs[a].at[2 * x + y], local_sems.at[a])

    def start(self, *refs):
        for a in range(len(self.ins)):
            self._local(refs, a).start()
            for k in range(1, 4):
                self._copy(refs, a, k, False).start()

    def finish(self, *refs):
        for a in range(len(self.ins)):
            for k in range(1, 4):
                self._copy(refs, a, k, True).wait_recv()
        for a in range(len(self.ins)):
            for k in range(1, 4):
                self._copy(refs, a, k, False).wait_send()
            self._local(refs, a).wait()


class _Joined:
    def __init__(self, comms):
        self.comms = list(comms)
        self.ins = [a for c in self.comms for a in c.ins]
        self.out_shapes = [o for c in self.comms for o in c.out_shapes]
        self.sem_shapes = [o for c in self.comms for o in c.sem_shapes]
        fractions = [c.mid_fraction for c in self.comms if c.mid_fraction is not None]
        self.mid_fraction = max(fractions) if fractions else None

    def _parts(self, refs):
        ins, outs, sems = refs
        pos = [0, 0, 0]
        for c in self.comms:
            n = (len(c.ins), len(c.out_shapes), len(c.sem_shapes))
            yield c, (ins[pos[0]:pos[0] + n[0]], outs[pos[1]:pos[1] + n[1]], sems[pos[2]:pos[2] + n[2]])
            pos = [p + k for p, k in zip(pos, n)]

    def start(self, *refs):
        for c, part in self._parts(refs):
            c.start(*part)

    def mid(self, *refs):
        for c, part in self._parts(refs):
            if c.mid_fraction is not None:
                c.mid(*part)

    def finish(self, *refs):
        for c, part in self._parts(refs):
            c.finish(*part)

    @property
    def results(self):
        return [r for c in self.comms for r in c.results]

    @results.setter
    def results(self, values):
        pos = 0
        for c in self.comms:
            c.results = values[pos:pos + len(c.out_shapes)]
            pos += len(c.out_shapes)


def _join(*comms):
    present = [c for c in comms if c is not None]
    return present[0] if len(present) == 1 else (_Joined(present) if present else None)


def _run_comm(name, comm):
    n_in, n_out = len(comm.ins), len(comm.out_shapes)

    def body(*refs):
        parts = (refs[:n_in], refs[n_in:n_in + n_out], refs[n_in + n_out:])
        comm.start(*parts)
        if comm.mid_fraction is not None:
            comm.mid(*parts)
        comm.finish(*parts)

    any_spec = pl.BlockSpec(memory_space=pl.ANY)
    return pl.pallas_call(
        body, name=name, out_shape=list(comm.out_shapes), in_specs=[any_spec] * n_in, out_specs=[any_spec] * n_out,
        scratch_shapes=list(comm.sem_shapes))(*comm.ins)


def _carry(comm, body, grid, in_specs, out_specs, out_shapes, scratch_shapes, operands):
    if comm is None:
        return body, in_specs, out_specs, out_shapes, scratch_shapes, operands
    n_in, n_out, n_scr = len(in_specs), len(out_specs), len(scratch_shapes)
    c_in, c_out = len(comm.ins), len(comm.out_shapes)
    total = math.prod(grid)
    mid_step = None if comm.mid_fraction is None else min(total - 1, int(total * comm.mid_fraction))

    def wrapped(*refs):
        ins, c_ins = refs[:n_in], refs[n_in:n_in + c_in]
        outs = refs[n_in + c_in:n_in + c_in + n_out]
        c_outs = refs[n_in + c_in + n_out:n_in + c_in + n_out + c_out]
        scr = refs[n_in + c_in + n_out + c_out:n_in + c_in + n_out + c_out + n_scr]
        parts = (c_ins, c_outs, refs[n_in + c_in + n_out + c_out + n_scr:])
        step = pl.program_id(0)
        for axis in range(1, len(grid)):
            step = step * grid[axis] + pl.program_id(axis)

        @pl.when(step == 0)
        def _():
            comm.start(*parts)

        body(*ins, *outs, *scr)

        if mid_step is not None:
            @pl.when(step == mid_step)
            def _():
                comm.mid(*parts)

        @pl.when(step == total - 1)
        def _():
            comm.finish(*parts)

    any_spec = pl.BlockSpec(memory_space=pl.ANY)
    return (wrapped, list(in_specs) + [any_spec] * c_in, list(out_specs) + [any_spec] * c_out,
            list(out_shapes) + list(comm.out_shapes), list(scratch_shapes) + list(comm.sem_shapes),
            list(operands) + list(comm.ins))


def _pair_add(name, partial, kind, shard_shape, theirs, core):
    r, w = shard_shape
    tb = _tile(r, max(16, (2**19 // w) // 16 * 16), 16)
    nb = r // tb
    if kind == "rows":
        mine = pl.BlockSpec((tb, w), lambda k, i, core_ref: ((2 * k + core_ref[0]) * nb + i, 0))
    elif kind == "cols":
        mine = pl.BlockSpec((tb, w), lambda k, i, core_ref: (i, 2 * k + core_ref[0]))
    else:
        mine = pl.BlockSpec((None, tb, w), lambda k, i, core_ref: (2 * k + core_ref[0], i, 0))
    slot = pl.BlockSpec((None, tb, w), lambda k, i, core_ref: (k, i, 0))

    def body(core_ref, a_ref, b_ref, o_ref):
        o_ref[...] = (a_ref[...].astype(F32) + b_ref[...].astype(F32)).astype(o_ref.dtype)

    return pl.pallas_call(
        body, name=name,
        grid_spec=pltpu.PrefetchScalarGridSpec(num_scalar_prefetch=1, grid=(4, nb), in_specs=[mine, slot],
                                               out_specs=slot),
        out_shape=jax.ShapeDtypeStruct(theirs.shape, theirs.dtype), compiler_params=_params(2))(core, partial, theirs)


def _matmul(name, mode, grid, a, a_spec, bs, b_specs, extras, extra_specs, out_shapes, out_specs, acc_shape,
            epilogue, a_fn=None, comm=None):
    nk, nb, ne, no = grid[2], len(bs), len(extras), len(out_shapes)

    def body(*refs):
        a_ref, b_refs, e_refs = refs[0], refs[1:1 + nb], refs[1 + nb:1 + nb + ne]
        o_refs, acc_refs = refs[1 + nb + ne:1 + nb + ne + no], refs[1 + nb + ne + no:]
        a_val = a_ref[...]
        if a_fn is not None:
            a_val = a_fn(a_val)
        a_val = a_val.astype(BF16)

        def product(b_ref):
            return lax.dot_general(a_val, b_ref[...].astype(BF16), _DN[mode], preferred_element_type=F32)

        def finish(accs):
            outs = epilogue(accs, [e[...] for e in e_refs])
            for o_ref, o in zip(o_refs, outs):
                o_ref[...] = o.astype(o_ref.dtype)

        if nk == 1:
            finish([product(b_ref) for b_ref in b_refs])
        else:
            kk = pl.program_id(2)

            @pl.when(kk == 0)
            def _():
                for acc in acc_refs:
                    acc[...] = jnp.zeros_like(acc)

            for acc, b_ref in zip(acc_refs, b_refs):
                acc[...] += product(b_ref)

            @pl.when(kk == nk - 1)
            def _():
                finish([acc[...] for acc in acc_refs])

    scratch = [] if nk == 1 else [pltpu.VMEM(acc_shape, F32) for _ in range(nb)]
    body, in_specs, out_specs, out_shape, scratch, operands = _carry(
        comm, body, grid, [a_spec] + list(b_specs) + list(extra_specs), list(out_specs), list(out_shapes), scratch,
        [a, *bs, *extras])
    outs = pl.pallas_call(
        body, name=name, grid=grid, in_specs=in_specs, out_specs=out_specs, out_shape=out_shape,
        scratch_shapes=scratch, compiler_params=_params(3))(*operands)
    if comm is not None:
        comm.results = outs[no:]
    return outs[:no]


def _mm_tiles(m, n, k, tiles):
    return _tile(m, tiles[0], 16), _tile(n, tiles[1], LANES), _tile(k, tiles[2], LANES)


FFN_TILES = {"in": (1024, 512, 2048), "out": (1024, 512, 5632), "out_bwd": (2048, 512, 2048),
             "in_bwd": (1024, 1024, 2816), "out_dw": (512, 2048, 2048), "in_dw": (1024, 512, 4096)}
FULL_K = (1024, 1024, 2048)
LOOP_K = (1024, 1024, 2048)


def _mm_nn(name, a, b, out_dtype, b_col0=0, n=None, tiles=FULL_K, comm=None):
    m, k = a.shape
    n = b.shape[1] if n is None else n
    tm, tn, tk = _mm_tiles(m, n, k, tiles)
    off = b_col0 // tn
    assert b_col0 % tn == 0
    return _matmul(
        name, "nn", (m // tm, n // tn, k // tk), a, pl.BlockSpec((tm, tk), lambda i, j, kk: (i, kk)),
        [b], [pl.BlockSpec((tk, tn), lambda i, j, kk: (kk, j + off))], [], [],
        [jax.ShapeDtypeStruct((m, n), out_dtype)], [pl.BlockSpec((tm, tn), lambda i, j, kk: (i, j))], (tm, tn),
        lambda accs, ex: accs, comm=comm)[0]


def _mm_bias_rows(name, a, b, bias, a_fn):
    m, k = a.shape
    n = b.shape[1]
    tn = _tile(n, 768, LANES)
    return _matmul(
        name, "nn", (1, n // tn, 1), a, pl.BlockSpec((m, k), lambda i, j, kk: (0, 0)),
        [b], [pl.BlockSpec((k, tn), lambda i, j, kk: (0, j))],
        [bias], [pl.BlockSpec((1, tn), lambda i, j, kk: (0, j))],
        [jax.ShapeDtypeStruct((m, n), F32)], [pl.BlockSpec((m, tn), lambda i, j, kk: (0, j))], (m, tn),
        lambda accs, ex: [accs[0] + ex[0]], a_fn=a_fn)[0]


def _mm_swiglu(name, h, w_in, f, tiles, comm=None):
    s, d = h.shape
    tm, tn, tk = _mm_tiles(s, f, d, tiles)
    nf = f // tn

    def epilogue(accs, ex):
        a_pre, b_pre = accs
        return [jnp.stack([a_pre, b_pre]), _silu(a_pre) * b_pre]

    return _matmul(
        name, "nn", (s // tm, nf, d // tk), h, pl.BlockSpec((tm, tk), lambda i, j, kk: (i, kk)),
        [w_in, w_in], [pl.BlockSpec((tk, tn), lambda i, j, kk: (kk, j)),
                       pl.BlockSpec((tk, tn), lambda i, j, kk: (kk, j + nf))], [], [],
        [jax.ShapeDtypeStruct((2, s, f), BF16), jax.ShapeDtypeStruct((s, f), BF16)],
        [pl.BlockSpec((2, tm, tn), lambda i, j, kk: (0, i, j)), pl.BlockSpec((tm, tn), lambda i, j, kk: (i, j))],
        (tm, tn), epilogue, comm=comm)


def _mm_residual(name, a, w, x_in, gvec, tiles=FULL_K, comm=None):
    s, k = a.shape
    d = w.shape[1]
    tm, tn, tk = _mm_tiles(s, d, k, tiles)

    def epilogue(accs, ex):
        x_blk, g_row = ex
        return [accs[0], x_blk + g_row * accs[0]]

    tile = pl.BlockSpec((tm, tn), lambda i, j, kk: (i, j))
    return _matmul(
        name, "nn", (s // tm, d // tn, k // tk), a, pl.BlockSpec((tm, tk), lambda i, j, kk: (i, kk)),
        [w], [pl.BlockSpec((tk, tn), lambda i, j, kk: (kk, j))],
        [x_in, gvec], [tile, pl.BlockSpec((1, tn), lambda i, j, kk: (0, j))],
        [jax.ShapeDtypeStruct((s, d), BF16), jax.ShapeDtypeStruct((s, d), F32)], [tile, tile], (tm, tn), epilogue,
        comm=comm)


def _mm_nt(name, a, w, out_dtype, w_col0=0, extra_add=None, tiles=FULL_K):
    s, k = a.shape
    n = w.shape[0]
    tm, tn, tk = _mm_tiles(s, n, k, tiles)
    off = w_col0 // tk
    assert w_col0 % tk == 0
    tile = pl.BlockSpec((tm, tn), lambda i, j, kk: (i, j))
    extras, especs = ([], []) if extra_add is None else ([extra_add], [tile])
    return _matmul(
        name, "nt", (s // tm, n // tn, k // tk), a, pl.BlockSpec((tm, tk), lambda i, j, kk: (i, kk)),
        [w], [pl.BlockSpec((tn, tk), lambda i, j, kk: (j, kk + off))], extras, especs,
        [jax.ShapeDtypeStruct((s, n), out_dtype)], [tile], (tm, tn),
        lambda accs, ex: [accs[0] + ex[0]] if ex else accs)[0]


def _mm_nt_parts(name, a3, w, out_dtype, tiles=LOOP_K, comm=None):
    p, s, k = a3.shape
    n = w.shape[0]
    tm, tn, tk = _mm_tiles(s, n, k, tiles)
    nkp = k // tk
    return _matmul(
        name, "nt", (s // tm, n // tn, p * nkp), a3,
        pl.BlockSpec((None, tm, tk), lambda i, j, kk: (kk // nkp, i, kk % nkp)),
        [w], [pl.BlockSpec((tn, tk), lambda i, j, kk: (j, kk))], [], [],
        [jax.ShapeDtypeStruct((s, n), out_dtype)], [pl.BlockSpec((tm, tn), lambda i, j, kk: (i, j))], (tm, tn),
        lambda accs, ex: accs, comm=comm)[0]


def _mm_nt_swiglu_bwd(name, dy, w_out, ab, tiles):
    s, d = dy.shape
    f = w_out.shape[0]
    tm, tn, tk = _mm_tiles(s, f, d, tiles)

    def epilogue(accs, ex):
        du, a_pre, b_pre = accs[0], ex[0][0].astype(F32), ex[0][1].astype(F32)
        sig = jax.nn.sigmoid(a_pre)
        da = du * b_pre * (sig * (1.0 + a_pre * (1.0 - sig)))
        db = du * (a_pre * sig)
        return [jnp.stack([da, db])]

    pair = pl.BlockSpec((2, tm, tn), lambda i, j, kk: (0, i, j))
    return _matmul(
        name, "nt", (s // tm, f // tn, d // tk), dy, pl.BlockSpec((tm, tk), lambda i, j, kk: (i, kk)),
        [w_out], [pl.BlockSpec((tn, tk), lambda i, j, kk: (j, kk))], [ab], [pair],
        [jax.ShapeDtypeStruct((2, s, f), BF16)], [pair], (tm, tn), epilogue)[0]


def _mm_tn(name, a, b, out_dtype, a_fn=None, tiles=LOOP_K, comm=None):
    s, m = a.shape
    tm, tk = _tile(m, tiles[0], LANES), _tile(s, tiles[2], 16)
    if b.ndim == 3:
        npart = b.shape[2]
        n = b.shape[0] * npart
        tn = _tile(npart, tiles[1], LANES)
        nj = npart // tn
        b_spec = pl.BlockSpec((None, tk, tn), lambda i, j, kk: (j // nj, kk, j % nj))
    else:
        n = b.shape[1]
        tn = _tile(n, tiles[1], LANES)
        b_spec = pl.BlockSpec((tk, tn), lambda i, j, kk: (kk, j))
    return _matmul(
        name, "tn", (m // tm, n // tn, s // tk), a, pl.BlockSpec((tk, tm), lambda i, j, kk: (kk, i)),
        [b], [b_spec], [], [],
        [jax.ShapeDtypeStruct((m, n), out_dtype)], [pl.BlockSpec((tm, tn), lambda i, j, kk: (i, j))], (tm, tn),
        lambda accs, ex: accs, a_fn=a_fn, comm=comm)[0]


def _row_tile(s):
    return _tile(s, 256, 16)


def _fold_rows(v):
    return jnp.sum(v.reshape(v.shape[0] // SUBLANES, SUBLANES, v.shape[1]), axis=0)


def _accumulate(i, acc_ref, val):
    @pl.when(i == 0)
    def _():
        acc_ref[...] = val

    @pl.when(i > 0)
    def _():
        acc_ref[...] += val


def _norm_mod(name, x, g, shift, scale):
    s, d = x.shape
    tb = _row_tile(s)

    def body(x_ref, g_ref, sh_ref, sc_ref, h_ref):
        xv = x_ref[...]
        rstd = lax.rsqrt(jnp.mean(xv * xv, axis=-1, keepdims=True) + EPS)
        h_ref[...] = ((xv * rstd) * g_ref[...] * (1.0 + sc_ref[...]) + sh_ref[...]).astype(BF16)

    row = pl.BlockSpec((tb, d), lambda i: (i, 0))
    vec = pl.BlockSpec((1, d), lambda i: (0, 0))
    return pl.pallas_call(body, name=name, grid=(s // tb,), in_specs=[row, vec, vec, vec], out_specs=row,
                          out_shape=jax.ShapeDtypeStruct((s, d), BF16), compiler_params=_params(1))(x, g, shift, scale)


def _rms_bwd(xv, dn_g, d):
    rstd = lax.rsqrt(jnp.mean(xv * xv, axis=-1, keepdims=True) + EPS)
    xhat = xv * rstd
    dx = rstd * (dn_g - xhat * (jnp.sum(dn_g * xhat, axis=-1, keepdims=True) * (1.0 / d)))
    return xhat, dx


def _norm_mod_bwd(name, dh, x, g, scale, dx_in, then=None):
    s, d = x.shape
    tb = _row_tile(s)
    nsteps = s // tb
    coef = None if then is None else then[2]

    def body(*refs):
        dh_ref, x_ref, g_ref, sc_ref, dxin_ref = refs[:5]
        if then is None:
            dx_ref, dsh_ref, dsc_ref, dg_ref, a_sh, a_sc, a_g = refs[5:]
        else:
            y_ref, gv_ref, dx_ref, dsh_ref, dsc_ref, dg_ref, dy_ref, dgate_ref, a_sh, a_sc, a_g, a_gate = refs[5:]
        i = pl.program_id(0)
        dhv, gv = dh_ref[...], g_ref[...]
        dn = dhv * (1.0 + sc_ref[...])
        xhat, dx = _rms_bwd(x_ref[...], dn * gv, d)
        dx_total = dxin_ref[...] + dx
        dx_ref[...] = dx_total
        _accumulate(i, a_sh, _fold_rows(dhv))
        _accumulate(i, a_sc, _fold_rows(dhv * (xhat * gv)))
        _accumulate(i, a_g, _fold_rows(dn * xhat))
        if then is not None:
            dy_ref[...] = (dx_total * gv_ref[...]).astype(BF16)
            _accumulate(i, a_gate, _fold_rows(dx_total * y_ref[...].astype(F32)))

        @pl.when(i == nsteps - 1)
        def _():
            dsh_ref[...] = jnp.sum(a_sh[...], axis=0, keepdims=True)
            dsc_ref[...] = jnp.sum(a_sc[...], axis=0, keepdims=True)
            dg_ref[...] = jnp.sum(a_g[...], axis=0, keepdims=True)
            if then is not None:
                dgate_ref[...] = coef * jnp.sum(a_gate[...], axis=0, keepdims=True)

    row = pl.BlockSpec((tb, d), lambda i: (i, 0))
    vec = pl.BlockSpec((1, d), lambda i: (0, 0))
    vshape = jax.ShapeDtypeStruct((1, d), F32)
    in_specs, operands = [row, row, vec, vec, row], [dh, x, g, scale, dx_in]
    out_specs, out_shape = [row, vec, vec, vec], [jax.ShapeDtypeStruct((s, d), F32), vshape, vshape, vshape]
    n_acc = 3
    if then is not None:
        in_specs, operands = in_specs + [row, vec], operands + [then[0], then[1]]
        out_specs, out_shape = out_specs + [row, vec], out_shape + [jax.ShapeDtypeStruct((s, d), BF16), vshape]
        n_acc = 4
    return pl.pallas_call(
        body, name=name, grid=(nsteps,), in_specs=in_specs, out_specs=out_specs, out_shape=out_shape,
        scratch_shapes=[pltpu.VMEM((SUBLANES, d), F32)] * n_acc, compiler_params=_params(1))(*operands)


def _gate_bwd(name, dx, y, gvec, coef):
    s, d = dx.shape
    tb = _row_tile(s)
    nsteps = s // tb

    def body(dx_ref, y_ref, g_ref, dy_ref, dg_ref, acc):
        i = pl.program_id(0)
        dxv = dx_ref[...]
        dy_ref[...] = (dxv * g_ref[...]).astype(BF16)
        _accumulate(i, acc, _fold_rows(dxv * y_ref[...].astype(F32)))

        @pl.when(i == nsteps - 1)
        def _():
            dg_ref[...] = coef * jnp.sum(acc[...], axis=0, keepdims=True)

    row = pl.BlockSpec((tb, d), lambda i: (i, 0))
    vec = pl.BlockSpec((1, d), lambda i: (0, 0))
    return pl.pallas_call(
        body, name=name, grid=(nsteps,), in_specs=[row, row, vec], out_specs=[row, vec],
        out_shape=[jax.ShapeDtypeStruct((s, d), BF16), jax.ShapeDtypeStruct((1, d), F32)],
        scratch_shapes=[pltpu.VMEM((SUBLANES, d), F32)], compiler_params=_params(1))(dx, y, gvec)


def _final_loss(name, x, target, g):
    s, d = x.shape
    tb = _row_tile(s)
    nsteps = s // tb

    def body(x_ref, t_ref, g_ref, dx_ref, loss_ref, dg_ref, a_loss, a_g):
        i = pl.program_id(0)
        xv, gv = x_ref[...], g_ref[...]
        rstd = lax.rsqrt(jnp.mean(xv * xv, axis=-1, keepdims=True) + EPS)
        xhat = xv * rstd
        err = xhat * gv - t_ref[...]
        dyv = err * (1.0 / d)
        dn_g = dyv * gv
        dx_ref[...] = rstd * (dn_g - xhat * (jnp.sum(dn_g * xhat, axis=-1, keepdims=True) * (1.0 / d)))
        _accumulate(i, a_loss, _fold_rows(err * err))
        _accumulate(i, a_g, _fold_rows(dyv * xhat))

        @pl.when(i == nsteps - 1)
        def _():
            total = jnp.sum(jnp.sum(a_loss[...], axis=0, keepdims=True), axis=1, keepdims=True) * (0.5 / d)
            loss_ref[...] = jnp.broadcast_to(total, (1, LANES))
            dg_ref[...] = jnp.sum(a_g[...], axis=0, keepdims=True)

    row = pl.BlockSpec((tb, d), lambda i: (i, 0))
    vec = pl.BlockSpec((1, d), lambda i: (0, 0))
    return pl.pallas_call(
        body, name=name, grid=(nsteps,), in_specs=[row, row, vec],
        out_specs=[row, pl.BlockSpec((1, LANES), lambda i: (0, 0)), vec],
        out_shape=[jax.ShapeDtypeStruct((s, d), F32), jax.ShapeDtypeStruct((1, LANES), F32),
                   jax.ShapeDtypeStruct((1, d), F32)],
        scratch_shapes=[pltpu.VMEM((SUBLANES, d), F32)] * 2, compiler_params=_params(1))(x, target, g)


def _shift_rows(cur, halo, n):
    rolled = pltpu.roll(cur, n, 0)
    row = lax.broadcasted_iota(jnp.int32, cur.shape, 0)
    for r in range(n):
        rolled = jnp.where(row == r, halo[SUBLANES - n + r:SUBLANES - n + r + 1, :], rolled)
    return rolled


def _shift_rows_up(cur, halo, n):
    tb = cur.shape[0]
    rolled = pltpu.roll(cur, tb - n, 0)
    row = lax.broadcasted_iota(jnp.int32, cur.shape, 0)
    for r in range(n):
        rolled = jnp.where(row == tb - n + r, halo[r:r + 1, :], rolled)
    return rolled


def _conv_specs(s, d, tb, tn):
    nd = d // tn
    hb = tb // SUBLANES
    cur = lambda part: pl.BlockSpec((tb, tn), lambda i, j: (i, j + part * nd))
    prev = lambda part: pl.BlockSpec((SUBLANES, tn), lambda i, j: (jnp.maximum(i * hb - 1, 0), j + part * nd))
    nxt = lambda part: pl.BlockSpec((SUBLANES, tn), lambda i, j: (jnp.minimum((i + 1) * hb, s // SUBLANES - 1),
                                                                  j + part * nd))
    return cur, prev, nxt


def _conv_fwd(name, p, conv_w, conv_b):
    s, d = p.shape[0], p.shape[1] // 3
    tb, tn = _row_tile(s), _tile(d, 512, LANES)
    cur, prev, _ = _conv_specs(s, d, tb, tn)

    def body(bg_ref, cg_ref, xv_ref, cgp_ref, xvp_ref, w_ref, b_ref, z_ref):
        i = pl.program_id(0)
        u = cg_ref[...] * xv_ref[...]
        up = jnp.where(i > 0, cgp_ref[...] * xvp_ref[...], 0.0)
        w = w_ref[...]
        uc = w[0:1, :] * _shift_rows(u, up, 2) + w[1:2, :] * _shift_rows(u, up, 1) + w[2:3, :] * u + b_ref[...]
        z_ref[...] = (bg_ref[...] * uc).astype(BF16)

    return pl.pallas_call(
        body, name=name, grid=(s // tb, d // tn),
        in_specs=[cur(0), cur(1), cur(2), prev(1), prev(2), pl.BlockSpec((3, tn), lambda i, j: (0, j)),
                  pl.BlockSpec((1, tn), lambda i, j: (0, j))],
        out_specs=pl.BlockSpec((tb, tn), lambda i, j: (i, j)),
        out_shape=jax.ShapeDtypeStruct((s, d), BF16), compiler_params=_params(2))(p, p, p, p, p, conv_w, conv_b)


def _conv_bwd(name, dz, p, conv_w, conv_b):
    s, d = dz.shape
    tb, tn = _row_tile(s), _tile(d, 512, LANES)
    nsteps = s // tb
    cur, prev, nxt = _conv_specs(s, d, tb, tn)

    def body(dz_ref, bg_ref, cg_ref, xv_ref, cgp_ref, xvp_ref, dzn_ref, bgn_ref, w_ref, b_ref,
             dp_ref, dw_ref, db_ref, a_w0, a_w1, a_w2, a_b):
        j, i = pl.program_id(0), pl.program_id(1)
        cg, xv, bg, dzv = cg_ref[...], xv_ref[...], bg_ref[...], dz_ref[...]
        u = cg * xv
        up = jnp.where(i > 0, cgp_ref[...] * xvp_ref[...], 0.0)
        w = w_ref[...]
        u1, u2 = _shift_rows(u, up, 1), _shift_rows(u, up, 2)
        uc = w[0:1, :] * u2 + w[1:2, :] * u1 + w[2:3, :] * u + b_ref[...]
        duc = dzv * bg
        ducn = jnp.where(i < nsteps - 1, dzn_ref[...] * bgn_ref[...], 0.0)
        du = w[2:3, :] * duc + w[1:2, :] * _shift_rows_up(duc, ducn, 1) + w[0:1, :] * _shift_rows_up(duc, ducn, 2)
        dp_ref[0] = (dzv * uc).astype(BF16)
        dp_ref[1] = (du * xv).astype(BF16)
        dp_ref[2] = (du * cg).astype(BF16)
        _accumulate(i, a_w0, _fold_rows(duc * u2))
        _accumulate(i, a_w1, _fold_rows(duc * u1))
        _accumulate(i, a_w2, _fold_rows(duc * u))
        _accumulate(i, a_b, _fold_rows(duc))

        @pl.when(i == nsteps - 1)
        def _():
            dw_ref[0:1, :] = jnp.sum(a_w0[...], axis=0, keepdims=True)
            dw_ref[1:2, :] = jnp.sum(a_w1[...], axis=0, keepdims=True)
            dw_ref[2:3, :] = jnp.sum(a_w2[...], axis=0, keepdims=True)
            db_ref[...] = jnp.sum(a_b[...], axis=0, keepdims=True)

    swap = lambda spec: pl.BlockSpec(spec.block_shape, lambda j, i, _m=spec.index_map: _m(i, j))
    return pl.pallas_call(
        body, name=name, grid=(d // tn, nsteps),
        in_specs=[pl.BlockSpec((tb, tn), lambda j, i: (i, j)), swap(cur(0)), swap(cur(1)), swap(cur(2)),
                  swap(prev(1)), swap(prev(2)), swap(nxt(0)), swap(nxt(0)),
                  pl.BlockSpec((3, tn), lambda j, i: (0, j)), pl.BlockSpec((1, tn), lambda j, i: (0, j))],
        out_specs=[pl.BlockSpec((3, tb, tn), lambda j, i: (0, i, j)), pl.BlockSpec((3, tn), lambda j, i: (0, j)),
                   pl.BlockSpec((1, tn), lambda j, i: (0, j))],
        out_shape=[jax.ShapeDtypeStruct((3, s, d), BF16), jax.ShapeDtypeStruct((3, d), F32),
                   jax.ShapeDtypeStruct((1, d), F32)],
        scratch_shapes=[pltpu.VMEM((SUBLANES, tn), F32)] * 4, compiler_params=_params(2),
    )(dz, p, p, p, p, p, dz, p, conv_w, conv_b)


def _block_cumsum(v):
    tb = v.shape[0]
    row = lax.broadcasted_iota(jnp.int32, v.shape, 0)
    sh = 1
    while sh < tb:
        v = v + jnp.where(row >= sh, pltpu.roll(v, sh, 0), 0.0)
        sh *= 2
    return v


def _fgate_fwd(name, zf, b_f):
    s = zf.shape[0]
    tb = _tile(s, 512, SUBLANES)

    def body(z_ref, b_ref, f_ref, ft_ref, carry):
        i = pl.program_id(0)
        z = z_ref[...] + b_ref[...]
        ls = jnp.minimum(z, 0.0) - jnp.log(1.0 + jnp.exp(-jnp.abs(z)))
        run = _block_cumsum(ls)

        @pl.when(i == 0)
        def _():
            carry[...] = jnp.zeros_like(carry)

        out = run + carry[...]
        f_ref[...] = out
        ft_ref[...] = jnp.transpose(out)
        carry[...] = out[tb - 1:tb, :]

    blk = pl.BlockSpec((tb, LANES), lambda i: (i, 0))
    return pl.pallas_call(
        body, name=name, grid=(s // tb,), in_specs=[blk, pl.BlockSpec((1, LANES), lambda i: (0, 0))],
        out_specs=[blk, pl.BlockSpec((LANES, tb), lambda i: (0, i))],
        out_shape=[jax.ShapeDtypeStruct((s, LANES), F32), jax.ShapeDtypeStruct((LANES, s), F32)],
        scratch_shapes=[pltpu.VMEM((1, LANES), F32)], compiler_params=_params(1))(zf, b_f)


def _fgate_bwd(name, dfcum, zf, b_f):
    s = zf.shape[0]
    tb = _tile(s, 512, SUBLANES)
    nsteps = s // tb

    def body(df_ref, z_ref, b_ref, dz_ref, db_ref, carry, acc):
        i = pl.program_id(0)
        df = df_ref[...]
        incl = _block_cumsum(df)
        total = incl[tb - 1:tb, :]

        @pl.when(i == 0)
        def _():
            carry[...] = jnp.zeros_like(carry)

        suffix = total - incl + df + carry[...]
        carry[...] += total
        dzv = suffix * jax.nn.sigmoid(-(z_ref[...] + b_ref[...]))
        dz_ref[...] = dzv
        _accumulate(i, acc, _fold_rows(dzv))

        @pl.when(i == nsteps - 1)
        def _():
            db_ref[...] = jnp.sum(acc[...], axis=0, keepdims=True)

    blk = pl.BlockSpec((tb, LANES), lambda i: (nsteps - 1 - i, 0))
    vec = pl.BlockSpec((1, LANES), lambda i: (0, 0))
    return pl.pallas_call(
        body, name=name, grid=(nsteps,), in_specs=[blk, blk, vec], out_specs=[blk, vec],
        out_shape=[jax.ShapeDtypeStruct((s, LANES), F32), jax.ShapeDtypeStruct((1, LANES), F32)],
        scratch_shapes=[pltpu.VMEM((1, LANES), F32), pltpu.VMEM((SUBLANES, LANES), F32)],
        compiler_params=_params(1))(dfcum, zf, b_f)


def _lane_column(block, lane):
    sel = lax.broadcasted_iota(jnp.int32, block.shape, 1) == lane
    return jnp.sum(jnp.where(sel, block, 0.0), axis=1, keepdims=True)


def _col_to_row(col):
    return jnp.transpose(jnp.broadcast_to(col, (col.shape[0], LANES)))[0:1, :]


def _attn_tiles(s):
    t = _tile(s, 512, LANES)
    return t, t


def _attn_fwd(name, q, kv, fcum_t, n_heads):
    s, d = q.shape
    dh = d // n_heads
    tq, tk = _attn_tiles(s)
    scale2 = LOG2E / math.sqrt(dh)

    def body(q_ref, k_ref, v_ref, ft_ref, o_ref, l_ref):
        qi = pl.program_id(1)
        qv = q_ref[...]

        def step(kj, carry, diagonal):
            m, l, acc = carry
            start = pl.multiple_of(kj * tk, tk)
            kb = k_ref[pl.ds(start, tk), :]
            vb = v_ref[pl.ds(start, tk), :]
            sc = lax.dot_general(qv, kb, _DN["nt"], preferred_element_type=F32) * scale2
            sc = sc - ft_ref[:, pl.ds(start, tk)] * LOG2E
            if diagonal:
                keep = lax.broadcasted_iota(jnp.int32, sc.shape, 0) >= lax.broadcasted_iota(jnp.int32, sc.shape, 1)
                sc = jnp.where(keep, sc, -jnp.inf)
            m_new = jnp.maximum(m, jnp.max(sc, axis=1, keepdims=True))
            alpha = jnp.exp2(m - m_new)
            pr = jnp.exp2(sc - m_new)
            l = alpha * l + jnp.sum(pr, axis=1, keepdims=True)
            acc = alpha * acc + lax.dot_general(pr.astype(BF16), vb, _DN["nn"], preferred_element_type=F32)
            return m_new, l, acc

        init = (jnp.full((tq, 1), -jnp.inf, F32), jnp.zeros((tq, 1), F32), jnp.zeros((tq, dh), F32))
        carry = lax.fori_loop(0, qi, lambda kj, cr: step(kj, cr, False), init)
        m, l, acc = step(qi, carry, True)
        o_ref[...] = (acc / l).astype(BF16)
        l_ref[...] = _col_to_row(m + jnp.log2(l))

    nh = n_heads
    return pl.pallas_call(
        body, name=name, grid=(nh, s // tq),
        in_specs=[pl.BlockSpec((tq, dh), lambda h, i: (i, h)),
                  pl.BlockSpec((s, dh), lambda h, i: (0, h)),
                  pl.BlockSpec((s, dh), lambda h, i: (0, nh + h)),
                  pl.BlockSpec((None, 1, s), lambda h, i: (h, 0, 0))],
        out_specs=[pl.BlockSpec((tq, dh), lambda h, i: (i, h)),
                   pl.BlockSpec((None, 1, tq), lambda h, i: (h, 0, i))],
        out_shape=[jax.ShapeDtypeStruct((s, d), BF16), jax.ShapeDtypeStruct((nh, 1, s), F32)],
        compiler_params=_params(2))(q, kv, kv, fcum_t)


def _attn_delta(name, d_o, o, n_heads):
    s, d = d_o.shape
    dh = d // n_heads
    tb = _row_tile(s)

    def body(do_ref, o_ref, dlt_ref, dob_ref):
        dov = do_ref[...]
        prod = dov * o_ref[...].astype(F32)
        lane = lax.broadcasted_iota(jnp.int32, (tb, LANES), 1)
        out = jnp.zeros((tb, LANES), F32)
        for h in range(n_heads):
            out = jnp.where(lane == h, jnp.sum(prod[:, h * dh:(h + 1) * dh], axis=1, keepdims=True), out)
        dlt_ref[...] = jnp.transpose(out)
        dob_ref[...] = dov.astype(BF16)

    row = pl.BlockSpec((tb, d), lambda i: (i, 0))
    return pl.pallas_call(
        body, name=name, grid=(s // tb,), in_specs=[row, row],
        out_specs=[pl.BlockSpec((LANES, tb), lambda i: (0, i)), row],
        out_shape=[jax.ShapeDtypeStruct((LANES, s), F32), jax.ShapeDtypeStruct((s, d), BF16)],
        compiler_params=_params(1))(d_o, o)


def _attn_bwd(name, q, kv, d_o, fcum, lse_t, delta_t, n_heads):
    s, d = q.shape
    dh = d // n_heads
    tq, tk = _attn_tiles(s)
    nq, nk = s // tq, s // tk
    scale = 1.0 / math.sqrt(dh)
    scale2 = LOG2E * scale

    def body(q_ref, k_ref, v_ref, do_ref, f_ref, lt_ref, dlt_ref, dq_ref, dkv_ref, dfq_ref, dfk_ref, dq_acc, dfq_acc):
        h, kj = pl.program_id(0), pl.program_id(1)
        kb, vb = k_ref[...], v_ref[...]
        fk2 = _lane_column(f_ref[...], h) * LOG2E

        @pl.when(kj == 0)
        def _():
            dq_acc[...] = jnp.zeros_like(dq_acc)
            dfq_acc[...] = jnp.zeros_like(dfq_acc)

        def step(qi, carry, diagonal):
            dk, dv, dfk = carry
            start = pl.multiple_of(qi * tq, tq)
            qb = q_ref[pl.ds(start, tq), :]
            dob = do_ref[pl.ds(start, tq), :]
            sc = lax.dot_general(kb, qb, _DN["nt"], preferred_element_type=F32) * scale2 - fk2
            pr = jnp.exp2(sc - lt_ref[:, pl.ds(start, tq)])
            if diagonal:
                keep = lax.broadcasted_iota(jnp.int32, sc.shape, 1) >= lax.broadcasted_iota(jnp.int32, sc.shape, 0)
                pr = jnp.where(keep, pr, 0.0)
            dv = dv + lax.dot_general(pr.astype(BF16), dob, _DN["nn"], preferred_element_type=F32)
            dp = lax.dot_general(vb, dob, _DN["nt"], preferred_element_type=F32)
            ds = pr * (dp - dlt_ref[:, pl.ds(start, tq)])
            ds_b = ds.astype(BF16)
            dk = dk + lax.dot_general(ds_b, qb, _DN["nn"], preferred_element_type=F32)
            dfk = dfk - jnp.sum(ds, axis=1, keepdims=True)
            dq_acc[pl.ds(start, tq), :] += lax.dot_general(ds_b, kb, _DN["tn"], preferred_element_type=F32)
            dfq_acc[:, pl.ds(start, tq)] += jnp.sum(ds, axis=0, keepdims=True)
            return dk, dv, dfk

        init = (jnp.zeros((tk, dh), F32), jnp.zeros((tk, dh), F32), jnp.zeros((tk, 1), F32))
        carry = step(kj, init, True)
        dk, dv, dfk = lax.fori_loop(kj + 1, nq, lambda qi, cr: step(qi, cr, False), carry)
        dkv_ref[0] = (dk * scale).astype(BF16)
        dkv_ref[1] = dv.astype(BF16)
        dfk_ref[...] = _col_to_row(dfk)

        @pl.when(kj == nk - 1)
        def _():
            dq_ref[...] = (dq_acc[...] * scale).astype(BF16)
            dfq_ref[...] = dfq_acc[...]

    nh = n_heads
    full = pl.BlockSpec((s, dh), lambda h, j: (0, h))
    row_t = pl.BlockSpec((None, 1, s), lambda h, j: (h, 0, 0))
    return pl.pallas_call(
        body, name=name, grid=(nh, nk),
        in_specs=[full, pl.BlockSpec((tk, dh), lambda h, j: (j, h)), pl.BlockSpec((tk, dh), lambda h, j: (j, nh + h)),
                  full, pl.BlockSpec((tk, LANES), lambda h, j: (j, 0)), row_t, row_t],
        out_specs=[full, pl.BlockSpec((2, tk, dh), lambda h, j: (0, j, h)), row_t,
                   pl.BlockSpec((None, 1, tk), lambda h, j: (h, 0, j))],
        out_shape=[jax.ShapeDtypeStruct((s, d), BF16), jax.ShapeDtypeStruct((2, s, d), BF16),
                   jax.ShapeDtypeStruct((nh, 1, s), F32), jax.ShapeDtypeStruct((nh, 1, s), F32)],
        scratch_shapes=[pltpu.VMEM((s, dh), F32), pltpu.VMEM((1, s), F32)],
        compiler_params=_params(2))(q, kv, kv, d_o, fcum, lse_t, delta_t)


def _sum_slots(name, g):
    n, r, lanes = g.shape

    def body(g_ref, o_ref):
        total = g_ref[0]
        for i in range(1, n):
            total = total + g_ref[i]
        o_ref[...] = total

    return pl.pallas_call(
        body, name=name, out_shape=jax.ShapeDtypeStruct((r, lanes), F32),
        in_specs=[pl.BlockSpec(memory_space=pltpu.VMEM)], out_specs=pl.BlockSpec(memory_space=pltpu.VMEM),
        compiler_params=pltpu.CompilerParams(vmem_limit_bytes=VMEM_LIMIT))(g)


def _adamw(name, parts, w, m, v, row0=0, previous=None):
    npart, r, c = parts.shape
    tb = _tile(math.gcd(r, row0) if row0 else r, max(SUBLANES, (2**18 // c) // SUBLANES * SUBLANES), SUBLANES)
    off = row0 // tb
    c1 = 1.0 - ADAM_B1 ** ADAM_STEP
    c2 = 1.0 - ADAM_B2 ** ADAM_STEP

    def body(p_ref, w_ref, m_ref, v_ref, *rest):
        g_out, d_out, m_out, v_out = rest[-4:]
        g = p_ref[0].astype(F32)
        for i in range(1, npart):
            g = g + p_ref[i].astype(F32)
        m_new = ADAM_B1 * m_ref[...] + (1.0 - ADAM_B1) * g
        v_new = ADAM_B2 * v_ref[...] + (1.0 - ADAM_B2) * (g * g)
        m_hat = m_new / c1
        v_hat = v_new / c2
        g_out[...] = g
        d_out[...] = -ADAM_LR * (m_hat / (jnp.sqrt(v_hat) + ADAM_EPS) + ADAM_WD * w_ref[...])
        m_out[...] = m_new
        v_out[...] = v_new

    blk = pl.BlockSpec((tb, c), lambda i: (i + off, 0))
    shape = jax.ShapeDtypeStruct(w.shape, F32)
    in_specs = [pl.BlockSpec((npart, tb, c), lambda i: (0, i, 0)), blk, blk, blk]
    operands = [parts, w, m, v]
    aliases = {}
    if previous is not None:
        in_specs += [pl.BlockSpec(memory_space=pl.ANY)] * 4
        operands += list(previous)
        aliases = {4 + k: k for k in range(4)}
    return pl.pallas_call(
        body, name=name, grid=(r // tb,), in_specs=in_specs, out_specs=[blk] * 4, out_shape=[shape] * 4,
        input_output_aliases=aliases, compiler_params=_params(1))(*operands)


def _pack(vectors):
    flat = jnp.concatenate([v.reshape(-1).astype(F32) for v in vectors])
    pad = (-flat.shape[0]) % (SUBLANES * LANES)
    return jnp.pad(flat, (0, pad)).reshape(-1, LANES)


def _unpack(flat, shapes):
    out, pos = [], 0
    for shp in shapes:
        size = math.prod(shp)
        out.append(flat[..., pos:pos + size].reshape(flat.shape[:-1] + tuple(shp)))
        pos += size
    return out


def _unshard_last(g, lead):
    nd = len(lead)
    return jnp.moveaxis(g, 0, nd).reshape(tuple(lead) + (-1,))


def _my_slice(full, me, width, axis):
    return lax.dynamic_slice_in_dim(full, me * width, width, axis)


def kernel(x, c, norm_g, w_ada, b_ada, w_ffn_in, w_ffn_out, w_conv_in, conv_w, conv_b, w_conv_out, kv_norm_g, w_ada_kv, b_ada_kv, w_kvf, b_fgate, w_q, w_o, final_g, loss_target, m_norm_g, m_w_ada, m_b_ada, m_w_ffn_in, m_w_ffn_out, m_w_conv_in, m_conv_w, m_conv_b, m_w_conv_out, m_kv_norm_g, m_w_ada_kv, m_b_ada_kv, m_w_kvf, m_b_fgate, m_w_q, m_w_o, m_final_g, v_norm_g, v_w_ada, v_b_ada, v_w_ffn_in, v_w_ffn_out, v_w_conv_in, v_conv_w, v_conv_b, v_w_conv_out, v_kv_norm_g, v_w_ada_kv, v_b_ada_kv, v_w_kvf, v_b_fgate, v_w_q, v_w_o, v_final_g):
    s, d = x.shape[1], x.shape[2]
    n_heads = b_fgate.shape[0]
    n_layers = w_ada.shape[0]
    dsh = d // N_DEV
    f = w_ffn_out.shape[2] * N_DEV
    ada_w = w_ada.shape[2]
    kv_w = w_ada_kv.shape[1]
    kvf_w = w_kvf.shape[1]
    assert n_layers == 2 and w_conv_in.shape[0] == 1 and w_q.shape[0] == 1
    assert (d // n_heads) % LANES == 0 and n_heads <= LANES and N_DEV * kvf_w == 2 * d + n_heads
    me = 4 * lax.axis_index("x") + 2 * lax.axis_index("y") + lax.axis_index("c")
    x0, target = x[0], loss_target[0]

    small = _allgather_small("gather_small_params", _pack([c, norm_g, conv_w, conv_b]))
    c_all, ng_sh, cw_sh, cb_sh = _unpack(small.reshape(N_DEV, -1), [(d,), (n_layers, 3, dsh), (3, dsh), (dsh,)])
    norm_g_full = _unshard_last(ng_sh, (n_layers, 3))
    conv_w_full = _unshard_last(cw_sh, (3,))
    conv_b_full = _unshard_last(cb_sh, ()).reshape(1, d)
    c_rows = jnp.pad(c_all, ((0, 16 - N_DEV), (0, 0)))

    ada_cols = [
        _mm_bias_rows(f"ada_rows_{l}", c_rows, w_ada[l], _my_slice(b_ada[l], me, ada_w, 0).reshape(1, ada_w), _silu)
        for l in range(n_layers)]
    ada_cols.append(_mm_bias_rows("ada_rows_kv", c_rows, w_ada_kv, _my_slice(b_ada_kv, me, kv_w, 0).reshape(1, kv_w),
                                  _silu))
    ada_part = jnp.concatenate([a[:N_DEV] for a in ada_cols], axis=1)
    ada_all = _allgather_small("gather_ada_rows", _pack([ada_part]))
    ada_all = ada_all.reshape(N_DEV, -1)[:, :ada_part.size].reshape(N_DEV, N_DEV, -1)
    ada_mine = lax.dynamic_index_in_dim(ada_all, me, axis=1, keepdims=False)
    ada = [ada_mine[:, l * ada_w:(l + 1) * ada_w].reshape(3, 3, 1, d) for l in range(n_layers)]
    ada_kv = ada_mine[:, n_layers * ada_w:].reshape(2, 1, d)

    def as_item(w, kind):
        return (w.astype(BF16), kind)

    gather_first = _GatherComm([as_item(w_ffn_in[0, 0], "cols")])
    gather_l0b = _GatherComm([as_item(w_ffn_out[0, 0], "rows"), as_item(w_ffn_in[0, 1], "cols")])
    gather_conv = _GatherComm([as_item(w_conv_in[0], "cols"), as_item(w_conv_out[0], "rows")])
    gather_l0b_out = _GatherComm([as_item(w_ffn_out[0, 1], "rows")])
    gather_l1a = _GatherComm([as_item(w_ffn_out[1, 0], "rows"), as_item(w_ffn_in[1, 0], "cols")])
    gather_attn = _GatherComm([as_item(w_q[0], "rows"), as_item(w_kvf, "lead"), as_item(w_o[0], "rows")])
    gather_l1b = _GatherComm([as_item(w_ffn_out[1, 1], "rows"), as_item(w_ffn_in[1, 1], "cols")])
    (wf_in00,) = _run_comm("gather_ffn_first", gather_first)
    b_f = jnp.pad(b_fgate, (0, LANES - n_heads)).reshape(1, LANES)

    def ffn_fwd(tag, xs, l, sub, w_in, w_out, comm_in, comm_out):
        shift, scale, gate = ada[l][sub]
        tiles = FFN_TILES
        h = _norm_mod(f"norm_{tag}", xs, norm_g_full[l, sub].reshape(1, d), shift, scale)
        ab, u = _mm_swiglu(f"ffn_in_{tag}", h, w_in, f, tiles["in"], comm=comm_in)
        y, x_next = _mm_residual(f"ffn_out_{tag}", u, w_out, xs, 0.5 * gate, tiles["out"], comm=comm_out)
        return x_next, (xs, h, ab, u, y, w_in, w_out)

    h = _norm_mod("norm_l0a", x0, norm_g_full[0, 0].reshape(1, d), ada[0][0][0], ada[0][0][1])
    ab, u = _mm_swiglu("ffn_in_l0a", h, wf_in00, f, FFN_TILES["in"], comm=gather_l0b)
    wf_out00, wf_in01 = gather_l0b.results
    y, x1 = _mm_residual("ffn_out_l0a", u, wf_out00, x0, 0.5 * ada[0][0][2], FFN_TILES["out"], comm=gather_conv)
    save_f0 = (x0, h, ab, u, y, wf_in00, wf_out00)
    wc_in, wc_out = gather_conv.results
    shift, scale, gate = ada[0][1]
    h_c = _norm_mod("norm_conv", x1, norm_g_full[0, 1].reshape(1, d), shift, scale)
    p_c = _mm_nn("conv_in", h_c, wc_in, F32, comm=gather_l0b_out)
    (wf_out01,) = gather_l0b_out.results
    z_c = _conv_fwd("conv_mix", p_c, conv_w_full, conv_b_full)
    y_c, x2 = _mm_residual("conv_out", z_c, wc_out, x1, gate)
    x3, save_f1 = ffn_fwd("l0b", x2, 0, 2, wf_in01, wf_out01, gather_l1a, gather_attn)
    wf_out10, wf_in10 = gather_l1a.results
    wq, g_kvf, wo = gather_attn.results
    wkvf = _unshard_last(g_kvf, (d,))
    wkv = wkvf[:, :2 * d]
    wf_gate = jnp.pad(wkvf[:, 2 * d:], ((0, 0), (0, LANES - n_heads)))
    x4, save_f2 = ffn_fwd("l1a", x3, 1, 0, wf_in10, wf_out10, gather_l1b, None)
    wf_out11, wf_in11 = gather_l1b.results
    shift, scale, gate_a = ada[1][1]
    h_q = _norm_mod("norm_q", x4, norm_g_full[1, 1].reshape(1, d), shift, scale)
    h_kv = _norm_mod("norm_kv", x4, kv_norm_g.reshape(1, d), ada_kv[0], ada_kv[1])
    q = _mm_nn("attn_q", h_q, wq, BF16)
    kv = _mm_nn("attn_kv", h_kv, wkv, BF16)
    zf = _mm_nn("attn_fgate", h_kv, wf_gate, F32)
    fcum, fcum_lanes = _fgate_fwd("fgate_cumsum", zf, b_f)
    fcum_t = fcum_lanes[:n_heads].reshape(n_heads, 1, s)
    o, lse_t = _attn_fwd("attn_fwd", q, kv, fcum_t, n_heads)
    y_a, x5 = _mm_residual("attn_out", o, wo, x4, gate_a)
    x6, save_f3 = ffn_fwd("l1b", x5, 1, 2, wf_in11, wf_out11, None, None)
    dx, loss_part, d_final_g = _final_loss("final_loss", x6, target, final_g.reshape(1, d))

    d_ada = [[[None] * 3 for _ in range(3)] for _ in range(n_layers)]
    d_norm_g = [[None] * 3 for _ in range(n_layers)]
    core = lax.axis_index("c").astype(jnp.int32).reshape(1)
    in_shard, out_shard, row_shard = (d, 2 * f // N_DEV), (f // N_DEV, d), (dsh, d)

    def chip_exchange(tag, items, from_sibling):
        return _ChipComm([_pair_add(f"add_sibling_{tag}_{n}", p, kind, shape, t, core)
                          for n, ((p, kind, shape), t) in enumerate(zip(items, from_sibling))])

    def reduce_within_chip(tag, items):
        return chip_exchange(tag, items, _run_comm(f"reduce_sibling_{tag}", _SiblingComm(items)))

    def ffn_bwd(tag, dxs, dy, saved, l, sub, comm_dw, then):
        xs, h, ab, u, y, w_in, w_out = saved
        tiles = FFN_TILES
        dab = _mm_nt_swiglu_bwd(f"ffn_out_bwd_{tag}", dy, w_out, ab, tiles["out_bwd"])
        dw_in = _mm_tn(f"ffn_in_dw_{tag}", h, dab, BF16, tiles=tiles["in_dw"])
        sibling_in = _SiblingComm([(dw_in, "cols", in_shard)])
        dw_out = _mm_tn(f"ffn_out_dw_{tag}", u, dy, BF16, tiles=tiles["out_dw"], comm=_join(sibling_in, comm_dw))
        sibling_out = _run_comm(f"reduce_sibling_{tag}", _SiblingComm([(dw_out, "rows", out_shard)]))
        scatter = chip_exchange(tag, [(dw_in, "cols", in_shard), (dw_out, "rows", out_shard)],
                                list(sibling_in.results) + list(sibling_out))
        dh = _mm_nt_parts(f"ffn_in_bwd_{tag}", dab, w_in, F32, tiles["in_bwd"], comm=scatter)
        outs = _norm_mod_bwd(f"norm_bwd_{tag}", dh, xs, norm_g_full[l, sub].reshape(1, d), ada[l][sub][1], dxs, then)
        dxs, d_ada[l][sub][0], d_ada[l][sub][1], d_norm_g[l][sub] = outs[:4]
        return dxs, outs[4:], scatter

    def gate_of(saved, l, sub):
        return (saved[4], 0.5 * ada[l][sub][2], 0.5)

    dy, d_ada[1][2][2] = _gate_bwd("gate_bwd_l1b", dx, save_f3[4], 0.5 * ada[1][2][2], 0.5)
    dx, (dy, d_ada[1][1][2]), scatter_l1b = ffn_bwd("l1b", dx, dy, save_f3, 1, 2, None, (y_a, gate_a, 1.0))
    d_o = _mm_nt("attn_out_bwd", dy, wo, F32)
    dwo = _mm_tn("attn_out_dw", o, dy, BF16)
    delta_lanes, d_ob = _attn_delta("attn_delta", d_o, o, n_heads)
    delta_t = delta_lanes[:n_heads].reshape(n_heads, 1, s)
    dq, dkv, dfq, dfk = _attn_bwd("attn_bwd", q, kv, d_ob, fcum, lse_t, delta_t, n_heads)
    dfcum = jnp.pad((dfq + dfk).reshape(n_heads, s).T, ((0, 0), (0, LANES - n_heads)))
    dzf, d_bf = _fgate_bwd("fgate_bwd", dfcum, zf, b_f)
    dh_q = _mm_nt("attn_q_bwd", dq, wq, F32)
    dwq = _mm_tn("attn_q_dw", h_q, dq, BF16)
    dh_kv = _mm_nt_parts("attn_kv_bwd", dkv, wkv, F32)
    dh_kv = _mm_nt("attn_fgate_bwd", dzf, wf_gate, F32, extra_add=dh_kv)
    dwkv = _mm_tn("attn_kv_dw", h_kv, dkv, BF16)
    dwf = _mm_tn("attn_fgate_dw", h_kv, dzf, BF16)
    dwkvf = jnp.concatenate([dwkv, dwf[:, :n_heads]], axis=1)
    p_kvf = jnp.moveaxis(dwkvf.reshape(d, N_DEV, kvf_w), 1, 0)
    scatter_attn = reduce_within_chip(
        "attn", [(dwo, "rows", row_shard), (dwq, "rows", row_shard), (p_kvf, "lead", (d, kvf_w))])
    dx, d_ada[1][1][0], d_ada[1][1][1], d_norm_g[1][1] = _norm_mod_bwd(
        "norm_bwd_q", dh_q, x4, norm_g_full[1, 1].reshape(1, d), ada[1][1][1], dx)
    dx, d_kv_shift, d_kv_scale, d_kv_norm_g, dy, d_ada[1][0][2] = _norm_mod_bwd(
        "norm_bwd_kv", dh_kv, x4, kv_norm_g.reshape(1, d), ada_kv[1], dx, gate_of(save_f2, 1, 0))
    dx, (dy, d_ada[0][2][2]), scatter_l1a = ffn_bwd("l1a", dx, dy, save_f2, 1, 0, scatter_attn, gate_of(save_f1, 0, 2))
    dx, (dy, d_ada[0][1][2]), scatter_l0b = ffn_bwd("l0b", dx, dy, save_f1, 0, 2, None, (y_c, ada[0][1][2], 1.0))
    dz = _mm_nt("conv_out_bwd", dy, wc_out, F32)
    dwc_out = _mm_tn("conv_out_dw", z_c, dy, BF16)
    dp, d_conv_w, d_conv_b = _conv_bwd("conv_mix_bwd", dz, p_c, conv_w_full, conv_b_full)
    dh_c = _mm_nt_parts("conv_in_bwd", dp, wc_in, F32)
    dwc_in = _mm_tn("conv_in_dw", h_c, dp, BF16)
    scatter_conv = reduce_within_chip("conv", [(dwc_out, "rows", row_shard), (dwc_in, "cols", (d, 3 * d // N_DEV))])
    dx, d_ada[0][1][0], d_ada[0][1][1], d_norm_g[0][1], dy, d_ada[0][0][2] = _norm_mod_bwd(
        "norm_bwd_conv", dh_c, x1, norm_g_full[0, 1].reshape(1, d), ada[0][1][1], dx, gate_of(save_f0, 0, 0))
    dx, _, scatter_l0a = ffn_bwd("l0a", dx, dy, save_f0, 0, 0, scatter_conv, None)
    grad_x = dx.reshape(1, s, d)

    d_ada_flat = jnp.concatenate([v.reshape(-1) for l in range(n_layers) for sub in range(3) for v in d_ada[l][sub]])
    d_norm_flat = jnp.concatenate([d_norm_g[l][sub].reshape(-1) for l in range(n_layers) for sub in range(3)])
    small_shapes = [(n_layers * 9 * d,), (2 * d,), (n_layers * 3 * d,), (3, d), (d,), (d,), (LANES,), (d,), (LANES,)]
    small_parts = _pack([d_ada_flat, d_kv_shift, d_kv_scale, d_norm_flat, d_conv_w, d_conv_b, d_kv_norm_g, d_bf,
                         d_final_g, loss_part])
    small_all = _allgather_small("gather_small_grads", small_parts)
    small_sum = _sum_slots("sum_small_grads", small_all).reshape(-1)
    g_b_ada, g_b_ada_kv, g_norm_full, g_conv_w_full, g_conv_b_full, g_kv_norm, g_bf, g_final, loss_v = _unpack(
        small_sum, small_shapes)
    loss = loss_v[0]
    d_ada_rows = small_all.reshape(N_DEV, -1)[:, :n_layers * 9 * d + 2 * d]
    d_ada_rows = jnp.pad(d_ada_rows, ((0, 16 - N_DEV), (0, 0)))

    results = {}

    def update(key, parts, w, m, v):
        shape = w.shape
        c_dim = shape[-1]
        outs = _adamw(f"adamw_{key}", parts.reshape(parts.shape[0], -1, c_dim), w.reshape(-1, c_dim),
                      m.reshape(-1, c_dim), v.reshape(-1, c_dim))
        results[key] = [o.reshape(shape) for o in outs]

    g_w_ada = []
    for l in range(n_layers):
        cols = _my_slice(d_ada_rows[:, l * 9 * d:(l + 1) * 9 * d], me, ada_w, 1)
        g_w_ada.append(_mm_tn(f"ada_dw_{l}", c_rows, cols, F32, a_fn=_silu))
    update("w_ada", jnp.stack(g_w_ada).reshape(1, n_layers * d, ada_w), w_ada, m_w_ada, v_w_ada)
    cols = _my_slice(d_ada_rows[:, n_layers * 9 * d:], me, kv_w, 1)
    update("w_ada_kv", _mm_tn("ada_dw_kv", c_rows, cols, F32, a_fn=_silu).reshape(1, d, kv_w),
           w_ada_kv, m_w_ada_kv, v_w_ada_kv)

    def update_rows(key, parts_list, w, m, v):
        c_dim = w.shape[-1]
        flat = [t.reshape(-1, c_dim) for t in (w, m, v)]
        outs = None
        for n, parts in enumerate(parts_list):
            outs = _adamw(f"adamw_{key}_{n}", parts, *flat, row0=n * parts.shape[1], previous=outs)
        results[key] = [o.reshape(w.shape) for o in outs]

    ffn_scatters = [scatter_l0a, scatter_l0b, scatter_l1a, scatter_l1b]
    update_rows("w_ffn_in", [sc.results[0] for sc in ffn_scatters], w_ffn_in, m_w_ffn_in, v_w_ffn_in)
    update_rows("w_ffn_out", [sc.results[1] for sc in ffn_scatters], w_ffn_out, m_w_ffn_out, v_w_ffn_out)
    r_co, r_ci = scatter_conv.results
    r_o, r_q, r_kvf = scatter_attn.results
    update_rows("w_conv_in", [r_ci], w_conv_in, m_w_conv_in, v_w_conv_in)
    update_rows("w_conv_out", [r_co], w_conv_out, m_w_conv_out, v_w_conv_out)
    update_rows("w_kvf", [r_kvf], w_kvf, m_w_kvf, v_w_kvf)
    update_rows("w_q", [r_q], w_q, m_w_q, v_w_q)
    update_rows("w_o", [r_o], w_o, m_w_o, v_w_o)

    small_keys = ["norm_g", "b_ada", "conv_w", "conv_b", "kv_norm_g", "b_ada_kv", "b_fgate", "final_g"]
    small_w = [norm_g, b_ada, conv_w, conv_b, kv_norm_g, b_ada_kv, b_fgate, final_g]
    small_m = [m_norm_g, m_b_ada, m_conv_w, m_conv_b, m_kv_norm_g, m_b_ada_kv, m_b_fgate, m_final_g]
    small_v = [v_norm_g, v_b_ada, v_conv_w, v_conv_b, v_kv_norm_g, v_b_ada_kv, v_b_fgate, v_final_g]
    small_g = [
        _my_slice(g_norm_full.reshape(n_layers, 3, d), me, dsh, 2), g_b_ada.reshape(b_ada.shape),
        _my_slice(g_conv_w_full, me, dsh, 1).reshape(conv_w.shape), _my_slice(g_conv_b_full, me, dsh, 0).reshape(
            conv_b.shape), g_kv_norm, g_b_ada_kv, g_bf[:n_heads], g_final]
    packed = _adamw("adamw_small", _pack(small_g)[None], _pack(small_w), _pack(small_m), _pack(small_v))
    for vals, idx in zip(packed, range(4)):
        for key, val in zip(small_keys, _unpack(vals.reshape(-1), [w.shape for w in small_w])):
            results.setdefault(key, [None] * 4)[idx] = val

    order = ["norm_g", "w_ada", "b_ada", "w_ffn_in", "w_ffn_out", "w_conv_in", "conv_w", "conv_b", "w_conv_out",
             "kv_norm_g", "w_ada_kv", "b_ada_kv", "w_kvf", "b_fgate", "w_q", "w_o", "final_g"]
    return (loss, grad_x, *[results[k][0] for k in order], *[results[k][1] for k in order],
            *[results[k][2] for k in order], *[results[k][3] for k in order])
```

```python
import functools
import math

import jax
import jax.numpy as jnp
from jax import lax
from jax.experimental import pallas as pl
from jax.experimental.pallas import tpu as pltpu

F32 = jnp.float32
BF16 = jnp.bfloat16
MESH = pl.DeviceIdType.MESH

N_DEV = 8
LANES = 128
SUBLANES = 8
VMEM_LIMIT = 56 * 2**20
EPS = 1e-6
LOG2E = math.log2(math.e)
ADAM_LR, ADAM_B1, ADAM_B2, ADAM_EPS, ADAM_WD, ADAM_STEP = 0.001, 0.9, 0.999, 1e-08, 0.01, 10

_DN = {"nn": (((1,), (0,)), ((), ())), "nt": (((1,), (1,)), ((), ())), "tn": (((0,), (0,)), ((), ()))}


def _tile(n, pref, align):
    t = min(pref, n)
    t -= t % align
    while t >= align:
        if n % t == 0:
            return t
        t -= align
    return n


def _params(n_grid):
    return pltpu.CompilerParams(vmem_limit_bytes=VMEM_LIMIT, dimension_semantics=("arbitrary",) * n_grid)


def _sigmoid(v):
    return 0.5 * jnp.tanh(0.5 * v) + 0.5


def _silu(v):
    return v * _sigmoid(v)


def _position():
    x, y, c = lax.axis_index("x"), lax.axis_index("y"), lax.axis_index("c")
    return x, y, c


def _allgather_small(name, v):
    def body(v_ref, out_ref, send_sems, recv_sems, local_sem):
        x, y, c = _position()
        me, sibling = (x, y, c), (x, y, 1 - c)
        chips = [(1 - x, y), (x, 1 - y), (1 - x, 1 - y)]

        def slot(px, py, pc):
            return out_ref.at[4 * px + 2 * py + pc]

        def copy(k, block, to, src=None):
            return pltpu.make_async_remote_copy(
                src_ref=slot(*block) if src is None else src, dst_ref=slot(*block),
                send_sem=send_sems.at[k], recv_sem=recv_sems.at[k], device_id=to, device_id_type=MESH)

        mine = pltpu.make_async_copy(v_ref, slot(*me), local_sem)
        mine.start()
        first = [copy(0, me, sibling, src=v_ref)]
        first += [copy(1 + j, me, (*chip, c), src=v_ref) for j, chip in enumerate(chips)]
        for cp in first:
            cp.start()
        passed = [copy(4 + j, (*chip, c), sibling) for j, chip in enumerate(chips)]
        for j, chip in enumerate(chips):
            copy(1 + j, (*chip, c), me).wait_recv()
            passed[j].start()
        copy(0, sibling, me).wait_recv()
        for j, chip in enumerate(chips):
            copy(4 + j, (*chip, 1 - c), me).wait_recv()
        for cp in first + passed:
            cp.wait_send()
        mine.wait()

    return pl.pallas_call(
        body, name=name,
        out_shape=jax.ShapeDtypeStruct((N_DEV,) + v.shape, v.dtype),
        in_specs=[pl.BlockSpec(memory_space=pltpu.VMEM)],
        out_specs=pl.BlockSpec(memory_space=pltpu.VMEM),
        scratch_shapes=[pltpu.SemaphoreType.DMA((7,)), pltpu.SemaphoreType.DMA((7,)), pltpu.SemaphoreType.DMA],
        compiler_params=pltpu.CompilerParams(vmem_limit_bytes=VMEM_LIMIT),
    )(v)


def _owner_view(ref, kind, shard_shape, owner):
    r, w = shard_shape
    if kind == "rows":
        return ref.at[pl.ds(pl.multiple_of(owner * r, 16), r)]
    if kind == "cols":
        return ref.at[:, pl.ds(pl.multiple_of(owner * w, LANES), w)]
    return ref.at[owner]


def _full_shape(kind, shard_shape):
    r, w = shard_shape
    return {"rows": (N_DEV * r, w), "cols": (r, N_DEV * w), "lead": (N_DEV, r, w)}[kind]


class _GatherComm:
    mid_fraction = 0.85

    def __init__(self, items):
        self.ins = [shard for shard, _ in items]
        self.kinds = [kind for _, kind in items]
        n = len(items)
        self.out_shapes = [jax.ShapeDtypeStruct(_full_shape(k, a.shape), a.dtype) for a, k in items]
        self.sem_shapes = [pltpu.SemaphoreType.DMA((7 * n,)), pltpu.SemaphoreType.DMA((7 * n,)),
                           pltpu.SemaphoreType.DMA((n,))]
        self.results = None

    def _copy(self, refs, a, k, block, to, from_input=False):
        ins, outs, (send_sems, recv_sems, _) = refs
        px, py, pc = block
        dst = _owner_view(outs[a], self.kinds[a], self.ins[a].shape, 4 * px + 2 * py + pc)
        return pltpu.make_async_remote_copy(
            src_ref=ins[a] if from_input else dst, dst_ref=dst,
            send_sem=send_sems.at[7 * a + k], recv_sem=recv_sems.at[7 * a + k], device_id=to, device_id_type=MESH)

    def _local(self, refs, a):
        ins, outs, (_, _, local_sems) = refs
        x, y, c = _position()
        dst = _owner_view(outs[a], self.kinds[a], self.ins[a].shape, 4 * x + 2 * y + c)
        return pltpu.make_async_copy(ins[a], dst, local_sems.at[a])

    def start(self, *refs):
        x, y, c = _position()
        me, sibling = (x, y, c), (x, y, 1 - c)
        chips = [(1 - x, y), (x, 1 - y), (1 - x, 1 - y)]
        for a in range(len(self.ins)):
            self._local(refs, a).start()
            for j, chip in enumerate(chips):
                self._copy(refs, a, 1 + j, me, (*chip, c), from_input=True).start()
            self._copy(refs, a, 0, me, sibling, from_input=True).start()

    def mid(self, *refs):
        x, y, c = _position()
        chips = [(1 - x, y), (x, 1 - y), (1 - x, 1 - y)]
        for a in range(len(self.ins)):
            for j, chip in enumerate(chips):
                self._copy(refs, a, 1 + j, (*chip, c), (x, y, c)).wait_recv()
                self._copy(refs, a, 4 + j, (*chip, c), (x, y, 1 - c)).start()

    def finish(self, *refs):
        x, y, c = _position()
        me, sibling = (x, y, c), (x, y, 1 - c)
        chips = [(1 - x, y), (x, 1 - y), (1 - x, 1 - y)]
        for a in range(len(self.ins)):
            self._copy(refs, a, 0, sibling, me).wait_recv()
            for j, chip in enumerate(chips):
                self._copy(refs, a, 4 + j, (*chip, 1 - c), me).wait_recv()
        for a in range(len(self.ins)):
            self._copy(refs, a, 0, me, sibling, from_input=True).wait_send()
            for j, chip in enumerate(chips):
                self._copy(refs, a, 1 + j, me, (*chip, c), from_input=True).wait_send()
                self._copy(refs, a, 4 + j, (*chip, c), sibling).wait_send()
            self._local(refs, a).wait()


class _SiblingComm:
    mid_fraction = None

    def __init__(self, items):
        self.ins = [p for p, _, _ in items]
        self.kinds = [kind for _, kind, _ in items]
        self.shapes = [shape for _, _, shape in items]
        n = len(items)
        self.out_shapes = [jax.ShapeDtypeStruct((4,) + tuple(shape), p.dtype) for p, _, shape in items]
        self.sem_shapes = [pltpu.SemaphoreType.DMA((4 * n,)), pltpu.SemaphoreType.DMA((4 * n,))]
        self.results = None

    def _copies(self, refs):
        ins, outs, (send_sems, recv_sems) = refs
        x, y, c = _position()
        return [pltpu.make_async_remote_copy(
            src_ref=_owner_view(ins[a], self.kinds[a], self.shapes[a], 2 * k + 1 - c), dst_ref=outs[a].at[k],
            send_sem=send_sems.at[4 * a + k], recv_sem=recv_sems.at[4 * a + k],
            device_id=(x, y, 1 - c), device_id_type=MESH) for a in range(len(self.ins)) for k in range(4)]

    def start(self, *refs):
        for cp in self._copies(refs):
            cp.start()

    def finish(self, *refs):
        for cp in self._copies(refs):
            cp.wait()


class _ChipComm:
    mid_fraction = None

    def __init__(self, items):
        self.ins = list(items)
        n = len(items)
        self.out_shapes = [jax.ShapeDtypeStruct(p.shape, p.dtype) for p in items]
        self.sem_shapes = [pltpu.SemaphoreType.DMA((3 * n,)), pltpu.SemaphoreType.DMA((3 * n,)),
                           pltpu.SemaphoreType.DMA((n,))]
        self.results = None

    def _copy(self, refs, a, k, landed):
        ins, outs, (send_sems, recv_sems, _) = refs
        x, y, c = _position()
        px = 1 - x if k & 2 else x
        py = 1 - y if k & 1 else y
        to_chip, my_chip = 2 * px + py, 2 * x + y
        return pltpu.make_async_remote_copy(
            src_ref=ins[a].at[to_chip], dst_ref=outs[a].at[to_chip if landed else my_chip],
            send_sem=send_sems.at[3 * a + k - 1], recv_sem=recv_sems.at[3 * a + k - 1],
            device_id=(px, py, c), device_id_type=MESH)

    def _local(self, refs, a):
        ins, outs, (_, _, local_sems) = refs
        x, y, _ = _position()
        return pltpu.make_async_copy(ins[a].at[2 * x + y], outs[a].at[2 * x + y], local_sems.at[a])

    def start(self, *refs):
        for a in range(len(self.ins)):
            self._local(refs, a).start()
            for k in range(1, 4):
                self._copy(refs, a, k, False).start()

    def finish(self, *refs):
        for a in range(len(self.ins)):
            for k in range(1, 4):
                self._copy(refs, a, k, True).wait_recv()
        for a in range(len(self.ins)):
            for k in range(1, 4):
                self._copy(refs, a, k, False).wait_send()
            self._local(refs, a).wait()


class _Joined:
    def __init__(self, comms):
        self.comms = list(comms)
        self.ins = [a for c in self.comms for a in c.ins]
        self.out_shapes = [o for c in self.comms for o in c.out_shapes]
        self.sem_shapes = [o for c in self.comms for o in c.sem_shapes]
        fractions = [c.mid_fraction for c in self.comms if c.mid_fraction is not None]
        self.mid_fraction = max(fractions) if fractions else None

    def _parts(self, refs):
        ins, outs, sems = refs
        pos = [0, 0, 0]
        for c in self.comms:
            n = (len(c.ins), len(c.out_shapes), len(c.sem_shapes))
            yield c, (ins[pos[0]:pos[0] + n[0]], outs[pos[1]:pos[1] + n[1]], sems[pos[2]:pos[2] + n[2]])
            pos = [p + k for p, k in zip(pos, n)]

    def start(self, *refs):
        for c, part in self._parts(refs):
            c.start(*part)

    def mid(self, *refs):
        for c, part in self._parts(refs):
            if c.mid_fraction is not None:
                c.mid(*part)

    def finish(self, *refs):
        for c, part in self._parts(refs):
            c.finish(*part)

    @property
    def results(self):
        return [r for c in self.comms for r in c.results]

    @results.setter
    def results(self, values):
        pos = 0
        for c in self.comms:
            c.results = values[pos:pos + len(c.out_shapes)]
            pos += len(c.out_shapes)


def _join(*comms):
    present = [c for c in comms if c is not None]
    return present[0] if len(present) == 1 else (_Joined(present) if present else None)


def _run_comm(name, comm):
    n_in, n_out = len(comm.ins), len(comm.out_shapes)

    def body(*refs):
        parts = (refs[:n_in], refs[n_in:n_in + n_out], refs[n_in + n_out:])
        comm.start(*parts)
        if comm.mid_fraction is not None:
            comm.mid(*parts)
        comm.finish(*parts)

    any_spec = pl.BlockSpec(memory_space=pl.ANY)
    return pl.pallas_call(
        body, name=name, out_shape=list(comm.out_shapes), in_specs=[any_spec] * n_in, out_specs=[any_spec] * n_out,
        scratch_shapes=list(comm.sem_shapes))(*comm.ins)


def _carry(comm, body, grid, in_specs, out_specs, out_shapes, scratch_shapes, operands):
    if comm is None:
        return body, in_specs, out_specs, out_shapes, scratch_shapes, operands
    n_in, n_out, n_scr = len(in_specs), len(out_specs), len(scratch_shapes)
    c_in, c_out = len(comm.ins), len(comm.out_shapes)
    total = math.prod(grid)
    mid_step = None if comm.mid_fraction is None else min(total - 1, int(total * comm.mid_fraction))

    def wrapped(*refs):
        ins, c_ins = refs[:n_in], refs[n_in:n_in + c_in]
        outs = refs[n_in + c_in:n_in + c_in + n_out]
        c_outs = refs[n_in + c_in + n_out:n_in + c_in + n_out + c_out]
        scr = refs[n_in + c_in + n_out + c_out:n_in + c_in + n_out + c_out + n_scr]
        parts = (c_ins, c_outs, refs[n_in + c_in + n_out + c_out + n_scr:])
        step = pl.program_id(0)
        for axis in range(1, len(grid)):
            step = step * grid[axis] + pl.program_id(axis)

        @pl.when(step == 0)
        def _():
            comm.start(*parts)

        body(*ins, *outs, *scr)

        if mid_step is not None:
            @pl.when(step == mid_step)
            def _():
                comm.mid(*parts)

        @pl.when(step == total - 1)
        def _():
            comm.finish(*parts)

    any_spec = pl.BlockSpec(memory_space=pl.ANY)
    return (wrapped, list(in_specs) + [any_spec] * c_in, list(out_specs) + [any_spec] * c_out,
            list(out_shapes) + list(comm.out_shapes), list(scratch_shapes) + list(comm.sem_shapes),
            list(operands) + list(comm.ins))


def _pair_add(name, partial, kind, shard_shape, theirs, core):
    r, w = shard_shape
    tb = _tile(r, max(16, (2**19 // w) // 16 * 16), 16)
    nb = r // tb
    if kind == "rows":
        mine = pl.BlockSpec((tb, w), lambda k, i, core_ref: ((2 * k + core_ref[0]) * nb + i, 0))
    elif kind == "cols":
        mine = pl.BlockSpec((tb, w), lambda k, i, core_ref: (i, 2 * k + core_ref[0]))
    else:
        mine = pl.BlockSpec((None, tb, w), lambda k, i, core_ref: (2 * k + core_ref[0], i, 0))
    slot = pl.BlockSpec((None, tb, w), lambda k, i, core_ref: (k, i, 0))

    def body(core_ref, a_ref, b_ref, o_ref):
        o_ref[...] = (a_ref[...].astype(F32) + b_ref[...].astype(F32)).astype(o_ref.dtype)

    return pl.pallas_call(
        body, name=name,
        grid_spec=pltpu.PrefetchScalarGridSpec(num_scalar_prefetch=1, grid=(4, nb), in_specs=[mine, slot],
                                               out_specs=slot),
        out_shape=jax.ShapeDtypeStruct(theirs.shape, theirs.dtype), compiler_params=_params(2))(core, partial, theirs)


def _matmul(name, mode, grid, a, a_spec, bs, b_specs, extras, extra_specs, out_shapes, out_specs, acc_shape,
            epilogue, a_fn=None, comm=None, a_norm=None):
    nk, nb, ne, no = grid[2], len(bs), len(extras), len(out_shapes)
    if a_norm is not None:
        assert nk == 1 and mode != "tn"
        tm_a, k_a = a_spec.block_shape
        vec = pl.BlockSpec((1, k_a), lambda i, j, kk: (0, 0))
        extras, extra_specs = list(extras) + list(a_norm), list(extra_specs) + [vec] * 3
        out_shapes = list(out_shapes) + [jax.ShapeDtypeStruct(a.shape, BF16)]
        out_specs = list(out_specs) + [pl.BlockSpec((tm_a, k_a), lambda i, j, kk: (i, 0))]

    def body(*refs):
        a_ref, b_refs, e_refs = refs[0], refs[1:1 + nb], refs[1 + nb:1 + nb + ne]
        o_refs, acc_refs = refs[1 + nb + ne:1 + nb + ne + no], refs[1 + nb + ne + no:]
        if a_norm is None:
            a_val = a_ref[...]
            if a_fn is not None:
                a_val = a_fn(a_val)
            a_val = a_val.astype(BF16)
        else:
            g_ref, sh_ref, sc_ref = refs[1 + nb + ne:4 + nb + ne]
            o_refs = refs[4 + nb + ne:4 + nb + ne + no]
            h_ref, h_scr = refs[4 + nb + ne + no], refs[5 + nb + ne + no]

            @pl.when(pl.program_id(1) == 0)
            def _():
                xv = a_ref[...]
                rstd = lax.rsqrt(jnp.mean(xv * xv, axis=-1, keepdims=True) + EPS)
                hv = ((xv * rstd) * g_ref[...] * (1.0 + sc_ref[...]) + sh_ref[...]).astype(BF16)
                h_scr[...] = hv
                h_ref[...] = hv

            a_val = h_scr[...]

        def product(b_ref):
            return lax.dot_general(a_val, b_ref[...].astype(BF16), _DN[mode], preferred_element_type=F32)

        def finish(accs):
            outs = epilogue(accs, [e[...] for e in e_refs])
            for o_ref, o in zip(o_refs, outs):
                o_ref[...] = o.astype(o_ref.dtype)

        if nk == 1:
            finish([product(b_ref) for b_ref in b_refs])
        else:
            kk = pl.program_id(2)

            @pl.when(kk == 0)
            def _():
                for acc in acc_refs:
                    acc[...] = jnp.zeros_like(acc)

            for acc, b_ref in zip(acc_refs, b_refs):
                acc[...] += product(b_ref)

            @pl.when(kk == nk - 1)
            def _():
                finish([acc[...] for acc in acc_refs])

    scratch = [] if nk == 1 else [pltpu.VMEM(acc_shape, F32) for _ in range(nb)]
    if a_norm is not None:
        scratch = [pltpu.VMEM(a_spec.block_shape, BF16)]
    body, in_specs, out_specs, out_shape, scratch, operands = _carry(
        comm, body, grid, [a_spec] + list(b_specs) + list(extra_specs), list(out_specs), list(out_shapes), scratch,
        [a, *bs, *extras])
    outs = pl.pallas_call(
        body, name=name, grid=grid, in_specs=in_specs, out_specs=out_specs, out_shape=out_shape,
        scratch_shapes=scratch, compiler_params=_params(3))(*operands)
    n_mine = len(out_shapes)
    if comm is not None:
        comm.results = outs[n_mine:]
    return outs[:n_mine]


def _mm_tiles(m, n, k, tiles):
    return _tile(m, tiles[0], 16), _tile(n, tiles[1], LANES), _tile(k, tiles[2], LANES)


FFN_TILES = {"in": (1024, 512, 2048), "out": (1024, 512, 5632), "out_bwd": (2048, 512, 2048),
             "in_bwd": (1024, 1024, 2816), "out_dw": (512, 2048, 2048), "in_dw": (1024, 512, 4096)}
FULL_K = (1024, 1024, 2048)
LOOP_K = (1024, 1024, 2048)


def _mm_nn(name, a, b, out_dtype, b_col0=0, n=None, tiles=FULL_K, comm=None, a_norm=None):
    m, k = a.shape
    n = b.shape[1] if n is None else n
    tm, tn, tk = _mm_tiles(m, n, k, tiles)
    off = b_col0 // tn
    assert b_col0 % tn == 0
    outs = _matmul(
        name, "nn", (m // tm, n // tn, k // tk), a, pl.BlockSpec((tm, tk), lambda i, j, kk: (i, kk)),
        [b], [pl.BlockSpec((tk, tn), lambda i, j, kk: (kk, j + off))], [], [],
        [jax.ShapeDtypeStruct((m, n), out_dtype)], [pl.BlockSpec((tm, tn), lambda i, j, kk: (i, j))], (tm, tn),
        lambda accs, ex: accs[:1], comm=comm, a_norm=a_norm)
    return outs[0] if a_norm is None else outs


def _mm_bias_rows(name, a, b, bias, a_fn):
    m, k = a.shape
    n = b.shape[1]
    tn = _tile(n, 768, LANES)
    return _matmul(
        name, "nn", (1, n // tn, 1), a, pl.BlockSpec((m, k), lambda i, j, kk: (0, 0)),
        [b], [pl.BlockSpec((k, tn), lambda i, j, kk: (0, j))],
        [bias], [pl.BlockSpec((1, tn), lambda i, j, kk: (0, j))],
        [jax.ShapeDtypeStruct((m, n), F32)], [pl.BlockSpec((m, tn), lambda i, j, kk: (0, j))], (m, tn),
        lambda accs, ex: [accs[0] + ex[0]], a_fn=a_fn)[0]


def _mm_swiglu(name, h, w_in, f, tiles, comm=None, a_norm=None):
    s, d = h.shape
    tm, tn, tk = _mm_tiles(s, f, d, tiles)
    nf = f // tn

    def epilogue(accs, ex):
        a_pre, b_pre = accs
        return [jnp.stack([a_pre, b_pre]), _silu(a_pre) * b_pre]

    return _matmul(
        name, "nn", (s // tm, nf, d // tk), h, pl.BlockSpec((tm, tk), lambda i, j, kk: (i, kk)),
        [w_in, w_in], [pl.BlockSpec((tk, tn), lambda i, j, kk: (kk, j)),
                       pl.BlockSpec((tk, tn), lambda i, j, kk: (kk, j + nf))], [], [],
        [jax.ShapeDtypeStruct((2, s, f), BF16), jax.ShapeDtypeStruct((s, f), BF16)],
        [pl.BlockSpec((2, tm, tn), lambda i, j, kk: (0, i, j)), pl.BlockSpec((tm, tn), lambda i, j, kk: (i, j))],
        (tm, tn), epilogue, comm=comm, a_norm=a_norm)


def _mm_residual(name, a, w, x_in, gvec, tiles=FULL_K, comm=None):
    s, k = a.shape
    d = w.shape[1]
    tm, tn, tk = _mm_tiles(s, d, k, tiles)

    def epilogue(accs, ex):
        x_blk, g_row = ex
        return [accs[0], x_blk + g_row * accs[0]]

    tile = pl.BlockSpec((tm, tn), lambda i, j, kk: (i, j))
    return _matmul(
        name, "nn", (s // tm, d // tn, k // tk), a, pl.BlockSpec((tm, tk), lambda i, j, kk: (i, kk)),
        [w], [pl.BlockSpec((tk, tn), lambda i, j, kk: (kk, j))],
        [x_in, gvec], [tile, pl.BlockSpec((1, tn), lambda i, j, kk: (0, j))],
        [jax.ShapeDtypeStruct((s, d), BF16), jax.ShapeDtypeStruct((s, d), F32)], [tile, tile], (tm, tn), epilogue,
        comm=comm)


def _mm_nt(name, a, w, out_dtype, w_col0=0, extra_add=None, tiles=FULL_K):
    s, k = a.shape
    n = w.shape[0]
    tm, tn, tk = _mm_tiles(s, n, k, tiles)
    off = w_col0 // tk
    assert w_col0 % tk == 0
    tile = pl.BlockSpec((tm, tn), lambda i, j, kk: (i, j))
    extras, especs = ([], []) if extra_add is None else ([extra_add], [tile])
    return _matmul(
        name, "nt", (s // tm, n // tn, k // tk), a, pl.BlockSpec((tm, tk), lambda i, j, kk: (i, kk)),
        [w], [pl.BlockSpec((tn, tk), lambda i, j, kk: (j, kk + off))], extras, especs,
        [jax.ShapeDtypeStruct((s, n), out_dtype)], [tile], (tm, tn),
        lambda accs, ex: [accs[0] + ex[0]] if ex else accs)[0]


def _mm_nt_parts(name, a3, w, out_dtype, tiles=LOOP_K, comm=None):
    p, s, k = a3.shape
    n = w.shape[0]
    tm, tn, tk = _mm_tiles(s, n, k, tiles)
    nkp = k // tk
    return _matmul(
        name, "nt", (s // tm, n // tn, p * nkp), a3,
        pl.BlockSpec((None, tm, tk), lambda i, j, kk: (kk // nkp, i, kk % nkp)),
        [w], [pl.BlockSpec((tn, tk), lambda i, j, kk: (j, kk))], [], [],
        [jax.ShapeDtypeStruct((s, n), out_dtype)], [pl.BlockSpec((tm, tn), lambda i, j, kk: (i, j))], (tm, tn),
        lambda accs, ex: accs, comm=comm)[0]


def _mm_nt_swiglu_bwd(name, dy, w_out, ab, tiles):
    s, d = dy.shape
    f = w_out.shape[0]
    tm, tn, tk = _mm_tiles(s, f, d, tiles)

    def epilogue(accs, ex):
        du, a_pre, b_pre = accs[0], ex[0][0].astype(F32), ex[0][1].astype(F32)
        sig = _sigmoid(a_pre)
        da = du * b_pre * (sig * (1.0 + a_pre * (1.0 - sig)))
        db = du * (a_pre * sig)
        return [jnp.stack([da, db])]

    pair = pl.BlockSpec((2, tm, tn), lambda i, j, kk: (0, i, j))
    return _matmul(
        name, "nt", (s // tm, f // tn, d // tk), dy, pl.BlockSpec((tm, tk), lambda i, j, kk: (i, kk)),
        [w_out], [pl.BlockSpec((tn, tk), lambda i, j, kk: (j, kk))], [ab], [pair],
        [jax.ShapeDtypeStruct((2, s, f), BF16)], [pair], (tm, tn), epilogue)[0]


def _mm_tn(name, a, b, out_dtype, a_fn=None, tiles=LOOP_K, comm=None):
    s, m = a.shape
    tm, tk = _tile(m, tiles[0], LANES), _tile(s, tiles[2], 16)
    if b.ndim == 3:
        npart = b.shape[2]
        n = b.shape[0] * npart
        tn = _tile(npart, tiles[1], LANES)
        nj = npart // tn
        b_spec = pl.BlockSpec((None, tk, tn), lambda i, j, kk: (j // nj, kk, j % nj))
    else:
        n = b.shape[1]
        tn = _tile(n, tiles[1], LANES)
        b_spec = pl.BlockSpec((tk, tn), lambda i, j, kk: (kk, j))
    return _matmul(
        name, "tn", (m // tm, n // tn, s // tk), a, pl.BlockSpec((tk, tm), lambda i, j, kk: (kk, i)),
        [b], [b_spec], [], [],
        [jax.ShapeDtypeStruct((m, n), out_dtype)], [pl.BlockSpec((tm, tn), lambda i, j, kk: (i, j))], (tm, tn),
        lambda accs, ex: accs, a_fn=a_fn, comm=comm)[0]


def _row_tile(s):
    return _tile(s, 256, 16)


def _fold_rows(v):
    return jnp.sum(v.reshape(v.shape[0] // SUBLANES, SUBLANES, v.shape[1]), axis=0)


def _accumulate(i, acc_ref, val):
    @pl.when(i == 0)
    def _():
        acc_ref[...] = val

    @pl.when(i > 0)
    def _():
        acc_ref[...] += val


def _rms_bwd(xv, dn_g, d):
    rstd = lax.rsqrt(jnp.mean(xv * xv, axis=-1, keepdims=True) + EPS)
    xhat = xv * rstd
    dx = rstd * (dn_g - xhat * (jnp.sum(dn_g * xhat, axis=-1, keepdims=True) * (1.0 / d)))
    return xhat, dx


def _norm_mod_bwd(name, dh, x, g, scale, dx_in, then=None):
    s, d = x.shape
    tb = _row_tile(s)
    nsteps = s // tb
    coef = None if then is None else then[2]

    def body(*refs):
        dh_ref, x_ref, g_ref, sc_ref, dxin_ref = refs[:5]
        if then is None:
            dx_ref, dsh_ref, dsc_ref, dg_ref, a_sh, a_sc, a_g = refs[5:]
        else:
            y_ref, gv_ref, dx_ref, dsh_ref, dsc_ref, dg_ref, dy_ref, dgate_ref, a_sh, a_sc, a_g, a_gate = refs[5:]
        i = pl.program_id(0)
        dhv, gv = dh_ref[...], g_ref[...]
        dn = dhv * (1.0 + sc_ref[...])
        xhat, dx = _rms_bwd(x_ref[...], dn * gv, d)
        dx_total = dxin_ref[...] + dx
        dx_ref[...] = dx_total
        _accumulate(i, a_sh, _fold_rows(dhv))
        _accumulate(i, a_sc, _fold_rows(dhv * (xhat * gv)))
        _accumulate(i, a_g, _fold_rows(dn * xhat))
        if then is not None:
            dy_ref[...] = (dx_total * gv_ref[...]).astype(BF16)
            _accumulate(i, a_gate, _fold_rows(dx_total * y_ref[...].astype(F32)))

        @pl.when(i == nsteps - 1)
        def _():
            dsh_ref[...] = jnp.sum(a_sh[...], axis=0, keepdims=True)
            dsc_ref[...] = jnp.sum(a_sc[...], axis=0, keepdims=True)
            dg_ref[...] = jnp.sum(a_g[...], axis=0, keepdims=True)
            if then is not None:
                dgate_ref[...] = coef * jnp.sum(a_gate[...], axis=0, keepdims=True)

    row = pl.BlockSpec((tb, d), lambda i: (i, 0))
    vec = pl.BlockSpec((1, d), lambda i: (0, 0))
    vshape = jax.ShapeDtypeStruct((1, d), F32)
    in_specs, operands = [row, row, vec, vec, row], [dh, x, g, scale, dx_in]
    out_specs, out_shape = [row, vec, vec, vec], [jax.ShapeDtypeStruct((s, d), F32), vshape, vshape, vshape]
    n_acc = 3
    if then is not None:
        in_specs, operands = in_specs + [row, vec], operands + [then[0], then[1]]
        out_specs, out_shape = out_specs + [row, vec], out_shape + [jax.ShapeDtypeStruct((s, d), BF16), vshape]
        n_acc = 4
    return pl.pallas_call(
        body, name=name, grid=(nsteps,), in_specs=in_specs, out_specs=out_specs, out_shape=out_shape,
        scratch_shapes=[pltpu.VMEM((SUBLANES, d), F32)] * n_acc, compiler_params=_params(1))(*operands)


def _gate_bwd(name, dx, y, gvec, coef):
    s, d = dx.shape
    tb = _row_tile(s)
    nsteps = s // tb

    def body(dx_ref, y_ref, g_ref, dy_ref, dg_ref, acc):
        i = pl.program_id(0)
        dxv = dx_ref[...]
        dy_ref[...] = (dxv * g_ref[...]).astype(BF16)
        _accumulate(i, acc, _fold_rows(dxv * y_ref[...].astype(F32)))

        @pl.when(i == nsteps - 1)
        def _():
            dg_ref[...] = coef * jnp.sum(acc[...], axis=0, keepdims=True)

    row = pl.BlockSpec((tb, d), lambda i: (i, 0))
    vec = pl.BlockSpec((1, d), lambda i: (0, 0))
    return pl.pallas_call(
        body, name=name, grid=(nsteps,), in_specs=[row, row, vec], out_specs=[row, vec],
        out_shape=[jax.ShapeDtypeStruct((s, d), BF16), jax.ShapeDtypeStruct((1, d), F32)],
        scratch_shapes=[pltpu.VMEM((SUBLANES, d), F32)], compiler_params=_params(1))(dx, y, gvec)


def _final_loss(name, x, target, g):
    s, d = x.shape
    tb = _row_tile(s)
    nsteps = s // tb

    def body(x_ref, t_ref, g_ref, dx_ref, loss_ref, dg_ref, a_loss, a_g):
        i = pl.program_id(0)
        xv, gv = x_ref[...], g_ref[...]
        rstd = lax.rsqrt(jnp.mean(xv * xv, axis=-1, keepdims=True) + EPS)
        xhat = xv * rstd
        err = xhat * gv - t_ref[...]
        dyv = err * (1.0 / d)
        dn_g = dyv * gv
        dx_ref[...] = rstd * (dn_g - xhat * (jnp.sum(dn_g * xhat, axis=-1, keepdims=True) * (1.0 / d)))
        _accumulate(i, a_loss, _fold_rows(err * err))
        _accumulate(i, a_g, _fold_rows(dyv * xhat))

        @pl.when(i == nsteps - 1)
        def _():
            total = jnp.sum(jnp.sum(a_loss[...], axis=0, keepdims=True), axis=1, keepdims=True) * (0.5 / d)
            loss_ref[...] = jnp.broadcast_to(total, (1, LANES))
            dg_ref[...] = jnp.sum(a_g[...], axis=0, keepdims=True)

    row = pl.BlockSpec((tb, d), lambda i: (i, 0))
    vec = pl.BlockSpec((1, d), lambda i: (0, 0))
    return pl.pallas_call(
        body, name=name, grid=(nsteps,), in_specs=[row, row, vec],
        out_specs=[row, pl.BlockSpec((1, LANES), lambda i: (0, 0)), vec],
        out_shape=[jax.ShapeDtypeStruct((s, d), F32), jax.ShapeDtypeStruct((1, LANES), F32),
                   jax.ShapeDtypeStruct((1, d), F32)],
        scratch_shapes=[pltpu.VMEM((SUBLANES, d), F32)] * 2, compiler_params=_params(1))(x, target, g)


def _shift_rows(cur, halo, n):
    rolled = pltpu.roll(cur, n, 0)
    row = lax.broadcasted_iota(jnp.int32, cur.shape, 0)
    for r in range(n):
        rolled = jnp.where(row == r, halo[SUBLANES - n + r:SUBLANES - n + r + 1, :], rolled)
    return rolled


def _shift_rows_up(cur, halo, n):
    tb = cur.shape[0]
    rolled = pltpu.roll(cur, tb - n, 0)
    row = lax.broadcasted_iota(jnp.int32, cur.shape, 0)
    for r in range(n):
        rolled = jnp.where(row == tb - n + r, halo[r:r + 1, :], rolled)
    return rolled


def _conv_specs(s, d, tb, tn):
    nd = d // tn
    hb = tb // SUBLANES
    cur = lambda part: pl.BlockSpec((tb, tn), lambda i, j: (i, j + part * nd))
    prev = lambda part: pl.BlockSpec((SUBLANES, tn), lambda i, j: (jnp.maximum(i * hb - 1, 0), j + part * nd))
    nxt = lambda part: pl.BlockSpec((SUBLANES, tn), lambda i, j: (jnp.minimum((i + 1) * hb, s // SUBLANES - 1),
                                                                  j + part * nd))
    return cur, prev, nxt


def _conv_fwd(name, p, conv_w, conv_b):
    s, d = p.shape[0], p.shape[1] // 3
    tb, tn = _row_tile(s), _tile(d, 512, LANES)
    cur, prev, _ = _conv_specs(s, d, tb, tn)

    def body(bg_ref, cg_ref, xv_ref, cgp_ref, xvp_ref, w_ref, b_ref, z_ref):
        i = pl.program_id(0)
        u = cg_ref[...] * xv_ref[...]
        up = jnp.where(i > 0, cgp_ref[...] * xvp_ref[...], 0.0)
        w = w_ref[...]
        uc = w[0:1, :] * _shift_rows(u, up, 2) + w[1:2, :] * _shift_rows(u, up, 1) + w[2:3, :] * u + b_ref[...]
        z_ref[...] = (bg_ref[...] * uc).astype(BF16)

    return pl.pallas_call(
        body, name=name, grid=(s // tb, d // tn),
        in_specs=[cur(0), cur(1), cur(2), prev(1), prev(2), pl.BlockSpec((3, tn), lambda i, j: (0, j)),
                  pl.BlockSpec((1, tn), lambda i, j: (0, j))],
        out_specs=pl.BlockSpec((tb, tn), lambda i, j: (i, j)),
        out_shape=jax.ShapeDtypeStruct((s, d), BF16), compiler_params=_params(2))(p, p, p, p, p, conv_w, conv_b)


def _conv_bwd(name, dz, p, conv_w, conv_b):
    s, d = dz.shape
    tb, tn = _row_tile(s), _tile(d, 512, LANES)
    nsteps = s // tb
    cur, prev, nxt = _conv_specs(s, d, tb, tn)

    def body(dz_ref, bg_ref, cg_ref, xv_ref, cgp_ref, xvp_ref, dzn_ref, bgn_ref, w_ref, b_ref,
             dp_ref, dw_ref, db_ref, a_w0, a_w1, a_w2, a_b):
        j, i = pl.program_id(0), pl.program_id(1)
        cg, xv, bg, dzv = cg_ref[...], xv_ref[...], bg_ref[...], dz_ref[...]
        u = cg * xv
        up = jnp.where(i > 0, cgp_ref[...] * xvp_ref[...], 0.0)
        w = w_ref[...]
        u1, u2 = _shift_rows(u, up, 1), _shift_rows(u, up, 2)
        uc = w[0:1, :] * u2 + w[1:2, :] * u1 + w[2:3, :] * u + b_ref[...]
        duc = dzv * bg
        ducn = jnp.where(i < nsteps - 1, dzn_ref[...] * bgn_ref[...], 0.0)
        du = w[2:3, :] * duc + w[1:2, :] * _shift_rows_up(duc, ducn, 1) + w[0:1, :] * _shift_rows_up(duc, ducn, 2)
        dp_ref[0] = (dzv * uc).astype(BF16)
        dp_ref[1] = (du * xv).astype(BF16)
        dp_ref[2] = (du * cg).astype(BF16)
        _accumulate(i, a_w0, _fold_rows(duc * u2))
        _accumulate(i, a_w1, _fold_rows(duc * u1))
        _accumulate(i, a_w2, _fold_rows(duc * u))
        _accumulate(i, a_b, _fold_rows(duc))

        @pl.when(i == nsteps - 1)
        def _():
            dw_ref[0:1, :] = jnp.sum(a_w0[...], axis=0, keepdims=True)
            dw_ref[1:2, :] = jnp.sum(a_w1[...], axis=0, keepdims=True)
            dw_ref[2:3, :] = jnp.sum(a_w2[...], axis=0, keepdims=True)
            db_ref[...] = jnp.sum(a_b[...], axis=0, keepdims=True)

    swap = lambda spec: pl.BlockSpec(spec.block_shape, lambda j, i, _m=spec.index_map: _m(i, j))
    return pl.pallas_call(
        body, name=name, grid=(d // tn, nsteps),
        in_specs=[pl.BlockSpec((tb, tn), lambda j, i: (i, j)), swap(cur(0)), swap(cur(1)), swap(cur(2)),
                  swap(prev(1)), swap(prev(2)), swap(nxt(0)), swap(nxt(0)),
                  pl.BlockSpec((3, tn), lambda j, i: (0, j)), pl.BlockSpec((1, tn), lambda j, i: (0, j))],
        out_specs=[pl.BlockSpec((3, tb, tn), lambda j, i: (0, i, j)), pl.BlockSpec((3, tn), lambda j, i: (0, j)),
                   pl.BlockSpec((1, tn), lambda j, i: (0, j))],
        out_shape=[jax.ShapeDtypeStruct((3, s, d), BF16), jax.ShapeDtypeStruct((3, d), F32),
                   jax.ShapeDtypeStruct((1, d), F32)],
        scratch_shapes=[pltpu.VMEM((SUBLANES, tn), F32)] * 4, compiler_params=_params(2),
    )(dz, p, p, p, p, p, dz, p, conv_w, conv_b)


def _block_cumsum(v):
    tb = v.shape[0]
    row = lax.broadcasted_iota(jnp.int32, v.shape, 0)
    sh = 1
    while sh < tb:
        v = v + jnp.where(row >= sh, pltpu.roll(v, sh, 0), 0.0)
        sh *= 2
    return v


def _fgate_fwd(name, zf, b_f):
    s = zf.shape[0]
    tb = _tile(s, 512, SUBLANES)

    def body(z_ref, b_ref, f_ref, ft_ref, carry):
        i = pl.program_id(0)
        z = z_ref[...] + b_ref[...]
        ls = jnp.minimum(z, 0.0) - jnp.log(1.0 + jnp.exp(-jnp.abs(z)))
        run = _block_cumsum(ls)

        @pl.when(i == 0)
        def _():
            carry[...] = jnp.zeros_like(carry)

        out = run + carry[...]
        f_ref[...] = out
        ft_ref[...] = jnp.transpose(out)
        carry[...] = out[tb - 1:tb, :]

    blk = pl.BlockSpec((tb, LANES), lambda i: (i, 0))
    return pl.pallas_call(
        body, name=name, grid=(s // tb,), in_specs=[blk, pl.BlockSpec((1, LANES), lambda i: (0, 0))],
        out_specs=[blk, pl.BlockSpec((LANES, tb), lambda i: (0, i))],
        out_shape=[jax.ShapeDtypeStruct((s, LANES), F32), jax.ShapeDtypeStruct((LANES, s), F32)],
        scratch_shapes=[pltpu.VMEM((1, LANES), F32)], compiler_params=_params(1))(zf, b_f)


def _fgate_bwd(name, dfcum, zf, b_f):
    s = zf.shape[0]
    tb = _tile(s, 512, SUBLANES)
    nsteps = s // tb

    def body(df_ref, z_ref, b_ref, dz_ref, db_ref, carry, acc):
        i = pl.program_id(0)
        df = df_ref[...]
        incl = _block_cumsum(df)
        total = incl[tb - 1:tb, :]

        @pl.when(i == 0)
        def _():
            carry[...] = jnp.zeros_like(carry)

        suffix = total - incl + df + carry[...]
        carry[...] += total
        dzv = suffix * jax.nn.sigmoid(-(z_ref[...] + b_ref[...]))
        dz_ref[...] = dzv
        _accumulate(i, acc, _fold_rows(dzv))

        @pl.when(i == nsteps - 1)
        def _():
            db_ref[...] = jnp.sum(acc[...], axis=0, keepdims=True)

    blk = pl.BlockSpec((tb, LANES), lambda i: (nsteps - 1 - i, 0))
    vec = pl.BlockSpec((1, LANES), lambda i: (0, 0))
    return pl.pallas_call(
        body, name=name, grid=(nsteps,), in_specs=[blk, blk, vec], out_specs=[blk, vec],
        out_shape=[jax.ShapeDtypeStruct((s, LANES), F32), jax.ShapeDtypeStruct((1, LANES), F32)],
        scratch_shapes=[pltpu.VMEM((1, LANES), F32), pltpu.VMEM((SUBLANES, LANES), F32)],
        compiler_params=_params(1))(dfcum, zf, b_f)


def _lane_column(block, lane):
    sel = lax.broadcasted_iota(jnp.int32, block.shape, 1) == lane
    return jnp.sum(jnp.where(sel, block, 0.0), axis=1, keepdims=True)


def _col_to_row(col):
    return jnp.transpose(jnp.broadcast_to(col, (col.shape[0], LANES)))[0:1, :]


def _attn_tiles(s):
    t = _tile(s, 512, LANES)
    return t, t


def _attn_fwd(name, q, kv, fcum_t, n_heads):
    s, d = q.shape
    dh = d // n_heads
    tq, tk = _attn_tiles(s)
    scale2 = LOG2E / math.sqrt(dh)

    def body(q_ref, k_ref, v_ref, ft_ref, o_ref, l_ref):
        qi = pl.program_id(1)
        qv = q_ref[...]

        def step(kj, carry, diagonal):
            m, l, acc = carry
            start = pl.multiple_of(kj * tk, tk)
            kb = k_ref[pl.ds(start, tk), :]
            vb = v_ref[pl.ds(start, tk), :]
            sc = lax.dot_general(qv, kb, _DN["nt"], preferred_element_type=F32) * scale2
            sc = sc - ft_ref[:, pl.ds(start, tk)] * LOG2E
            if diagonal:
                keep = lax.broadcasted_iota(jnp.int32, sc.shape, 0) >= lax.broadcasted_iota(jnp.int32, sc.shape, 1)
                sc = jnp.where(keep, sc, -jnp.inf)
            m_new = jnp.maximum(m, jnp.max(sc, axis=1, keepdims=True))
            alpha = jnp.exp2(m - m_new)
            pr = jnp.exp2(sc - m_new)
            l = alpha * l + jnp.sum(pr, axis=1, keepdims=True)
            acc = alpha * acc + lax.dot_general(pr.astype(BF16), vb, _DN["nn"], preferred_element_type=F32)
            return m_new, l, acc

        init = (jnp.full((tq, 1), -jnp.inf, F32), jnp.zeros((tq, 1), F32), jnp.zeros((tq, dh), F32))
        carry = lax.fori_loop(0, qi, lambda kj, cr: step(kj, cr, False), init)
        m, l, acc = step(qi, carry, True)
        o_ref[...] = (acc / l).astype(BF16)
        l_ref[...] = _col_to_row(m + jnp.log2(l))

    nh = n_heads
    return pl.pallas_call(
        body, name=name, grid=(nh, s // tq),
        in_specs=[pl.BlockSpec((tq, dh), lambda h, i: (i, h)),
                  pl.BlockSpec((s, dh), lambda h, i: (0, h)),
                  pl.BlockSpec((s, dh), lambda h, i: (0, nh + h)),
                  pl.BlockSpec((None, 1, s), lambda h, i: (h, 0, 0))],
        out_specs=[pl.BlockSpec((tq, dh), lambda h, i: (i, h)),
                   pl.BlockSpec((None, 1, tq), lambda h, i: (h, 0, i))],
        out_shape=[jax.ShapeDtypeStruct((s, d), BF16), jax.ShapeDtypeStruct((nh, 1, s), F32)],
        compiler_params=_params(2))(q, kv, kv, fcum_t)


def _attn_delta(name, d_o, o, n_heads):
    s, d = d_o.shape
    dh = d // n_heads
    tb = _row_tile(s)

    def body(do_ref, o_ref, dlt_ref, dob_ref):
        dov = do_ref[...]
        prod = dov * o_ref[...].astype(F32)
        lane = lax.broadcasted_iota(jnp.int32, (tb, LANES), 1)
        out = jnp.zeros((tb, LANES), F32)
        for h in range(n_heads):
            out = jnp.where(lane == h, jnp.sum(prod[:, h * dh:(h + 1) * dh], axis=1, keepdims=True), out)
        dlt_ref[...] = jnp.transpose(out)
        dob_ref[...] = dov.astype(BF16)

    row = pl.BlockSpec((tb, d), lambda i: (i, 0))
    return pl.pallas_call(
        body, name=name, grid=(s // tb,), in_specs=[row, row],
        out_specs=[pl.BlockSpec((LANES, tb), lambda i: (0, i)), row],
        out_shape=[jax.ShapeDtypeStruct((LANES, s), F32), jax.ShapeDtypeStruct((s, d), BF16)],
        compiler_params=_params(1))(d_o, o)


def _attn_bwd(name, q, kv, d_o, fcum, lse_t, delta_t, n_heads):
    s, d = q.shape
    dh = d // n_heads
    tq, tk = _attn_tiles(s)
    nq, nk = s // tq, s // tk
    scale = 1.0 / math.sqrt(dh)
    scale2 = LOG2E * scale

    def body(q_ref, k_ref, v_ref, do_ref, f_ref, lt_ref, dlt_ref, dq_ref, dkv_ref, dfq_ref, dfk_ref, dq_acc, dfq_acc):
        h, kj = pl.program_id(0), pl.program_id(1)
        kb, vb = k_ref[...], v_ref[...]
        fk2 = _lane_column(f_ref[...], h) * LOG2E

        @pl.when(kj == 0)
        def _():
            dq_acc[...] = jnp.zeros_like(dq_acc)
            dfq_acc[...] = jnp.zeros_like(dfq_acc)

        def step(qi, carry, diagonal):
            dk, dv, dfk = carry
            start = pl.multiple_of(qi * tq, tq)
            qb = q_ref[pl.ds(start, tq), :]
            dob = do_ref[pl.ds(start, tq), :]
            sc = lax.dot_general(kb, qb, _DN["nt"], preferred_element_type=F32) * scale2 - fk2
            pr = jnp.exp2(sc - lt_ref[:, pl.ds(start, tq)])
            if diagonal:
                keep = lax.broadcasted_iota(jnp.int32, sc.shape, 1) >= lax.broadcasted_iota(jnp.int32, sc.shape, 0)
                pr = jnp.where(keep, pr, 0.0)
            dv = dv + lax.dot_general(pr.astype(BF16), dob, _DN["nn"], preferred_element_type=F32)
            dp = lax.dot_general(vb, dob, _DN["nt"], preferred_element_type=F32)
            ds = pr * (dp - dlt_ref[:, pl.ds(start, tq)])
            ds_b = ds.astype(BF16)
            dk = dk + lax.dot_general(ds_b, qb, _DN["nn"], preferred_element_type=F32)
            dfk = dfk - jnp.sum(ds, axis=1, keepdims=True)
            dq_acc[pl.ds(start, tq), :] += lax.dot_general(ds_b, kb, _DN["tn"], preferred_element_type=F32)
            dfq_acc[:, pl.ds(start, tq)] += jnp.sum(ds, axis=0, keepdims=True)
            return dk, dv, dfk

        init = (jnp.zeros((tk, dh), F32), jnp.zeros((tk, dh), F32), jnp.zeros((tk, 1), F32))
        carry = step(kj, init, True)
        dk, dv, dfk = lax.fori_loop(kj + 1, nq, lambda qi, cr: step(qi, cr, False), carry)
        dkv_ref[0] = (dk * scale).astype(BF16)
        dkv_ref[1] = dv.astype(BF16)
        dfk_ref[...] = _col_to_row(dfk)

        @pl.when(kj == nk - 1)
        def _():
            dq_ref[...] = (dq_acc[...] * scale).astype(BF16)
            dfq_ref[...] = dfq_acc[...]

    nh = n_heads
    full = pl.BlockSpec((s, dh), lambda h, j: (0, h))
    row_t = pl.BlockSpec((None, 1, s), lambda h, j: (h, 0, 0))
    return pl.pallas_call(
        body, name=name, grid=(nh, nk),
        in_specs=[full, pl.BlockSpec((tk, dh), lambda h, j: (j, h)), pl.BlockSpec((tk, dh), lambda h, j: (j, nh + h)),
                  full, pl.BlockSpec((tk, LANES), lambda h, j: (j, 0)), row_t, row_t],
        out_specs=[full, pl.BlockSpec((2, tk, dh), lambda h, j: (0, j, h)), row_t,
                   pl.BlockSpec((None, 1, tk), lambda h, j: (h, 0, j))],
        out_shape=[jax.ShapeDtypeStruct((s, d), BF16), jax.ShapeDtypeStruct((2, s, d), BF16),
                   jax.ShapeDtypeStruct((nh, 1, s), F32), jax.ShapeDtypeStruct((nh, 1, s), F32)],
        scratch_shapes=[pltpu.VMEM((s, dh), F32), pltpu.VMEM((1, s), F32)],
        compiler_params=_params(2))(q, kv, kv, d_o, fcum, lse_t, delta_t)


def _sum_slots(name, g):
    n, r, lanes = g.shape

    def body(g_ref, o_ref):
        total = g_ref[0]
        for i in range(1, n):
            total = total + g_ref[i]
        o_ref[...] = total

    return pl.pallas_call(
        body, name=name, out_shape=jax.ShapeDtypeStruct((r, lanes), F32),
        in_specs=[pl.BlockSpec(memory_space=pltpu.VMEM)], out_specs=pl.BlockSpec(memory_space=pltpu.VMEM),
        compiler_params=pltpu.CompilerParams(vmem_limit_bytes=VMEM_LIMIT))(g)


def _adamw(name, parts, w, m, v, row0=0, previous=None):
    npart, r, c = parts.shape
    tb = _tile(math.gcd(r, row0) if row0 else r, max(SUBLANES, (2**18 // c) // SUBLANES * SUBLANES), SUBLANES)
    off = row0 // tb
    c1 = 1.0 - ADAM_B1 ** ADAM_STEP
    c2 = 1.0 - ADAM_B2 ** ADAM_STEP

    def body(p_ref, w_ref, m_ref, v_ref, *rest):
        g_out, d_out, m_out, v_out = rest[-4:]
        g = p_ref[0].astype(F32)
        for i in range(1, npart):
            g = g + p_ref[i].astype(F32)
        m_new = ADAM_B1 * m_ref[...] + (1.0 - ADAM_B1) * g
        v_new = ADAM_B2 * v_ref[...] + (1.0 - ADAM_B2) * (g * g)
        m_hat = m_new / c1
        v_hat = v_new / c2
        g_out[...] = g
        d_out[...] = -ADAM_LR * (m_hat / (jnp.sqrt(v_hat) + ADAM_EPS) + ADAM_WD * w_ref[...])
        m_out[...] = m_new
        v_out[...] = v_new

    blk = pl.BlockSpec((tb, c), lambda i: (i + off, 0))
    shape = jax.ShapeDtypeStruct(w.shape, F32)
    in_specs = [pl.BlockSpec((npart, tb, c), lambda i: (0, i, 0)), blk, blk, blk]
    operands = [parts, w, m, v]
    aliases = {}
    if previous is not None:
        in_specs += [pl.BlockSpec(memory_space=pl.ANY)] * 4
        operands += list(previous)
        aliases = {4 + k: k for k in range(4)}
    return pl.pallas_call(
        body, name=name, grid=(r // tb,), in_specs=in_specs, out_specs=[blk] * 4, out_shape=[shape] * 4,
        input_output_aliases=aliases, compiler_params=_params(1))(*operands)


def _pack(vectors):
    flat = jnp.concatenate([v.reshape(-1).astype(F32) for v in vectors])
    pad = (-flat.shape[0]) % (SUBLANES * LANES)
    return jnp.pad(flat, (0, pad)).reshape(-1, LANES)


def _unpack(flat, shapes):
    out, pos = [], 0
    for shp in shapes:
        size = math.prod(shp)
        out.append(flat[..., pos:pos + size].reshape(flat.shape[:-1] + tuple(shp)))
        pos += size
    return out


def _unshard_last(g, lead):
    nd = len(lead)
    return jnp.moveaxis(g, 0, nd).reshape(tuple(lead) + (-1,))


def _my_slice(full, me, width, axis):
    return lax.dynamic_slice_in_dim(full, me * width, width, axis)


def kernel(x, c, norm_g, w_ada, b_ada, w_ffn_in, w_ffn_out, w_conv_in, conv_w, conv_b, w_conv_out, kv_norm_g, w_ada_kv, b_ada_kv, w_kvf, b_fgate, w_q, w_o, final_g, loss_target, m_norm_g, m_w_ada, m_b_ada, m_w_ffn_in, m_w_ffn_out, m_w_conv_in, m_conv_w, m_conv_b, m_w_conv_out, m_kv_norm_g, m_w_ada_kv, m_b_ada_kv, m_w_kvf, m_b_fgate, m_w_q, m_w_o, m_final_g, v_norm_g, v_w_ada, v_b_ada, v_w_ffn_in, v_w_ffn_out, v_w_conv_in, v_conv_w, v_conv_b, v_w_conv_out, v_kv_norm_g, v_w_ada_kv, v_b_ada_kv, v_w_kvf, v_b_fgate, v_w_q, v_w_o, v_final_g):
    s, d = x.shape[1], x.shape[2]
    n_heads = b_fgate.shape[0]
    n_layers = w_ada.shape[0]
    dsh = d // N_DEV
    f = w_ffn_out.shape[2] * N_DEV
    ada_w = w_ada.shape[2]
    kv_w = w_ada_kv.shape[1]
    kvf_w = w_kvf.shape[1]
    assert n_layers == 2 and w_conv_in.shape[0] == 1 and w_q.shape[0] == 1
    assert (d // n_heads) % LANES == 0 and n_heads <= LANES and N_DEV * kvf_w == 2 * d + n_heads
    me = 4 * lax.axis_index("x") + 2 * lax.axis_index("y") + lax.axis_index("c")
    x0, target = x[0], loss_target[0]

    small = _allgather_small("gather_small_params", _pack([c, norm_g, conv_w, conv_b]))
    c_all, ng_sh, cw_sh, cb_sh = _unpack(small.reshape(N_DEV, -1), [(d,), (n_layers, 3, dsh), (3, dsh), (dsh,)])
    norm_g_full = _unshard_last(ng_sh, (n_layers, 3))
    conv_w_full = _unshard_last(cw_sh, (3,))
    conv_b_full = _unshard_last(cb_sh, ()).reshape(1, d)
    c_rows = jnp.pad(c_all, ((0, 16 - N_DEV), (0, 0)))

    ada_cols = [
        _mm_bias_rows(f"ada_rows_{l}", c_rows, w_ada[l], _my_slice(b_ada[l], me, ada_w, 0).reshape(1, ada_w), _silu)
        for l in range(n_layers)]
    ada_cols.append(_mm_bias_rows("ada_rows_kv", c_rows, w_ada_kv, _my_slice(b_ada_kv, me, kv_w, 0).reshape(1, kv_w),
                                  _silu))
    ada_part = jnp.concatenate([a[:N_DEV] for a in ada_cols], axis=1)
    ada_all = _allgather_small("gather_ada_rows", _pack([ada_part]))
    ada_all = ada_all.reshape(N_DEV, -1)[:, :ada_part.size].reshape(N_DEV, N_DEV, -1)
    ada_mine = lax.dynamic_index_in_dim(ada_all, me, axis=1, keepdims=False)
    ada = [ada_mine[:, l * ada_w:(l + 1) * ada_w].reshape(3, 3, 1, d) for l in range(n_layers)]
    ada_kv = ada_mine[:, n_layers * ada_w:].reshape(2, 1, d)

    def as_item(w, kind):
        return (w.astype(BF16), kind)

    gather_first = _GatherComm([as_item(w_ffn_in[0, 0], "cols")])
    gather_l0b = _GatherComm([as_item(w_ffn_out[0, 0], "rows"), as_item(w_ffn_in[0, 1], "cols")])
    gather_conv = _GatherComm([as_item(w_conv_in[0], "cols"), as_item(w_conv_out[0], "rows")])
    gather_l0b_out = _GatherComm([as_item(w_ffn_out[0, 1], "rows")])
    gather_l1a = _GatherComm([as_item(w_ffn_out[1, 0], "rows"), as_item(w_ffn_in[1, 0], "cols")])
    gather_attn = _GatherComm([as_item(w_q[0], "rows"), as_item(w_kvf, "lead"), as_item(w_o[0], "rows")])
    gather_l1b = _GatherComm([as_item(w_ffn_out[1, 1], "rows"), as_item(w_ffn_in[1, 1], "cols")])
    (wf_in00,) = _run_comm("gather_ffn_first", gather_first)
    b_f = jnp.pad(b_fgate, (0, LANES - n_heads)).reshape(1, LANES)

    def ffn_fwd(tag, xs, l, sub, w_in, w_out, comm_in, comm_out):
        shift, scale, gate = ada[l][sub]
        tiles = FFN_TILES
        ab, u, h = _mm_swiglu(f"ffn_in_{tag}", xs, w_in, f, tiles["in"], comm=comm_in,
                              a_norm=(norm_g_full[l, sub].reshape(1, d), shift, scale))
        y, x_next = _mm_residual(f"ffn_out_{tag}", u, w_out, xs, 0.5 * gate, tiles["out"], comm=comm_out)
        return x_next, (xs, h, ab, u, y, w_in, w_out)

    ab, u, h = _mm_swiglu("ffn_in_l0a", x0, wf_in00, f, FFN_TILES["in"], comm=gather_l0b,
                          a_norm=(norm_g_full[0, 0].reshape(1, d), ada[0][0][0], ada[0][0][1]))
    wf_out00, wf_in01 = gather_l0b.results
    y, x1 = _mm_residual("ffn_out_l0a", u, wf_out00, x0, 0.5 * ada[0][0][2], FFN_TILES["out"], comm=gather_conv)
    save_f0 = (x0, h, ab, u, y, wf_in00, wf_out00)
    wc_in, wc_out = gather_conv.results
    shift, scale, gate = ada[0][1]
    p_c, h_c = _mm_nn("conv_in", x1, wc_in, F32, comm=gather_l0b_out,
                      a_norm=(norm_g_full[0, 1].reshape(1, d), shift, scale))
    (wf_out01,) = gather_l0b_out.results
    z_c = _conv_fwd("conv_mix", p_c, conv_w_full, conv_b_full)
    y_c, x2 = _mm_residual("conv_out", z_c, wc_out, x1, gate)
    x3, save_f1 = ffn_fwd("l0b", x2, 0, 2, wf_in01, wf_out01, gather_l1a, gather_attn)
    wf_out10, wf_in10 = gather_l1a.results
    wq, g_kvf, wo = gather_attn.results
    wkvf = _unshard_last(g_kvf, (d,))
    wkv = wkvf[:, :2 * d]
    wf_gate = jnp.pad(wkvf[:, 2 * d:], ((0, 0), (0, LANES - n_heads)))
    x4, save_f2 = ffn_fwd("l1a", x3, 1, 0, wf_in10, wf_out10, gather_l1b, None)
    wf_out11, wf_in11 = gather_l1b.results
    shift, scale, gate_a = ada[1][1]
    q, h_q = _mm_nn("attn_q", x4, wq, BF16, a_norm=(norm_g_full[1, 1].reshape(1, d), shift, scale))
    kv, h_kv = _mm_nn("attn_kv", x4, wkv, BF16, a_norm=(kv_norm_g.reshape(1, d), ada_kv[0], ada_kv[1]))
    zf = _mm_nn("attn_fgate", h_kv, wf_gate, F32)
    fcum, fcum_lanes = _fgate_fwd("fgate_cumsum", zf, b_f)
    fcum_t = fcum_lanes[:n_heads].reshape(n_heads, 1, s)
    o, lse_t = _attn_fwd("attn_fwd", q, kv, fcum_t, n_heads)
    y_a, x5 = _mm_residual("attn_out", o, wo, x4, gate_a)
    x6, save_f3 = ffn_fwd("l1b", x5, 1, 2, wf_in11, wf_out11, None, None)
    dx, loss_part, d_final_g = _final_loss("final_loss", x6, target, final_g.reshape(1, d))

    d_ada = [[[None] * 3 for _ in range(3)] for _ in range(n_layers)]
    d_norm_g = [[None] * 3 for _ in range(n_layers)]
    core = lax.axis_index("c").astype(jnp.int32).reshape(1)
    in_shard, out_shard, row_shard = (d, 2 * f // N_DEV), (f // N_DEV, d), (dsh, d)

    def chip_exchange(tag, items, from_sibling):
        return _ChipComm([_pair_add(f"add_sibling_{tag}_{n}", p, kind, shape, t, core)
                          for n, ((p, kind, shape), t) in enumerate(zip(items, from_sibling))])

    def reduce_within_chip(tag, items):
        return chip_exchange(tag, items, _run_comm(f"reduce_sibling_{tag}", _SiblingComm(items)))

    def ffn_bwd(tag, dxs, dy, saved, l, sub, comm_dw, then):
        xs, h, ab, u, y, w_in, w_out = saved
        tiles = FFN_TILES
        dab = _mm_nt_swiglu_bwd(f"ffn_out_bwd_{tag}", dy, w_out, ab, tiles["out_bwd"])
        dw_in = _mm_tn(f"ffn_in_dw_{tag}", h, dab, BF16, tiles=tiles["in_dw"])
        sibling_in = _SiblingComm([(dw_in, "cols", in_shard)])
        dw_out = _mm_tn(f"ffn_out_dw_{tag}", u, dy, BF16, tiles=tiles["out_dw"], comm=_join(sibling_in, comm_dw))
        sibling_out = _run_comm(f"reduce_sibling_{tag}", _SiblingComm([(dw_out, "rows", out_shard)]))
        scatter = chip_exchange(tag, [(dw_in, "cols", in_shard), (dw_out, "rows", out_shard)],
                                list(sibling_in.results) + list(sibling_out))
        dh = _mm_nt_parts(f"ffn_in_bwd_{tag}", dab, w_in, F32, tiles["in_bwd"], comm=scatter)
        outs = _norm_mod_bwd(f"norm_bwd_{tag}", dh, xs, norm_g_full[l, sub].reshape(1, d), ada[l][sub][1], dxs, then)
        dxs, d_ada[l][sub][0], d_ada[l][sub][1], d_norm_g[l][sub] = outs[:4]
        return dxs, outs[4:], scatter

    def gate_of(saved, l, sub):
        return (saved[4], 0.5 * ada[l][sub][2], 0.5)

    dy, d_ada[1][2][2] = _gate_bwd("gate_bwd_l1b", dx, save_f3[4], 0.5 * ada[1][2][2], 0.5)
    dx, (dy, d_ada[1][1][2]), scatter_l1b = ffn_bwd("l1b", dx, dy, save_f3, 1, 2, None, (y_a, gate_a, 1.0))
    d_o = _mm_nt("attn_out_bwd", dy, wo, F32)
    dwo = _mm_tn("attn_out_dw", o, dy, BF16)
    delta_lanes, d_ob = _attn_delta("attn_delta", d_o, o, n_heads)
    delta_t = delta_lanes[:n_heads].reshape(n_heads, 1, s)
    dq, dkv, dfq, dfk = _attn_bwd("attn_bwd", q, kv, d_ob, fcum, lse_t, delta_t, n_heads)
    dfcum = jnp.pad((dfq + dfk).reshape(n_heads, s).T, ((0, 0), (0, LANES - n_heads)))
    dzf, d_bf = _fgate_bwd("fgate_bwd", dfcum, zf, b_f)
    dh_q = _mm_nt("attn_q_bwd", dq, wq, F32)
    dwq = _mm_tn("attn_q_dw", h_q, dq, BF16)
    dh_kv = _mm_nt_parts("attn_kv_bwd", dkv, wkv, F32)
    dh_kv = _mm_nt("attn_fgate_bwd", dzf, wf_gate, F32, extra_add=dh_kv)
    dwkv = _mm_tn("attn_kv_dw", h_kv, dkv, BF16)
    dwf = _mm_tn("attn_fgate_dw", h_kv, dzf, BF16)
    dwkvf = jnp.concatenate([dwkv, dwf[:, :n_heads]], axis=1)
    p_kvf = jnp.moveaxis(dwkvf.reshape(d, N_DEV, kvf_w), 1, 0)
    scatter_attn = reduce_within_chip(
        "attn", [(dwo, "rows", row_shard), (dwq, "rows", row_shard), (p_kvf, "lead", (d, kvf_w))])
    dx, d_ada[1][1][0], d_ada[1][1][1], d_norm_g[1][1] = _norm_mod_bwd(
        "norm_bwd_q", dh_q, x4, norm_g_full[1, 1].reshape(1, d), ada[1][1][1], dx)
    dx, d_kv_shift, d_kv_scale, d_kv_norm_g, dy, d_ada[1][0][2] = _norm_mod_bwd(
        "norm_bwd_kv", dh_kv, x4, kv_norm_g.reshape(1, d), ada_kv[1], dx, gate_of(save_f2, 1, 0))
    dx, (dy, d_ada[0][2][2]), scatter_l1a = ffn_bwd("l1a", dx, dy, save_f2, 1, 0, scatter_attn, gate_of(save_f1, 0, 2))
    dx, (dy, d_ada[0][1][2]), scatter_l0b = ffn_bwd("l0b", dx, dy, save_f1, 0, 2, None, (y_c, ada[0][1][2], 1.0))
    dz = _mm_nt("conv_out_bwd", dy, wc_out, F32)
    dwc_out = _mm_tn("conv_out_dw", z_c, dy, BF16)
    dp, d_conv_w, d_conv_b = _conv_bwd("conv_mix_bwd", dz, p_c, conv_w_full, conv_b_full)
    dh_c = _mm_nt_parts("conv_in_bwd", dp, wc_in, F32)
    dwc_in = _mm_tn("conv_in_dw", h_c, dp, BF16)
    scatter_conv = reduce_within_chip("conv", [(dwc_out, "rows", row_shard), (dwc_in, "cols", (d, 3 * d // N_DEV))])
    dx, d_ada[0][1][0], d_ada[0][1][1], d_norm_g[0][1], dy, d_ada[0][0][2] = _norm_mod_bwd(
        "norm_bwd_conv", dh_c, x1, norm_g_full[0, 1].reshape(1, d), ada[0][1][1], dx, gate_of(save_f0, 0, 0))
    dx, _, scatter_l0a = ffn_bwd("l0a", dx, dy, save_f0, 0, 0, scatter_conv, None)
    grad_x = dx.reshape(1, s, d)

    d_ada_flat = jnp.concatenate([v.reshape(-1) for l in range(n_layers) for sub in range(3) for v in d_ada[l][sub]])
    d_norm_flat = jnp.concatenate([d_norm_g[l][sub].reshape(-1) for l in range(n_layers) for sub in range(3)])
    small_shapes = [(n_layers * 9 * d,), (2 * d,), (n_layers * 3 * d,), (3, d), (d,), (d,), (LANES,), (d,), (LANES,)]
    small_parts = _pack([d_ada_flat, d_kv_shift, d_kv_scale, d_norm_flat, d_conv_w, d_conv_b, d_kv_norm_g, d_bf,
                         d_final_g, loss_part])
    small_all = _allgather_small("gather_small_grads", small_parts)
    small_sum = _sum_slots("sum_small_grads", small_all).reshape(-1)
    g_b_ada, g_b_ada_kv, g_norm_full, g_conv_w_full, g_conv_b_full, g_kv_norm, g_bf, g_final, loss_v = _unpack(
        small_sum, small_shapes)
    loss = loss_v[0]
    d_ada_rows = small_all.reshape(N_DEV, -1)[:, :n_layers * 9 * d + 2 * d]
    d_ada_rows = jnp.pad(d_ada_rows, ((0, 16 - N_DEV), (0, 0)))

    results = {}

    def update(key, parts, w, m, v):
        shape = w.shape
        c_dim = shape[-1]
        outs = _adamw(f"adamw_{key}", parts.reshape(parts.shape[0], -1, c_dim), w.reshape(-1, c_dim),
                      m.reshape(-1, c_dim), v.reshape(-1, c_dim))
        results[key] = [o.reshape(shape) for o in outs]

    g_w_ada = []
    for l in range(n_layers):
        cols = _my_slice(d_ada_rows[:, l * 9 * d:(l + 1) * 9 * d], me, ada_w, 1)
        g_w_ada.append(_mm_tn(f"ada_dw_{l}", c_rows, cols, F32, a_fn=_silu))
    update("w_ada", jnp.stack(g_w_ada).reshape(1, n_layers * d, ada_w), w_ada, m_w_ada, v_w_ada)
    cols = _my_slice(d_ada_rows[:, n_layers * 9 * d:], me, kv_w, 1)
    update("w_ada_kv", _mm_tn("ada_dw_kv", c_rows, cols, F32, a_fn=_silu).reshape(1, d, kv_w),
           w_ada_kv, m_w_ada_kv, v_w_ada_kv)

    def update_rows(key, parts_list, w, m, v):
        c_dim = w.shape[-1]
        flat = [t.reshape(-1, c_dim) for t in (w, m, v)]
        outs = None
        for n, parts in enumerate(parts_list):
            outs = _adamw(f"adamw_{key}_{n}", parts, *flat, row0=n * parts.shape[1], previous=outs)
        results[key] = [o.reshape(w.shape) for o in outs]

    ffn_scatters = [scatter_l0a, scatter_l0b, scatter_l1a, scatter_l1b]
    update_rows("w_ffn_in", [sc.results[0] for sc in ffn_scatters], w_ffn_in, m_w_ffn_in, v_w_ffn_in)
    update_rows("w_ffn_out", [sc.results[1] for sc in ffn_scatters], w_ffn_out, m_w_ffn_out, v_w_ffn_out)
    r_co, r_ci = scatter_conv.results
    r_o, r_q, r_kvf = scatter_attn.results
    update_rows("w_conv_in", [r_ci], w_conv_in, m_w_conv_in, v_w_conv_in)
    update_rows("w_conv_out", [r_co], w_conv_out, m_w_conv_out, v_w_conv_out)
    update_rows("w_kvf", [r_kvf], w_kvf, m_w_kvf, v_w_kvf)
    update_rows("w_q", [r_q], w_q, m_w_q, v_w_q)
    update_rows("w_o", [r_o], w_o, m_w_o, v_w_o)

    small_keys = ["norm_g", "b_ada", "conv_w", "conv_b", "kv_norm_g", "b_ada_kv", "b_fgate", "final_g"]
    small_w = [norm_g, b_ada, conv_w, conv_b, kv_norm_g, b_ada_kv, b_fgate, final_g]
    small_m = [m_norm_g, m_b_ada, m_conv_w, m_conv_b, m_kv_norm_g, m_b_ada_kv, m_b_fgate, m_final_g]
    small_v = [v_norm_g, v_b_ada, v_conv_w, v_conv_b, v_kv_norm_g, v_b_ada_kv, v_b_fgate, v_final_g]
    small_g = [
        _my_slice(g_norm_full.reshape(n_layers, 3, d), me, dsh, 2), g_b_ada.reshape(b_ada.shape),
        _my_slice(g_conv_w_full, me, dsh, 1).reshape(conv_w.shape), _my_slice(g_conv_b_full, me, dsh, 0).reshape(
            conv_b.shape), g_kv_norm, g_b_ada_kv, g_bf[:n_heads], g_final]
    packed = _adamw("adamw_small", _pack(small_g)[None], _pack(small_w), _pack(small_m), _pack(small_v))
    for vals, idx in zip(packed, range(4)):
        for key, val in zip(small_keys, _unpack(vals.reshape(-1), [w.shape for w in small_w])):
            results.setdefault(key, [None] * 4)[idx] = val

    order = ["norm_g", "w_ada", "b_ada", "w_ffn_in", "w_ffn_out", "w_conv_in", "conv_w", "conv_b", "w_conv_out",
             "kv_norm_g", "w_ada_kv", "b_ada_kv", "w_kvf", "b_fgate", "w_q", "w_o", "final_g"]
    return (loss, grad_x, *[results[k][0] for k in order], *[results[k][1] for k in order],
            *[results[k][2] for k in order], *[results[k][3] for k in order])
```

```python
import functools
import math

import jax
import jax.numpy as jnp
from jax import lax
from jax.experimental import pallas as pl
from jax.experimental.pallas import tpu as pltpu

F32 = jnp.float32
BF16 = jnp.bfloat16
MESH = pl.DeviceIdType.MESH

N_DEV = 8
LANES = 128
SUBLANES = 8
VMEM_LIMIT = 56 * 2**20
EPS = 1e-6
LOG2E = math.log2(math.e)
ADAM_LR, ADAM_B1, ADAM_B2, ADAM_EPS, ADAM_WD, ADAM_STEP = 0.001, 0.9, 0.999, 1e-08, 0.01, 10

_DN = {"nn": (((1,), (0,)), ((), ())), "nt": (((1,), (1,)), ((), ())), "tn": (((0,), (0,)), ((), ()))}


def _tile(n, pref, align):
    t = min(pref, n)
    t -= t % align
    while t >= align:
        if n % t == 0:
            return t
        t -= align
    return n


def _params(n_grid):
    return pltpu.CompilerParams(vmem_limit_bytes=VMEM_LIMIT, dimension_semantics=("arbitrary",) * n_grid)


def _sigmoid(v):
    return 0.5 * jnp.tanh(0.5 * v) + 0.5


def _silu(v):
    return v * _sigmoid(v)


def _position():
    x, y, c = lax.axis_index("x"), lax.axis_index("y"), lax.axis_index("c")
    return x, y, c


def _allgather_small(name, v):
    def body(v_ref, out_ref, send_sems, recv_sems, local_sem):
        x, y, c = _position()
        me, sibling = (x, y, c), (x, y, 1 - c)
        chips = [(1 - x, y), (x, 1 - y), (1 - x, 1 - y)]

        def slot(px, py, pc):
            return out_ref.at[4 * px + 2 * py + pc]

        def copy(k, block, to, src=None):
            return pltpu.make_async_remote_copy(
                src_ref=slot(*block) if src is None else src, dst_ref=slot(*block),
                send_sem=send_sems.at[k], recv_sem=recv_sems.at[k], device_id=to, device_id_type=MESH)

        mine = pltpu.make_async_copy(v_ref, slot(*me), local_sem)
        mine.start()
        first = [copy(0, me, sibling, src=v_ref)]
        first += [copy(1 + j, me, (*chip, c), src=v_ref) for j, chip in enumerate(chips)]
        for cp in first:
            cp.start()
        passed = [copy(4 + j, (*chip, c), sibling) for j, chip in enumerate(chips)]
        for j, chip in enumerate(chips):
            copy(1 + j, (*chip, c), me).wait_recv()
            passed[j].start()
        copy(0, sibling, me).wait_recv()
        for j, chip in enumerate(chips):
            copy(4 + j, (*chip, 1 - c), me).wait_recv()
        for cp in first + passed:
            cp.wait_send()
        mine.wait()

    return pl.pallas_call(
        body, name=name,
        out_shape=jax.ShapeDtypeStruct((N_DEV,) + v.shape, v.dtype),
        in_specs=[pl.BlockSpec(memory_space=pltpu.VMEM)],
        out_specs=pl.BlockSpec(memory_space=pltpu.VMEM),
        scratch_shapes=[pltpu.SemaphoreType.DMA((7,)), pltpu.SemaphoreType.DMA((7,)), pltpu.SemaphoreType.DMA],
        compiler_params=pltpu.CompilerParams(vmem_limit_bytes=VMEM_LIMIT),
    )(v)


def _owner_view(ref, kind, shard_shape, owner):
    r, w = shard_shape
    if kind == "rows":
        return ref.at[pl.ds(pl.multiple_of(owner * r, 16), r)]
    if kind == "cols":
        return ref.at[:, pl.ds(pl.multiple_of(owner * w, LANES), w)]
    return ref.at[owner]


def _full_shape(kind, shard_shape):
    r, w = shard_shape
    return {"rows": (N_DEV * r, w), "cols": (r, N_DEV * w), "lead": (N_DEV, r, w)}[kind]


class _GatherComm:
    mid_fraction = 0.85

    def __init__(self, items):
        self.ins = [shard for shard, _ in items]
        self.kinds = [kind for _, kind in items]
        n = len(items)
        self.out_shapes = [jax.ShapeDtypeStruct(_full_shape(k, a.shape), a.dtype) for a, k in items]
        self.sem_shapes = [pltpu.SemaphoreType.DMA((7 * n,)), pltpu.SemaphoreType.DMA((7 * n,)),
                           pltpu.SemaphoreType.DMA((n,))]
        self.results = None

    def _copy(self, refs, a, k, block, to, from_input=False):
        ins, outs, (send_sems, recv_sems, _) = refs
        px, py, pc = block
        dst = _owner_view(outs[a], self.kinds[a], self.ins[a].shape, 4 * px + 2 * py + pc)
        return pltpu.make_async_remote_copy(
            src_ref=ins[a] if from_input else dst, dst_ref=dst,
            send_sem=send_sems.at[7 * a + k], recv_sem=recv_sems.at[7 * a + k], device_id=to, device_id_type=MESH)

    def _local(self, refs, a):
        ins, outs, (_, _, local_sems) = refs
        x, y, c = _position()
        dst = _owner_view(outs[a], self.kinds[a], self.ins[a].shape, 4 * x + 2 * y + c)
        return pltpu.make_async_copy(ins[a], dst, local_sems.at[a])

    def start(self, *refs):
        x, y, c = _position()
        me, sibling = (x, y, c), (x, y, 1 - c)
        chips = [(1 - x, y), (x, 1 - y), (1 - x, 1 - y)]
        for a in range(len(self.ins)):
            self._local(refs, a).start()
            for j, chip in enumerate(chips):
                self._copy(refs, a, 1 + j, me, (*chip, c), from_input=True).start()
            self._copy(refs, a, 0, me, sibling, from_input=True).start()

    def mid(self, *refs):
        x, y, c = _position()
        chips = [(1 - x, y), (x, 1 - y), (1 - x, 1 - y)]
        for a in range(len(self.ins)):
            for j, chip in enumerate(chips):
                self._copy(refs, a, 1 + j, (*chip, c), (x, y, c)).wait_recv()
                self._copy(refs, a, 4 + j, (*chip, c), (x, y, 1 - c)).start()

    def finish(self, *refs):
        x, y, c = _position()
        me, sibling = (x, y, c), (x, y, 1 - c)
        chips = [(1 - x, y), (x, 1 - y), (1 - x, 1 - y)]
        for a in range(len(self.ins)):
            self._copy(refs, a, 0, sibling, me).wait_recv()
            for j, chip in enumerate(chips):
                self._copy(refs, a, 4 + j, (*chip, 1 - c), me).wait_recv()
        for a in range(len(self.ins)):
            self._copy(refs, a, 0, me, sibling, from_input=True).wait_send()
            for j, chip in enumerate(chips):
                self._copy(refs, a, 1 + j, me, (*chip, c), from_input=True).wait_send()
                self._copy(refs, a, 4 + j, (*chip, c), sibling).wait_send()
            self._local(refs, a).wait()


class _SiblingComm:
    mid_fraction = None

    def __init__(self, items):
        self.ins = [p for p, _, _ in items]
        self.kinds = [kind for _, kind, _ in items]
        self.shapes = [shape for _, _, shape in items]
        n = len(items)
        self.out_shapes = [jax.ShapeDtypeStruct((4,) + tuple(shape), p.dtype) for p, _, shape in items]
        self.sem_shapes = [pltpu.SemaphoreType.DMA((4 * n,)), pltpu.SemaphoreType.DMA((4 * n,))]
        self.results = None

    def _copies(self, refs):
        ins, outs, (send_sems, recv_sems) = refs
        x, y, c = _position()
        return [pltpu.make_async_remote_copy(
            src_ref=_owner_view(ins[a], self.kinds[a], self.shapes[a], 2 * k + 1 - c), dst_ref=outs[a].at[k],
            send_sem=send_sems.at[4 * a + k], recv_sem=recv_sems.at[4 * a + k],
            device_id=(x, y, 1 - c), device_id_type=MESH) for a in range(len(self.ins)) for k in range(4)]

    def start(self, *refs):
        for cp in self._copies(refs):
            cp.start()

    def finish(self, *refs):
        for cp in self._copies(refs):
            cp.wait()


class _ChipComm:
    mid_fraction = None

    def __init__(self, items):
        self.ins = list(items)
        n = len(items)
        self.out_shapes = [jax.ShapeDtypeStruct(p.shape, p.dtype) for p in items]
        self.sem_shapes = [pltpu.SemaphoreType.DMA((3 * n,)), pltpu.SemaphoreType.DMA((3 * n,)),
                           pltpu.SemaphoreType.DMA((n,))]
        self.results = None

    def _copy(self, refs, a, k, landed):
        ins, outs, (send_sems, recv_sems, _) = refs
        x, y, c = _position()
        px = 1 - x if k & 2 else x
        py = 1 - y if k & 1 else y
        to_chip, my_chip = 2 * px + py, 2 * x + y
        return pltpu.make_async_remote_copy(
            src_ref=ins[a].at[to_chip], dst_ref=outs[a].at[to_chip if landed else my_chip],
            send_sem=send_sems.at[3 * a + k - 1], recv_sem=recv_sems.at[3 * a + k - 1],
            device_id=(px, py, c), device_id_type=MESH)

    def _local(self, refs, a):
        ins, outs, (_, _, local_sems) = refs
        x, y, _ = _position()
        return pltpu.make_async_copy(ins[a].at[2 * x + y], outs[a].at[2 * x + y], local_sems.at[a])

    def start(self, *refs):
        for a in range(len(self.ins)):
            self._local(refs, a).start()
            for k in range(1, 4):
                self._copy(refs, a, k, False).start()

    def finish(self, *refs):
        for a in range(len(self.ins)):
            for k in range(1, 4):
                self._copy(refs, a, k, True).wait_recv()
        for a in range(len(self.ins)):
            for k in range(1, 4):
                self._copy(refs, a, k, False).wait_send()
            self._local(refs, a).wait()


class _Joined:
    def __init__(self, comms):
        self.comms = list(comms)
        self.ins = [a for c in self.comms for a in c.ins]
        self.out_shapes = [o for c in self.comms for o in c.out_shapes]
        self.sem_shapes = [o for c in self.comms for o in c.sem_shapes]
        fractions = [c.mid_fraction for c in self.comms if c.mid_fraction is not None]
        self.mid_fraction = max(fractions) if fractions else None

    def _parts(self, refs):
        ins, outs, sems = refs
        pos = [0, 0, 0]
        for c in self.comms:
            n = (len(c.ins), len(c.out_shapes), len(c.sem_shapes))
            yield c, (ins[pos[0]:pos[0] + n[0]], outs[pos[1]:pos[1] + n[1]], sems[pos[2]:pos[2] + n[2]])
            pos = [p + k for p, k in zip(pos, n)]

    def start(self, *refs):
        for c, part in self._parts(refs):
            c.start(*part)

    def mid(self, *refs):
        for c, part in self._parts(refs):
            if c.mid_fraction is not None:
                c.mid(*part)

    def finish(self, *refs):
        for c, part in self._parts(refs):
            c.finish(*part)

    @property
    def results(self):
        return [r for c in self.comms for r in c.results]

    @results.setter
    def results(self, values):
        pos = 0
        for c in self.comms:
            c.results = values[pos:pos + len(c.out_shapes)]
            pos += len(c.out_shapes)


def _join(*comms):
    present = [c for c in comms if c is not None]
    return present[0] if len(present) == 1 else (_Joined(present) if present else None)


def _run_comm(name, comm):
    n_in, n_out = len(comm.ins), len(comm.out_shapes)

    def body(*refs):
        parts = (refs[:n_in], refs[n_in:n_in + n_out], refs[n_in + n_out:])
        comm.start(*parts)
        if comm.mid_fraction is not None:
            comm.mid(*parts)
        comm.finish(*parts)

    any_spec = pl.BlockSpec(memory_space=pl.ANY)
    return pl.pallas_call(
        body, name=name, out_shape=list(comm.out_shapes), in_specs=[any_spec] * n_in, out_specs=[any_spec] * n_out,
        scratch_shapes=list(comm.sem_shapes))(*comm.ins)


def _carry(comm, body, grid, in_specs, out_specs, out_shapes, scratch_shapes, operands):
    if comm is None:
        return body, in_specs, out_specs, out_shapes, scratch_shapes, operands
    n_in, n_out, n_scr = len(in_specs), len(out_specs), len(scratch_shapes)
    c_in, c_out = len(comm.ins), len(comm.out_shapes)
    total = math.prod(grid)
    mid_step = None if comm.mid_fraction is None else min(total - 1, int(total * comm.mid_fraction))

    def wrapped(*refs):
        ins, c_ins = refs[:n_in], refs[n_in:n_in + c_in]
        outs = refs[n_in + c_in:n_in + c_in + n_out]
        c_outs = refs[n_in + c_in + n_out:n_in + c_in + n_out + c_out]
        scr = refs[n_in + c_in + n_out + c_out:n_in + c_in + n_out + c_out + n_scr]
        parts = (c_ins, c_outs, refs[n_in + c_in + n_out + c_out + n_scr:])
        step = pl.program_id(0)
        for axis in range(1, len(grid)):
            step = step * grid[axis] + pl.program_id(axis)

        @pl.when(step == 0)
        def _():
            comm.start(*parts)

        body(*ins, *outs, *scr)

        if mid_step is not None:
            @pl.when(step == mid_step)
            def _():
                comm.mid(*parts)

        @pl.when(step == total - 1)
        def _():
            comm.finish(*parts)

    any_spec = pl.BlockSpec(memory_space=pl.ANY)
    return (wrapped, list(in_specs) + [any_spec] * c_in, list(out_specs) + [any_spec] * c_out,
            list(out_shapes) + list(comm.out_shapes), list(scratch_shapes) + list(comm.sem_shapes),
            list(operands) + list(comm.ins))


def _pair_add(name, partial, kind, shard_shape, theirs, core):
    r, w = shard_shape
    tb = _tile(r, max(16, (2**19 // w) // 16 * 16), 16)
    nb = r // tb
    if kind == "rows":
        mine = pl.BlockSpec((tb, w), lambda k, i, core_ref: ((2 * k + core_ref[0]) * nb + i, 0))
    elif kind == "cols":
        mine = pl.BlockSpec((tb, w), lambda k, i, core_ref: (i, 2 * k + core_ref[0]))
    else:
        mine = pl.BlockSpec((None, tb, w), lambda k, i, core_ref: (2 * k + core_ref[0], i, 0))
    slot = pl.BlockSpec((None, tb, w), lambda k, i, core_ref: (k, i, 0))

    def body(core_ref, a_ref, b_ref, o_ref):
        o_ref[...] = (a_ref[...].astype(F32) + b_ref[...].astype(F32)).astype(o_ref.dtype)

    return pl.pallas_call(
        body, name=name,
        grid_spec=pltpu.PrefetchScalarGridSpec(num_scalar_prefetch=1, grid=(4, nb), in_specs=[mine, slot],
                                               out_specs=slot),
        out_shape=jax.ShapeDtypeStruct(theirs.shape, theirs.dtype), compiler_params=_params(2))(core, partial, theirs)


def _matmul(name, mode, grid, a, a_spec, bs, b_specs, extras, extra_specs, out_shapes, out_specs, acc_shape,
            epilogue, a_fn=None, comm=None, a_norm=None):
    nk, nb, ne, no = grid[2], len(bs), len(extras), len(out_shapes)
    if a_norm is not None:
        assert nk == 1 and mode != "tn"
        tm_a, k_a = a_spec.block_shape
        vec = pl.BlockSpec((1, k_a), lambda i, j, kk: (0, 0))
        extras, extra_specs = list(extras) + list(a_norm), list(extra_specs) + [vec] * 3
        out_shapes = list(out_shapes) + [jax.ShapeDtypeStruct(a.shape, BF16)]
        out_specs = list(out_specs) + [pl.BlockSpec((tm_a, k_a), lambda i, j, kk: (i, 0))]

    def body(*refs):
        a_ref, b_refs, e_refs = refs[0], refs[1:1 + nb], refs[1 + nb:1 + nb + ne]
        o_refs, acc_refs = refs[1 + nb + ne:1 + nb + ne + no], refs[1 + nb + ne + no:]
        if a_norm is None:
            a_val = a_ref[...]
            if a_fn is not None:
                a_val = a_fn(a_val)
            a_val = a_val.astype(BF16)
        else:
            g_ref, sh_ref, sc_ref = refs[1 + nb + ne:4 + nb + ne]
            o_refs = refs[4 + nb + ne:4 + nb + ne + no]
            h_ref, h_scr = refs[4 + nb + ne + no], refs[5 + nb + ne + no]

            @pl.when(pl.program_id(1) == 0)
            def _():
                xv = a_ref[...]
                rstd = lax.rsqrt(jnp.mean(xv * xv, axis=-1, keepdims=True) + EPS)
                hv = ((xv * rstd) * g_ref[...] * (1.0 + sc_ref[...]) + sh_ref[...]).astype(BF16)
                h_scr[...] = hv
                h_ref[...] = hv

            a_val = h_scr[...]

        def product(b_ref):
            return lax.dot_general(a_val, b_ref[...].astype(BF16), _DN[mode], preferred_element_type=F32)

        def finish(accs):
            outs = epilogue(accs, [e[...] for e in e_refs])
            for o_ref, o in zip(o_refs, outs):
                o_ref[...] = o.astype(o_ref.dtype)

        if nk == 1:
            finish([product(b_ref) for b_ref in b_refs])
        else:
            kk = pl.program_id(2)

            @pl.when(kk == 0)
            def _():
                for acc in acc_refs:
                    acc[...] = jnp.zeros_like(acc)

            for acc, b_ref in zip(acc_refs, b_refs):
                acc[...] += product(b_ref)

            @pl.when(kk == nk - 1)
            def _():
                finish([acc[...] for acc in acc_refs])

    scratch = [] if nk == 1 else [pltpu.VMEM(acc_shape, F32) for _ in range(nb)]
    if a_norm is not None:
        scratch = [pltpu.VMEM(a_spec.block_shape, BF16)]
    body, in_specs, out_specs, out_shape, scratch, operands = _carry(
        comm, body, grid, [a_spec] + list(b_specs) + list(extra_specs), list(out_specs), list(out_shapes), scratch,
        [a, *bs, *extras])
    outs = pl.pallas_call(
        body, name=name, grid=grid, in_specs=in_specs, out_specs=out_specs, out_shape=out_shape,
        scratch_shapes=scratch, compiler_params=_params(3))(*operands)
    n_mine = len(out_shapes)
    if comm is not None:
        comm.results = outs[n_mine:]
    return outs[:n_mine]


def _mm_tiles(m, n, k, tiles):
    return _tile(m, tiles[0], 16), _tile(n, tiles[1], LANES), _tile(k, tiles[2], LANES)


FFN_TILES = {"in": (1024, 512, 2048), "out": (1024, 512, 5632), "out_bwd": (2048, 512, 2048),
             "in_bwd": (1024, 1024, 2816), "out_dw": (512, 2048, 2048), "in_dw": (1024, 512, 4096)}
FULL_K = (1024, 1024, 2048)
LOOP_K = (1024, 1024, 2048)


def _mm_nn(name, a, b, out_dtype, b_col0=0, n=None, tiles=FULL_K, comm=None, a_norm=None):
    m, k = a.shape
    n = b.shape[1] if n is None else n
    tm, tn, tk = _mm_tiles(m, n, k, tiles)
    off = b_col0 // tn
    assert b_col0 % tn == 0
    outs = _matmul(
        name, "nn", (m // tm, n // tn, k // tk), a, pl.BlockSpec((tm, tk), lambda i, j, kk: (i, kk)),
        [b], [pl.BlockSpec((tk, tn), lambda i, j, kk: (kk, j + off))], [], [],
        [jax.ShapeDtypeStruct((m, n), out_dtype)], [pl.BlockSpec((tm, tn), lambda i, j, kk: (i, j))], (tm, tn),
        lambda accs, ex: accs[:1], comm=comm, a_norm=a_norm)
    return outs[0] if a_norm is None else outs


def _mm_bias_rows(name, a, b, bias, a_fn):
    m, k = a.shape
    n = b.shape[1]
    tn = _tile(n, 768, LANES)
    return _matmul(
        name, "nn", (1, n // tn, 1), a, pl.BlockSpec((m, k), lambda i, j, kk: (0, 0)),
        [b], [pl.BlockSpec((k, tn), lambda i, j, kk: (0, j))],
        [bias], [pl.BlockSpec((1, tn), lambda i, j, kk: (0, j))],
        [jax.ShapeDtypeStruct((m, n), F32)], [pl.BlockSpec((m, tn), lambda i, j, kk: (0, j))], (m, tn),
        lambda accs, ex: [accs[0] + ex[0]], a_fn=a_fn)[0]


def _mm_swiglu(name, h, w_in, f, tiles, comm=None, a_norm=None):
    s, d = h.shape
    tm, tn, tk = _mm_tiles(s, f, d, tiles)
    nf = f // tn

    def epilogue(accs, ex):
        a_pre, b_pre = accs
        return [jnp.stack([a_pre, b_pre]), _silu(a_pre) * b_pre]

    return _matmul(
        name, "nn", (s // tm, nf, d // tk), h, pl.BlockSpec((tm, tk), lambda i, j, kk: (i, kk)),
        [w_in, w_in], [pl.BlockSpec((tk, tn), lambda i, j, kk: (kk, j)),
                       pl.BlockSpec((tk, tn), lambda i, j, kk: (kk, j + nf))], [], [],
        [jax.ShapeDtypeStruct((2, s, f), BF16), jax.ShapeDtypeStruct((s, f), BF16)],
        [pl.BlockSpec((2, tm, tn), lambda i, j, kk: (0, i, j)), pl.BlockSpec((tm, tn), lambda i, j, kk: (i, j))],
        (tm, tn), epilogue, comm=comm, a_norm=a_norm)


def _mm_residual(name, a, w, x_in, gvec, tiles=FULL_K, comm=None):
    s, k = a.shape
    d = w.shape[1]
    tm, tn, tk = _mm_tiles(s, d, k, tiles)

    def epilogue(accs, ex):
        x_blk, g_row = ex
        return [accs[0], x_blk + g_row * accs[0]]

    tile = pl.BlockSpec((tm, tn), lambda i, j, kk: (i, j))
    return _matmul(
        name, "nn", (s // tm, d // tn, k // tk), a, pl.BlockSpec((tm, tk), lambda i, j, kk: (i, kk)),
        [w], [pl.BlockSpec((tk, tn), lambda i, j, kk: (kk, j))],
        [x_in, gvec], [tile, pl.BlockSpec((1, tn), lambda i, j, kk: (0, j))],
        [jax.ShapeDtypeStruct((s, d), BF16), jax.ShapeDtypeStruct((s, d), F32)], [tile, tile], (tm, tn), epilogue,
        comm=comm)


def _mm_nt(name, a, w, out_dtype, w_col0=0, extra_add=None, tiles=FULL_K):
    s, k = a.shape
    n = w.shape[0]
    tm, tn, tk = _mm_tiles(s, n, k, tiles)
    off = w_col0 // tk
    assert w_col0 % tk == 0
    tile = pl.BlockSpec((tm, tn), lambda i, j, kk: (i, j))
    extras, especs = ([], []) if extra_add is None else ([extra_add], [tile])
    return _matmul(
        name, "nt", (s // tm, n // tn, k // tk), a, pl.BlockSpec((tm, tk), lambda i, j, kk: (i, kk)),
        [w], [pl.BlockSpec((tn, tk), lambda i, j, kk: (j, kk + off))], extras, especs,
        [jax.ShapeDtypeStruct((s, n), out_dtype)], [tile], (tm, tn),
        lambda accs, ex: [accs[0] + ex[0]] if ex else accs)[0]


def _mm_nt_parts(name, a3, w, out_dtype, tiles=LOOP_K, comm=None):
    p, s, k = a3.shape
    n = w.shape[0]
    tm, tn, tk = _mm_tiles(s, n, k, tiles)
    nkp = k // tk
    return _matmul(
        name, "nt", (s // tm, n // tn, p * nkp), a3,
        pl.BlockSpec((None, tm, tk), lambda i, j, kk: (kk // nkp, i, kk % nkp)),
        [w], [pl.BlockSpec((tn, tk), lambda i, j, kk: (j, kk))], [], [],
        [jax.ShapeDtypeStruct((s, n), out_dtype)], [pl.BlockSpec((tm, tn), lambda i, j, kk: (i, j))], (tm, tn),
        lambda accs, ex: accs, comm=comm)[0]


def _mm_nt_swiglu_bwd(name, dy, w_out, ab, tiles):
    s, d = dy.shape
    f = w_out.shape[0]
    tm, tn, tk = _mm_tiles(s, f, d, tiles)

    def epilogue(accs, ex):
        du, a_pre, b_pre = accs[0], ex[0][0].astype(F32), ex[0][1].astype(F32)
        sig = _sigmoid(a_pre)
        da = du * b_pre * (sig * (1.0 + a_pre * (1.0 - sig)))
        db = du * (a_pre * sig)
        return [jnp.stack([da, db])]

    pair = pl.BlockSpec((2, tm, tn), lambda i, j, kk: (0, i, j))
    return _matmul(
        name, "nt", (s // tm, f // tn, d // tk), dy, pl.BlockSpec((tm, tk), lambda i, j, kk: (i, kk)),
        [w_out], [pl.BlockSpec((tn, tk), lambda i, j, kk: (j, kk))], [ab], [pair],
        [jax.ShapeDtypeStruct((2, s, f), BF16)], [pair], (tm, tn), epilogue)[0]


def _mm_tn(name, a, b, out_dtype, a_fn=None, tiles=LOOP_K, comm=None):
    s, m = a.shape
    tm, tk = _tile(m, tiles[0], LANES), _tile(s, tiles[2], 16)
    if b.ndim == 3:
        npart = b.shape[2]
        n = b.shape[0] * npart
        tn = _tile(npart, tiles[1], LANES)
        nj = npart // tn
        b_spec = pl.BlockSpec((None, tk, tn), lambda i, j, kk: (j // nj, kk, j % nj))
    else:
        n = b.shape[1]
        tn = _tile(n, tiles[1], LANES)
        b_spec = pl.BlockSpec((tk, tn), lambda i, j, kk: (kk, j))
    return _matmul(
        name, "tn", (m // tm, n // tn, s // tk), a, pl.BlockSpec((tk, tm), lambda i, j, kk: (kk, i)),
        [b], [b_spec], [], [],
        [jax.ShapeDtypeStruct((m, n), out_dtype)], [pl.BlockSpec((tm, tn), lambda i, j, kk: (i, j))], (tm, tn),
        lambda accs, ex: accs, a_fn=a_fn, comm=comm)[0]


def _row_tile(s):
    return _tile(s, 256, 16)


def _fold_rows(v):
    return jnp.sum(v.reshape(v.shape[0] // SUBLANES, SUBLANES, v.shape[1]), axis=0)


def _accumulate(i, acc_ref, val):
    @pl.when(i == 0)
    def _():
        acc_ref[...] = val

    @pl.when(i > 0)
    def _():
        acc_ref[...] += val


def _rms_bwd(xv, dn_g, d):
    rstd = lax.rsqrt(jnp.mean(xv * xv, axis=-1, keepdims=True) + EPS)
    xhat = xv * rstd
    dx = rstd * (dn_g - xhat * (jnp.sum(dn_g * xhat, axis=-1, keepdims=True) * (1.0 / d)))
    return xhat, dx


def _norm_mod_bwd(name, dh, x, g, scale, dx_in, then=None):
    s, d = x.shape
    tb = _row_tile(s)
    nsteps = s // tb
    coef = None if then is None else then[2]

    def body(*refs):
        dh_ref, x_ref, g_ref, sc_ref, dxin_ref = refs[:5]
        if then is None:
            dx_ref, dsh_ref, dsc_ref, dg_ref, a_sh, a_sc, a_g = refs[5:]
        else:
            y_ref, gv_ref, dx_ref, dsh_ref, dsc_ref, dg_ref, dy_ref, dgate_ref, a_sh, a_sc, a_g, a_gate = refs[5:]
        i = pl.program_id(0)
        dhv, gv = dh_ref[...], g_ref[...]
        dn = dhv * (1.0 + sc_ref[...])
        xhat, dx = _rms_bwd(x_ref[...], dn * gv, d)
        dx_total = dxin_ref[...] + dx
        dx_ref[...] = dx_total
        _accumulate(i, a_sh, _fold_rows(dhv))
        _accumulate(i, a_sc, _fold_rows(dhv * (xhat * gv)))
        _accumulate(i, a_g, _fold_rows(dn * xhat))
        if then is not None:
            dy_ref[...] = (dx_total * gv_ref[...]).astype(BF16)
            _accumulate(i, a_gate, _fold_rows(dx_total * y_ref[...].astype(F32)))

        @pl.when(i == nsteps - 1)
        def _():
            dsh_ref[...] = jnp.sum(a_sh[...], axis=0, keepdims=True)
            dsc_ref[...] = jnp.sum(a_sc[...], axis=0, keepdims=True)
            dg_ref[...] = jnp.sum(a_g[...], axis=0, keepdims=True)
            if then is not None:
                dgate_ref[...] = coef * jnp.sum(a_gate[...], axis=0, keepdims=True)

    row = pl.BlockSpec((tb, d), lambda i: (i, 0))
    vec = pl.BlockSpec((1, d), lambda i: (0, 0))
    vshape = jax.ShapeDtypeStruct((1, d), F32)
    in_specs, operands = [row, row, vec, vec, row], [dh, x, g, scale, dx_in]
    out_specs, out_shape = [row, vec, vec, vec], [jax.ShapeDtypeStruct((s, d), F32), vshape, vshape, vshape]
    n_acc = 3
    if then is not None:
        in_specs, operands = in_specs + [row, vec], operands + [then[0], then[1]]
        out_specs, out_shape = out_specs + [row, vec], out_shape + [jax.ShapeDtypeStruct((s, d), BF16), vshape]
        n_acc = 4
    return pl.pallas_call(
        body, name=name, grid=(nsteps,), in_specs=in_specs, out_specs=out_specs, out_shape=out_shape,
        scratch_shapes=[pltpu.VMEM((SUBLANES, d), F32)] * n_acc, compiler_params=_params(1))(*operands)


def _gate_bwd(name, dx, y, gvec, coef):
    s, d = dx.shape
    tb = _row_tile(s)
    nsteps = s // tb

    def body(dx_ref, y_ref, g_ref, dy_ref, dg_ref, acc):
        i = pl.program_id(0)
        dxv = dx_ref[...]
        dy_ref[...] = (dxv * g_ref[...]).astype(BF16)
        _accumulate(i, acc, _fold_rows(dxv * y_ref[...].astype(F32)))

        @pl.when(i == nsteps - 1)
        def _():
            dg_ref[...] = coef * jnp.sum(acc[...], axis=0, keepdims=True)

    row = pl.BlockSpec((tb, d), lambda i: (i, 0))
    vec = pl.BlockSpec((1, d), lambda i: (0, 0))
    return pl.pallas_call(
        body, name=name, grid=(nsteps,), in_specs=[row, row, vec], out_specs=[row, vec],
        out_shape=[jax.ShapeDtypeStruct((s, d), BF16), jax.ShapeDtypeStruct((1, d), F32)],
        scratch_shapes=[pltpu.VMEM((SUBLANES, d), F32)], compiler_params=_params(1))(dx, y, gvec)


def _final_loss(name, x, target, g):
    s, d = x.shape
    tb = _row_tile(s)
    nsteps = s // tb

    def body(x_ref, t_ref, g_ref, dx_ref, loss_ref, dg_ref, a_loss, a_g):
        i = pl.program_id(0)
        xv, gv = x_ref[...], g_ref[...]
        rstd = lax.rsqrt(jnp.mean(xv * xv, axis=-1, keepdims=True) + EPS)
        xhat = xv * rstd
        err = xhat * gv - t_ref[...]
        dyv = err * (1.0 / d)
        dn_g = dyv * gv
        dx_ref[...] = rstd * (dn_g - xhat * (jnp.sum(dn_g * xhat, axis=-1, keepdims=True) * (1.0 / d)))
        _accumulate(i, a_loss, _fold_rows(err * err))
        _accumulate(i, a_g, _fold_rows(dyv * xhat))

        @pl.when(i == nsteps - 1)
        def _():
            total = jnp.sum(jnp.sum(a_loss[...], axis=0, keepdims=True), axis=1, keepdims=True) * (0.5 / d)
            loss_ref[...] = jnp.broadcast_to(total, (1, LANES))
            dg_ref[...] = jnp.sum(a_g[...], axis=0, keepdims=True)

    row = pl.BlockSpec((tb, d), lambda i: (i, 0))
    vec = pl.BlockSpec((1, d), lambda i: (0, 0))
    return pl.pallas_call(
        body, name=name, grid=(nsteps,), in_specs=[row, row, vec],
        out_specs=[row, pl.BlockSpec((1, LANES), lambda i: (0, 0)), vec],
        out_shape=[jax.ShapeDtypeStruct((s, d), F32), jax.ShapeDtypeStruct((1, LANES), F32),
                   jax.ShapeDtypeStruct((1, d), F32)],
        scratch_shapes=[pltpu.VMEM((SUBLANES, d), F32)] * 2, compiler_params=_params(1))(x, target, g)


def _shift_rows(cur, halo, n):
    rolled = pltpu.roll(cur, n, 0)
    row = lax.broadcasted_iota(jnp.int32, cur.shape, 0)
    for r in range(n):
        rolled = jnp.where(row == r, halo[SUBLANES - n + r:SUBLANES - n + r + 1, :], rolled)
    return rolled


def _shift_rows_up(cur, halo, n):
    tb = cur.shape[0]
    rolled = pltpu.roll(cur, tb - n, 0)
    row = lax.broadcasted_iota(jnp.int32, cur.shape, 0)
    for r in range(n):
        rolled = jnp.where(row == tb - n + r, halo[r:r + 1, :], rolled)
    return rolled


def _conv_specs(s, d, tb, tn):
    nd = d // tn
    hb = tb // SUBLANES
    cur = lambda part: pl.BlockSpec((tb, tn), lambda i, j: (i, j + part * nd))
    prev = lambda part: pl.BlockSpec((SUBLANES, tn), lambda i, j: (jnp.maximum(i * hb - 1, 0), j + part * nd))
    nxt = lambda part: pl.BlockSpec((SUBLANES, tn), lambda i, j: (jnp.minimum((i + 1) * hb, s // SUBLANES - 1),
                                                                  j + part * nd))
    return cur, prev, nxt


def _conv_fwd(name, p, conv_w, conv_b):
    s, d = p.shape[0], p.shape[1] // 3
    tb, tn = _row_tile(s), _tile(d, 512, LANES)
    cur, prev, _ = _conv_specs(s, d, tb, tn)

    def body(bg_ref, cg_ref, xv_ref, cgp_ref, xvp_ref, w_ref, b_ref, z_ref):
        i = pl.program_id(0)
        u = cg_ref[...] * xv_ref[...]
        up = jnp.where(i > 0, cgp_ref[...] * xvp_ref[...], 0.0)
        w = w_ref[...]
        uc = w[0:1, :] * _shift_rows(u, up, 2) + w[1:2, :] * _shift_rows(u, up, 1) + w[2:3, :] * u + b_ref[...]
        z_ref[...] = (bg_ref[...] * uc).astype(BF16)

    return pl.pallas_call(
        body, name=name, grid=(s // tb, d // tn),
        in_specs=[cur(0), cur(1), cur(2), prev(1), prev(2), pl.BlockSpec((3, tn), lambda i, j: (0, j)),
                  pl.BlockSpec((1, tn), lambda i, j: (0, j))],
        out_specs=pl.BlockSpec((tb, tn), lambda i, j: (i, j)),
        out_shape=jax.ShapeDtypeStruct((s, d), BF16), compiler_params=_params(2))(p, p, p, p, p, conv_w, conv_b)


def _conv_bwd(name, dz, p, conv_w, conv_b):
    s, d = dz.shape
    tb, tn = _row_tile(s), _tile(d, 512, LANES)
    nsteps = s // tb
    cur, prev, nxt = _conv_specs(s, d, tb, tn)

    def body(dz_ref, bg_ref, cg_ref, xv_ref, cgp_ref, xvp_ref, dzn_ref, bgn_ref, w_ref, b_ref,
             dp_ref, dw_ref, db_ref, a_w0, a_w1, a_w2, a_b):
        j, i = pl.program_id(0), pl.program_id(1)
        cg, xv, bg, dzv = cg_ref[...], xv_ref[...], bg_ref[...], dz_ref[...]
        u = cg * xv
        up = jnp.where(i > 0, cgp_ref[...] * xvp_ref[...], 0.0)
        w = w_ref[...]
        u1, u2 = _shift_rows(u, up, 1), _shift_rows(u, up, 2)
        uc = w[0:1, :] * u2 + w[1:2, :] * u1 + w[2:3, :] * u + b_ref[...]
        duc = dzv * bg
        ducn = jnp.where(i < nsteps - 1, dzn_ref[...] * bgn_ref[...], 0.0)
        du = w[2:3, :] * duc + w[1:2, :] * _shift_rows_up(duc, ducn, 1) + w[0:1, :] * _shift_rows_up(duc, ducn, 2)
        dp_ref[0] = (dzv * uc).astype(BF16)
        dp_ref[1] = (du * xv).astype(BF16)
        dp_ref[2] = (du * cg).astype(BF16)
        _accumulate(i, a_w0, _fold_rows(duc * u2))
        _accumulate(i, a_w1, _fold_rows(duc * u1))
        _accumulate(i, a_w2, _fold_rows(duc * u))
        _accumulate(i, a_b, _fold_rows(duc))

        @pl.when(i == nsteps - 1)
        def _():
            dw_ref[0:1, :] = jnp.sum(a_w0[...], axis=0, keepdims=True)
            dw_ref[1:2, :] = jnp.sum(a_w1[...], axis=0, keepdims=True)
            dw_ref[2:3, :] = jnp.sum(a_w2[...], axis=0, keepdims=True)
            db_ref[...] = jnp.sum(a_b[...], axis=0, keepdims=True)

    swap = lambda spec: pl.BlockSpec(spec.block_shape, lambda j, i, _m=spec.index_map: _m(i, j))
    return pl.pallas_call(
        body, name=name, grid=(d // tn, nsteps),
        in_specs=[pl.BlockSpec((tb, tn), lambda j, i: (i, j)), swap(cur(0)), swap(cur(1)), swap(cur(2)),
                  swap(prev(1)), swap(prev(2)), swap(nxt(0)), swap(nxt(0)),
                  pl.BlockSpec((3, tn), lambda j, i: (0, j)), pl.BlockSpec((1, tn), lambda j, i: (0, j))],
        out_specs=[pl.BlockSpec((3, tb, tn), lambda j, i: (0, i, j)), pl.BlockSpec((3, tn), lambda j, i: (0, j)),
                   pl.BlockSpec((1, tn), lambda j, i: (0, j))],
        out_shape=[jax.ShapeDtypeStruct((3, s, d), BF16), jax.ShapeDtypeStruct((3, d), F32),
                   jax.ShapeDtypeStruct((1, d), F32)],
        scratch_shapes=[pltpu.VMEM((SUBLANES, tn), F32)] * 4, compiler_params=_params(2),
    )(dz, p, p, p, p, p, dz, p, conv_w, conv_b)


def _block_cumsum(v):
    tb = v.shape[0]
    row = lax.broadcasted_iota(jnp.int32, v.shape, 0)
    sh = 1
    while sh < tb:
        v = v + jnp.where(row >= sh, pltpu.roll(v, sh, 0), 0.0)
        sh *= 2
    return v


def _fgate_fwd(name, zf, b_f):
    s = zf.shape[0]
    tb = _tile(s, 512, SUBLANES)

    def body(z_ref, b_ref, f_ref, ft_ref, carry):
        i = pl.program_id(0)
        z = z_ref[...] + b_ref[...]
        ls = jnp.minimum(z, 0.0) - jnp.log(1.0 + jnp.exp(-jnp.abs(z)))
        run = _block_cumsum(ls)

        @pl.when(i == 0)
        def _():
            carry[...] = jnp.zeros_like(carry)

        out = run + carry[...]
        f_ref[...] = out
        ft_ref[...] = jnp.transpose(out)
        carry[...] = out[tb - 1:tb, :]

    blk = pl.BlockSpec((tb, LANES), lambda i: (i, 0))
    return pl.pallas_call(
        body, name=name, grid=(s // tb,), in_specs=[blk, pl.BlockSpec((1, LANES), lambda i: (0, 0))],
        out_specs=[blk, pl.BlockSpec((LANES, tb), lambda i: (0, i))],
        out_shape=[jax.ShapeDtypeStruct((s, LANES), F32), jax.ShapeDtypeStruct((LANES, s), F32)],
        scratch_shapes=[pltpu.VMEM((1, LANES), F32)], compiler_params=_params(1))(zf, b_f)


def _fgate_bwd(name, dfcum, zf, b_f):
    s = zf.shape[0]
    tb = _tile(s, 512, SUBLANES)
    nsteps = s // tb

    def body(df_ref, z_ref, b_ref, dz_ref, db_ref, carry, acc):
        i = pl.program_id(0)
        df = df_ref[...]
        incl = _block_cumsum(df)
        total = incl[tb - 1:tb, :]

        @pl.when(i == 0)
        def _():
            carry[...] = jnp.zeros_like(carry)

        suffix = total - incl + df + carry[...]
        carry[...] += total
        dzv = suffix * jax.nn.sigmoid(-(z_ref[...] + b_ref[...]))
        dz_ref[...] = dzv
        _accumulate(i, acc, _fold_rows(dzv))

        @pl.when(i == nsteps - 1)
        def _():
            db_ref[...] = jnp.sum(acc[...], axis=0, keepdims=True)

    blk = pl.BlockSpec((tb, LANES), lambda i: (nsteps - 1 - i, 0))
    vec = pl.BlockSpec((1, LANES), lambda i: (0, 0))
    return pl.pallas_call(
        body, name=name, grid=(nsteps,), in_specs=[blk, blk, vec], out_specs=[blk, vec],
        out_shape=[jax.ShapeDtypeStruct((s, LANES), F32), jax.ShapeDtypeStruct((1, LANES), F32)],
        scratch_shapes=[pltpu.VMEM((1, LANES), F32), pltpu.VMEM((SUBLANES, LANES), F32)],
        compiler_params=_params(1))(dfcum, zf, b_f)


def _lane_column(block, lane):
    sel = lax.broadcasted_iota(jnp.int32, block.shape, 1) == lane
    return jnp.sum(jnp.where(sel, block, 0.0), axis=1, keepdims=True)


def _col_to_row(col):
    return jnp.transpose(jnp.broadcast_to(col, (col.shape[0], LANES)))[0:1, :]


def _attn_tiles(s):
    t = _tile(s, 1024, LANES)
    return t, t


def _attn_fwd(name, q, kv, fcum_t, n_heads):
    s, d = q.shape
    dh = d // n_heads
    tq, tk = _attn_tiles(s)
    scale2 = LOG2E / math.sqrt(dh)

    def body(q_ref, k_ref, v_ref, ft_ref, o_ref, l_ref):
        qi = pl.program_id(1)
        qv = q_ref[...]

        def step(kj, carry, diagonal):
            m, l, acc = carry
            start = pl.multiple_of(kj * tk, tk)
            kb = k_ref[pl.ds(start, tk), :]
            vb = v_ref[pl.ds(start, tk), :]
            sc = lax.dot_general(qv, kb, _DN["nt"], preferred_element_type=F32) * scale2
            sc = sc - ft_ref[:, pl.ds(start, tk)] * LOG2E
            if diagonal:
                keep = lax.broadcasted_iota(jnp.int32, sc.shape, 0) >= lax.broadcasted_iota(jnp.int32, sc.shape, 1)
                sc = jnp.where(keep, sc, -jnp.inf)
            m_new = jnp.maximum(m, jnp.max(sc, axis=1, keepdims=True))
            alpha = jnp.exp2(m - m_new)
            pr = jnp.exp2(sc - m_new)
            l = alpha * l + jnp.sum(pr, axis=1, keepdims=True)
            acc = alpha * acc + lax.dot_general(pr.astype(BF16), vb, _DN["nn"], preferred_element_type=F32)
            return m_new, l, acc

        init = (jnp.full((tq, 1), -jnp.inf, F32), jnp.zeros((tq, 1), F32), jnp.zeros((tq, dh), F32))
        carry = lax.fori_loop(0, qi, lambda kj, cr: step(kj, cr, False), init)
        m, l, acc = step(qi, carry, True)
        o_ref[...] = (acc / l).astype(BF16)
        l_ref[...] = _col_to_row(m + jnp.log2(l))

    nh = n_heads
    return pl.pallas_call(
        body, name=name, grid=(nh, s // tq),
        in_specs=[pl.BlockSpec((tq, dh), lambda h, i: (i, h)),
                  pl.BlockSpec((s, dh), lambda h, i: (0, h)),
                  pl.BlockSpec((s, dh), lambda h, i: (0, nh + h)),
                  pl.BlockSpec((None, 1, s), lambda h, i: (h, 0, 0))],
        out_specs=[pl.BlockSpec((tq, dh), lambda h, i: (i, h)),
                   pl.BlockSpec((None, 1, tq), lambda h, i: (h, 0, i))],
        out_shape=[jax.ShapeDtypeStruct((s, d), BF16), jax.ShapeDtypeStruct((nh, 1, s), F32)],
        compiler_params=_params(2))(q, kv, kv, fcum_t)


def _attn_delta(name, d_o, o, n_heads):
    s, d = d_o.shape
    dh = d // n_heads
    tb = _row_tile(s)

    def body(do_ref, o_ref, dlt_ref, dob_ref):
        dov = do_ref[...]
        prod = dov * o_ref[...].astype(F32)
        lane = lax.broadcasted_iota(jnp.int32, (tb, LANES), 1)
        out = jnp.zeros((tb, LANES), F32)
        for h in range(n_heads):
            out = jnp.where(lane == h, jnp.sum(prod[:, h * dh:(h + 1) * dh], axis=1, keepdims=True), out)
        dlt_ref[...] = jnp.transpose(out)
        dob_ref[...] = dov.astype(BF16)

    row = pl.BlockSpec((tb, d), lambda i: (i, 0))
    return pl.pallas_call(
        body, name=name, grid=(s // tb,), in_specs=[row, row],
        out_specs=[pl.BlockSpec((LANES, tb), lambda i: (0, i)), row],
        out_shape=[jax.ShapeDtypeStruct((LANES, s), F32), jax.ShapeDtypeStruct((s, d), BF16)],
        compiler_params=_params(1))(d_o, o)


def _attn_bwd(name, q, kv, d_o, fcum, lse_t, delta_t, n_heads):
    s, d = q.shape
    dh = d // n_heads
    tq, tk = _attn_tiles(s)
    nq, nk = s // tq, s // tk
    scale = 1.0 / math.sqrt(dh)
    scale2 = LOG2E * scale

    def body(q_ref, k_ref, v_ref, do_ref, f_ref, lt_ref, dlt_ref, dq_ref, dkv_ref, dfq_ref, dfk_ref, dq_acc, dfq_acc):
        h, kj = pl.program_id(0), pl.program_id(1)
        kb, vb = k_ref[...], v_ref[...]
        fk2 = _lane_column(f_ref[...], h) * LOG2E

        @pl.when(kj == 0)
        def _():
            dq_acc[...] = jnp.zeros_like(dq_acc)
            dfq_acc[...] = jnp.zeros_like(dfq_acc)

        def step(qi, carry, diagonal):
            dk, dv, dfk = carry
            start = pl.multiple_of(qi * tq, tq)
            qb = q_ref[pl.ds(start, tq), :]
            dob = do_ref[pl.ds(start, tq), :]
            sc = lax.dot_general(kb, qb, _DN["nt"], preferred_element_type=F32) * scale2 - fk2
            pr = jnp.exp2(sc - lt_ref[:, pl.ds(start, tq)])
            if diagonal:
                keep = lax.broadcasted_iota(jnp.int32, sc.shape, 1) >= lax.broadcasted_iota(jnp.int32, sc.shape, 0)
                pr = jnp.where(keep, pr, 0.0)
            dv = dv + lax.dot_general(pr.astype(BF16), dob, _DN["nn"], preferred_element_type=F32)
            dp = lax.dot_general(vb, dob, _DN["nt"], preferred_element_type=F32)
            ds = pr * (dp - dlt_ref[:, pl.ds(start, tq)])
            ds_b = ds.astype(BF16)
            dk = dk + lax.dot_general(ds_b, qb, _DN["nn"], preferred_element_type=F32)
            dfk = dfk - jnp.sum(ds, axis=1, keepdims=True)
            dq_acc[pl.ds(start, tq), :] += lax.dot_general(ds_b, kb, _DN["tn"], preferred_element_type=F32)
            dfq_acc[:, pl.ds(start, tq)] += jnp.sum(ds, axis=0, keepdims=True)
            return dk, dv, dfk

        init = (jnp.zeros((tk, dh), F32), jnp.zeros((tk, dh), F32), jnp.zeros((tk, 1), F32))
        carry = step(kj, init, True)
        dk, dv, dfk = lax.fori_loop(kj + 1, nq, lambda qi, cr: step(qi, cr, False), carry)
        dkv_ref[0] = (dk * scale).astype(BF16)
        dkv_ref[1] = dv.astype(BF16)
        dfk_ref[...] = _col_to_row(dfk)

        @pl.when(kj == nk - 1)
        def _():
            dq_ref[...] = (dq_acc[...] * scale).astype(BF16)
            dfq_ref[...] = dfq_acc[...]

    nh = n_heads
    full = pl.BlockSpec((s, dh), lambda h, j: (0, h))
    row_t = pl.BlockSpec((None, 1, s), lambda h, j: (h, 0, 0))
    return pl.pallas_call(
        body, name=name, grid=(nh, nk),
        in_specs=[full, pl.BlockSpec((tk, dh), lambda h, j: (j, h)), pl.BlockSpec((tk, dh), lambda h, j: (j, nh + h)),
                  full, pl.BlockSpec((tk, LANES), lambda h, j: (j, 0)), row_t, row_t],
        out_specs=[full, pl.BlockSpec((2, tk, dh), lambda h, j: (0, j, h)), row_t,
                   pl.BlockSpec((None, 1, tk), lambda h, j: (h, 0, j))],
        out_shape=[jax.ShapeDtypeStruct((s, d), BF16), jax.ShapeDtypeStruct((2, s, d), BF16),
                   jax.ShapeDtypeStruct((nh, 1, s), F32), jax.ShapeDtypeStruct((nh, 1, s), F32)],
        scratch_shapes=[pltpu.VMEM((s, dh), F32), pltpu.VMEM((1, s), F32)],
        compiler_params=_params(2))(q, kv, kv, d_o, fcum, lse_t, delta_t)


def _sum_slots(name, g):
    n, r, lanes = g.shape

    def body(g_ref, o_ref):
        total = g_ref[0]
        for i in range(1, n):
            total = total + g_ref[i]
        o_ref[...] = total

    return pl.pallas_call(
        body, name=name, out_shape=jax.ShapeDtypeStruct((r, lanes), F32),
        in_specs=[pl.BlockSpec(memory_space=pltpu.VMEM)], out_specs=pl.BlockSpec(memory_space=pltpu.VMEM),
        compiler_params=pltpu.CompilerParams(vmem_limit_bytes=VMEM_LIMIT))(g)


def _adamw(name, parts, w, m, v, row0=0, previous=None):
    npart, r, c = parts.shape
    tb = _tile(math.gcd(r, row0) if row0 else r, max(SUBLANES, (2**18 // c) // SUBLANES * SUBLANES), SUBLANES)
    off = row0 // tb
    c1 = 1.0 - ADAM_B1 ** ADAM_STEP
    c2 = 1.0 - ADAM_B2 ** ADAM_STEP

    def body(p_ref, w_ref, m_ref, v_ref, *rest):
        g_out, d_out, m_out, v_out = rest[-4:]
        g = p_ref[0].astype(F32)
        for i in range(1, npart):
            g = g + p_ref[i].astype(F32)
        m_new = ADAM_B1 * m_ref[...] + (1.0 - ADAM_B1) * g
        v_new = ADAM_B2 * v_ref[...] + (1.0 - ADAM_B2) * (g * g)
        m_hat = m_new / c1
        v_hat = v_new / c2
        g_out[...] = g
        d_out[...] = -ADAM_LR * (m_hat / (jnp.sqrt(v_hat) + ADAM_EPS) + ADAM_WD * w_ref[...])
        m_out[...] = m_new
        v_out[...] = v_new

    blk = pl.BlockSpec((tb, c), lambda i: (i + off, 0))
    shape = jax.ShapeDtypeStruct(w.shape, F32)
    in_specs = [pl.BlockSpec((npart, tb, c), lambda i: (0, i, 0)), blk, blk, blk]
    operands = [parts, w, m, v]
    aliases = {}
    if previous is not None:
        in_specs += [pl.BlockSpec(memory_space=pl.ANY)] * 4
        operands += list(previous)
        aliases = {4 + k: k for k in range(4)}
    return pl.pallas_call(
        body, name=name, grid=(r // tb,), in_specs=in_specs, out_specs=[blk] * 4, out_shape=[shape] * 4,
        input_output_aliases=aliases, compiler_params=_params(1))(*operands)


def _pack(vectors):
    flat = jnp.concatenate([v.reshape(-1).astype(F32) for v in vectors])
    pad = (-flat.shape[0]) % (SUBLANES * LANES)
    return jnp.pad(flat, (0, pad)).reshape(-1, LANES)


def _unpack(flat, shapes):
    out, pos = [], 0
    for shp in shapes:
        size = math.prod(shp)
        out.append(flat[..., pos:pos + size].reshape(flat.shape[:-1] + tuple(shp)))
        pos += size
    return out


def _unshard_last(g, lead):
    nd = len(lead)
    return jnp.moveaxis(g, 0, nd).reshape(tuple(lead) + (-1,))


def _my_slice(full, me, width, axis):
    return lax.dynamic_slice_in_dim(full, me * width, width, axis)


def kernel(x, c, norm_g, w_ada, b_ada, w_ffn_in, w_ffn_out, w_conv_in, conv_w, conv_b, w_conv_out, kv_norm_g, w_ada_kv, b_ada_kv, w_kvf, b_fgate, w_q, w_o, final_g, loss_target, m_norm_g, m_w_ada, m_b_ada, m_w_ffn_in, m_w_ffn_out, m_w_conv_in, m_conv_w, m_conv_b, m_w_conv_out, m_kv_norm_g, m_w_ada_kv, m_b_ada_kv, m_w_kvf, m_b_fgate, m_w_q, m_w_o, m_final_g, v_norm_g, v_w_ada, v_b_ada, v_w_ffn_in, v_w_ffn_out, v_w_conv_in, v_conv_w, v_conv_b, v_w_conv_out, v_kv_norm_g, v_w_ada_kv, v_b_ada_kv, v_w_kvf, v_b_fgate, v_w_q, v_w_o, v_final_g):
    s, d = x.shape[1], x.shape[2]
    n_heads = b_fgate.shape[0]
    n_layers = w_ada.shape[0]
    dsh = d // N_DEV
    f = w_ffn_out.shape[2] * N_DEV
    ada_w = w_ada.shape[2]
    kv_w = w_ada_kv.shape[1]
    kvf_w = w_kvf.shape[1]
    assert n_layers == 2 and w_conv_in.shape[0] == 1 and w_q.shape[0] == 1
    assert (d // n_heads) % LANES == 0 and n_heads <= LANES and N_DEV * kvf_w == 2 * d + n_heads
    me = 4 * lax.axis_index("x") + 2 * lax.axis_index("y") + lax.axis_index("c")
    x0, target = x[0], loss_target[0]

    small = _allgather_small("gather_small_params", _pack([c, norm_g, conv_w, conv_b]))
    c_all, ng_sh, cw_sh, cb_sh = _unpack(small.reshape(N_DEV, -1), [(d,), (n_layers, 3, dsh), (3, dsh), (dsh,)])
    norm_g_full = _unshard_last(ng_sh, (n_layers, 3))
    conv_w_full = _unshard_last(cw_sh, (3,))
    conv_b_full = _unshard_last(cb_sh, ()).reshape(1, d)
    c_rows = jnp.pad(c_all, ((0, 16 - N_DEV), (0, 0)))

    ada_cols = [
        _mm_bias_rows(f"ada_rows_{l}", c_rows, w_ada[l], _my_slice(b_ada[l], me, ada_w, 0).reshape(1, ada_w), _silu)
        for l in range(n_layers)]
    ada_cols.append(_mm_bias_rows("ada_rows_kv", c_rows, w_ada_kv, _my_slice(b_ada_kv, me, kv_w, 0).reshape(1, kv_w),
                                  _silu))
    ada_part = jnp.concatenate([a[:N_DEV] for a in ada_cols], axis=1)
    ada_all = _allgather_small("gather_ada_rows", _pack([ada_part]))
    ada_all = ada_all.reshape(N_DEV, -1)[:, :ada_part.size].reshape(N_DEV, N_DEV, -1)
    ada_mine = lax.dynamic_index_in_dim(ada_all, me, axis=1, keepdims=False)
    ada = [ada_mine[:, l * ada_w:(l + 1) * ada_w].reshape(3, 3, 1, d) for l in range(n_layers)]
    ada_kv = ada_mine[:, n_layers * ada_w:].reshape(2, 1, d)

    def as_item(w, kind):
        return (w.astype(BF16), kind)

    gather_first = _GatherComm([as_item(w_ffn_in[0, 0], "cols")])
    gather_l0b = _GatherComm([as_item(w_ffn_out[0, 0], "rows"), as_item(w_ffn_in[0, 1], "cols")])
    gather_conv = _GatherComm([as_item(w_conv_in[0], "cols"), as_item(w_conv_out[0], "rows")])
    gather_l0b_out = _GatherComm([as_item(w_ffn_out[0, 1], "rows")])
    gather_l1a = _GatherComm([as_item(w_ffn_out[1, 0], "rows"), as_item(w_ffn_in[1, 0], "cols")])
    gather_attn = _GatherComm([as_item(w_q[0], "rows"), as_item(w_kvf, "lead"), as_item(w_o[0], "rows")])
    gather_l1b = _GatherComm([as_item(w_ffn_out[1, 1], "rows"), as_item(w_ffn_in[1, 1], "cols")])
    (wf_in00,) = _run_comm("gather_ffn_first", gather_first)
    b_f = jnp.pad(b_fgate, (0, LANES - n_heads)).reshape(1, LANES)

    def ffn_fwd(tag, xs, l, sub, w_in, w_out, comm_in, comm_out):
        shift, scale, gate = ada[l][sub]
        tiles = FFN_TILES
        ab, u, h = _mm_swiglu(f"ffn_in_{tag}", xs, w_in, f, tiles["in"], comm=comm_in,
                              a_norm=(norm_g_full[l, sub].reshape(1, d), shift, scale))
        y, x_next = _mm_residual(f"ffn_out_{tag}", u, w_out, xs, 0.5 * gate, tiles["out"], comm=comm_out)
        return x_next, (xs, h, ab, u, y, w_in, w_out)

    ab, u, h = _mm_swiglu("ffn_in_l0a", x0, wf_in00, f, FFN_TILES["in"], comm=gather_l0b,
                          a_norm=(norm_g_full[0, 0].reshape(1, d), ada[0][0][0], ada[0][0][1]))
    wf_out00, wf_in01 = gather_l0b.results
    y, x1 = _mm_residual("ffn_out_l0a", u, wf_out00, x0, 0.5 * ada[0][0][2], FFN_TILES["out"], comm=gather_conv)
    save_f0 = (x0, h, ab, u, y, wf_in00, wf_out00)
    wc_in, wc_out = gather_conv.results
    shift, scale, gate = ada[0][1]
    p_c, h_c = _mm_nn("conv_in", x1, wc_in, F32, comm=gather_l0b_out,
                      a_norm=(norm_g_full[0, 1].reshape(1, d), shift, scale))
    (wf_out01,) = gather_l0b_out.results
    z_c = _conv_fwd("conv_mix", p_c, conv_w_full, conv_b_full)
    y_c, x2 = _mm_residual("conv_out", z_c, wc_out, x1, gate)
    x3, save_f1 = ffn_fwd("l0b", x2, 0, 2, wf_in01, wf_out01, gather_l1a, gather_attn)
    wf_out10, wf_in10 = gather_l1a.results
    wq, g_kvf, wo = gather_attn.results
    wkvf = _unshard_last(g_kvf, (d,))
    wkv = wkvf[:, :2 * d]
    wf_gate = jnp.pad(wkvf[:, 2 * d:], ((0, 0), (0, LANES - n_heads)))
    x4, save_f2 = ffn_fwd("l1a", x3, 1, 0, wf_in10, wf_out10, gather_l1b, None)
    wf_out11, wf_in11 = gather_l1b.results
    shift, scale, gate_a = ada[1][1]
    q, h_q = _mm_nn("attn_q", x4, wq, BF16, a_norm=(norm_g_full[1, 1].reshape(1, d), shift, scale))
    kv, h_kv = _mm_nn("attn_kv", x4, wkv, BF16, a_norm=(kv_norm_g.reshape(1, d), ada_kv[0], ada_kv[1]))
    zf = _mm_nn("attn_fgate", h_kv, wf_gate, F32)
    fcum, fcum_lanes = _fgate_fwd("fgate_cumsum", zf, b_f)
    fcum_t = fcum_lanes[:n_heads].reshape(n_heads, 1, s)
    o, lse_t = _attn_fwd("attn_fwd", q, kv, fcum_t, n_heads)
    y_a, x5 = _mm_residual("attn_out", o, wo, x4, gate_a)
    x6, save_f3 = ffn_fwd("l1b", x5, 1, 2, wf_in11, wf_out11, None, None)
    dx, loss_part, d_final_g = _final_loss("final_loss", x6, target, final_g.reshape(1, d))

    d_ada = [[[None] * 3 for _ in range(3)] for _ in range(n_layers)]
    d_norm_g = [[None] * 3 for _ in range(n_layers)]
    core = lax.axis_index("c").astype(jnp.int32).reshape(1)
    in_shard, out_shard, row_shard = (d, 2 * f // N_DEV), (f // N_DEV, d), (dsh, d)

    def chip_exchange(tag, items, from_sibling):
        return _ChipComm([_pair_add(f"add_sibling_{tag}_{n}", p, kind, shape, t, core)
                          for n, ((p, kind, shape), t) in enumerate(zip(items, from_sibling))])

    def reduce_within_chip(tag, items):
        return chip_exchange(tag, items, _run_comm(f"reduce_sibling_{tag}", _SiblingComm(items)))

    def ffn_bwd(tag, dxs, dy, saved, l, sub, comm_dw, then):
        xs, h, ab, u, y, w_in, w_out = saved
        tiles = FFN_TILES
        dab = _mm_nt_swiglu_bwd(f"ffn_out_bwd_{tag}", dy, w_out, ab, tiles["out_bwd"])
        dw_in = _mm_tn(f"ffn_in_dw_{tag}", h, dab, BF16, tiles=tiles["in_dw"])
        sibling_in = _SiblingComm([(dw_in, "cols", in_shard)])
        dw_out = _mm_tn(f"ffn_out_dw_{tag}", u, dy, BF16, tiles=tiles["out_dw"], comm=_join(sibling_in, comm_dw))
        sibling_out = _run_comm(f"reduce_sibling_{tag}", _SiblingComm([(dw_out, "rows", out_shard)]))
        scatter = chip_exchange(tag, [(dw_in, "cols", in_shard), (dw_out, "rows", out_shard)],
                                list(sibling_in.results) + list(sibling_out))
        dh = _mm_nt_parts(f"ffn_in_bwd_{tag}", dab, w_in, F32, tiles["in_bwd"], comm=scatter)
        outs = _norm_mod_bwd(f"norm_bwd_{tag}", dh, xs, norm_g_full[l, sub].reshape(1, d), ada[l][sub][1], dxs, then)
        dxs, d_ada[l][sub][0], d_ada[l][sub][1], d_norm_g[l][sub] = outs[:4]
        return dxs, outs[4:], scatter

    def gate_of(saved, l, sub):
        return (saved[4], 0.5 * ada[l][sub][2], 0.5)

    dy, d_ada[1][2][2] = _gate_bwd("gate_bwd_l1b", dx, save_f3[4], 0.5 * ada[1][2][2], 0.5)
    dx, (dy, d_ada[1][1][2]), scatter_l1b = ffn_bwd("l1b", dx, dy, save_f3, 1, 2, None, (y_a, gate_a, 1.0))
    d_o = _mm_nt("attn_out_bwd", dy, wo, F32)
    dwo = _mm_tn("attn_out_dw", o, dy, BF16)
    delta_lanes, d_ob = _attn_delta("attn_delta", d_o, o, n_heads)
    delta_t = delta_lanes[:n_heads].reshape(n_heads, 1, s)
    dq, dkv, dfq, dfk = _attn_bwd("attn_bwd", q, kv, d_ob, fcum, lse_t, delta_t, n_heads)
    dfcum = jnp.pad((dfq + dfk).reshape(n_heads, s).T, ((0, 0), (0, LANES - n_heads)))
    dzf, d_bf = _fgate_bwd("fgate_bwd", dfcum, zf, b_f)
    dh_q = _mm_nt("attn_q_bwd", dq, wq, F32)
    dwq = _mm_tn("attn_q_dw", h_q, dq, BF16)
    dh_kv = _mm_nt_parts("attn_kv_bwd", dkv, wkv, F32)
    dh_kv = _mm_nt("attn_fgate_bwd", dzf, wf_gate, F32, extra_add=dh_kv)
    dwkv = _mm_tn("attn_kv_dw", h_kv, dkv, BF16)
    dwf = _mm_tn("attn_fgate_dw", h_kv, dzf, BF16)
    dwkvf = jnp.concatenate([dwkv, dwf[:, :n_heads]], axis=1)
    p_kvf = jnp.moveaxis(dwkvf.reshape(d, N_DEV, kvf_w), 1, 0)
    scatter_attn = reduce_within_chip(
        "attn", [(dwo, "rows", row_shard), (dwq, "rows", row_shard), (p_kvf, "lead", (d, kvf_w))])
    dx, d_ada[1][1][0], d_ada[1][1][1], d_norm_g[1][1] = _norm_mod_bwd(
        "norm_bwd_q", dh_q, x4, norm_g_full[1, 1].reshape(1, d), ada[1][1][1], dx)
    dx, d_kv_shift, d_kv_scale, d_kv_norm_g, dy, d_ada[1][0][2] = _norm_mod_bwd(
        "norm_bwd_kv", dh_kv, x4, kv_norm_g.reshape(1, d), ada_kv[1], dx, gate_of(save_f2, 1, 0))
    dx, (dy, d_ada[0][2][2]), scatter_l1a = ffn_bwd("l1a", dx, dy, save_f2, 1, 0, scatter_attn, gate_of(save_f1, 0, 2))
    dx, (dy, d_ada[0][1][2]), scatter_l0b = ffn_bwd("l0b", dx, dy, save_f1, 0, 2, None, (y_c, ada[0][1][2], 1.0))
    dz = _mm_nt("conv_out_bwd", dy, wc_out, F32)
    dwc_out = _mm_tn("conv_out_dw", z_c, dy, BF16)
    dp, d_conv_w, d_conv_b = _conv_bwd("conv_mix_bwd", dz, p_c, conv_w_full, conv_b_full)
    dh_c = _mm_nt_parts("conv_in_bwd", dp, wc_in, F32)
    dwc_in = _mm_tn("conv_in_dw", h_c, dp, BF16)
    scatter_conv = reduce_within_chip("conv", [(dwc_out, "rows", row_shard), (dwc_in, "cols", (d, 3 * d // N_DEV))])
    dx, d_ada[0][1][0], d_ada[0][1][1], d_norm_g[0][1], dy, d_ada[0][0][2] = _norm_mod_bwd(
        "norm_bwd_conv", dh_c, x1, norm_g_full[0, 1].reshape(1, d), ada[0][1][1], dx, gate_of(save_f0, 0, 0))
    dx, _, scatter_l0a = ffn_bwd("l0a", dx, dy, save_f0, 0, 0, scatter_conv, None)
    grad_x = dx.reshape(1, s, d)

    d_ada_flat = jnp.concatenate([v.reshape(-1) for l in range(n_layers) for sub in range(3) for v in d_ada[l][sub]])
    d_norm_flat = jnp.concatenate([d_norm_g[l][sub].reshape(-1) for l in range(n_layers) for sub in range(3)])
    small_shapes = [(n_layers * 9 * d,), (2 * d,), (n_layers * 3 * d,), (3, d), (d,), (d,), (LANES,), (d,), (LANES,)]
    small_parts = _pack([d_ada_flat, d_kv_shift, d_kv_scale, d_norm_flat, d_conv_w, d_conv_b, d_kv_norm_g, d_bf,
                         d_final_g, loss_part])
    small_all = _allgather_small("gather_small_grads", small_parts)
    small_sum = _sum_slots("sum_small_grads", small_all).reshape(-1)
    g_b_ada, g_b_ada_kv, g_norm_full, g_conv_w_full, g_conv_b_full, g_kv_norm, g_bf, g_final, loss_v = _unpack(
        small_sum, small_shapes)
    loss = loss_v[0]
    d_ada_rows = small_all.reshape(N_DEV, -1)[:, :n_layers * 9 * d + 2 * d]
    d_ada_rows = jnp.pad(d_ada_rows, ((0, 16 - N_DEV), (0, 0)))

    results = {}

    def update(key, parts, w, m, v):
        shape = w.shape
        c_dim = shape[-1]
        outs = _adamw(f"adamw_{key}", parts.reshape(parts.shape[0], -1, c_dim), w.reshape(-1, c_dim),
                      m.reshape(-1, c_dim), v.reshape(-1, c_dim))
        results[key] = [o.reshape(shape) for o in outs]

    g_w_ada = []
    for l in range(n_layers):
        cols = _my_slice(d_ada_rows[:, l * 9 * d:(l + 1) * 9 * d], me, ada_w, 1)
        g_w_ada.append(_mm_tn(f"ada_dw_{l}", c_rows, cols, F32, a_fn=_silu))
    update("w_ada", jnp.stack(g_w_ada).reshape(1, n_layers * d, ada_w), w_ada, m_w_ada, v_w_ada)
    cols = _my_slice(d_ada_rows[:, n_layers * 9 * d:], me, kv_w, 1)
    update("w_ada_kv", _mm_tn("ada_dw_kv", c_rows, cols, F32, a_fn=_silu).reshape(1, d, kv_w),
           w_ada_kv, m_w_ada_kv, v_w_ada_kv)

    def update_rows(key, parts_list, w, m, v):
        c_dim = w.shape[-1]
        flat = [t.reshape(-1, c_dim) for t in (w, m, v)]
        outs = None
        for n, parts in enumerate(parts_list):
            outs = _adamw(f"adamw_{key}_{n}", parts, *flat, row0=n * parts.shape[1], previous=outs)
        results[key] = [o.reshape(w.shape) for o in outs]

    ffn_scatters = [scatter_l0a, scatter_l0b, scatter_l1a, scatter_l1b]
    update_rows("w_ffn_in", [sc.results[0] for sc in ffn_scatters], w_ffn_in, m_w_ffn_in, v_w_ffn_in)
    update_rows("w_ffn_out", [sc.results[1] for sc in ffn_scatters], w_ffn_out, m_w_ffn_out, v_w_ffn_out)
    r_co, r_ci = scatter_conv.results
    r_o, r_q, r_kvf = scatter_attn.results
    update_rows("w_conv_in", [r_ci], w_conv_in, m_w_conv_in, v_w_conv_in)
    update_rows("w_conv_out", [r_co], w_conv_out, m_w_conv_out, v_w_conv_out)
    update_rows("w_kvf", [r_kvf], w_kvf, m_w_kvf, v_w_kvf)
    update_rows("w_q", [r_q], w_q, m_w_q, v_w_q)
    update_rows("w_o", [r_o], w_o, m_w_o, v_w_o)

    small_keys = ["norm_g", "b_ada", "conv_w", "conv_b", "kv_norm_g", "b_ada_kv", "b_fgate", "final_g"]
    small_w = [norm_g, b_ada, conv_w, conv_b, kv_norm_g, b_ada_kv, b_fgate, final_g]
    small_m = [m_norm_g, m_b_ada, m_conv_w, m_conv_b, m_kv_norm_g, m_b_ada_kv, m_b_fgate, m_final_g]
    small_v = [v_norm_g, v_b_ada, v_conv_w, v_conv_b, v_kv_norm_g, v_b_ada_kv, v_b_fgate, v_final_g]
    small_g = [
        _my_slice(g_norm_full.reshape(n_layers, 3, d), me, dsh, 2), g_b_ada.reshape(b_ada.shape),
        _my_slice(g_conv_w_full, me, dsh, 1).reshape(conv_w.shape), _my_slice(g_conv_b_full, me, dsh, 0).reshape(
            conv_b.shape), g_kv_norm, g_b_ada_kv, g_bf[:n_heads], g_final]
    packed = _adamw("adamw_small", _pack(small_g)[None], _pack(small_w), _pack(small_m), _pack(small_v))
    for vals, idx in zip(packed, range(4)):
        for key, val in zip(small_keys, _unpack(vals.reshape(-1), [w.shape for w in small_w])):
            results.setdefault(key, [None] * 4)[idx] = val

    order = ["norm_g", "w_ada", "b_ada", "w_ffn_in", "w_ffn_out", "w_conv_in", "conv_w", "conv_b", "w_conv_out",
             "kv_norm_g", "w_ada_kv", "b_ada_kv", "w_kvf", "b_fgate", "w_q", "w_o", "final_g"]
    return (loss, grad_x, *[results[k][0] for k in order], *[results[k][1] for k in order],
            *[results[k][2] for k in order], *[results[k][3] for k in order])
```

```python
import functools
import math

import jax
import jax.numpy as jnp
from jax import lax
from jax.experimental import pallas as pl
from jax.experimental.pallas import tpu as pltpu

F32 = jnp.float32
BF16 = jnp.bfloat16
MESH = pl.DeviceIdType.MESH

N_DEV = 8
LANES = 128
SUBLANES = 8
VMEM_LIMIT = 56 * 2**20
EPS = 1e-6
LOG2E = math.log2(math.e)
ADAM_LR, ADAM_B1, ADAM_B2, ADAM_EPS, ADAM_WD, ADAM_STEP = 0.001, 0.9, 0.999, 1e-08, 0.01, 10

_DN = {"nn": (((1,), (0,)), ((), ())), "nt": (((1,), (1,)), ((), ())), "tn": (((0,), (0,)), ((), ()))}


def _tile(n, pref, align):
    t = min(pref, n)
    t -= t % align
    while t >= align:
        if n % t == 0:
            return t
        t -= align
    return n


def _params(n_grid):
    return pltpu.CompilerParams(vmem_limit_bytes=VMEM_LIMIT, dimension_semantics=("arbitrary",) * n_grid)


def _sigmoid(v):
    return 0.5 * jnp.tanh(0.5 * v) + 0.5


def _silu(v):
    return v * _sigmoid(v)


def _position():
    x, y, c = lax.axis_index("x"), lax.axis_index("y"), lax.axis_index("c")
    return x, y, c


def _allgather_small(name, v):
    def body(v_ref, out_ref, send_sems, recv_sems, local_sem):
        x, y, c = _position()
        me, sibling = (x, y, c), (x, y, 1 - c)
        chips = [(1 - x, y), (x, 1 - y), (1 - x, 1 - y)]

        def slot(px, py, pc):
            return out_ref.at[4 * px + 2 * py + pc]

        def copy(k, block, to, src=None):
            return pltpu.make_async_remote_copy(
                src_ref=slot(*block) if src is None else src, dst_ref=slot(*block),
                send_sem=send_sems.at[k], recv_sem=recv_sems.at[k], device_id=to, device_id_type=MESH)

        mine = pltpu.make_async_copy(v_ref, slot(*me), local_sem)
        mine.start()
        first = [copy(0, me, sibling, src=v_ref)]
        first += [copy(1 + j, me, (*chip, c), src=v_ref) for j, chip in enumerate(chips)]
        for cp in first:
            cp.start()
        passed = [copy(4 + j, (*chip, c), sibling) for j, chip in enumerate(chips)]
        for j, chip in enumerate(chips):
            copy(1 + j, (*chip, c), me).wait_recv()
            passed[j].start()
        copy(0, sibling, me).wait_recv()
        for j, chip in enumerate(chips):
            copy(4 + j, (*chip, 1 - c), me).wait_recv()
        for cp in first + passed:
            cp.wait_send()
        mine.wait()

    return pl.pallas_call(
        body, name=name,
        out_shape=jax.ShapeDtypeStruct((N_DEV,) + v.shape, v.dtype),
        in_specs=[pl.BlockSpec(memory_space=pltpu.VMEM)],
        out_specs=pl.BlockSpec(memory_space=pltpu.VMEM),
        scratch_shapes=[pltpu.SemaphoreType.DMA((7,)), pltpu.SemaphoreType.DMA((7,)), pltpu.SemaphoreType.DMA],
        compiler_params=pltpu.CompilerParams(vmem_limit_bytes=VMEM_LIMIT),
    )(v)


def _owner_view(ref, kind, shard_shape, owner):
    r, w = shard_shape
    if kind == "rows":
        return ref.at[pl.ds(pl.multiple_of(owner * r, 16), r)]
    if kind == "cols":
        return ref.at[:, pl.ds(pl.multiple_of(owner * w, LANES), w)]
    return ref.at[owner]


def _full_shape(kind, shard_shape):
    r, w = shard_shape
    return {"rows": (N_DEV * r, w), "cols": (r, N_DEV * w), "lead": (N_DEV, r, w)}[kind]


class _GatherComm:
    mid_fraction = 0.85

    def __init__(self, items):
        self.ins = [shard for shard, _ in items]
        self.kinds = [kind for _, kind in items]
        n = len(items)
        self.out_shapes = [jax.ShapeDtypeStruct(_full_shape(k, a.shape), a.dtype) for a, k in items]
        self.sem_shapes = [pltpu.SemaphoreType.DMA((7 * n,)), pltpu.SemaphoreType.DMA((7 * n,)),
                           pltpu.SemaphoreType.DMA((n,))]
        self.results = None

    def _copy(self, refs, a, k, block, to, from_input=False):
        ins, outs, (send_sems, recv_sems, _) = refs
        px, py, pc = block
        dst = _owner_view(outs[a], self.kinds[a], self.ins[a].shape, 4 * px + 2 * py + pc)
        return pltpu.make_async_remote_copy(
            src_ref=ins[a] if from_input else dst, dst_ref=dst,
            send_sem=send_sems.at[7 * a + k], recv_sem=recv_sems.at[7 * a + k], device_id=to, device_id_type=MESH)

    def _local(self, refs, a):
        ins, outs, (_, _, local_sems) = refs
        x, y, c = _position()
        dst = _owner_view(outs[a], self.kinds[a], self.ins[a].shape, 4 * x + 2 * y + c)
        return pltpu.make_async_copy(ins[a], dst, local_sems.at[a])

    def start(self, *refs):
        x, y, c = _position()
        me, sibling = (x, y, c), (x, y, 1 - c)
        chips = [(1 - x, y), (x, 1 - y), (1 - x, 1 - y)]
        for a in range(len(self.ins)):
            self._local(refs, a).start()
            for j, chip in enumerate(chips):
                self._copy(refs, a, 1 + j, me, (*chip, c), from_input=True).start()
            self._copy(refs, a, 0, me, sibling, from_input=True).start()

    def mid(self, *refs):
        x, y, c = _position()
        chips = [(1 - x, y), (x, 1 - y), (1 - x, 1 - y)]
        for a in range(len(self.ins)):
            for j, chip in enumerate(chips):
                self._copy(refs, a, 1 + j, (*chip, c), (x, y, c)).wait_recv()
                self._copy(refs, a, 4 + j, (*chip, c), (x, y, 1 - c)).start()

    def finish(self, *refs):
        x, y, c = _position()
        me, sibling = (x, y, c), (x, y, 1 - c)
        chips = [(1 - x, y), (x, 1 - y), (1 - x, 1 - y)]
        for a in range(len(self.ins)):
            self._copy(refs, a, 0, sibling, me).wait_recv()
            for j, chip in enumerate(chips):
                self._copy(refs, a, 4 + j, (*chip, 1 - c), me).wait_recv()
        for a in range(len(self.ins)):
            self._copy(refs, a, 0, me, sibling, from_input=True).wait_send()
            for j, chip in enumerate(chips):
                self._copy(refs, a, 1 + j, me, (*chip, c), from_input=True).wait_send()
                self._copy(refs, a, 4 + j, (*chip, c), sibling).wait_send()
            self._local(refs, a).wait()


class _SiblingComm:
    mid_fraction = None

    def __init__(self, items):
        self.ins = [p for p, _, _ in items]
        self.kinds = [kind for _, kind, _ in items]
        self.shapes = [shape for _, _, shape in items]
        n = len(items)
        self.out_shapes = [jax.ShapeDtypeStruct((4,) + tuple(shape), p.dtype) for p, _, shape in items]
        self.sem_shapes = [pltpu.SemaphoreType.DMA((4 * n,)), pltpu.SemaphoreType.DMA((4 * n,))]
        self.results = None

    def _copies(self, refs):
        ins, outs, (send_sems, recv_sems) = refs
        x, y, c = _position()
        return [pltpu.make_async_remote_copy(
            src_ref=_owner_view(ins[a], self.kinds[a], self.shapes[a], 2 * k + 1 - c), dst_ref=outs[a].at[k],
            send_sem=send_sems.at[4 * a + k], recv_sem=recv_sems.at[4 * a + k],
            device_id=(x, y, 1 - c), device_id_type=MESH) for a in range(len(self.ins)) for k in range(4)]

    def start(self, *refs):
        for cp in self._copies(refs):
            cp.start()

    def finish(self, *refs):
        for cp in self._copies(refs):
            cp.wait()


class _ChipComm:
    mid_fraction = None

    def __init__(self, items):
        self.ins = list(items)
        n = len(items)
        self.out_shapes = [jax.ShapeDtypeStruct(p.shape, p.dtype) for p in items]
        self.sem_shapes = [pltpu.SemaphoreType.DMA((3 * n,)), pltpu.SemaphoreType.DMA((3 * n,)),
                           pltpu.SemaphoreType.DMA((n,))]
        self.results = None

    def _copy(self, refs, a, k, landed):
        ins, outs, (send_sems, recv_sems, _) = refs
        x, y, c = _position()
        px = 1 - x if k & 2 else x
        py = 1 - y if k & 1 else y
        to_chip, my_chip = 2 * px + py, 2 * x + y
        return pltpu.make_async_remote_copy(
            src_ref=ins[a].at[to_chip], dst_ref=outs[a].at[to_chip if landed else my_chip],
            send_sem=send_sems.at[3 * a + k - 1], recv_sem=recv_sems.at[3 * a + k - 1],
            device_id=(px, py, c), device_id_type=MESH)

    def _local(self, refs, a):
        ins, outs, (_, _, local_sems) = refs
        x, y, _ = _position()
        return pltpu.make_async_copy(ins[a].at[2 * x + y], outs[a].at[2 * x + y], local_sems.at[a])

    def start(self, *refs):
        for a in range(len(self.ins)):
            self._local(refs, a).start()
            for k in range(1, 4):
                self._copy(refs, a, k, False).start()

    def finish(self, *refs):
        for a in range(len(self.ins)):
            for k in range(1, 4):
                self._copy(refs, a, k, True).wait_recv()
        for a in range(len(self.ins)):
            for k in range(1, 4):
                self._copy(refs, a, k, False).wait_send()
            self._local(refs, a).wait()


class _Joined:
    def __init__(self, comms):
        self.comms = list(comms)
        self.ins = [a for c in self.comms for a in c.ins]
        self.out_shapes = [o for c in self.comms for o in c.out_shapes]
        self.sem_shapes = [o for c in self.comms for o in c.sem_shapes]
        fractions = [c.mid_fraction for c in self.comms if c.mid_fraction is not None]
        self.mid_fraction = max(fractions) if fractions else None

    def _parts(self, refs):
        ins, outs, sems = refs
        pos = [0, 0, 0]
        for c in self.comms:
            n = (len(c.ins), len(c.out_shapes), len(c.sem_shapes))
            yield c, (ins[pos[0]:pos[0] + n[0]], outs[pos[1]:pos[1] + n[1]], sems[pos[2]:pos[2] + n[2]])
            pos = [p + k for p, k in zip(pos, n)]

    def start(self, *refs):
        for c, part in self._parts(refs):
            c.start(*part)

    def mid(self, *refs):
        for c, part in self._parts(refs):
            if c.mid_fraction is not None:
                c.mid(*part)

    def finish(self, *refs):
        for c, part in self._parts(refs):
            c.finish(*part)

    @property
    def results(self):
        return [r for c in self.comms for r in c.results]

    @results.setter
    def results(self, values):
        pos = 0
        for c in self.comms:
            c.results = values[pos:pos + len(c.out_shapes)]
            pos += len(c.out_shapes)


def _join(*comms):
    present = [c for c in comms if c is not None]
    return present[0] if len(present) == 1 else (_Joined(present) if present else None)


def _run_comm(name, comm):
    n_in, n_out = len(comm.ins), len(comm.out_shapes)

    def body(*refs):
        parts = (refs[:n_in], refs[n_in:n_in + n_out], refs[n_in + n_out:])
        comm.start(*parts)
        if comm.mid_fraction is not None:
            comm.mid(*parts)
        comm.finish(*parts)

    any_spec = pl.BlockSpec(memory_space=pl.ANY)
    return pl.pallas_call(
        body, name=name, out_shape=list(comm.out_shapes), in_specs=[any_spec] * n_in, out_specs=[any_spec] * n_out,
        scratch_shapes=list(comm.sem_shapes))(*comm.ins)


def _carry(comm, body, grid, in_specs, out_specs, out_shapes, scratch_shapes, operands):
    if comm is None:
        return body, in_specs, out_specs, out_shapes, scratch_shapes, operands
    n_in, n_out, n_scr = len(in_specs), len(out_specs), len(scratch_shapes)
    c_in, c_out = len(comm.ins), len(comm.out_shapes)
    total = math.prod(grid)
    mid_step = None if comm.mid_fraction is None else min(total - 1, int(total * comm.mid_fraction))

    def wrapped(*refs):
        ins, c_ins = refs[:n_in], refs[n_in:n_in + c_in]
        outs = refs[n_in + c_in:n_in + c_in + n_out]
        c_outs = refs[n_in + c_in + n_out:n_in + c_in + n_out + c_out]
        scr = refs[n_in + c_in + n_out + c_out:n_in + c_in + n_out + c_out + n_scr]
        parts = (c_ins, c_outs, refs[n_in + c_in + n_out + c_out + n_scr:])
        step = pl.program_id(0)
        for axis in range(1, len(grid)):
            step = step * grid[axis] + pl.program_id(axis)

        @pl.when(step == 0)
        def _():
            comm.start(*parts)

        body(*ins, *outs, *scr)

        if mid_step is not None:
            @pl.when(step == mid_step)
            def _():
                comm.mid(*parts)

        @pl.when(step == total - 1)
        def _():
            comm.finish(*parts)

    any_spec = pl.BlockSpec(memory_space=pl.ANY)
    return (wrapped, list(in_specs) + [any_spec] * c_in, list(out_specs) + [any_spec] * c_out,
            list(out_shapes) + list(comm.out_shapes), list(scratch_shapes) + list(comm.sem_shapes),
            list(operands) + list(comm.ins))


def _pair_add(name, partial, kind, shard_shape, theirs, core):
    r, w = shard_shape
    tb = _tile(r, max(16, (2**19 // w) // 16 * 16), 16)
    nb = r // tb
    if kind == "rows":
        mine = pl.BlockSpec((tb, w), lambda k, i, core_ref: ((2 * k + core_ref[0]) * nb + i, 0))
    elif kind == "cols":
        mine = pl.BlockSpec((tb, w), lambda k, i, core_ref: (i, 2 * k + core_ref[0]))
    else:
        mine = pl.BlockSpec((None, tb, w), lambda k, i, core_ref: (2 * k + core_ref[0], i, 0))
    slot = pl.BlockSpec((None, tb, w), lambda k, i, core_ref: (k, i, 0))

    def body(core_ref, a_ref, b_ref, o_ref):
        o_ref[...] = (a_ref[...].astype(F32) + b_ref[...].astype(F32)).astype(o_ref.dtype)

    return pl.pallas_call(
        body, name=name,
        grid_spec=pltpu.PrefetchScalarGridSpec(num_scalar_prefetch=1, grid=(4, nb), in_specs=[mine, slot],
                                               out_specs=slot),
        out_shape=jax.ShapeDtypeStruct(theirs.shape, theirs.dtype), compiler_params=_params(2))(core, partial, theirs)


def _matmul(name, mode, grid, a, a_spec, bs, b_specs, extras, extra_specs, out_shapes, out_specs, acc_shape,
            epilogue, a_fn=None, comm=None, a_norm=None):
    nk, nb, ne, no = grid[2], len(bs), len(extras), len(out_shapes)
    if a_norm is not None:
        assert nk == 1 and mode != "tn"
        tm_a, k_a = a_spec.block_shape
        vec = pl.BlockSpec((1, k_a), lambda i, j, kk: (0, 0))
        extras, extra_specs = list(extras) + list(a_norm), list(extra_specs) + [vec] * 3
        out_shapes = list(out_shapes) + [jax.ShapeDtypeStruct(a.shape, BF16)]
        out_specs = list(out_specs) + [pl.BlockSpec((tm_a, k_a), lambda i, j, kk: (i, 0))]

    def body(*refs):
        a_ref, b_refs, e_refs = refs[0], refs[1:1 + nb], refs[1 + nb:1 + nb + ne]
        o_refs, acc_refs = refs[1 + nb + ne:1 + nb + ne + no], refs[1 + nb + ne + no:]
        if a_norm is None:
            a_val = a_ref[...]
            if a_fn is not None:
                a_val = a_fn(a_val)
            a_val = a_val.astype(BF16)
        else:
            g_ref, sh_ref, sc_ref = refs[1 + nb + ne:4 + nb + ne]
            o_refs = refs[4 + nb + ne:4 + nb + ne + no]
            h_ref, h_scr = refs[4 + nb + ne + no], refs[5 + nb + ne + no]

            @pl.when(pl.program_id(1) == 0)
            def _():
                xv = a_ref[...]
                rstd = lax.rsqrt(jnp.mean(xv * xv, axis=-1, keepdims=True) + EPS)
                hv = ((xv * rstd) * g_ref[...] * (1.0 + sc_ref[...]) + sh_ref[...]).astype(BF16)
                h_scr[...] = hv
                h_ref[...] = hv

            a_val = h_scr[...]

        def product(b_ref):
            return lax.dot_general(a_val, b_ref[...].astype(BF16), _DN[mode], preferred_element_type=F32)

        def finish(accs):
            outs = epilogue(accs, [e[...] for e in e_refs])
            for o_ref, o in zip(o_refs, outs):
                o_ref[...] = o.astype(o_ref.dtype)

        if nk == 1:
            finish([product(b_ref) for b_ref in b_refs])
        else:
            kk = pl.program_id(2)

            @pl.when(kk == 0)
            def _():
                for acc in acc_refs:
                    acc[...] = jnp.zeros_like(acc)

            for acc, b_ref in zip(acc_refs, b_refs):
                acc[...] += product(b_ref)

            @pl.when(kk == nk - 1)
            def _():
                finish([acc[...] for acc in acc_refs])

    scratch = [] if nk == 1 else [pltpu.VMEM(acc_shape, F32) for _ in range(nb)]
    if a_norm is not None:
        scratch = [pltpu.VMEM(a_spec.block_shape, BF16)]
    body, in_specs, out_specs, out_shape, scratch, operands = _carry(
        comm, body, grid, [a_spec] + list(b_specs) + list(extra_specs), list(out_specs), list(out_shapes), scratch,
        [a, *bs, *extras])
    outs = pl.pallas_call(
        body, name=name, grid=grid, in_specs=in_specs, out_specs=out_specs, out_shape=out_shape,
        scratch_shapes=scratch, compiler_params=_params(3))(*operands)
    n_mine = len(out_shapes)
    if comm is not None:
        comm.results = outs[n_mine:]
    return outs[:n_mine]


def _mm_tiles(m, n, k, tiles):
    return _tile(m, tiles[0], 16), _tile(n, tiles[1], LANES), _tile(k, tiles[2], LANES)


FFN_TILES = {"in": (1024, 512, 2048), "out": (1024, 512, 5632), "out_bwd": (2048, 512, 2048),
             "in_bwd": (1024, 1024, 2816), "out_dw": (512, 2048, 2048), "in_dw": (1024, 512, 4096)}
FULL_K = (1024, 1024, 2048)
LOOP_K = (1024, 1024, 2048)


def _mm_nn(name, a, b, out_dtype, b_col0=0, n=None, tiles=FULL_K, comm=None, a_norm=None):
    m, k = a.shape
    n = b.shape[1] if n is None else n
    tm, tn, tk = _mm_tiles(m, n, k, tiles)
    off = b_col0 // tn
    assert b_col0 % tn == 0
    outs = _matmul(
        name, "nn", (m // tm, n // tn, k // tk), a, pl.BlockSpec((tm, tk), lambda i, j, kk: (i, kk)),
        [b], [pl.BlockSpec((tk, tn), lambda i, j, kk: (kk, j + off))], [], [],
        [jax.ShapeDtypeStruct((m, n), out_dtype)], [pl.BlockSpec((tm, tn), lambda i, j, kk: (i, j))], (tm, tn),
        lambda accs, ex: accs[:1], comm=comm, a_norm=a_norm)
    return outs[0] if a_norm is None else outs


def _mm_bias_rows(name, a, b, bias, a_fn, b_block=0):
    m, k = a.shape
    n = b.shape[1]
    tn = _tile(n, 768, LANES)
    return _matmul(
        name, "nn", (1, n // tn, 1), a, pl.BlockSpec((m, k), lambda i, j, kk: (0, 0)),
        [b], [pl.BlockSpec((k, tn), lambda i, j, kk: (b_block, j))],
        [bias], [pl.BlockSpec((1, tn), lambda i, j, kk: (0, j))],
        [jax.ShapeDtypeStruct((m, n), F32)], [pl.BlockSpec((m, tn), lambda i, j, kk: (0, j))], (m, tn),
        lambda accs, ex: [accs[0] + ex[0]], a_fn=a_fn)[0]


def _mm_swiglu(name, h, w_in, f, tiles, comm=None, a_norm=None):
    s, d = h.shape
    tm, tn, tk = _mm_tiles(s, f, d, tiles)
    nf = f // tn

    def epilogue(accs, ex):
        a_pre, b_pre = accs
        return [jnp.stack([a_pre, b_pre]), _silu(a_pre) * b_pre]

    return _matmul(
        name, "nn", (s // tm, nf, d // tk), h, pl.BlockSpec((tm, tk), lambda i, j, kk: (i, kk)),
        [w_in, w_in], [pl.BlockSpec((tk, tn), lambda i, j, kk: (kk, j)),
                       pl.BlockSpec((tk, tn), lambda i, j, kk: (kk, j + nf))], [], [],
        [jax.ShapeDtypeStruct((2, s, f), BF16), jax.ShapeDtypeStruct((s, f), BF16)],
        [pl.BlockSpec((2, tm, tn), lambda i, j, kk: (0, i, j)), pl.BlockSpec((tm, tn), lambda i, j, kk: (i, j))],
        (tm, tn), epilogue, comm=comm, a_norm=a_norm)


def _mm_residual(name, a, w, x_in, gvec, tiles=FULL_K, comm=None):
    s, k = a.shape
    d = w.shape[1]
    tm, tn, tk = _mm_tiles(s, d, k, tiles)

    def epilogue(accs, ex):
        x_blk, g_row = ex
        return [accs[0], x_blk + g_row * accs[0]]

    tile = pl.BlockSpec((tm, tn), lambda i, j, kk: (i, j))
    return _matmul(
        name, "nn", (s // tm, d // tn, k // tk), a, pl.BlockSpec((tm, tk), lambda i, j, kk: (i, kk)),
        [w], [pl.BlockSpec((tk, tn), lambda i, j, kk: (kk, j))],
        [x_in, gvec], [tile, pl.BlockSpec((1, tn), lambda i, j, kk: (0, j))],
        [jax.ShapeDtypeStruct((s, d), BF16), jax.ShapeDtypeStruct((s, d), F32)], [tile, tile], (tm, tn), epilogue,
        comm=comm)


def _mm_nt(name, a, w, out_dtype, w_col0=0, extra_add=None, tiles=FULL_K):
    s, k = a.shape
    n = w.shape[0]
    tm, tn, tk = _mm_tiles(s, n, k, tiles)
    off = w_col0 // tk
    assert w_col0 % tk == 0
    tile = pl.BlockSpec((tm, tn), lambda i, j, kk: (i, j))
    extras, especs = ([], []) if extra_add is None else ([extra_add], [tile])
    return _matmul(
        name, "nt", (s // tm, n // tn, k // tk), a, pl.BlockSpec((tm, tk), lambda i, j, kk: (i, kk)),
        [w], [pl.BlockSpec((tn, tk), lambda i, j, kk: (j, kk + off))], extras, especs,
        [jax.ShapeDtypeStruct((s, n), out_dtype)], [tile], (tm, tn),
        lambda accs, ex: [accs[0] + ex[0]] if ex else accs)[0]


def _mm_nt_parts(name, a3, w, out_dtype, tiles=LOOP_K, comm=None):
    p, s, k = a3.shape
    n = w.shape[0]
    tm, tn, tk = _mm_tiles(s, n, k, tiles)
    nkp = k // tk
    return _matmul(
        name, "nt", (s // tm, n // tn, p * nkp), a3,
        pl.BlockSpec((None, tm, tk), lambda i, j, kk: (kk // nkp, i, kk % nkp)),
        [w], [pl.BlockSpec((tn, tk), lambda i, j, kk: (j, kk))], [], [],
        [jax.ShapeDtypeStruct((s, n), out_dtype)], [pl.BlockSpec((tm, tn), lambda i, j, kk: (i, j))], (tm, tn),
        lambda accs, ex: accs, comm=comm)[0]


def _mm_nt_swiglu_bwd(name, dy, w_out, ab, tiles):
    s, d = dy.shape
    f = w_out.shape[0]
    tm, tn, tk = _mm_tiles(s, f, d, tiles)

    def epilogue(accs, ex):
        du, a_pre, b_pre = accs[0], ex[0][0].astype(F32), ex[0][1].astype(F32)
        sig = _sigmoid(a_pre)
        da = du * b_pre * (sig * (1.0 + a_pre * (1.0 - sig)))
        db = du * (a_pre * sig)
        return [jnp.stack([da, db])]

    pair = pl.BlockSpec((2, tm, tn), lambda i, j, kk: (0, i, j))
    return _matmul(
        name, "nt", (s // tm, f // tn, d // tk), dy, pl.BlockSpec((tm, tk), lambda i, j, kk: (i, kk)),
        [w_out], [pl.BlockSpec((tn, tk), lambda i, j, kk: (j, kk))], [ab], [pair],
        [jax.ShapeDtypeStruct((2, s, f), BF16)], [pair], (tm, tn), epilogue)[0]


def _mm_tn(name, a, b, out_dtype, a_fn=None, tiles=LOOP_K, comm=None):
    s, m = a.shape
    tm, tk = _tile(m, tiles[0], LANES), _tile(s, tiles[2], 16)
    if b.ndim == 3:
        npart = b.shape[2]
        n = b.shape[0] * npart
        tn = _tile(npart, tiles[1], LANES)
        nj = npart // tn
        b_spec = pl.BlockSpec((None, tk, tn), lambda i, j, kk: (j // nj, kk, j % nj))
    else:
        n = b.shape[1]
        tn = _tile(n, tiles[1], LANES)
        b_spec = pl.BlockSpec((tk, tn), lambda i, j, kk: (kk, j))
    return _matmul(
        name, "tn", (m // tm, n // tn, s // tk), a, pl.BlockSpec((tk, tm), lambda i, j, kk: (kk, i)),
        [b], [b_spec], [], [],
        [jax.ShapeDtypeStruct((m, n), out_dtype)], [pl.BlockSpec((tm, tn), lambda i, j, kk: (i, j))], (tm, tn),
        lambda accs, ex: accs, a_fn=a_fn, comm=comm)[0]


def _row_tile(s):
    return _tile(s, 256, 16)


def _fold_rows(v):
    return jnp.sum(v.reshape(v.shape[0] // SUBLANES, SUBLANES, v.shape[1]), axis=0)


def _accumulate(i, acc_ref, val):
    @pl.when(i == 0)
    def _():
        acc_ref[...] = val

    @pl.when(i > 0)
    def _():
        acc_ref[...] += val


def _rms_bwd(xv, dn_g, d):
    rstd = lax.rsqrt(jnp.mean(xv * xv, axis=-1, keepdims=True) + EPS)
    xhat = xv * rstd
    dx = rstd * (dn_g - xhat * (jnp.sum(dn_g * xhat, axis=-1, keepdims=True) * (1.0 / d)))
    return xhat, dx


def _norm_mod_bwd(name, dh, x, g, scale, dx_in, then=None):
    s, d = x.shape
    tb = _row_tile(s)
    nsteps = s // tb
    coef = None if then is None else then[2]

    def body(*refs):
        dh_ref, x_ref, g_ref, sc_ref, dxin_ref = refs[:5]
        if then is None:
            dx_ref, dsh_ref, dsc_ref, dg_ref, a_sh, a_sc, a_g = refs[5:]
        else:
            y_ref, gv_ref, dx_ref, dsh_ref, dsc_ref, dg_ref, dy_ref, dgate_ref, a_sh, a_sc, a_g, a_gate = refs[5:]
        i = pl.program_id(0)
        dhv, gv = dh_ref[...], g_ref[...]
        dn = dhv * (1.0 + sc_ref[...])
        xhat, dx = _rms_bwd(x_ref[...], dn * gv, d)
        dx_total = dxin_ref[...] + dx
        dx_ref[...] = dx_total
        _accumulate(i, a_sh, _fold_rows(dhv))
        _accumulate(i, a_sc, _fold_rows(dhv * (xhat * gv)))
        _accumulate(i, a_g, _fold_rows(dn * xhat))
        if then is not None:
            dy_ref[...] = (dx_total * gv_ref[...]).astype(BF16)
            _accumulate(i, a_gate, _fold_rows(dx_total * y_ref[...].astype(F32)))

        @pl.when(i == nsteps - 1)
        def _():
            dsh_ref[...] = jnp.sum(a_sh[...], axis=0, keepdims=True)
            dsc_ref[...] = jnp.sum(a_sc[...], axis=0, keepdims=True)
            dg_ref[...] = jnp.sum(a_g[...], axis=0, keepdims=True)
            if then is not None:
                dgate_ref[...] = coef * jnp.sum(a_gate[...], axis=0, keepdims=True)

    row = pl.BlockSpec((tb, d), lambda i: (i, 0))
    vec = pl.BlockSpec((1, d), lambda i: (0, 0))
    vshape = jax.ShapeDtypeStruct((1, d), F32)
    in_specs, operands = [row, row, vec, vec, row], [dh, x, g, scale, dx_in]
    out_specs, out_shape = [row, vec, vec, vec], [jax.ShapeDtypeStruct((s, d), F32), vshape, vshape, vshape]
    n_acc = 3
    if then is not None:
        in_specs, operands = in_specs + [row, vec], operands + [then[0], then[1]]
        out_specs, out_shape = out_specs + [row, vec], out_shape + [jax.ShapeDtypeStruct((s, d), BF16), vshape]
        n_acc = 4
    return pl.pallas_call(
        body, name=name, grid=(nsteps,), in_specs=in_specs, out_specs=out_specs, out_shape=out_shape,
        scratch_shapes=[pltpu.VMEM((SUBLANES, d), F32)] * n_acc, compiler_params=_params(1))(*operands)


def _gate_bwd(name, dx, y, gvec, coef):
    s, d = dx.shape
    tb = _row_tile(s)
    nsteps = s // tb

    def body(dx_ref, y_ref, g_ref, dy_ref, dg_ref, acc):
        i = pl.program_id(0)
        dxv = dx_ref[...]
        dy_ref[...] = (dxv * g_ref[...]).astype(BF16)
        _accumulate(i, acc, _fold_rows(dxv * y_ref[...].astype(F32)))

        @pl.when(i == nsteps - 1)
        def _():
            dg_ref[...] = coef * jnp.sum(acc[...], axis=0, keepdims=True)

    row = pl.BlockSpec((tb, d), lambda i: (i, 0))
    vec = pl.BlockSpec((1, d), lambda i: (0, 0))
    return pl.pallas_call(
        body, name=name, grid=(nsteps,), in_specs=[row, row, vec], out_specs=[row, vec],
        out_shape=[jax.ShapeDtypeStruct((s, d), BF16), jax.ShapeDtypeStruct((1, d), F32)],
        scratch_shapes=[pltpu.VMEM((SUBLANES, d), F32)], compiler_params=_params(1))(dx, y, gvec)


def _final_loss(name, x, target, g):
    s, d = x.shape
    tb = _row_tile(s)
    nsteps = s // tb

    def body(x_ref, t_ref, g_ref, dx_ref, loss_ref, dg_ref, a_loss, a_g):
        i = pl.program_id(0)
        xv, gv = x_ref[...], g_ref[...]
        rstd = lax.rsqrt(jnp.mean(xv * xv, axis=-1, keepdims=True) + EPS)
        xhat = xv * rstd
        err = xhat * gv - t_ref[...]
        dyv = err * (1.0 / d)
        dn_g = dyv * gv
        dx_ref[...] = rstd * (dn_g - xhat * (jnp.sum(dn_g * xhat, axis=-1, keepdims=True) * (1.0 / d)))
        _accumulate(i, a_loss, _fold_rows(err * err))
        _accumulate(i, a_g, _fold_rows(dyv * xhat))

        @pl.when(i == nsteps - 1)
        def _():
            total = jnp.sum(jnp.sum(a_loss[...], axis=0, keepdims=True), axis=1, keepdims=True) * (0.5 / d)
            loss_ref[...] = jnp.broadcast_to(total, (1, LANES))
            dg_ref[...] = jnp.sum(a_g[...], axis=0, keepdims=True)

    row = pl.BlockSpec((tb, d), lambda i: (i, 0))
    vec = pl.BlockSpec((1, d), lambda i: (0, 0))
    return pl.pallas_call(
        body, name=name, grid=(nsteps,), in_specs=[row, row, vec],
        out_specs=[row, pl.BlockSpec((1, LANES), lambda i: (0, 0)), vec],
        out_shape=[jax.ShapeDtypeStruct((s, d), F32), jax.ShapeDtypeStruct((1, LANES), F32),
                   jax.ShapeDtypeStruct((1, d), F32)],
        scratch_shapes=[pltpu.VMEM((SUBLANES, d), F32)] * 2, compiler_params=_params(1))(x, target, g)


def _shift_rows(cur, halo, n):
    rolled = pltpu.roll(cur, n, 0)
    row = lax.broadcasted_iota(jnp.int32, cur.shape, 0)
    for r in range(n):
        rolled = jnp.where(row == r, halo[SUBLANES - n + r:SUBLANES - n + r + 1, :], rolled)
    return rolled


def _shift_rows_up(cur, halo, n):
    tb = cur.shape[0]
    rolled = pltpu.roll(cur, tb - n, 0)
    row = lax.broadcasted_iota(jnp.int32, cur.shape, 0)
    for r in range(n):
        rolled = jnp.where(row == tb - n + r, halo[r:r + 1, :], rolled)
    return rolled


def _conv_specs(s, d, tb, tn):
    nd = d // tn
    hb = tb // SUBLANES
    cur = lambda part: pl.BlockSpec((tb, tn), lambda i, j: (i, j + part * nd))
    prev = lambda part: pl.BlockSpec((SUBLANES, tn), lambda i, j: (jnp.maximum(i * hb - 1, 0), j + part * nd))
    nxt = lambda part: pl.BlockSpec((SUBLANES, tn), lambda i, j: (jnp.minimum((i + 1) * hb, s // SUBLANES - 1),
                                                                  j + part * nd))
    return cur, prev, nxt


def _conv_fwd(name, p, conv_w, conv_b):
    s, d = p.shape[0], p.shape[1] // 3
    tb, tn = _row_tile(s), _tile(d, 2048, LANES)
    cur, prev, _ = _conv_specs(s, d, tb, tn)

    def body(bg_ref, cg_ref, xv_ref, cgp_ref, xvp_ref, w_ref, b_ref, z_ref):
        i = pl.program_id(0)
        u = cg_ref[...] * xv_ref[...]
        up = jnp.where(i > 0, cgp_ref[...] * xvp_ref[...], 0.0)
        w = w_ref[...]
        uc = w[0:1, :] * _shift_rows(u, up, 2) + w[1:2, :] * _shift_rows(u, up, 1) + w[2:3, :] * u + b_ref[...]
        z_ref[...] = (bg_ref[...] * uc).astype(BF16)

    return pl.pallas_call(
        body, name=name, grid=(s // tb, d // tn),
        in_specs=[cur(0), cur(1), cur(2), prev(1), prev(2), pl.BlockSpec((3, tn), lambda i, j: (0, j)),
                  pl.BlockSpec((1, tn), lambda i, j: (0, j))],
        out_specs=pl.BlockSpec((tb, tn), lambda i, j: (i, j)),
        out_shape=jax.ShapeDtypeStruct((s, d), BF16), compiler_params=_params(2))(p, p, p, p, p, conv_w, conv_b)


def _conv_bwd(name, dz, p, conv_w, conv_b):
    s, d = dz.shape
    tb, tn = _row_tile(s), _tile(d, 2048, LANES)
    nsteps = s // tb
    cur, prev, nxt = _conv_specs(s, d, tb, tn)

    def body(dz_ref, bg_ref, cg_ref, xv_ref, cgp_ref, xvp_ref, dzn_ref, bgn_ref, w_ref, b_ref,
             dp_ref, dw_ref, db_ref, a_w0, a_w1, a_w2, a_b):
        j, i = pl.program_id(0), pl.program_id(1)
        cg, xv, bg, dzv = cg_ref[...], xv_ref[...], bg_ref[...], dz_ref[...]
        u = cg * xv
        up = jnp.where(i > 0, cgp_ref[...] * xvp_ref[...], 0.0)
        w = w_ref[...]
        u1, u2 = _shift_rows(u, up, 1), _shift_rows(u, up, 2)
        uc = w[0:1, :] * u2 + w[1:2, :] * u1 + w[2:3, :] * u + b_ref[...]
        duc = dzv * bg
        ducn = jnp.where(i < nsteps - 1, dzn_ref[...] * bgn_ref[...], 0.0)
        du = w[2:3, :] * duc + w[1:2, :] * _shift_rows_up(duc, ducn, 1) + w[0:1, :] * _shift_rows_up(duc, ducn, 2)
        dp_ref[0] = (dzv * uc).astype(BF16)
        dp_ref[1] = (du * xv).astype(BF16)
        dp_ref[2] = (du * cg).astype(BF16)
        _accumulate(i, a_w0, _fold_rows(duc * u2))
        _accumulate(i, a_w1, _fold_rows(duc * u1))
        _accumulate(i, a_w2, _fold_rows(duc * u))
        _accumulate(i, a_b, _fold_rows(duc))

        @pl.when(i == nsteps - 1)
        def _():
            dw_ref[0:1, :] = jnp.sum(a_w0[...], axis=0, keepdims=True)
            dw_ref[1:2, :] = jnp.sum(a_w1[...], axis=0, keepdims=True)
            dw_ref[2:3, :] = jnp.sum(a_w2[...], axis=0, keepdims=True)
            db_ref[...] = jnp.sum(a_b[...], axis=0, keepdims=True)

    swap = lambda spec: pl.BlockSpec(spec.block_shape, lambda j, i, _m=spec.index_map: _m(i, j))
    return pl.pallas_call(
        body, name=name, grid=(d // tn, nsteps),
        in_specs=[pl.BlockSpec((tb, tn), lambda j, i: (i, j)), swap(cur(0)), swap(cur(1)), swap(cur(2)),
                  swap(prev(1)), swap(prev(2)), swap(nxt(0)), swap(nxt(0)),
                  pl.BlockSpec((3, tn), lambda j, i: (0, j)), pl.BlockSpec((1, tn), lambda j, i: (0, j))],
        out_specs=[pl.BlockSpec((3, tb, tn), lambda j, i: (0, i, j)), pl.BlockSpec((3, tn), lambda j, i: (0, j)),
                   pl.BlockSpec((1, tn), lambda j, i: (0, j))],
        out_shape=[jax.ShapeDtypeStruct((3, s, d), BF16), jax.ShapeDtypeStruct((3, d), F32),
                   jax.ShapeDtypeStruct((1, d), F32)],
        scratch_shapes=[pltpu.VMEM((SUBLANES, tn), F32)] * 4, compiler_params=_params(2),
    )(dz, p, p, p, p, p, dz, p, conv_w, conv_b)


def _block_cumsum(v):
    tb = v.shape[0]
    row = lax.broadcasted_iota(jnp.int32, v.shape, 0)
    sh = 1
    while sh < tb:
        v = v + jnp.where(row >= sh, pltpu.roll(v, sh, 0), 0.0)
        sh *= 2
    return v


def _fgate_fwd(name, zf, b_f):
    s = zf.shape[0]
    tb = _tile(s, 512, SUBLANES)

    def body(z_ref, b_ref, f_ref, ft_ref, carry):
        i = pl.program_id(0)
        z = z_ref[...] + b_ref[...]
        ls = jnp.minimum(z, 0.0) - jnp.log(1.0 + jnp.exp(-jnp.abs(z)))
        run = _block_cumsum(ls)

        @pl.when(i == 0)
        def _():
            carry[...] = jnp.zeros_like(carry)

        out = run + carry[...]
        f_ref[...] = out
        ft_ref[...] = jnp.transpose(out)
        carry[...] = out[tb - 1:tb, :]

    blk = pl.BlockSpec((tb, LANES), lambda i: (i, 0))
    return pl.pallas_call(
        body, name=name, grid=(s // tb,), in_specs=[blk, pl.BlockSpec((1, LANES), lambda i: (0, 0))],
        out_specs=[blk, pl.BlockSpec((LANES, tb), lambda i: (0, i))],
        out_shape=[jax.ShapeDtypeStruct((s, LANES), F32), jax.ShapeDtypeStruct((LANES, s), F32)],
        scratch_shapes=[pltpu.VMEM((1, LANES), F32)], compiler_params=_params(1))(zf, b_f)


def _fgate_bwd(name, dfcum, zf, b_f):
    s = zf.shape[0]
    tb = _tile(s, 512, SUBLANES)
    nsteps = s // tb

    def body(df_ref, z_ref, b_ref, dz_ref, db_ref, carry, acc):
        i = pl.program_id(0)
        df = df_ref[...]
        incl = _block_cumsum(df)
        total = incl[tb - 1:tb, :]

        @pl.when(i == 0)
        def _():
            carry[...] = jnp.zeros_like(carry)

        suffix = total - incl + df + carry[...]
        carry[...] += total
        dzv = suffix * jax.nn.sigmoid(-(z_ref[...] + b_ref[...]))
        dz_ref[...] = dzv
        _accumulate(i, acc, _fold_rows(dzv))

        @pl.when(i == nsteps - 1)
        def _():
            db_ref[...] = jnp.sum(acc[...], axis=0, keepdims=True)

    blk = pl.BlockSpec((tb, LANES), lambda i: (nsteps - 1 - i, 0))
    vec = pl.BlockSpec((1, LANES), lambda i: (0, 0))
    return pl.pallas_call(
        body, name=name, grid=(nsteps,), in_specs=[blk, blk, vec], out_specs=[blk, vec],
        out_shape=[jax.ShapeDtypeStruct((s, LANES), F32), jax.ShapeDtypeStruct((1, LANES), F32)],
        scratch_shapes=[pltpu.VMEM((1, LANES), F32), pltpu.VMEM((SUBLANES, LANES), F32)],
        compiler_params=_params(1))(dfcum, zf, b_f)


def _lane_column(block, lane):
    sel = lax.broadcasted_iota(jnp.int32, block.shape, 1) == lane
    return jnp.sum(jnp.where(sel, block, 0.0), axis=1, keepdims=True)


def _col_to_row(col):
    return jnp.transpose(jnp.broadcast_to(col, (col.shape[0], LANES)))[0:1, :]


def _attn_tiles(s):
    t = _tile(s, 1024, LANES)
    return t, t


def _attn_fwd(name, q, kv, fcum_t, n_heads):
    s, d = q.shape
    dh = d // n_heads
    tq, tk = _attn_tiles(s)
    scale2 = LOG2E / math.sqrt(dh)

    def body(q_ref, k_ref, v_ref, ft_ref, o_ref, l_ref):
        qi = pl.program_id(1)
        qv = q_ref[...]

        def step(kj, carry, diagonal):
            m, l, acc = carry
            start = pl.multiple_of(kj * tk, tk)
            kb = k_ref[pl.ds(start, tk), :]
            vb = v_ref[pl.ds(start, tk), :]
            sc = lax.dot_general(qv, kb, _DN["nt"], preferred_element_type=F32) * scale2
            sc = sc - ft_ref[:, pl.ds(start, tk)] * LOG2E
            if diagonal:
                keep = lax.broadcasted_iota(jnp.int32, sc.shape, 0) >= lax.broadcasted_iota(jnp.int32, sc.shape, 1)
                sc = jnp.where(keep, sc, -jnp.inf)
            m_new = jnp.maximum(m, jnp.max(sc, axis=1, keepdims=True))
            alpha = jnp.exp2(m - m_new)
            pr = jnp.exp2(sc - m_new)
            l = alpha * l + jnp.sum(pr, axis=1, keepdims=True)
            acc = alpha * acc + lax.dot_general(pr.astype(BF16), vb, _DN["nn"], preferred_element_type=F32)
            return m_new, l, acc

        init = (jnp.full((tq, 1), -jnp.inf, F32), jnp.zeros((tq, 1), F32), jnp.zeros((tq, dh), F32))
        carry = lax.fori_loop(0, qi, lambda kj, cr: step(kj, cr, False), init)
        m, l, acc = step(qi, carry, True)
        o_ref[...] = (acc / l).astype(BF16)
        l_ref[...] = _col_to_row(m + jnp.log2(l))

    nh = n_heads
    return pl.pallas_call(
        body, name=name, grid=(nh, s // tq),
        in_specs=[pl.BlockSpec((tq, dh), lambda h, i: (i, h)),
                  pl.BlockSpec((s, dh), lambda h, i: (0, h)),
                  pl.BlockSpec((s, dh), lambda h, i: (0, nh + h)),
                  pl.BlockSpec((None, 1, s), lambda h, i: (h, 0, 0))],
        out_specs=[pl.BlockSpec((tq, dh), lambda h, i: (i, h)),
                   pl.BlockSpec((None, 1, tq), lambda h, i: (h, 0, i))],
        out_shape=[jax.ShapeDtypeStruct((s, d), BF16), jax.ShapeDtypeStruct((nh, 1, s), F32)],
        compiler_params=_params(2))(q, kv, kv, fcum_t)


def _attn_delta(name, d_o, o, n_heads):
    s, d = d_o.shape
    dh = d // n_heads
    tb = _row_tile(s)

    def body(do_ref, o_ref, dlt_ref, dob_ref):
        dov = do_ref[...]
        prod = dov * o_ref[...].astype(F32)
        lane = lax.broadcasted_iota(jnp.int32, (tb, LANES), 1)
        out = jnp.zeros((tb, LANES), F32)
        for h in range(n_heads):
            out = jnp.where(lane == h, jnp.sum(prod[:, h * dh:(h + 1) * dh], axis=1, keepdims=True), out)
        dlt_ref[...] = jnp.transpose(out)
        dob_ref[...] = dov.astype(BF16)

    row = pl.BlockSpec((tb, d), lambda i: (i, 0))
    return pl.pallas_call(
        body, name=name, grid=(s // tb,), in_specs=[row, row],
        out_specs=[pl.BlockSpec((LANES, tb), lambda i: (0, i)), row],
        out_shape=[jax.ShapeDtypeStruct((LANES, s), F32), jax.ShapeDtypeStruct((s, d), BF16)],
        compiler_params=_params(1))(d_o, o)


def _attn_bwd(name, q, kv, d_o, fcum, lse_t, delta_t, n_heads):
    s, d = q.shape
    dh = d // n_heads
    tq, tk = _attn_tiles(s)
    nq, nk = s // tq, s // tk
    scale = 1.0 / math.sqrt(dh)
    scale2 = LOG2E * scale

    def body(q_ref, k_ref, v_ref, do_ref, f_ref, lt_ref, dlt_ref, dq_ref, dkv_ref, dfq_ref, dfk_ref, dq_acc, dfq_acc):
        h, kj = pl.program_id(0), pl.program_id(1)
        kb, vb = k_ref[...], v_ref[...]
        fk2 = _lane_column(f_ref[...], h) * LOG2E

        @pl.when(kj == 0)
        def _():
            dq_acc[...] = jnp.zeros_like(dq_acc)
            dfq_acc[...] = jnp.zeros_like(dfq_acc)

        def step(qi, carry, diagonal):
            dk, dv, dfk = carry
            start = pl.multiple_of(qi * tq, tq)
            qb = q_ref[pl.ds(start, tq), :]
            dob = do_ref[pl.ds(start, tq), :]
            sc = lax.dot_general(kb, qb, _DN["nt"], preferred_element_type=F32) * scale2 - fk2
            pr = jnp.exp2(sc - lt_ref[:, pl.ds(start, tq)])
            if diagonal:
                keep = lax.broadcasted_iota(jnp.int32, sc.shape, 1) >= lax.broadcasted_iota(jnp.int32, sc.shape, 0)
                pr = jnp.where(keep, pr, 0.0)
            dv = dv + lax.dot_general(pr.astype(BF16), dob, _DN["nn"], preferred_element_type=F32)
            dp = lax.dot_general(vb, dob, _DN["nt"], preferred_element_type=F32)
            ds = pr * (dp - dlt_ref[:, pl.ds(start, tq)])
            ds_b = ds.astype(BF16)
            dk = dk + lax.dot_general(ds_b, qb, _DN["nn"], preferred_element_type=F32)
            dfk = dfk - jnp.sum(ds, axis=1, keepdims=True)
            dq_acc[pl.ds(start, tq), :] += lax.dot_general(ds_b, kb, _DN["tn"], preferred_element_type=F32)
            dfq_acc[:, pl.ds(start, tq)] += jnp.sum(ds, axis=0, keepdims=True)
            return dk, dv, dfk

        init = (jnp.zeros((tk, dh), F32), jnp.zeros((tk, dh), F32), jnp.zeros((tk, 1), F32))
        carry = step(kj, init, True)
        dk, dv, dfk = lax.fori_loop(kj + 1, nq, lambda qi, cr: step(qi, cr, False), carry)
        dkv_ref[0] = (dk * scale).astype(BF16)
        dkv_ref[1] = dv.astype(BF16)
        dfk_ref[...] = _col_to_row(dfk)

        @pl.when(kj == nk - 1)
        def _():
            dq_ref[...] = (dq_acc[...] * scale).astype(BF16)
            dfq_ref[...] = dfq_acc[...]

    nh = n_heads
    full = pl.BlockSpec((s, dh), lambda h, j: (0, h))
    row_t = pl.BlockSpec((None, 1, s), lambda h, j: (h, 0, 0))
    return pl.pallas_call(
        body, name=name, grid=(nh, nk),
        in_specs=[full, pl.BlockSpec((tk, dh), lambda h, j: (j, h)), pl.BlockSpec((tk, dh), lambda h, j: (j, nh + h)),
                  full, pl.BlockSpec((tk, LANES), lambda h, j: (j, 0)), row_t, row_t],
        out_specs=[full, pl.BlockSpec((2, tk, dh), lambda h, j: (0, j, h)), row_t,
                   pl.BlockSpec((None, 1, tk), lambda h, j: (h, 0, j))],
        out_shape=[jax.ShapeDtypeStruct((s, d), BF16), jax.ShapeDtypeStruct((2, s, d), BF16),
                   jax.ShapeDtypeStruct((nh, 1, s), F32), jax.ShapeDtypeStruct((nh, 1, s), F32)],
        scratch_shapes=[pltpu.VMEM((s, dh), F32), pltpu.VMEM((1, s), F32)],
        compiler_params=_params(2))(q, kv, kv, d_o, fcum, lse_t, delta_t)


def _sum_slots(name, g):
    n, r, lanes = g.shape

    def body(g_ref, o_ref):
        total = g_ref[0]
        for i in range(1, n):
            total = total + g_ref[i]
        o_ref[...] = total

    return pl.pallas_call(
        body, name=name, out_shape=jax.ShapeDtypeStruct((r, lanes), F32),
        in_specs=[pl.BlockSpec(memory_space=pltpu.VMEM)], out_specs=pl.BlockSpec(memory_space=pltpu.VMEM),
        compiler_params=pltpu.CompilerParams(vmem_limit_bytes=VMEM_LIMIT))(g)


def _adamw(name, parts, w, m, v, row0=0, previous=None):
    npart, r, c = parts.shape
    tb = _tile(math.gcd(r, row0) if row0 else r, max(SUBLANES, (2**18 // c) // SUBLANES * SUBLANES), SUBLANES)
    off = row0 // tb
    c1 = 1.0 - ADAM_B1 ** ADAM_STEP
    c2 = 1.0 - ADAM_B2 ** ADAM_STEP

    def body(p_ref, w_ref, m_ref, v_ref, *rest):
        g_out, d_out, m_out, v_out = rest[-4:]
        g = p_ref[0].astype(F32)
        for i in range(1, npart):
            g = g + p_ref[i].astype(F32)
        m_new = ADAM_B1 * m_ref[...] + (1.0 - ADAM_B1) * g
        v_new = ADAM_B2 * v_ref[...] + (1.0 - ADAM_B2) * (g * g)
        m_hat = m_new / c1
        v_hat = v_new / c2
        g_out[...] = g
        d_out[...] = -ADAM_LR * (m_hat / (jnp.sqrt(v_hat) + ADAM_EPS) + ADAM_WD * w_ref[...])
        m_out[...] = m_new
        v_out[...] = v_new

    blk = pl.BlockSpec((tb, c), lambda i: (i + off, 0))
    shape = jax.ShapeDtypeStruct(w.shape, F32)
    in_specs = [pl.BlockSpec((npart, tb, c), lambda i: (0, i, 0)), blk, blk, blk]
    operands = [parts, w, m, v]
    aliases = {}
    if previous is not None:
        in_specs += [pl.BlockSpec(memory_space=pl.ANY)] * 4
        operands += list(previous)
        aliases = {4 + k: k for k in range(4)}
    return pl.pallas_call(
        body, name=name, grid=(r // tb,), in_specs=in_specs, out_specs=[blk] * 4, out_shape=[shape] * 4,
        input_output_aliases=aliases, compiler_params=_params(1))(*operands)


def _pack(vectors):
    flat = jnp.concatenate([v.reshape(-1).astype(F32) for v in vectors])
    pad = (-flat.shape[0]) % (SUBLANES * LANES)
    return jnp.pad(flat, (0, pad)).reshape(-1, LANES)


def _unpack(flat, shapes):
    out, pos = [], 0
    for shp in shapes:
        size = math.prod(shp)
        out.append(flat[..., pos:pos + size].reshape(flat.shape[:-1] + tuple(shp)))
        pos += size
    return out


def _unshard_last(g, lead):
    nd = len(lead)
    return jnp.moveaxis(g, 0, nd).reshape(tuple(lead) + (-1,))


def _my_slice(full, me, width, axis):
    return lax.dynamic_slice_in_dim(full, me * width, width, axis)


def kernel(x, c, norm_g, w_ada, b_ada, w_ffn_in, w_ffn_out, w_conv_in, conv_w, conv_b, w_conv_out, kv_norm_g, w_ada_kv, b_ada_kv, w_kvf, b_fgate, w_q, w_o, final_g, loss_target, m_norm_g, m_w_ada, m_b_ada, m_w_ffn_in, m_w_ffn_out, m_w_conv_in, m_conv_w, m_conv_b, m_w_conv_out, m_kv_norm_g, m_w_ada_kv, m_b_ada_kv, m_w_kvf, m_b_fgate, m_w_q, m_w_o, m_final_g, v_norm_g, v_w_ada, v_b_ada, v_w_ffn_in, v_w_ffn_out, v_w_conv_in, v_conv_w, v_conv_b, v_w_conv_out, v_kv_norm_g, v_w_ada_kv, v_b_ada_kv, v_w_kvf, v_b_fgate, v_w_q, v_w_o, v_final_g):
    s, d = x.shape[1], x.shape[2]
    n_heads = b_fgate.shape[0]
    n_layers = w_ada.shape[0]
    dsh = d // N_DEV
    f = w_ffn_out.shape[2] * N_DEV
    ada_w = w_ada.shape[2]
    kv_w = w_ada_kv.shape[1]
    kvf_w = w_kvf.shape[1]
    assert n_layers == 2 and w_conv_in.shape[0] == 1 and w_q.shape[0] == 1
    assert (d // n_heads) % LANES == 0 and n_heads <= LANES and N_DEV * kvf_w == 2 * d + n_heads
    me = 4 * lax.axis_index("x") + 2 * lax.axis_index("y") + lax.axis_index("c")
    x0, target = x[0], loss_target[0]

    small = _allgather_small("gather_small_params", _pack([c, norm_g, conv_w, conv_b]))
    c_all, ng_sh, cw_sh, cb_sh = _unpack(small.reshape(N_DEV, -1), [(d,), (n_layers, 3, dsh), (3, dsh), (dsh,)])
    norm_g_full = _unshard_last(ng_sh, (n_layers, 3))
    conv_w_full = _unshard_last(cw_sh, (3,))
    conv_b_full = _unshard_last(cb_sh, ()).reshape(1, d)
    c_rows = jnp.pad(c_all, ((0, 16 - N_DEV), (0, 0)))

    ada_cols = [
        _mm_bias_rows(f"ada_rows_{l}", c_rows, w_ada.reshape(n_layers * d, ada_w),
                      _my_slice(b_ada[l], me, ada_w, 0).reshape(1, ada_w), _silu, b_block=l)
        for l in range(n_layers)]
    ada_cols.append(_mm_bias_rows("ada_rows_kv", c_rows, w_ada_kv, _my_slice(b_ada_kv, me, kv_w, 0).reshape(1, kv_w),
                                  _silu))
    ada_part = jnp.concatenate([a[:N_DEV] for a in ada_cols], axis=1)
    ada_all = _allgather_small("gather_ada_rows", _pack([ada_part]))
    ada_all = ada_all.reshape(N_DEV, -1)[:, :ada_part.size].reshape(N_DEV, N_DEV, -1)
    ada_mine = lax.dynamic_index_in_dim(ada_all, me, axis=1, keepdims=False)
    ada = [ada_mine[:, l * ada_w:(l + 1) * ada_w].reshape(3, 3, 1, d) for l in range(n_layers)]
    ada_kv = ada_mine[:, n_layers * ada_w:].reshape(2, 1, d)

    def as_item(w, kind):
        return (w.astype(BF16), kind)

    gather_first = _GatherComm([as_item(w_ffn_in[0, 0], "cols")])
    gather_l0b = _GatherComm([as_item(w_ffn_out[0, 0], "rows"), as_item(w_ffn_in[0, 1], "cols")])
    gather_conv = _GatherComm([as_item(w_conv_in[0], "cols"), as_item(w_conv_out[0], "rows")])
    gather_l0b_out = _GatherComm([as_item(w_ffn_out[0, 1], "rows")])
    gather_l1a = _GatherComm([as_item(w_ffn_out[1, 0], "rows"), as_item(w_ffn_in[1, 0], "cols")])
    gather_attn = _GatherComm([as_item(w_q[0], "rows"), as_item(w_kvf, "lead"), as_item(w_o[0], "rows")])
    gather_l1b = _GatherComm([as_item(w_ffn_out[1, 1], "rows"), as_item(w_ffn_in[1, 1], "cols")])
    (wf_in00,) = _run_comm("gather_ffn_first", gather_first)
    b_f = jnp.pad(b_fgate, (0, LANES - n_heads)).reshape(1, LANES)

    def ffn_fwd(tag, xs, l, sub, w_in, w_out, comm_in, comm_out):
        shift, scale, gate = ada[l][sub]
        tiles = FFN_TILES
        ab, u, h = _mm_swiglu(f"ffn_in_{tag}", xs, w_in, f, tiles["in"], comm=comm_in,
                              a_norm=(norm_g_full[l, sub].reshape(1, d), shift, scale))
        y, x_next = _mm_residual(f"ffn_out_{tag}", u, w_out, xs, 0.5 * gate, tiles["out"], comm=comm_out)
        return x_next, (xs, h, ab, u, y, w_in, w_out)

    ab, u, h = _mm_swiglu("ffn_in_l0a", x0, wf_in00, f, FFN_TILES["in"], comm=gather_l0b,
                          a_norm=(norm_g_full[0, 0].reshape(1, d), ada[0][0][0], ada[0][0][1]))
    wf_out00, wf_in01 = gather_l0b.results
    y, x1 = _mm_residual("ffn_out_l0a", u, wf_out00, x0, 0.5 * ada[0][0][2], FFN_TILES["out"], comm=gather_conv)
    save_f0 = (x0, h, ab, u, y, wf_in00, wf_out00)
    wc_in, wc_out = gather_conv.results
    shift, scale, gate = ada[0][1]
    p_c, h_c = _mm_nn("conv_in", x1, wc_in, F32, comm=gather_l0b_out,
                      a_norm=(norm_g_full[0, 1].reshape(1, d), shift, scale))
    (wf_out01,) = gather_l0b_out.results
    z_c = _conv_fwd("conv_mix", p_c, conv_w_full, conv_b_full)
    y_c, x2 = _mm_residual("conv_out", z_c, wc_out, x1, gate)
    x3, save_f1 = ffn_fwd("l0b", x2, 0, 2, wf_in01, wf_out01, gather_l1a, gather_attn)
    wf_out10, wf_in10 = gather_l1a.results
    wq, g_kvf, wo = gather_attn.results
    wkvf = _unshard_last(g_kvf, (d,))
    wkv = wkvf[:, :2 * d]
    wf_gate = jnp.pad(wkvf[:, 2 * d:], ((0, 0), (0, LANES - n_heads)))
    x4, save_f2 = ffn_fwd("l1a", x3, 1, 0, wf_in10, wf_out10, gather_l1b, None)
    wf_out11, wf_in11 = gather_l1b.results
    shift, scale, gate_a = ada[1][1]
    q, h_q = _mm_nn("attn_q", x4, wq, BF16, a_norm=(norm_g_full[1, 1].reshape(1, d), shift, scale))
    kv, h_kv = _mm_nn("attn_kv", x4, wkv, BF16, a_norm=(kv_norm_g.reshape(1, d), ada_kv[0], ada_kv[1]))
    zf = _mm_nn("attn_fgate", h_kv, wf_gate, F32)
    fcum, fcum_lanes = _fgate_fwd("fgate_cumsum", zf, b_f)
    fcum_t = fcum_lanes[:n_heads].reshape(n_heads, 1, s)
    o, lse_t = _attn_fwd("attn_fwd", q, kv, fcum_t, n_heads)
    y_a, x5 = _mm_residual("attn_out", o, wo, x4, gate_a)
    x6, save_f3 = ffn_fwd("l1b", x5, 1, 2, wf_in11, wf_out11, None, None)
    dx, loss_part, d_final_g = _final_loss("final_loss", x6, target, final_g.reshape(1, d))

    d_ada = [[[None] * 3 for _ in range(3)] for _ in range(n_layers)]
    d_norm_g = [[None] * 3 for _ in range(n_layers)]
    core = lax.axis_index("c").astype(jnp.int32).reshape(1)
    in_shard, out_shard, row_shard = (d, 2 * f // N_DEV), (f // N_DEV, d), (dsh, d)

    def chip_exchange(tag, items, from_sibling):
        return _ChipComm([_pair_add(f"add_sibling_{tag}_{n}", p, kind, shape, t, core)
                          for n, ((p, kind, shape), t) in enumerate(zip(items, from_sibling))])

    def reduce_within_chip(tag, items):
        return chip_exchange(tag, items, _run_comm(f"reduce_sibling_{tag}", _SiblingComm(items)))

    def ffn_bwd(tag, dxs, dy, saved, l, sub, comm_dw, then):
        xs, h, ab, u, y, w_in, w_out = saved
        tiles = FFN_TILES
        dab = _mm_nt_swiglu_bwd(f"ffn_out_bwd_{tag}", dy, w_out, ab, tiles["out_bwd"])
        dw_in = _mm_tn(f"ffn_in_dw_{tag}", h, dab, BF16, tiles=tiles["in_dw"])
        sibling_in = _SiblingComm([(dw_in, "cols", in_shard)])
        dw_out = _mm_tn(f"ffn_out_dw_{tag}", u, dy, BF16, tiles=tiles["out_dw"], comm=_join(sibling_in, comm_dw))
        sibling_out = _run_comm(f"reduce_sibling_{tag}", _SiblingComm([(dw_out, "rows", out_shard)]))
        scatter = chip_exchange(tag, [(dw_in, "cols", in_shard), (dw_out, "rows", out_shard)],
                                list(sibling_in.results) + list(sibling_out))
        dh = _mm_nt_parts(f"ffn_in_bwd_{tag}", dab, w_in, F32, tiles["in_bwd"], comm=scatter)
        outs = _norm_mod_bwd(f"norm_bwd_{tag}", dh, xs, norm_g_full[l, sub].reshape(1, d), ada[l][sub][1], dxs, then)
        dxs, d_ada[l][sub][0], d_ada[l][sub][1], d_norm_g[l][sub] = outs[:4]
        return dxs, outs[4:], scatter

    def gate_of(saved, l, sub):
        return (saved[4], 0.5 * ada[l][sub][2], 0.5)

    dy, d_ada[1][2][2] = _gate_bwd("gate_bwd_l1b", dx, save_f3[4], 0.5 * ada[1][2][2], 0.5)
    dx, (dy, d_ada[1][1][2]), scatter_l1b = ffn_bwd("l1b", dx, dy, save_f3, 1, 2, None, (y_a, gate_a, 1.0))
    d_o = _mm_nt("attn_out_bwd", dy, wo, F32)
    dwo = _mm_tn("attn_out_dw", o, dy, BF16)
    delta_lanes, d_ob = _attn_delta("attn_delta", d_o, o, n_heads)
    delta_t = delta_lanes[:n_heads].reshape(n_heads, 1, s)
    dq, dkv, dfq, dfk = _attn_bwd("attn_bwd", q, kv, d_ob, fcum, lse_t, delta_t, n_heads)
    dfcum = jnp.pad((dfq + dfk).reshape(n_heads, s).T, ((0, 0), (0, LANES - n_heads)))
    dzf, d_bf = _fgate_bwd("fgate_bwd", dfcum, zf, b_f)
    dh_q = _mm_nt("attn_q_bwd", dq, wq, F32)
    dwq = _mm_tn("attn_q_dw", h_q, dq, BF16)
    dh_kv = _mm_nt_parts("attn_kv_bwd", dkv, wkv, F32)
    dh_kv = _mm_nt("attn_fgate_bwd", dzf, wf_gate, F32, extra_add=dh_kv)
    dwkv = _mm_tn("attn_kv_dw", h_kv, dkv, BF16)
    dwf = _mm_tn("attn_fgate_dw", h_kv, dzf, BF16)
    dwkvf = jnp.concatenate([dwkv, dwf[:, :n_heads]], axis=1)
    p_kvf = jnp.moveaxis(dwkvf.reshape(d, N_DEV, kvf_w), 1, 0)
    scatter_attn = reduce_within_chip(
        "attn", [(dwo, "rows", row_shard), (dwq, "rows", row_shard), (p_kvf, "lead", (d, kvf_w))])
    dx, d_ada[1][1][0], d_ada[1][1][1], d_norm_g[1][1] = _norm_mod_bwd(
        "norm_bwd_q", dh_q, x4, norm_g_full[1, 1].reshape(1, d), ada[1][1][1], dx)
    dx, d_kv_shift, d_kv_scale, d_kv_norm_g, dy, d_ada[1][0][2] = _norm_mod_bwd(
        "norm_bwd_kv", dh_kv, x4, kv_norm_g.reshape(1, d), ada_kv[1], dx, gate_of(save_f2, 1, 0))
    dx, (dy, d_ada[0][2][2]), scatter_l1a = ffn_bwd("l1a", dx, dy, save_f2, 1, 0, scatter_attn, gate_of(save_f1, 0, 2))
    dx, (dy, d_ada[0][1][2]), scatter_l0b = ffn_bwd("l0b", dx, dy, save_f1, 0, 2, None, (y_c, ada[0][1][2], 1.0))
    dz = _mm_nt("conv_out_bwd", dy, wc_out, F32)
    dwc_out = _mm_tn("conv_out_dw", z_c, dy, BF16)
    dp, d_conv_w, d_conv_b = _conv_bwd("conv_mix_bwd", dz, p_c, conv_w_full, conv_b_full)
    dh_c = _mm_nt_parts("conv_in_bwd", dp, wc_in, F32)
    dwc_in = _mm_tn("conv_in_dw", h_c, dp, BF16)
    scatter_conv = reduce_within_chip("conv", [(dwc_out, "rows", row_shard), (dwc_in, "cols", (d, 3 * d // N_DEV))])
    dx, d_ada[0][1][0], d_ada[0][1][1], d_norm_g[0][1], dy, d_ada[0][0][2] = _norm_mod_bwd(
        "norm_bwd_conv", dh_c, x1, norm_g_full[0, 1].reshape(1, d), ada[0][1][1], dx, gate_of(save_f0, 0, 0))
    dx, _, scatter_l0a = ffn_bwd("l0a", dx, dy, save_f0, 0, 0, scatter_conv, None)
    grad_x = dx.reshape(1, s, d)

    d_ada_flat = jnp.concatenate([v.reshape(-1) for l in range(n_layers) for sub in range(3) for v in d_ada[l][sub]])
    d_norm_flat = jnp.concatenate([d_norm_g[l][sub].reshape(-1) for l in range(n_layers) for sub in range(3)])
    small_shapes = [(n_layers * 9 * d,), (2 * d,), (n_layers * 3 * d,), (3, d), (d,), (d,), (LANES,), (d,), (LANES,)]
    small_parts = _pack([d_ada_flat, d_kv_shift, d_kv_scale, d_norm_flat, d_conv_w, d_conv_b, d_kv_norm_g, d_bf,
                         d_final_g, loss_part])
    small_all = _allgather_small("gather_small_grads", small_parts)
    small_sum = _sum_slots("sum_small_grads", small_all).reshape(-1)
    g_b_ada, g_b_ada_kv, g_norm_full, g_conv_w_full, g_conv_b_full, g_kv_norm, g_bf, g_final, loss_v = _unpack(
        small_sum, small_shapes)
    loss = loss_v[0]
    d_ada_rows = small_all.reshape(N_DEV, -1)[:, :n_layers * 9 * d + 2 * d]
    d_ada_rows = jnp.pad(d_ada_rows, ((0, 16 - N_DEV), (0, 0)))

    results = {}

    def update(key, parts, w, m, v):
        shape = w.shape
        c_dim = shape[-1]
        outs = _adamw(f"adamw_{key}", parts.reshape(parts.shape[0], -1, c_dim), w.reshape(-1, c_dim),
                      m.reshape(-1, c_dim), v.reshape(-1, c_dim))
        results[key] = [o.reshape(shape) for o in outs]

    def update_rows(key, parts_list, w, m, v):
        c_dim = w.shape[-1]
        flat = [t.reshape(-1, c_dim) for t in (w, m, v)]
        outs = None
        for n, parts in enumerate(parts_list):
            outs = _adamw(f"adamw_{key}_{n}", parts, *flat, row0=n * parts.shape[1], previous=outs)
        results[key] = [o.reshape(w.shape) for o in outs]

    g_w_ada = []
    for l in range(n_layers):
        cols = _my_slice(d_ada_rows[:, l * 9 * d:(l + 1) * 9 * d], me, ada_w, 1)
        g_w_ada.append(_mm_tn(f"ada_dw_{l}", c_rows, cols, F32, a_fn=_silu)[None])
    update_rows("w_ada", g_w_ada, w_ada, m_w_ada, v_w_ada)
    cols = _my_slice(d_ada_rows[:, n_layers * 9 * d:], me, kv_w, 1)
    update("w_ada_kv", _mm_tn("ada_dw_kv", c_rows, cols, F32, a_fn=_silu).reshape(1, d, kv_w),
           w_ada_kv, m_w_ada_kv, v_w_ada_kv)


    ffn_scatters = [scatter_l0a, scatter_l0b, scatter_l1a, scatter_l1b]
    update_rows("w_ffn_in", [sc.results[0] for sc in ffn_scatters], w_ffn_in, m_w_ffn_in, v_w_ffn_in)
    update_rows("w_ffn_out", [sc.results[1] for sc in ffn_scatters], w_ffn_out, m_w_ffn_out, v_w_ffn_out)
    r_co, r_ci = scatter_conv.results
    r_o, r_q, r_kvf = scatter_attn.results
    update_rows("w_conv_in", [r_ci], w_conv_in, m_w_conv_in, v_w_conv_in)
    update_rows("w_conv_out", [r_co], w_conv_out, m_w_conv_out, v_w_conv_out)
    update_rows("w_kvf", [r_kvf], w_kvf, m_w_kvf, v_w_kvf)
    update_rows("w_q", [r_q], w_q, m_w_q, v_w_q)
    update_rows("w_o", [r_o], w_o, m_w_o, v_w_o)

    small_keys = ["norm_g", "b_ada", "conv_w", "conv_b", "kv_norm_g", "b_ada_kv", "b_fgate", "final_g"]
    small_w = [norm_g, b_ada, conv_w, conv_b, kv_norm_g, b_ada_kv, b_fgate, final_g]
    small_m = [m_norm_g, m_b_ada, m_conv_w, m_conv_b, m_kv_norm_g, m_b_ada_kv, m_b_fgate, m_final_g]
    small_v = [v_norm_g, v_b_ada, v_conv_w, v_conv_b, v_kv_norm_g, v_b_ada_kv, v_b_fgate, v_final_g]
    small_g = [
        _my_slice(g_norm_full.reshape(n_layers, 3, d), me, dsh, 2), g_b_ada.reshape(b_ada.shape),
        _my_slice(g_conv_w_full, me, dsh, 1).reshape(conv_w.shape), _my_slice(g_conv_b_full, me, dsh, 0).reshape(
            conv_b.shape), g_kv_norm, g_b_ada_kv, g_bf[:n_heads], g_final]
    packed = _adamw("adamw_small", _pack(small_g)[None], _pack(small_w), _pack(small_m), _pack(small_v))
    for vals, idx in zip(packed, range(4)):
        for key, val in zip(small_keys, _unpack(vals.reshape(-1), [w.shape for w in small_w])):
            results.setdefault(key, [None] * 4)[idx] = val

    order = ["norm_g", "w_ada", "b_ada", "w_ffn_in", "w_ffn_out", "w_conv_in", "conv_w", "conv_b", "w_conv_out",
             "kv_norm_g", "w_ada_kv", "b_ada_kv", "w_kvf", "b_fgate", "w_q", "w_o", "final_g"]
    return (loss, grad_x, *[results[k][0] for k in order], *[results[k][1] for k in order],
            *[results[k][2] for k in order], *[results[k][3] for k in order])
```

```python
import functools
import math

import jax
import jax.numpy as jnp
from jax import lax
from jax.experimental import pallas as pl
from jax.experimental.pallas import tpu as pltpu

F32 = jnp.float32
BF16 = jnp.bfloat16
MESH = pl.DeviceIdType.MESH

N_DEV = 8
LANES = 128
SUBLANES = 8
VMEM_LIMIT = 56 * 2**20
EPS = 1e-6
LOG2E = math.log2(math.e)
ADAM_LR, ADAM_B1, ADAM_B2, ADAM_EPS, ADAM_WD, ADAM_STEP = 0.001, 0.9, 0.999, 1e-08, 0.01, 10

_DN = {"nn": (((1,), (0,)), ((), ())), "nt": (((1,), (1,)), ((), ())), "tn": (((0,), (0,)), ((), ()))}


def _tile(n, pref, align):
    t = min(pref, n)
    t -= t % align
    while t >= align:
        if n % t == 0:
            return t
        t -= align
    return n


def _params(n_grid):
    return pltpu.CompilerParams(vmem_limit_bytes=VMEM_LIMIT, dimension_semantics=("arbitrary",) * n_grid)


def _sigmoid(v):
    return 0.5 * jnp.tanh(0.5 * v) + 0.5


def _silu(v):
    return v * _sigmoid(v)


def _position():
    x, y, c = lax.axis_index("x"), lax.axis_index("y"), lax.axis_index("c")
    return x, y, c


def _allgather_small(name, v):
    def body(v_ref, out_ref, send_sems, recv_sems, local_sem):
        x, y, c = _position()
        me, sibling = (x, y, c), (x, y, 1 - c)
        chips = [(1 - x, y), (x, 1 - y), (1 - x, 1 - y)]

        def slot(px, py, pc):
            return out_ref.at[4 * px + 2 * py + pc]

        def copy(k, block, to, src=None):
            return pltpu.make_async_remote_copy(
                src_ref=slot(*block) if src is None else src, dst_ref=slot(*block),
                send_sem=send_sems.at[k], recv_sem=recv_sems.at[k], device_id=to, device_id_type=MESH)

        mine = pltpu.make_async_copy(v_ref, slot(*me), local_sem)
        mine.start()
        first = [copy(0, me, sibling, src=v_ref)]
        first += [copy(1 + j, me, (*chip, c), src=v_ref) for j, chip in enumerate(chips)]
        for cp in first:
            cp.start()
        passed = [copy(4 + j, (*chip, c), sibling) for j, chip in enumerate(chips)]
        for j, chip in enumerate(chips):
            copy(1 + j, (*chip, c), me).wait_recv()
            passed[j].start()
        copy(0, sibling, me).wait_recv()
        for j, chip in enumerate(chips):
            copy(4 + j, (*chip, 1 - c), me).wait_recv()
        for cp in first + passed:
            cp.wait_send()
        mine.wait()

    return pl.pallas_call(
        body, name=name,
        out_shape=jax.ShapeDtypeStruct((N_DEV,) + v.shape, v.dtype),
        in_specs=[pl.BlockSpec(memory_space=pltpu.VMEM)],
        out_specs=pl.BlockSpec(memory_space=pltpu.VMEM),
        scratch_shapes=[pltpu.SemaphoreType.DMA((7,)), pltpu.SemaphoreType.DMA((7,)), pltpu.SemaphoreType.DMA],
        compiler_params=pltpu.CompilerParams(vmem_limit_bytes=VMEM_LIMIT),
    )(v)


def _owner_view(ref, kind, shard_shape, owner):
    r, w = shard_shape
    if kind == "rows":
        return ref.at[pl.ds(pl.multiple_of(owner * r, 16), r)]
    if kind == "cols":
        return ref.at[:, pl.ds(pl.multiple_of(owner * w, LANES), w)]
    return ref.at[owner]


def _full_shape(kind, shard_shape):
    r, w = shard_shape
    return {"rows": (N_DEV * r, w), "cols": (r, N_DEV * w), "lead": (N_DEV, r, w)}[kind]


class _GatherComm:
    mid_fraction = 0.85

    def __init__(self, items):
        self.ins = [shard for shard, _ in items]
        self.kinds = [kind for _, kind in items]
        n = len(items)
        self.out_shapes = [jax.ShapeDtypeStruct(_full_shape(k, a.shape), a.dtype) for a, k in items]
        self.sem_shapes = [pltpu.SemaphoreType.DMA((7 * n,)), pltpu.SemaphoreType.DMA((7 * n,)),
                           pltpu.SemaphoreType.DMA((n,))]
        self.results = None

    def _copy(self, refs, a, k, block, to, from_input=False):
        ins, outs, (send_sems, recv_sems, _) = refs
        px, py, pc = block
        dst = _owner_view(outs[a], self.kinds[a], self.ins[a].shape, 4 * px + 2 * py + pc)
        return pltpu.make_async_remote_copy(
            src_ref=ins[a] if from_input else dst, dst_ref=dst,
            send_sem=send_sems.at[7 * a + k], recv_sem=recv_sems.at[7 * a + k], device_id=to, device_id_type=MESH)

    def _local(self, refs, a):
        ins, outs, (_, _, local_sems) = refs
        x, y, c = _position()
        dst = _owner_view(outs[a], self.kinds[a], self.ins[a].shape, 4 * x + 2 * y + c)
        return pltpu.make_async_copy(ins[a], dst, local_sems.at[a])

    def start(self, *refs):
        x, y, c = _position()
        me, sibling = (x, y, c), (x, y, 1 - c)
        chips = [(1 - x, y), (x, 1 - y), (1 - x, 1 - y)]
        for a in range(len(self.ins)):
            self._local(refs, a).start()
            for j, chip in enumerate(chips):
                self._copy(refs, a, 1 + j, me, (*chip, c), from_input=True).start()
            self._copy(refs, a, 0, me, sibling, from_input=True).start()

    def mid(self, *refs):
        x, y, c = _position()
        chips = [(1 - x, y), (x, 1 - y), (1 - x, 1 - y)]
        for a in range(len(self.ins)):
            for j, chip in enumerate(chips):
                self._copy(refs, a, 1 + j, (*chip, c), (x, y, c)).wait_recv()
                self._copy(refs, a, 4 + j, (*chip, c), (x, y, 1 - c)).start()

    def finish(self, *refs):
        x, y, c = _position()
        me, sibling = (x, y, c), (x, y, 1 - c)
        chips = [(1 - x, y), (x, 1 - y), (1 - x, 1 - y)]
        for a in range(len(self.ins)):
            self._copy(refs, a, 0, sibling, me).wait_recv()
            for j, chip in enumerate(chips):
                self._copy(refs, a, 4 + j, (*chip, 1 - c), me).wait_recv()
        for a in range(len(self.ins)):
            self._copy(refs, a, 0, me, sibling, from_input=True).wait_send()
            for j, chip in enumerate(chips):
                self._copy(refs, a, 1 + j, me, (*chip, c), from_input=True).wait_send()
                self._copy(refs, a, 4 + j, (*chip, c), sibling).wait_send()
            self._local(refs, a).wait()


class _SiblingComm:
    mid_fraction = None

    def __init__(self, items):
        self.ins = [p for p, _, _ in items]
        self.kinds = [kind for _, kind, _ in items]
        self.shapes = [shape for _, _, shape in items]
        n = len(items)
        self.out_shapes = [jax.ShapeDtypeStruct((4,) + tuple(shape), p.dtype) for p, _, shape in items]
        self.sem_shapes = [pltpu.SemaphoreType.DMA((4 * n,)), pltpu.SemaphoreType.DMA((4 * n,))]
        self.results = None

    def _copies(self, refs):
        ins, outs, (send_sems, recv_sems) = refs
        x, y, c = _position()
        return [pltpu.make_async_remote_copy(
            src_ref=_owner_view(ins[a], self.kinds[a], self.shapes[a], 2 * k + 1 - c), dst_ref=outs[a].at[k],
            send_sem=send_sems.at[4 * a + k], recv_sem=recv_sems.at[4 * a + k],
            device_id=(x, y, 1 - c), device_id_type=MESH) for a in range(len(self.ins)) for k in range(4)]

    def start(self, *refs):
        for cp in self._copies(refs):
            cp.start()

    def finish(self, *refs):
        for cp in self._copies(refs):
            cp.wait()


class _ChipComm:
    mid_fraction = None

    def __init__(self, items):
        self.ins = list(items)
        n = len(items)
        self.out_shapes = [jax.ShapeDtypeStruct(p.shape, p.dtype) for p in items]
        self.sem_shapes = [pltpu.SemaphoreType.DMA((3 * n,)), pltpu.SemaphoreType.DMA((3 * n,)),
                           pltpu.SemaphoreType.DMA((n,))]
        self.results = None

    def _copy(self, refs, a, k, landed):
        ins, outs, (send_sems, recv_sems, _) = refs
        x, y, c = _position()
        px = 1 - x if k & 2 else x
        py = 1 - y if k & 1 else y
        to_chip, my_chip = 2 * px + py, 2 * x + y
        return pltpu.make_async_remote_copy(
            src_ref=ins[a].at[to_chip], dst_ref=outs[a].at[to_chip if landed else my_chip],
            send_sem=send_sems.at[3 * a + k - 1], recv_sem=recv_sems.at[3 * a + k - 1],
            device_id=(px, py, c), device_id_type=MESH)

    def _local(self, refs, a):
        ins, outs, (_, _, local_sems) = refs
        x, y, _ = _position()
        return pltpu.make_async_copy(ins[a].at[2 * x + y], outs[a].at[2 * x + y], local_sems.at[a])

    def start(self, *refs):
        for a in range(len(self.ins)):
            self._local(refs, a).start()
            for k in range(1, 4):
                self._copy(refs, a, k, False).start()

    def finish(self, *refs):
        for a in range(len(self.ins)):
            for k in range(1, 4):
                self._copy(refs, a, k, True).wait_recv()
        for a in range(len(self.ins)):
            for k in range(1, 4):
                self._copy(refs, a, k, False).wait_send()
            self._local(refs, a).wait()


class _Joined:
    def __init__(self, comms):
        self.comms = list(comms)
        self.ins = [a for c in self.comms for a in c.ins]
        self.out_shapes = [o for c in self.comms for o in c.out_shapes]
        self.sem_shapes = [o for c in self.comms for o in c.sem_shapes]
        fractions = [c.mid_fraction for c in self.comms if c.mid_fraction is not None]
        self.mid_fraction = max(fractions) if fractions else None

    def _parts(self, refs):
        ins, outs, sems = refs
        pos = [0, 0, 0]
        for c in self.comms:
            n = (len(c.ins), len(c.out_shapes), len(c.sem_shapes))
            yield c, (ins[pos[0]:pos[0] + n[0]], outs[pos[1]:pos[1] + n[1]], sems[pos[2]:pos[2] + n[2]])
            pos = [p + k for p, k in zip(pos, n)]

    def start(self, *refs):
        for c, part in self._parts(refs):
            c.start(*part)

    def mid(self, *refs):
        for c, part in self._parts(refs):
            if c.mid_fraction is not None:
                c.mid(*part)

    def finish(self, *refs):
        for c, part in self._parts(refs):
            c.finish(*part)

    @property
    def results(self):
        return [r for c in self.comms for r in c.results]

    @results.setter
    def results(self, values):
        pos = 0
        for c in self.comms:
            c.results = values[pos:pos + len(c.out_shapes)]
            pos += len(c.out_shapes)


def _join(*comms):
    present = [c for c in comms if c is not None]
    return present[0] if len(present) == 1 else (_Joined(present) if present else None)


def _run_comm(name, comm):
    n_in, n_out = len(comm.ins), len(comm.out_shapes)

    def body(*refs):
        parts = (refs[:n_in], refs[n_in:n_in + n_out], refs[n_in + n_out:])
        comm.start(*parts)
        if comm.mid_fraction is not None:
            comm.mid(*parts)
        comm.finish(*parts)

    any_spec = pl.BlockSpec(memory_space=pl.ANY)
    return pl.pallas_call(
        body, name=name, out_shape=list(comm.out_shapes), in_specs=[any_spec] * n_in, out_specs=[any_spec] * n_out,
        scratch_shapes=list(comm.sem_shapes))(*comm.ins)


def _carry(comm, body, grid, in_specs, out_specs, out_shapes, scratch_shapes, operands):
    if comm is None:
        return body, in_specs, out_specs, out_shapes, scratch_shapes, operands
    n_in, n_out, n_scr = len(in_specs), len(out_specs), len(scratch_shapes)
    c_in, c_out = len(comm.ins), len(comm.out_shapes)
    total = math.prod(grid)
    mid_step = None if comm.mid_fraction is None else min(total - 1, int(total * comm.mid_fraction))

    def wrapped(*refs):
        ins, c_ins = refs[:n_in], refs[n_in:n_in + c_in]
        outs = refs[n_in + c_in:n_in + c_in + n_out]
        c_outs = refs[n_in + c_in + n_out:n_in + c_in + n_out + c_out]
        scr = refs[n_in + c_in + n_out + c_out:n_in + c_in + n_out + c_out + n_scr]
        parts = (c_ins, c_outs, refs[n_in + c_in + n_out + c_out + n_scr:])
        step = pl.program_id(0)
        for axis in range(1, len(grid)):
            step = step * grid[axis] + pl.program_id(axis)

        @pl.when(step == 0)
        def _():
            comm.start(*parts)

        body(*ins, *outs, *scr)

        if mid_step is not None:
            @pl.when(step == mid_step)
            def _():
                comm.mid(*parts)

        @pl.when(step == total - 1)
        def _():
            comm.finish(*parts)

    any_spec = pl.BlockSpec(memory_space=pl.ANY)
    return (wrapped, list(in_specs) + [any_spec] * c_in, list(out_specs) + [any_spec] * c_out,
            list(out_shapes) + list(comm.out_shapes), list(scratch_shapes) + list(comm.sem_shapes),
            list(operands) + list(comm.ins))


def _pair_add(name, partial, kind, shard_shape, theirs, core):
    r, w = shard_shape
    tb = _tile(r, max(16, (2**19 // w) // 16 * 16), 16)
    nb = r // tb
    if kind == "rows":
        mine = pl.BlockSpec((tb, w), lambda k, i, core_ref: ((2 * k + core_ref[0]) * nb + i, 0))
    elif kind == "cols":
        mine = pl.BlockSpec((tb, w), lambda k, i, core_ref: (i, 2 * k + core_ref[0]))
    else:
        mine = pl.BlockSpec((None, tb, w), lambda k, i, core_ref: (2 * k + core_ref[0], i, 0))
    slot = pl.BlockSpec((None, tb, w), lambda k, i, core_ref: (k, i, 0))

    def body(core_ref, a_ref, b_ref, o_ref):
        o_ref[...] = (a_ref[...].astype(F32) + b_ref[...].astype(F32)).astype(o_ref.dtype)

    return pl.pallas_call(
        body, name=name,
        grid_spec=pltpu.PrefetchScalarGridSpec(num_scalar_prefetch=1, grid=(4, nb), in_specs=[mine, slot],
                                               out_specs=slot),
        out_shape=jax.ShapeDtypeStruct(theirs.shape, theirs.dtype), compiler_params=_params(2))(core, partial, theirs)


def _matmul(name, mode, grid, a, a_spec, bs, b_specs, extras, extra_specs, out_shapes, out_specs, acc_shape,
            epilogue, a_fn=None, comm=None, a_norm=None):
    nk, nb, ne, no = grid[2], len(bs), len(extras), len(out_shapes)
    if a_norm is not None:
        assert nk == 1 and mode != "tn"
        tm_a, k_a = a_spec.block_shape
        vec = pl.BlockSpec((1, k_a), lambda i, j, kk: (0, 0))
        extras, extra_specs = list(extras) + list(a_norm), list(extra_specs) + [vec] * 3
        out_shapes = list(out_shapes) + [jax.ShapeDtypeStruct(a.shape, BF16)]
        out_specs = list(out_specs) + [pl.BlockSpec((tm_a, k_a), lambda i, j, kk: (i, 0))]

    def body(*refs):
        a_ref, b_refs, e_refs = refs[0], refs[1:1 + nb], refs[1 + nb:1 + nb + ne]
        o_refs, acc_refs = refs[1 + nb + ne:1 + nb + ne + no], refs[1 + nb + ne + no:]
        if a_norm is None:
            a_val = a_ref[...]
            if a_fn is not None:
                a_val = a_fn(a_val)
            a_val = a_val.astype(BF16)
        else:
            g_ref, sh_ref, sc_ref = refs[1 + nb + ne:4 + nb + ne]
            o_refs = refs[4 + nb + ne:4 + nb + ne + no]
            h_ref, h_scr = refs[4 + nb + ne + no], refs[5 + nb + ne + no]

            @pl.when(pl.program_id(1) == 0)
            def _():
                xv = a_ref[...]
                rstd = lax.rsqrt(jnp.mean(xv * xv, axis=-1, keepdims=True) + EPS)
                hv = ((xv * rstd) * g_ref[...] * (1.0 + sc_ref[...]) + sh_ref[...]).astype(BF16)
                h_scr[...] = hv
                h_ref[...] = hv

            a_val = h_scr[...]

        def product(b_ref):
            return lax.dot_general(a_val, b_ref[...].astype(BF16), _DN[mode], preferred_element_type=F32)

        def finish(accs):
            outs = epilogue(accs, [e[...] for e in e_refs])
            for o_ref, o in zip(o_refs, outs):
                o_ref[...] = o.astype(o_ref.dtype)

        if nk == 1:
            finish([product(b_ref) for b_ref in b_refs])
        else:
            kk = pl.program_id(2)

            @pl.when(kk == 0)
            def _():
                for acc in acc_refs:
                    acc[...] = jnp.zeros_like(acc)

            for acc, b_ref in zip(acc_refs, b_refs):
                acc[...] += product(b_ref)

            @pl.when(kk == nk - 1)
            def _():
                finish([acc[...] for acc in acc_refs])

    scratch = [] if nk == 1 else [pltpu.VMEM(acc_shape, F32) for _ in range(nb)]
    if a_norm is not None:
        scratch = [pltpu.VMEM(a_spec.block_shape, BF16)]
    body, in_specs, out_specs, out_shape, scratch, operands = _carry(
        comm, body, grid, [a_spec] + list(b_specs) + list(extra_specs), list(out_specs), list(out_shapes), scratch,
        [a, *bs, *extras])
    outs = pl.pallas_call(
        body, name=name, grid=grid, in_specs=in_specs, out_specs=out_specs, out_shape=out_shape,
        scratch_shapes=scratch, compiler_params=_params(3))(*operands)
    n_mine = len(out_shapes)
    if comm is not None:
        comm.results = outs[n_mine:]
    return outs[:n_mine]


def _mm_tiles(m, n, k, tiles):
    return _tile(m, tiles[0], 16), _tile(n, tiles[1], LANES), _tile(k, tiles[2], LANES)


FFN_TILES = {"in": (1024, 512, 2048), "out": (1024, 512, 5632), "out_bwd": (2048, 512, 2048),
             "in_bwd": (1024, 1024, 2816), "out_dw": (512, 2048, 2048), "in_dw": (1024, 512, 4096)}
FULL_K = (1024, 1024, 2048)
NARROW_K = (1024, 512, 2048)
LOOP_K = (1024, 1024, 2048)


def _mm_nn(name, a, b, out_dtype, b_col0=0, n=None, tiles=FULL_K, comm=None, a_norm=None):
    m, k = a.shape
    n = b.shape[1] if n is None else n
    tm, tn, tk = _mm_tiles(m, n, k, tiles)
    off = b_col0 // tn
    assert b_col0 % tn == 0
    outs = _matmul(
        name, "nn", (m // tm, n // tn, k // tk), a, pl.BlockSpec((tm, tk), lambda i, j, kk: (i, kk)),
        [b], [pl.BlockSpec((tk, tn), lambda i, j, kk: (kk, j + off))], [], [],
        [jax.ShapeDtypeStruct((m, n), out_dtype)], [pl.BlockSpec((tm, tn), lambda i, j, kk: (i, j))], (tm, tn),
        lambda accs, ex: accs[:1], comm=comm, a_norm=a_norm)
    return outs[0] if a_norm is None else outs


def _mm_bias_rows(name, a, b, bias, a_fn, b_block=0):
    m, k = a.shape
    n = b.shape[1]
    tn = _tile(n, 768, LANES)
    return _matmul(
        name, "nn", (1, n // tn, 1), a, pl.BlockSpec((m, k), lambda i, j, kk: (0, 0)),
        [b], [pl.BlockSpec((k, tn), lambda i, j, kk: (b_block, j))],
        [bias], [pl.BlockSpec((1, tn), lambda i, j, kk: (0, j))],
        [jax.ShapeDtypeStruct((m, n), F32)], [pl.BlockSpec((m, tn), lambda i, j, kk: (0, j))], (m, tn),
        lambda accs, ex: [accs[0] + ex[0]], a_fn=a_fn)[0]


def _mm_swiglu(name, h, w_in, f, tiles, comm=None, a_norm=None):
    s, d = h.shape
    tm, tn, tk = _mm_tiles(s, f, d, tiles)
    nf = f // tn

    def epilogue(accs, ex):
        a_pre, b_pre = accs
        return [jnp.stack([a_pre, b_pre]), _silu(a_pre) * b_pre]

    return _matmul(
        name, "nn", (s // tm, nf, d // tk), h, pl.BlockSpec((tm, tk), lambda i, j, kk: (i, kk)),
        [w_in, w_in], [pl.BlockSpec((tk, tn), lambda i, j, kk: (kk, j)),
                       pl.BlockSpec((tk, tn), lambda i, j, kk: (kk, j + nf))], [], [],
        [jax.ShapeDtypeStruct((2, s, f), BF16), jax.ShapeDtypeStruct((s, f), BF16)],
        [pl.BlockSpec((2, tm, tn), lambda i, j, kk: (0, i, j)), pl.BlockSpec((tm, tn), lambda i, j, kk: (i, j))],
        (tm, tn), epilogue, comm=comm, a_norm=a_norm)


def _mm_residual(name, a, w, x_in, gvec, tiles=FULL_K, comm=None):
    s, k = a.shape
    d = w.shape[1]
    tm, tn, tk = _mm_tiles(s, d, k, tiles)

    def epilogue(accs, ex):
        x_blk, g_row = ex
        return [accs[0], x_blk + g_row * accs[0]]

    tile = pl.BlockSpec((tm, tn), lambda i, j, kk: (i, j))
    return _matmul(
        name, "nn", (s // tm, d // tn, k // tk), a, pl.BlockSpec((tm, tk), lambda i, j, kk: (i, kk)),
        [w], [pl.BlockSpec((tk, tn), lambda i, j, kk: (kk, j))],
        [x_in, gvec], [tile, pl.BlockSpec((1, tn), lambda i, j, kk: (0, j))],
        [jax.ShapeDtypeStruct((s, d), BF16), jax.ShapeDtypeStruct((s, d), F32)], [tile, tile], (tm, tn), epilogue,
        comm=comm)


def _mm_nt(name, a, w, out_dtype, w_col0=0, extra_add=None, tiles=FULL_K):
    s, k = a.shape
    n = w.shape[0]
    tm, tn, tk = _mm_tiles(s, n, k, tiles)
    off = w_col0 // tk
    assert w_col0 % tk == 0
    tile = pl.BlockSpec((tm, tn), lambda i, j, kk: (i, j))
    extras, especs = ([], []) if extra_add is None else ([extra_add], [tile])
    return _matmul(
        name, "nt", (s // tm, n // tn, k // tk), a, pl.BlockSpec((tm, tk), lambda i, j, kk: (i, kk)),
        [w], [pl.BlockSpec((tn, tk), lambda i, j, kk: (j, kk + off))], extras, especs,
        [jax.ShapeDtypeStruct((s, n), out_dtype)], [tile], (tm, tn),
        lambda accs, ex: [accs[0] + ex[0]] if ex else accs)[0]


def _mm_nt_parts(name, a3, w, out_dtype, tiles=LOOP_K, comm=None):
    p, s, k = a3.shape
    n = w.shape[0]
    tm, tn, tk = _mm_tiles(s, n, k, tiles)
    nkp = k // tk
    return _matmul(
        name, "nt", (s // tm, n // tn, p * nkp), a3,
        pl.BlockSpec((None, tm, tk), lambda i, j, kk: (kk // nkp, i, kk % nkp)),
        [w], [pl.BlockSpec((tn, tk), lambda i, j, kk: (j, kk))], [], [],
        [jax.ShapeDtypeStruct((s, n), out_dtype)], [pl.BlockSpec((tm, tn), lambda i, j, kk: (i, j))], (tm, tn),
        lambda accs, ex: accs, comm=comm)[0]


def _mm_nt_swiglu_bwd(name, dy, w_out, ab, tiles):
    s, d = dy.shape
    f = w_out.shape[0]
    tm, tn, tk = _mm_tiles(s, f, d, tiles)

    def epilogue(accs, ex):
        du, a_pre, b_pre = accs[0], ex[0][0].astype(F32), ex[0][1].astype(F32)
        sig = _sigmoid(a_pre)
        da = du * b_pre * (sig * (1.0 + a_pre * (1.0 - sig)))
        db = du * (a_pre * sig)
        return [jnp.stack([da, db])]

    pair = pl.BlockSpec((2, tm, tn), lambda i, j, kk: (0, i, j))
    return _matmul(
        name, "nt", (s // tm, f // tn, d // tk), dy, pl.BlockSpec((tm, tk), lambda i, j, kk: (i, kk)),
        [w_out], [pl.BlockSpec((tn, tk), lambda i, j, kk: (j, kk))], [ab], [pair],
        [jax.ShapeDtypeStruct((2, s, f), BF16)], [pair], (tm, tn), epilogue)[0]


def _mm_tn(name, a, b, out_dtype, a_fn=None, tiles=LOOP_K, comm=None):
    s, m = a.shape
    tm, tk = _tile(m, tiles[0], LANES), _tile(s, tiles[2], 16)
    if b.ndim == 3:
        npart = b.shape[2]
        n = b.shape[0] * npart
        tn = _tile(npart, tiles[1], LANES)
        nj = npart // tn
        b_spec = pl.BlockSpec((None, tk, tn), lambda i, j, kk: (j // nj, kk, j % nj))
    else:
        n = b.shape[1]
        tn = _tile(n, tiles[1], LANES)
        b_spec = pl.BlockSpec((tk, tn), lambda i, j, kk: (kk, j))
    return _matmul(
        name, "tn", (m // tm, n // tn, s // tk), a, pl.BlockSpec((tk, tm), lambda i, j, kk: (kk, i)),
        [b], [b_spec], [], [],
        [jax.ShapeDtypeStruct((m, n), out_dtype)], [pl.BlockSpec((tm, tn), lambda i, j, kk: (i, j))], (tm, tn),
        lambda accs, ex: accs, a_fn=a_fn, comm=comm)[0]


def _row_tile(s):
    return _tile(s, 256, 16)


def _fold_rows(v):
    return jnp.sum(v.reshape(v.shape[0] // SUBLANES, SUBLANES, v.shape[1]), axis=0)


def _accumulate(i, acc_ref, val):
    @pl.when(i == 0)
    def _():
        acc_ref[...] = val

    @pl.when(i > 0)
    def _():
        acc_ref[...] += val


def _rms_bwd(xv, dn_g, d):
    rstd = lax.rsqrt(jnp.mean(xv * xv, axis=-1, keepdims=True) + EPS)
    xhat = xv * rstd
    dx = rstd * (dn_g - xhat * (jnp.sum(dn_g * xhat, axis=-1, keepdims=True) * (1.0 / d)))
    return xhat, dx


def _norm_mod_bwd(name, dh, x, g, scale, dx_in, then=None):
    s, d = x.shape
    tb = _row_tile(s)
    nsteps = s // tb
    coef = None if then is None else then[2]

    def body(*refs):
        dh_ref, x_ref, g_ref, sc_ref, dxin_ref = refs[:5]
        if then is None:
            dx_ref, dsh_ref, dsc_ref, dg_ref, a_sh, a_sc, a_g = refs[5:]
        else:
            y_ref, gv_ref, dx_ref, dsh_ref, dsc_ref, dg_ref, dy_ref, dgate_ref, a_sh, a_sc, a_g, a_gate = refs[5:]
        i = pl.program_id(0)
        dhv, gv = dh_ref[...], g_ref[...]
        dn = dhv * (1.0 + sc_ref[...])
        xhat, dx = _rms_bwd(x_ref[...], dn * gv, d)
        dx_total = dxin_ref[...] + dx
        dx_ref[...] = dx_total
        _accumulate(i, a_sh, _fold_rows(dhv))
        _accumulate(i, a_sc, _fold_rows(dhv * (xhat * gv)))
        _accumulate(i, a_g, _fold_rows(dn * xhat))
        if then is not None:
            dy_ref[...] = (dx_total * gv_ref[...]).astype(BF16)
            _accumulate(i, a_gate, _fold_rows(dx_total * y_ref[...].astype(F32)))

        @pl.when(i == nsteps - 1)
        def _():
            dsh_ref[...] = jnp.sum(a_sh[...], axis=0, keepdims=True)
            dsc_ref[...] = jnp.sum(a_sc[...], axis=0, keepdims=True)
            dg_ref[...] = jnp.sum(a_g[...], axis=0, keepdims=True)
            if then is not None:
                dgate_ref[...] = coef * jnp.sum(a_gate[...], axis=0, keepdims=True)

    row = pl.BlockSpec((tb, d), lambda i: (i, 0))
    vec = pl.BlockSpec((1, d), lambda i: (0, 0))
    vshape = jax.ShapeDtypeStruct((1, d), F32)
    in_specs, operands = [row, row, vec, vec, row], [dh, x, g, scale, dx_in]
    out_specs, out_shape = [row, vec, vec, vec], [jax.ShapeDtypeStruct((s, d), F32), vshape, vshape, vshape]
    n_acc = 3
    if then is not None:
        in_specs, operands = in_specs + [row, vec], operands + [then[0], then[1]]
        out_specs, out_shape = out_specs + [row, vec], out_shape + [jax.ShapeDtypeStruct((s, d), BF16), vshape]
        n_acc = 4
    return pl.pallas_call(
        body, name=name, grid=(nsteps,), in_specs=in_specs, out_specs=out_specs, out_shape=out_shape,
        scratch_shapes=[pltpu.VMEM((SUBLANES, d), F32)] * n_acc, compiler_params=_params(1))(*operands)


def _gate_bwd(name, dx, y, gvec, coef):
    s, d = dx.shape
    tb = _row_tile(s)
    nsteps = s // tb

    def body(dx_ref, y_ref, g_ref, dy_ref, dg_ref, acc):
        i = pl.program_id(0)
        dxv = dx_ref[...]
        dy_ref[...] = (dxv * g_ref[...]).astype(BF16)
        _accumulate(i, acc, _fold_rows(dxv * y_ref[...].astype(F32)))

        @pl.when(i == nsteps - 1)
        def _():
            dg_ref[...] = coef * jnp.sum(acc[...], axis=0, keepdims=True)

    row = pl.BlockSpec((tb, d), lambda i: (i, 0))
    vec = pl.BlockSpec((1, d), lambda i: (0, 0))
    return pl.pallas_call(
        body, name=name, grid=(nsteps,), in_specs=[row, row, vec], out_specs=[row, vec],
        out_shape=[jax.ShapeDtypeStruct((s, d), BF16), jax.ShapeDtypeStruct((1, d), F32)],
        scratch_shapes=[pltpu.VMEM((SUBLANES, d), F32)], compiler_params=_params(1))(dx, y, gvec)


def _final_loss(name, x, target, g):
    s, d = x.shape
    tb = _row_tile(s)
    nsteps = s // tb

    def body(x_ref, t_ref, g_ref, dx_ref, loss_ref, dg_ref, a_loss, a_g):
        i = pl.program_id(0)
        xv, gv = x_ref[...], g_ref[...]
        rstd = lax.rsqrt(jnp.mean(xv * xv, axis=-1, keepdims=True) + EPS)
        xhat = xv * rstd
        err = xhat * gv - t_ref[...]
        dyv = err * (1.0 / d)
        dn_g = dyv * gv
        dx_ref[...] = rstd * (dn_g - xhat * (jnp.sum(dn_g * xhat, axis=-1, keepdims=True) * (1.0 / d)))
        _accumulate(i, a_loss, _fold_rows(err * err))
        _accumulate(i, a_g, _fold_rows(dyv * xhat))

        @pl.when(i == nsteps - 1)
        def _():
            total = jnp.sum(jnp.sum(a_loss[...], axis=0, keepdims=True), axis=1, keepdims=True) * (0.5 / d)
            loss_ref[...] = jnp.broadcast_to(total, (1, LANES))
            dg_ref[...] = jnp.sum(a_g[...], axis=0, keepdims=True)

    row = pl.BlockSpec((tb, d), lambda i: (i, 0))
    vec = pl.BlockSpec((1, d), lambda i: (0, 0))
    return pl.pallas_call(
        body, name=name, grid=(nsteps,), in_specs=[row, row, vec],
        out_specs=[row, pl.BlockSpec((1, LANES), lambda i: (0, 0)), vec],
        out_shape=[jax.ShapeDtypeStruct((s, d), F32), jax.ShapeDtypeStruct((1, LANES), F32),
                   jax.ShapeDtypeStruct((1, d), F32)],
        scratch_shapes=[pltpu.VMEM((SUBLANES, d), F32)] * 2, compiler_params=_params(1))(x, target, g)


def _shift_rows(cur, halo, n):
    rolled = pltpu.roll(cur, n, 0)
    row = lax.broadcasted_iota(jnp.int32, cur.shape, 0)
    for r in range(n):
        rolled = jnp.where(row == r, halo[SUBLANES - n + r:SUBLANES - n + r + 1, :], rolled)
    return rolled


def _shift_rows_up(cur, halo, n):
    tb = cur.shape[0]
    rolled = pltpu.roll(cur, tb - n, 0)
    row = lax.broadcasted_iota(jnp.int32, cur.shape, 0)
    for r in range(n):
        rolled = jnp.where(row == tb - n + r, halo[r:r + 1, :], rolled)
    return rolled


def _conv_specs(s, d, tb, tn):
    nd = d // tn
    hb = tb // SUBLANES
    cur = lambda part: pl.BlockSpec((tb, tn), lambda i, j: (i, j + part * nd))
    prev = lambda part: pl.BlockSpec((SUBLANES, tn), lambda i, j: (jnp.maximum(i * hb - 1, 0), j + part * nd))
    nxt = lambda part: pl.BlockSpec((SUBLANES, tn), lambda i, j: (jnp.minimum((i + 1) * hb, s // SUBLANES - 1),
                                                                  j + part * nd))
    return cur, prev, nxt


def _conv_fwd(name, p, conv_w, conv_b):
    s, d = p.shape[0], p.shape[1] // 3
    tb, tn = _row_tile(s), _tile(d, 2048, LANES)
    cur, prev, _ = _conv_specs(s, d, tb, tn)

    def body(bg_ref, cg_ref, xv_ref, cgp_ref, xvp_ref, w_ref, b_ref, z_ref):
        i = pl.program_id(0)
        u = cg_ref[...] * xv_ref[...]
        up = jnp.where(i > 0, cgp_ref[...] * xvp_ref[...], 0.0)
        w = w_ref[...]
        uc = w[0:1, :] * _shift_rows(u, up, 2) + w[1:2, :] * _shift_rows(u, up, 1) + w[2:3, :] * u + b_ref[...]
        z_ref[...] = (bg_ref[...] * uc).astype(BF16)

    return pl.pallas_call(
        body, name=name, grid=(s // tb, d // tn),
        in_specs=[cur(0), cur(1), cur(2), prev(1), prev(2), pl.BlockSpec((3, tn), lambda i, j: (0, j)),
                  pl.BlockSpec((1, tn), lambda i, j: (0, j))],
        out_specs=pl.BlockSpec((tb, tn), lambda i, j: (i, j)),
        out_shape=jax.ShapeDtypeStruct((s, d), BF16), compiler_params=_params(2))(p, p, p, p, p, conv_w, conv_b)


def _conv_bwd(name, dz, p, conv_w, conv_b):
    s, d = dz.shape
    tb, tn = _row_tile(s), _tile(d, 2048, LANES)
    nsteps = s // tb
    cur, prev, nxt = _conv_specs(s, d, tb, tn)

    def body(dz_ref, bg_ref, cg_ref, xv_ref, cgp_ref, xvp_ref, dzn_ref, bgn_ref, w_ref, b_ref,
             dp_ref, dw_ref, db_ref, a_w0, a_w1, a_w2, a_b):
        j, i = pl.program_id(0), pl.program_id(1)
        cg, xv, bg, dzv = cg_ref[...], xv_ref[...], bg_ref[...], dz_ref[...]
        u = cg * xv
        up = jnp.where(i > 0, cgp_ref[...] * xvp_ref[...], 0.0)
        w = w_ref[...]
        u1, u2 = _shift_rows(u, up, 1), _shift_rows(u, up, 2)
        uc = w[0:1, :] * u2 + w[1:2, :] * u1 + w[2:3, :] * u + b_ref[...]
        duc = dzv * bg
        ducn = jnp.where(i < nsteps - 1, dzn_ref[...] * bgn_ref[...], 0.0)
        du = w[2:3, :] * duc + w[1:2, :] * _shift_rows_up(duc, ducn, 1) + w[0:1, :] * _shift_rows_up(duc, ducn, 2)
        dp_ref[0] = (dzv * uc).astype(BF16)
        dp_ref[1] = (du * xv).astype(BF16)
        dp_ref[2] = (du * cg).astype(BF16)
        _accumulate(i, a_w0, _fold_rows(duc * u2))
        _accumulate(i, a_w1, _fold_rows(duc * u1))
        _accumulate(i, a_w2, _fold_rows(duc * u))
        _accumulate(i, a_b, _fold_rows(duc))

        @pl.when(i == nsteps - 1)
        def _():
            dw_ref[0:1, :] = jnp.sum(a_w0[...], axis=0, keepdims=True)
            dw_ref[1:2, :] = jnp.sum(a_w1[...], axis=0, keepdims=True)
            dw_ref[2:3, :] = jnp.sum(a_w2[...], axis=0, keepdims=True)
            db_ref[...] = jnp.sum(a_b[...], axis=0, keepdims=True)

    swap = lambda spec: pl.BlockSpec(spec.block_shape, lambda j, i, _m=spec.index_map: _m(i, j))
    return pl.pallas_call(
        body, name=name, grid=(d // tn, nsteps),
        in_specs=[pl.BlockSpec((tb, tn), lambda j, i: (i, j)), swap(cur(0)), swap(cur(1)), swap(cur(2)),
                  swap(prev(1)), swap(prev(2)), swap(nxt(0)), swap(nxt(0)),
                  pl.BlockSpec((3, tn), lambda j, i: (0, j)), pl.BlockSpec((1, tn), lambda j, i: (0, j))],
        out_specs=[pl.BlockSpec((3, tb, tn), lambda j, i: (0, i, j)), pl.BlockSpec((3, tn), lambda j, i: (0, j)),
                   pl.BlockSpec((1, tn), lambda j, i: (0, j))],
        out_shape=[jax.ShapeDtypeStruct((3, s, d), BF16), jax.ShapeDtypeStruct((3, d), F32),
                   jax.ShapeDtypeStruct((1, d), F32)],
        scratch_shapes=[pltpu.VMEM((SUBLANES, tn), F32)] * 4, compiler_params=_params(2),
    )(dz, p, p, p, p, p, dz, p, conv_w, conv_b)


def _block_cumsum(v):
    tb = v.shape[0]
    row = lax.broadcasted_iota(jnp.int32, v.shape, 0)
    sh = 1
    while sh < tb:
        v = v + jnp.where(row >= sh, pltpu.roll(v, sh, 0), 0.0)
        sh *= 2
    return v


def _fgate_fwd(name, zf, b_f):
    s = zf.shape[0]
    tb = _tile(s, 512, SUBLANES)

    def body(z_ref, b_ref, f_ref, ft_ref, carry):
        i = pl.program_id(0)
        z = z_ref[...] + b_ref[...]
        ls = jnp.minimum(z, 0.0) - jnp.log(1.0 + jnp.exp(-jnp.abs(z)))
        run = _block_cumsum(ls)

        @pl.when(i == 0)
        def _():
            carry[...] = jnp.zeros_like(carry)

        out = run + carry[...]
        f_ref[...] = out
        ft_ref[...] = jnp.transpose(out)
        carry[...] = out[tb - 1:tb, :]

    blk = pl.BlockSpec((tb, LANES), lambda i: (i, 0))
    return pl.pallas_call(
        body, name=name, grid=(s // tb,), in_specs=[blk, pl.BlockSpec((1, LANES), lambda i: (0, 0))],
        out_specs=[blk, pl.BlockSpec((LANES, tb), lambda i: (0, i))],
        out_shape=[jax.ShapeDtypeStruct((s, LANES), F32), jax.ShapeDtypeStruct((LANES, s), F32)],
        scratch_shapes=[pltpu.VMEM((1, LANES), F32)], compiler_params=_params(1))(zf, b_f)


def _fgate_bwd(name, dfcum, zf, b_f):
    s = zf.shape[0]
    tb = _tile(s, 512, SUBLANES)
    nsteps = s // tb

    def body(df_ref, z_ref, b_ref, dz_ref, db_ref, carry, acc):
        i = pl.program_id(0)
        df = df_ref[...]
        incl = _block_cumsum(df)
        total = incl[tb - 1:tb, :]

        @pl.when(i == 0)
        def _():
            carry[...] = jnp.zeros_like(carry)

        suffix = total - incl + df + carry[...]
        carry[...] += total
        dzv = suffix * jax.nn.sigmoid(-(z_ref[...] + b_ref[...]))
        dz_ref[...] = dzv
        _accumulate(i, acc, _fold_rows(dzv))

        @pl.when(i == nsteps - 1)
        def _():
            db_ref[...] = jnp.sum(acc[...], axis=0, keepdims=True)

    blk = pl.BlockSpec((tb, LANES), lambda i: (nsteps - 1 - i, 0))
    vec = pl.BlockSpec((1, LANES), lambda i: (0, 0))
    return pl.pallas_call(
        body, name=name, grid=(nsteps,), in_specs=[blk, blk, vec], out_specs=[blk, vec],
        out_shape=[jax.ShapeDtypeStruct((s, LANES), F32), jax.ShapeDtypeStruct((1, LANES), F32)],
        scratch_shapes=[pltpu.VMEM((1, LANES), F32), pltpu.VMEM((SUBLANES, LANES), F32)],
        compiler_params=_params(1))(dfcum, zf, b_f)


def _lane_column(block, lane):
    sel = lax.broadcasted_iota(jnp.int32, block.shape, 1) == lane
    return jnp.sum(jnp.where(sel, block, 0.0), axis=1, keepdims=True)


def _col_to_row(col):
    return jnp.transpose(jnp.broadcast_to(col, (col.shape[0], LANES)))[0:1, :]


def _attn_tiles(s):
    t = _tile(s, 1024, LANES)
    return t, t


def _attn_fwd(name, q, kv, fcum_t, n_heads):
    s, d = q.shape
    dh = d // n_heads
    tq, tk = _attn_tiles(s)
    scale2 = LOG2E / math.sqrt(dh)

    def body(q_ref, k_ref, v_ref, ft_ref, o_ref, l_ref):
        qi = pl.program_id(1)
        qv = q_ref[...]

        def step(kj, carry, diagonal):
            m, l, acc = carry
            start = pl.multiple_of(kj * tk, tk)
            kb = k_ref[pl.ds(start, tk), :]
            vb = v_ref[pl.ds(start, tk), :]
            sc = lax.dot_general(qv, kb, _DN["nt"], preferred_element_type=F32) * scale2
            sc = sc - ft_ref[:, pl.ds(start, tk)] * LOG2E
            if diagonal:
                keep = lax.broadcasted_iota(jnp.int32, sc.shape, 0) >= lax.broadcasted_iota(jnp.int32, sc.shape, 1)
                sc = jnp.where(keep, sc, -jnp.inf)
            m_new = jnp.maximum(m, jnp.max(sc, axis=1, keepdims=True))
            alpha = jnp.exp2(m - m_new)
            pr = jnp.exp2(sc - m_new)
            l = alpha * l + jnp.sum(pr, axis=1, keepdims=True)
            acc = alpha * acc + lax.dot_general(pr.astype(BF16), vb, _DN["nn"], preferred_element_type=F32)
            return m_new, l, acc

        init = (jnp.full((tq, 1), -jnp.inf, F32), jnp.zeros((tq, 1), F32), jnp.zeros((tq, dh), F32))
        carry = lax.fori_loop(0, qi, lambda kj, cr: step(kj, cr, False), init)
        m, l, acc = step(qi, carry, True)
        o_ref[...] = (acc / l).astype(BF16)
        l_ref[...] = _col_to_row(m + jnp.log2(l))

    nh = n_heads
    return pl.pallas_call(
        body, name=name, grid=(nh, s // tq),
        in_specs=[pl.BlockSpec((tq, dh), lambda h, i: (i, h)),
                  pl.BlockSpec((s, dh), lambda h, i: (0, h)),
                  pl.BlockSpec((s, dh), lambda h, i: (0, nh + h)),
                  pl.BlockSpec((None, 1, s), lambda h, i: (h, 0, 0))],
        out_specs=[pl.BlockSpec((tq, dh), lambda h, i: (i, h)),
                   pl.BlockSpec((None, 1, tq), lambda h, i: (h, 0, i))],
        out_shape=[jax.ShapeDtypeStruct((s, d), BF16), jax.ShapeDtypeStruct((nh, 1, s), F32)],
        compiler_params=_params(2))(q, kv, kv, fcum_t)


def _attn_delta(name, d_o, o, n_heads):
    s, d = d_o.shape
    dh = d // n_heads
    tb = _row_tile(s)

    def body(do_ref, o_ref, dlt_ref, dob_ref):
        dov = do_ref[...]
        prod = dov * o_ref[...].astype(F32)
        lane = lax.broadcasted_iota(jnp.int32, (tb, LANES), 1)
        out = jnp.zeros((tb, LANES), F32)
        for h in range(n_heads):
            out = jnp.where(lane == h, jnp.sum(prod[:, h * dh:(h + 1) * dh], axis=1, keepdims=True), out)
        dlt_ref[...] = jnp.transpose(out)
        dob_ref[...] = dov.astype(BF16)

    row = pl.BlockSpec((tb, d), lambda i: (i, 0))
    return pl.pallas_call(
        body, name=name, grid=(s // tb,), in_specs=[row, row],
        out_specs=[pl.BlockSpec((LANES, tb), lambda i: (0, i)), row],
        out_shape=[jax.ShapeDtypeStruct((LANES, s), F32), jax.ShapeDtypeStruct((s, d), BF16)],
        compiler_params=_params(1))(d_o, o)


def _attn_bwd(name, q, kv, d_o, fcum, lse_t, delta_t, n_heads):
    s, d = q.shape
    dh = d // n_heads
    tq, tk = _attn_tiles(s)
    nq, nk = s // tq, s // tk
    scale = 1.0 / math.sqrt(dh)
    scale2 = LOG2E * scale

    def body(q_ref, k_ref, v_ref, do_ref, f_ref, lt_ref, dlt_ref, dq_ref, dkv_ref, dfq_ref, dfk_ref, dq_acc, dfq_acc):
        h, kj = pl.program_id(0), pl.program_id(1)
        kb, vb = k_ref[...], v_ref[...]
        fk2 = _lane_column(f_ref[...], h) * LOG2E

        @pl.when(kj == 0)
        def _():
            dq_acc[...] = jnp.zeros_like(dq_acc)
            dfq_acc[...] = jnp.zeros_like(dfq_acc)

        def step(qi, carry, diagonal):
            dk, dv, dfk = carry
            start = pl.multiple_of(qi * tq, tq)
            qb = q_ref[pl.ds(start, tq), :]
            dob = do_ref[pl.ds(start, tq), :]
            sc = lax.dot_general(kb, qb, _DN["nt"], preferred_element_type=F32) * scale2 - fk2
            pr = jnp.exp2(sc - lt_ref[:, pl.ds(start, tq)])
            if diagonal:
                keep = lax.broadcasted_iota(jnp.int32, sc.shape, 1) >= lax.broadcasted_iota(jnp.int32, sc.shape, 0)
                pr = jnp.where(keep, pr, 0.0)
            dv = dv + lax.dot_general(pr.astype(BF16), dob, _DN["nn"], preferred_element_type=F32)
            dp = lax.dot_general(vb, dob, _DN["nt"], preferred_element_type=F32)
            ds = pr * (dp - dlt_ref[:, pl.ds(start, tq)])
            ds_b = ds.astype(BF16)
            dk = dk + lax.dot_general(ds_b, qb, _DN["nn"], preferred_element_type=F32)
            dfk = dfk - jnp.sum(ds, axis=1, keepdims=True)
            dq_acc[pl.ds(start, tq), :] += lax.dot_general(ds_b, kb, _DN["tn"], preferred_element_type=F32)
            dfq_acc[:, pl.ds(start, tq)] += jnp.sum(ds, axis=0, keepdims=True)
            return dk, dv, dfk

        init = (jnp.zeros((tk, dh), F32), jnp.zeros((tk, dh), F32), jnp.zeros((tk, 1), F32))
        carry = step(kj, init, True)
        dk, dv, dfk = lax.fori_loop(kj + 1, nq, lambda qi, cr: step(qi, cr, False), carry)
        dkv_ref[0] = (dk * scale).astype(BF16)
        dkv_ref[1] = dv.astype(BF16)
        dfk_ref[...] = _col_to_row(dfk)

        @pl.when(kj == nk - 1)
        def _():
            dq_ref[...] = (dq_acc[...] * scale).astype(BF16)
            dfq_ref[...] = dfq_acc[...]

    nh = n_heads
    full = pl.BlockSpec((s, dh), lambda h, j: (0, h))
    row_t = pl.BlockSpec((None, 1, s), lambda h, j: (h, 0, 0))
    return pl.pallas_call(
        body, name=name, grid=(nh, nk),
        in_specs=[full, pl.BlockSpec((tk, dh), lambda h, j: (j, h)), pl.BlockSpec((tk, dh), lambda h, j: (j, nh + h)),
                  full, pl.BlockSpec((tk, LANES), lambda h, j: (j, 0)), row_t, row_t],
        out_specs=[full, pl.BlockSpec((2, tk, dh), lambda h, j: (0, j, h)), row_t,
                   pl.BlockSpec((None, 1, tk), lambda h, j: (h, 0, j))],
        out_shape=[jax.ShapeDtypeStruct((s, d), BF16), jax.ShapeDtypeStruct((2, s, d), BF16),
                   jax.ShapeDtypeStruct((nh, 1, s), F32), jax.ShapeDtypeStruct((nh, 1, s), F32)],
        scratch_shapes=[pltpu.VMEM((s, dh), F32), pltpu.VMEM((1, s), F32)],
        compiler_params=_params(2))(q, kv, kv, d_o, fcum, lse_t, delta_t)


def _sum_slots(name, g):
    n, r, lanes = g.shape

    def body(g_ref, o_ref):
        total = g_ref[0]
        for i in range(1, n):
            total = total + g_ref[i]
        o_ref[...] = total

    return pl.pallas_call(
        body, name=name, out_shape=jax.ShapeDtypeStruct((r, lanes), F32),
        in_specs=[pl.BlockSpec(memory_space=pltpu.VMEM)], out_specs=pl.BlockSpec(memory_space=pltpu.VMEM),
        compiler_params=pltpu.CompilerParams(vmem_limit_bytes=VMEM_LIMIT))(g)


def _adamw(name, parts, w, m, v, row0=0, previous=None):
    npart, r, c = parts.shape
    tb = _tile(math.gcd(r, row0) if row0 else r, max(SUBLANES, (2**18 // c) // SUBLANES * SUBLANES), SUBLANES)
    off = row0 // tb
    c1 = 1.0 - ADAM_B1 ** ADAM_STEP
    c2 = 1.0 - ADAM_B2 ** ADAM_STEP

    def body(p_ref, w_ref, m_ref, v_ref, *rest):
        g_out, d_out, m_out, v_out = rest[-4:]
        g = p_ref[0].astype(F32)
        for i in range(1, npart):
            g = g + p_ref[i].astype(F32)
        m_new = ADAM_B1 * m_ref[...] + (1.0 - ADAM_B1) * g
        v_new = ADAM_B2 * v_ref[...] + (1.0 - ADAM_B2) * (g * g)
        m_hat = m_new / c1
        v_hat = v_new / c2
        g_out[...] = g
        d_out[...] = -ADAM_LR * (m_hat / (jnp.sqrt(v_hat) + ADAM_EPS) + ADAM_WD * w_ref[...])
        m_out[...] = m_new
        v_out[...] = v_new

    blk = pl.BlockSpec((tb, c), lambda i: (i + off, 0))
    shape = jax.ShapeDtypeStruct(w.shape, F32)
    in_specs = [pl.BlockSpec((npart, tb, c), lambda i: (0, i, 0)), blk, blk, blk]
    operands = [parts, w, m, v]
    aliases = {}
    if previous is not None:
        in_specs += [pl.BlockSpec(memory_space=pl.ANY)] * 4
        operands += list(previous)
        aliases = {4 + k: k for k in range(4)}
    return pl.pallas_call(
        body, name=name, grid=(r // tb,), in_specs=in_specs, out_specs=[blk] * 4, out_shape=[shape] * 4,
        input_output_aliases=aliases, compiler_params=_params(1))(*operands)


def _pack(vectors):
    flat = jnp.concatenate([v.reshape(-1).astype(F32) for v in vectors])
    pad = (-flat.shape[0]) % (SUBLANES * LANES)
    return jnp.pad(flat, (0, pad)).reshape(-1, LANES)


def _unpack(flat, shapes):
    out, pos = [], 0
    for shp in shapes:
        size = math.prod(shp)
        out.append(flat[..., pos:pos + size].reshape(flat.shape[:-1] + tuple(shp)))
        pos += size
    return out


def _unshard_last(g, lead):
    nd = len(lead)
    return jnp.moveaxis(g, 0, nd).reshape(tuple(lead) + (-1,))


def _my_slice(full, me, width, axis):
    return lax.dynamic_slice_in_dim(full, me * width, width, axis)


def kernel(x, c, norm_g, w_ada, b_ada, w_ffn_in, w_ffn_out, w_conv_in, conv_w, conv_b, w_conv_out, kv_norm_g, w_ada_kv, b_ada_kv, w_kvf, b_fgate, w_q, w_o, final_g, loss_target, m_norm_g, m_w_ada, m_b_ada, m_w_ffn_in, m_w_ffn_out, m_w_conv_in, m_conv_w, m_conv_b, m_w_conv_out, m_kv_norm_g, m_w_ada_kv, m_b_ada_kv, m_w_kvf, m_b_fgate, m_w_q, m_w_o, m_final_g, v_norm_g, v_w_ada, v_b_ada, v_w_ffn_in, v_w_ffn_out, v_w_conv_in, v_conv_w, v_conv_b, v_w_conv_out, v_kv_norm_g, v_w_ada_kv, v_b_ada_kv, v_w_kvf, v_b_fgate, v_w_q, v_w_o, v_final_g):
    s, d = x.shape[1], x.shape[2]
    n_heads = b_fgate.shape[0]
    n_layers = w_ada.shape[0]
    dsh = d // N_DEV
    f = w_ffn_out.shape[2] * N_DEV
    ada_w = w_ada.shape[2]
    kv_w = w_ada_kv.shape[1]
    kvf_w = w_kvf.shape[1]
    assert n_layers == 2 and w_conv_in.shape[0] == 1 and w_q.shape[0] == 1
    assert (d // n_heads) % LANES == 0 and n_heads <= LANES and N_DEV * kvf_w == 2 * d + n_heads
    me = 4 * lax.axis_index("x") + 2 * lax.axis_index("y") + lax.axis_index("c")
    x0, target = x[0], loss_target[0]

    small = _allgather_small("gather_small_params", _pack([c, norm_g, conv_w, conv_b]))
    c_all, ng_sh, cw_sh, cb_sh = _unpack(small.reshape(N_DEV, -1), [(d,), (n_layers, 3, dsh), (3, dsh), (dsh,)])
    norm_g_full = _unshard_last(ng_sh, (n_layers, 3))
    conv_w_full = _unshard_last(cw_sh, (3,))
    conv_b_full = _unshard_last(cb_sh, ()).reshape(1, d)
    c_rows = jnp.pad(c_all, ((0, 16 - N_DEV), (0, 0)))

    ada_cols = [
        _mm_bias_rows(f"ada_rows_{l}", c_rows, w_ada.reshape(n_layers * d, ada_w),
                      _my_slice(b_ada[l], me, ada_w, 0).reshape(1, ada_w), _silu, b_block=l)
        for l in range(n_layers)]
    ada_cols.append(_mm_bias_rows("ada_rows_kv", c_rows, w_ada_kv, _my_slice(b_ada_kv, me, kv_w, 0).reshape(1, kv_w),
                                  _silu))
    ada_part = jnp.concatenate([a[:N_DEV] for a in ada_cols], axis=1)
    ada_all = _allgather_small("gather_ada_rows", _pack([ada_part]))
    ada_all = ada_all.reshape(N_DEV, -1)[:, :ada_part.size].reshape(N_DEV, N_DEV, -1)
    ada_mine = lax.dynamic_index_in_dim(ada_all, me, axis=1, keepdims=False)
    ada = [ada_mine[:, l * ada_w:(l + 1) * ada_w].reshape(3, 3, 1, d) for l in range(n_layers)]
    ada_kv = ada_mine[:, n_layers * ada_w:].reshape(2, 1, d)

    def as_item(w, kind):
        return (w.astype(BF16), kind)

    gather_first = _GatherComm([as_item(w_ffn_in[0, 0], "cols")])
    gather_l0b = _GatherComm([as_item(w_ffn_out[0, 0], "rows"), as_item(w_ffn_in[0, 1], "cols")])
    gather_conv = _GatherComm([as_item(w_conv_in[0], "cols"), as_item(w_conv_out[0], "rows")])
    gather_l0b_out = _GatherComm([as_item(w_ffn_out[0, 1], "rows")])
    gather_l1a = _GatherComm([as_item(w_ffn_out[1, 0], "rows"), as_item(w_ffn_in[1, 0], "cols")])
    gather_attn = _GatherComm([as_item(w_q[0], "rows"), as_item(w_kvf, "lead"), as_item(w_o[0], "rows")])
    gather_l1b = _GatherComm([as_item(w_ffn_out[1, 1], "rows"), as_item(w_ffn_in[1, 1], "cols")])
    (wf_in00,) = _run_comm("gather_ffn_first", gather_first)
    b_f = jnp.pad(b_fgate, (0, LANES - n_heads)).reshape(1, LANES)

    def ffn_fwd(tag, xs, l, sub, w_in, w_out, comm_in, comm_out):
        shift, scale, gate = ada[l][sub]
        tiles = FFN_TILES
        ab, u, h = _mm_swiglu(f"ffn_in_{tag}", xs, w_in, f, tiles["in"], comm=comm_in,
                              a_norm=(norm_g_full[l, sub].reshape(1, d), shift, scale))
        y, x_next = _mm_residual(f"ffn_out_{tag}", u, w_out, xs, 0.5 * gate, tiles["out"], comm=comm_out)
        return x_next, (xs, h, ab, u, y, w_in, w_out)

    ab, u, h = _mm_swiglu("ffn_in_l0a", x0, wf_in00, f, FFN_TILES["in"], comm=gather_l0b,
                          a_norm=(norm_g_full[0, 0].reshape(1, d), ada[0][0][0], ada[0][0][1]))
    wf_out00, wf_in01 = gather_l0b.results
    y, x1 = _mm_residual("ffn_out_l0a", u, wf_out00, x0, 0.5 * ada[0][0][2], FFN_TILES["out"], comm=gather_conv)
    save_f0 = (x0, h, ab, u, y, wf_in00, wf_out00)
    wc_in, wc_out = gather_conv.results
    shift, scale, gate = ada[0][1]
    p_c, h_c = _mm_nn("conv_in", x1, wc_in, F32, comm=gather_l0b_out, tiles=NARROW_K,
                      a_norm=(norm_g_full[0, 1].reshape(1, d), shift, scale))
    (wf_out01,) = gather_l0b_out.results
    z_c = _conv_fwd("conv_mix", p_c, conv_w_full, conv_b_full)
    y_c, x2 = _mm_residual("conv_out", z_c, wc_out, x1, gate)
    x3, save_f1 = ffn_fwd("l0b", x2, 0, 2, wf_in01, wf_out01, gather_l1a, gather_attn)
    wf_out10, wf_in10 = gather_l1a.results
    wq, g_kvf, wo = gather_attn.results
    wkvf = _unshard_last(g_kvf, (d,))
    wkv = wkvf[:, :2 * d]
    wf_gate = jnp.pad(wkvf[:, 2 * d:], ((0, 0), (0, LANES - n_heads)))
    x4, save_f2 = ffn_fwd("l1a", x3, 1, 0, wf_in10, wf_out10, gather_l1b, None)
    wf_out11, wf_in11 = gather_l1b.results
    shift, scale, gate_a = ada[1][1]
    q, h_q = _mm_nn("attn_q", x4, wq, BF16, tiles=NARROW_K,
                    a_norm=(norm_g_full[1, 1].reshape(1, d), shift, scale))
    kv, h_kv = _mm_nn("attn_kv", x4, wkv, BF16, tiles=NARROW_K,
                      a_norm=(kv_norm_g.reshape(1, d), ada_kv[0], ada_kv[1]))
    zf = _mm_nn("attn_fgate", h_kv, wf_gate, F32)
    fcum, fcum_lanes = _fgate_fwd("fgate_cumsum", zf, b_f)
    fcum_t = fcum_lanes[:n_heads].reshape(n_heads, 1, s)
    o, lse_t = _attn_fwd("attn_fwd", q, kv, fcum_t, n_heads)
    y_a, x5 = _mm_residual("attn_out", o, wo, x4, gate_a)
    x6, save_f3 = ffn_fwd("l1b", x5, 1, 2, wf_in11, wf_out11, None, None)
    dx, loss_part, d_final_g = _final_loss("final_loss", x6, target, final_g.reshape(1, d))

    d_ada = [[[None] * 3 for _ in range(3)] for _ in range(n_layers)]
    d_norm_g = [[None] * 3 for _ in range(n_layers)]
    core = lax.axis_index("c").astype(jnp.int32).reshape(1)
    in_shard, out_shard, row_shard = (d, 2 * f // N_DEV), (f // N_DEV, d), (dsh, d)

    def chip_exchange(tag, items, from_sibling):
        return _ChipComm([_pair_add(f"add_sibling_{tag}_{n}", p, kind, shape, t, core)
                          for n, ((p, kind, shape), t) in enumerate(zip(items, from_sibling))])

    def reduce_within_chip(tag, items):
        return chip_exchange(tag, items, _run_comm(f"reduce_sibling_{tag}", _SiblingComm(items)))

    def ffn_bwd(tag, dxs, dy, saved, l, sub, comm_dw, then):
        xs, h, ab, u, y, w_in, w_out = saved
        tiles = FFN_TILES
        dab = _mm_nt_swiglu_bwd(f"ffn_out_bwd_{tag}", dy, w_out, ab, tiles["out_bwd"])
        dw_in = _mm_tn(f"ffn_in_dw_{tag}", h, dab, BF16, tiles=tiles["in_dw"])
        sibling_in = _SiblingComm([(dw_in, "cols", in_shard)])
        dw_out = _mm_tn(f"ffn_out_dw_{tag}", u, dy, BF16, tiles=tiles["out_dw"], comm=_join(sibling_in, comm_dw))
        sibling_out = _run_comm(f"reduce_sibling_{tag}", _SiblingComm([(dw_out, "rows", out_shard)]))
        scatter = chip_exchange(tag, [(dw_in, "cols", in_shard), (dw_out, "rows", out_shard)],
                                list(sibling_in.results) + list(sibling_out))
        dh = _mm_nt_parts(f"ffn_in_bwd_{tag}", dab, w_in, F32, tiles["in_bwd"], comm=scatter)
        outs = _norm_mod_bwd(f"norm_bwd_{tag}", dh, xs, norm_g_full[l, sub].reshape(1, d), ada[l][sub][1], dxs, then)
        dxs, d_ada[l][sub][0], d_ada[l][sub][1], d_norm_g[l][sub] = outs[:4]
        return dxs, outs[4:], scatter

    def gate_of(saved, l, sub):
        return (saved[4], 0.5 * ada[l][sub][2], 0.5)

    dy, d_ada[1][2][2] = _gate_bwd("gate_bwd_l1b", dx, save_f3[4], 0.5 * ada[1][2][2], 0.5)
    dx, (dy, d_ada[1][1][2]), scatter_l1b = ffn_bwd("l1b", dx, dy, save_f3, 1, 2, None, (y_a, gate_a, 1.0))
    d_o = _mm_nt("attn_out_bwd", dy, wo, F32)
    dwo = _mm_tn("attn_out_dw", o, dy, BF16)
    delta_lanes, d_ob = _attn_delta("attn_delta", d_o, o, n_heads)
    delta_t = delta_lanes[:n_heads].reshape(n_heads, 1, s)
    dq, dkv, dfq, dfk = _attn_bwd("attn_bwd", q, kv, d_ob, fcum, lse_t, delta_t, n_heads)
    dfcum = jnp.pad((dfq + dfk).reshape(n_heads, s).T, ((0, 0), (0, LANES - n_heads)))
    dzf, d_bf = _fgate_bwd("fgate_bwd", dfcum, zf, b_f)
    dh_q = _mm_nt("attn_q_bwd", dq, wq, F32)
    dwq = _mm_tn("attn_q_dw", h_q, dq, BF16)
    dh_kv = _mm_nt_parts("attn_kv_bwd", dkv, wkv, F32)
    dh_kv = _mm_nt("attn_fgate_bwd", dzf, wf_gate, F32, extra_add=dh_kv)
    dwkv = _mm_tn("attn_kv_dw", h_kv, dkv, BF16)
    dwf = _mm_tn("attn_fgate_dw", h_kv, dzf, BF16)
    dwkvf = jnp.concatenate([dwkv, dwf[:, :n_heads]], axis=1)
    p_kvf = jnp.moveaxis(dwkvf.reshape(d, N_DEV, kvf_w), 1, 0)
    scatter_attn = reduce_within_chip(
        "attn", [(dwo, "rows", row_shard), (dwq, "rows", row_shard), (p_kvf, "lead", (d, kvf_w))])
    dx, d_ada[1][1][0], d_ada[1][1][1], d_norm_g[1][1] = _norm_mod_bwd(
        "norm_bwd_q", dh_q, x4, norm_g_full[1, 1].reshape(1, d), ada[1][1][1], dx)
    dx, d_kv_shift, d_kv_scale, d_kv_norm_g, dy, d_ada[1][0][2] = _norm_mod_bwd(
        "norm_bwd_kv", dh_kv, x4, kv_norm_g.reshape(1, d), ada_kv[1], dx, gate_of(save_f2, 1, 0))
    dx, (dy, d_ada[0][2][2]), scatter_l1a = ffn_bwd("l1a", dx, dy, save_f2, 1, 0, scatter_attn, gate_of(save_f1, 0, 2))
    dx, (dy, d_ada[0][1][2]), scatter_l0b = ffn_bwd("l0b", dx, dy, save_f1, 0, 2, None, (y_c, ada[0][1][2], 1.0))
    dz = _mm_nt("conv_out_bwd", dy, wc_out, F32, tiles=NARROW_K)
    dwc_out = _mm_tn("conv_out_dw", z_c, dy, BF16)
    dp, d_conv_w, d_conv_b = _conv_bwd("conv_mix_bwd", dz, p_c, conv_w_full, conv_b_full)
    dh_c = _mm_nt_parts("conv_in_bwd", dp, wc_in, F32)
    dwc_in = _mm_tn("conv_in_dw", h_c, dp, BF16)
    scatter_conv = reduce_within_chip("conv", [(dwc_out, "rows", row_shard), (dwc_in, "cols", (d, 3 * d // N_DEV))])
    dx, d_ada[0][1][0], d_ada[0][1][1], d_norm_g[0][1], dy, d_ada[0][0][2] = _norm_mod_bwd(
        "norm_bwd_conv", dh_c, x1, norm_g_full[0, 1].reshape(1, d), ada[0][1][1], dx, gate_of(save_f0, 0, 0))
    dx, _, scatter_l0a = ffn_bwd("l0a", dx, dy, save_f0, 0, 0, scatter_conv, None)
    grad_x = dx.reshape(1, s, d)

    d_ada_flat = jnp.concatenate([v.reshape(-1) for l in range(n_layers) for sub in range(3) for v in d_ada[l][sub]])
    d_norm_flat = jnp.concatenate([d_norm_g[l][sub].reshape(-1) for l in range(n_layers) for sub in range(3)])
    small_shapes = [(n_layers * 9 * d,), (2 * d,), (n_layers * 3 * d,), (3, d), (d,), (d,), (LANES,), (d,), (LANES,)]
    small_parts = _pack([d_ada_flat, d_kv_shift, d_kv_scale, d_norm_flat, d_conv_w, d_conv_b, d_kv_norm_g, d_bf,
                         d_final_g, loss_part])
    small_all = _allgather_small("gather_small_grads", small_parts)
    small_sum = _sum_slots("sum_small_grads", small_all).reshape(-1)
    g_b_ada, g_b_ada_kv, g_norm_full, g_conv_w_full, g_conv_b_full, g_kv_norm, g_bf, g_final, loss_v = _unpack(
        small_sum, small_shapes)
    loss = loss_v[0]
    d_ada_rows = small_all.reshape(N_DEV, -1)[:, :n_layers * 9 * d + 2 * d]
    d_ada_rows = jnp.pad(d_ada_rows, ((0, 16 - N_DEV), (0, 0)))

    results = {}

    def update(key, parts, w, m, v):
        shape = w.shape
        c_dim = shape[-1]
        outs = _adamw(f"adamw_{key}", parts.reshape(parts.shape[0], -1, c_dim), w.reshape(-1, c_dim),
                      m.reshape(-1, c_dim), v.reshape(-1, c_dim))
        results[key] = [o.reshape(shape) for o in outs]

    def update_rows(key, parts_list, w, m, v):
        c_dim = w.shape[-1]
        flat = [t.reshape(-1, c_dim) for t in (w, m, v)]
        outs = None
        for n, parts in enumerate(parts_list):
            outs = _adamw(f"adamw_{key}_{n}", parts, *flat, row0=n * parts.shape[1], previous=outs)
        results[key] = [o.reshape(w.shape) for o in outs]

    g_w_ada = []
    for l in range(n_layers):
        cols = _my_slice(d_ada_rows[:, l * 9 * d:(l + 1) * 9 * d], me, ada_w, 1)
        g_w_ada.append(_mm_tn(f"ada_dw_{l}", c_rows, cols, F32, a_fn=_silu)[None])
    update_rows("w_ada", g_w_ada, w_ada, m_w_ada, v_w_ada)
    cols = _my_slice(d_ada_rows[:, n_layers * 9 * d:], me, kv_w, 1)
    update("w_ada_kv", _mm_tn("ada_dw_kv", c_rows, cols, F32, a_fn=_silu).reshape(1, d, kv_w),
           w_ada_kv, m_w_ada_kv, v_w_ada_kv)


    ffn_scatters = [scatter_l0a, scatter_l0b, scatter_l1a, scatter_l1b]
    update_rows("w_ffn_in", [sc.results[0] for sc in ffn_scatters], w_ffn_in, m_w_ffn_in, v_w_ffn_in)
    update_rows("w_ffn_out", [sc.results[1] for sc in ffn_scatters], w_ffn_out, m_w_ffn_out, v_w_ffn_out)
    r_co, r_ci = scatter_conv.results
    r_o, r_q, r_kvf = scatter_attn.results
    update_rows("w_conv_in", [r_ci], w_conv_in, m_w_conv_in, v_w_conv_in)
    update_rows("w_conv_out", [r_co], w_conv_out, m_w_conv_out, v_w_conv_out)
    update_rows("w_kvf", [r_kvf], w_kvf, m_w_kvf, v_w_kvf)
    update_rows("w_q", [r_q], w_q, m_w_q, v_w_q)
    update_rows("w_o", [r_o], w_o, m_w_o, v_w_o)

    small_keys = ["norm_g", "b_ada", "conv_w", "conv_b", "kv_norm_g", "b_ada_kv", "b_fgate", "final_g"]
    small_w = [norm_g, b_ada, conv_w, conv_b, kv_norm_g, b_ada_kv, b_fgate, final_g]
    small_m = [m_norm_g, m_b_ada, m_conv_w, m_conv_b, m_kv_norm_g, m_b_ada_kv, m_b_fgate, m_final_g]
    small_v = [v_norm_g, v_b_ada, v_conv_w, v_conv_b, v_kv_norm_g, v_b_ada_kv, v_b_fgate, v_final_g]
    small_g = [
        _my_slice(g_norm_full.reshape(n_layers, 3, d), me, dsh, 2), g_b_ada.reshape(b_ada.shape),
        _my_slice(g_conv_w_full, me, dsh, 1).reshape(conv_w.shape), _my_slice(g_conv_b_full, me, dsh, 0).reshape(
            conv_b.shape), g_kv_norm, g_b_ada_kv, g_bf[:n_heads], g_final]
    packed = _adamw("adamw_small", _pack(small_g)[None], _pack(small_w), _pack(small_m), _pack(small_v))
    for vals, idx in zip(packed, range(4)):
        for key, val in zip(small_keys, _unpack(vals.reshape(-1), [w.shape for w in small_w])):
            results.setdefault(key, [None] * 4)[idx] = val

    order = ["norm_g", "w_ada", "b_ada", "w_ffn_in", "w_ffn_out", "w_conv_in", "conv_w", "conv_b", "w_conv_out",
             "kv_norm_g", "w_ada_kv", "b_ada_kv", "w_kvf", "b_fgate", "w_q", "w_o", "final_g"]
    return (loss, grad_x, *[results[k][0] for k in order], *[results[k][1] for k in order],
            *[results[k][2] for k in order], *[results[k][3] for k in order])
```

```python
import functools
import math

import jax
import jax.numpy as jnp
from jax import lax
from jax.experimental import pallas as pl
from jax.experimental.pallas import tpu as pltpu

F32 = jnp.float32
BF16 = jnp.bfloat16
MESH = pl.DeviceIdType.MESH

N_DEV = 8
LANES = 128
SUBLANES = 8
VMEM_LIMIT = 56 * 2**20
EPS = 1e-6
LOG2E = math.log2(math.e)
ADAM_LR, ADAM_B1, ADAM_B2, ADAM_EPS, ADAM_WD, ADAM_STEP = 0.001, 0.9, 0.999, 1e-08, 0.01, 10

_DN = {"nn": (((1,), (0,)), ((), ())), "nt": (((1,), (1,)), ((), ())), "tn": (((0,), (0,)), ((), ()))}


def _tile(n, pref, align):
    t = min(pref, n)
    t -= t % align
    while t >= align:
        if n % t == 0:
            return t
        t -= align
    return n


def _params(n_grid):
    return pltpu.CompilerParams(vmem_limit_bytes=VMEM_LIMIT, dimension_semantics=("arbitrary",) * n_grid)


def _sigmoid(v):
    return 0.5 * jnp.tanh(0.5 * v) + 0.5


def _silu(v):
    return v * _sigmoid(v)


def _position():
    x, y, c = lax.axis_index("x"), lax.axis_index("y"), lax.axis_index("c")
    return x, y, c


def _allgather_small(name, v):
    def body(v_ref, out_ref, send_sems, recv_sems, local_sem):
        x, y, c = _position()
        me, sibling = (x, y, c), (x, y, 1 - c)
        chips = [(1 - x, y), (x, 1 - y), (1 - x, 1 - y)]

        def slot(px, py, pc):
            return out_ref.at[4 * px + 2 * py + pc]

        def copy(k, block, to, src=None):
            return pltpu.make_async_remote_copy(
                src_ref=slot(*block) if src is None else src, dst_ref=slot(*block),
                send_sem=send_sems.at[k], recv_sem=recv_sems.at[k], device_id=to, device_id_type=MESH)

        mine = pltpu.make_async_copy(v_ref, slot(*me), local_sem)
        mine.start()
        first = [copy(0, me, sibling, src=v_ref)]
        first += [copy(1 + j, me, (*chip, c), src=v_ref) for j, chip in enumerate(chips)]
        for cp in first:
            cp.start()
        passed = [copy(4 + j, (*chip, c), sibling) for j, chip in enumerate(chips)]
        for j, chip in enumerate(chips):
            copy(1 + j, (*chip, c), me).wait_recv()
            passed[j].start()
        copy(0, sibling, me).wait_recv()
        for j, chip in enumerate(chips):
            copy(4 + j, (*chip, 1 - c), me).wait_recv()
        for cp in first + passed:
            cp.wait_send()
        mine.wait()

    return pl.pallas_call(
        body, name=name,
        out_shape=jax.ShapeDtypeStruct((N_DEV,) + v.shape, v.dtype),
        in_specs=[pl.BlockSpec(memory_space=pltpu.VMEM)],
        out_specs=pl.BlockSpec(memory_space=pltpu.VMEM),
        scratch_shapes=[pltpu.SemaphoreType.DMA((7,)), pltpu.SemaphoreType.DMA((7,)), pltpu.SemaphoreType.DMA],
        compiler_params=pltpu.CompilerParams(vmem_limit_bytes=VMEM_LIMIT),
    )(v)


def _owner_view(ref, kind, shard_shape, owner):
    r, w = shard_shape
    if kind == "rows":
        return ref.at[pl.ds(pl.multiple_of(owner * r, 16), r)]
    if kind == "cols":
        return ref.at[:, pl.ds(pl.multiple_of(owner * w, LANES), w)]
    return ref.at[owner]


def _full_shape(kind, shard_shape):
    r, w = shard_shape
    return {"rows": (N_DEV * r, w), "cols": (r, N_DEV * w), "lead": (N_DEV, r, w)}[kind]


class _GatherComm:
    mid_fraction = 0.85

    def __init__(self, items):
        self.ins = [shard for shard, _ in items]
        self.kinds = [kind for _, kind in items]
        n = len(items)
        self.out_shapes = [jax.ShapeDtypeStruct(_full_shape(k, a.shape), a.dtype) for a, k in items]
        self.sem_shapes = [pltpu.SemaphoreType.DMA((7 * n,)), pltpu.SemaphoreType.DMA((7 * n,)),
                           pltpu.SemaphoreType.DMA((n,))]
        self.results = None

    def _copy(self, refs, a, k, block, to, from_input=False):
        ins, outs, (send_sems, recv_sems, _) = refs
        px, py, pc = block
        dst = _owner_view(outs[a], self.kinds[a], self.ins[a].shape, 4 * px + 2 * py + pc)
        return pltpu.make_async_remote_copy(
            src_ref=ins[a] if from_input else dst, dst_ref=dst,
            send_sem=send_sems.at[7 * a + k], recv_sem=recv_sems.at[7 * a + k], device_id=to, device_id_type=MESH)

    def _local(self, refs, a):
        ins, outs, (_, _, local_sems) = refs
        x, y, c = _position()
        dst = _owner_view(outs[a], self.kinds[a], self.ins[a].shape, 4 * x + 2 * y + c)
        return pltpu.make_async_copy(ins[a], dst, local_sems.at[a])

    def start(self, *refs):
        x, y, c = _position()
        me, sibling = (x, y, c), (x, y, 1 - c)
        chips = [(1 - x, y), (x, 1 - y), (1 - x, 1 - y)]
        for a in range(len(self.ins)):
            self._local(refs, a).start()
            for j, chip in enumerate(chips):
                self._copy(refs, a, 1 + j, me, (*chip, c), from_input=True).start()
            self._copy(refs, a, 0, me, sibling, from_input=True).start()

    def mid(self, *refs):
        x, y, c = _position()
        chips = [(1 - x, y), (x, 1 - y), (1 - x, 1 - y)]
        for a in range(len(self.ins)):
            for j, chip in enumerate(chips):
                self._copy(refs, a, 1 + j, (*chip, c), (x, y, c)).wait_recv()
                self._copy(refs, a, 4 + j, (*chip, c), (x, y, 1 - c)).start()

    def finish(self, *refs):
        x, y, c = _position()
        me, sibling = (x, y, c), (x, y, 1 - c)
        chips = [(1 - x, y), (x, 1 - y), (1 - x, 1 - y)]
        for a in range(len(self.ins)):
            self._copy(refs, a, 0, sibling, me).wait_recv()
            for j, chip in enumerate(chips):
                self._copy(refs, a, 4 + j, (*chip, 1 - c), me).wait_recv()
        for a in range(len(self.ins)):
            self._copy(refs, a, 0, me, sibling, from_input=True).wait_send()
            for j, chip in enumerate(chips):
                self._copy(refs, a, 1 + j, me, (*chip, c), from_input=True).wait_send()
                self._copy(refs, a, 4 + j, (*chip, c), sibling).wait_send()
            self._local(refs, a).wait()


class _SiblingComm:
    mid_fraction = None

    def __init__(self, items):
        self.ins = [p for p, _, _ in items]
        self.kinds = [kind for _, kind, _ in items]
        self.shapes = [shape for _, _, shape in items]
        n = len(items)
        self.out_shapes = [jax.ShapeDtypeStruct((4,) + tuple(shape), p.dtype) for p, _, shape in items]
        self.sem_shapes = [pltpu.SemaphoreType.DMA((4 * n,)), pltpu.SemaphoreType.DMA((4 * n,))]
        self.results = None

    def _copies(self, refs):
        ins, outs, (send_sems, recv_sems) = refs
        x, y, c = _position()
        return [pltpu.make_async_remote_copy(
            src_ref=_owner_view(ins[a], self.kinds[a], self.shapes[a], 2 * k + 1 - c), dst_ref=outs[a].at[k],
            send_sem=send_sems.at[4 * a + k], recv_sem=recv_sems.at[4 * a + k],
            device_id=(x, y, 1 - c), device_id_type=MESH) for a in range(len(self.ins)) for k in range(4)]

    def start(self, *refs):
        for cp in self._copies(refs):
            cp.start()

    def finish(self, *refs):
        for cp in self._copies(refs):
            cp.wait()


class _ChipComm:
    mid_fraction = None

    def __init__(self, items):
        self.ins = list(items)
        n = len(items)
        self.out_shapes = [jax.ShapeDtypeStruct(p.shape, p.dtype) for p in items]
        self.sem_shapes = [pltpu.SemaphoreType.DMA((3 * n,)), pltpu.SemaphoreType.DMA((3 * n,)),
                           pltpu.SemaphoreType.DMA((n,))]
        self.results = None

    def _copy(self, refs, a, k, landed):
        ins, outs, (send_sems, recv_sems, _) = refs
        x, y, c = _position()
        px = 1 - x if k & 2 else x
        py = 1 - y if k & 1 else y
        to_chip, my_chip = 2 * px + py, 2 * x + y
        return pltpu.make_async_remote_copy(
            src_ref=ins[a].at[to_chip], dst_ref=outs[a].at[to_chip if landed else my_chip],
            send_sem=send_sems.at[3 * a + k - 1], recv_sem=recv_sems.at[3 * a + k - 1],
            device_id=(px, py, c), device_id_type=MESH)

    def _local(self, refs, a):
        ins, outs, (_, _, local_sems) = refs
        x, y, _ = _position()
        return pltpu.make_async_copy(ins[a].at[2 * x + y], outs[a].at[2 * x + y], local_sems.at[a])

    def start(self, *refs):
        for a in range(len(self.ins)):
            self._local(refs, a).start()
            for k in range(1, 4):
                self._copy(refs, a, k, False).start()

    def finish(self, *refs):
        for a in range(len(self.ins)):
            for k in range(1, 4):
                self._copy(refs, a, k, True).wait_recv()
        for a in range(len(self.ins)):
            for k in range(1, 4):
                self._copy(refs, a, k, False).wait_send()
            self._local(refs, a).wait()


class _Joined:
    def __init__(self, comms):
        self.comms = list(comms)
        self.ins = [a for c in self.comms for a in c.ins]
        self.out_shapes = [o for c in self.comms for o in c.out_shapes]
        self.sem_shapes = [o for c in self.comms for o in c.sem_shapes]
        fractions = [c.mid_fraction for c in self.comms if c.mid_fraction is not None]
        self.mid_fraction = max(fractions) if fractions else None

    def _parts(self, refs):
        ins, outs, sems = refs
        pos = [0, 0, 0]
        for c in self.comms:
            n = (len(c.ins), len(c.out_shapes), len(c.sem_shapes))
            yield c, (ins[pos[0]:pos[0] + n[0]], outs[pos[1]:pos[1] + n[1]], sems[pos[2]:pos[2] + n[2]])
            pos = [p + k for p, k in zip(pos, n)]

    def start(self, *refs):
        for c, part in self._parts(refs):
            c.start(*part)

    def mid(self, *refs):
        for c, part in self._parts(refs):
            if c.mid_fraction is not None:
                c.mid(*part)

    def finish(self, *refs):
        for c, part in self._parts(refs):
            c.finish(*part)

    @property
    def results(self):
        return [r for c in self.comms for r in c.results]

    @results.setter
    def results(self, values):
        pos = 0
        for c in self.comms:
            c.results = values[pos:pos + len(c.out_shapes)]
            pos += len(c.out_shapes)


def _join(*comms):
    present = [c for c in comms if c is not None]
    return present[0] if len(present) == 1 else (_Joined(present) if present else None)


def _run_comm(name, comm):
    n_in, n_out = len(comm.ins), len(comm.out_shapes)

    def body(*refs):
        parts = (refs[:n_in], refs[n_in:n_in + n_out], refs[n_in + n_out:])
        comm.start(*parts)
        if comm.mid_fraction is not None:
            comm.mid(*parts)
        comm.finish(*parts)

    any_spec = pl.BlockSpec(memory_space=pl.ANY)
    return pl.pallas_call(
        body, name=name, out_shape=list(comm.out_shapes), in_specs=[any_spec] * n_in, out_specs=[any_spec] * n_out,
        scratch_shapes=list(comm.sem_shapes))(*comm.ins)


def _carry(comm, body, grid, in_specs, out_specs, out_shapes, scratch_shapes, operands):
    if comm is None:
        return body, in_specs, out_specs, out_shapes, scratch_shapes, operands
    n_in, n_out, n_scr = len(in_specs), len(out_specs), len(scratch_shapes)
    c_in, c_out = len(comm.ins), len(comm.out_shapes)
    total = math.prod(grid)
    mid_step = None if comm.mid_fraction is None else min(total - 1, int(total * comm.mid_fraction))

    def wrapped(*refs):
        ins, c_ins = refs[:n_in], refs[n_in:n_in + c_in]
        outs = refs[n_in + c_in:n_in + c_in + n_out]
        c_outs = refs[n_in + c_in + n_out:n_in + c_in + n_out + c_out]
        scr = refs[n_in + c_in + n_out + c_out:n_in + c_in + n_out + c_out + n_scr]
        parts = (c_ins, c_outs, refs[n_in + c_in + n_out + c_out + n_scr:])
        step = pl.program_id(0)
        for axis in range(1, len(grid)):
            step = step * grid[axis] + pl.program_id(axis)

        @pl.when(step == 0)
        def _():
            comm.start(*parts)

        body(*ins, *outs, *scr)

        if mid_step is not None:
            @pl.when(step == mid_step)
            def _():
                comm.mid(*parts)

        @pl.when(step == total - 1)
        def _():
            comm.finish(*parts)

    any_spec = pl.BlockSpec(memory_space=pl.ANY)
    return (wrapped, list(in_specs) + [any_spec] * c_in, list(out_specs) + [any_spec] * c_out,
            list(out_shapes) + list(comm.out_shapes), list(scratch_shapes) + list(comm.sem_shapes),
            list(operands) + list(comm.ins))


def _pair_add(name, partial, kind, shard_shape, theirs, core):
    r, w = shard_shape
    tb = _tile(r, max(16, (2**19 // w) // 16 * 16), 16)
    nb = r // tb
    if kind == "rows":
        mine = pl.BlockSpec((tb, w), lambda k, i, core_ref: ((2 * k + core_ref[0]) * nb + i, 0))
    elif kind == "cols":
        mine = pl.BlockSpec((tb, w), lambda k, i, core_ref: (i, 2 * k + core_ref[0]))
    else:
        mine = pl.BlockSpec((None, tb, w), lambda k, i, core_ref: (2 * k + core_ref[0], i, 0))
    slot = pl.BlockSpec((None, tb, w), lambda k, i, core_ref: (k, i, 0))

    def body(core_ref, a_ref, b_ref, o_ref):
        o_ref[...] = (a_ref[...].astype(F32) + b_ref[...].astype(F32)).astype(o_ref.dtype)

    return pl.pallas_call(
        body, name=name,
        grid_spec=pltpu.PrefetchScalarGridSpec(num_scalar_prefetch=1, grid=(4, nb), in_specs=[mine, slot],
                                               out_specs=slot),
        out_shape=jax.ShapeDtypeStruct(theirs.shape, theirs.dtype), compiler_params=_params(2))(core, partial, theirs)


def _matmul(name, mode, grid, a, a_spec, bs, b_specs, extras, extra_specs, out_shapes, out_specs, acc_shape,
            epilogue, a_fn=None, comm=None, a_norm=None):
    nk, nb, ne, no = grid[2], len(bs), len(extras), len(out_shapes)
    manual_a = (nk == 1 and grid[0] > 1 and grid[1] > 1 and mode != "tn" and a_fn is None
                and len(a_spec.block_shape) == 2)
    a_block = a_spec.block_shape
    if a_norm is not None:
        assert nk == 1 and mode != "tn"
        tm_a, k_a = a_spec.block_shape
        vec = pl.BlockSpec((1, k_a), lambda i, j, kk: (0, 0))
        extras, extra_specs = list(extras) + list(a_norm), list(extra_specs) + [vec] * 3
        out_shapes = list(out_shapes) + [jax.ShapeDtypeStruct(a.shape, BF16)]
        out_specs = list(out_specs) + [pl.BlockSpec((tm_a, k_a), lambda i, j, kk: (i, 0))]

    def body(*refs):
        a_ref, b_refs, e_refs = refs[0], refs[1:1 + nb], refs[1 + nb:1 + nb + ne]
        o_refs, acc_refs = refs[1 + nb + ne:1 + nb + ne + no], refs[1 + nb + ne + no:]
        if manual_a:
            a_buf, a_sem = refs[-2], refs[-1]
            i, j = pl.program_id(0), pl.program_id(1)
            slot = lax.rem(i, 2)

            def fetch(block, into):
                rows = pl.ds(pl.multiple_of(block * a_block[0], a_block[0]), a_block[0])
                return pltpu.make_async_copy(a_ref.at[rows, :], a_buf.at[into], a_sem.at[into])

            @pl.when((i == 0) & (j == 0))
            def _():
                fetch(0, 0).start()

            @pl.when((j == 0) & (i + 1 < grid[0]))
            def _():
                fetch(i + 1, 1 - slot).start()

            @pl.when(j == 0)
            def _():
                fetch(i, slot).wait()

            a_ref = a_buf.at[slot]
        if a_norm is None:
            a_val = a_ref[...]
            if a_fn is not None:
                a_val = a_fn(a_val)
            a_val = a_val.astype(BF16)
        else:
            g_ref, sh_ref, sc_ref = refs[1 + nb + ne:4 + nb + ne]
            o_refs = refs[4 + nb + ne:4 + nb + ne + no]
            h_ref, h_scr = refs[4 + nb + ne + no], refs[5 + nb + ne + no]

            @pl.when(pl.program_id(1) == 0)
            def _():
                xv = a_ref[...]
                rstd = lax.rsqrt(jnp.mean(xv * xv, axis=-1, keepdims=True) + EPS)
                hv = ((xv * rstd) * g_ref[...] * (1.0 + sc_ref[...]) + sh_ref[...]).astype(BF16)
                h_scr[...] = hv
                h_ref[...] = hv

            a_val = h_scr[...]

        def product(b_ref):
            return lax.dot_general(a_val, b_ref[...].astype(BF16), _DN[mode], preferred_element_type=F32)

        def finish(accs):
            outs = epilogue(accs, [e[...] for e in e_refs])
            for o_ref, o in zip(o_refs, outs):
                o_ref[...] = o.astype(o_ref.dtype)

        if nk == 1:
            finish([product(b_ref) for b_ref in b_refs])
        else:
            kk = pl.program_id(2)

            @pl.when(kk == 0)
            def _():
                for acc in acc_refs:
                    acc[...] = jnp.zeros_like(acc)

            for acc, b_ref in zip(acc_refs, b_refs):
                acc[...] += product(b_ref)

            @pl.when(kk == nk - 1)
            def _():
                finish([acc[...] for acc in acc_refs])

    scratch = [] if nk == 1 else [pltpu.VMEM(acc_shape, F32) for _ in range(nb)]
    if a_norm is not None:
        scratch = [pltpu.VMEM(a_spec.block_shape, BF16)]
    if manual_a:
        scratch = scratch + [pltpu.VMEM((2,) + tuple(a_block), a.dtype), pltpu.SemaphoreType.DMA((2,))]
        a_spec = pl.BlockSpec(memory_space=pl.ANY)
    body, in_specs, out_specs, out_shape, scratch, operands = _carry(
        comm, body, grid, [a_spec] + list(b_specs) + list(extra_specs), list(out_specs), list(out_shapes), scratch,
        [a, *bs, *extras])
    outs = pl.pallas_call(
        body, name=name, grid=grid, in_specs=in_specs, out_specs=out_specs, out_shape=out_shape,
        scratch_shapes=scratch, compiler_params=_params(3))(*operands)
    n_mine = len(out_shapes)
    if comm is not None:
        comm.results = outs[n_mine:]
    return outs[:n_mine]


def _mm_tiles(m, n, k, tiles):
    return _tile(m, tiles[0], 16), _tile(n, tiles[1], LANES), _tile(k, tiles[2], LANES)


FFN_TILES = {"in": (1024, 512, 2048), "out": (1024, 512, 5632), "out_bwd": (2048, 512, 2048),
             "in_bwd": (1024, 1024, 2816), "out_dw": (512, 2048, 2048), "in_dw": (1024, 512, 4096)}
FULL_K = (1024, 1024, 2048)
LOOP_K = (1024, 1024, 2048)


def _mm_nn(name, a, b, out_dtype, b_col0=0, n=None, tiles=FULL_K, comm=None, a_norm=None):
    m, k = a.shape
    n = b.shape[1] if n is None else n
    tm, tn, tk = _mm_tiles(m, n, k, tiles)
    off = b_col0 // tn
    assert b_col0 % tn == 0
    outs = _matmul(
        name, "nn", (m // tm, n // tn, k // tk), a, pl.BlockSpec((tm, tk), lambda i, j, kk: (i, kk)),
        [b], [pl.BlockSpec((tk, tn), lambda i, j, kk: (kk, j + off))], [], [],
        [jax.ShapeDtypeStruct((m, n), out_dtype)], [pl.BlockSpec((tm, tn), lambda i, j, kk: (i, j))], (tm, tn),
        lambda accs, ex: accs[:1], comm=comm, a_norm=a_norm)
    return outs[0] if a_norm is None else outs


def _mm_bias_rows(name, a, b, bias, a_fn, b_block=0):
    m, k = a.shape
    n = b.shape[1]
    tn = _tile(n, 768, LANES)
    return _matmul(
        name, "nn", (1, n // tn, 1), a, pl.BlockSpec((m, k), lambda i, j, kk: (0, 0)),
        [b], [pl.BlockSpec((k, tn), lambda i, j, kk: (b_block, j))],
        [bias], [pl.BlockSpec((1, tn), lambda i, j, kk: (0, j))],
        [jax.ShapeDtypeStruct((m, n), F32)], [pl.BlockSpec((m, tn), lambda i, j, kk: (0, j))], (m, tn),
        lambda accs, ex: [accs[0] + ex[0]], a_fn=a_fn)[0]


def _mm_swiglu(name, h, w_in, f, tiles, comm=None, a_norm=None):
    s, d = h.shape
    tm, tn, tk = _mm_tiles(s, f, d, tiles)
    nf = f // tn

    def epilogue(accs, ex):
        a_pre, b_pre = accs
        return [jnp.stack([a_pre, b_pre]), _silu(a_pre) * b_pre]

    return _matmul(
        name, "nn", (s // tm, nf, d // tk), h, pl.BlockSpec((tm, tk), lambda i, j, kk: (i, kk)),
        [w_in, w_in], [pl.BlockSpec((tk, tn), lambda i, j, kk: (kk, j)),
                       pl.BlockSpec((tk, tn), lambda i, j, kk: (kk, j + nf))], [], [],
        [jax.ShapeDtypeStruct((2, s, f), BF16), jax.ShapeDtypeStruct((s, f), BF16)],
        [pl.BlockSpec((2, tm, tn), lambda i, j, kk: (0, i, j)), pl.BlockSpec((tm, tn), lambda i, j, kk: (i, j))],
        (tm, tn), epilogue, comm=comm, a_norm=a_norm)


def _mm_residual(name, a, w, x_in, gvec, tiles=FULL_K, comm=None):
    s, k = a.shape
    d = w.shape[1]
    tm, tn, tk = _mm_tiles(s, d, k, tiles)

    def epilogue(accs, ex):
        x_blk, g_row = ex
        return [accs[0], x_blk + g_row * accs[0]]

    tile = pl.BlockSpec((tm, tn), lambda i, j, kk: (i, j))
    return _matmul(
        name, "nn", (s // tm, d // tn, k // tk), a, pl.BlockSpec((tm, tk), lambda i, j, kk: (i, kk)),
        [w], [pl.BlockSpec((tk, tn), lambda i, j, kk: (kk, j))],
        [x_in, gvec], [tile, pl.BlockSpec((1, tn), lambda i, j, kk: (0, j))],
        [jax.ShapeDtypeStruct((s, d), BF16), jax.ShapeDtypeStruct((s, d), F32)], [tile, tile], (tm, tn), epilogue,
        comm=comm)


def _mm_nt(name, a, w, out_dtype, w_col0=0, extra_add=None, tiles=FULL_K):
    s, k = a.shape
    n = w.shape[0]
    tm, tn, tk = _mm_tiles(s, n, k, tiles)
    off = w_col0 // tk
    assert w_col0 % tk == 0
    tile = pl.BlockSpec((tm, tn), lambda i, j, kk: (i, j))
    extras, especs = ([], []) if extra_add is None else ([extra_add], [tile])
    return _matmul(
        name, "nt", (s // tm, n // tn, k // tk), a, pl.BlockSpec((tm, tk), lambda i, j, kk: (i, kk)),
        [w], [pl.BlockSpec((tn, tk), lambda i, j, kk: (j, kk + off))], extras, especs,
        [jax.ShapeDtypeStruct((s, n), out_dtype)], [tile], (tm, tn),
        lambda accs, ex: [accs[0] + ex[0]] if ex else accs)[0]


def _mm_nt_parts(name, a3, w, out_dtype, tiles=LOOP_K, comm=None):
    p, s, k = a3.shape
    n = w.shape[0]
    tm, tn, tk = _mm_tiles(s, n, k, tiles)
    nkp = k // tk
    return _matmul(
        name, "nt", (s // tm, n // tn, p * nkp), a3,
        pl.BlockSpec((None, tm, tk), lambda i, j, kk: (kk // nkp, i, kk % nkp)),
        [w], [pl.BlockSpec((tn, tk), lambda i, j, kk: (j, kk))], [], [],
        [jax.ShapeDtypeStruct((s, n), out_dtype)], [pl.BlockSpec((tm, tn), lambda i, j, kk: (i, j))], (tm, tn),
        lambda accs, ex: accs, comm=comm)[0]


def _mm_nt_swiglu_bwd(name, dy, w_out, ab, tiles):
    s, d = dy.shape
    f = w_out.shape[0]
    tm, tn, tk = _mm_tiles(s, f, d, tiles)

    def epilogue(accs, ex):
        du, a_pre, b_pre = accs[0], ex[0][0].astype(F32), ex[0][1].astype(F32)
        sig = _sigmoid(a_pre)
        da = du * b_pre * (sig * (1.0 + a_pre * (1.0 - sig)))
        db = du * (a_pre * sig)
        return [jnp.stack([da, db])]

    pair = pl.BlockSpec((2, tm, tn), lambda i, j, kk: (0, i, j))
    return _matmul(
        name, "nt", (s // tm, f // tn, d // tk), dy, pl.BlockSpec((tm, tk), lambda i, j, kk: (i, kk)),
        [w_out], [pl.BlockSpec((tn, tk), lambda i, j, kk: (j, kk))], [ab], [pair],
        [jax.ShapeDtypeStruct((2, s, f), BF16)], [pair], (tm, tn), epilogue)[0]


def _mm_tn(name, a, b, out_dtype, a_fn=None, tiles=LOOP_K, comm=None):
    s, m = a.shape
    tm, tk = _tile(m, tiles[0], LANES), _tile(s, tiles[2], 16)
    if b.ndim == 3:
        npart = b.shape[2]
        n = b.shape[0] * npart
        tn = _tile(npart, tiles[1], LANES)
        nj = npart // tn
        b_spec = pl.BlockSpec((None, tk, tn), lambda i, j, kk: (j // nj, kk, j % nj))
    else:
        n = b.shape[1]
        tn = _tile(n, tiles[1], LANES)
        b_spec = pl.BlockSpec((tk, tn), lambda i, j, kk: (kk, j))
    return _matmul(
        name, "tn", (m // tm, n // tn, s // tk), a, pl.BlockSpec((tk, tm), lambda i, j, kk: (kk, i)),
        [b], [b_spec], [], [],
        [jax.ShapeDtypeStruct((m, n), out_dtype)], [pl.BlockSpec((tm, tn), lambda i, j, kk: (i, j))], (tm, tn),
        lambda accs, ex: accs, a_fn=a_fn, comm=comm)[0]


def _row_tile(s):
    return _tile(s, 256, 16)


def _fold_rows(v):
    return jnp.sum(v.reshape(v.shape[0] // SUBLANES, SUBLANES, v.shape[1]), axis=0)


def _accumulate(i, acc_ref, val):
    @pl.when(i == 0)
    def _():
        acc_ref[...] = val

    @pl.when(i > 0)
    def _():
        acc_ref[...] += val


def _rms_bwd(xv, dn_g, d):
    rstd = lax.rsqrt(jnp.mean(xv * xv, axis=-1, keepdims=True) + EPS)
    xhat = xv * rstd
    dx = rstd * (dn_g - xhat * (jnp.sum(dn_g * xhat, axis=-1, keepdims=True) * (1.0 / d)))
    return xhat, dx


def _norm_mod_bwd(name, dh, x, g, scale, dx_in, then=None):
    s, d = x.shape
    tb = _row_tile(s)
    nsteps = s // tb
    coef = None if then is None else then[2]

    def body(*refs):
        dh_ref, x_ref, g_ref, sc_ref, dxin_ref = refs[:5]
        if then is None:
            dx_ref, dsh_ref, dsc_ref, dg_ref, a_sh, a_sc, a_g = refs[5:]
        else:
            y_ref, gv_ref, dx_ref, dsh_ref, dsc_ref, dg_ref, dy_ref, dgate_ref, a_sh, a_sc, a_g, a_gate = refs[5:]
        i = pl.program_id(0)
        dhv, gv = dh_ref[...], g_ref[...]
        dn = dhv * (1.0 + sc_ref[...])
        xhat, dx = _rms_bwd(x_ref[...], dn * gv, d)
        dx_total = dxin_ref[...] + dx
        dx_ref[...] = dx_total
        _accumulate(i, a_sh, _fold_rows(dhv))
        _accumulate(i, a_sc, _fold_rows(dhv * (xhat * gv)))
        _accumulate(i, a_g, _fold_rows(dn * xhat))
        if then is not None:
            dy_ref[...] = (dx_total * gv_ref[...]).astype(BF16)
            _accumulate(i, a_gate, _fold_rows(dx_total * y_ref[...].astype(F32)))

        @pl.when(i == nsteps - 1)
        def _():
            dsh_ref[...] = jnp.sum(a_sh[...], axis=0, keepdims=True)
            dsc_ref[...] = jnp.sum(a_sc[...], axis=0, keepdims=True)
            dg_ref[...] = jnp.sum(a_g[...], axis=0, keepdims=True)
            if then is not None:
                dgate_ref[...] = coef * jnp.sum(a_gate[...], axis=0, keepdims=True)

    row = pl.BlockSpec((tb, d), lambda i: (i, 0))
    vec = pl.BlockSpec((1, d), lambda i: (0, 0))
    vshape = jax.ShapeDtypeStruct((1, d), F32)
    in_specs, operands = [row, row, vec, vec, row], [dh, x, g, scale, dx_in]
    out_specs, out_shape = [row, vec, vec, vec], [jax.ShapeDtypeStruct((s, d), F32), vshape, vshape, vshape]
    n_acc = 3
    if then is not None:
        in_specs, operands = in_specs + [row, vec], operands + [then[0], then[1]]
        out_specs, out_shape = out_specs + [row, vec], out_shape + [jax.ShapeDtypeStruct((s, d), BF16), vshape]
        n_acc = 4
    return pl.pallas_call(
        body, name=name, grid=(nsteps,), in_specs=in_specs, out_specs=out_specs, out_shape=out_shape,
        scratch_shapes=[pltpu.VMEM((SUBLANES, d), F32)] * n_acc, compiler_params=_params(1))(*operands)


def _gate_bwd(name, dx, y, gvec, coef):
    s, d = dx.shape
    tb = _row_tile(s)
    nsteps = s // tb

    def body(dx_ref, y_ref, g_ref, dy_ref, dg_ref, acc):
        i = pl.program_id(0)
        dxv = dx_ref[...]
        dy_ref[...] = (dxv * g_ref[...]).astype(BF16)
        _accumulate(i, acc, _fold_rows(dxv * y_ref[...].astype(F32)))

        @pl.when(i == nsteps - 1)
        def _():
            dg_ref[...] = coef * jnp.sum(acc[...], axis=0, keepdims=True)

    row = pl.BlockSpec((tb, d), lambda i: (i, 0))
    vec = pl.BlockSpec((1, d), lambda i: (0, 0))
    return pl.pallas_call(
        body, name=name, grid=(nsteps,), in_specs=[row, row, vec], out_specs=[row, vec],
        out_shape=[jax.ShapeDtypeStruct((s, d), BF16), jax.ShapeDtypeStruct((1, d), F32)],
        scratch_shapes=[pltpu.VMEM((SUBLANES, d), F32)], compiler_params=_params(1))(dx, y, gvec)


def _final_loss(name, x, target, g):
    s, d = x.shape
    tb = _row_tile(s)
    nsteps = s // tb

    def body(x_ref, t_ref, g_ref, dx_ref, loss_ref, dg_ref, a_loss, a_g):
        i = pl.program_id(0)
        xv, gv = x_ref[...], g_ref[...]
        rstd = lax.rsqrt(jnp.mean(xv * xv, axis=-1, keepdims=True) + EPS)
        xhat = xv * rstd
        err = xhat * gv - t_ref[...]
        dyv = err * (1.0 / d)
        dn_g = dyv * gv
        dx_ref[...] = rstd * (dn_g - xhat * (jnp.sum(dn_g * xhat, axis=-1, keepdims=True) * (1.0 / d)))
        _accumulate(i, a_loss, _fold_rows(err * err))
        _accumulate(i, a_g, _fold_rows(dyv * xhat))

        @pl.when(i == nsteps - 1)
        def _():
            total = jnp.sum(jnp.sum(a_loss[...], axis=0, keepdims=True), axis=1, keepdims=True) * (0.5 / d)
            loss_ref[...] = jnp.broadcast_to(total, (1, LANES))
            dg_ref[...] = jnp.sum(a_g[...], axis=0, keepdims=True)

    row = pl.BlockSpec((tb, d), lambda i: (i, 0))
    vec = pl.BlockSpec((1, d), lambda i: (0, 0))
    return pl.pallas_call(
        body, name=name, grid=(nsteps,), in_specs=[row, row, vec],
        out_specs=[row, pl.BlockSpec((1, LANES), lambda i: (0, 0)), vec],
        out_shape=[jax.ShapeDtypeStruct((s, d), F32), jax.ShapeDtypeStruct((1, LANES), F32),
                   jax.ShapeDtypeStruct((1, d), F32)],
        scratch_shapes=[pltpu.VMEM((SUBLANES, d), F32)] * 2, compiler_params=_params(1))(x, target, g)


def _shift_rows(cur, halo, n):
    rolled = pltpu.roll(cur, n, 0)
    row = lax.broadcasted_iota(jnp.int32, cur.shape, 0)
    for r in range(n):
        rolled = jnp.where(row == r, halo[SUBLANES - n + r:SUBLANES - n + r + 1, :], rolled)
    return rolled


def _shift_rows_up(cur, halo, n):
    tb = cur.shape[0]
    rolled = pltpu.roll(cur, tb - n, 0)
    row = lax.broadcasted_iota(jnp.int32, cur.shape, 0)
    for r in range(n):
        rolled = jnp.where(row == tb - n + r, halo[r:r + 1, :], rolled)
    return rolled


def _conv_specs(s, d, tb, tn):
    nd = d // tn
    hb = tb // SUBLANES
    cur = lambda part: pl.BlockSpec((tb, tn), lambda i, j: (i, j + part * nd))
    prev = lambda part: pl.BlockSpec((SUBLANES, tn), lambda i, j: (jnp.maximum(i * hb - 1, 0), j + part * nd))
    nxt = lambda part: pl.BlockSpec((SUBLANES, tn), lambda i, j: (jnp.minimum((i + 1) * hb, s // SUBLANES - 1),
                                                                  j + part * nd))
    return cur, prev, nxt


def _conv_fwd(name, p, conv_w, conv_b):
    s, d = p.shape[0], p.shape[1] // 3
    tb, tn = _row_tile(s), _tile(d, 2048, LANES)
    cur, prev, _ = _conv_specs(s, d, tb, tn)

    def body(bg_ref, cg_ref, xv_ref, cgp_ref, xvp_ref, w_ref, b_ref, z_ref):
        i = pl.program_id(0)
        u = cg_ref[...] * xv_ref[...]
        up = jnp.where(i > 0, cgp_ref[...] * xvp_ref[...], 0.0)
        w = w_ref[...]
        uc = w[0:1, :] * _shift_rows(u, up, 2) + w[1:2, :] * _shift_rows(u, up, 1) + w[2:3, :] * u + b_ref[...]
        z_ref[...] = (bg_ref[...] * uc).astype(BF16)

    return pl.pallas_call(
        body, name=name, grid=(s // tb, d // tn),
        in_specs=[cur(0), cur(1), cur(2), prev(1), prev(2), pl.BlockSpec((3, tn), lambda i, j: (0, j)),
                  pl.BlockSpec((1, tn), lambda i, j: (0, j))],
        out_specs=pl.BlockSpec((tb, tn), lambda i, j: (i, j)),
        out_shape=jax.ShapeDtypeStruct((s, d), BF16), compiler_params=_params(2))(p, p, p, p, p, conv_w, conv_b)


def _conv_bwd(name, dz, p, conv_w, conv_b):
    s, d = dz.shape
    tb, tn = _row_tile(s), _tile(d, 2048, LANES)
    nsteps = s // tb
    cur, prev, nxt = _conv_specs(s, d, tb, tn)

    def body(dz_ref, bg_ref, cg_ref, xv_ref, cgp_ref, xvp_ref, dzn_ref, bgn_ref, w_ref, b_ref,
             dp_ref, dw_ref, db_ref, a_w0, a_w1, a_w2, a_b):
        j, i = pl.program_id(0), pl.program_id(1)
        cg, xv, bg, dzv = cg_ref[...], xv_ref[...], bg_ref[...], dz_ref[...]
        u = cg * xv
        up = jnp.where(i > 0, cgp_ref[...] * xvp_ref[...], 0.0)
        w = w_ref[...]
        u1, u2 = _shift_rows(u, up, 1), _shift_rows(u, up, 2)
        uc = w[0:1, :] * u2 + w[1:2, :] * u1 + w[2:3, :] * u + b_ref[...]
        duc = dzv * bg
        ducn = jnp.where(i < nsteps - 1, dzn_ref[...] * bgn_ref[...], 0.0)
        du = w[2:3, :] * duc + w[1:2, :] * _shift_rows_up(duc, ducn, 1) + w[0:1, :] * _shift_rows_up(duc, ducn, 2)
        dp_ref[0] = (dzv * uc).astype(BF16)
        dp_ref[1] = (du * xv).astype(BF16)
        dp_ref[2] = (du * cg).astype(BF16)
        _accumulate(i, a_w0, _fold_rows(duc * u2))
        _accumulate(i, a_w1, _fold_rows(duc * u1))
        _accumulate(i, a_w2, _fold_rows(duc * u))
        _accumulate(i, a_b, _fold_rows(duc))

        @pl.when(i == nsteps - 1)
        def _():
            dw_ref[0:1, :] = jnp.sum(a_w0[...], axis=0, keepdims=True)
            dw_ref[1:2, :] = jnp.sum(a_w1[...], axis=0, keepdims=True)
            dw_ref[2:3, :] = jnp.sum(a_w2[...], axis=0, keepdims=True)
            db_ref[...] = jnp.sum(a_b[...], axis=0, keepdims=True)

    swap = lambda spec: pl.BlockSpec(spec.block_shape, lambda j, i, _m=spec.index_map: _m(i, j))
    return pl.pallas_call(
        body, name=name, grid=(d // tn, nsteps),
        in_specs=[pl.BlockSpec((tb, tn), lambda j, i: (i, j)), swap(cur(0)), swap(cur(1)), swap(cur(2)),
                  swap(prev(1)), swap(prev(2)), swap(nxt(0)), swap(nxt(0)),
                  pl.BlockSpec((3, tn), lambda j, i: (0, j)), pl.BlockSpec((1, tn), lambda j, i: (0, j))],
        out_specs=[pl.BlockSpec((3, tb, tn), lambda j, i: (0, i, j)), pl.BlockSpec((3, tn), lambda j, i: (0, j)),
                   pl.BlockSpec((1, tn), lambda j, i: (0, j))],
        out_shape=[jax.ShapeDtypeStruct((3, s, d), BF16), jax.ShapeDtypeStruct((3, d), F32),
                   jax.ShapeDtypeStruct((1, d), F32)],
        scratch_shapes=[pltpu.VMEM((SUBLANES, tn), F32)] * 4, compiler_params=_params(2),
    )(dz, p, p, p, p, p, dz, p, conv_w, conv_b)


def _block_cumsum(v):
    tb = v.shape[0]
    row = lax.broadcasted_iota(jnp.int32, v.shape, 0)
    sh = 1
    while sh < tb:
        v = v + jnp.where(row >= sh, pltpu.roll(v, sh, 0), 0.0)
        sh *= 2
    return v


def _fgate_fwd(name, zf, b_f):
    s = zf.shape[0]
    tb = _tile(s, 512, SUBLANES)

    def body(z_ref, b_ref, f_ref, ft_ref, carry):
        i = pl.program_id(0)
        z = z_ref[...] + b_ref[...]
        ls = jnp.minimum(z, 0.0) - jnp.log(1.0 + jnp.exp(-jnp.abs(z)))
        run = _block_cumsum(ls)

        @pl.when(i == 0)
        def _():
            carry[...] = jnp.zeros_like(carry)

        out = run + carry[...]
        f_ref[...] = out
        ft_ref[...] = jnp.transpose(out)
        carry[...] = out[tb - 1:tb, :]

    blk = pl.BlockSpec((tb, LANES), lambda i: (i, 0))
    return pl.pallas_call(
        body, name=name, grid=(s // tb,), in_specs=[blk, pl.BlockSpec((1, LANES), lambda i: (0, 0))],
        out_specs=[blk, pl.BlockSpec((LANES, tb), lambda i: (0, i))],
        out_shape=[jax.ShapeDtypeStruct((s, LANES), F32), jax.ShapeDtypeStruct((LANES, s), F32)],
        scratch_shapes=[pltpu.VMEM((1, LANES), F32)], compiler_params=_params(1))(zf, b_f)


def _fgate_bwd(name, dfcum, zf, b_f):
    s = zf.shape[0]
    tb = _tile(s, 512, SUBLANES)
    nsteps = s // tb

    def body(df_ref, z_ref, b_ref, dz_ref, db_ref, carry, acc):
        i = pl.program_id(0)
        df = df_ref[...]
        incl = _block_cumsum(df)
        total = incl[tb - 1:tb, :]

        @pl.when(i == 0)
        def _():
            carry[...] = jnp.zeros_like(carry)

        suffix = total - incl + df + carry[...]
        carry[...] += total
        dzv = suffix * jax.nn.sigmoid(-(z_ref[...] + b_ref[...]))
        dz_ref[...] = dzv
        _accumulate(i, acc, _fold_rows(dzv))

        @pl.when(i == nsteps - 1)
        def _():
            db_ref[...] = jnp.sum(acc[...], axis=0, keepdims=True)

    blk = pl.BlockSpec((tb, LANES), lambda i: (nsteps - 1 - i, 0))
    vec = pl.BlockSpec((1, LANES), lambda i: (0, 0))
    return pl.pallas_call(
        body, name=name, grid=(nsteps,), in_specs=[blk, blk, vec], out_specs=[blk, vec],
        out_shape=[jax.ShapeDtypeStruct((s, LANES), F32), jax.ShapeDtypeStruct((1, LANES), F32)],
        scratch_shapes=[pltpu.VMEM((1, LANES), F32), pltpu.VMEM((SUBLANES, LANES), F32)],
        compiler_params=_params(1))(dfcum, zf, b_f)


def _lane_column(block, lane):
    sel = lax.broadcasted_iota(jnp.int32, block.shape, 1) == lane
    return jnp.sum(jnp.where(sel, block, 0.0), axis=1, keepdims=True)


def _col_to_row(col):
    return jnp.transpose(jnp.broadcast_to(col, (col.shape[0], LANES)))[0:1, :]


def _attn_tiles(s):
    t = _tile(s, 1024, LANES)
    return t, t


def _attn_fwd(name, q, kv, fcum_t, n_heads):
    s, d = q.shape
    dh = d // n_heads
    tq, tk = _attn_tiles(s)
    scale2 = LOG2E / math.sqrt(dh)

    def body(q_ref, k_ref, v_ref, ft_ref, o_ref, l_ref):
        qi = pl.program_id(1)
        qv = q_ref[...]

        def step(kj, carry, diagonal):
            m, l, acc = carry
            start = pl.multiple_of(kj * tk, tk)
            kb = k_ref[pl.ds(start, tk), :]
            vb = v_ref[pl.ds(start, tk), :]
            sc = lax.dot_general(qv, kb, _DN["nt"], preferred_element_type=F32) * scale2
            sc = sc - ft_ref[:, pl.ds(start, tk)] * LOG2E
            if diagonal:
                keep = lax.broadcasted_iota(jnp.int32, sc.shape, 0) >= lax.broadcasted_iota(jnp.int32, sc.shape, 1)
                sc = jnp.where(keep, sc, -jnp.inf)
            m_new = jnp.maximum(m, jnp.max(sc, axis=1, keepdims=True))
            alpha = jnp.exp2(m - m_new)
            pr = jnp.exp2(sc - m_new)
            l = alpha * l + jnp.sum(pr, axis=1, keepdims=True)
            acc = alpha * acc + lax.dot_general(pr.astype(BF16), vb, _DN["nn"], preferred_element_type=F32)
            return m_new, l, acc

        init = (jnp.full((tq, 1), -jnp.inf, F32), jnp.zeros((tq, 1), F32), jnp.zeros((tq, dh), F32))
        carry = lax.fori_loop(0, qi, lambda kj, cr: step(kj, cr, False), init)
        m, l, acc = step(qi, carry, True)
        o_ref[...] = (acc / l).astype(BF16)
        l_ref[...] = _col_to_row(m + jnp.log2(l))

    nh = n_heads
    return pl.pallas_call(
        body, name=name, grid=(nh, s // tq),
        in_specs=[pl.BlockSpec((tq, dh), lambda h, i: (i, h)),
                  pl.BlockSpec((s, dh), lambda h, i: (0, h)),
                  pl.BlockSpec((s, dh), lambda h, i: (0, nh + h)),
                  pl.BlockSpec((None, 1, s), lambda h, i: (h, 0, 0))],
        out_specs=[pl.BlockSpec((tq, dh), lambda h, i: (i, h)),
                   pl.BlockSpec((None, 1, tq), lambda h, i: (h, 0, i))],
        out_shape=[jax.ShapeDtypeStruct((s, d), BF16), jax.ShapeDtypeStruct((nh, 1, s), F32)],
        compiler_params=_params(2))(q, kv, kv, fcum_t)


def _attn_delta(name, d_o, o, n_heads):
    s, d = d_o.shape
    dh = d // n_heads
    tb = _row_tile(s)

    def body(do_ref, o_ref, dlt_ref, dob_ref):
        dov = do_ref[...]
        prod = dov * o_ref[...].astype(F32)
        lane = lax.broadcasted_iota(jnp.int32, (tb, LANES), 1)
        out = jnp.zeros((tb, LANES), F32)
        for h in range(n_heads):
            out = jnp.where(lane == h, jnp.sum(prod[:, h * dh:(h + 1) * dh], axis=1, keepdims=True), out)
        dlt_ref[...] = jnp.transpose(out)
        dob_ref[...] = dov.astype(BF16)

    row = pl.BlockSpec((tb, d), lambda i: (i, 0))
    return pl.pallas_call(
        body, name=name, grid=(s // tb,), in_specs=[row, row],
        out_specs=[pl.BlockSpec((LANES, tb), lambda i: (0, i)), row],
        out_shape=[jax.ShapeDtypeStruct((LANES, s), F32), jax.ShapeDtypeStruct((s, d), BF16)],
        compiler_params=_params(1))(d_o, o)


def _attn_bwd(name, q, kv, d_o, fcum, lse_t, delta_t, n_heads):
    s, d = q.shape
    dh = d // n_heads
    tq, tk = _attn_tiles(s)
    nq, nk = s // tq, s // tk
    scale = 1.0 / math.sqrt(dh)
    scale2 = LOG2E * scale

    def body(q_ref, k_ref, v_ref, do_ref, f_ref, lt_ref, dlt_ref, dq_ref, dkv_ref, dfq_ref, dfk_ref, dq_acc, dfq_acc):
        h, kj = pl.program_id(0), pl.program_id(1)
        kb, vb = k_ref[...], v_ref[...]
        fk2 = _lane_column(f_ref[...], h) * LOG2E

        @pl.when(kj == 0)
        def _():
            dq_acc[...] = jnp.zeros_like(dq_acc)
            dfq_acc[...] = jnp.zeros_like(dfq_acc)

        def step(qi, carry, diagonal):
            dk, dv, dfk = carry
            start = pl.multiple_of(qi * tq, tq)
            qb = q_ref[pl.ds(start, tq), :]
            dob = do_ref[pl.ds(start, tq), :]
            sc = lax.dot_general(kb, qb, _DN["nt"], preferred_element_type=F32) * scale2 - fk2
            pr = jnp.exp2(sc - lt_ref[:, pl.ds(start, tq)])
            if diagonal:
                keep = lax.broadcasted_iota(jnp.int32, sc.shape, 1) >= lax.broadcasted_iota(jnp.int32, sc.shape, 0)
                pr = jnp.where(keep, pr, 0.0)
            dv = dv + lax.dot_general(pr.astype(BF16), dob, _DN["nn"], preferred_element_type=F32)
            dp = lax.dot_general(vb, dob, _DN["nt"], preferred_element_type=F32)
            ds = pr * (dp - dlt_ref[:, pl.ds(start, tq)])
            ds_b = ds.astype(BF16)
            dk = dk + lax.dot_general(ds_b, qb, _DN["nn"], preferred_element_type=F32)
            dfk = dfk - jnp.sum(ds, axis=1, keepdims=True)
            dq_acc[pl.ds(start, tq), :] += lax.dot_general(ds_b, kb, _DN["tn"], preferred_element_type=F32)
            dfq_acc[:, pl.ds(start, tq)] += jnp.sum(ds, axis=0, keepdims=True)
            return dk, dv, dfk

        init = (jnp.zeros((tk, dh), F32), jnp.zeros((tk, dh), F32), jnp.zeros((tk, 1), F32))
        carry = step(kj, init, True)
        dk, dv, dfk = lax.fori_loop(kj + 1, nq, lambda qi, cr: step(qi, cr, False), carry)
        dkv_ref[0] = (dk * scale).astype(BF16)
        dkv_ref[1] = dv.astype(BF16)
        dfk_ref[...] = _col_to_row(dfk)

        @pl.when(kj == nk - 1)
        def _():
            dq_ref[...] = (dq_acc[...] * scale).astype(BF16)
            dfq_ref[...] = dfq_acc[...]

    nh = n_heads
    full = pl.BlockSpec((s, dh), lambda h, j: (0, h))
    row_t = pl.BlockSpec((None, 1, s), lambda h, j: (h, 0, 0))
    return pl.pallas_call(
        body, name=name, grid=(nh, nk),
        in_specs=[full, pl.BlockSpec((tk, dh), lambda h, j: (j, h)), pl.BlockSpec((tk, dh), lambda h, j: (j, nh + h)),
                  full, pl.BlockSpec((tk, LANES), lambda h, j: (j, 0)), row_t, row_t],
        out_specs=[full, pl.BlockSpec((2, tk, dh), lambda h, j: (0, j, h)), row_t,
                   pl.BlockSpec((None, 1, tk), lambda h, j: (h, 0, j))],
        out_shape=[jax.ShapeDtypeStruct((s, d), BF16), jax.ShapeDtypeStruct((2, s, d), BF16),
                   jax.ShapeDtypeStruct((nh, 1, s), F32), jax.ShapeDtypeStruct((nh, 1, s), F32)],
        scratch_shapes=[pltpu.VMEM((s, dh), F32), pltpu.VMEM((1, s), F32)],
        compiler_params=_params(2))(q, kv, kv, d_o, fcum, lse_t, delta_t)


def _sum_slots(name, g):
    n, r, lanes = g.shape

    def body(g_ref, o_ref):
        total = g_ref[0]
        for i in range(1, n):
            total = total + g_ref[i]
        o_ref[...] = total

    return pl.pallas_call(
        body, name=name, out_shape=jax.ShapeDtypeStruct((r, lanes), F32),
        in_specs=[pl.BlockSpec(memory_space=pltpu.VMEM)], out_specs=pl.BlockSpec(memory_space=pltpu.VMEM),
        compiler_params=pltpu.CompilerParams(vmem_limit_bytes=VMEM_LIMIT))(g)


def _adamw(name, parts, w, m, v, row0=0, previous=None):
    npart, r, c = parts.shape
    tb = _tile(math.gcd(r, row0) if row0 else r, max(SUBLANES, (2**18 // c) // SUBLANES * SUBLANES), SUBLANES)
    off = row0 // tb
    c1 = 1.0 - ADAM_B1 ** ADAM_STEP
    c2 = 1.0 - ADAM_B2 ** ADAM_STEP

    def body(p_ref, w_ref, m_ref, v_ref, *rest):
        g_out, d_out, m_out, v_out = rest[-4:]
        g = p_ref[0].astype(F32)
        for i in range(1, npart):
            g = g + p_ref[i].astype(F32)
        m_new = ADAM_B1 * m_ref[...] + (1.0 - ADAM_B1) * g
        v_new = ADAM_B2 * v_ref[...] + (1.0 - ADAM_B2) * (g * g)
        m_hat = m_new / c1
        v_hat = v_new / c2
        g_out[...] = g
        d_out[...] = -ADAM_LR * (m_hat / (jnp.sqrt(v_hat) + ADAM_EPS) + ADAM_WD * w_ref[...])
        m_out[...] = m_new
        v_out[...] = v_new

    blk = pl.BlockSpec((tb, c), lambda i: (i + off, 0))
    shape = jax.ShapeDtypeStruct(w.shape, F32)
    in_specs = [pl.BlockSpec((npart, tb, c), lambda i: (0, i, 0)), blk, blk, blk]
    operands = [parts, w, m, v]
    aliases = {}
    if previous is not None:
        in_specs += [pl.BlockSpec(memory_space=pl.ANY)] * 4
        operands += list(previous)
        aliases = {4 + k: k for k in range(4)}
    return pl.pallas_call(
        body, name=name, grid=(r // tb,), in_specs=in_specs, out_specs=[blk] * 4, out_shape=[shape] * 4,
        input_output_aliases=aliases, compiler_params=_params(1))(*operands)


def _pack(vectors):
    flat = jnp.concatenate([v.reshape(-1).astype(F32) for v in vectors])
    pad = (-flat.shape[0]) % (SUBLANES * LANES)
    return jnp.pad(flat, (0, pad)).reshape(-1, LANES)


def _unpack(flat, shapes):
    out, pos = [], 0
    for shp in shapes:
        size = math.prod(shp)
        out.append(flat[..., pos:pos + size].reshape(flat.shape[:-1] + tuple(shp)))
        pos += size
    return out


def _unshard_last(g, lead):
    nd = len(lead)
    return jnp.moveaxis(g, 0, nd).reshape(tuple(lead) + (-1,))


def _my_slice(full, me, width, axis):
    return lax.dynamic_slice_in_dim(full, me * width, width, axis)


def kernel(x, c, norm_g, w_ada, b_ada, w_ffn_in, w_ffn_out, w_conv_in, conv_w, conv_b, w_conv_out, kv_norm_g, w_ada_kv, b_ada_kv, w_kvf, b_fgate, w_q, w_o, final_g, loss_target, m_norm_g, m_w_ada, m_b_ada, m_w_ffn_in, m_w_ffn_out, m_w_conv_in, m_conv_w, m_conv_b, m_w_conv_out, m_kv_norm_g, m_w_ada_kv, m_b_ada_kv, m_w_kvf, m_b_fgate, m_w_q, m_w_o, m_final_g, v_norm_g, v_w_ada, v_b_ada, v_w_ffn_in, v_w_ffn_out, v_w_conv_in, v_conv_w, v_conv_b, v_w_conv_out, v_kv_norm_g, v_w_ada_kv, v_b_ada_kv, v_w_kvf, v_b_fgate, v_w_q, v_w_o, v_final_g):
    s, d = x.shape[1], x.shape[2]
    n_heads = b_fgate.shape[0]
    n_layers = w_ada.shape[0]
    dsh = d // N_DEV
    f = w_ffn_out.shape[2] * N_DEV
    ada_w = w_ada.shape[2]
    kv_w = w_ada_kv.shape[1]
    kvf_w = w_kvf.shape[1]
    assert n_layers == 2 and w_conv_in.shape[0] == 1 and w_q.shape[0] == 1
    assert (d // n_heads) % LANES == 0 and n_heads <= LANES and N_DEV * kvf_w == 2 * d + n_heads
    me = 4 * lax.axis_index("x") + 2 * lax.axis_index("y") + lax.axis_index("c")
    x0, target = x[0], loss_target[0]

    small = _allgather_small("gather_small_params", _pack([c, norm_g, conv_w, conv_b]))
    c_all, ng_sh, cw_sh, cb_sh = _unpack(small.reshape(N_DEV, -1), [(d,), (n_layers, 3, dsh), (3, dsh), (dsh,)])
    norm_g_full = _unshard_last(ng_sh, (n_layers, 3))
    conv_w_full = _unshard_last(cw_sh, (3,))
    conv_b_full = _unshard_last(cb_sh, ()).reshape(1, d)
    c_rows = jnp.pad(c_all, ((0, 16 - N_DEV), (0, 0)))

    ada_cols = [
        _mm_bias_rows(f"ada_rows_{l}", c_rows, w_ada.reshape(n_layers * d, ada_w),
                      _my_slice(b_ada[l], me, ada_w, 0).reshape(1, ada_w), _silu, b_block=l)
        for l in range(n_layers)]
    ada_cols.append(_mm_bias_rows("ada_rows_kv", c_rows, w_ada_kv, _my_slice(b_ada_kv, me, kv_w, 0).reshape(1, kv_w),
                                  _silu))
    ada_part = jnp.concatenate([a[:N_DEV] for a in ada_cols], axis=1)
    ada_all = _allgather_small("gather_ada_rows", _pack([ada_part]))
    ada_all = ada_all.reshape(N_DEV, -1)[:, :ada_part.size].reshape(N_DEV, N_DEV, -1)
    ada_mine = lax.dynamic_index_in_dim(ada_all, me, axis=1, keepdims=False)
    ada = [ada_mine[:, l * ada_w:(l + 1) * ada_w].reshape(3, 3, 1, d) for l in range(n_layers)]
    ada_kv = ada_mine[:, n_layers * ada_w:].reshape(2, 1, d)

    def as_item(w, kind):
        return (w.astype(BF16), kind)

    gather_first = _GatherComm([as_item(w_ffn_in[0, 0], "cols")])
    gather_l0b = _GatherComm([as_item(w_ffn_out[0, 0], "rows"), as_item(w_ffn_in[0, 1], "cols")])
    gather_conv = _GatherComm([as_item(w_conv_in[0], "cols"), as_item(w_conv_out[0], "rows")])
    gather_l0b_out = _GatherComm([as_item(w_ffn_out[0, 1], "rows")])
    gather_l1a = _GatherComm([as_item(w_ffn_out[1, 0], "rows"), as_item(w_ffn_in[1, 0], "cols")])
    gather_attn = _GatherComm([as_item(w_q[0], "rows"), as_item(w_kvf, "lead"), as_item(w_o[0], "rows")])
    gather_l1b = _GatherComm([as_item(w_ffn_out[1, 1], "rows"), as_item(w_ffn_in[1, 1], "cols")])
    (wf_in00,) = _run_comm("gather_ffn_first", gather_first)
    b_f = jnp.pad(b_fgate, (0, LANES - n_heads)).reshape(1, LANES)

    def ffn_fwd(tag, xs, l, sub, w_in, w_out, comm_in, comm_out):
        shift, scale, gate = ada[l][sub]
        tiles = FFN_TILES
        ab, u, h = _mm_swiglu(f"ffn_in_{tag}", xs, w_in, f, tiles["in"], comm=comm_in,
                              a_norm=(norm_g_full[l, sub].reshape(1, d), shift, scale))
        y, x_next = _mm_residual(f"ffn_out_{tag}", u, w_out, xs, 0.5 * gate, tiles["out"], comm=comm_out)
        return x_next, (xs, h, ab, u, y, w_in, w_out)

    ab, u, h = _mm_swiglu("ffn_in_l0a", x0, wf_in00, f, FFN_TILES["in"], comm=gather_l0b,
                          a_norm=(norm_g_full[0, 0].reshape(1, d), ada[0][0][0], ada[0][0][1]))
    wf_out00, wf_in01 = gather_l0b.results
    y, x1 = _mm_residual("ffn_out_l0a", u, wf_out00, x0, 0.5 * ada[0][0][2], FFN_TILES["out"], comm=gather_conv)
    save_f0 = (x0, h, ab, u, y, wf_in00, wf_out00)
    wc_in, wc_out = gather_conv.results
    shift, scale, gate = ada[0][1]
    p_c, h_c = _mm_nn("conv_in", x1, wc_in, F32, comm=gather_l0b_out,
                      a_norm=(norm_g_full[0, 1].reshape(1, d), shift, scale))
    (wf_out01,) = gather_l0b_out.results
    z_c = _conv_fwd("conv_mix", p_c, conv_w_full, conv_b_full)
    y_c, x2 = _mm_residual("conv_out", z_c, wc_out, x1, gate)
    x3, save_f1 = ffn_fwd("l0b", x2, 0, 2, wf_in01, wf_out01, gather_l1a, gather_attn)
    wf_out10, wf_in10 = gather_l1a.results
    wq, g_kvf, wo = gather_attn.results
    wkvf = _unshard_last(g_kvf, (d,))
    wkv = wkvf[:, :2 * d]
    wf_gate = jnp.pad(wkvf[:, 2 * d:], ((0, 0), (0, LANES - n_heads)))
    x4, save_f2 = ffn_fwd("l1a", x3, 1, 0, wf_in10, wf_out10, gather_l1b, None)
    wf_out11, wf_in11 = gather_l1b.results
    shift, scale, gate_a = ada[1][1]
    q, h_q = _mm_nn("attn_q", x4, wq, BF16, a_norm=(norm_g_full[1, 1].reshape(1, d), shift, scale))
    kv, h_kv = _mm_nn("attn_kv", x4, wkv, BF16, a_norm=(kv_norm_g.reshape(1, d), ada_kv[0], ada_kv[1]))
    zf = _mm_nn("attn_fgate", h_kv, wf_gate, F32)
    fcum, fcum_lanes = _fgate_fwd("fgate_cumsum", zf, b_f)
    fcum_t = fcum_lanes[:n_heads].reshape(n_heads, 1, s)
    o, lse_t = _attn_fwd("attn_fwd", q, kv, fcum_t, n_heads)
    y_a, x5 = _mm_residual("attn_out", o, wo, x4, gate_a)
    x6, save_f3 = ffn_fwd("l1b", x5, 1, 2, wf_in11, wf_out11, None, None)
    dx, loss_part, d_final_g = _final_loss("final_loss", x6, target, final_g.reshape(1, d))

    d_ada = [[[None] * 3 for _ in range(3)] for _ in range(n_layers)]
    d_norm_g = [[None] * 3 for _ in range(n_layers)]
    core = lax.axis_index("c").astype(jnp.int32).reshape(1)
    in_shard, out_shard, row_shard = (d, 2 * f // N_DEV), (f // N_DEV, d), (dsh, d)

    def chip_exchange(tag, items, from_sibling):
        return _ChipComm([_pair_add(f"add_sibling_{tag}_{n}", p, kind, shape, t, core)
                          for n, ((p, kind, shape), t) in enumerate(zip(items, from_sibling))])

    def reduce_within_chip(tag, items):
        return chip_exchange(tag, items, _run_comm(f"reduce_sibling_{tag}", _SiblingComm(items)))

    def ffn_bwd(tag, dxs, dy, saved, l, sub, comm_dw, then):
        xs, h, ab, u, y, w_in, w_out = saved
        tiles = FFN_TILES
        dab = _mm_nt_swiglu_bwd(f"ffn_out_bwd_{tag}", dy, w_out, ab, tiles["out_bwd"])
        dw_in = _mm_tn(f"ffn_in_dw_{tag}", h, dab, BF16, tiles=tiles["in_dw"])
        sibling_in = _SiblingComm([(dw_in, "cols", in_shard)])
        dw_out = _mm_tn(f"ffn_out_dw_{tag}", u, dy, BF16, tiles=tiles["out_dw"], comm=_join(sibling_in, comm_dw))
        sibling_out = _run_comm(f"reduce_sibling_{tag}", _SiblingComm([(dw_out, "rows", out_shard)]))
        scatter = chip_exchange(tag, [(dw_in, "cols", in_shard), (dw_out, "rows", out_shard)],
                                list(sibling_in.results) + list(sibling_out))
        dh = _mm_nt_parts(f"ffn_in_bwd_{tag}", dab, w_in, F32, tiles["in_bwd"], comm=scatter)
        outs = _norm_mod_bwd(f"norm_bwd_{tag}", dh, xs, norm_g_full[l, sub].reshape(1, d), ada[l][sub][1], dxs, then)
        dxs, d_ada[l][sub][0], d_ada[l][sub][1], d_norm_g[l][sub] = outs[:4]
        return dxs, outs[4:], scatter

    def gate_of(saved, l, sub):
        return (saved[4], 0.5 * ada[l][sub][2], 0.5)

    dy, d_ada[1][2][2] = _gate_bwd("gate_bwd_l1b", dx, save_f3[4], 0.5 * ada[1][2][2], 0.5)
    dx, (dy, d_ada[1][1][2]), scatter_l1b = ffn_bwd("l1b", dx, dy, save_f3, 1, 2, None, (y_a, gate_a, 1.0))
    d_o = _mm_nt("attn_out_bwd", dy, wo, F32)
    dwo = _mm_tn("attn_out_dw", o, dy, BF16)
    delta_lanes, d_ob = _attn_delta("attn_delta", d_o, o, n_heads)
    delta_t = delta_lanes[:n_heads].reshape(n_heads, 1, s)
    dq, dkv, dfq, dfk = _attn_bwd("attn_bwd", q, kv, d_ob, fcum, lse_t, delta_t, n_heads)
    dfcum = jnp.pad((dfq + dfk).reshape(n_heads, s).T, ((0, 0), (0, LANES - n_heads)))
    dzf, d_bf = _fgate_bwd("fgate_bwd", dfcum, zf, b_f)
    dh_q = _mm_nt("attn_q_bwd", dq, wq, F32)
    dwq = _mm_tn("attn_q_dw", h_q, dq, BF16)
    dh_kv = _mm_nt_parts("attn_kv_bwd", dkv, wkv, F32)
    dh_kv = _mm_nt("attn_fgate_bwd", dzf, wf_gate, F32, extra_add=dh_kv)
    dwkv = _mm_tn("attn_kv_dw", h_kv, dkv, BF16)
    dwf = _mm_tn("attn_fgate_dw", h_kv, dzf, BF16)
    dwkvf = jnp.concatenate([dwkv, dwf[:, :n_heads]], axis=1)
    p_kvf = jnp.moveaxis(dwkvf.reshape(d, N_DEV, kvf_w), 1, 0)
    scatter_attn = reduce_within_chip(
        "attn", [(dwo, "rows", row_shard), (dwq, "rows", row_shard), (p_kvf, "lead", (d, kvf_w))])
    dx, d_ada[1][1][0], d_ada[1][1][1], d_norm_g[1][1] = _norm_mod_bwd(
        "norm_bwd_q", dh_q, x4, norm_g_full[1, 1].reshape(1, d), ada[1][1][1], dx)
    dx, d_kv_shift, d_kv_scale, d_kv_norm_g, dy, d_ada[1][0][2] = _norm_mod_bwd(
        "norm_bwd_kv", dh_kv, x4, kv_norm_g.reshape(1, d), ada_kv[1], dx, gate_of(save_f2, 1, 0))
    dx, (dy, d_ada[0][2][2]), scatter_l1a = ffn_bwd("l1a", dx, dy, save_f2, 1, 0, scatter_attn, gate_of(save_f1, 0, 2))
    dx, (dy, d_ada[0][1][2]), scatter_l0b = ffn_bwd("l0b", dx, dy, save_f1, 0, 2, None, (y_c, ada[0][1][2], 1.0))
    dz = _mm_nt("conv_out_bwd", dy, wc_out, F32)
    dwc_out = _mm_tn("conv_out_dw", z_c, dy, BF16)
    dp, d_conv_w, d_conv_b = _conv_bwd("conv_mix_bwd", dz, p_c, conv_w_full, conv_b_full)
    dh_c = _mm_nt_parts("conv_in_bwd", dp, wc_in, F32)
    dwc_in = _mm_tn("conv_in_dw", h_c, dp, BF16)
    scatter_conv = reduce_within_chip("conv", [(dwc_out, "rows", row_shard), (dwc_in, "cols", (d, 3 * d // N_DEV))])
    dx, d_ada[0][1][0], d_ada[0][1][1], d_norm_g[0][1], dy, d_ada[0][0][2] = _norm_mod_bwd(
        "norm_bwd_conv", dh_c, x1, norm_g_full[0, 1].reshape(1, d), ada[0][1][1], dx, gate_of(save_f0, 0, 0))
    dx, _, scatter_l0a = ffn_bwd("l0a", dx, dy, save_f0, 0, 0, scatter_conv, None)
    grad_x = dx.reshape(1, s, d)

    d_ada_flat = jnp.concatenate([v.reshape(-1) for l in range(n_layers) for sub in range(3) for v in d_ada[l][sub]])
    d_norm_flat = jnp.concatenate([d_norm_g[l][sub].reshape(-1) for l in range(n_layers) for sub in range(3)])
    small_shapes = [(n_layers * 9 * d,), (2 * d,), (n_layers * 3 * d,), (3, d), (d,), (d,), (LANES,), (d,), (LANES,)]
    small_parts = _pack([d_ada_flat, d_kv_shift, d_kv_scale, d_norm_flat, d_conv_w, d_conv_b, d_kv_norm_g, d_bf,
                         d_final_g, loss_part])
    small_all = _allgather_small("gather_small_grads", small_parts)
    small_sum = _sum_slots("sum_small_grads", small_all).reshape(-1)
    g_b_ada, g_b_ada_kv, g_norm_full, g_conv_w_full, g_conv_b_full, g_kv_norm, g_bf, g_final, loss_v = _unpack(
        small_sum, small_shapes)
    loss = loss_v[0]
    d_ada_rows = small_all.reshape(N_DEV, -1)[:, :n_layers * 9 * d + 2 * d]
    d_ada_rows = jnp.pad(d_ada_rows, ((0, 16 - N_DEV), (0, 0)))

    results = {}

    def update(key, parts, w, m, v):
        shape = w.shape
        c_dim = shape[-1]
        outs = _adamw(f"adamw_{key}", parts.reshape(parts.shape[0], -1, c_dim), w.reshape(-1, c_dim),
                      m.reshape(-1, c_dim), v.reshape(-1, c_dim))
        results[key] = [o.reshape(shape) for o in outs]

    def update_rows(key, parts_list, w, m, v):
        c_dim = w.shape[-1]
        flat = [t.reshape(-1, c_dim) for t in (w, m, v)]
        outs = None
        for n, parts in enumerate(parts_list):
            outs = _adamw(f"adamw_{key}_{n}", parts, *flat, row0=n * parts.shape[1], previous=outs)
        results[key] = [o.reshape(w.shape) for o in outs]

    g_w_ada = []
    for l in range(n_layers):
        cols = _my_slice(d_ada_rows[:, l * 9 * d:(l + 1) * 9 * d], me, ada_w, 1)
        g_w_ada.append(_mm_tn(f"ada_dw_{l}", c_rows, cols, F32, a_fn=_silu)[None])
    update_rows("w_ada", g_w_ada, w_ada, m_w_ada, v_w_ada)
    cols = _my_slice(d_ada_rows[:, n_layers * 9 * d:], me, kv_w, 1)
    update("w_ada_kv", _mm_tn("ada_dw_kv", c_rows, cols, F32, a_fn=_silu).reshape(1, d, kv_w),
           w_ada_kv, m_w_ada_kv, v_w_ada_kv)


    ffn_scatters = [scatter_l0a, scatter_l0b, scatter_l1a, scatter_l1b]
    update_rows("w_ffn_in", [sc.results[0] for sc in ffn_scatters], w_ffn_in, m_w_ffn_in, v_w_ffn_in)
    update_rows("w_ffn_out", [sc.results[1] for sc in ffn_scatters], w_ffn_out, m_w_ffn_out, v_w_ffn_out)
    r_co, r_ci = scatter_conv.results
    r_o, r_q, r_kvf = scatter_attn.results
    update_rows("w_conv_in", [r_ci], w_conv_in, m_w_conv_in, v_w_conv_in)
    update_rows("w_conv_out", [r_co], w_conv_out, m_w_conv_out, v_w_conv_out)
    update_rows("w_kvf", [r_kvf], w_kvf, m_w_kvf, v_w_kvf)
    update_rows("w_q", [r_q], w_q, m_w_q, v_w_q)
    update_rows("w_o", [r_o], w_o, m_w_o, v_w_o)

    small_keys = ["norm_g", "b_ada", "conv_w", "conv_b", "kv_norm_g", "b_ada_kv", "b_fgate", "final_g"]
    small_w = [norm_g, b_ada, conv_w, conv_b, kv_norm_g, b_ada_kv, b_fgate, final_g]
    small_m = [m_norm_g, m_b_ada, m_conv_w, m_conv_b, m_kv_norm_g, m_b_ada_kv, m_b_fgate, m_final_g]
    small_v = [v_norm_g, v_b_ada, v_conv_w, v_conv_b, v_kv_norm_g, v_b_ada_kv, v_b_fgate, v_final_g]
    small_g = [
        _my_slice(g_norm_full.reshape(n_layers, 3, d), me, dsh, 2), g_b_ada.reshape(b_ada.shape),
        _my_slice(g_conv_w_full, me, dsh, 1).reshape(conv_w.shape), _my_slice(g_conv_b_full, me, dsh, 0).reshape(
            conv_b.shape), g_kv_norm, g_b_ada_kv, g_bf[:n_heads], g_final]
    packed = _adamw("adamw_small", _pack(small_g)[None], _pack(small_w), _pack(small_m), _pack(small_v))
    for vals, idx in zip(packed, range(4)):
        for key, val in zip(small_keys, _unpack(vals.reshape(-1), [w.shape for w in small_w])):
            results.setdefault(key, [None] * 4)[idx] = val

    order = ["norm_g", "w_ada", "b_ada", "w_ffn_in", "w_ffn_out", "w_conv_in", "conv_w", "conv_b", "w_conv_out",
             "kv_norm_g", "w_ada_kv", "b_ada_kv", "w_kvf", "b_fgate", "w_q", "w_o", "final_g"]
    return (loss, grad_x, *[results[k][0] for k in order], *[results[k][1] for k in order],
            *[results[k][2] for k in order], *[results[k][3] for k in order])
```
